```python
import jax, jax.numpy as jnp
from jax import lax
import numpy as np

D_MODEL = 1024
BATCH = 8
SEQ = 16384
DEPTH = 1

CHUNK = 64
Q_BLOCK = 128
N_MEM = 256
EPS = 1e-6

GLA_HEADS = 4
GLA_DK = D_MODEL // 16
GLA_DV = D_MODEL // 8
GLA_LOWRANK = 16
GLA_TAU = 16.0
FOX_HEADS = 8
FOX_DH = D_MODEL // 16
MEM_HEADS = 4
MEM_DH = D_MODEL // 8
D_FF = 4 * D_MODEL
N_BRANCH = 3

GLA_K = GLA_HEADS * GLA_DK
GLA_V = GLA_HEADS * GLA_DV
FOX_W = FOX_HEADS * FOX_DH
MEM_W = MEM_HEADS * MEM_DH
PROJ_SIZES = (GLA_K, GLA_K, GLA_V, GLA_V, GLA_LOWRANK, FOX_W, FOX_W, FOX_W, FOX_HEADS, MEM_W, N_BRANCH * D_MODEL)
D_IN = 2 * GLA_K + 2 * GLA_V + GLA_LOWRANK + 3 * FOX_W + FOX_HEADS + MEM_W + N_BRANCH * D_MODEL

kernel_name = "hybrid_gla_fox_memory_gated_block"


def rmsnorm(x, g):
    xf = x.astype(jnp.float32)
    r = lax.rsqrt(jnp.mean(xf * xf, axis=-1, keepdims=True) + EPS)
    return (xf * r).astype(x.dtype) * g


def split_cols(t, sizes):
    offs = np.cumsum(np.array(sizes))[:-1].tolist()
    return jnp.split(t, offs, axis=-1)


def gla_chunk_causal(q, k, v, log_a):
    B, S, H, DK = q.shape
    DV = v.shape[-1]
    N = S // CHUNK
    f32 = jnp.float32
    qc = q.reshape(B, N, CHUNK, H, DK).astype(f32)
    kc = k.reshape(B, N, CHUNK, H, DK).astype(f32)
    vc = v.reshape(B, N, CHUNK, H, DV).astype(f32)
    b = jnp.cumsum(log_a.reshape(B, N, CHUNK, H, DK).astype(f32), axis=2)
    b_last = b[:, :, -1:]
    e_pos = jnp.exp(b)
    e_neg = jnp.exp(-b)
    q_pos = qc * e_pos
    a_causal = jnp.einsum('bnthd,bnshd->bnhts', q_pos, kc * e_neg)
    a_anti = jnp.einsum('bnthd,bnshd->bnhts', qc * e_neg, kc * e_pos)
    t_idx = jnp.arange(CHUNK)
    lower = t_idx[:, None] >= t_idx[None, :]
    attn = jnp.where(lower, a_causal, a_anti)
    o_intra = jnp.einsum('bnhts,bnshv->bnthv', attn, vc)
    chunk_kv = jnp.einsum('bnshd,bnshv->bnhdv', kc * jnp.exp(b_last - b), vc)
    chunk_decay = jnp.exp(b_last[:, :, 0])

    def step(state, inp):
        kv, dec = inp
        return state * dec[..., None] + kv, state

    init = jnp.zeros((B, H, DK, DV), f32)
    _, prev = lax.scan(step, init, (jnp.moveaxis(chunk_kv, 1, 0), jnp.moveaxis(chunk_decay, 1, 0)))
    prev = jnp.moveaxis(prev, 0, 1)
    o_inter = jnp.einsum('bnthd,bnhdv->bnthv', q_pos, prev)
    return (o_intra + o_inter).reshape(B, S, H, DV).astype(v.dtype)


def forgetting_attention(q, k, v, log_f):
    B, S, H, Dh = q.shape
    nb = S // Q_BLOCK
    scale = Dh ** -0.5
    F = jnp.cumsum(log_f.astype(jnp.float32), axis=1).transpose(0, 2, 1)
    kh = k.transpose(0, 2, 1, 3)
    vh = v.transpose(0, 2, 1, 3)
    qb = q.reshape(B, nb, Q_BLOCK, H, Dh).transpose(1, 0, 3, 2, 4)
    Fq = F.reshape(B, H, nb, Q_BLOCK).transpose(2, 0, 1, 3)
    k_pos = jnp.arange(S)

    def block(args):
        qi, Fi, i = args
        s = jnp.einsum('bhqd,bhkd->bhqk', qi, kh).astype(jnp.float32) * scale
        s = s + Fi[..., None] - F[:, :, None, :]
        q_pos = i * Q_BLOCK + jnp.arange(Q_BLOCK)
        mask = k_pos[None, :] <= q_pos[:, None]
        p = jax.nn.softmax(jnp.where(mask, s, -jnp.inf), axis=-1)
        return jnp.einsum('bhqk,bhkd->bhqd', p.astype(vh.dtype), vh)

    out = lax.map(block, (qb, Fq, jnp.arange(nb)))
    return out.transpose(1, 0, 3, 2, 4).reshape(B, S, H, Dh)


def memory_attention(q, mk, mv):
    scale = q.shape[-1] ** -0.5
    s = jnp.einsum('bshd,bmhd->bhsm', q, mk).astype(jnp.float32) * scale
    p = jax.nn.softmax(s, axis=-1)
    return jnp.einsum('bhsm,bmhd->bshd', p.astype(mv.dtype), mv)


def _fwd_setup_inputs(seed: int = 0) -> dict:
    key = jax.random.key(seed)
    ks = jax.random.split(key, 20)
    f32 = jnp.float32

    def nrm(k, shape, fan_in):
        return jax.random.normal(k, shape, f32) * (fan_in ** -0.5)

    def gain(k, shape):
        return 1.0 + 0.02 * jax.random.normal(k, shape, f32)

    L = DEPTH
    return {
        "x": jax.random.normal(ks[0], (BATCH, SEQ, D_MODEL), f32),
        "mem": jax.random.normal(ks[1], (BATCH, N_MEM, D_MODEL), f32),
        "g_mix": gain(ks[2], (L, D_MODEL)),
        "w_in": nrm(ks[3], (L, D_MODEL, D_IN), D_MODEL),
        "w_alpha_up": nrm(ks[4], (L, GLA_LOWRANK, GLA_K), GLA_LOWRANK),
        "b_alpha": 0.02 * jax.random.normal(ks[5], (L, GLA_K), f32),
        "b_forget": jax.random.uniform(ks[6], (L, FOX_HEADS), f32, 1.0, 5.0),
        "g_gla_head": gain(ks[7], (L, GLA_HEADS, GLA_DV)),
        "g_mem": gain(ks[8], (L, D_MODEL)),
        "w_mem_kv": nrm(ks[9], (L, D_MODEL, 2 * MEM_W), D_MODEL),
        "w_gla_o": nrm(ks[10], (L, GLA_V, D_MODEL), GLA_V),
        "w_fox_o": nrm(ks[11], (L, FOX_W, D_MODEL), FOX_W),
        "w_mem_o": nrm(ks[12], (L, MEM_W, D_MODEL), MEM_W),
        "w_out": nrm(ks[13], (L, D_MODEL, D_MODEL), D_MODEL),
        "g_ffn": gain(ks[14], (L, D_MODEL)),
        "w_ff1": nrm(ks[15], (L, D_MODEL, D_FF), D_MODEL),
        "w_ff2": nrm(ks[16], (L, D_FF, D_MODEL), D_FF),
        "g_final": gain(ks[17], (D_MODEL,)),
    }


def _fwd_reference(x, mem, g_mix, w_in, w_alpha_up, b_alpha, b_forget, g_gla_head, g_mem, w_mem_kv,
              w_gla_o, w_fox_o, w_mem_o, w_out, g_ffn, w_ff1, w_ff2, g_final):
    B, S, D = x.shape
    M = mem.shape[1]
    h = x
    for l in range(DEPTH):
        u = rmsnorm(h, g_mix[l])
        proj = u @ w_in[l]
        (gq, gk, gv, gg, ga, fq, fk, fv, ff, mq, gates) = split_cols(proj, PROJ_SIZES)

        log_a = jax.nn.log_sigmoid(ga @ w_alpha_up[l] + b_alpha[l]) / GLA_TAU
        o_gla = gla_chunk_causal(
            gq.reshape(B, S, GLA_HEADS, GLA_DK) * (GLA_DK ** -0.5),
            gk.reshape(B, S, GLA_HEADS, GLA_DK),
            gv.reshape(B, S, GLA_HEADS, GLA_DV),
            log_a.reshape(B, S, GLA_HEADS, GLA_DK))
        o_gla = rmsnorm(o_gla, g_gla_head[l]) * jax.nn.silu(gg.reshape(B, S, GLA_HEADS, GLA_DV))
        y_gla = o_gla.reshape(B, S, GLA_V) @ w_gla_o[l]

        log_f = jax.nn.log_sigmoid(ff + b_forget[l])
        o_fox = forgetting_attention(
            fq.reshape(B, S, FOX_HEADS, FOX_DH),
            fk.reshape(B, S, FOX_HEADS, FOX_DH),
            fv.reshape(B, S, FOX_HEADS, FOX_DH),
            log_f)
        y_fox = o_fox.reshape(B, S, FOX_W) @ w_fox_o[l]

        mkv = rmsnorm(mem, g_mem[l]) @ w_mem_kv[l]
        mk, mv = jnp.split(mkv, 2, axis=-1)
        o_mem = memory_attention(
            mq.reshape(B, S, MEM_HEADS, MEM_DH),
            mk.reshape(B, M, MEM_HEADS, MEM_DH),
            mv.reshape(B, M, MEM_HEADS, MEM_DH))
        y_mem = o_mem.reshape(B, S, MEM_W) @ w_mem_o[l]

        gt = jax.nn.sigmoid(gates.reshape(B, S, N_BRANCH, D))
        merged = gt[:, :, 0] * y_gla + gt[:, :, 1] * y_fox + gt[:, :, 2] * y_mem
        h = h + merged @ w_out[l]

        u2 = rmsnorm(h, g_ffn[l])
        h = h + jnp.square(jax.nn.relu(u2 @ w_ff1[l])) @ w_ff2[l]
    return rmsnorm(h, g_final)


import jax as _jax
import jax.numpy as _jnp

TWIN_FORMAT = 'train_step'
FWD_PARAMS = ['x', 'mem', 'g_mix', 'w_in', 'w_alpha_up', 'b_alpha', 'b_forget', 'g_gla_head', 'g_mem', 'w_mem_kv', 'w_gla_o', 'w_fox_o', 'w_mem_o', 'w_out', 'g_ffn', 'w_ff1', 'w_ff2', 'g_final']
TWIN_WEIGHTS = ['g_mix', 'w_in', 'w_alpha_up', 'b_alpha', 'b_forget', 'g_gla_head', 'g_mem', 'w_mem_kv', 'w_gla_o', 'w_fox_o', 'w_mem_o', 'w_out', 'g_ffn', 'w_ff1', 'w_ff2', 'g_final']
TWIN_DIFF_INPUT = 'x'
TWIN_INPUTS = ['x', 'mem', 'g_mix', 'w_in', 'w_alpha_up', 'b_alpha', 'b_forget', 'g_gla_head', 'g_mem', 'w_mem_kv', 'w_gla_o', 'w_fox_o', 'w_mem_o', 'w_out', 'g_ffn', 'w_ff1', 'w_ff2', 'g_final', 'loss_target', 'm_g_mix', 'm_w_in', 'm_w_alpha_up', 'm_b_alpha', 'm_b_forget', 'm_g_gla_head', 'm_g_mem', 'm_w_mem_kv', 'm_w_gla_o', 'm_w_fox_o', 'm_w_mem_o', 'm_w_out', 'm_g_ffn', 'm_w_ff1', 'm_w_ff2', 'm_g_final', 'v_g_mix', 'v_w_in', 'v_w_alpha_up', 'v_b_alpha', 'v_b_forget', 'v_g_gla_head', 'v_g_mem', 'v_w_mem_kv', 'v_w_gla_o', 'v_w_fox_o', 'v_w_mem_o', 'v_w_out', 'v_g_ffn', 'v_w_ff1', 'v_w_ff2', 'v_g_final']
TWIN_OUTPUTS = ['loss', 'grad_x', 'grad_g_mix', 'grad_w_in', 'grad_w_alpha_up', 'grad_b_alpha', 'grad_b_forget', 'grad_g_gla_head', 'grad_g_mem', 'grad_w_mem_kv', 'grad_w_gla_o', 'grad_w_fox_o', 'grad_w_mem_o', 'grad_w_out', 'grad_g_ffn', 'grad_w_ff1', 'grad_w_ff2', 'grad_g_final', 'delta_g_mix', 'delta_w_in', 'delta_w_alpha_up', 'delta_b_alpha', 'delta_b_forget', 'delta_g_gla_head', 'delta_g_mem', 'delta_w_mem_kv', 'delta_w_gla_o', 'delta_w_fox_o', 'delta_w_mem_o', 'delta_w_out', 'delta_g_ffn', 'delta_w_ff1', 'delta_w_ff2', 'delta_g_final', 'new_m_g_mix', 'new_m_w_in', 'new_m_w_alpha_up', 'new_m_b_alpha', 'new_m_b_forget', 'new_m_g_gla_head', 'new_m_g_mem', 'new_m_w_mem_kv', 'new_m_w_gla_o', 'new_m_w_fox_o', 'new_m_w_mem_o', 'new_m_w_out', 'new_m_g_ffn', 'new_m_w_ff1', 'new_m_w_ff2', 'new_m_g_final', 'new_v_g_mix', 'new_v_w_in', 'new_v_w_alpha_up', 'new_v_b_alpha', 'new_v_b_forget', 'new_v_g_gla_head', 'new_v_g_mem', 'new_v_w_mem_kv', 'new_v_w_gla_o', 'new_v_w_fox_o', 'new_v_w_mem_o', 'new_v_w_out', 'new_v_g_ffn', 'new_v_w_ff1', 'new_v_w_ff2', 'new_v_g_final']
TWIN_LEAF_KINDS = {'loss': 'loss', 'grad_x': 'grad_x', 'grad_g_mix': 'grad_w', 'grad_w_in': 'grad_w', 'grad_w_alpha_up': 'grad_w', 'grad_b_alpha': 'grad_w', 'grad_b_forget': 'grad_w', 'grad_g_gla_head': 'grad_w', 'grad_g_mem': 'grad_w', 'grad_w_mem_kv': 'grad_w', 'grad_w_gla_o': 'grad_w', 'grad_w_fox_o': 'grad_w', 'grad_w_mem_o': 'grad_w', 'grad_w_out': 'grad_w', 'grad_g_ffn': 'grad_w', 'grad_w_ff1': 'grad_w', 'grad_w_ff2': 'grad_w', 'grad_g_final': 'grad_w', 'delta_g_mix': 'delta_w', 'delta_w_in': 'delta_w', 'delta_w_alpha_up': 'delta_w', 'delta_b_alpha': 'delta_w', 'delta_b_forget': 'delta_w', 'delta_g_gla_head': 'delta_w', 'delta_g_mem': 'delta_w', 'delta_w_mem_kv': 'delta_w', 'delta_w_gla_o': 'delta_w', 'delta_w_fox_o': 'delta_w', 'delta_w_mem_o': 'delta_w', 'delta_w_out': 'delta_w', 'delta_g_ffn': 'delta_w', 'delta_w_ff1': 'delta_w', 'delta_w_ff2': 'delta_w', 'delta_g_final': 'delta_w', 'new_m_g_mix': 'new_m', 'new_m_w_in': 'new_m', 'new_m_w_alpha_up': 'new_m', 'new_m_b_alpha': 'new_m', 'new_m_b_forget': 'new_m', 'new_m_g_gla_head': 'new_m', 'new_m_g_mem': 'new_m', 'new_m_w_mem_kv': 'new_m', 'new_m_w_gla_o': 'new_m', 'new_m_w_fox_o': 'new_m', 'new_m_w_mem_o': 'new_m', 'new_m_w_out': 'new_m', 'new_m_g_ffn': 'new_m', 'new_m_w_ff1': 'new_m', 'new_m_w_ff2': 'new_m', 'new_m_g_final': 'new_m', 'new_v_g_mix': 'new_v', 'new_v_w_in': 'new_v', 'new_v_w_alpha_up': 'new_v', 'new_v_b_alpha': 'new_v', 'new_v_b_forget': 'new_v', 'new_v_g_gla_head': 'new_v', 'new_v_g_mem': 'new_v', 'new_v_w_mem_kv': 'new_v', 'new_v_w_gla_o': 'new_v', 'new_v_w_fox_o': 'new_v', 'new_v_w_mem_o': 'new_v', 'new_v_w_out': 'new_v', 'new_v_g_ffn': 'new_v', 'new_v_w_ff1': 'new_v', 'new_v_w_ff2': 'new_v', 'new_v_g_final': 'new_v'}


def _forward(args):
    return _fwd_reference(*[args[k] for k in FWD_PARAMS])


def _output_shape():
    def fwd():
        inp = _fwd_setup_inputs(0)
        return _fwd_reference(*[inp[k] for k in FWD_PARAMS])
    out = _jax.eval_shape(fwd)
    return out.shape, out.dtype

N_MICROBATCH = 1
ADAM_LR = 0.001
ADAM_B1 = 0.9
ADAM_B2 = 0.999
ADAM_EPS = 1e-08
ADAM_WD = 0.01
ADAM_STEP = 10
PER_EXAMPLE_BATCH_AXIS = {'x': 0, 'mem': 0, 'loss_target': 0}
SHARED_INPUTS = []
_WEIGHT_DTYPES = {'g_mix': _jnp.float32, 'w_in': _jnp.float32, 'w_alpha_up': _jnp.float32, 'b_alpha': _jnp.float32, 'b_forget': _jnp.float32, 'g_gla_head': _jnp.float32, 'g_mem': _jnp.float32, 'w_mem_kv': _jnp.float32, 'w_gla_o': _jnp.float32, 'w_fox_o': _jnp.float32, 'w_mem_o': _jnp.float32, 'w_out': _jnp.float32, 'g_ffn': _jnp.float32, 'w_ff1': _jnp.float32, 'w_ff2': _jnp.float32, 'g_final': _jnp.float32}
MOMENT_SCALE = {'g_mix': 2.771444e-01, 'w_in': 1.033335e-01, 'w_alpha_up': 2.536296e-02, 'b_alpha': 9.021908e-02, 'b_forget': 4.135332e-01, 'g_gla_head': 1.698079e-01, 'g_mem': 3.151247e-02, 'w_mem_kv': 3.015700e-02, 'w_gla_o': 1.174762e-01, 'w_fox_o': 6.305828e-02, 'w_mem_o': 2.178274e-02, 'w_out': 1.347879e-01, 'g_ffn': 3.115486e-01, 'w_ff1': 1.554567e-01, 'w_ff2': 3.436044e-01, 'g_final': 1.290114e+02}


def _to_microbatches(a, axis):
    t = _jnp.moveaxis(a, axis, 0)
    t = t.reshape((N_MICROBATCH, t.shape[0] // N_MICROBATCH) + t.shape[1:])
    return _jnp.moveaxis(t, 1, axis + 1)


def setup_inputs(seed: int = 0) -> dict:
    inp = _fwd_setup_inputs(seed)
    key = _jax.random.fold_in(_jax.random.key(seed), 7919)
    shape, _ = _output_shape()
    out = dict(inp)
    out["loss_target"] = _jax.random.normal(_jax.random.fold_in(key, 0), shape, _jnp.float32)
    for i, name in enumerate(TWIN_WEIGHTS):
        w = inp[name].astype(_jnp.float32)
        if MOMENT_SCALE is None:
            s = _jnp.sqrt(_jnp.mean(_jnp.square(w)) + 1e-30)
        else:
            s = MOMENT_SCALE[name]
        km, kv = _jax.random.split(_jax.random.fold_in(key, i + 1))
        out[name] = w
        out["m_" + name] = s * _jax.random.normal(km, w.shape, _jnp.float32)
        out["v_" + name] = (s * s) * _jax.random.uniform(kv, w.shape, _jnp.float32, 0.5, 1.5)
    if N_MICROBATCH > 1:
        for name, axis in PER_EXAMPLE_BATCH_AXIS.items():
            out[name] = _to_microbatches(out[name], axis)
    return {'x': out['x'], 'mem': out['mem'], 'g_mix': out['g_mix'], 'w_in': out['w_in'], 'w_alpha_up': out['w_alpha_up'], 'b_alpha': out['b_alpha'], 'b_forget': out['b_forget'], 'g_gla_head': out['g_gla_head'], 'g_mem': out['g_mem'], 'w_mem_kv': out['w_mem_kv'], 'w_gla_o': out['w_gla_o'], 'w_fox_o': out['w_fox_o'], 'w_mem_o': out['w_mem_o'], 'w_out': out['w_out'], 'g_ffn': out['g_ffn'], 'w_ff1': out['w_ff1'], 'w_ff2': out['w_ff2'], 'g_final': out['g_final'], 'loss_target': out['loss_target'], 'm_g_mix': out['m_g_mix'], 'm_w_in': out['m_w_in'], 'm_w_alpha_up': out['m_w_alpha_up'], 'm_b_alpha': out['m_b_alpha'], 'm_b_forget': out['m_b_forget'], 'm_g_gla_head': out['m_g_gla_head'], 'm_g_mem': out['m_g_mem'], 'm_w_mem_kv': out['m_w_mem_kv'], 'm_w_gla_o': out['m_w_gla_o'], 'm_w_fox_o': out['m_w_fox_o'], 'm_w_mem_o': out['m_w_mem_o'], 'm_w_out': out['m_w_out'], 'm_g_ffn': out['m_g_ffn'], 'm_w_ff1': out['m_w_ff1'], 'm_w_ff2': out['m_w_ff2'], 'm_g_final': out['m_g_final'], 'v_g_mix': out['v_g_mix'], 'v_w_in': out['v_w_in'], 'v_w_alpha_up': out['v_w_alpha_up'], 'v_b_alpha': out['v_b_alpha'], 'v_b_forget': out['v_b_forget'], 'v_g_gla_head': out['v_g_gla_head'], 'v_g_mem': out['v_g_mem'], 'v_w_mem_kv': out['v_w_mem_kv'], 'v_w_gla_o': out['v_w_gla_o'], 'v_w_fox_o': out['v_w_fox_o'], 'v_w_mem_o': out['v_w_mem_o'], 'v_w_out': out['v_w_out'], 'v_g_ffn': out['v_g_ffn'], 'v_w_ff1': out['v_w_ff1'], 'v_w_ff2': out['v_w_ff2'], 'v_g_final': out['v_g_final']}


def _loss(weights, diff, rest, loss_target):
    with _jax.named_scope("forward"):
        args = {**rest, TWIN_DIFF_INPUT: diff, **{k: w.astype(_WEIGHT_DTYPES[k]) for k, w in weights.items()}}
        y = _forward(args)
    with _jax.named_scope("loss_head"):
        err = _jnp.square(y.astype(_jnp.float32) - loss_target)
        return 0.5 * _jnp.sum(_jnp.mean(err, axis=-1)) if err.ndim else 0.5 * err


def _adamw(w, g, m, v):
    m = ADAM_B1 * m + (1.0 - ADAM_B1) * g
    v = ADAM_B2 * v + (1.0 - ADAM_B2) * _jnp.square(g)
    m_hat = m / (1.0 - ADAM_B1 ** ADAM_STEP)
    v_hat = v / (1.0 - ADAM_B2 ** ADAM_STEP)
    delta = -ADAM_LR * (m_hat / (_jnp.sqrt(v_hat) + ADAM_EPS) + ADAM_WD * w)
    return delta, m, v


def reference(x, mem, g_mix, w_in, w_alpha_up, b_alpha, b_forget, g_gla_head, g_mem, w_mem_kv, w_gla_o, w_fox_o, w_mem_o, w_out, g_ffn, w_ff1, w_ff2, g_final, loss_target, m_g_mix, m_w_in, m_w_alpha_up, m_b_alpha, m_b_forget, m_g_gla_head, m_g_mem, m_w_mem_kv, m_w_gla_o, m_w_fox_o, m_w_mem_o, m_w_out, m_g_ffn, m_w_ff1, m_w_ff2, m_g_final, v_g_mix, v_w_in, v_w_alpha_up, v_b_alpha, v_b_forget, v_g_gla_head, v_g_mem, v_w_mem_kv, v_w_gla_o, v_w_fox_o, v_w_mem_o, v_w_out, v_g_ffn, v_w_ff1, v_w_ff2, v_g_final):
    given = dict(x=x, mem=mem, g_mix=g_mix, w_in=w_in, w_alpha_up=w_alpha_up, b_alpha=b_alpha, b_forget=b_forget, g_gla_head=g_gla_head, g_mem=g_mem, w_mem_kv=w_mem_kv, w_gla_o=w_gla_o, w_fox_o=w_fox_o, w_mem_o=w_mem_o, w_out=w_out, g_ffn=g_ffn, w_ff1=w_ff1, w_ff2=w_ff2, g_final=g_final, loss_target=loss_target, m_g_mix=m_g_mix, m_w_in=m_w_in, m_w_alpha_up=m_w_alpha_up, m_b_alpha=m_b_alpha, m_b_forget=m_b_forget, m_g_gla_head=m_g_gla_head, m_g_mem=m_g_mem, m_w_mem_kv=m_w_mem_kv, m_w_gla_o=m_w_gla_o, m_w_fox_o=m_w_fox_o, m_w_mem_o=m_w_mem_o, m_w_out=m_w_out, m_g_ffn=m_g_ffn, m_w_ff1=m_w_ff1, m_w_ff2=m_w_ff2, m_g_final=m_g_final, v_g_mix=v_g_mix, v_w_in=v_w_in, v_w_alpha_up=v_w_alpha_up, v_b_alpha=v_b_alpha, v_b_forget=v_b_forget, v_g_gla_head=v_g_gla_head, v_g_mem=v_g_mem, v_w_mem_kv=v_w_mem_kv, v_w_gla_o=v_w_gla_o, v_w_fox_o=v_w_fox_o, v_w_mem_o=v_w_mem_o, v_w_out=v_w_out, v_g_ffn=v_g_ffn, v_w_ff1=v_w_ff1, v_w_ff2=v_w_ff2, v_g_final=v_g_final)
    weights = {n: given[n] for n in TWIN_WEIGHTS}
    shared = {n: given[n] for n in SHARED_INPUTS}
    per_example = {n: given[n] for n in ['x', 'mem']}
    grad_fn = _jax.value_and_grad(_loss, argnums=(0, 1))

    def one_microbatch(ex, loss_target):
        ex = dict(ex)
        diff = ex.pop(TWIN_DIFF_INPUT)
        return grad_fn(weights, diff, {**shared, **ex}, loss_target)

    if N_MICROBATCH == 1:
        loss, (grad_w, grad_x) = one_microbatch(per_example, given["loss_target"])
    else:
        def body(carry, xs):
            loss_sum, grad_sum = carry
            l_k, (gw_k, gx_k) = one_microbatch(xs[0], xs[1])
            with _jax.named_scope("update"):
                return (loss_sum + l_k, _jax.tree.map(_jnp.add, grad_sum, gw_k)), gx_k

        init = (_jnp.zeros((), _jnp.float32), _jax.tree.map(_jnp.zeros_like, weights))
        (loss, grad_w), grad_x = _jax.lax.scan(body, init, (per_example, given["loss_target"]))
    with _jax.named_scope("update"):
        delta_w, new_m, new_v = {}, {}, {}
        for n in TWIN_WEIGHTS:
            delta_w[n], new_m[n], new_v[n] = _adamw(weights[n], grad_w[n], given["m_" + n], given["v_" + n])
    return (loss, grad_x, *[grad_w[n] for n in TWIN_WEIGHTS], *[delta_w[n] for n in TWIN_WEIGHTS],
            *[new_m[n] for n in TWIN_WEIGHTS], *[new_v[n] for n in TWIN_WEIGHTS])
```

```python
import functools

import numpy as np
import jax
import jax.numpy as jnp
from jax import lax
from jax.experimental import pallas as pl
from jax.experimental.pallas import tpu as pltpu

F32 = jnp.float32
BF16 = jnp.bfloat16
HI = lax.Precision.HIGHEST
MESH = pl.DeviceIdType.MESH

EPS = 1e-6
D = 1024
CHUNK = 64
GLA_TAU = 16.0
N_CHIPS = 4
N_DEV = 8
VMEM_LIMIT_BYTES = 56 * 1024 * 1024

ADAM_LR, ADAM_B1, ADAM_B2, ADAM_EPS, ADAM_WD, ADAM_STEP = 0.001, 0.9, 0.999, 1e-08, 0.01, 10

PM_W = 6656
PE_W = 128
DP_W = 7168
C_GQ, C_GK, C_GV, C_GG, C_FQ, C_FK, C_FV, C_MQ = 3072, 3328, 3584, 4096, 4608, 5120, 5632, 6144
FF_LANE = 16

WEIGHTS = ("w_in", "w_alpha_up", "w_mem_kv", "w_gla_o", "w_fox_o", "w_mem_o", "w_out", "w_ff1", "w_ff2")
SHARD_AXIS = {"w_in": 1, "w_alpha_up": 1, "w_mem_kv": 0, "w_gla_o": 1, "w_fox_o": 1, "w_mem_o": 1, "w_out": 0,
              "w_ff1": 1, "w_ff2": 0}
SMALL = ("g_mix", "g_mem", "g_ffn", "g_final", "b_alpha", "b_forget", "g_gla_head")
PACK_W = 1024
PACK_ROWS = 4640


def _cp(*sem):
    return pltpu.CompilerParams(dimension_semantics=sem, vmem_limit_bytes=VMEM_LIMIT_BYTES)


def _dot(a, b, **kw):
    return jnp.dot(a, b, preferred_element_type=F32, **kw)


def _dot_nt(a, b, **kw):
    return lax.dot_general(a, b, (((1,), (1,)), ((), ())), preferred_element_type=F32, **kw)


def _dot_tn(a, b, **kw):
    return lax.dot_general(a, b, (((0,), (0,)), ((), ())), preferred_element_type=F32, **kw)


def _sigmoid(x):
    return 1.0 / (1.0 + jnp.exp(-x))


def _log_sigmoid(x):
    return -(jnp.maximum(-x, 0.0) + jnp.log1p(jnp.exp(-jnp.abs(x))))


def _fold8(x):
    m, n = x.shape
    return x.reshape(m // 8, 8, n).sum(axis=0)


def _iota(shape, dim):
    return lax.broadcasted_iota(jnp.int32, shape, dim)


def _row_tile(s):
    return min(s, 512)


def _mm_nn(a, b, *, out_dtype, tm, tn, tk, name, a_fn=None, epi=None, extra=None):
    m, k = a.shape
    _, n = b.shape
    nk = k // tk

    def body(*refs):
        if extra is None:
            a_ref, b_ref, o_ref, acc = refs
            x_ref = None
        else:
            a_ref, b_ref, x_ref, o_ref, acc = refs
        kk = pl.program_id(2)

        @pl.when(kk == 0)
        def _():
            acc[...] = jnp.zeros_like(acc)

        at = a_ref[...]
        if a_fn is not None:
            at = a_fn(at)
        acc[...] += _dot(at, b_ref[...])

        @pl.when(kk == nk - 1)
        def _():
            r = acc[...]
            if epi is not None:
                r = epi(r, None if x_ref is None else x_ref[...])
            o_ref[...] = r.astype(out_dtype)

    in_specs = [pl.BlockSpec((tm, tk), lambda i, j, kk: (i, kk)), pl.BlockSpec((tk, tn), lambda i, j, kk: (kk, j))]
    args = [a, b]
    if extra is not None:
        in_specs.append(pl.BlockSpec((tm, tn), lambda i, j, kk: (i, j)))
        args.append(extra)
    return pl.pallas_call(
        body, grid=(m // tm, n // tn, nk), in_specs=in_specs,
        out_specs=pl.BlockSpec((tm, tn), lambda i, j, kk: (i, j)),
        out_shape=jax.ShapeDtypeStruct((m, n), out_dtype),
        scratch_shapes=[pltpu.VMEM((tm, tn), F32)], name=name,
        compiler_params=_cp("parallel", "parallel", "arbitrary"))(*args)


def _mm_tn(a, b, *, tm, tn, ts, name, a_fn=None):
    s, m = a.shape
    _, n = b.shape
    ns = s // ts

    def body(a_ref, b_ref, o_ref, acc):
        kk = pl.program_id(2)

        @pl.when(kk == 0)
        def _():
            acc[...] = jnp.zeros_like(acc)

        at = a_ref[...]
        if a_fn is not None:
            at = a_fn(at)
        acc[...] += _dot_tn(at, b_ref[...])

        @pl.when(kk == ns - 1)
        def _():
            o_ref[...] = acc[...]

    return pl.pallas_call(
        body, grid=(m // tm, n // tn, ns),
        in_specs=[pl.BlockSpec((ts, tm), lambda i, j, kk: (kk, i)), pl.BlockSpec((ts, tn), lambda i, j, kk: (kk, j))],
        out_specs=pl.BlockSpec((tm, tn), lambda i, j, kk: (i, j)),
        out_shape=jax.ShapeDtypeStruct((m, n), F32),
        scratch_shapes=[pltpu.VMEM((tm, tn), F32)], name=name,
        compiler_params=_cp("parallel", "parallel", "arbitrary"))(a, b)


def _relu2_bf16(t):
    r = jnp.maximum(t.astype(F32), 0.0)
    return (r * r).astype(BF16)


def _rms_fwd(x, g, name):
    s, d = x.shape
    tm = min(s, 512)

    def body(x_ref, g_ref, u_ref, r_ref):
        xv = x_ref[...]
        r = lax.rsqrt(jnp.mean(xv * xv, axis=-1, keepdims=True) + EPS)
        u_ref[...] = ((xv * r) * g_ref[...]).astype(BF16)
        r_ref[...] = r

    return pl.pallas_call(
        body, grid=(s // tm,),
        in_specs=[pl.BlockSpec((tm, d), lambda i: (i, 0)), pl.BlockSpec((1, d), lambda i: (0, 0))],
        out_specs=[pl.BlockSpec((tm, d), lambda i: (i, 0)), pl.BlockSpec((tm, 1), lambda i: (i, 0))],
        out_shape=[jax.ShapeDtypeStruct((s, d), BF16), jax.ShapeDtypeStruct((s, 1), F32)],
        name=name, compiler_params=_cp("parallel"))(x, g)


def _mm_norm_bwd(a, b, xin, r, g, dres, *, tk, name, want_bf16):
    s, k = a.shape
    tm = min(s, 256)
    nk = k // tk

    def body(a_ref, b_ref, x_ref, r_ref, g_ref, dres_ref, *rest):
        if want_bf16:
            dx_ref, dxb_ref, dg_ref, acc = rest
        else:
            dx_ref, dg_ref, acc = rest
        i, kk = pl.program_id(0), pl.program_id(1)

        @pl.when(kk == 0)
        def _():
            acc[...] = jnp.zeros_like(acc)

        @pl.when((i == 0) & (kk == 0))
        def _():
            dg_ref[...] = jnp.zeros_like(dg_ref)

        acc[...] += _dot(a_ref[...], b_ref[...])

        @pl.when(kk == nk - 1)
        def _():
            du = acc[...]
            xn = x_ref[...] * r_ref[...]
            dg_ref[...] += _fold8(du * xn)
            dxn = du * g_ref[...]
            dx = dres_ref[...] + r_ref[...] * (dxn - xn * jnp.mean(dxn * xn, axis=-1, keepdims=True))
            dx_ref[...] = dx
            if want_bf16:
                dxb_ref[...] = dx.astype(BF16)

    row = lambda i, kk: (i, 0)
    out_specs = [pl.BlockSpec((tm, D), row)]
    out_shape = [jax.ShapeDtypeStruct((s, D), F32)]
    if want_bf16:
        out_specs.append(pl.BlockSpec((tm, D), row))
        out_shape.append(jax.ShapeDtypeStruct((s, D), BF16))
    out_specs.append(pl.BlockSpec((8, D), lambda i, kk: (0, 0)))
    out_shape.append(jax.ShapeDtypeStruct((8, D), F32))
    return pl.pallas_call(
        body, grid=(s // tm, nk),
        in_specs=[pl.BlockSpec((tm, tk), lambda i, kk: (i, kk)), pl.BlockSpec((tk, D), lambda i, kk: (kk, 0)),
                  pl.BlockSpec((tm, D), row), pl.BlockSpec((tm, 1), row), pl.BlockSpec((1, D), lambda i, kk: (0, 0)),
                  pl.BlockSpec((tm, D), row)],
        out_specs=out_specs, out_shape=out_shape, scratch_shapes=[pltpu.VMEM((tm, D), F32)],
        name=name, compiler_params=_cp("arbitrary", "arbitrary"))(a, b, xin, r, g, dres)


def _gla_consts():
    lmask = _iota((4 * CHUNK, CHUNK), 0) % CHUNK >= _iota((4 * CHUNK, CHUNK), 1)
    hmask = _iota((256, 256), 0) // CHUNK == _iota((256, 256), 1) // CHUNK
    bd = _iota((256, 512), 0) // CHUNK == _iota((256, 512), 1) // 128
    return lmask, hmask, bd


def _fold_heads(x):
    return x[0:64] + x[64:128] + x[128:192] + x[192:256]


def _gla_chunk(lac, qc, kc):
    tri = (_iota((CHUNK, CHUNK), 0) >= _iota((CHUNK, CHUNK), 1)).astype(F32)
    b = _dot(tri, lac, precision=HI)
    bl = b[CHUNK - 1:CHUNK, :]
    ep, en, ek = jnp.exp(b), jnp.exp(-b), jnp.exp(bl - b)
    decb = jnp.exp(_dot_tn(lac, jnp.ones((CHUNK, 128), F32), precision=HI))
    decb = jnp.concatenate([decb] * 4, axis=1)
    return bl, ep, en, ek, decb, qc * ep, qc * en, kc * en, kc * ep, kc * ek


def _gla_fwd(pm, pe, wau_p, b_alpha):
    s = pm.shape[0]
    t = _row_tile(s)
    nc = t // CHUNK

    def body(q_ref, k_ref, v_ref, e_ref, wau_ref, ba_ref, o_ref, st_ref, state, la_scr):
        @pl.when(pl.program_id(0) == 0)
        def _():
            state[...] = jnp.zeros_like(state)

        z = _dot(e_ref[...].astype(BF16), wau_ref[...]) + ba_ref[...]
        la_scr[...] = _log_sigmoid(z) * (1.0 / GLA_TAU)
        lmask, hmask, bd = _gla_consts()

        def chunk(c, carry):
            rows = pl.ds(pl.multiple_of(c * CHUNK, CHUNK), CHUNK)
            qc = q_ref[rows, :].astype(F32) * 0.125
            kc = k_ref[rows, :].astype(F32)
            vc = v_ref[rows, :]
            _, _, _, _, decb, qp, qn, kn, kp, kk = _gla_chunk(la_scr[rows, :], qc, kc)
            qs = jnp.where(hmask, jnp.concatenate([qp] * 4, axis=0), 0.0).astype(BF16)
            qns = jnp.where(hmask, jnp.concatenate([qn] * 4, axis=0), 0.0).astype(BF16)
            attn = jnp.where(lmask, _dot_nt(qs, kn.astype(BF16)), _dot_nt(qns, kp.astype(BF16))).astype(BF16)
            st = state[...]
            o_intra = _fold_heads(jnp.where(bd, _dot(attn, vc), 0.0))
            o_ref[rows, :] = o_intra + _dot(qp.astype(BF16), st.astype(BF16))
            for h in range(4):
                st_ref[c, :, 128 * h:128 * (h + 1)] = st[64 * h:64 * (h + 1), 128 * h:128 * (h + 1)]
            kv = jnp.where(bd, _dot_tn(kk.astype(BF16), vc), 0.0)
            state[...] = st * decb + kv
            return carry

        lax.fori_loop(0, nc, chunk, 0)

    return pl.pallas_call(
        body, grid=(s // t,),
        in_specs=[pl.BlockSpec((t, 256), lambda i: (i, C_GQ // 256)), pl.BlockSpec((t, 256), lambda i: (i, C_GK // 256)),
                  pl.BlockSpec((t, 512), lambda i: (i, C_GV // 512)), pl.BlockSpec((t, PE_W), lambda i: (i, 0)),
                  pl.BlockSpec((PE_W, 256), lambda i: (0, 0)), pl.BlockSpec((1, 256), lambda i: (0, 0))],
        out_specs=[pl.BlockSpec((t, 512), lambda i: (i, 0)), pl.BlockSpec((nc, CHUNK, 512), lambda i: (i, 0, 0))],
        out_shape=[jax.ShapeDtypeStruct((s, 512), F32), jax.ShapeDtypeStruct((s // CHUNK, CHUNK, 512), F32)],
        scratch_shapes=[pltpu.VMEM((256, 512), F32), pltpu.VMEM((t, 256), F32)],
        name="gla_fwd", compiler_params=_cp("arbitrary"))(pm, pm, pm, pe, wau_p, b_alpha)


def _gla_bwd(pm, pe, wau_p, wau_pt, b_alpha, do, states):
    s = pm.shape[0]
    t = _row_tile(s)
    nc = t // CHUNK
    nb = s // t

    def body(q_ref, k_ref, v_ref, e_ref, wau_ref, waut_ref, ba_ref, do_ref, st_ref,
             dq_ref, dk_ref, dv_ref, de_ref, dwau_ref, dba_ref, gstate, la_scr, dla_scr):
        @pl.when(pl.program_id(0) == 0)
        def _():
            gstate[...] = jnp.zeros_like(gstate)
            dwau_ref[...] = jnp.zeros_like(dwau_ref)
            dba_ref[...] = jnp.zeros_like(dba_ref)

        eb = e_ref[...].astype(BF16)
        z = _dot(eb, wau_ref[...]) + ba_ref[...]
        la_scr[...] = _log_sigmoid(z) * (1.0 / GLA_TAU)
        lmask, hmask, bd = _gla_consts()
        triu = (_iota((CHUNK, CHUNK), 0) <= _iota((CHUNK, CHUNK), 1)).astype(F32)
        last_row = _iota((CHUNK, 256), 0) == CHUNK - 1

        def chunk(cc, carry):
            c = nc - 1 - cc
            rows = pl.ds(pl.multiple_of(c * CHUNK, CHUNK), CHUNK)
            qc = q_ref[rows, :].astype(F32) * 0.125
            kc = k_ref[rows, :].astype(F32)
            vc = v_ref[rows, :]
            dob = do_ref[rows, :]
            bl, ep, en, ek, decb, qp, qn, kn, kp, kk = _gla_chunk(la_scr[rows, :], qc, kc)
            qs = jnp.where(hmask, jnp.concatenate([qp] * 4, axis=0), 0.0).astype(BF16)
            qns = jnp.where(hmask, jnp.concatenate([qn] * 4, axis=0), 0.0).astype(BF16)
            knb, kpb = kn.astype(BF16), kp.astype(BF16)
            attn = jnp.where(lmask, _dot_nt(qs, knb), _dot_nt(qns, kpb)).astype(BF16)
            st = jnp.where(bd, jnp.concatenate([st_ref[c]] * 4, axis=0), 0.0)
            g = gstate[...]
            gb = g.astype(BF16)
            do_s = jnp.where(bd, jnp.concatenate([dob] * 4, axis=0), jnp.zeros((), BF16))
            dattn = _dot_nt(do_s, vc)
            dv_ref[rows, :] = (_dot_tn(attn, do_s) + _dot(kk.astype(BF16), gb)).astype(BF16)
            dac = jnp.where(lmask, dattn, 0.0).astype(BF16)
            daa = jnp.where(lmask, 0.0, dattn).astype(BF16)
            dqp = _fold_heads(jnp.where(hmask, _dot(dac, knb), 0.0)) + _dot_nt(dob, st.astype(BF16))
            dqn = _fold_heads(jnp.where(hmask, _dot(daa, kpb), 0.0))
            dkn = _dot_tn(dac, qs)
            dkp = _dot_tn(daa, qns)
            dkk = _dot_nt(vc, gb)
            ddec = _dot_nt(jnp.ones((8, 512), F32), g * st, precision=HI)[0:1, :]
            gstate[...] = decb * g + jnp.where(bd, _dot_tn(qp.astype(BF16), dob), 0.0)
            dq_ref[rows, :] = ((dqp * ep + dqn * en) * 0.125).astype(BF16)
            dk_ref[rows, :] = (dkn * en + dkp * ep + dkk * ek).astype(BF16)
            dek = dkk * kc * ek
            db = (dqp * qc + dkp * kc) * ep - (dqn * qc + dkn * kc) * en - dek
            dbl = jnp.sum(dek, axis=0, keepdims=True) + ddec * jnp.exp(bl)
            db = db + jnp.where(last_row, dbl, 0.0)
            dla_scr[rows, :] = _dot(triu, db, precision=HI)
            return carry

        lax.fori_loop(0, nc, chunk, 0)
        dz = dla_scr[...] * (1.0 / GLA_TAU) * _sigmoid(-z)
        dzb = dz.astype(BF16)
        dwau_ref[...] += _dot_tn(eb, dzb)
        dba_ref[...] += _fold8(dz)
        de_ref[...] = _dot(dzb, waut_ref[...])

    rev = lambda i: nb - 1 - i
    return pl.pallas_call(
        body, grid=(nb,),
        in_specs=[pl.BlockSpec((t, 256), lambda i: (rev(i), C_GQ // 256)), pl.BlockSpec((t, 256), lambda i: (rev(i), C_GK // 256)),
                  pl.BlockSpec((t, 512), lambda i: (rev(i), C_GV // 512)), pl.BlockSpec((t, PE_W), lambda i: (rev(i), 0)),
                  pl.BlockSpec((PE_W, 256), lambda i: (0, 0)), pl.BlockSpec((256, PE_W), lambda i: (0, 0)),
                  pl.BlockSpec((1, 256), lambda i: (0, 0)), pl.BlockSpec((t, 512), lambda i: (rev(i), 0)),
                  pl.BlockSpec((nc, CHUNK, 512), lambda i: (rev(i), 0, 0))],
        out_specs=[pl.BlockSpec((t, 256), lambda i: (rev(i), 0)), pl.BlockSpec((t, 256), lambda i: (rev(i), 0)),
                   pl.BlockSpec((t, 512), lambda i: (rev(i), 0)), pl.BlockSpec((t, PE_W), lambda i: (rev(i), 0)),
                   pl.BlockSpec((PE_W, 256), lambda i: (0, 0)), pl.BlockSpec((8, 256), lambda i: (0, 0))],
        out_shape=[jax.ShapeDtypeStruct((s, 256), BF16), jax.ShapeDtypeStruct((s, 256), BF16),
                   jax.ShapeDtypeStruct((s, 512), BF16), jax.ShapeDtypeStruct((s, PE_W), F32),
                   jax.ShapeDtypeStruct((PE_W, 256), F32), jax.ShapeDtypeStruct((8, 256), F32)],
        scratch_shapes=[pltpu.VMEM((256, 512), F32), pltpu.VMEM((t, 256), F32), pltpu.VMEM((t, 256), F32)],
        name="gla_bwd", compiler_params=_cp("arbitrary"))(pm, pm, pm, pe, wau_p, wau_pt, b_alpha, do, states)


def _fcum_fwd(pe, bias):
    s = pe.shape[0]
    t = min(s, 256)

    def body(e_ref, b_ref, f_ref, carry):
        @pl.when(pl.program_id(0) == 0)
        def _():
            carry[...] = jnp.zeros_like(carry)

        lf = _log_sigmoid(e_ref[...] + b_ref[...])
        tri = (_iota((t, t), 0) >= _iota((t, t), 1)).astype(F32)
        f = _dot(tri, lf, precision=HI) + carry[0:1, :]
        f_ref[...] = f
        carry[...] = jnp.broadcast_to(f[t - 1:t, :], carry.shape)

    return pl.pallas_call(
        body, grid=(s // t,),
        in_specs=[pl.BlockSpec((t, PE_W), lambda i: (i, 0)), pl.BlockSpec((1, PE_W), lambda i: (0, 0))],
        out_specs=pl.BlockSpec((t, PE_W), lambda i: (i, 0)),
        out_shape=jax.ShapeDtypeStruct((s, PE_W), F32), scratch_shapes=[pltpu.VMEM((8, PE_W), F32)],
        name="fcum_fwd", compiler_params=_cp("arbitrary"))(pe, bias)


def _fcum_bwd(pe, bias, df):
    s = pe.shape[0]
    t = min(s, 256)
    nb = s // t

    def body(e_ref, b_ref, df_ref, de_ref, db_ref, carry):
        @pl.when(pl.program_id(0) == 0)
        def _():
            carry[...] = jnp.zeros_like(carry)
            db_ref[...] = jnp.zeros_like(db_ref)

        triu = (_iota((t, t), 0) <= _iota((t, t), 1)).astype(F32)
        dlf = _dot(triu, df_ref[...], precision=HI) + carry[0:1, :]
        carry[...] = jnp.broadcast_to(dlf[0:1, :], carry.shape)
        lane = _iota((t, PE_W), 1)
        dff = jnp.where((lane >= FF_LANE) & (lane < FF_LANE + 8), dlf * _sigmoid(-(e_ref[...] + b_ref[...])), 0.0)
        de_ref[...] = dff
        db_ref[...] += _fold8(dff)

    rev = lambda i: (nb - 1 - i, 0)
    return pl.pallas_call(
        body, grid=(nb,),
        in_specs=[pl.BlockSpec((t, PE_W), rev), pl.BlockSpec((1, PE_W), lambda i: (0, 0)), pl.BlockSpec((t, PE_W), rev)],
        out_specs=[pl.BlockSpec((t, PE_W), rev), pl.BlockSpec((8, PE_W), lambda i: (0, 0))],
        out_shape=[jax.ShapeDtypeStruct((s, PE_W), F32), jax.ShapeDtypeStruct((8, PE_W), F32)],
        scratch_shapes=[pltpu.VMEM((8, PE_W), F32)],
        name="fcum_bwd", compiler_params=_cp("arbitrary"))(pe, bias, df)


def _fox_fwd(q, k, v, f_col, f_rows):
    nh, s, dh = q.shape
    t = _row_tile(s)

    def body(q_ref, k_ref, v_ref, fc_ref, fr_ref, o_ref, lse_ref):
        i = pl.program_id(1)
        qb = q_ref[0]
        fq = fc_ref[0]

        def scores(j):
            rows = pl.ds(pl.multiple_of(j * t, t), t)
            return _dot_nt(qb, k_ref[0, rows, :]) + fq - fr_ref[0, j], v_ref[0, rows, :]

        def update(sc, vb, carry):
            m, l, acc = carry
            m2 = jnp.maximum(m, jnp.max(sc, axis=-1, keepdims=True))
            p = jnp.exp(sc - m2)
            a = jnp.exp(m - m2)
            return m2, a * l + jnp.sum(p, axis=-1, keepdims=True), a * acc + _dot(p.astype(BF16), vb)

        def step(j, carry):
            sc, vb = scores(j)
            return update(sc, vb, carry)

        init = (jnp.full((t, 1), -1e30, F32), jnp.zeros((t, 1), F32), jnp.zeros((t, dh), F32))
        carry = lax.fori_loop(0, i, step, init)
        sc, vb = scores(i)
        sc = jnp.where(_iota((t, t), 1) <= _iota((t, t), 0), sc, -1e30)
        m, l, acc = update(sc, vb, carry)
        o_ref[0] = (acc / l).astype(BF16)
        lse_ref[0] = m + jnp.log(l)

    return pl.pallas_call(
        body, grid=(nh, s // t),
        in_specs=[pl.BlockSpec((1, t, dh), lambda h, i: (h, i, 0)), pl.BlockSpec((1, s, dh), lambda h, i: (h, 0, 0)),
                  pl.BlockSpec((1, s, dh), lambda h, i: (h, 0, 0)), pl.BlockSpec((1, t, 1), lambda h, i: (h, i, 0)),
                  pl.BlockSpec((1, s // t, 1, t), lambda h, i: (h, 0, 0, 0))],
        out_specs=[pl.BlockSpec((1, t, dh), lambda h, i: (h, i, 0)), pl.BlockSpec((1, t, 1), lambda h, i: (h, i, 0))],
        out_shape=[jax.ShapeDtypeStruct((nh, s, dh), BF16), jax.ShapeDtypeStruct((nh, s, 1), F32)],
        name="fox_fwd", compiler_params=_cp("parallel", "arbitrary"))(q, k, v, f_col, f_rows)


def _fox_bwd(q, k, v, do, f_col, f_rows, lse_rows, delta_rows):
    nh, s, dh = q.shape
    t = _row_tile(s)
    nb = s // t

    def body(k_ref, v_ref, fc_ref, q_ref, do_ref, fr_ref, lse_ref, dl_ref, dq_ref, dk_ref, dv_ref, dfk_ref, dfq_ref):
        j = pl.program_id(1)

        @pl.when(j == 0)
        def _():
            dq_ref[...] = jnp.zeros_like(dq_ref)
            dfq_ref[...] = jnp.zeros_like(dfq_ref)

        kb = k_ref[0]
        vb = v_ref[0]
        kt = (kb.astype(F32) * 0.125).astype(BF16)
        fk = fc_ref[0]

        def tile(i, masked, carry):
            dk, dv, dfk = carry
            rows = pl.ds(pl.multiple_of(i * t, t), t)
            qi = q_ref[0, rows, :]
            doi = do_ref[0, rows, :]
            st = _dot_nt(kb, qi) + fr_ref[0, i] - fk - lse_ref[0, i]
            if masked:
                st = jnp.where(_iota((t, t), 0) <= _iota((t, t), 1), st, -1e30)
            pt = jnp.exp(st)
            dst = pt * (_dot_nt(vb, doi) - dl_ref[0, i])
            dsb = dst.astype(BF16)
            dq_ref[0, rows, :] += _dot_tn(dsb, kt)
            dfq_ref[0, i] += jnp.sum(dst, axis=0, keepdims=True)
            return dk + _dot(dsb, qi), dv + _dot(pt.astype(BF16), doi), dfk + jnp.sum(dst, axis=-1, keepdims=True)

        init = (jnp.zeros((t, dh), F32), jnp.zeros((t, dh), F32), jnp.zeros((t, 1), F32))
        carry = tile(j, True, init)
        dk, dv, dfk = lax.fori_loop(j + 1, nb, lambda i, c: tile(i, False, c), carry)
        dk_ref[0] = dk.astype(BF16)
        dv_ref[0] = dv.astype(BF16)
        dfk_ref[0] = dfk

    blk = lambda h, j: (h, j, 0)
    whole = lambda h, j: (h, 0, 0)
    rows4 = pl.BlockSpec((1, nb, 1, t), lambda h, j: (h, 0, 0, 0))
    return pl.pallas_call(
        body, grid=(nh, nb),
        in_specs=[pl.BlockSpec((1, t, dh), blk), pl.BlockSpec((1, t, dh), blk), pl.BlockSpec((1, t, 1), blk),
                  pl.BlockSpec((1, s, dh), whole), pl.BlockSpec((1, s, dh), whole), rows4, rows4, rows4],
        out_specs=[pl.BlockSpec((1, s, dh), whole), pl.BlockSpec((1, t, dh), blk), pl.BlockSpec((1, t, dh), blk),
                   pl.BlockSpec((1, t, 1), blk), rows4],
        out_shape=[jax.ShapeDtypeStruct((nh, s, dh), F32), jax.ShapeDtypeStruct((nh, s, dh), BF16),
                   jax.ShapeDtypeStruct((nh, s, dh), BF16), jax.ShapeDtypeStruct((nh, s, 1), F32),
                   jax.ShapeDtypeStruct((nh, nb, 1, t), F32)],
        name="fox_bwd", compiler_params=_cp("parallel", "arbitrary"))(k, v, f_col, q, do, f_rows, lse_rows, delta_rows)


MEM_SCALE = 128 ** -0.5


def _mem_attn_fwd(pm, mkv):
    s = pm.shape[0]
    t = _row_tile(s)
    nm = mkv.shape[0]

    def body(q_ref, mk_ref, mv_ref, o_ref):
        for h in range(4):
            cols = slice(128 * h, 128 * (h + 1))
            sc = _dot_nt(q_ref[:, cols], mk_ref[:, cols]) * MEM_SCALE
            p = jnp.exp(sc - jnp.max(sc, axis=-1, keepdims=True))
            p = p / jnp.sum(p, axis=-1, keepdims=True)
            o_ref[:, cols] = _dot(p.astype(BF16), mv_ref[:, cols]).astype(BF16)

    return pl.pallas_call(
        body, grid=(s // t,),
        in_specs=[pl.BlockSpec((t, 512), lambda i: (i, C_MQ // 512)), pl.BlockSpec((nm, 512), lambda i: (0, 0)),
                  pl.BlockSpec((nm, 512), lambda i: (0, 1))],
        out_specs=pl.BlockSpec((t, 512), lambda i: (i, 0)),
        out_shape=jax.ShapeDtypeStruct((s, 512), BF16),
        name="mem_attn_fwd", compiler_params=_cp("parallel"))(pm, mkv, mkv)


def _mem_attn_bwd(pm, mkv, do):
    s = pm.shape[0]
    t = _row_tile(s)
    nm = mkv.shape[0]

    def body(q_ref, mk_ref, mv_ref, do_ref, dq_ref, dmk_ref, dmv_ref):
        @pl.when(pl.program_id(0) == 0)
        def _():
            dmk_ref[...] = jnp.zeros_like(dmk_ref)
            dmv_ref[...] = jnp.zeros_like(dmv_ref)

        for h in range(4):
            cols = slice(128 * h, 128 * (h + 1))
            qh, kh, vh, doh = q_ref[:, cols], mk_ref[:, cols], mv_ref[:, cols], do_ref[:, cols]
            sc = _dot_nt(qh, kh) * MEM_SCALE
            p = jnp.exp(sc - jnp.max(sc, axis=-1, keepdims=True))
            p = p / jnp.sum(p, axis=-1, keepdims=True)
            pb = p.astype(BF16)
            dp = _dot_nt(doh, vh)
            ds = (p * (dp - jnp.sum(p * dp, axis=-1, keepdims=True)) * MEM_SCALE).astype(BF16)
            dq_ref[:, cols] = _dot(ds, kh).astype(BF16)
            dmk_ref[:, cols] += _dot_tn(ds, qh)
            dmv_ref[:, cols] += _dot_tn(pb, doh)

    return pl.pallas_call(
        body, grid=(s // t,),
        in_specs=[pl.BlockSpec((t, 512), lambda i: (i, C_MQ // 512)), pl.BlockSpec((nm, 512), lambda i: (0, 0)),
                  pl.BlockSpec((nm, 512), lambda i: (0, 1)), pl.BlockSpec((t, 512), lambda i: (i, 0))],
        out_specs=[pl.BlockSpec((t, 512), lambda i: (i, 0)), pl.BlockSpec((nm, 512), lambda i: (0, 0)),
                   pl.BlockSpec((nm, 512), lambda i: (0, 0))],
        out_shape=[jax.ShapeDtypeStruct((s, 512), BF16), jax.ShapeDtypeStruct((nm, 512), F32),
                   jax.ShapeDtypeStruct((nm, 512), F32)],
        name="mem_attn_bwd", compiler_params=_cp("arbitrary"))(pm, mkv, mkv, do)


def _gain_grad(dxn_g, x, r, name):
    m, d = x.shape

    def body(d_ref, x_ref, r_ref, o_ref):
        o_ref[...] = _fold8(d_ref[...] * (x_ref[...] * r_ref[...]))

    return pl.pallas_call(body, out_shape=jax.ShapeDtypeStruct((8, d), F32), name=name,
                          compiler_params=pltpu.CompilerParams(vmem_limit_bytes=VMEM_LIMIT_BYTES))(dxn_g, x, r)


def _head_norm(o, gh):
    xs, rs = [], []
    for h in range(4):
        oh = o[:, 128 * h:128 * (h + 1)]
        r = lax.rsqrt(jnp.mean(oh * oh, axis=-1, keepdims=True) + EPS)
        xs.append(oh * r)
        rs.append(r)
    return xs, rs


def _merge_fwd(x, pm, o_gla, o_fox, o_mem, g_head, wg, wf, wm, wo, g_ffn):
    s = x.shape[0]
    t = min(s, 256)

    def body(x_ref, g0_ref, g1_ref, g2_ref, gg_ref, og_ref, of_ref, om_ref, gh_ref, wg_ref, wf_ref, wm_ref, wo_ref, gf_ref,
             mg_ref, h1_ref, u2_ref, r2_ref):
        xs, _ = _head_norm(og_ref[...], None)
        gg = gg_ref[...].astype(F32)
        sil = gg * _sigmoid(gg)
        ogn = jnp.concatenate(xs, axis=1) * gh_ref[...] * sil
        merged = (_sigmoid(g0_ref[...].astype(F32)) * _dot(ogn.astype(BF16), wg_ref[...])
                  + _sigmoid(g1_ref[...].astype(F32)) * _dot(of_ref[...], wf_ref[...])
                  + _sigmoid(g2_ref[...].astype(F32)) * _dot(om_ref[...], wm_ref[...]))
        mb = merged.astype(BF16)
        mg_ref[...] = mb
        h1 = x_ref[...] + _dot(mb, wo_ref[...])
        h1_ref[...] = h1
        r = lax.rsqrt(jnp.mean(h1 * h1, axis=-1, keepdims=True) + EPS)
        u2_ref[...] = ((h1 * r) * gf_ref[...]).astype(BF16)
        r2_ref[...] = r

    row = lambda i: (i, 0)
    const = lambda i: (0, 0)
    return pl.pallas_call(
        body, grid=(s // t,),
        in_specs=[pl.BlockSpec((t, D), row), pl.BlockSpec((t, D), lambda i: (i, 0)), pl.BlockSpec((t, D), lambda i: (i, 1)),
                  pl.BlockSpec((t, D), lambda i: (i, 2)), pl.BlockSpec((t, 512), lambda i: (i, C_GG // 512)),
                  pl.BlockSpec((t, 512), row), pl.BlockSpec((t, 512), row), pl.BlockSpec((t, 512), row),
                  pl.BlockSpec((1, 512), const), pl.BlockSpec((512, D), const), pl.BlockSpec((512, D), const),
                  pl.BlockSpec((512, D), const), pl.BlockSpec((D, D), const), pl.BlockSpec((1, D), const)],
        out_specs=[pl.BlockSpec((t, D), row), pl.BlockSpec((t, D), row), pl.BlockSpec((t, D), row), pl.BlockSpec((t, 1), row)],
        out_shape=[jax.ShapeDtypeStruct((s, D), BF16), jax.ShapeDtypeStruct((s, D), F32),
                   jax.ShapeDtypeStruct((s, D), BF16), jax.ShapeDtypeStruct((s, 1), F32)],
        name="merge_fwd", compiler_params=_cp("parallel"))(x, pm, pm, pm, pm, o_gla, o_fox, o_mem, g_head, wg, wf, wm, wo, g_ffn)


def _merge_bwd(dh1b, pm, o_gla, o_fox, o_mem, g_head, wg, wf, wm, wgt, wft, wmt, wot):
    s = dh1b.shape[0]
    t = min(s, 256)

    def body(dh_ref, g0_ref, g1_ref, g2_ref, gg_ref, og_ref, of_ref, om_ref, gh_ref, wg_ref, wf_ref, wm_ref,
             wgt_ref, wft_ref, wmt_ref, wot_ref,
             dgt_ref, dgg_ref, dog_ref, dof_ref, dom_ref, dl_ref, dwg_ref, dwf_ref, dwm_ref, dgh_ref):
        @pl.when(pl.program_id(0) == 0)
        def _():
            dwg_ref[...] = jnp.zeros_like(dwg_ref)
            dwf_ref[...] = jnp.zeros_like(dwf_ref)
            dwm_ref[...] = jnp.zeros_like(dwm_ref)
            dgh_ref[...] = jnp.zeros_like(dgh_ref)

        dmerged = _dot(dh_ref[...], wot_ref[...])
        og = og_ref[...]
        xs, rs = _head_norm(og, None)
        on = jnp.concatenate(xs, axis=1)
        gg = gg_ref[...].astype(F32)
        sg = _sigmoid(gg)
        sil = gg * sg
        gh = gh_ref[...]
        ognb = (on * gh * sil).astype(BF16)
        ofb, omb = of_ref[...], om_ref[...]
        douts = []
        for idx, (gref, ob, w_ref, wt_ref, dw_ref) in enumerate((
                (g0_ref, ognb, wg_ref, wgt_ref, dwg_ref), (g1_ref, ofb, wf_ref, wft_ref, dwf_ref),
                (g2_ref, omb, wm_ref, wmt_ref, dwm_ref))):
            gt = _sigmoid(gref[...].astype(F32))
            y = _dot(ob, w_ref[...])
            dgt_ref[:, D * idx:D * (idx + 1)] = (dmerged * y * gt * (1.0 - gt)).astype(BF16)
            dy = (gt * dmerged).astype(BF16)
            dw_ref[...] += _dot_tn(ob, dy)
            douts.append(_dot(dy, wt_ref[...]))
        dogn, dof, dom = douts
        dof_ref[...] = dof.astype(BF16)
        dom_ref[...] = dom.astype(BF16)
        ind = (_iota((512, 128), 0) // 64 == _iota((512, 128), 1)).astype(F32)
        dl_ref[...] = _dot(dof.astype(BF16).astype(F32) * ofb.astype(F32), ind, precision=HI)
        dgg_ref[...] = (dogn * on * gh * (sg * (1.0 + gg * (1.0 - sg)))).astype(BF16)
        d_on = dogn * sil
        dgh_ref[...] += _fold8(d_on * on)
        dxn = d_on * gh
        outs = []
        for h in range(4):
            cols = slice(128 * h, 128 * (h + 1))
            dh_, xh = dxn[:, cols], xs[h]
            outs.append(rs[h] * (dh_ - xh * jnp.mean(dh_ * xh, axis=-1, keepdims=True)))
        dog_ref[...] = jnp.concatenate(outs, axis=1).astype(BF16)

    row = lambda i: (i, 0)
    const = lambda i: (0, 0)
    return pl.pallas_call(
        body, grid=(s // t,),
        in_specs=[pl.BlockSpec((t, D), row), pl.BlockSpec((t, D), lambda i: (i, 0)), pl.BlockSpec((t, D), lambda i: (i, 1)),
                  pl.BlockSpec((t, D), lambda i: (i, 2)), pl.BlockSpec((t, 512), lambda i: (i, C_GG // 512)),
                  pl.BlockSpec((t, 512), row), pl.BlockSpec((t, 512), row), pl.BlockSpec((t, 512), row),
                  pl.BlockSpec((1, 512), const), pl.BlockSpec((512, D), const), pl.BlockSpec((512, D), const),
                  pl.BlockSpec((512, D), const), pl.BlockSpec((D, 512), const), pl.BlockSpec((D, 512), const),
                  pl.BlockSpec((D, 512), const), pl.BlockSpec((D, D), const)],
        out_specs=[pl.BlockSpec((t, 3 * D), row), pl.BlockSpec((t, 512), row), pl.BlockSpec((t, 512), row),
                   pl.BlockSpec((t, 512), row), pl.BlockSpec((t, 512), row), pl.BlockSpec((t, 128), row),
                   pl.BlockSpec((512, D), const), pl.BlockSpec((512, D), const), pl.BlockSpec((512, D), const),
                   pl.BlockSpec((8, 512), const)],
        out_shape=[jax.ShapeDtypeStruct((s, 3 * D), BF16), jax.ShapeDtypeStruct((s, 512), BF16),
                   jax.ShapeDtypeStruct((s, 512), BF16), jax.ShapeDtypeStruct((s, 512), BF16),
                   jax.ShapeDtypeStruct((s, 512), BF16), jax.ShapeDtypeStruct((s, 128), F32),
                   jax.ShapeDtypeStruct((512, D), F32), jax.ShapeDtypeStruct((512, D), F32),
                   jax.ShapeDtypeStruct((512, D), F32), jax.ShapeDtypeStruct((8, 512), F32)],
        name="merge_bwd", compiler_params=_cp("arbitrary"))(
            dh1b, pm, pm, pm, pm, o_gla, o_fox, o_mem, g_head, wg, wf, wm, wgt, wft, wmt, wot)


def _ff2_loss(a, w2, h1, g_final, target):
    s, k = a.shape
    tm = min(s, 256)
    tk = 1024
    nk = k // tk

    def body(a_ref, w_ref, h1_ref, g_ref, t_ref, dh_ref, dhb_ref, loss_ref, dg_ref, acc):
        i, kk = pl.program_id(0), pl.program_id(1)

        @pl.when(kk == 0)
        def _():
            acc[...] = jnp.zeros_like(acc)

        @pl.when((i == 0) & (kk == 0))
        def _():
            loss_ref[...] = jnp.zeros_like(loss_ref)
            dg_ref[...] = jnp.zeros_like(dg_ref)

        acc[...] += _dot(_relu2_bf16(a_ref[...]), w_ref[...])

        @pl.when(kk == nk - 1)
        def _():
            h2 = h1_ref[...] + acc[...]
            r = lax.rsqrt(jnp.mean(h2 * h2, axis=-1, keepdims=True) + EPS)
            xn = h2 * r
            g = g_ref[...]
            err = xn * g - t_ref[...]
            e2 = _fold8(err * err)
            part = e2[:, 0:128]
            for c in range(1, D // 128):
                part = part + e2[:, 128 * c:128 * (c + 1)]
            loss_ref[...] += part
            dy = err * (1.0 / D)
            dg_ref[...] += _fold8(dy * xn)
            dxn = dy * g
            dh = r * (dxn - xn * jnp.mean(dxn * xn, axis=-1, keepdims=True))
            dh_ref[...] = dh
            dhb_ref[...] = dh.astype(BF16)

    row = lambda i, kk: (i, 0)
    return pl.pallas_call(
        body, grid=(s // tm, nk),
        in_specs=[pl.BlockSpec((tm, tk), lambda i, kk: (i, kk)), pl.BlockSpec((tk, D), lambda i, kk: (kk, 0)),
                  pl.BlockSpec((tm, D), row), pl.BlockSpec((1, D), lambda i, kk: (0, 0)), pl.BlockSpec((tm, D), row)],
        out_specs=[pl.BlockSpec((tm, D), row), pl.BlockSpec((tm, D), row), pl.BlockSpec((8, 128), lambda i, kk: (0, 0)),
                   pl.BlockSpec((8, D), lambda i, kk: (0, 0))],
        out_shape=[jax.ShapeDtypeStruct((s, D), F32), jax.ShapeDtypeStruct((s, D), BF16),
                   jax.ShapeDtypeStruct((8, 128), F32), jax.ShapeDtypeStruct((8, D), F32)],
        scratch_shapes=[pltpu.VMEM((tm, D), F32)],
        name="ff2_loss", compiler_params=_cp("arbitrary", "arbitrary"))(a, w2, h1, g_final, target)


def _adam(w, g, m, v, name):
    r, c = w.shape
    tr = r
    for cand in (464, 512, 256, 232, 128, 64, 32, 16, 8):
        if r % cand == 0:
            tr = cand
            break
    c1 = 1.0 - ADAM_B1 ** ADAM_STEP
    c2 = 1.0 - ADAM_B2 ** ADAM_STEP

    def body(w_ref, g_ref, m_ref, v_ref, d_ref, nm_ref, nv_ref):
        gv = g_ref[...]
        nm = ADAM_B1 * m_ref[...] + (1.0 - ADAM_B1) * gv
        nv = ADAM_B2 * v_ref[...] + (1.0 - ADAM_B2) * (gv * gv)
        d_ref[...] = -ADAM_LR * ((nm / c1) / (jnp.sqrt(nv / c2) + ADAM_EPS) + ADAM_WD * w_ref[...])
        nm_ref[...] = nm
        nv_ref[...] = nv

    spec = pl.BlockSpec((tr, c), lambda i: (i, 0))
    return pl.pallas_call(
        body, grid=(r // tr,), in_specs=[spec] * 4, out_specs=[spec] * 3,
        out_shape=[jax.ShapeDtypeStruct((r, c), F32)] * 3, name=name, compiler_params=_cp("parallel"))(w, g, m, v)


def _add2(a, b, name):
    n, r, c = a.shape
    tr = 464 if r % 464 == 0 else r

    def body(a_ref, b_ref, o_ref):
        o_ref[...] = a_ref[...] + b_ref[...]

    spec = pl.BlockSpec((1, tr, c), lambda k, i: (k, i, 0))
    return pl.pallas_call(body, grid=(n, r // tr), in_specs=[spec, spec], out_specs=spec,
                          out_shape=jax.ShapeDtypeStruct((n, r, c), F32), name=name,
                          compiler_params=_cp("parallel", "parallel"))(a, b)


def _sum4(a, name):
    _, r, c = a.shape
    tr = 464 if r % 464 == 0 else r

    def body(a_ref, o_ref):
        o_ref[...] = ((a_ref[0] + a_ref[1]) + a_ref[2]) + a_ref[3]

    return pl.pallas_call(body, grid=(r // tr,), in_specs=[pl.BlockSpec((4, tr, c), lambda i: (0, i, 0))],
                          out_specs=pl.BlockSpec((tr, c), lambda i: (i, 0)),
                          out_shape=jax.ShapeDtypeStruct((r, c), F32), name=name, compiler_params=_cp("parallel"))(a)


def _adam_small(w, gathered, m, v):
    c1 = 1.0 - ADAM_B1 ** ADAM_STEP
    c2 = 1.0 - ADAM_B2 ** ADAM_STEP

    def body(w_ref, g_ref, m_ref, v_ref, gs_ref, d_ref, nm_ref, nv_ref):
        gv = g_ref[0]
        for dev in range(1, N_DEV):
            gv = gv + g_ref[dev]
        gs_ref[...] = gv
        nm = ADAM_B1 * m_ref[...] + (1.0 - ADAM_B1) * gv
        nv = ADAM_B2 * v_ref[...] + (1.0 - ADAM_B2) * (gv * gv)
        d_ref[...] = -ADAM_LR * ((nm / c1) / (jnp.sqrt(nv / c2) + ADAM_EPS) + ADAM_WD * w_ref[...])
        nm_ref[...] = nm
        nv_ref[...] = nv

    return pl.pallas_call(body, out_shape=[jax.ShapeDtypeStruct((8, D), F32)] * 4, name="adam_small")(w, gathered, m, v)


def _place():
    return lax.axis_index("x"), lax.axis_index("y"), lax.axis_index("c")


def _other_chips(x, y):
    return [(1 - x, y), (x, 1 - y), (1 - x, 1 - y)]


def _gather_shards(p):
    r, c = p.shape
    hr = r // 2

    def body(p_ref, out_ref, send_sems, recv_sems, local_sem):
        x, y, cc = _place()
        sibling = (x, y, 1 - cc)
        chips = _other_chips(x, y)

        def half(chip, core):
            return out_ref.at[2 * chip[0] + chip[1], pl.ds(core * hr, hr), :]

        def copy(k, chip, core, to, src=None):
            return pltpu.make_async_remote_copy(
                src_ref=half(chip, core) if src is None else src, dst_ref=half(chip, core),
                send_sem=send_sems.at[k], recv_sem=recv_sems.at[k], device_id=to, device_id_type=MESH)

        mine = pltpu.make_async_copy(p_ref, out_ref.at[2 * x + y], local_sem)
        mine.start()
        my_half = p_ref.at[pl.ds(cc * hr, hr), :]
        first = [copy(j, (x, y), cc, (*chip, cc), src=my_half) for j, chip in enumerate(chips)]
        for cp in first:
            cp.start()
        passed = [copy(3 + j, chip, cc, sibling) for j, chip in enumerate(chips)]
        for j, chip in enumerate(chips):
            copy(j, chip, cc, (x, y, cc)).wait_recv()
            passed[j].start()
        for j, chip in enumerate(chips):
            copy(3 + j, chip, 1 - cc, (x, y, cc)).wait_recv()
        for cp in first + passed:
            cp.wait_send()
        mine.wait()

    any_spec = pl.BlockSpec(memory_space=pl.ANY)
    return pl.pallas_call(
        body, out_shape=jax.ShapeDtypeStruct((N_CHIPS, r, c), p.dtype), in_specs=[any_spec], out_specs=any_spec,
        scratch_shapes=[pltpu.SemaphoreType.DMA((6,)), pltpu.SemaphoreType.DMA((6,)), pltpu.SemaphoreType.DMA],
        name="gather_shards")(p)


def _swap_halves(g):
    n, r, c = g.shape
    hr = r // 2

    def body(g_ref, out_ref, send_sem, recv_sem):
        x, y, cc = _place()
        cp = pltpu.make_async_remote_copy(
            src_ref=g_ref.at[:, pl.ds((1 - cc) * hr, hr), :], dst_ref=out_ref,
            send_sem=send_sem, recv_sem=recv_sem, device_id=(x, y, 1 - cc), device_id_type=MESH)
        cp.start()
        cp.wait()

    any_spec = pl.BlockSpec(memory_space=pl.ANY)
    return pl.pallas_call(
        body, out_shape=jax.ShapeDtypeStruct((n, hr, c), g.dtype), in_specs=[any_spec], out_specs=any_spec,
        scratch_shapes=[pltpu.SemaphoreType.DMA, pltpu.SemaphoreType.DMA], name="swap_halves")(g)


def _scatter_partials(p):
    n, hr, c = p.shape

    def body(p_ref, out_ref, send_sems, recv_sems, local_sem):
        x, y, cc = _place()
        me = 2 * x + y
        chips = _other_chips(x, y)
        mine = pltpu.make_async_copy(p_ref.at[me], out_ref.at[me], local_sem)
        mine.start()
        sends = []
        for j, chip in enumerate(chips):
            sends.append(pltpu.make_async_remote_copy(
                src_ref=p_ref.at[2 * chip[0] + chip[1]], dst_ref=out_ref.at[me],
                send_sem=send_sems.at[j], recv_sem=recv_sems.at[j], device_id=(*chip, cc), device_id_type=MESH))
        for cp in sends:
            cp.start()
        for j, chip in enumerate(chips):
            src = 2 * chip[0] + chip[1]
            pltpu.make_async_remote_copy(
                src_ref=p_ref.at[me], dst_ref=out_ref.at[src], send_sem=send_sems.at[j], recv_sem=recv_sems.at[j],
                device_id=(*chip, cc), device_id_type=MESH).wait_recv()
        for cp in sends:
            cp.wait_send()
        mine.wait()

    any_spec = pl.BlockSpec(memory_space=pl.ANY)
    return pl.pallas_call(
        body, out_shape=jax.ShapeDtypeStruct((n, hr, c), p.dtype), in_specs=[any_spec], out_specs=any_spec,
        scratch_shapes=[pltpu.SemaphoreType.DMA((3,)), pltpu.SemaphoreType.DMA((3,)), pltpu.SemaphoreType.DMA],
        name="scatter_partials")(p)


def _join_halves(f):
    hr, c = f.shape

    def body(f_ref, out_ref, send_sem, recv_sem, local_sem):
        x, y, cc = _place()
        mine = pltpu.make_async_copy(f_ref, out_ref.at[pl.ds(cc * hr, hr), :], local_sem)
        mine.start()
        cp = pltpu.make_async_remote_copy(
            src_ref=f_ref, dst_ref=out_ref.at[pl.ds(cc * hr, hr), :],
            send_sem=send_sem, recv_sem=recv_sem, device_id=(x, y, 1 - cc), device_id_type=MESH)
        cp.start()
        pltpu.make_async_remote_copy(
            src_ref=f_ref, dst_ref=out_ref.at[pl.ds((1 - cc) * hr, hr), :],
            send_sem=send_sem, recv_sem=recv_sem, device_id=(x, y, 1 - cc), device_id_type=MESH).wait_recv()
        cp.wait_send()
        mine.wait()

    any_spec = pl.BlockSpec(memory_space=pl.ANY)
    return pl.pallas_call(
        body, out_shape=jax.ShapeDtypeStruct((2 * hr, c), f.dtype), in_specs=[any_spec], out_specs=any_spec,
        scratch_shapes=[pltpu.SemaphoreType.DMA, pltpu.SemaphoreType.DMA, pltpu.SemaphoreType.DMA],
        name="join_halves")(f)


def _gather_small(blk):
    m, n = blk.shape

    def body(x_ref, out_ref, send_sems, recv_sems, local_sem):
        x, y, cc = _place()
        me, sibling = (x, y, cc), (x, y, 1 - cc)
        chips = _other_chips(x, y)

        def slot(px, py, pc):
            return out_ref.at[4 * px + 2 * py + pc]

        def copy(k, block, to, src=None):
            return pltpu.make_async_remote_copy(
                src_ref=slot(*block) if src is None else src, dst_ref=slot(*block),
                send_sem=send_sems.at[k], recv_sem=recv_sems.at[k], device_id=to, device_id_type=MESH)

        mine = pltpu.make_async_copy(x_ref, slot(*me), local_sem)
        mine.start()
        first = [copy(0, me, sibling, src=x_ref)]
        first += [copy(1 + j, me, (*chip, cc), src=x_ref) for j, chip in enumerate(chips)]
        for cp in first:
            cp.start()
        passed = [copy(4 + j, (*chip, cc), sibling) for j, chip in enumerate(chips)]
        for j, chip in enumerate(chips):
            copy(1 + j, (*chip, cc), me).wait_recv()
            passed[j].start()
        copy(0, sibling, me).wait_recv()
        for j, chip in enumerate(chips):
            copy(4 + j, (*chip, 1 - cc), me).wait_recv()
        for cp in first + passed:
            cp.wait_send()
        mine.wait()

    vmem = pl.BlockSpec(memory_space=pltpu.VMEM)
    return pl.pallas_call(
        body, out_shape=jax.ShapeDtypeStruct((N_DEV, m, n), blk.dtype), in_specs=[vmem], out_specs=vmem,
        scratch_shapes=[pltpu.SemaphoreType.DMA((7,)), pltpu.SemaphoreType.DMA((7,)), pltpu.SemaphoreType.DMA],
        name="gather_small")(blk)


def _shard_shape(name, full_shape):
    shp = list(full_shape)
    shp[SHARD_AXIS[name]] //= N_CHIPS
    return tuple(shp)


FULL_SHAPES = {"w_in": (D, 6680), "w_alpha_up": (16, 256), "w_mem_kv": (D, D), "w_gla_o": (512, D), "w_fox_o": (512, D),
               "w_mem_o": (512, D), "w_out": (D, D), "w_ff1": (D, 4 * D), "w_ff2": (4 * D, D)}


def _pack(shards, dtype):
    flat = [shards[n].astype(dtype).reshape(-1) for n in WEIGHTS]
    total = sum(f.shape[0] for f in flat)
    flat.append(jnp.zeros((PACK_ROWS * PACK_W - total,), dtype))
    return jnp.concatenate(flat).reshape(PACK_ROWS, PACK_W)


def _unpack(packed):
    flat = packed.reshape(-1)
    out, off = {}, 0
    for n in WEIGHTS:
        shp = _shard_shape(n, FULL_SHAPES[n])
        size = shp[0] * shp[1]
        out[n] = flat[off:off + size].reshape(shp)
        off += size
    return out


def _split_shards(name, full):
    return jnp.split(full, N_CHIPS, axis=SHARD_AXIS[name])


def _pack_small(vals):
    row4 = jnp.concatenate([vals["b_alpha"].reshape(-1), vals["b_forget"].reshape(-1), jnp.zeros((D - 264,), F32)])
    row5 = jnp.concatenate([vals["g_gla_head"].reshape(-1), jnp.zeros((D - 512,), F32)])
    rows = [vals["g_mix"].reshape(-1), vals["g_mem"].reshape(-1), vals["g_ffn"].reshape(-1), vals["g_final"].reshape(-1),
            row4, row5, jnp.zeros((D,), F32), jnp.zeros((D,), F32)]
    return jnp.stack(rows)


def _unpack_small(blk):
    return {"g_mix": blk[0].reshape(1, D), "g_mem": blk[1].reshape(1, D), "g_ffn": blk[2].reshape(1, D),
            "g_final": blk[3].reshape(D), "b_alpha": blk[4, 0:256].reshape(1, 256), "b_forget": blk[4, 256:264].reshape(1, 8),
            "g_gla_head": blk[5, 0:512].reshape(1, 4, 128)}


def _local_step(x, mem, target, wb, small):
    s = x.shape[0]
    nm = mem.shape[0]
    t = _row_tile(s)
    nb = s // t
    w_in = wb["w_in"]
    w_main = jnp.concatenate([w_in[:, 3608:6680], w_in[:, 0:1536], w_in[:, 1552:3088], w_in[:, 3096:3608]], axis=1)
    w_e = jnp.concatenate([w_in[:, 1536:1552], w_in[:, 3088:3096], jnp.zeros((D, PE_W - 24), BF16)], axis=1)
    w_in_pt = jnp.concatenate([w_main, w_e, jnp.zeros((D, DP_W - PM_W - PE_W), BF16)], axis=1).T
    wau_p = jnp.concatenate([wb["w_alpha_up"], jnp.zeros((PE_W - 16, 256), BF16)], axis=0)
    b_alpha = small["b_alpha"].reshape(1, 256)
    bias_e = jnp.concatenate([jnp.zeros((FF_LANE,), F32), small["b_forget"].reshape(-1),
                              jnp.zeros((PE_W - FF_LANE - 8,), F32)]).reshape(1, PE_W)
    g_mix, g_mem, g_ffn = small["g_mix"].reshape(1, D), small["g_mem"].reshape(1, D), small["g_ffn"].reshape(1, D)
    g_final = small["g_final"].reshape(1, D)
    g_head = small["g_gla_head"].reshape(1, 512)

    u, r1 = _rms_fwd(x, g_mix, "norm_mix")
    pm = _mm_nn(u, w_main, out_dtype=BF16, tm=t, tn=512, tk=D, name="proj_main")
    pe = _mm_nn(u, w_e, out_dtype=F32, tm=t, tn=PE_W, tk=D, name="proj_narrow")
    o_gla, states = _gla_fwd(pm, pe, wau_p, b_alpha)
    fcum = _fcum_fwd(pe, bias_e)
    f_hs = fcum[:, FF_LANE:FF_LANE + 8].T
    f_col = f_hs.reshape(8, s, 1)
    f_rows = f_hs.reshape(8, nb, 1, t)
    qkv = pm[:, C_FQ:C_FQ + 1536].reshape(s, 3, 8, 64).transpose(1, 2, 0, 3)
    fq = (qkv[0].astype(F32) * 0.125).astype(BF16)
    o_fox_h, lse = _fox_fwd(fq, qkv[1], qkv[2], f_col, f_rows)
    o_fox = o_fox_h.transpose(1, 0, 2).reshape(s, 512)
    mn, rm = _rms_fwd(mem, g_mem, "norm_mem")
    mkv = _mm_nn(mn, wb["w_mem_kv"], out_dtype=BF16, tm=nm, tn=512, tk=D, name="mem_kv")
    o_mem = _mem_attn_fwd(pm, mkv)
    merged, h1, u2, r2 = _merge_fwd(x, pm, o_gla, o_fox, o_mem, g_head, wb["w_gla_o"], wb["w_fox_o"], wb["w_mem_o"],
                                    wb["w_out"], g_ffn)
    a = _mm_nn(u2, wb["w_ff1"], out_dtype=BF16, tm=t, tn=1024, tk=D, name="ff1")
    dh2, dh2b, loss8, dgfin8 = _ff2_loss(a, wb["w_ff2"], h1, g_final, target)
    loss = 0.5 * jnp.sum(loss8) / D

    da = _mm_nn(dh2b, wb["w_ff2"].T, out_dtype=BF16, tm=t, tn=1024, tk=D, name="d_act",
                epi=lambda acc, at: acc * (2.0 * jnp.maximum(at.astype(F32), 0.0)), extra=a)
    gw = {}
    gw["w_ff2"] = _mm_tn(a, dh2b, tm=1024, tn=D, ts=t, name="dw_ff2", a_fn=_relu2_bf16)
    gw["w_ff1"] = _mm_tn(u2, da, tm=D, tn=1024, ts=t, name="dw_ff1")
    dh1, dh1b, dgffn8 = _mm_norm_bwd(da, wb["w_ff1"].T, h1, r2, g_ffn, dh2, tk=1024, name="d_h1", want_bf16=True)
    gw["w_out"] = _mm_tn(merged, dh1b, tm=D, tn=D, ts=t, name="dw_out")
    (dgates, dgg, do_gla, do_fox, do_mem, delta, gw["w_gla_o"], gw["w_fox_o"], gw["w_mem_o"], dgh8) = _merge_bwd(
        dh1b, pm, o_gla, o_fox, o_mem, g_head, wb["w_gla_o"], wb["w_fox_o"], wb["w_mem_o"],
        wb["w_gla_o"].T, wb["w_fox_o"].T, wb["w_mem_o"].T, wb["w_out"].T)
    dgq, dgk, dgv, de_gla, dwau_p, dba8 = _gla_bwd(pm, pe, wau_p, wau_p.T, b_alpha, do_gla, states)
    gw["w_alpha_up"] = dwau_p[0:16, :]
    do_fox_h = do_fox.reshape(s, 8, 64).transpose(1, 0, 2)
    lse_rows = lse.reshape(8, nb, 1, t)
    delta_rows = delta[:, 0:8].T.reshape(8, nb, 1, t)
    dfq, dfk, dfv, dfcol, dfrow = _fox_bwd(fq, qkv[1], qkv[2], do_fox_h, f_col, f_rows, lse_rows, delta_rows)
    back = lambda z: z.transpose(1, 0, 2).reshape(s, 512)
    df = jnp.pad((dfrow.reshape(8, s) - dfcol.reshape(8, s)).T, ((0, 0), (FF_LANE, PE_W - FF_LANE - 8)))
    de_fox, dbf8 = _fcum_bwd(pe, bias_e, df)
    dmq, dmk, dmv = _mem_attn_bwd(pm, mkv, do_mem)
    dmkv = jnp.concatenate([dmk, dmv], axis=1).astype(BF16)
    gw["w_mem_kv"] = _mm_tn(mn, dmkv, tm=D, tn=D, ts=nm, name="dw_mem_kv")
    dmn_g = _mm_nn(dmkv, wb["w_mem_kv"].T, out_dtype=F32, tm=nm, tn=D, tk=D, name="d_mem_norm")
    dgmem8 = _gain_grad(dmn_g, mem, rm, "dg_mem")
    dproj = jnp.concatenate(
        [dgates, dgq, dgk, dgv, dgg, back(dfq).astype(BF16), back(dfk), back(dfv), dmq,
         (de_gla + de_fox).astype(BF16), jnp.zeros((s, DP_W - PM_W - PE_W), BF16)], axis=1)
    dwp = _mm_tn(u, dproj, tm=D, tn=1024, ts=t, name="dw_in")
    gw["w_in"] = jnp.concatenate([dwp[:, 3072:4608], dwp[:, PM_W:PM_W + 16], dwp[:, 4608:6144],
                                  dwp[:, PM_W + 16:PM_W + 24], dwp[:, 6144:6656], dwp[:, 0:3072]], axis=1)
    grad_x, dgmix8 = _mm_norm_bwd(dproj, w_in_pt, x, r1, g_mix, dh1, tk=1024, name="d_x", want_bf16=False)
    gs = {"g_mix": dgmix8.sum(0), "g_mem": dgmem8.sum(0), "g_ffn": dgffn8.sum(0), "g_final": dgfin8.sum(0),
          "b_alpha": dba8.sum(0), "b_forget": dbf8.sum(0)[FF_LANE:FF_LANE + 8], "g_gla_head": dgh8.sum(0)}
    return loss, grad_x, gw, gs


def kernel(x, mem, g_mix, w_in, w_alpha_up, b_alpha, b_forget, g_gla_head, g_mem, w_mem_kv, w_gla_o, w_fox_o, w_mem_o, w_out, g_ffn, w_ff1, w_ff2, g_final, loss_target, m_g_mix, m_w_in, m_w_alpha_up, m_b_alpha, m_b_forget, m_g_gla_head, m_g_mem, m_w_mem_kv, m_w_gla_o, m_w_fox_o, m_w_mem_o, m_w_out, m_g_ffn, m_w_ff1, m_w_ff2, m_g_final, v_g_mix, v_w_in, v_w_alpha_up, v_b_alpha, v_b_forget, v_g_gla_head, v_g_mem, v_w_mem_kv, v_w_gla_o, v_w_fox_o, v_w_mem_o, v_w_out, v_g_ffn, v_w_ff1, v_w_ff2, v_g_final):
    args = dict(locals())
    w_sh = {n: args[n][0] for n in WEIGHTS}
    m_sh = {n: args["m_" + n][0] for n in WEIGHTS}
    v_sh = {n: args["v_" + n][0] for n in WEIGHTS}
    small = {n: args[n] for n in SMALL}

    gathered = _gather_shards(_pack(w_sh, BF16))
    parts = [_unpack(gathered[k]) for k in range(N_CHIPS)]
    wb = {n: jnp.concatenate([parts[k][n] for k in range(N_CHIPS)], axis=SHARD_AXIS[n]) for n in WEIGHTS}

    loss, grad_x, gw, gs = _local_step(x[0], mem[0], loss_target[0], wb, small)
    loss = lax.psum(loss, ("x", "y", "c"))

    g_by_chip = [_split_shards(n, gw[n]) for n in WEIGHTS]
    g_packed = jnp.stack([_pack({n: g_by_chip[i][k] for i, n in enumerate(WEIGHTS)}, F32) for k in range(N_CHIPS)])
    hr = PACK_ROWS // 2
    cc = lax.axis_index("c")
    from_sibling = _swap_halves(g_packed)
    my_half = lax.dynamic_slice_in_dim(g_packed, cc * hr, hr, axis=1)
    chip_sum = _add2(my_half, from_sibling, "chip_sum")
    by_chip = _scatter_partials(chip_sum)
    g_shard = _join_halves(_sum4(by_chip, "shard_sum"))

    delta_p, new_m_p, new_v_p = _adam(_pack(w_sh, F32), g_shard, _pack(m_sh, F32), _pack(v_sh, F32), "adam")
    g_out, d_out, m_out, v_out = _unpack(g_shard), _unpack(delta_p), _unpack(new_m_p), _unpack(new_v_p)

    small_all = _gather_small(_pack_small(gs))
    sm = {n: args["m_" + n] for n in SMALL}
    sv = {n: args["v_" + n] for n in SMALL}
    gs_sum, sd, snm, snv = _adam_small(_pack_small(small), small_all, _pack_small(sm), _pack_small(sv))
    gs_o, sd_o, snm_o, snv_o = _unpack_small(gs_sum), _unpack_small(sd), _unpack_small(snm), _unpack_small(snv)

    names = ["g_mix", "w_in", "w_alpha_up", "b_alpha", "b_forget", "g_gla_head", "g_mem", "w_mem_kv", "w_gla_o", "w_fox_o",
             "w_mem_o", "w_out", "g_ffn", "w_ff1", "w_ff2", "g_final"]

    def pick(big, sml, n):
        return big[n][None] if n in big else sml[n]

    outs = [loss, grad_x[None]]
    for big, sml in ((g_out, gs_o), (d_out, sd_o), (m_out, snm_o), (v_out, snv_o)):
        outs += [pick(big, sml, n) for n in names]
    return tuple(outs)
```

```python
import functools

import numpy as np
import jax
import jax.numpy as jnp
from jax import lax
from jax.experimental import pallas as pl
from jax.experimental.pallas import tpu as pltpu

F32 = jnp.float32
BF16 = jnp.bfloat16
HI = lax.Precision.HIGHEST
MESH = pl.DeviceIdType.MESH

EPS = 1e-6
D = 1024
CHUNK = 64
GLA_TAU = 16.0
N_CHIPS = 4
N_DEV = 8
VMEM_LIMIT_BYTES = 56 * 1024 * 1024

ADAM_LR, ADAM_B1, ADAM_B2, ADAM_EPS, ADAM_WD, ADAM_STEP = 0.001, 0.9, 0.999, 1e-08, 0.01, 10

PM_W = 6656
PE_W = 128
DP_W = 7168
C_GQ, C_GK, C_GV, C_GG, C_FQ, C_FK, C_FV, C_MQ = 3072, 3328, 3584, 4096, 4608, 5120, 5632, 6144
FF_LANE = 16

WEIGHTS = ("w_in", "w_alpha_up", "w_mem_kv", "w_gla_o", "w_fox_o", "w_mem_o", "w_out", "w_ff1", "w_ff2")
SHARD_AXIS = {"w_in": 1, "w_alpha_up": 1, "w_mem_kv": 0, "w_gla_o": 1, "w_fox_o": 1, "w_mem_o": 1, "w_out": 0,
              "w_ff1": 1, "w_ff2": 0}
SMALL = ("g_mix", "g_mem", "g_ffn", "g_final", "b_alpha", "b_forget", "g_gla_head")
PACK_W = 1024
PACK_ROWS = 4640


def _cp(*sem):
    return pltpu.CompilerParams(dimension_semantics=sem, vmem_limit_bytes=VMEM_LIMIT_BYTES)


def _dot(a, b, **kw):
    return jnp.dot(a, b, preferred_element_type=F32, **kw)


def _dot_nt(a, b, **kw):
    return lax.dot_general(a, b, (((1,), (1,)), ((), ())), preferred_element_type=F32, **kw)


def _dot_tn(a, b, **kw):
    return lax.dot_general(a, b, (((0,), (0,)), ((), ())), preferred_element_type=F32, **kw)


def _sigmoid(x):
    return 1.0 / (1.0 + jnp.exp(-x))


def _log_sigmoid(x):
    return -(jnp.maximum(-x, 0.0) + jnp.log1p(jnp.exp(-jnp.abs(x))))


def _fold8(x):
    m, n = x.shape
    return x.reshape(m // 8, 8, n).sum(axis=0)


def _iota(shape, dim):
    return lax.broadcasted_iota(jnp.int32, shape, dim)


def _row_tile(s):
    return min(s, 512)


def _mm_nn(a, b, *, out_dtype, tm, tn, tk, name, a_fn=None, epi=None, extra=None):
    m, k = a.shape
    _, n = b.shape
    nk = k // tk

    def body(*refs):
        if extra is None:
            a_ref, b_ref, o_ref, acc = refs
            x_ref = None
        else:
            a_ref, b_ref, x_ref, o_ref, acc = refs
        kk = pl.program_id(2)

        @pl.when(kk == 0)
        def _():
            acc[...] = jnp.zeros_like(acc)

        at = a_ref[...]
        if a_fn is not None:
            at = a_fn(at)
        acc[...] += _dot(at, b_ref[...])

        @pl.when(kk == nk - 1)
        def _():
            r = acc[...]
            if epi is not None:
                r = epi(r, None if x_ref is None else x_ref[...])
            o_ref[...] = r.astype(out_dtype)

    in_specs = [pl.BlockSpec((tm, tk), lambda i, j, kk: (i, kk)), pl.BlockSpec((tk, tn), lambda i, j, kk: (kk, j))]
    args = [a, b]
    if extra is not None:
        in_specs.append(pl.BlockSpec((tm, tn), lambda i, j, kk: (i, j)))
        args.append(extra)
    return pl.pallas_call(
        body, grid=(m // tm, n // tn, nk), in_specs=in_specs,
        out_specs=pl.BlockSpec((tm, tn), lambda i, j, kk: (i, j)),
        out_shape=jax.ShapeDtypeStruct((m, n), out_dtype),
        scratch_shapes=[pltpu.VMEM((tm, tn), F32)], name=name,
        compiler_params=_cp("parallel", "parallel", "arbitrary"))(*args)


def _mm_tn(a, b, *, tm, tn, ts, name, a_fn=None):
    s, m = a.shape
    _, n = b.shape
    ns = s // ts

    def body(a_ref, b_ref, o_ref, acc):
        kk = pl.program_id(2)

        @pl.when(kk == 0)
        def _():
            acc[...] = jnp.zeros_like(acc)

        at = a_ref[...]
        if a_fn is not None:
            at = a_fn(at)
        acc[...] += _dot_tn(at, b_ref[...])

        @pl.when(kk == ns - 1)
        def _():
            o_ref[...] = acc[...]

    return pl.pallas_call(
        body, grid=(m // tm, n // tn, ns),
        in_specs=[pl.BlockSpec((ts, tm), lambda i, j, kk: (kk, i)), pl.BlockSpec((ts, tn), lambda i, j, kk: (kk, j))],
        out_specs=pl.BlockSpec((tm, tn), lambda i, j, kk: (i, j)),
        out_shape=jax.ShapeDtypeStruct((m, n), F32),
        scratch_shapes=[pltpu.VMEM((tm, tn), F32)], name=name,
        compiler_params=_cp("parallel", "parallel", "arbitrary"))(a, b)


def _relu2_bf16(t):
    r = jnp.maximum(t.astype(F32), 0.0)
    return (r * r).astype(BF16)


def _rms_fwd(x, g, name):
    s, d = x.shape
    tm = min(s, 512)

    def body(x_ref, g_ref, u_ref, r_ref):
        xv = x_ref[...]
        r = lax.rsqrt(jnp.mean(xv * xv, axis=-1, keepdims=True) + EPS)
        u_ref[...] = ((xv * r) * g_ref[...]).astype(BF16)
        r_ref[...] = r

    return pl.pallas_call(
        body, grid=(s // tm,),
        in_specs=[pl.BlockSpec((tm, d), lambda i: (i, 0)), pl.BlockSpec((1, d), lambda i: (0, 0))],
        out_specs=[pl.BlockSpec((tm, d), lambda i: (i, 0)), pl.BlockSpec((tm, 1), lambda i: (i, 0))],
        out_shape=[jax.ShapeDtypeStruct((s, d), BF16), jax.ShapeDtypeStruct((s, 1), F32)],
        name=name, compiler_params=_cp("parallel"))(x, g)


def _mm_norm_bwd(a, b, xin, r, g, dres, *, tk, name, want_bf16):
    s, k = a.shape
    tm = min(s, 256)
    nk = k // tk

    def body(a_ref, b_ref, x_ref, r_ref, g_ref, dres_ref, *rest):
        if want_bf16:
            dx_ref, dxb_ref, dg_ref, acc = rest
        else:
            dx_ref, dg_ref, acc = rest
        i, kk = pl.program_id(0), pl.program_id(1)

        @pl.when(kk == 0)
        def _():
            acc[...] = jnp.zeros_like(acc)

        @pl.when((i == 0) & (kk == 0))
        def _():
            dg_ref[...] = jnp.zeros_like(dg_ref)

        acc[...] += _dot(a_ref[...], b_ref[...])

        @pl.when(kk == nk - 1)
        def _():
            du = acc[...]
            xn = x_ref[...] * r_ref[...]
            dg_ref[...] += _fold8(du * xn)
            dxn = du * g_ref[...]
            dx = dres_ref[...] + r_ref[...] * (dxn - xn * jnp.mean(dxn * xn, axis=-1, keepdims=True))
            dx_ref[...] = dx
            if want_bf16:
                dxb_ref[...] = dx.astype(BF16)

    row = lambda i, kk: (i, 0)
    out_specs = [pl.BlockSpec((tm, D), row)]
    out_shape = [jax.ShapeDtypeStruct((s, D), F32)]
    if want_bf16:
        out_specs.append(pl.BlockSpec((tm, D), row))
        out_shape.append(jax.ShapeDtypeStruct((s, D), BF16))
    out_specs.append(pl.BlockSpec((8, D), lambda i, kk: (0, 0)))
    out_shape.append(jax.ShapeDtypeStruct((8, D), F32))
    return pl.pallas_call(
        body, grid=(s // tm, nk),
        in_specs=[pl.BlockSpec((tm, tk), lambda i, kk: (i, kk)), pl.BlockSpec((tk, D), lambda i, kk: (kk, 0)),
                  pl.BlockSpec((tm, D), row), pl.BlockSpec((tm, 1), row), pl.BlockSpec((1, D), lambda i, kk: (0, 0)),
                  pl.BlockSpec((tm, D), row)],
        out_specs=out_specs, out_shape=out_shape, scratch_shapes=[pltpu.VMEM((tm, D), F32)],
        name=name, compiler_params=_cp("arbitrary", "arbitrary"))(a, b, xin, r, g, dres)


def _gla_consts():
    lmask = _iota((4 * CHUNK, CHUNK), 0) % CHUNK >= _iota((4 * CHUNK, CHUNK), 1)
    hmask = _iota((256, 256), 0) // CHUNK == _iota((256, 256), 1) // CHUNK
    bd = _iota((256, 512), 0) // CHUNK == _iota((256, 512), 1) // 128
    return lmask, hmask, bd


def _fold_heads(x):
    return x[0:64] + x[64:128] + x[128:192] + x[192:256]


def _gla_chunk(lac, qc, kc):
    tri = (_iota((CHUNK, CHUNK), 0) >= _iota((CHUNK, CHUNK), 1)).astype(F32)
    b = _dot(tri, lac, precision=HI)
    bl = b[CHUNK - 1:CHUNK, :]
    ep, en, ek = jnp.exp(b), jnp.exp(-b), jnp.exp(bl - b)
    decb = jnp.exp(_dot_tn(lac, jnp.ones((CHUNK, 128), F32), precision=HI))
    decb = jnp.concatenate([decb] * 4, axis=1)
    return bl, ep, en, ek, decb, qc * ep, qc * en, kc * en, kc * ep, kc * ek


def _gla_fwd(pm, pe, wau_p, b_alpha):
    s = pm.shape[0]
    t = _row_tile(s)
    nc = t // CHUNK

    def body(q_ref, k_ref, v_ref, e_ref, wau_ref, ba_ref, o_ref, st_ref, state, la_scr):
        @pl.when(pl.program_id(0) == 0)
        def _():
            state[...] = jnp.zeros_like(state)

        z = _dot(e_ref[...].astype(BF16), wau_ref[...]) + ba_ref[...]
        la_scr[...] = _log_sigmoid(z) * (1.0 / GLA_TAU)
        lmask, hmask, bd = _gla_consts()

        def chunk(c, carry):
            rows = pl.ds(pl.multiple_of(c * CHUNK, CHUNK), CHUNK)
            qc = q_ref[rows, :].astype(F32) * 0.125
            kc = k_ref[rows, :].astype(F32)
            vc = v_ref[rows, :]
            _, _, _, _, decb, qp, qn, kn, kp, kk = _gla_chunk(la_scr[rows, :], qc, kc)
            qs = jnp.where(hmask, jnp.concatenate([qp] * 4, axis=0), 0.0).astype(BF16)
            qns = jnp.where(hmask, jnp.concatenate([qn] * 4, axis=0), 0.0).astype(BF16)
            attn = jnp.where(lmask, _dot_nt(qs, kn.astype(BF16)), _dot_nt(qns, kp.astype(BF16))).astype(BF16)
            st = state[...]
            o_intra = _fold_heads(jnp.where(bd, _dot(attn, vc), 0.0))
            o_ref[rows, :] = o_intra + _dot(qp.astype(BF16), st.astype(BF16))
            for h in range(4):
                st_ref[c, :, 128 * h:128 * (h + 1)] = st[64 * h:64 * (h + 1), 128 * h:128 * (h + 1)]
            kv = jnp.where(bd, _dot_tn(kk.astype(BF16), vc), 0.0)
            state[...] = st * decb + kv
            return carry

        lax.fori_loop(0, nc, chunk, 0)

    return pl.pallas_call(
        body, grid=(s // t,),
        in_specs=[pl.BlockSpec((t, 256), lambda i: (i, C_GQ // 256)), pl.BlockSpec((t, 256), lambda i: (i, C_GK // 256)),
                  pl.BlockSpec((t, 512), lambda i: (i, C_GV // 512)), pl.BlockSpec((t, PE_W), lambda i: (i, 0)),
                  pl.BlockSpec((PE_W, 256), lambda i: (0, 0)), pl.BlockSpec((1, 256), lambda i: (0, 0))],
        out_specs=[pl.BlockSpec((t, 512), lambda i: (i, 0)), pl.BlockSpec((nc, CHUNK, 512), lambda i: (i, 0, 0))],
        out_shape=[jax.ShapeDtypeStruct((s, 512), F32), jax.ShapeDtypeStruct((s // CHUNK, CHUNK, 512), F32)],
        scratch_shapes=[pltpu.VMEM((256, 512), F32), pltpu.VMEM((t, 256), F32)],
        name="gla_fwd", compiler_params=_cp("arbitrary"))(pm, pm, pm, pe, wau_p, b_alpha)


def _gla_bwd(pm, pe, wau_p, wau_pt, b_alpha, do, states):
    s = pm.shape[0]
    t = _row_tile(s)
    nc = t // CHUNK
    nb = s // t

    def body(q_ref, k_ref, v_ref, e_ref, wau_ref, waut_ref, ba_ref, do_ref, st_ref,
             dq_ref, dk_ref, dv_ref, de_ref, dwau_ref, dba_ref, gstate, la_scr, dla_scr):
        @pl.when(pl.program_id(0) == 0)
        def _():
            gstate[...] = jnp.zeros_like(gstate)
            dwau_ref[...] = jnp.zeros_like(dwau_ref)
            dba_ref[...] = jnp.zeros_like(dba_ref)

        eb = e_ref[...].astype(BF16)
        z = _dot(eb, wau_ref[...]) + ba_ref[...]
        la_scr[...] = _log_sigmoid(z) * (1.0 / GLA_TAU)
        lmask, hmask, bd = _gla_consts()
        triu = (_iota((CHUNK, CHUNK), 0) <= _iota((CHUNK, CHUNK), 1)).astype(F32)
        last_row = _iota((CHUNK, 256), 0) == CHUNK - 1

        def chunk(cc, carry):
            c = nc - 1 - cc
            rows = pl.ds(pl.multiple_of(c * CHUNK, CHUNK), CHUNK)
            qc = q_ref[rows, :].astype(F32) * 0.125
            kc = k_ref[rows, :].astype(F32)
            vc = v_ref[rows, :]
            dob = do_ref[rows, :]
            bl, ep, en, ek, decb, qp, qn, kn, kp, kk = _gla_chunk(la_scr[rows, :], qc, kc)
            qs = jnp.where(hmask, jnp.concatenate([qp] * 4, axis=0), 0.0).astype(BF16)
            qns = jnp.where(hmask, jnp.concatenate([qn] * 4, axis=0), 0.0).astype(BF16)
            knb, kpb = kn.astype(BF16), kp.astype(BF16)
            attn = jnp.where(lmask, _dot_nt(qs, knb), _dot_nt(qns, kpb)).astype(BF16)
            st = jnp.where(bd, jnp.concatenate([st_ref[c]] * 4, axis=0), 0.0)
            g = gstate[...]
            gb = g.astype(BF16)
            do_s = jnp.where(bd, jnp.concatenate([dob] * 4, axis=0), jnp.zeros((), BF16))
            dattn = _dot_nt(do_s, vc)
            dv_ref[rows, :] = (_dot_tn(attn, do_s) + _dot(kk.astype(BF16), gb)).astype(BF16)
            dac = jnp.where(lmask, dattn, 0.0).astype(BF16)
            daa = jnp.where(lmask, 0.0, dattn).astype(BF16)
            dqp = _fold_heads(jnp.where(hmask, _dot(dac, knb), 0.0)) + _dot_nt(dob, st.astype(BF16))
            dqn = _fold_heads(jnp.where(hmask, _dot(daa, kpb), 0.0))
            dkn = _dot_tn(dac, qs)
            dkp = _dot_tn(daa, qns)
            dkk = _dot_nt(vc, gb)
            ddec = _dot_nt(jnp.ones((8, 512), F32), g * st, precision=HI)[0:1, :]
            gstate[...] = decb * g + jnp.where(bd, _dot_tn(qp.astype(BF16), dob), 0.0)
            dq_ref[rows, :] = ((dqp * ep + dqn * en) * 0.125).astype(BF16)
            dk_ref[rows, :] = (dkn * en + dkp * ep + dkk * ek).astype(BF16)
            dek = dkk * kc * ek
            db = (dqp * qc + dkp * kc) * ep - (dqn * qc + dkn * kc) * en - dek
            dbl = jnp.sum(dek, axis=0, keepdims=True) + ddec * jnp.exp(bl)
            db = db + jnp.where(last_row, dbl, 0.0)
            dla_scr[rows, :] = _dot(triu, db, precision=HI)
            return carry

        lax.fori_loop(0, nc, chunk, 0)
        dz = dla_scr[...] * (1.0 / GLA_TAU) * _sigmoid(-z)
        dzb = dz.astype(BF16)
        dwau_ref[...] += _dot_tn(eb, dzb)
        dba_ref[...] += _fold8(dz)
        de_ref[...] = _dot(dzb, waut_ref[...])

    rev = lambda i: nb - 1 - i
    return pl.pallas_call(
        body, grid=(nb,),
        in_specs=[pl.BlockSpec((t, 256), lambda i: (rev(i), C_GQ // 256)), pl.BlockSpec((t, 256), lambda i: (rev(i), C_GK // 256)),
                  pl.BlockSpec((t, 512), lambda i: (rev(i), C_GV // 512)), pl.BlockSpec((t, PE_W), lambda i: (rev(i), 0)),
                  pl.BlockSpec((PE_W, 256), lambda i: (0, 0)), pl.BlockSpec((256, PE_W), lambda i: (0, 0)),
                  pl.BlockSpec((1, 256), lambda i: (0, 0)), pl.BlockSpec((t, 512), lambda i: (rev(i), 0)),
                  pl.BlockSpec((nc, CHUNK, 512), lambda i: (rev(i), 0, 0))],
        out_specs=[pl.BlockSpec((t, 256), lambda i: (rev(i), 0)), pl.BlockSpec((t, 256), lambda i: (rev(i), 0)),
                   pl.BlockSpec((t, 512), lambda i: (rev(i), 0)), pl.BlockSpec((t, PE_W), lambda i: (rev(i), 0)),
                   pl.BlockSpec((PE_W, 256), lambda i: (0, 0)), pl.BlockSpec((8, 256), lambda i: (0, 0))],
        out_shape=[jax.ShapeDtypeStruct((s, 256), BF16), jax.ShapeDtypeStruct((s, 256), BF16),
                   jax.ShapeDtypeStruct((s, 512), BF16), jax.ShapeDtypeStruct((s, PE_W), F32),
                   jax.ShapeDtypeStruct((PE_W, 256), F32), jax.ShapeDtypeStruct((8, 256), F32)],
        scratch_shapes=[pltpu.VMEM((256, 512), F32), pltpu.VMEM((t, 256), F32), pltpu.VMEM((t, 256), F32)],
        name="gla_bwd", compiler_params=_cp("arbitrary"))(pm, pm, pm, pe, wau_p, wau_pt, b_alpha, do, states)


def _fcum_fwd(pe, bias):
    s = pe.shape[0]
    t = min(s, 256)

    def body(e_ref, b_ref, f_ref, carry):
        @pl.when(pl.program_id(0) == 0)
        def _():
            carry[...] = jnp.zeros_like(carry)

        lf = _log_sigmoid(e_ref[...] + b_ref[...])
        tri = (_iota((t, t), 0) >= _iota((t, t), 1)).astype(F32)
        f = _dot(tri, lf, precision=HI) + carry[0:1, :]
        f_ref[...] = f
        carry[...] = jnp.broadcast_to(f[t - 1:t, :], carry.shape)

    return pl.pallas_call(
        body, grid=(s // t,),
        in_specs=[pl.BlockSpec((t, PE_W), lambda i: (i, 0)), pl.BlockSpec((1, PE_W), lambda i: (0, 0))],
        out_specs=pl.BlockSpec((t, PE_W), lambda i: (i, 0)),
        out_shape=jax.ShapeDtypeStruct((s, PE_W), F32), scratch_shapes=[pltpu.VMEM((8, PE_W), F32)],
        name="fcum_fwd", compiler_params=_cp("arbitrary"))(pe, bias)


def _fcum_bwd(pe, bias, df):
    s = pe.shape[0]
    t = min(s, 256)
    nb = s // t

    def body(e_ref, b_ref, df_ref, de_ref, db_ref, carry):
        @pl.when(pl.program_id(0) == 0)
        def _():
            carry[...] = jnp.zeros_like(carry)
            db_ref[...] = jnp.zeros_like(db_ref)

        triu = (_iota((t, t), 0) <= _iota((t, t), 1)).astype(F32)
        dlf = _dot(triu, df_ref[...], precision=HI) + carry[0:1, :]
        carry[...] = jnp.broadcast_to(dlf[0:1, :], carry.shape)
        lane = _iota((t, PE_W), 1)
        dff = jnp.where((lane >= FF_LANE) & (lane < FF_LANE + 8), dlf * _sigmoid(-(e_ref[...] + b_ref[...])), 0.0)
        de_ref[...] = dff
        db_ref[...] += _fold8(dff)

    rev = lambda i: (nb - 1 - i, 0)
    return pl.pallas_call(
        body, grid=(nb,),
        in_specs=[pl.BlockSpec((t, PE_W), rev), pl.BlockSpec((1, PE_W), lambda i: (0, 0)), pl.BlockSpec((t, PE_W), rev)],
        out_specs=[pl.BlockSpec((t, PE_W), rev), pl.BlockSpec((8, PE_W), lambda i: (0, 0))],
        out_shape=[jax.ShapeDtypeStruct((s, PE_W), F32), jax.ShapeDtypeStruct((8, PE_W), F32)],
        scratch_shapes=[pltpu.VMEM((8, PE_W), F32)],
        name="fcum_bwd", compiler_params=_cp("arbitrary"))(pe, bias, df)


FOX_KA = 80
FOX_WIDE = 1024


def _split3(x):
    hi = lax.reduce_precision(x, 8, 7)
    mid = lax.reduce_precision(x - hi, 8, 7)
    lo = lax.reduce_precision(x - hi - mid, 8, 7)
    return hi.astype(BF16), mid.astype(BF16), lo.astype(BF16)


def _fox_fwd(k_aug, q_aug, vt):
    nh, s, ka = k_aug.shape
    tk = _row_tile(s)
    tq = min(s, FOX_WIDE)
    per = tq // tk

    def body(k_ref, q_ref, v_ref, o_ref, lse_ref, sbuf):
        i = pl.program_id(1)
        qa = q_ref[0]

        def scores(j):
            return _dot_nt(k_ref[0, pl.ds(pl.multiple_of(j * tk, tk), tk), :], qa)

        def update(st, j, carry):
            m, l, acc = carry
            m2 = jnp.maximum(m, jnp.max(st, axis=0, keepdims=True))
            p = jnp.exp(st - m2)
            a = jnp.exp(m - m2)
            return m2, a * l + jnp.sum(p, axis=0, keepdims=True), a * acc + _dot(v_ref[0, j], p.astype(BF16))

        def step(a, carry):
            sbuf[1] = scores(2 * a + 1)
            carry = update(sbuf[0], 2 * a, carry)
            sbuf[0] = scores(2 * a + 2)
            return update(sbuf[1], 2 * a + 1, carry)

        n = i * per
        sbuf[0] = scores(0)
        carry = (jnp.full((1, tq), -1e30, F32), jnp.zeros((1, tq), F32), jnp.zeros((64, tq), F32))
        carry = lax.fori_loop(0, n // 2, step, carry)
        for r in range(per):
            j = n + r
            st = sbuf[0] if r == 0 else scores(j)
            st = jnp.where(j * tk + _iota((tk, tq), 0) <= i * tq + _iota((tk, tq), 1), st, -1e30)
            carry = update(st, j, carry)
        m, l, acc = carry
        o_ref[0] = (acc / l).astype(BF16)
        lse_ref[0] = m + jnp.log(l)

    return pl.pallas_call(
        body, grid=(nh, s // tq),
        in_specs=[pl.BlockSpec((1, s, ka), lambda h, i: (h, 0, 0)), pl.BlockSpec((1, tq, ka), lambda h, i: (h, i, 0)),
                  pl.BlockSpec((1, s // tk, 64, tk), lambda h, i: (h, 0, 0, 0))],
        out_specs=[pl.BlockSpec((1, 64, tq), lambda h, i: (h, 0, i)), pl.BlockSpec((1, 1, tq), lambda h, i: (h, 0, i))],
        out_shape=[jax.ShapeDtypeStruct((nh, 64, s), BF16), jax.ShapeDtypeStruct((nh, 1, s), F32)],
        scratch_shapes=[pltpu.VMEM((2, tk, tq), F32)],
        name="fox_fwd", compiler_params=_cp("parallel", "arbitrary"))(k_aug, q_aug, vt)


def _fox_bwd(q_aug, do_aug, qt, dot_, k_aug, v_aug, ks_aug):
    nh, s, ka = q_aug.shape
    tq = _row_tile(s)
    tk = min(s, FOX_WIDE)
    per = tk // tq
    nqb = s // tq

    def body(qa_ref, da_ref, qt_ref, dt_ref, kt_ref, vt_ref, ks_ref, dq_ref, dk_ref, dv_ref, dfk_ref):
        j = pl.program_id(1)

        @pl.when(j == 0)
        def _():
            dq_ref[...] = jnp.zeros_like(dq_ref)

        kta, vta, ksa = kt_ref[0], vt_ref[0], ks_ref[0]
        lane64 = _iota((tq, 128), 1) == 64

        def tile(i, masked, carry):
            dk, dv, dfk = carry
            rows = pl.ds(pl.multiple_of(i * tq, tq), tq)
            sp = _dot_nt(qa_ref[0, rows, :], kta)
            if masked:
                sp = jnp.where(i * tq + _iota((tq, tk), 0) >= j * tk + _iota((tq, tk), 1), sp, -1e30)
            p = jnp.exp(sp)
            ds = p * _dot_nt(da_ref[0, rows, :], vta)
            dsb = ds.astype(BF16)
            dq_ref[0, rows, :] += _dot(dsb, ksa) + jnp.where(lane64, jnp.sum(ds, axis=1, keepdims=True), 0.0)
            return (dk + _dot(qt_ref[0, i], dsb), dv + _dot(dt_ref[0, i], p.astype(BF16)),
                    dfk + jnp.sum(ds, axis=0, keepdims=True))

        carry = (jnp.zeros((64, tk), F32), jnp.zeros((64, tk), F32), jnp.zeros((1, tk), F32))
        for r in range(per):
            carry = tile(j * per + r, True, carry)
        dk, dv, dfk = lax.fori_loop((j + 1) * per, nqb, lambda i, c: tile(i, False, c), carry)
        dk_ref[0] = dk.astype(BF16)
        dv_ref[0] = dv.astype(BF16)
        dfk_ref[0] = dfk

    whole = lambda h, j: (h, 0, 0)
    whole4 = lambda h, j: (h, 0, 0, 0)
    col = lambda h, j: (h, 0, j)
    t = tk
    once = dict(pipeline_mode=pl.Buffered(1))
    return pl.pallas_call(
        body, grid=(nh, s // tk),
        in_specs=[pl.BlockSpec((1, s, ka), whole, **once), pl.BlockSpec((1, s, ka), whole, **once),
                  pl.BlockSpec((1, nqb, 64, tq), whole4, **once), pl.BlockSpec((1, nqb, 64, tq), whole4, **once),
                  pl.BlockSpec((1, t, ka), lambda h, j: (h, j, 0)), pl.BlockSpec((1, t, ka), lambda h, j: (h, j, 0)),
                  pl.BlockSpec((1, t, 128), lambda h, j: (h, j, 0))],
        out_specs=[pl.BlockSpec((1, s, 128), whole), pl.BlockSpec((1, 64, t), col), pl.BlockSpec((1, 64, t), col),
                   pl.BlockSpec((1, 1, t), col)],
        out_shape=[jax.ShapeDtypeStruct((nh, s, 128), F32), jax.ShapeDtypeStruct((nh, 64, s), BF16),
                   jax.ShapeDtypeStruct((nh, 64, s), BF16), jax.ShapeDtypeStruct((nh, 1, s), F32)],
        name="fox_bwd", compiler_params=_cp("parallel", "arbitrary"))(q_aug, do_aug, qt, dot_, k_aug, v_aug, ks_aug)


MEM_SCALE = 128 ** -0.5


def _mem_attn_fwd(pm, mkv):
    s = pm.shape[0]
    t = _row_tile(s)
    nm = mkv.shape[0]

    def body(q_ref, mk_ref, mv_ref, o_ref):
        for h in range(4):
            cols = slice(128 * h, 128 * (h + 1))
            sc = _dot_nt(q_ref[:, cols], mk_ref[:, cols]) * MEM_SCALE
            p = jnp.exp(sc - jnp.max(sc, axis=-1, keepdims=True))
            p = p / jnp.sum(p, axis=-1, keepdims=True)
            o_ref[:, cols] = _dot(p.astype(BF16), mv_ref[:, cols]).astype(BF16)

    return pl.pallas_call(
        body, grid=(s // t,),
        in_specs=[pl.BlockSpec((t, 512), lambda i: (i, C_MQ // 512)), pl.BlockSpec((nm, 512), lambda i: (0, 0)),
                  pl.BlockSpec((nm, 512), lambda i: (0, 1))],
        out_specs=pl.BlockSpec((t, 512), lambda i: (i, 0)),
        out_shape=jax.ShapeDtypeStruct((s, 512), BF16),
        name="mem_attn_fwd", compiler_params=_cp("parallel"))(pm, mkv, mkv)


def _mem_attn_bwd(pm, mkv, do):
    s = pm.shape[0]
    t = _row_tile(s)
    nm = mkv.shape[0]

    def body(q_ref, mk_ref, mv_ref, do_ref, dq_ref, dmk_ref, dmv_ref):
        @pl.when(pl.program_id(0) == 0)
        def _():
            dmk_ref[...] = jnp.zeros_like(dmk_ref)
            dmv_ref[...] = jnp.zeros_like(dmv_ref)

        for h in range(4):
            cols = slice(128 * h, 128 * (h + 1))
            qh, kh, vh, doh = q_ref[:, cols], mk_ref[:, cols], mv_ref[:, cols], do_ref[:, cols]
            sc = _dot_nt(qh, kh) * MEM_SCALE
            p = jnp.exp(sc - jnp.max(sc, axis=-1, keepdims=True))
            p = p / jnp.sum(p, axis=-1, keepdims=True)
            pb = p.astype(BF16)
            dp = _dot_nt(doh, vh)
            ds = (p * (dp - jnp.sum(p * dp, axis=-1, keepdims=True)) * MEM_SCALE).astype(BF16)
            dq_ref[:, cols] = _dot(ds, kh).astype(BF16)
            dmk_ref[:, cols] += _dot_tn(ds, qh)
            dmv_ref[:, cols] += _dot_tn(pb, doh)

    return pl.pallas_call(
        body, grid=(s // t,),
        in_specs=[pl.BlockSpec((t, 512), lambda i: (i, C_MQ // 512)), pl.BlockSpec((nm, 512), lambda i: (0, 0)),
                  pl.BlockSpec((nm, 512), lambda i: (0, 1)), pl.BlockSpec((t, 512), lambda i: (i, 0))],
        out_specs=[pl.BlockSpec((t, 512), lambda i: (i, 0)), pl.BlockSpec((nm, 512), lambda i: (0, 0)),
                   pl.BlockSpec((nm, 512), lambda i: (0, 0))],
        out_shape=[jax.ShapeDtypeStruct((s, 512), BF16), jax.ShapeDtypeStruct((nm, 512), F32),
                   jax.ShapeDtypeStruct((nm, 512), F32)],
        name="mem_attn_bwd", compiler_params=_cp("arbitrary"))(pm, mkv, mkv, do)


def _gain_grad(dxn_g, x, r, name):
    m, d = x.shape

    def body(d_ref, x_ref, r_ref, o_ref):
        o_ref[...] = _fold8(d_ref[...] * (x_ref[...] * r_ref[...]))

    return pl.pallas_call(body, out_shape=jax.ShapeDtypeStruct((8, d), F32), name=name,
                          compiler_params=pltpu.CompilerParams(vmem_limit_bytes=VMEM_LIMIT_BYTES))(dxn_g, x, r)


def _head_norm(o, gh):
    xs, rs = [], []
    for h in range(4):
        oh = o[:, 128 * h:128 * (h + 1)]
        r = lax.rsqrt(jnp.mean(oh * oh, axis=-1, keepdims=True) + EPS)
        xs.append(oh * r)
        rs.append(r)
    return xs, rs


def _merge_fwd(x, pm, o_gla, o_fox, o_mem, g_head, wg, wf, wm, wo, g_ffn):
    s = x.shape[0]
    t = min(s, 256)

    def body(x_ref, g0_ref, g1_ref, g2_ref, gg_ref, og_ref, of_ref, om_ref, gh_ref, wg_ref, wf_ref, wm_ref, wo_ref, gf_ref,
             mg_ref, h1_ref, u2_ref, r2_ref):
        xs, _ = _head_norm(og_ref[...], None)
        gg = gg_ref[...].astype(F32)
        sil = gg * _sigmoid(gg)
        ogn = jnp.concatenate(xs, axis=1) * gh_ref[...] * sil
        merged = (_sigmoid(g0_ref[...].astype(F32)) * _dot(ogn.astype(BF16), wg_ref[...])
                  + _sigmoid(g1_ref[...].astype(F32)) * _dot(of_ref[...], wf_ref[...])
                  + _sigmoid(g2_ref[...].astype(F32)) * _dot(om_ref[...], wm_ref[...]))
        mb = merged.astype(BF16)
        mg_ref[...] = mb
        h1 = x_ref[...] + _dot(mb, wo_ref[...])
        h1_ref[...] = h1
        r = lax.rsqrt(jnp.mean(h1 * h1, axis=-1, keepdims=True) + EPS)
        u2_ref[...] = ((h1 * r) * gf_ref[...]).astype(BF16)
        r2_ref[...] = r

    row = lambda i: (i, 0)
    const = lambda i: (0, 0)
    return pl.pallas_call(
        body, grid=(s // t,),
        in_specs=[pl.BlockSpec((t, D), row), pl.BlockSpec((t, D), lambda i: (i, 0)), pl.BlockSpec((t, D), lambda i: (i, 1)),
                  pl.BlockSpec((t, D), lambda i: (i, 2)), pl.BlockSpec((t, 512), lambda i: (i, C_GG // 512)),
                  pl.BlockSpec((t, 512), row), pl.BlockSpec((t, 512), row), pl.BlockSpec((t, 512), row),
                  pl.BlockSpec((1, 512), const), pl.BlockSpec((512, D), const), pl.BlockSpec((512, D), const),
                  pl.BlockSpec((512, D), const), pl.BlockSpec((D, D), const), pl.BlockSpec((1, D), const)],
        out_specs=[pl.BlockSpec((t, D), row), pl.BlockSpec((t, D), row), pl.BlockSpec((t, D), row), pl.BlockSpec((t, 1), row)],
        out_shape=[jax.ShapeDtypeStruct((s, D), BF16), jax.ShapeDtypeStruct((s, D), F32),
                   jax.ShapeDtypeStruct((s, D), BF16), jax.ShapeDtypeStruct((s, 1), F32)],
        name="merge_fwd", compiler_params=_cp("parallel"))(x, pm, pm, pm, pm, o_gla, o_fox, o_mem, g_head, wg, wf, wm, wo, g_ffn)


def _merge_bwd(dh1b, pm, o_gla, o_fox, o_mem, g_head, wg, wf, wm, wgt, wft, wmt, wot):
    s = dh1b.shape[0]
    t = min(s, 256)

    def body(dh_ref, g0_ref, g1_ref, g2_ref, gg_ref, og_ref, of_ref, om_ref, gh_ref, wg_ref, wf_ref, wm_ref,
             wgt_ref, wft_ref, wmt_ref, wot_ref,
             dgt_ref, dgg_ref, dog_ref, dof_ref, dom_ref, dl_ref, dwg_ref, dwf_ref, dwm_ref, dgh_ref):
        @pl.when(pl.program_id(0) == 0)
        def _():
            dwg_ref[...] = jnp.zeros_like(dwg_ref)
            dwf_ref[...] = jnp.zeros_like(dwf_ref)
            dwm_ref[...] = jnp.zeros_like(dwm_ref)
            dgh_ref[...] = jnp.zeros_like(dgh_ref)

        dmerged = _dot(dh_ref[...], wot_ref[...])
        og = og_ref[...]
        xs, rs = _head_norm(og, None)
        on = jnp.concatenate(xs, axis=1)
        gg = gg_ref[...].astype(F32)
        sg = _sigmoid(gg)
        sil = gg * sg
        gh = gh_ref[...]
        ognb = (on * gh * sil).astype(BF16)
        ofb, omb = of_ref[...], om_ref[...]
        douts = []
        for idx, (gref, ob, w_ref, wt_ref, dw_ref) in enumerate((
                (g0_ref, ognb, wg_ref, wgt_ref, dwg_ref), (g1_ref, ofb, wf_ref, wft_ref, dwf_ref),
                (g2_ref, omb, wm_ref, wmt_ref, dwm_ref))):
            gt = _sigmoid(gref[...].astype(F32))
            y = _dot(ob, w_ref[...])
            dgt_ref[:, D * idx:D * (idx + 1)] = (dmerged * y * gt * (1.0 - gt)).astype(BF16)
            dy = (gt * dmerged).astype(BF16)
            dw_ref[...] += _dot_tn(ob, dy)
            douts.append(_dot(dy, wt_ref[...]))
        dogn, dof, dom = douts
        dof_ref[...] = dof.astype(BF16)
        dom_ref[...] = dom.astype(BF16)
        ind = (_iota((512, 128), 0) // 64 == _iota((512, 128), 1)).astype(F32)
        dl_ref[...] = _dot(dof.astype(BF16).astype(F32) * ofb.astype(F32), ind, precision=HI)
        dgg_ref[...] = (dogn * on * gh * (sg * (1.0 + gg * (1.0 - sg)))).astype(BF16)
        d_on = dogn * sil
        dgh_ref[...] += _fold8(d_on * on)
        dxn = d_on * gh
        outs = []
        for h in range(4):
            cols = slice(128 * h, 128 * (h + 1))
            dh_, xh = dxn[:, cols], xs[h]
            outs.append(rs[h] * (dh_ - xh * jnp.mean(dh_ * xh, axis=-1, keepdims=True)))
        dog_ref[...] = jnp.concatenate(outs, axis=1).astype(BF16)

    row = lambda i: (i, 0)
    const = lambda i: (0, 0)
    return pl.pallas_call(
        body, grid=(s // t,),
        in_specs=[pl.BlockSpec((t, D), row), pl.BlockSpec((t, D), lambda i: (i, 0)), pl.BlockSpec((t, D), lambda i: (i, 1)),
                  pl.BlockSpec((t, D), lambda i: (i, 2)), pl.BlockSpec((t, 512), lambda i: (i, C_GG // 512)),
                  pl.BlockSpec((t, 512), row), pl.BlockSpec((t, 512), row), pl.BlockSpec((t, 512), row),
                  pl.BlockSpec((1, 512), const), pl.BlockSpec((512, D), const), pl.BlockSpec((512, D), const),
                  pl.BlockSpec((512, D), const), pl.BlockSpec((D, 512), const), pl.BlockSpec((D, 512), const),
                  pl.BlockSpec((D, 512), const), pl.BlockSpec((D, D), const)],
        out_specs=[pl.BlockSpec((t, 3 * D), row), pl.BlockSpec((t, 512), row), pl.BlockSpec((t, 512), row),
                   pl.BlockSpec((t, 512), row), pl.BlockSpec((t, 512), row), pl.BlockSpec((t, 128), row),
                   pl.BlockSpec((512, D), const), pl.BlockSpec((512, D), const), pl.BlockSpec((512, D), const),
                   pl.BlockSpec((8, 512), const)],
        out_shape=[jax.ShapeDtypeStruct((s, 3 * D), BF16), jax.ShapeDtypeStruct((s, 512), BF16),
                   jax.ShapeDtypeStruct((s, 512), BF16), jax.ShapeDtypeStruct((s, 512), BF16),
                   jax.ShapeDtypeStruct((s, 512), BF16), jax.ShapeDtypeStruct((s, 128), F32),
                   jax.ShapeDtypeStruct((512, D), F32), jax.ShapeDtypeStruct((512, D), F32),
                   jax.ShapeDtypeStruct((512, D), F32), jax.ShapeDtypeStruct((8, 512), F32)],
        name="merge_bwd", compiler_params=_cp("arbitrary"))(
            dh1b, pm, pm, pm, pm, o_gla, o_fox, o_mem, g_head, wg, wf, wm, wgt, wft, wmt, wot)


def _ff2_loss(a, w2, h1, g_final, target):
    s, k = a.shape
    tm = min(s, 256)
    tk = 1024
    nk = k // tk

    def body(a_ref, w_ref, h1_ref, g_ref, t_ref, dh_ref, dhb_ref, loss_ref, dg_ref, acc):
        i, kk = pl.program_id(0), pl.program_id(1)

        @pl.when(kk == 0)
        def _():
            acc[...] = jnp.zeros_like(acc)

        @pl.when((i == 0) & (kk == 0))
        def _():
            loss_ref[...] = jnp.zeros_like(loss_ref)
            dg_ref[...] = jnp.zeros_like(dg_ref)

        acc[...] += _dot(_relu2_bf16(a_ref[...]), w_ref[...])

        @pl.when(kk == nk - 1)
        def _():
            h2 = h1_ref[...] + acc[...]
            r = lax.rsqrt(jnp.mean(h2 * h2, axis=-1, keepdims=True) + EPS)
            xn = h2 * r
            g = g_ref[...]
            err = xn * g - t_ref[...]
            e2 = _fold8(err * err)
            part = e2[:, 0:128]
            for c in range(1, D // 128):
                part = part + e2[:, 128 * c:128 * (c + 1)]
            loss_ref[...] += part
            dy = err * (1.0 / D)
            dg_ref[...] += _fold8(dy * xn)
            dxn = dy * g
            dh = r * (dxn - xn * jnp.mean(dxn * xn, axis=-1, keepdims=True))
            dh_ref[...] = dh
            dhb_ref[...] = dh.astype(BF16)

    row = lambda i, kk: (i, 0)
    return pl.pallas_call(
        body, grid=(s // tm, nk),
        in_specs=[pl.BlockSpec((tm, tk), lambda i, kk: (i, kk)), pl.BlockSpec((tk, D), lambda i, kk: (kk, 0)),
                  pl.BlockSpec((tm, D), row), pl.BlockSpec((1, D), lambda i, kk: (0, 0)), pl.BlockSpec((tm, D), row)],
        out_specs=[pl.BlockSpec((tm, D), row), pl.BlockSpec((tm, D), row), pl.BlockSpec((8, 128), lambda i, kk: (0, 0)),
                   pl.BlockSpec((8, D), lambda i, kk: (0, 0))],
        out_shape=[jax.ShapeDtypeStruct((s, D), F32), jax.ShapeDtypeStruct((s, D), BF16),
                   jax.ShapeDtypeStruct((8, 128), F32), jax.ShapeDtypeStruct((8, D), F32)],
        scratch_shapes=[pltpu.VMEM((tm, D), F32)],
        name="ff2_loss", compiler_params=_cp("arbitrary", "arbitrary"))(a, w2, h1, g_final, target)


def _adam(w, g, m, v, name):
    r, c = w.shape
    tr = r
    for cand in (512, 256, 128, 64, 32, 16, 8):
        if r % cand == 0 and cand * c * 4 <= (1 << 20):
            tr = cand
            break
    c1 = 1.0 - ADAM_B1 ** ADAM_STEP
    c2 = 1.0 - ADAM_B2 ** ADAM_STEP

    def body(w_ref, g_ref, m_ref, v_ref, d_ref, nm_ref, nv_ref):
        gv = g_ref[...]
        nm = ADAM_B1 * m_ref[...] + (1.0 - ADAM_B1) * gv
        nv = ADAM_B2 * v_ref[...] + (1.0 - ADAM_B2) * (gv * gv)
        d_ref[...] = -ADAM_LR * ((nm / c1) / (jnp.sqrt(nv / c2) + ADAM_EPS) + ADAM_WD * w_ref[...])
        nm_ref[...] = nm
        nv_ref[...] = nv

    spec = pl.BlockSpec((tr, c), lambda i: (i, 0))
    return pl.pallas_call(
        body, grid=(r // tr,), in_specs=[spec] * 4, out_specs=[spec] * 3,
        out_shape=[jax.ShapeDtypeStruct((r, c), F32)] * 3, name=name, compiler_params=_cp("parallel"))(w, g, m, v)


def _add2(a, b, name):
    n, r, c = a.shape
    tr = 464 if r % 464 == 0 else r

    def body(a_ref, b_ref, o_ref):
        o_ref[...] = a_ref[...] + b_ref[...]

    spec = pl.BlockSpec((1, tr, c), lambda k, i: (k, i, 0))
    return pl.pallas_call(body, grid=(n, r // tr), in_specs=[spec, spec], out_specs=spec,
                          out_shape=jax.ShapeDtypeStruct((n, r, c), F32), name=name,
                          compiler_params=_cp("parallel", "parallel"))(a, b)


def _sum4(a, name):
    _, r, c = a.shape
    tr = 464 if r % 464 == 0 else r

    def body(a_ref, o_ref):
        o_ref[...] = ((a_ref[0] + a_ref[1]) + a_ref[2]) + a_ref[3]

    return pl.pallas_call(body, grid=(r // tr,), in_specs=[pl.BlockSpec((4, tr, c), lambda i: (0, i, 0))],
                          out_specs=pl.BlockSpec((tr, c), lambda i: (i, 0)),
                          out_shape=jax.ShapeDtypeStruct((r, c), F32), name=name, compiler_params=_cp("parallel"))(a)


def _adam_small(w, gathered, m, v):
    c1 = 1.0 - ADAM_B1 ** ADAM_STEP
    c2 = 1.0 - ADAM_B2 ** ADAM_STEP

    def body(w_ref, g_ref, m_ref, v_ref, gs_ref, d_ref, nm_ref, nv_ref):
        gv = g_ref[0]
        for dev in range(1, N_DEV):
            gv = gv + g_ref[dev]
        gs_ref[...] = gv
        nm = ADAM_B1 * m_ref[...] + (1.0 - ADAM_B1) * gv
        nv = ADAM_B2 * v_ref[...] + (1.0 - ADAM_B2) * (gv * gv)
        d_ref[...] = -ADAM_LR * ((nm / c1) / (jnp.sqrt(nv / c2) + ADAM_EPS) + ADAM_WD * w_ref[...])
        nm_ref[...] = nm
        nv_ref[...] = nv

    return pl.pallas_call(body, out_shape=[jax.ShapeDtypeStruct((8, D), F32)] * 4, name="adam_small")(w, gathered, m, v)


def _place():
    return lax.axis_index("x"), lax.axis_index("y"), lax.axis_index("c")


def _other_chips(x, y):
    return [(1 - x, y), (x, 1 - y), (1 - x, 1 - y)]


def _gather_shards(p):
    r, c = p.shape
    hr = r // 2

    def body(p_ref, out_ref, send_sems, recv_sems, local_sem):
        x, y, cc = _place()
        sibling = (x, y, 1 - cc)
        chips = _other_chips(x, y)

        def half(chip, core):
            return out_ref.at[2 * chip[0] + chip[1], pl.ds(core * hr, hr), :]

        def copy(k, chip, core, to, src=None):
            return pltpu.make_async_remote_copy(
                src_ref=half(chip, core) if src is None else src, dst_ref=half(chip, core),
                send_sem=send_sems.at[k], recv_sem=recv_sems.at[k], device_id=to, device_id_type=MESH)

        mine = pltpu.make_async_copy(p_ref, out_ref.at[2 * x + y], local_sem)
        mine.start()
        my_half = p_ref.at[pl.ds(cc * hr, hr), :]
        first = [copy(j, (x, y), cc, (*chip, cc), src=my_half) for j, chip in enumerate(chips)]
        for cp in first:
            cp.start()
        passed = [copy(3 + j, chip, cc, sibling) for j, chip in enumerate(chips)]
        for j, chip in enumerate(chips):
            copy(j, chip, cc, (x, y, cc)).wait_recv()
            passed[j].start()
        for j, chip in enumerate(chips):
            copy(3 + j, chip, 1 - cc, (x, y, cc)).wait_recv()
        for cp in first + passed:
            cp.wait_send()
        mine.wait()

    any_spec = pl.BlockSpec(memory_space=pl.ANY)
    return pl.pallas_call(
        body, out_shape=jax.ShapeDtypeStruct((N_CHIPS, r, c), p.dtype), in_specs=[any_spec], out_specs=any_spec,
        scratch_shapes=[pltpu.SemaphoreType.DMA((6,)), pltpu.SemaphoreType.DMA((6,)), pltpu.SemaphoreType.DMA],
        name="gather_shards")(p)


def _swap_halves(g):
    n, r, c = g.shape
    hr = r // 2

    def body(g_ref, out_ref, send_sem, recv_sem):
        x, y, cc = _place()
        cp = pltpu.make_async_remote_copy(
            src_ref=g_ref.at[:, pl.ds((1 - cc) * hr, hr), :], dst_ref=out_ref,
            send_sem=send_sem, recv_sem=recv_sem, device_id=(x, y, 1 - cc), device_id_type=MESH)
        cp.start()
        cp.wait()

    any_spec = pl.BlockSpec(memory_space=pl.ANY)
    return pl.pallas_call(
        body, out_shape=jax.ShapeDtypeStruct((n, hr, c), g.dtype), in_specs=[any_spec], out_specs=any_spec,
        scratch_shapes=[pltpu.SemaphoreType.DMA, pltpu.SemaphoreType.DMA], name="swap_halves")(g)


def _scatter_partials(p):
    n, hr, c = p.shape

    def body(p_ref, out_ref, send_sems, recv_sems, local_sem):
        x, y, cc = _place()
        me = 2 * x + y
        chips = _other_chips(x, y)
        mine = pltpu.make_async_copy(p_ref.at[me], out_ref.at[me], local_sem)
        mine.start()
        sends = []
        for j, chip in enumerate(chips):
            sends.append(pltpu.make_async_remote_copy(
                src_ref=p_ref.at[2 * chip[0] + chip[1]], dst_ref=out_ref.at[me],
                send_sem=send_sems.at[j], recv_sem=recv_sems.at[j], device_id=(*chip, cc), device_id_type=MESH))
        for cp in sends:
            cp.start()
        for j, chip in enumerate(chips):
            src = 2 * chip[0] + chip[1]
            pltpu.make_async_remote_copy(
                src_ref=p_ref.at[me], dst_ref=out_ref.at[src], send_sem=send_sems.at[j], recv_sem=recv_sems.at[j],
                device_id=(*chip, cc), device_id_type=MESH).wait_recv()
        for cp in sends:
            cp.wait_send()
        mine.wait()

    any_spec = pl.BlockSpec(memory_space=pl.ANY)
    return pl.pallas_call(
        body, out_shape=jax.ShapeDtypeStruct((n, hr, c), p.dtype), in_specs=[any_spec], out_specs=any_spec,
        scratch_shapes=[pltpu.SemaphoreType.DMA((3,)), pltpu.SemaphoreType.DMA((3,)), pltpu.SemaphoreType.DMA],
        name="scatter_partials")(p)


def _join_halves(f):
    hr, c = f.shape

    def body(f_ref, out_ref, send_sem, recv_sem, local_sem):
        x, y, cc = _place()
        mine = pltpu.make_async_copy(f_ref, out_ref.at[pl.ds(cc * hr, hr), :], local_sem)
        mine.start()
        cp = pltpu.make_async_remote_copy(
            src_ref=f_ref, dst_ref=out_ref.at[pl.ds(cc * hr, hr), :],
            send_sem=send_sem, recv_sem=recv_sem, device_id=(x, y, 1 - cc), device_id_type=MESH)
        cp.start()
        pltpu.make_async_remote_copy(
            src_ref=f_ref, dst_ref=out_ref.at[pl.ds((1 - cc) * hr, hr), :],
            send_sem=send_sem, recv_sem=recv_sem, device_id=(x, y, 1 - cc), device_id_type=MESH).wait_recv()
        cp.wait_send()
        mine.wait()

    any_spec = pl.BlockSpec(memory_space=pl.ANY)
    return pl.pallas_call(
        body, out_shape=jax.ShapeDtypeStruct((2 * hr, c), f.dtype), in_specs=[any_spec], out_specs=any_spec,
        scratch_shapes=[pltpu.SemaphoreType.DMA, pltpu.SemaphoreType.DMA, pltpu.SemaphoreType.DMA],
        name="join_halves")(f)


def _gather_small(blk):
    m, n = blk.shape

    def body(x_ref, out_ref, send_sems, recv_sems, local_sem):
        x, y, cc = _place()
        me, sibling = (x, y, cc), (x, y, 1 - cc)
        chips = _other_chips(x, y)

        def slot(px, py, pc):
            return out_ref.at[4 * px + 2 * py + pc]

        def copy(k, block, to, src=None):
            return pltpu.make_async_remote_copy(
                src_ref=slot(*block) if src is None else src, dst_ref=slot(*block),
                send_sem=send_sems.at[k], recv_sem=recv_sems.at[k], device_id=to, device_id_type=MESH)

        mine = pltpu.make_async_copy(x_ref, slot(*me), local_sem)
        mine.start()
        first = [copy(0, me, sibling, src=x_ref)]
        first += [copy(1 + j, me, (*chip, cc), src=x_ref) for j, chip in enumerate(chips)]
        for cp in first:
            cp.start()
        passed = [copy(4 + j, (*chip, cc), sibling) for j, chip in enumerate(chips)]
        for j, chip in enumerate(chips):
            copy(1 + j, (*chip, cc), me).wait_recv()
            passed[j].start()
        copy(0, sibling, me).wait_recv()
        for j, chip in enumerate(chips):
            copy(4 + j, (*chip, 1 - cc), me).wait_recv()
        for cp in first + passed:
            cp.wait_send()
        mine.wait()

    vmem = pl.BlockSpec(memory_space=pltpu.VMEM)
    return pl.pallas_call(
        body, out_shape=jax.ShapeDtypeStruct((N_DEV, m, n), blk.dtype), in_specs=[vmem], out_specs=vmem,
        scratch_shapes=[pltpu.SemaphoreType.DMA((7,)), pltpu.SemaphoreType.DMA((7,)), pltpu.SemaphoreType.DMA],
        name="gather_small")(blk)


def _shard_shape(name, full_shape):
    shp = list(full_shape)
    shp[SHARD_AXIS[name]] //= N_CHIPS
    return tuple(shp)


FULL_SHAPES = {"w_in": (D, 6680), "w_alpha_up": (16, 256), "w_mem_kv": (D, D), "w_gla_o": (512, D), "w_fox_o": (512, D),
               "w_mem_o": (512, D), "w_out": (D, D), "w_ff1": (D, 4 * D), "w_ff2": (4 * D, D)}


def _pack(shards, dtype):
    flat = [shards[n].astype(dtype).reshape(-1) for n in WEIGHTS]
    total = sum(f.shape[0] for f in flat)
    flat.append(jnp.zeros((PACK_ROWS * PACK_W - total,), dtype))
    return jnp.concatenate(flat).reshape(PACK_ROWS, PACK_W)


def _unpack(packed):
    flat = packed.reshape(-1)
    out, off = {}, 0
    for n in WEIGHTS:
        shp = _shard_shape(n, FULL_SHAPES[n])
        size = shp[0] * shp[1]
        out[n] = flat[off:off + size].reshape(shp)
        off += size
    return out


def _split_shards(name, full):
    return jnp.split(full, N_CHIPS, axis=SHARD_AXIS[name])


def _pack_small(vals, scalar=None):
    row4 = jnp.concatenate([vals["b_alpha"].reshape(-1), vals["b_forget"].reshape(-1), jnp.zeros((D - 264,), F32)])
    row5 = jnp.concatenate([vals["g_gla_head"].reshape(-1), jnp.zeros((D - 512,), F32)])
    row6 = jnp.zeros((D,), F32) if scalar is None else jnp.broadcast_to(scalar, (D,))
    rows = [vals["g_mix"].reshape(-1), vals["g_mem"].reshape(-1), vals["g_ffn"].reshape(-1), vals["g_final"].reshape(-1),
            row4, row5, row6, jnp.zeros((D,), F32)]
    return jnp.stack(rows)


def _unpack_small(blk):
    return {"g_mix": blk[0].reshape(1, D), "g_mem": blk[1].reshape(1, D), "g_ffn": blk[2].reshape(1, D),
            "g_final": blk[3].reshape(D), "b_alpha": blk[4, 0:256].reshape(1, 256), "b_forget": blk[4, 256:264].reshape(1, 8),
            "g_gla_head": blk[5, 0:512].reshape(1, 4, 128)}


def _local_step(x, mem, target, wb, small):
    s = x.shape[0]
    nm = mem.shape[0]
    t = _row_tile(s)
    nb = s // t
    w_in = wb["w_in"]
    w_main = jnp.concatenate([w_in[:, 3608:6680], w_in[:, 0:1536], w_in[:, 1552:3088], w_in[:, 3096:3608]], axis=1)
    w_e = jnp.concatenate([w_in[:, 1536:1552], w_in[:, 3088:3096], jnp.zeros((D, PE_W - 24), BF16)], axis=1)
    w_in_pt = jnp.concatenate([w_main, w_e, jnp.zeros((D, DP_W - PM_W - PE_W), BF16)], axis=1).T
    wau_p = jnp.concatenate([wb["w_alpha_up"], jnp.zeros((PE_W - 16, 256), BF16)], axis=0)
    b_alpha = small["b_alpha"].reshape(1, 256)
    bias_e = jnp.concatenate([jnp.zeros((FF_LANE,), F32), small["b_forget"].reshape(-1),
                              jnp.zeros((PE_W - FF_LANE - 8,), F32)]).reshape(1, PE_W)
    g_mix, g_mem, g_ffn = small["g_mix"].reshape(1, D), small["g_mem"].reshape(1, D), small["g_ffn"].reshape(1, D)
    g_final = small["g_final"].reshape(1, D)
    g_head = small["g_gla_head"].reshape(1, 512)

    u, r1 = _rms_fwd(x, g_mix, "norm_mix")
    pm = _mm_nn(u, w_main, out_dtype=BF16, tm=t, tn=512, tk=D, name="proj_main")
    pe = _mm_nn(u, w_e, out_dtype=F32, tm=t, tn=PE_W, tk=D, name="proj_narrow")
    o_gla, states = _gla_fwd(pm, pe, wau_p, b_alpha)
    fcum = _fcum_fwd(pe, bias_e)
    f_hs = fcum[:, FF_LANE:FF_LANE + 8].T
    qkv = pm[:, C_FQ:C_FQ + 1536].reshape(s, 3, 8, 64)
    q_h = (qkv[:, 0].astype(F32) * 0.125).astype(BF16).transpose(1, 0, 2)
    k_h = qkv[:, 1].transpose(1, 0, 2)
    v_h = qkv[:, 2].transpose(1, 0, 2)
    ones3 = jnp.ones((8, s, 3), BF16)
    f3 = jnp.stack(_split3(f_hs), axis=-1)
    k_aug = jnp.concatenate([k_h, ones3, -f3, jnp.zeros((8, s, FOX_KA - 70), BF16)], axis=-1)
    qf_aug = jnp.concatenate([q_h, f3, ones3, jnp.zeros((8, s, FOX_KA - 70), BF16)], axis=-1)
    vt = v_h.reshape(8, nb, t, 64).transpose(0, 1, 3, 2)
    o_fox_t, lse = _fox_fwd(k_aug, qf_aug, vt)
    o_fox = o_fox_t.transpose(2, 0, 1).reshape(s, 512)
    mn, rm = _rms_fwd(mem, g_mem, "norm_mem")
    mkv = _mm_nn(mn, wb["w_mem_kv"], out_dtype=BF16, tm=nm, tn=512, tk=D, name="mem_kv")
    o_mem = _mem_attn_fwd(pm, mkv)
    merged, h1, u2, r2 = _merge_fwd(x, pm, o_gla, o_fox, o_mem, g_head, wb["w_gla_o"], wb["w_fox_o"], wb["w_mem_o"],
                                    wb["w_out"], g_ffn)
    a = _mm_nn(u2, wb["w_ff1"], out_dtype=BF16, tm=t, tn=1024, tk=D, name="ff1")
    dh2, dh2b, loss8, dgfin8 = _ff2_loss(a, wb["w_ff2"], h1, g_final, target)
    loss = 0.5 * jnp.sum(loss8) / D

    da = _mm_nn(dh2b, wb["w_ff2"].T, out_dtype=BF16, tm=t, tn=1024, tk=D, name="d_act",
                epi=lambda acc, at: acc * (2.0 * jnp.maximum(at.astype(F32), 0.0)), extra=a)
    gw = {}
    gw["w_ff2"] = _mm_tn(a, dh2b, tm=1024, tn=D, ts=t, name="dw_ff2", a_fn=_relu2_bf16)
    gw["w_ff1"] = _mm_tn(u2, da, tm=D, tn=1024, ts=t, name="dw_ff1")
    dh1, dh1b, dgffn8 = _mm_norm_bwd(da, wb["w_ff1"].T, h1, r2, g_ffn, dh2, tk=1024, name="d_h1", want_bf16=True)
    gw["w_out"] = _mm_tn(merged, dh1b, tm=D, tn=D, ts=t, name="dw_out")
    (dgates, dgg, do_gla, do_fox, do_mem, delta, gw["w_gla_o"], gw["w_fox_o"], gw["w_mem_o"], dgh8) = _merge_bwd(
        dh1b, pm, o_gla, o_fox, o_mem, g_head, wb["w_gla_o"], wb["w_fox_o"], wb["w_mem_o"],
        wb["w_gla_o"].T, wb["w_fox_o"].T, wb["w_mem_o"].T, wb["w_out"].T)
    dgq, dgk, dgv, de_gla, dwau_p, dba8 = _gla_bwd(pm, pe, wau_p, wau_p.T, b_alpha, do_gla, states)
    gw["w_alpha_up"] = dwau_p[0:16, :]
    do_h = do_fox.reshape(s, 8, 64).transpose(1, 0, 2)
    g3 = jnp.stack(_split3(f_hs - lse[:, 0, :]), axis=-1)
    d3 = jnp.stack(_split3(delta[:, 0:8].T), axis=-1)
    q_aug = jnp.concatenate([q_h, g3, ones3, jnp.zeros((8, s, FOX_KA - 70), BF16)], axis=-1)
    do_aug = jnp.concatenate([do_h, d3, jnp.zeros((8, s, FOX_KA - 67), BF16)], axis=-1)
    v_aug = jnp.concatenate([v_h, -ones3, jnp.zeros((8, s, FOX_KA - 67), BF16)], axis=-1)
    ks_aug = jnp.concatenate([(k_h.astype(F32) * 0.125).astype(BF16), jnp.zeros((8, s, 64), BF16)], axis=-1)
    dq_aug, dfk_t, dfv_t, dfcol = _fox_bwd(
        q_aug, do_aug, q_h.reshape(8, nb, t, 64).transpose(0, 1, 3, 2), do_h.reshape(8, nb, t, 64).transpose(0, 1, 3, 2),
        k_aug, v_aug, ks_aug)
    dfq = dq_aug[:, :, 0:64].astype(BF16).transpose(1, 0, 2).reshape(s, 512)
    back = lambda z: z.transpose(2, 0, 1).reshape(s, 512)
    dfk, dfv = back(dfk_t), back(dfv_t)
    df = jnp.pad((dq_aug[:, :, 64] - dfcol[:, 0, :]).T, ((0, 0), (FF_LANE, PE_W - FF_LANE - 8)))
    de_fox, dbf8 = _fcum_bwd(pe, bias_e, df)
    dmq, dmk, dmv = _mem_attn_bwd(pm, mkv, do_mem)
    dmkv = jnp.concatenate([dmk, dmv], axis=1).astype(BF16)
    gw["w_mem_kv"] = _mm_tn(mn, dmkv, tm=D, tn=D, ts=nm, name="dw_mem_kv")
    dmn_g = _mm_nn(dmkv, wb["w_mem_kv"].T, out_dtype=F32, tm=nm, tn=D, tk=D, name="d_mem_norm")
    dgmem8 = _gain_grad(dmn_g, mem, rm, "dg_mem")
    dproj = jnp.concatenate(
        [dgates, dgq, dgk, dgv, dgg, dfq, dfk, dfv, dmq,
         (de_gla + de_fox).astype(BF16), jnp.zeros((s, DP_W - PM_W - PE_W), BF16)], axis=1)
    dwp = _mm_tn(u, dproj, tm=D, tn=1024, ts=t, name="dw_in")
    gw["w_in"] = jnp.concatenate([dwp[:, 3072:4608], dwp[:, PM_W:PM_W + 16], dwp[:, 4608:6144],
                                  dwp[:, PM_W + 16:PM_W + 24], dwp[:, 6144:6656], dwp[:, 0:3072]], axis=1)
    grad_x, dgmix8 = _mm_norm_bwd(dproj, w_in_pt, x, r1, g_mix, dh1, tk=1024, name="d_x", want_bf16=False)
    gs = {"g_mix": dgmix8.sum(0), "g_mem": dgmem8.sum(0), "g_ffn": dgffn8.sum(0), "g_final": dgfin8.sum(0),
          "b_alpha": dba8.sum(0), "b_forget": dbf8.sum(0)[FF_LANE:FF_LANE + 8], "g_gla_head": dgh8.sum(0)}
    return loss, grad_x, gw, gs


def kernel(x, mem, g_mix, w_in, w_alpha_up, b_alpha, b_forget, g_gla_head, g_mem, w_mem_kv, w_gla_o, w_fox_o, w_mem_o, w_out, g_ffn, w_ff1, w_ff2, g_final, loss_target, m_g_mix, m_w_in, m_w_alpha_up, m_b_alpha, m_b_forget, m_g_gla_head, m_g_mem, m_w_mem_kv, m_w_gla_o, m_w_fox_o, m_w_mem_o, m_w_out, m_g_ffn, m_w_ff1, m_w_ff2, m_g_final, v_g_mix, v_w_in, v_w_alpha_up, v_b_alpha, v_b_forget, v_g_gla_head, v_g_mem, v_w_mem_kv, v_w_gla_o, v_w_fox_o, v_w_mem_o, v_w_out, v_g_ffn, v_w_ff1, v_w_ff2, v_g_final):
    args = dict(locals())
    w_sh = {n: args[n][0] for n in WEIGHTS}
    m_sh = {n: args["m_" + n][0] for n in WEIGHTS}
    v_sh = {n: args["v_" + n][0] for n in WEIGHTS}
    small = {n: args[n] for n in SMALL}

    gathered = _gather_shards(_pack(w_sh, BF16))
    parts = [_unpack(gathered[k]) for k in range(N_CHIPS)]
    wb = {n: jnp.concatenate([parts[k][n] for k in range(N_CHIPS)], axis=SHARD_AXIS[n]) for n in WEIGHTS}

    loss, grad_x, gw, gs = _local_step(x[0], mem[0], loss_target[0], wb, small)

    g_by_chip = [_split_shards(n, gw[n]) for n in WEIGHTS]
    g_packed = jnp.stack([_pack({n: g_by_chip[i][k] for i, n in enumerate(WEIGHTS)}, F32) for k in range(N_CHIPS)])
    hr = PACK_ROWS // 2
    cc = lax.axis_index("c")
    from_sibling = _swap_halves(g_packed)
    my_half = lax.dynamic_slice_in_dim(g_packed, cc * hr, hr, axis=1)
    chip_sum = _add2(my_half, from_sibling, "chip_sum")
    by_chip = _scatter_partials(chip_sum)
    g_shard = _join_halves(_sum4(by_chip, "shard_sum"))

    g_out = _unpack(g_shard)
    d_out, m_out, v_out = {}, {}, {}
    for n in WEIGHTS:
        d_out[n], m_out[n], v_out[n] = _adam(w_sh[n], g_out[n], m_sh[n], v_sh[n], "adam_" + n)

    small_all = _gather_small(_pack_small(gs, loss))
    sm = {n: args["m_" + n] for n in SMALL}
    sv = {n: args["v_" + n] for n in SMALL}
    gs_sum, sd, snm, snv = _adam_small(_pack_small(small), small_all, _pack_small(sm), _pack_small(sv))
    gs_o, sd_o, snm_o, snv_o = _unpack_small(gs_sum), _unpack_small(sd), _unpack_small(snm), _unpack_small(snv)

    names = ["g_mix", "w_in", "w_alpha_up", "b_alpha", "b_forget", "g_gla_head", "g_mem", "w_mem_kv", "w_gla_o", "w_fox_o",
             "w_mem_o", "w_out", "g_ffn", "w_ff1", "w_ff2", "g_final"]

    def pick(big, sml, n):
        return big[n][None] if n in big else sml[n]

    outs = [gs_sum[6, 0], grad_x[None]]
    for big, sml in ((g_out, gs_o), (d_out, sd_o), (m_out, snm_o), (v_out, snv_o)):
        outs += [pick(big, sml, n) for n in names]
    return tuple(outs)
```

```python
import functools

import numpy as np
import jax
import jax.numpy as jnp
from jax import lax
from jax.experimental import pallas as pl
from jax.experimental.pallas import tpu as pltpu

F32 = jnp.float32
BF16 = jnp.bfloat16
HI = lax.Precision.HIGHEST
MESH = pl.DeviceIdType.MESH

EPS = 1e-6
D = 1024
CHUNK = 64
GLA_TAU = 16.0
N_CHIPS = 4
N_DEV = 8
VMEM_LIMIT_BYTES = 56 * 1024 * 1024

ADAM_LR, ADAM_B1, ADAM_B2, ADAM_EPS, ADAM_WD, ADAM_STEP = 0.001, 0.9, 0.999, 1e-08, 0.01, 10

PM_W = 6656
PE_W = 128
DP_W = 7168
C_GQ, C_GK, C_GV, C_GG, C_FQ, C_FK, C_FV, C_MQ = 3072, 3328, 3584, 4096, 4608, 5120, 5632, 6144
FF_LANE = 16

WEIGHTS = ("w_in", "w_alpha_up", "w_mem_kv", "w_gla_o", "w_fox_o", "w_mem_o", "w_out", "w_ff1", "w_ff2")
SHARD_AXIS = {"w_in": 1, "w_alpha_up": 1, "w_mem_kv": 0, "w_gla_o": 1, "w_fox_o": 1, "w_mem_o": 1, "w_out": 0,
              "w_ff1": 1, "w_ff2": 0}
SMALL = ("g_mix", "g_mem", "g_ffn", "g_final", "b_alpha", "b_forget", "g_gla_head")
PACK_W = 1024
PACK_ROWS = 4640


def _cp(*sem):
    return pltpu.CompilerParams(dimension_semantics=sem, vmem_limit_bytes=VMEM_LIMIT_BYTES)


def _dot(a, b, **kw):
    return jnp.dot(a, b, preferred_element_type=F32, **kw)


def _dot_nt(a, b, **kw):
    return lax.dot_general(a, b, (((1,), (1,)), ((), ())), preferred_element_type=F32, **kw)


def _dot_tn(a, b, **kw):
    return lax.dot_general(a, b, (((0,), (0,)), ((), ())), preferred_element_type=F32, **kw)


def _sigmoid(x):
    return 1.0 / (1.0 + jnp.exp(-x))


def _log_sigmoid(x):
    return -(jnp.maximum(-x, 0.0) + jnp.log1p(jnp.exp(-jnp.abs(x))))


def _fold8(x):
    m, n = x.shape
    return x.reshape(m // 8, 8, n).sum(axis=0)


def _iota(shape, dim):
    return lax.broadcasted_iota(jnp.int32, shape, dim)


def _row_tile(s):
    return min(s, 512)


def _mm_nn(a, b, *, out_dtype, tm, tn, tk, name, a_fn=None, epi=None, extra=None):
    m, k = a.shape
    _, n = b.shape
    nk = k // tk

    def body(*refs):
        if extra is None:
            a_ref, b_ref, o_ref, acc = refs
            x_ref = None
        else:
            a_ref, b_ref, x_ref, o_ref, acc = refs
        kk = pl.program_id(2)

        @pl.when(kk == 0)
        def _():
            acc[...] = jnp.zeros_like(acc)

        at = a_ref[...]
        if a_fn is not None:
            at = a_fn(at)
        acc[...] += _dot(at, b_ref[...])

        @pl.when(kk == nk - 1)
        def _():
            r = acc[...]
            if epi is not None:
                r = epi(r, None if x_ref is None else x_ref[...])
            o_ref[...] = r.astype(out_dtype)

    in_specs = [pl.BlockSpec((tm, tk), lambda i, j, kk: (i, kk)), pl.BlockSpec((tk, tn), lambda i, j, kk: (kk, j))]
    args = [a, b]
    if extra is not None:
        in_specs.append(pl.BlockSpec((tm, tn), lambda i, j, kk: (i, j)))
        args.append(extra)
    return pl.pallas_call(
        body, grid=(m // tm, n // tn, nk), in_specs=in_specs,
        out_specs=pl.BlockSpec((tm, tn), lambda i, j, kk: (i, j)),
        out_shape=jax.ShapeDtypeStruct((m, n), out_dtype),
        scratch_shapes=[pltpu.VMEM((tm, tn), F32)], name=name,
        compiler_params=_cp("parallel", "parallel", "arbitrary"))(*args)


def _mm_tn(a, b, *, tm, tn, ts, name, a_fn=None):
    s, m = a.shape
    _, n = b.shape
    ns = s // ts

    def body(a_ref, b_ref, o_ref, acc):
        kk = pl.program_id(2)

        @pl.when(kk == 0)
        def _():
            acc[...] = jnp.zeros_like(acc)

        at = a_ref[...]
        if a_fn is not None:
            at = a_fn(at)
        acc[...] += _dot_tn(at, b_ref[...])

        @pl.when(kk == ns - 1)
        def _():
            o_ref[...] = acc[...]

    return pl.pallas_call(
        body, grid=(m // tm, n // tn, ns),
        in_specs=[pl.BlockSpec((ts, tm), lambda i, j, kk: (kk, i)), pl.BlockSpec((ts, tn), lambda i, j, kk: (kk, j))],
        out_specs=pl.BlockSpec((tm, tn), lambda i, j, kk: (i, j)),
        out_shape=jax.ShapeDtypeStruct((m, n), F32),
        scratch_shapes=[pltpu.VMEM((tm, tn), F32)], name=name,
        compiler_params=_cp("parallel", "parallel", "arbitrary"))(a, b)


def _relu2_bf16(t):
    r = jnp.maximum(t.astype(F32), 0.0)
    return (r * r).astype(BF16)


def _rms_fwd(x, g, name):
    s, d = x.shape
    tm = min(s, 512)

    def body(x_ref, g_ref, u_ref, r_ref):
        xv = x_ref[...]
        r = lax.rsqrt(jnp.mean(xv * xv, axis=-1, keepdims=True) + EPS)
        u_ref[...] = ((xv * r) * g_ref[...]).astype(BF16)
        r_ref[...] = r

    return pl.pallas_call(
        body, grid=(s // tm,),
        in_specs=[pl.BlockSpec((tm, d), lambda i: (i, 0)), pl.BlockSpec((1, d), lambda i: (0, 0))],
        out_specs=[pl.BlockSpec((tm, d), lambda i: (i, 0)), pl.BlockSpec((tm, 1), lambda i: (i, 0))],
        out_shape=[jax.ShapeDtypeStruct((s, d), BF16), jax.ShapeDtypeStruct((s, 1), F32)],
        name=name, compiler_params=_cp("parallel"))(x, g)


def _mm_norm_bwd(a, b, xin, r, g, dres, *, tk, name, want_bf16):
    s, k = a.shape
    tm = min(s, 256)
    nk = k // tk

    def body(a_ref, b_ref, x_ref, r_ref, g_ref, dres_ref, *rest):
        if want_bf16:
            dx_ref, dxb_ref, dg_ref, acc = rest
        else:
            dx_ref, dg_ref, acc = rest
        i, kk = pl.program_id(0), pl.program_id(1)

        @pl.when(kk == 0)
        def _():
            acc[...] = jnp.zeros_like(acc)

        @pl.when((i == 0) & (kk == 0))
        def _():
            dg_ref[...] = jnp.zeros_like(dg_ref)

        acc[...] += _dot(a_ref[...], b_ref[...])

        @pl.when(kk == nk - 1)
        def _():
            du = acc[...]
            xn = x_ref[...] * r_ref[...]
            dg_ref[...] += _fold8(du * xn)
            dxn = du * g_ref[...]
            dx = dres_ref[...] + r_ref[...] * (dxn - xn * jnp.mean(dxn * xn, axis=-1, keepdims=True))
            dx_ref[...] = dx
            if want_bf16:
                dxb_ref[...] = dx.astype(BF16)

    row = lambda i, kk: (i, 0)
    out_specs = [pl.BlockSpec((tm, D), row)]
    out_shape = [jax.ShapeDtypeStruct((s, D), F32)]
    if want_bf16:
        out_specs.append(pl.BlockSpec((tm, D), row))
        out_shape.append(jax.ShapeDtypeStruct((s, D), BF16))
    out_specs.append(pl.BlockSpec((8, D), lambda i, kk: (0, 0)))
    out_shape.append(jax.ShapeDtypeStruct((8, D), F32))
    return pl.pallas_call(
        body, grid=(s // tm, nk),
        in_specs=[pl.BlockSpec((tm, tk), lambda i, kk: (i, kk)), pl.BlockSpec((tk, D), lambda i, kk: (kk, 0)),
                  pl.BlockSpec((tm, D), row), pl.BlockSpec((tm, 1), row), pl.BlockSpec((1, D), lambda i, kk: (0, 0)),
                  pl.BlockSpec((tm, D), row)],
        out_specs=out_specs, out_shape=out_shape, scratch_shapes=[pltpu.VMEM((tm, D), F32)],
        name=name, compiler_params=_cp("arbitrary", "arbitrary"))(a, b, xin, r, g, dres)


def _gla_consts():
    lmask = _iota((4 * CHUNK, CHUNK), 0) % CHUNK >= _iota((4 * CHUNK, CHUNK), 1)
    hmask = _iota((256, 256), 0) // CHUNK == _iota((256, 256), 1) // CHUNK
    bd = _iota((256, 512), 0) // CHUNK == _iota((256, 512), 1) // 128
    return lmask, hmask, bd


def _fold_heads(x):
    return x[0:64] + x[64:128] + x[128:192] + x[192:256]


def _gla_chunk(lac, qc, kc):
    tri = (_iota((CHUNK, CHUNK), 0) >= _iota((CHUNK, CHUNK), 1)).astype(F32)
    b = _dot(tri, lac, precision=HI)
    bl = b[CHUNK - 1:CHUNK, :]
    ep, en, ek = jnp.exp(b), jnp.exp(-b), jnp.exp(bl - b)
    decb = jnp.exp(_dot_tn(lac, jnp.ones((CHUNK, 128), F32), precision=HI))
    decb = jnp.concatenate([decb] * 4, axis=1)
    return bl, ep, en, ek, decb, qc * ep, qc * en, kc * en, kc * ep, kc * ek


def _gla_fwd(pm, pe, wau_p, b_alpha):
    s = pm.shape[0]
    t = _row_tile(s)
    nc = t // CHUNK

    def body(q_ref, k_ref, v_ref, e_ref, wau_ref, ba_ref, o_ref, st_ref, state, la_scr):
        @pl.when(pl.program_id(0) == 0)
        def _():
            state[...] = jnp.zeros_like(state)

        z = _dot(e_ref[...].astype(BF16), wau_ref[...]) + ba_ref[...]
        la_scr[...] = _log_sigmoid(z) * (1.0 / GLA_TAU)
        lmask, hmask, bd = _gla_consts()

        def chunk(c, carry):
            rows = pl.ds(pl.multiple_of(c * CHUNK, CHUNK), CHUNK)
            qc = q_ref[rows, :].astype(F32) * 0.125
            kc = k_ref[rows, :].astype(F32)
            vc = v_ref[rows, :]
            _, _, _, _, decb, qp, qn, kn, kp, kk = _gla_chunk(la_scr[rows, :], qc, kc)
            qs = jnp.where(hmask, jnp.concatenate([qp] * 4, axis=0), 0.0).astype(BF16)
            qns = jnp.where(hmask, jnp.concatenate([qn] * 4, axis=0), 0.0).astype(BF16)
            attn = jnp.where(lmask, _dot_nt(qs, kn.astype(BF16)), _dot_nt(qns, kp.astype(BF16))).astype(BF16)
            st = state[...]
            o_intra = _fold_heads(jnp.where(bd, _dot(attn, vc), 0.0))
            o_ref[rows, :] = o_intra + _dot(qp.astype(BF16), st.astype(BF16))
            for h in range(4):
                st_ref[c, :, 128 * h:128 * (h + 1)] = st[64 * h:64 * (h + 1), 128 * h:128 * (h + 1)]
            kv = jnp.where(bd, _dot_tn(kk.astype(BF16), vc), 0.0)
            state[...] = st * decb + kv
            return carry

        lax.fori_loop(0, nc, chunk, 0)

    return pl.pallas_call(
        body, grid=(s // t,),
        in_specs=[pl.BlockSpec((t, 256), lambda i: (i, C_GQ // 256)), pl.BlockSpec((t, 256), lambda i: (i, C_GK // 256)),
                  pl.BlockSpec((t, 512), lambda i: (i, C_GV // 512)), pl.BlockSpec((t, PE_W), lambda i: (i, 0)),
                  pl.BlockSpec((PE_W, 256), lambda i: (0, 0)), pl.BlockSpec((1, 256), lambda i: (0, 0))],
        out_specs=[pl.BlockSpec((t, 512), lambda i: (i, 0)), pl.BlockSpec((nc, CHUNK, 512), lambda i: (i, 0, 0))],
        out_shape=[jax.ShapeDtypeStruct((s, 512), F32), jax.ShapeDtypeStruct((s // CHUNK, CHUNK, 512), F32)],
        scratch_shapes=[pltpu.VMEM((256, 512), F32), pltpu.VMEM((t, 256), F32)],
        name="gla_fwd", compiler_params=_cp("arbitrary"))(pm, pm, pm, pe, wau_p, b_alpha)


def _gla_bwd(pm, pe, wau_p, wau_pt, b_alpha, do, states):
    s = pm.shape[0]
    t = _row_tile(s)
    nc = t // CHUNK
    nb = s // t

    def body(q_ref, k_ref, v_ref, e_ref, wau_ref, waut_ref, ba_ref, do_ref, st_ref,
             dq_ref, dk_ref, dv_ref, de_ref, dwau_ref, dba_ref, gstate, la_scr, dla_scr):
        @pl.when(pl.program_id(0) == 0)
        def _():
            gstate[...] = jnp.zeros_like(gstate)
            dwau_ref[...] = jnp.zeros_like(dwau_ref)
            dba_ref[...] = jnp.zeros_like(dba_ref)

        eb = e_ref[...].astype(BF16)
        z = _dot(eb, wau_ref[...]) + ba_ref[...]
        la_scr[...] = _log_sigmoid(z) * (1.0 / GLA_TAU)
        lmask, hmask, bd = _gla_consts()
        triu = (_iota((CHUNK, CHUNK), 0) <= _iota((CHUNK, CHUNK), 1)).astype(F32)
        last_row = _iota((CHUNK, 256), 0) == CHUNK - 1

        def chunk(cc, carry):
            c = nc - 1 - cc
            rows = pl.ds(pl.multiple_of(c * CHUNK, CHUNK), CHUNK)
            qc = q_ref[rows, :].astype(F32) * 0.125
            kc = k_ref[rows, :].astype(F32)
            vc = v_ref[rows, :]
            dob = do_ref[rows, :]
            bl, ep, en, ek, decb, qp, qn, kn, kp, kk = _gla_chunk(la_scr[rows, :], qc, kc)
            qs = jnp.where(hmask, jnp.concatenate([qp] * 4, axis=0), 0.0).astype(BF16)
            qns = jnp.where(hmask, jnp.concatenate([qn] * 4, axis=0), 0.0).astype(BF16)
            knb, kpb = kn.astype(BF16), kp.astype(BF16)
            attn = jnp.where(lmask, _dot_nt(qs, knb), _dot_nt(qns, kpb)).astype(BF16)
            st = jnp.where(bd, jnp.concatenate([st_ref[c]] * 4, axis=0), 0.0)
            g = gstate[...]
            gb = g.astype(BF16)
            do_s = jnp.where(bd, jnp.concatenate([dob] * 4, axis=0), jnp.zeros((), BF16))
            dattn = _dot_nt(do_s, vc)
            dv_ref[rows, :] = (_dot_tn(attn, do_s) + _dot(kk.astype(BF16), gb)).astype(BF16)
            dac = jnp.where(lmask, dattn, 0.0).astype(BF16)
            daa = jnp.where(lmask, 0.0, dattn).astype(BF16)
            dqp = _fold_heads(jnp.where(hmask, _dot(dac, knb), 0.0)) + _dot_nt(dob, st.astype(BF16))
            dqn = _fold_heads(jnp.where(hmask, _dot(daa, kpb), 0.0))
            dkn = _dot_tn(dac, qs)
            dkp = _dot_tn(daa, qns)
            dkk = _dot_nt(vc, gb)
            ddec = _dot_nt(jnp.ones((8, 512), F32), g * st, precision=HI)[0:1, :]
            gstate[...] = decb * g + jnp.where(bd, _dot_tn(qp.astype(BF16), dob), 0.0)
            dq_ref[rows, :] = ((dqp * ep + dqn * en) * 0.125).astype(BF16)
            dk_ref[rows, :] = (dkn * en + dkp * ep + dkk * ek).astype(BF16)
            dek = dkk * kc * ek
            db = (dqp * qc + dkp * kc) * ep - (dqn * qc + dkn * kc) * en - dek
            dbl = jnp.sum(dek, axis=0, keepdims=True) + ddec * jnp.exp(bl)
            db = db + jnp.where(last_row, dbl, 0.0)
            dla_scr[rows, :] = _dot(triu, db, precision=HI)
            return carry

        lax.fori_loop(0, nc, chunk, 0)
        dz = dla_scr[...] * (1.0 / GLA_TAU) * _sigmoid(-z)
        dzb = dz.astype(BF16)
        dwau_ref[...] += _dot_tn(eb, dzb)
        dba_ref[...] += _fold8(dz)
        de_ref[...] = _dot(dzb, waut_ref[...])

    rev = lambda i: nb - 1 - i
    return pl.pallas_call(
        body, grid=(nb,),
        in_specs=[pl.BlockSpec((t, 256), lambda i: (rev(i), C_GQ // 256)), pl.BlockSpec((t, 256), lambda i: (rev(i), C_GK // 256)),
                  pl.BlockSpec((t, 512), lambda i: (rev(i), C_GV // 512)), pl.BlockSpec((t, PE_W), lambda i: (rev(i), 0)),
                  pl.BlockSpec((PE_W, 256), lambda i: (0, 0)), pl.BlockSpec((256, PE_W), lambda i: (0, 0)),
                  pl.BlockSpec((1, 256), lambda i: (0, 0)), pl.BlockSpec((t, 512), lambda i: (rev(i), 0)),
                  pl.BlockSpec((nc, CHUNK, 512), lambda i: (rev(i), 0, 0))],
        out_specs=[pl.BlockSpec((t, 256), lambda i: (rev(i), 0)), pl.BlockSpec((t, 256), lambda i: (rev(i), 0)),
                   pl.BlockSpec((t, 512), lambda i: (rev(i), 0)), pl.BlockSpec((t, PE_W), lambda i: (rev(i), 0)),
                   pl.BlockSpec((PE_W, 256), lambda i: (0, 0)), pl.BlockSpec((8, 256), lambda i: (0, 0))],
        out_shape=[jax.ShapeDtypeStruct((s, 256), BF16), jax.ShapeDtypeStruct((s, 256), BF16),
                   jax.ShapeDtypeStruct((s, 512), BF16), jax.ShapeDtypeStruct((s, PE_W), F32),
                   jax.ShapeDtypeStruct((PE_W, 256), F32), jax.ShapeDtypeStruct((8, 256), F32)],
        scratch_shapes=[pltpu.VMEM((256, 512), F32), pltpu.VMEM((t, 256), F32), pltpu.VMEM((t, 256), F32)],
        name="gla_bwd", compiler_params=_cp("arbitrary"))(pm, pm, pm, pe, wau_p, wau_pt, b_alpha, do, states)


def _fcum_fwd(pe, bias):
    s = pe.shape[0]
    t = min(s, 256)

    def body(e_ref, b_ref, f_ref, carry):
        @pl.when(pl.program_id(0) == 0)
        def _():
            carry[...] = jnp.zeros_like(carry)

        lf = _log_sigmoid(e_ref[...] + b_ref[...])
        tri = (_iota((t, t), 0) >= _iota((t, t), 1)).astype(F32)
        f = _dot(tri, lf, precision=HI) + carry[0:1, :]
        f_ref[...] = f
        carry[...] = jnp.broadcast_to(f[t - 1:t, :], carry.shape)

    return pl.pallas_call(
        body, grid=(s // t,),
        in_specs=[pl.BlockSpec((t, PE_W), lambda i: (i, 0)), pl.BlockSpec((1, PE_W), lambda i: (0, 0))],
        out_specs=pl.BlockSpec((t, PE_W), lambda i: (i, 0)),
        out_shape=jax.ShapeDtypeStruct((s, PE_W), F32), scratch_shapes=[pltpu.VMEM((8, PE_W), F32)],
        name="fcum_fwd", compiler_params=_cp("arbitrary"))(pe, bias)


def _fcum_bwd(pe, bias, df):
    s = pe.shape[0]
    t = min(s, 256)
    nb = s // t

    def body(e_ref, b_ref, df_ref, de_ref, db_ref, carry):
        @pl.when(pl.program_id(0) == 0)
        def _():
            carry[...] = jnp.zeros_like(carry)
            db_ref[...] = jnp.zeros_like(db_ref)

        triu = (_iota((t, t), 0) <= _iota((t, t), 1)).astype(F32)
        dlf = _dot(triu, df_ref[...], precision=HI) + carry[0:1, :]
        carry[...] = jnp.broadcast_to(dlf[0:1, :], carry.shape)
        lane = _iota((t, PE_W), 1)
        dff = jnp.where((lane >= FF_LANE) & (lane < FF_LANE + 8), dlf * _sigmoid(-(e_ref[...] + b_ref[...])), 0.0)
        de_ref[...] = dff
        db_ref[...] += _fold8(dff)

    rev = lambda i: (nb - 1 - i, 0)
    return pl.pallas_call(
        body, grid=(nb,),
        in_specs=[pl.BlockSpec((t, PE_W), rev), pl.BlockSpec((1, PE_W), lambda i: (0, 0)), pl.BlockSpec((t, PE_W), rev)],
        out_specs=[pl.BlockSpec((t, PE_W), rev), pl.BlockSpec((8, PE_W), lambda i: (0, 0))],
        out_shape=[jax.ShapeDtypeStruct((s, PE_W), F32), jax.ShapeDtypeStruct((8, PE_W), F32)],
        scratch_shapes=[pltpu.VMEM((8, PE_W), F32)],
        name="fcum_bwd", compiler_params=_cp("arbitrary"))(pe, bias, df)


FOX_KA = 80
FOX_WIDE = 1024


def _split3(x):
    hi = x.astype(BF16)
    r = x - hi.astype(F32)
    mid = r.astype(BF16)
    lo = (r - mid.astype(F32)).astype(BF16)
    return jnp.concatenate([hi, mid, lo], axis=1)


def _fox_tables():
    heads, lane = np.arange(8), np.arange(64)
    spread = np.zeros((512, 1024), np.float32)
    spread[(64 * heads[:, None] + lane).ravel(), (128 * heads[:, None] + lane).ravel()] = 1.0
    def place(src_lane0, dst_off, val):
        t = np.zeros((384, 1024), np.float32)
        for p in range(3):
            t[128 * p + src_lane0 + heads, 128 * heads + dst_off + p] = val
        return t
    def const(off, val):
        c = np.zeros((1, 1024), np.float32)
        for p in range(3):
            c[0, 128 * heads + off + p] = val
        return c
    pick = np.zeros((1024, 128), np.float32)
    pick[128 * heads + 64, FF_LANE + heads] = 1.0
    rows = np.zeros((8, 128), np.float32)
    rows[heads, FF_LANE + heads] = 1.0
    bf = lambda a: jnp.asarray(a, BF16)
    return dict(spread=bf(spread), spread_q=bf(0.125 * spread), compact=bf(spread.T),
                f_to_q=bf(place(FF_LANE, 64, 1.0)), f_to_k=bf(place(FF_LANE, 67, -1.0)), d_to_do=bf(place(0, 64, 1.0)),
                ones_q=jnp.asarray(const(67, 1.0)), ones_k=jnp.asarray(const(64, 1.0)), ones_v=jnp.asarray(const(64, -1.0)),
                pick=bf(pick), rows=jnp.asarray(rows))


def _fox_prep(pm, f128, lse8, tb, *, backward):
    s = pm.shape[0]
    tm = _row_tile(s)

    def body(*refs):
        if backward:
            q_ref, f_ref, lse_ref, spq_ref, fq_ref, cq_ref, rows_ref, qa_ref = refs
            f = f_ref[...] - _dot_tn(lse_ref[...], rows_ref[...], precision=HI)
            qa_ref[...] = (_dot(q_ref[...], spq_ref[...]) + _dot(_split3(f), fq_ref[...]) + cq_ref[...]).astype(BF16)
            return
        (q_ref, k_ref, v_ref, f_ref, spq_ref, sp_ref, fq_ref, fk_ref, cq_ref, ck_ref, cv_ref,
         qa_ref, ka_ref, va_ref, vt_ref, qt_ref) = refs
        f3 = _split3(f_ref[...])
        q, v = q_ref[...], v_ref[...]
        qa_ref[...] = (_dot(q, spq_ref[...]) + _dot(f3, fq_ref[...]) + cq_ref[...]).astype(BF16)
        ka_ref[...] = (_dot(k_ref[...], sp_ref[...]) + _dot(f3, fk_ref[...]) + ck_ref[...]).astype(BF16)
        va_ref[...] = (_dot(v, sp_ref[...]) + cv_ref[...]).astype(BF16)
        vt_ref[...] = v.T
        qt_ref[...] = (q.astype(F32) * 0.125).astype(BF16).T

    row = lambda i: (i, 0)
    const = lambda i: (0, 0)
    blk = lambda c: pl.BlockSpec((tm, 512), lambda i: (i, c // 512))
    wide = pl.BlockSpec((tm, 1024), row)
    mat = lambda a: pl.BlockSpec(a.shape, const)
    if backward:
        ins = [pm, f128, lse8, tb["spread_q"], tb["f_to_q"], tb["ones_q"], tb["rows"]]
        in_specs = [blk(C_FQ), pl.BlockSpec((tm, 128), row), pl.BlockSpec((8, tm), lambda i: (0, i))] + [mat(a) for a in ins[3:]]
        out_specs, out_shape = wide, jax.ShapeDtypeStruct((s, 1024), BF16)
    else:
        ins = [pm, pm, pm, f128, tb["spread_q"], tb["spread"], tb["f_to_q"], tb["f_to_k"], tb["ones_q"], tb["ones_k"], tb["ones_v"]]
        in_specs = [blk(C_FQ), blk(C_FK), blk(C_FV), pl.BlockSpec((tm, 128), row)] + [mat(a) for a in ins[4:]]
        tr = pl.BlockSpec((512, tm), lambda i: (0, i))
        out_specs = [wide, wide, wide, tr, tr]
        out_shape = [jax.ShapeDtypeStruct((s, 1024), BF16)] * 3 + [jax.ShapeDtypeStruct((512, s), BF16)] * 2
    return pl.pallas_call(body, grid=(s // tm,), in_specs=in_specs, out_specs=out_specs, out_shape=out_shape,
                          name="fox_prep_bwd" if backward else "fox_prep", compiler_params=_cp("parallel"))(*ins)


def _fox_post(dq, dkt, dvt, dfcol8, tb):
    s = dq.shape[0]
    tm = _row_tile(s)

    def body(dq_ref, dkt_ref, dvt_ref, dfc_ref, cmp_ref, pick_ref, rows_ref, dfq_ref, dfk_ref, dfv_ref, df_ref):
        d = dq_ref[...]
        dfq_ref[...] = _dot(d.astype(BF16), cmp_ref[...]).astype(BF16)
        d3 = _split3(d)
        pick = pick_ref[...]
        rsum = _dot(d3[:, 0:1024], pick) + _dot(d3[:, 1024:2048], pick) + _dot(d3[:, 2048:3072], pick)
        df_ref[...] = rsum - _dot_tn(dfc_ref[...], rows_ref[...], precision=HI)
        dfk_ref[...] = dkt_ref[...].T
        dfv_ref[...] = dvt_ref[...].T

    row = lambda i: (i, 0)
    const = lambda i: (0, 0)
    tr = pl.BlockSpec((512, tm), lambda i: (0, i))
    out = pl.BlockSpec((tm, 512), row)
    return pl.pallas_call(
        body, grid=(s // tm,),
        in_specs=[pl.BlockSpec((tm, 1024), row), tr, tr, pl.BlockSpec((8, tm), lambda i: (0, i)),
                  pl.BlockSpec((1024, 512), const), pl.BlockSpec((1024, 128), const), pl.BlockSpec((8, 128), const)],
        out_specs=[out, out, out, pl.BlockSpec((tm, 128), row)],
        out_shape=[jax.ShapeDtypeStruct((s, 512), BF16)] * 3 + [jax.ShapeDtypeStruct((s, 128), F32)],
        name="fox_post", compiler_params=_cp("parallel"))(dq, dkt, dvt, dfcol8, tb["compact"], tb["pick"], tb["rows"])


def _fox_fwd(k_aug, q_aug, vt):
    s = k_aug.shape[0]
    nh = 8
    tk = _row_tile(s)
    tq = min(s, FOX_WIDE)
    per = tq // tk

    def body(k_ref, q_ref, v_ref, o_ref, lse_ref, sbuf):
        i = pl.program_id(1)
        qa = q_ref[...]

        def scores(j):
            return _dot_nt(k_ref[pl.ds(pl.multiple_of(j * tk, tk), tk), :], qa)

        def update(st, j, carry):
            m, l, acc = carry
            m2 = jnp.maximum(m, jnp.max(st, axis=0, keepdims=True))
            p = jnp.exp(st - m2)
            a = jnp.exp(m - m2)
            vj = v_ref[:, pl.ds(pl.multiple_of(j * tk, tk), tk)]
            return m2, a * l + jnp.sum(p, axis=0, keepdims=True), a * acc + _dot(vj, p.astype(BF16))

        def step(a, carry):
            sbuf[1] = scores(2 * a + 1)
            carry = update(sbuf[0], 2 * a, carry)
            sbuf[0] = scores(2 * a + 2)
            return update(sbuf[1], 2 * a + 1, carry)

        n = i * per
        sbuf[0] = scores(0)
        carry = (jnp.full((1, tq), -1e30, F32), jnp.zeros((1, tq), F32), jnp.zeros((64, tq), F32))
        carry = lax.fori_loop(0, n // 2, step, carry)
        for r in range(per):
            j = n + r
            st = sbuf[0] if r == 0 else scores(j)
            st = jnp.where(j * tk + _iota((tk, tq), 0) <= i * tq + _iota((tk, tq), 1), st, -1e30)
            carry = update(st, j, carry)
        m, l, acc = carry
        o_ref[...] = (acc / l).astype(BF16)
        lse_ref[0] = m + jnp.log(l)

    return pl.pallas_call(
        body, grid=(nh, s // tq),
        in_specs=[pl.BlockSpec((s, 128), lambda h, i: (0, h)), pl.BlockSpec((tq, 128), lambda h, i: (i, h)),
                  pl.BlockSpec((64, s), lambda h, i: (h, 0))],
        out_specs=[pl.BlockSpec((64, tq), lambda h, i: (h, i)), pl.BlockSpec((1, 1, tq), lambda h, i: (h, 0, i))],
        out_shape=[jax.ShapeDtypeStruct((512, s), BF16), jax.ShapeDtypeStruct((nh, 1, s), F32)],
        scratch_shapes=[pltpu.VMEM((2, tk, tq), F32)],
        name="fox_fwd", compiler_params=_cp("parallel", "arbitrary"))(k_aug, q_aug, vt)


def _fox_bwd(q_aug, do_aug, qt, dot_, k_aug, v_aug):
    s = q_aug.shape[0]
    nh = 8
    tq = _row_tile(s)
    tk = min(s, FOX_WIDE)
    per = tk // tq
    nqb = s // tq

    def body(qa_ref, da_ref, qt_ref, dt_ref, ka_ref, va_ref, dq_ref, dk_ref, dv_ref, dfk_ref):
        j = pl.program_id(1)

        @pl.when(j == 0)
        def _():
            dq_ref[...] = jnp.zeros_like(dq_ref)

        ka, va = ka_ref[...], va_ref[...]
        ks = jnp.where(_iota((tk, 128), 1) < 64, ka.astype(F32) * 0.125, 0.0).astype(BF16)
        lane64 = _iota((tq, 128), 1) == 64

        def tile(i, masked, carry):
            dk, dv, dfk = carry
            rows = pl.ds(pl.multiple_of(i * tq, tq), tq)
            sp = _dot_nt(qa_ref[rows, :], ka)
            if masked:
                sp = jnp.where(i * tq + _iota((tq, tk), 0) >= j * tk + _iota((tq, tk), 1), sp, -1e30)
            p = jnp.exp(sp)
            ds = p * _dot_nt(da_ref[rows, :], va)
            dsb = ds.astype(BF16)
            dq_ref[rows, :] += _dot(dsb, ks) + jnp.where(lane64, jnp.sum(ds, axis=1, keepdims=True), 0.0)
            return (dk + _dot(qt_ref[:, rows], dsb), dv + _dot(dt_ref[:, rows], p.astype(BF16)),
                    dfk + jnp.sum(ds, axis=0, keepdims=True))

        carry = (jnp.zeros((64, tk), F32), jnp.zeros((64, tk), F32), jnp.zeros((1, tk), F32))
        for r in range(per):
            carry = tile(j * per + r, True, carry)
        dk, dv, dfk = lax.fori_loop((j + 1) * per, nqb, lambda i, c: tile(i, False, c), carry)
        dk_ref[...] = dk.astype(BF16)
        dv_ref[...] = dv.astype(BF16)
        dfk_ref[0] = dfk

    head_cols = lambda h, j: (0, h)
    head_rows = lambda h, j: (h, 0)
    once = dict(pipeline_mode=pl.Buffered(1))
    return pl.pallas_call(
        body, grid=(nh, s // tk),
        in_specs=[pl.BlockSpec((s, 128), head_cols, **once), pl.BlockSpec((s, 128), head_cols, **once),
                  pl.BlockSpec((64, s), head_rows, **once), pl.BlockSpec((64, s), head_rows, **once),
                  pl.BlockSpec((tk, 128), lambda h, j: (j, h)), pl.BlockSpec((tk, 128), lambda h, j: (j, h))],
        out_specs=[pl.BlockSpec((s, 128), head_cols), pl.BlockSpec((64, tk), lambda h, j: (h, j)),
                   pl.BlockSpec((64, tk), lambda h, j: (h, j)), pl.BlockSpec((1, 1, tk), lambda h, j: (h, 0, j))],
        out_shape=[jax.ShapeDtypeStruct((s, 1024), F32), jax.ShapeDtypeStruct((512, s), BF16),
                   jax.ShapeDtypeStruct((512, s), BF16), jax.ShapeDtypeStruct((nh, 1, s), F32)],
        name="fox_bwd", compiler_params=_cp("parallel", "arbitrary"))(q_aug, do_aug, qt, dot_, k_aug, v_aug)


MEM_SCALE = 128 ** -0.5


def _mem_attn_fwd(pm, mkv):
    s = pm.shape[0]
    t = _row_tile(s)
    nm = mkv.shape[0]

    def body(q_ref, mk_ref, mv_ref, o_ref):
        for h in range(4):
            cols = slice(128 * h, 128 * (h + 1))
            sc = _dot_nt(q_ref[:, cols], mk_ref[:, cols]) * MEM_SCALE
            p = jnp.exp(sc - jnp.max(sc, axis=-1, keepdims=True))
            p = p / jnp.sum(p, axis=-1, keepdims=True)
            o_ref[:, cols] = _dot(p.astype(BF16), mv_ref[:, cols]).astype(BF16)

    return pl.pallas_call(
        body, grid=(s // t,),
        in_specs=[pl.BlockSpec((t, 512), lambda i: (i, C_MQ // 512)), pl.BlockSpec((nm, 512), lambda i: (0, 0)),
                  pl.BlockSpec((nm, 512), lambda i: (0, 1))],
        out_specs=pl.BlockSpec((t, 512), lambda i: (i, 0)),
        out_shape=jax.ShapeDtypeStruct((s, 512), BF16),
        name="mem_attn_fwd", compiler_params=_cp("parallel"))(pm, mkv, mkv)


def _mem_attn_bwd(pm, mkv, do):
    s = pm.shape[0]
    t = _row_tile(s)
    nm = mkv.shape[0]

    def body(q_ref, mk_ref, mv_ref, do_ref, dq_ref, dmk_ref, dmv_ref):
        @pl.when(pl.program_id(0) == 0)
        def _():
            dmk_ref[...] = jnp.zeros_like(dmk_ref)
            dmv_ref[...] = jnp.zeros_like(dmv_ref)

        for h in range(4):
            cols = slice(128 * h, 128 * (h + 1))
            qh, kh, vh, doh = q_ref[:, cols], mk_ref[:, cols], mv_ref[:, cols], do_ref[:, cols]
            sc = _dot_nt(qh, kh) * MEM_SCALE
            p = jnp.exp(sc - jnp.max(sc, axis=-1, keepdims=True))
            p = p / jnp.sum(p, axis=-1, keepdims=True)
            pb = p.astype(BF16)
            dp = _dot_nt(doh, vh)
            ds = (p * (dp - jnp.sum(p * dp, axis=-1, keepdims=True)) * MEM_SCALE).astype(BF16)
            dq_ref[:, cols] = _dot(ds, kh).astype(BF16)
            dmk_ref[:, cols] += _dot_tn(ds, qh)
            dmv_ref[:, cols] += _dot_tn(pb, doh)

    return pl.pallas_call(
        body, grid=(s // t,),
        in_specs=[pl.BlockSpec((t, 512), lambda i: (i, C_MQ // 512)), pl.BlockSpec((nm, 512), lambda i: (0, 0)),
                  pl.BlockSpec((nm, 512), lambda i: (0, 1)), pl.BlockSpec((t, 512), lambda i: (i, 0))],
        out_specs=[pl.BlockSpec((t, 512), lambda i: (i, 0)), pl.BlockSpec((nm, 512), lambda i: (0, 0)),
                   pl.BlockSpec((nm, 512), lambda i: (0, 0))],
        out_shape=[jax.ShapeDtypeStruct((s, 512), BF16), jax.ShapeDtypeStruct((nm, 512), F32),
                   jax.ShapeDtypeStruct((nm, 512), F32)],
        name="mem_attn_bwd", compiler_params=_cp("arbitrary"))(pm, mkv, mkv, do)


def _gain_grad(dxn_g, x, r, name):
    m, d = x.shape

    def body(d_ref, x_ref, r_ref, o_ref):
        o_ref[...] = _fold8(d_ref[...] * (x_ref[...] * r_ref[...]))

    return pl.pallas_call(body, out_shape=jax.ShapeDtypeStruct((8, d), F32), name=name,
                          compiler_params=pltpu.CompilerParams(vmem_limit_bytes=VMEM_LIMIT_BYTES))(dxn_g, x, r)


def _head_norm(o, gh):
    xs, rs = [], []
    for h in range(4):
        oh = o[:, 128 * h:128 * (h + 1)]
        r = lax.rsqrt(jnp.mean(oh * oh, axis=-1, keepdims=True) + EPS)
        xs.append(oh * r)
        rs.append(r)
    return xs, rs


def _merge_fwd(x, pm, o_gla, o_fox_t, o_mem, g_head, wg, wf, wm, wo, g_ffn):
    s = x.shape[0]
    t = min(s, 256)

    def body(x_ref, g0_ref, g1_ref, g2_ref, gg_ref, og_ref, of_ref, om_ref, gh_ref, wg_ref, wf_ref, wm_ref, wo_ref, gf_ref,
             mg_ref, h1_ref, u2_ref, r2_ref):
        xs, _ = _head_norm(og_ref[...], None)
        gg = gg_ref[...].astype(F32)
        sil = gg * _sigmoid(gg)
        ogn = jnp.concatenate(xs, axis=1) * gh_ref[...] * sil
        merged = (_sigmoid(g0_ref[...].astype(F32)) * _dot(ogn.astype(BF16), wg_ref[...])
                  + _sigmoid(g1_ref[...].astype(F32)) * _dot(of_ref[...].T, wf_ref[...])
                  + _sigmoid(g2_ref[...].astype(F32)) * _dot(om_ref[...], wm_ref[...]))
        mb = merged.astype(BF16)
        mg_ref[...] = mb
        h1 = x_ref[...] + _dot(mb, wo_ref[...])
        h1_ref[...] = h1
        r = lax.rsqrt(jnp.mean(h1 * h1, axis=-1, keepdims=True) + EPS)
        u2_ref[...] = ((h1 * r) * gf_ref[...]).astype(BF16)
        r2_ref[...] = r

    row = lambda i: (i, 0)
    const = lambda i: (0, 0)
    return pl.pallas_call(
        body, grid=(s // t,),
        in_specs=[pl.BlockSpec((t, D), row), pl.BlockSpec((t, D), lambda i: (i, 0)), pl.BlockSpec((t, D), lambda i: (i, 1)),
                  pl.BlockSpec((t, D), lambda i: (i, 2)), pl.BlockSpec((t, 512), lambda i: (i, C_GG // 512)),
                  pl.BlockSpec((t, 512), row), pl.BlockSpec((512, t), lambda i: (0, i)), pl.BlockSpec((t, 512), row),
                  pl.BlockSpec((1, 512), const), pl.BlockSpec((512, D), const), pl.BlockSpec((512, D), const),
                  pl.BlockSpec((512, D), const), pl.BlockSpec((D, D), const), pl.BlockSpec((1, D), const)],
        out_specs=[pl.BlockSpec((t, D), row), pl.BlockSpec((t, D), row), pl.BlockSpec((t, D), row), pl.BlockSpec((t, 1), row)],
        out_shape=[jax.ShapeDtypeStruct((s, D), BF16), jax.ShapeDtypeStruct((s, D), F32),
                   jax.ShapeDtypeStruct((s, D), BF16), jax.ShapeDtypeStruct((s, 1), F32)],
        name="merge_fwd", compiler_params=_cp("parallel"))(x, pm, pm, pm, pm, o_gla, o_fox_t, o_mem, g_head, wg, wf, wm, wo, g_ffn)


def _merge_bwd(dh1b, pm, o_gla, o_fox_t, o_mem, g_head, wg, wf, wm, wgt, wft, wmt, wot, spread, d_to_do):
    s = dh1b.shape[0]
    t = min(s, 256)

    def body(dh_ref, g0_ref, g1_ref, g2_ref, gg_ref, og_ref, of_ref, om_ref, gh_ref, wg_ref, wf_ref, wm_ref,
             wgt_ref, wft_ref, wmt_ref, wot_ref, sp_ref, dd_ref,
             dgt_ref, dgg_ref, dog_ref, da_ref, dot_ref, dom_ref, dwg_ref, dwf_ref, dwm_ref, dgh_ref):
        @pl.when(pl.program_id(0) == 0)
        def _():
            dwg_ref[...] = jnp.zeros_like(dwg_ref)
            dwf_ref[...] = jnp.zeros_like(dwf_ref)
            dwm_ref[...] = jnp.zeros_like(dwm_ref)
            dgh_ref[...] = jnp.zeros_like(dgh_ref)

        dmerged = _dot(dh_ref[...], wot_ref[...])
        og = og_ref[...]
        xs, rs = _head_norm(og, None)
        on = jnp.concatenate(xs, axis=1)
        gg = gg_ref[...].astype(F32)
        sg = _sigmoid(gg)
        sil = gg * sg
        gh = gh_ref[...]
        ognb = (on * gh * sil).astype(BF16)
        ofb, omb = of_ref[...].T, om_ref[...]
        douts = []
        for idx, (gref, ob, w_ref, wt_ref, dw_ref) in enumerate((
                (g0_ref, ognb, wg_ref, wgt_ref, dwg_ref), (g1_ref, ofb, wf_ref, wft_ref, dwf_ref),
                (g2_ref, omb, wm_ref, wmt_ref, dwm_ref))):
            gt = _sigmoid(gref[...].astype(F32))
            y = _dot(ob, w_ref[...])
            dgt_ref[:, D * idx:D * (idx + 1)] = (dmerged * y * gt * (1.0 - gt)).astype(BF16)
            dy = (gt * dmerged).astype(BF16)
            dw_ref[...] += _dot_tn(ob, dy)
            douts.append(_dot(dy, wt_ref[...]))
        dogn, dof, dom = douts
        dofb = dof.astype(BF16)
        dom_ref[...] = dom.astype(BF16)
        ind = (_iota((512, 128), 0) // 64 == _iota((512, 128), 1)).astype(F32)
        delta = _dot(dofb.astype(F32) * ofb.astype(F32), ind, precision=HI)
        da_ref[...] = (_dot(dofb, sp_ref[...]) + _dot(_split3(delta), dd_ref[...])).astype(BF16)
        dot_ref[...] = dofb.T
        dgg_ref[...] = (dogn * on * gh * (sg * (1.0 + gg * (1.0 - sg)))).astype(BF16)
        d_on = dogn * sil
        dgh_ref[...] += _fold8(d_on * on)
        dxn = d_on * gh
        outs = []
        for h in range(4):
            cols = slice(128 * h, 128 * (h + 1))
            dh_, xh = dxn[:, cols], xs[h]
            outs.append(rs[h] * (dh_ - xh * jnp.mean(dh_ * xh, axis=-1, keepdims=True)))
        dog_ref[...] = jnp.concatenate(outs, axis=1).astype(BF16)

    row = lambda i: (i, 0)
    const = lambda i: (0, 0)
    return pl.pallas_call(
        body, grid=(s // t,),
        in_specs=[pl.BlockSpec((t, D), row), pl.BlockSpec((t, D), lambda i: (i, 0)), pl.BlockSpec((t, D), lambda i: (i, 1)),
                  pl.BlockSpec((t, D), lambda i: (i, 2)), pl.BlockSpec((t, 512), lambda i: (i, C_GG // 512)),
                  pl.BlockSpec((t, 512), row), pl.BlockSpec((512, t), lambda i: (0, i)), pl.BlockSpec((t, 512), row),
                  pl.BlockSpec((1, 512), const), pl.BlockSpec((512, D), const), pl.BlockSpec((512, D), const),
                  pl.BlockSpec((512, D), const), pl.BlockSpec((D, 512), const), pl.BlockSpec((D, 512), const),
                  pl.BlockSpec((D, 512), const), pl.BlockSpec((D, D), const),
                  pl.BlockSpec((512, 1024), const), pl.BlockSpec((384, 1024), const)],
        out_specs=[pl.BlockSpec((t, 3 * D), row), pl.BlockSpec((t, 512), row), pl.BlockSpec((t, 512), row),
                   pl.BlockSpec((t, 1024), row), pl.BlockSpec((512, t), lambda i: (0, i)), pl.BlockSpec((t, 512), row),
                   pl.BlockSpec((512, D), const), pl.BlockSpec((512, D), const), pl.BlockSpec((512, D), const),
                   pl.BlockSpec((8, 512), const)],
        out_shape=[jax.ShapeDtypeStruct((s, 3 * D), BF16), jax.ShapeDtypeStruct((s, 512), BF16),
                   jax.ShapeDtypeStruct((s, 512), BF16), jax.ShapeDtypeStruct((s, 1024), BF16),
                   jax.ShapeDtypeStruct((512, s), BF16), jax.ShapeDtypeStruct((s, 512), BF16),
                   jax.ShapeDtypeStruct((512, D), F32), jax.ShapeDtypeStruct((512, D), F32),
                   jax.ShapeDtypeStruct((512, D), F32), jax.ShapeDtypeStruct((8, 512), F32)],
        name="merge_bwd", compiler_params=_cp("arbitrary"))(
            dh1b, pm, pm, pm, pm, o_gla, o_fox_t, o_mem, g_head, wg, wf, wm, wgt, wft, wmt, wot, spread, d_to_do)


def _ff2_loss(a, w2, h1, g_final, target):
    s, k = a.shape
    tm = min(s, 256)
    tk = 1024
    nk = k // tk

    def body(a_ref, w_ref, h1_ref, g_ref, t_ref, dh_ref, dhb_ref, loss_ref, dg_ref, acc):
        i, kk = pl.program_id(0), pl.program_id(1)

        @pl.when(kk == 0)
        def _():
            acc[...] = jnp.zeros_like(acc)

        @pl.when((i == 0) & (kk == 0))
        def _():
            loss_ref[...] = jnp.zeros_like(loss_ref)
            dg_ref[...] = jnp.zeros_like(dg_ref)

        acc[...] += _dot(_relu2_bf16(a_ref[...]), w_ref[...])

        @pl.when(kk == nk - 1)
        def _():
            h2 = h1_ref[...] + acc[...]
            r = lax.rsqrt(jnp.mean(h2 * h2, axis=-1, keepdims=True) + EPS)
            xn = h2 * r
            g = g_ref[...]
            err = xn * g - t_ref[...]
            e2 = _fold8(err * err)
            part = e2[:, 0:128]
            for c in range(1, D // 128):
                part = part + e2[:, 128 * c:128 * (c + 1)]
            loss_ref[...] += part
            dy = err * (1.0 / D)
            dg_ref[...] += _fold8(dy * xn)
            dxn = dy * g
            dh = r * (dxn - xn * jnp.mean(dxn * xn, axis=-1, keepdims=True))
            dh_ref[...] = dh
            dhb_ref[...] = dh.astype(BF16)

    row = lambda i, kk: (i, 0)
    return pl.pallas_call(
        body, grid=(s // tm, nk),
        in_specs=[pl.BlockSpec((tm, tk), lambda i, kk: (i, kk)), pl.BlockSpec((tk, D), lambda i, kk: (kk, 0)),
                  pl.BlockSpec((tm, D), row), pl.BlockSpec((1, D), lambda i, kk: (0, 0)), pl.BlockSpec((tm, D), row)],
        out_specs=[pl.BlockSpec((tm, D), row), pl.BlockSpec((tm, D), row), pl.BlockSpec((8, 128), lambda i, kk: (0, 0)),
                   pl.BlockSpec((8, D), lambda i, kk: (0, 0))],
        out_shape=[jax.ShapeDtypeStruct((s, D), F32), jax.ShapeDtypeStruct((s, D), BF16),
                   jax.ShapeDtypeStruct((8, 128), F32), jax.ShapeDtypeStruct((8, D), F32)],
        scratch_shapes=[pltpu.VMEM((tm, D), F32)],
        name="ff2_loss", compiler_params=_cp("arbitrary", "arbitrary"))(a, w2, h1, g_final, target)


def _adam(w, g, m, v, name):
    r, c = w.shape
    tr = r
    for cand in (512, 256, 128, 64, 32, 16, 8):
        if r % cand == 0 and cand * c * 4 <= (1 << 20):
            tr = cand
            break
    c1 = 1.0 - ADAM_B1 ** ADAM_STEP
    c2 = 1.0 - ADAM_B2 ** ADAM_STEP

    def body(w_ref, g_ref, m_ref, v_ref, d_ref, nm_ref, nv_ref):
        gv = g_ref[...]
        nm = ADAM_B1 * m_ref[...] + (1.0 - ADAM_B1) * gv
        nv = ADAM_B2 * v_ref[...] + (1.0 - ADAM_B2) * (gv * gv)
        d_ref[...] = -ADAM_LR * ((nm / c1) / (jnp.sqrt(nv / c2) + ADAM_EPS) + ADAM_WD * w_ref[...])
        nm_ref[...] = nm
        nv_ref[...] = nv

    spec = pl.BlockSpec((tr, c), lambda i: (i, 0))
    return pl.pallas_call(
        body, grid=(r // tr,), in_specs=[spec] * 4, out_specs=[spec] * 3,
        out_shape=[jax.ShapeDtypeStruct((r, c), F32)] * 3, name=name, compiler_params=_cp("parallel"))(w, g, m, v)


def _add2(a, b, name):
    n, r, c = a.shape
    tr = 464 if r % 464 == 0 else r

    def body(a_ref, b_ref, o_ref):
        o_ref[...] = a_ref[...] + b_ref[...]

    spec = pl.BlockSpec((1, tr, c), lambda k, i: (k, i, 0))
    return pl.pallas_call(body, grid=(n, r // tr), in_specs=[spec, spec], out_specs=spec,
                          out_shape=jax.ShapeDtypeStruct((n, r, c), F32), name=name,
                          compiler_params=_cp("parallel", "parallel"))(a, b)


def _sum4(a, name):
    _, r, c = a.shape
    tr = 464 if r % 464 == 0 else r

    def body(a_ref, o_ref):
        o_ref[...] = ((a_ref[0] + a_ref[1]) + a_ref[2]) + a_ref[3]

    return pl.pallas_call(body, grid=(r // tr,), in_specs=[pl.BlockSpec((4, tr, c), lambda i: (0, i, 0))],
                          out_specs=pl.BlockSpec((tr, c), lambda i: (i, 0)),
                          out_shape=jax.ShapeDtypeStruct((r, c), F32), name=name, compiler_params=_cp("parallel"))(a)


def _adam_small(w, gathered, m, v):
    c1 = 1.0 - ADAM_B1 ** ADAM_STEP
    c2 = 1.0 - ADAM_B2 ** ADAM_STEP

    def body(w_ref, g_ref, m_ref, v_ref, gs_ref, d_ref, nm_ref, nv_ref):
        gv = g_ref[0]
        for dev in range(1, N_DEV):
            gv = gv + g_ref[dev]
        gs_ref[...] = gv
        nm = ADAM_B1 * m_ref[...] + (1.0 - ADAM_B1) * gv
        nv = ADAM_B2 * v_ref[...] + (1.0 - ADAM_B2) * (gv * gv)
        d_ref[...] = -ADAM_LR * ((nm / c1) / (jnp.sqrt(nv / c2) + ADAM_EPS) + ADAM_WD * w_ref[...])
        nm_ref[...] = nm
        nv_ref[...] = nv

    return pl.pallas_call(body, out_shape=[jax.ShapeDtypeStruct((8, D), F32)] * 4, name="adam_small")(w, gathered, m, v)


def _place():
    return lax.axis_index("x"), lax.axis_index("y"), lax.axis_index("c")


def _other_chips(x, y):
    return [(1 - x, y), (x, 1 - y), (1 - x, 1 - y)]


def _gather_shards(p):
    r, c = p.shape
    hr = r // 2

    def body(p_ref, out_ref, send_sems, recv_sems, local_sem):
        x, y, cc = _place()
        sibling = (x, y, 1 - cc)
        chips = _other_chips(x, y)

        def half(chip, core):
            return out_ref.at[2 * chip[0] + chip[1], pl.ds(core * hr, hr), :]

        def copy(k, chip, core, to, src=None):
            return pltpu.make_async_remote_copy(
                src_ref=half(chip, core) if src is None else src, dst_ref=half(chip, core),
                send_sem=send_sems.at[k], recv_sem=recv_sems.at[k], device_id=to, device_id_type=MESH)

        mine = pltpu.make_async_copy(p_ref, out_ref.at[2 * x + y], local_sem)
        mine.start()
        my_half = p_ref.at[pl.ds(cc * hr, hr), :]
        first = [copy(j, (x, y), cc, (*chip, cc), src=my_half) for j, chip in enumerate(chips)]
        for cp in first:
            cp.start()
        passed = [copy(3 + j, chip, cc, sibling) for j, chip in enumerate(chips)]
        for j, chip in enumerate(chips):
            copy(j, chip, cc, (x, y, cc)).wait_recv()
            passed[j].start()
        for j, chip in enumerate(chips):
            copy(3 + j, chip, 1 - cc, (x, y, cc)).wait_recv()
        for cp in first + passed:
            cp.wait_send()
        mine.wait()

    any_spec = pl.BlockSpec(memory_space=pl.ANY)
    return pl.pallas_call(
        body, out_shape=jax.ShapeDtypeStruct((N_CHIPS, r, c), p.dtype), in_specs=[any_spec], out_specs=any_spec,
        scratch_shapes=[pltpu.SemaphoreType.DMA((6,)), pltpu.SemaphoreType.DMA((6,)), pltpu.SemaphoreType.DMA],
        name="gather_shards")(p)


def _swap_halves(g):
    n, r, c = g.shape
    hr = r // 2

    def body(g_ref, out_ref, send_sem, recv_sem):
        x, y, cc = _place()
        cp = pltpu.make_async_remote_copy(
            src_ref=g_ref.at[:, pl.ds((1 - cc) * hr, hr), :], dst_ref=out_ref,
            send_sem=send_sem, recv_sem=recv_sem, device_id=(x, y, 1 - cc), device_id_type=MESH)
        cp.start()
        cp.wait()

    any_spec = pl.BlockSpec(memory_space=pl.ANY)
    return pl.pallas_call(
        body, out_shape=jax.ShapeDtypeStruct((n, hr, c), g.dtype), in_specs=[any_spec], out_specs=any_spec,
        scratch_shapes=[pltpu.SemaphoreType.DMA, pltpu.SemaphoreType.DMA], name="swap_halves")(g)


def _scatter_partials(p):
    n, hr, c = p.shape

    def body(p_ref, out_ref, send_sems, recv_sems, local_sem):
        x, y, cc = _place()
        me = 2 * x + y
        chips = _other_chips(x, y)
        mine = pltpu.make_async_copy(p_ref.at[me], out_ref.at[me], local_sem)
        mine.start()
        sends = []
        for j, chip in enumerate(chips):
            sends.append(pltpu.make_async_remote_copy(
                src_ref=p_ref.at[2 * chip[0] + chip[1]], dst_ref=out_ref.at[me],
                send_sem=send_sems.at[j], recv_sem=recv_sems.at[j], device_id=(*chip, cc), device_id_type=MESH))
        for cp in sends:
            cp.start()
        for j, chip in enumerate(chips):
            src = 2 * chip[0] + chip[1]
            pltpu.make_async_remote_copy(
                src_ref=p_ref.at[me], dst_ref=out_ref.at[src], send_sem=send_sems.at[j], recv_sem=recv_sems.at[j],
                device_id=(*chip, cc), device_id_type=MESH).wait_recv()
        for cp in sends:
            cp.wait_send()
        mine.wait()

    any_spec = pl.BlockSpec(memory_space=pl.ANY)
    return pl.pallas_call(
        body, out_shape=jax.ShapeDtypeStruct((n, hr, c), p.dtype), in_specs=[any_spec], out_specs=any_spec,
        scratch_shapes=[pltpu.SemaphoreType.DMA((3,)), pltpu.SemaphoreType.DMA((3,)), pltpu.SemaphoreType.DMA],
        name="scatter_partials")(p)


def _join_halves(f):
    hr, c = f.shape

    def body(f_ref, out_ref, send_sem, recv_sem, local_sem):
        x, y, cc = _place()
        mine = pltpu.make_async_copy(f_ref, out_ref.at[pl.ds(cc * hr, hr), :], local_sem)
        mine.start()
        cp = pltpu.make_async_remote_copy(
            src_ref=f_ref, dst_ref=out_ref.at[pl.ds(cc * hr, hr), :],
            send_sem=send_sem, recv_sem=recv_sem, device_id=(x, y, 1 - cc), device_id_type=MESH)
        cp.start()
        pltpu.make_async_remote_copy(
            src_ref=f_ref, dst_ref=out_ref.at[pl.ds((1 - cc) * hr, hr), :],
            send_sem=send_sem, recv_sem=recv_sem, device_id=(x, y, 1 - cc), device_id_type=MESH).wait_recv()
        cp.wait_send()
        mine.wait()

    any_spec = pl.BlockSpec(memory_space=pl.ANY)
    return pl.pallas_call(
        body, out_shape=jax.ShapeDtypeStruct((2 * hr, c), f.dtype), in_specs=[any_spec], out_specs=any_spec,
        scratch_shapes=[pltpu.SemaphoreType.DMA, pltpu.SemaphoreType.DMA, pltpu.SemaphoreType.DMA],
        name="join_halves")(f)


def _gather_small(blk):
    m, n = blk.shape

    def body(x_ref, out_ref, send_sems, recv_sems, local_sem):
        x, y, cc = _place()
        me, sibling = (x, y, cc), (x, y, 1 - cc)
        chips = _other_chips(x, y)

        def slot(px, py, pc):
            return out_ref.at[4 * px + 2 * py + pc]

        def copy(k, block, to, src=None):
            return pltpu.make_async_remote_copy(
                src_ref=slot(*block) if src is None else src, dst_ref=slot(*block),
                send_sem=send_sems.at[k], recv_sem=recv_sems.at[k], device_id=to, device_id_type=MESH)

        mine = pltpu.make_async_copy(x_ref, slot(*me), local_sem)
        mine.start()
        first = [copy(0, me, sibling, src=x_ref)]
        first += [copy(1 + j, me, (*chip, cc), src=x_ref) for j, chip in enumerate(chips)]
        for cp in first:
            cp.start()
        passed = [copy(4 + j, (*chip, cc), sibling) for j, chip in enumerate(chips)]
        for j, chip in enumerate(chips):
            copy(1 + j, (*chip, cc), me).wait_recv()
            passed[j].start()
        copy(0, sibling, me).wait_recv()
        for j, chip in enumerate(chips):
            copy(4 + j, (*chip, 1 - cc), me).wait_recv()
        for cp in first + passed:
            cp.wait_send()
        mine.wait()

    vmem = pl.BlockSpec(memory_space=pltpu.VMEM)
    return pl.pallas_call(
        body, out_shape=jax.ShapeDtypeStruct((N_DEV, m, n), blk.dtype), in_specs=[vmem], out_specs=vmem,
        scratch_shapes=[pltpu.SemaphoreType.DMA((7,)), pltpu.SemaphoreType.DMA((7,)), pltpu.SemaphoreType.DMA],
        name="gather_small")(blk)


def _shard_shape(name, full_shape):
    shp = list(full_shape)
    shp[SHARD_AXIS[name]] //= N_CHIPS
    return tuple(shp)


FULL_SHAPES = {"w_in": (D, 6680), "w_alpha_up": (16, 256), "w_mem_kv": (D, D), "w_gla_o": (512, D), "w_fox_o": (512, D),
               "w_mem_o": (512, D), "w_out": (D, D), "w_ff1": (D, 4 * D), "w_ff2": (4 * D, D)}


def _pack(shards, dtype):
    flat = [shards[n].astype(dtype).reshape(-1) for n in WEIGHTS]
    total = sum(f.shape[0] for f in flat)
    flat.append(jnp.zeros((PACK_ROWS * PACK_W - total,), dtype))
    return jnp.concatenate(flat).reshape(PACK_ROWS, PACK_W)


def _unpack(packed):
    flat = packed.reshape(-1)
    out, off = {}, 0
    for n in WEIGHTS:
        shp = _shard_shape(n, FULL_SHAPES[n])
        size = shp[0] * shp[1]
        out[n] = flat[off:off + size].reshape(shp)
        off += size
    return out


def _split_shards(name, full):
    return jnp.split(full, N_CHIPS, axis=SHARD_AXIS[name])


def _pack_small(vals, scalar=None):
    row4 = jnp.concatenate([vals["b_alpha"].reshape(-1), vals["b_forget"].reshape(-1), jnp.zeros((D - 264,), F32)])
    row5 = jnp.concatenate([vals["g_gla_head"].reshape(-1), jnp.zeros((D - 512,), F32)])
    row6 = jnp.zeros((D,), F32) if scalar is None else jnp.broadcast_to(scalar, (D,))
    rows = [vals["g_mix"].reshape(-1), vals["g_mem"].reshape(-1), vals["g_ffn"].reshape(-1), vals["g_final"].reshape(-1),
            row4, row5, row6, jnp.zeros((D,), F32)]
    return jnp.stack(rows)


def _unpack_small(blk):
    return {"g_mix": blk[0].reshape(1, D), "g_mem": blk[1].reshape(1, D), "g_ffn": blk[2].reshape(1, D),
            "g_final": blk[3].reshape(D), "b_alpha": blk[4, 0:256].reshape(1, 256), "b_forget": blk[4, 256:264].reshape(1, 8),
            "g_gla_head": blk[5, 0:512].reshape(1, 4, 128)}


def _local_step(x, mem, target, wb, small):
    s = x.shape[0]
    nm = mem.shape[0]
    t = _row_tile(s)
    nb = s // t
    w_in = wb["w_in"]
    w_main = jnp.concatenate([w_in[:, 3608:6680], w_in[:, 0:1536], w_in[:, 1552:3088], w_in[:, 3096:3608]], axis=1)
    w_e = jnp.concatenate([w_in[:, 1536:1552], w_in[:, 3088:3096], jnp.zeros((D, PE_W - 24), BF16)], axis=1)
    w_in_pt = jnp.concatenate([w_main, w_e, jnp.zeros((D, DP_W - PM_W - PE_W), BF16)], axis=1).T
    wau_p = jnp.concatenate([wb["w_alpha_up"], jnp.zeros((PE_W - 16, 256), BF16)], axis=0)
    b_alpha = small["b_alpha"].reshape(1, 256)
    bias_e = jnp.concatenate([jnp.zeros((FF_LANE,), F32), small["b_forget"].reshape(-1),
                              jnp.zeros((PE_W - FF_LANE - 8,), F32)]).reshape(1, PE_W)
    g_mix, g_mem, g_ffn = small["g_mix"].reshape(1, D), small["g_mem"].reshape(1, D), small["g_ffn"].reshape(1, D)
    g_final = small["g_final"].reshape(1, D)
    g_head = small["g_gla_head"].reshape(1, 512)

    u, r1 = _rms_fwd(x, g_mix, "norm_mix")
    pm = _mm_nn(u, w_main, out_dtype=BF16, tm=t, tn=512, tk=D, name="proj_main")
    pe = _mm_nn(u, w_e, out_dtype=F32, tm=t, tn=PE_W, tk=D, name="proj_narrow")
    o_gla, states = _gla_fwd(pm, pe, wau_p, b_alpha)
    fcum = _fcum_fwd(pe, bias_e)
    tb = _fox_tables()
    qf_aug, k_aug, v_aug, vt, qt = _fox_prep(pm, fcum, None, tb, backward=False)
    o_fox, lse = _fox_fwd(k_aug, qf_aug, vt)
    mn, rm = _rms_fwd(mem, g_mem, "norm_mem")
    mkv = _mm_nn(mn, wb["w_mem_kv"], out_dtype=BF16, tm=nm, tn=512, tk=D, name="mem_kv")
    o_mem = _mem_attn_fwd(pm, mkv)
    merged, h1, u2, r2 = _merge_fwd(x, pm, o_gla, o_fox, o_mem, g_head, wb["w_gla_o"], wb["w_fox_o"], wb["w_mem_o"],
                                    wb["w_out"], g_ffn)
    a = _mm_nn(u2, wb["w_ff1"], out_dtype=BF16, tm=t, tn=1024, tk=D, name="ff1")
    dh2, dh2b, loss8, dgfin8 = _ff2_loss(a, wb["w_ff2"], h1, g_final, target)
    loss = 0.5 * jnp.sum(loss8) / D

    da = _mm_nn(dh2b, wb["w_ff2"].T, out_dtype=BF16, tm=t, tn=1024, tk=D, name="d_act",
                epi=lambda acc, at: acc * (2.0 * jnp.maximum(at.astype(F32), 0.0)), extra=a)
    gw = {}
    gw["w_ff2"] = _mm_tn(a, dh2b, tm=1024, tn=D, ts=t, name="dw_ff2", a_fn=_relu2_bf16)
    gw["w_ff1"] = _mm_tn(u2, da, tm=D, tn=1024, ts=t, name="dw_ff1")
    dh1, dh1b, dgffn8 = _mm_norm_bwd(da, wb["w_ff1"].T, h1, r2, g_ffn, dh2, tk=1024, name="d_h1", want_bf16=True)
    gw["w_out"] = _mm_tn(merged, dh1b, tm=D, tn=D, ts=t, name="dw_out")
    (dgates, dgg, do_gla, do_aug, do_t, do_mem, gw["w_gla_o"], gw["w_fox_o"], gw["w_mem_o"], dgh8) = _merge_bwd(
        dh1b, pm, o_gla, o_fox, o_mem, g_head, wb["w_gla_o"], wb["w_fox_o"], wb["w_mem_o"],
        wb["w_gla_o"].T, wb["w_fox_o"].T, wb["w_mem_o"].T, wb["w_out"].T, tb["spread"], tb["d_to_do"])
    dgq, dgk, dgv, de_gla, dwau_p, dba8 = _gla_bwd(pm, pe, wau_p, wau_p.T, b_alpha, do_gla, states)
    gw["w_alpha_up"] = dwau_p[0:16, :]
    q_aug = _fox_prep(pm, fcum, lse.reshape(8, s), tb, backward=True)
    dq_aug, dfk_t, dfv_t, dfcol = _fox_bwd(q_aug, do_aug, qt, do_t, k_aug, v_aug)
    dfq, dfk, dfv, df = _fox_post(dq_aug, dfk_t, dfv_t, dfcol.reshape(8, s), tb)
    de_fox, dbf8 = _fcum_bwd(pe, bias_e, df)
    dmq, dmk, dmv = _mem_attn_bwd(pm, mkv, do_mem)
    dmkv = jnp.concatenate([dmk, dmv], axis=1).astype(BF16)
    gw["w_mem_kv"] = _mm_tn(mn, dmkv, tm=D, tn=D, ts=nm, name="dw_mem_kv")
    dmn_g = _mm_nn(dmkv, wb["w_mem_kv"].T, out_dtype=F32, tm=nm, tn=D, tk=D, name="d_mem_norm")
    dgmem8 = _gain_grad(dmn_g, mem, rm, "dg_mem")
    dproj = jnp.concatenate(
        [dgates, dgq, dgk, dgv, dgg, dfq, dfk, dfv, dmq,
         (de_gla + de_fox).astype(BF16), jnp.zeros((s, DP_W - PM_W - PE_W), BF16)], axis=1)
    dwp = _mm_tn(u, dproj, tm=D, tn=1024, ts=t, name="dw_in")
    gw["w_in"] = jnp.concatenate([dwp[:, 3072:4608], dwp[:, PM_W:PM_W + 16], dwp[:, 4608:6144],
                                  dwp[:, PM_W + 16:PM_W + 24], dwp[:, 6144:6656], dwp[:, 0:3072]], axis=1)
    grad_x, dgmix8 = _mm_norm_bwd(dproj, w_in_pt, x, r1, g_mix, dh1, tk=1024, name="d_x", want_bf16=False)
    gs = {"g_mix": dgmix8.sum(0), "g_mem": dgmem8.sum(0), "g_ffn": dgffn8.sum(0), "g_final": dgfin8.sum(0),
          "b_alpha": dba8.sum(0), "b_forget": dbf8.sum(0)[FF_LANE:FF_LANE + 8], "g_gla_head": dgh8.sum(0)}
    return loss, grad_x, gw, gs


def kernel(x, mem, g_mix, w_in, w_alpha_up, b_alpha, b_forget, g_gla_head, g_mem, w_mem_kv, w_gla_o, w_fox_o, w_mem_o, w_out, g_ffn, w_ff1, w_ff2, g_final, loss_target, m_g_mix, m_w_in, m_w_alpha_up, m_b_alpha, m_b_forget, m_g_gla_head, m_g_mem, m_w_mem_kv, m_w_gla_o, m_w_fox_o, m_w_mem_o, m_w_out, m_g_ffn, m_w_ff1, m_w_ff2, m_g_final, v_g_mix, v_w_in, v_w_alpha_up, v_b_alpha, v_b_forget, v_g_gla_head, v_g_mem, v_w_mem_kv, v_w_gla_o, v_w_fox_o, v_w_mem_o, v_w_out, v_g_ffn, v_w_ff1, v_w_ff2, v_g_final):
    args = dict(locals())
    w_sh = {n: args[n][0] for n in WEIGHTS}
    m_sh = {n: args["m_" + n][0] for n in WEIGHTS}
    v_sh = {n: args["v_" + n][0] for n in WEIGHTS}
    small = {n: args[n] for n in SMALL}

    gathered = _gather_shards(_pack(w_sh, BF16))
    parts = [_unpack(gathered[k]) for k in range(N_CHIPS)]
    wb = {n: jnp.concatenate([parts[k][n] for k in range(N_CHIPS)], axis=SHARD_AXIS[n]) for n in WEIGHTS}

    loss, grad_x, gw, gs = _local_step(x[0], mem[0], loss_target[0], wb, small)

    g_by_chip = [_split_shards(n, gw[n]) for n in WEIGHTS]
    g_packed = jnp.stack([_pack({n: g_by_chip[i][k] for i, n in enumerate(WEIGHTS)}, F32) for k in range(N_CHIPS)])
    hr = PACK_ROWS // 2
    cc = lax.axis_index("c")
    from_sibling = _swap_halves(g_packed)
    my_half = lax.dynamic_slice_in_dim(g_packed, cc * hr, hr, axis=1)
    chip_sum = _add2(my_half, from_sibling, "chip_sum")
    by_chip = _scatter_partials(chip_sum)
    g_shard = _join_halves(_sum4(by_chip, "shard_sum"))

    g_out = _unpack(g_shard)
    d_out, m_out, v_out = {}, {}, {}
    for n in WEIGHTS:
        d_out[n], m_out[n], v_out[n] = _adam(w_sh[n], g_out[n], m_sh[n], v_sh[n], "adam_" + n)

    small_all = _gather_small(_pack_small(gs, loss))
    sm = {n: args["m_" + n] for n in SMALL}
    sv = {n: args["v_" + n] for n in SMALL}
    gs_sum, sd, snm, snv = _adam_small(_pack_small(small), small_all, _pack_small(sm), _pack_small(sv))
    gs_o, sd_o, snm_o, snv_o = _unpack_small(gs_sum), _unpack_small(sd), _unpack_small(snm), _unpack_small(snv)

    names = ["g_mix", "w_in", "w_alpha_up", "b_alpha", "b_forget", "g_gla_head", "g_mem", "w_mem_kv", "w_gla_o", "w_fox_o",
             "w_mem_o", "w_out", "g_ffn", "w_ff1", "w_ff2", "g_final"]

    def pick(big, sml, n):
        return big[n][None] if n in big else sml[n]

    outs = [gs_sum[6, 0], grad_x[None]]
    for big, sml in ((g_out, gs_o), (d_out, sd_o), (m_out, snm_o), (v_out, snv_o)):
        outs += [pick(big, sml, n) for n in names]
    return tuple(outs)
```

```python
import functools

import numpy as np
import jax
import jax.numpy as jnp
from jax import lax
from jax.experimental import pallas as pl
from jax.experimental.pallas import tpu as pltpu

F32 = jnp.float32
BF16 = jnp.bfloat16
HI = lax.Precision.HIGHEST
MESH = pl.DeviceIdType.MESH

EPS = 1e-6
D = 1024
CHUNK = 64
GLA_TAU = 16.0
N_CHIPS = 4
N_DEV = 8
VMEM_LIMIT_BYTES = 56 * 1024 * 1024

ADAM_LR, ADAM_B1, ADAM_B2, ADAM_EPS, ADAM_WD, ADAM_STEP = 0.001, 0.9, 0.999, 1e-08, 0.01, 10

PM_W = 6656
PE_W = 128
DP_W = 7168
C_GQ, C_GK, C_GV, C_GG, C_FQ, C_FK, C_FV, C_MQ = 3072, 3328, 3584, 4096, 4608, 5120, 5632, 6144
FF_LANE = 16

WEIGHTS = ("w_in", "w_alpha_up", "w_mem_kv", "w_gla_o", "w_fox_o", "w_mem_o", "w_out", "w_ff1", "w_ff2")
SHARD_AXIS = {"w_in": 1, "w_alpha_up": 1, "w_mem_kv": 0, "w_gla_o": 1, "w_fox_o": 1, "w_mem_o": 1, "w_out": 0,
              "w_ff1": 1, "w_ff2": 0}
SMALL = ("g_mix", "g_mem", "g_ffn", "g_final", "b_alpha", "b_forget", "g_gla_head")
PACK_W = 1024
PACK_ROWS = 4640


def _cp(*sem):
    return pltpu.CompilerParams(dimension_semantics=sem, vmem_limit_bytes=VMEM_LIMIT_BYTES)


def _dot(a, b, **kw):
    return jnp.dot(a, b, preferred_element_type=F32, **kw)


def _dot_nt(a, b, **kw):
    return lax.dot_general(a, b, (((1,), (1,)), ((), ())), preferred_element_type=F32, **kw)


def _dot_tn(a, b, **kw):
    return lax.dot_general(a, b, (((0,), (0,)), ((), ())), preferred_element_type=F32, **kw)


def _sigmoid(x):
    return 1.0 / (1.0 + jnp.exp(-x))


def _log_sigmoid(x):
    return -(jnp.maximum(-x, 0.0) + jnp.log1p(jnp.exp(-jnp.abs(x))))


def _fold8(x):
    m, n = x.shape
    return x.reshape(m // 8, 8, n).sum(axis=0)


def _iota(shape, dim):
    return lax.broadcasted_iota(jnp.int32, shape, dim)


def _row_tile(s):
    return min(s, 512)


def _mm_nn(a, b, *, out_dtype, tm, tn, tk, name, a_fn=None, epi=None, extra=None):
    m, k = a.shape
    _, n = b.shape
    nk = k // tk

    def body_one(*refs):
        a_ref, b_ref = refs[0], refs[1]
        at = a_ref[...] if a_fn is None else a_fn(a_ref[...])
        r = _dot(at, b_ref[...])
        if epi is not None:
            r = epi(r, None if extra is None else refs[2][...])
        refs[-1][...] = r.astype(out_dtype)

    if nk == 1:
        in_specs = [pl.BlockSpec((tm, k), lambda i, j: (i, 0)), pl.BlockSpec((k, tn), lambda i, j: (0, j))]
        args = [a, b]
        if extra is not None:
            in_specs.append(pl.BlockSpec((tm, tn), lambda i, j: (i, j)))
            args.append(extra)
        return pl.pallas_call(
            body_one, grid=(m // tm, n // tn), in_specs=in_specs, out_specs=pl.BlockSpec((tm, tn), lambda i, j: (i, j)),
            out_shape=jax.ShapeDtypeStruct((m, n), out_dtype), name=name, compiler_params=_cp("parallel", "parallel"))(*args)

    def body(*refs):
        if extra is None:
            a_ref, b_ref, o_ref, acc = refs
            x_ref = None
        else:
            a_ref, b_ref, x_ref, o_ref, acc = refs
        kk = pl.program_id(2)

        @pl.when(kk == 0)
        def _():
            acc[...] = jnp.zeros_like(acc)

        at = a_ref[...]
        if a_fn is not None:
            at = a_fn(at)
        acc[...] += _dot(at, b_ref[...])

        @pl.when(kk == nk - 1)
        def _():
            r = acc[...]
            if epi is not None:
                r = epi(r, None if x_ref is None else x_ref[...])
            o_ref[...] = r.astype(out_dtype)

    in_specs = [pl.BlockSpec((tm, tk), lambda i, j, kk: (i, kk)), pl.BlockSpec((tk, tn), lambda i, j, kk: (kk, j))]
    args = [a, b]
    if extra is not None:
        in_specs.append(pl.BlockSpec((tm, tn), lambda i, j, kk: (i, j)))
        args.append(extra)
    return pl.pallas_call(
        body, grid=(m // tm, n // tn, nk), in_specs=in_specs,
        out_specs=pl.BlockSpec((tm, tn), lambda i, j, kk: (i, j)),
        out_shape=jax.ShapeDtypeStruct((m, n), out_dtype),
        scratch_shapes=[pltpu.VMEM((tm, tn), F32)], name=name,
        compiler_params=_cp("parallel", "parallel", "arbitrary"))(*args)


def _mm_tn(a, b, *, tm, tn, ts, name, a_fn=None):
    s, m = a.shape
    _, n = b.shape
    ns = s // ts

    def body(a_ref, b_ref, o_ref, acc):
        kk = pl.program_id(2)

        @pl.when(kk == 0)
        def _():
            acc[...] = jnp.zeros_like(acc)

        at = a_ref[...]
        if a_fn is not None:
            at = a_fn(at)
        acc[...] += _dot_tn(at, b_ref[...])

        @pl.when(kk == ns - 1)
        def _():
            o_ref[...] = acc[...]

    return pl.pallas_call(
        body, grid=(m // tm, n // tn, ns),
        in_specs=[pl.BlockSpec((ts, tm), lambda i, j, kk: (kk, i)), pl.BlockSpec((ts, tn), lambda i, j, kk: (kk, j))],
        out_specs=pl.BlockSpec((tm, tn), lambda i, j, kk: (i, j)),
        out_shape=jax.ShapeDtypeStruct((m, n), F32),
        scratch_shapes=[pltpu.VMEM((tm, tn), F32)], name=name,
        compiler_params=_cp("parallel", "parallel", "arbitrary"))(a, b)


def _relu2_bf16(t):
    r = jnp.maximum(t.astype(F32), 0.0)
    return (r * r).astype(BF16)


def _rms_fwd(x, g, name):
    s, d = x.shape
    tm = min(s, 512)

    def body(x_ref, g_ref, u_ref, r_ref):
        xv = x_ref[...]
        r = lax.rsqrt(jnp.mean(xv * xv, axis=-1, keepdims=True) + EPS)
        u_ref[...] = ((xv * r) * g_ref[...]).astype(BF16)
        r_ref[...] = r

    return pl.pallas_call(
        body, grid=(s // tm,),
        in_specs=[pl.BlockSpec((tm, d), lambda i: (i, 0)), pl.BlockSpec((1, d), lambda i: (0, 0))],
        out_specs=[pl.BlockSpec((tm, d), lambda i: (i, 0)), pl.BlockSpec((tm, 1), lambda i: (i, 0))],
        out_shape=[jax.ShapeDtypeStruct((s, d), BF16), jax.ShapeDtypeStruct((s, 1), F32)],
        name=name, compiler_params=_cp("parallel"))(x, g)


def _mm_norm_bwd(a, b, xin, r, g, dres, *, name, want_bf16):
    s, k = a.shape
    tm = min(s, 256)

    def body(a_ref, b_ref, x_ref, r_ref, g_ref, dres_ref, *rest):
        if want_bf16:
            dx_ref, dxb_ref, dg_ref = rest
        else:
            dx_ref, dg_ref = rest

        @pl.when(pl.program_id(0) == 0)
        def _():
            dg_ref[...] = jnp.zeros_like(dg_ref)

        du = _dot(a_ref[...], b_ref[...])
        xn = x_ref[...] * r_ref[...]
        dg_ref[...] += _fold8(du * xn)
        dxn = du * g_ref[...]
        dx = dres_ref[...] + r_ref[...] * (dxn - xn * jnp.mean(dxn * xn, axis=-1, keepdims=True))
        dx_ref[...] = dx
        if want_bf16:
            dxb_ref[...] = dx.astype(BF16)

    row = lambda i: (i, 0)
    const = lambda i: (0, 0)
    out_specs = [pl.BlockSpec((tm, D), row)]
    out_shape = [jax.ShapeDtypeStruct((s, D), F32)]
    if want_bf16:
        out_specs.append(pl.BlockSpec((tm, D), row))
        out_shape.append(jax.ShapeDtypeStruct((s, D), BF16))
    out_specs.append(pl.BlockSpec((8, D), const))
    out_shape.append(jax.ShapeDtypeStruct((8, D), F32))
    return pl.pallas_call(
        body, grid=(s // tm,),
        in_specs=[pl.BlockSpec((tm, k), row), pl.BlockSpec((k, D), const, pipeline_mode=pl.Buffered(1)),
                  pl.BlockSpec((tm, D), row), pl.BlockSpec((tm, 1), row), pl.BlockSpec((1, D), const),
                  pl.BlockSpec((tm, D), row)],
        out_specs=out_specs, out_shape=out_shape,
        name=name, compiler_params=_cp("arbitrary"))(a, b, xin, r, g, dres)


def _gla_consts():
    lmask = _iota((4 * CHUNK, CHUNK), 0) % CHUNK >= _iota((4 * CHUNK, CHUNK), 1)
    hmask = _iota((256, 256), 0) // CHUNK == _iota((256, 256), 1) // CHUNK
    bd = _iota((256, 512), 0) // CHUNK == _iota((256, 512), 1) // 128
    return lmask, hmask, bd


def _fold_heads(x):
    return x[0:64] + x[64:128] + x[128:192] + x[192:256]


def _gla_chunk(lac, qc, kc):
    tri = (_iota((CHUNK, CHUNK), 0) >= _iota((CHUNK, CHUNK), 1)).astype(F32)
    b = _dot(tri, lac, precision=HI)
    bl = b[CHUNK - 1:CHUNK, :]
    ep, en, ek = jnp.exp(b), jnp.exp(-b), jnp.exp(bl - b)
    decb = jnp.exp(_dot_tn(lac, jnp.ones((CHUNK, 128), F32), precision=HI))
    decb = jnp.concatenate([decb] * 4, axis=1)
    return bl, ep, en, ek, decb, qc * ep, qc * en, kc * en, kc * ep, kc * ek


def _gla_fwd(pm, pe, wau_p, b_alpha):
    s = pm.shape[0]
    t = _row_tile(s)
    nc = t // CHUNK

    def body(q_ref, k_ref, v_ref, e_ref, wau_ref, ba_ref, o_ref, st_ref, state, la_scr):
        @pl.when(pl.program_id(0) == 0)
        def _():
            state[...] = jnp.zeros_like(state)

        z = _dot(e_ref[...].astype(BF16), wau_ref[...]) + ba_ref[...]
        la_scr[...] = _log_sigmoid(z) * (1.0 / GLA_TAU)
        lmask, hmask, bd = _gla_consts()

        def chunk(c, carry):
            rows = pl.ds(pl.multiple_of(c * CHUNK, CHUNK), CHUNK)
            qc = q_ref[rows, :].astype(F32) * 0.125
            kc = k_ref[rows, :].astype(F32)
            vc = v_ref[rows, :]
            _, _, _, _, decb, qp, qn, kn, kp, kk = _gla_chunk(la_scr[rows, :], qc, kc)
            qs = jnp.where(hmask, jnp.concatenate([qp] * 4, axis=0), 0.0).astype(BF16)
            qns = jnp.where(hmask, jnp.concatenate([qn] * 4, axis=0), 0.0).astype(BF16)
            attn = jnp.where(lmask, _dot_nt(qs, kn.astype(BF16)), _dot_nt(qns, kp.astype(BF16))).astype(BF16)
            st = state[...]
            o_intra = _fold_heads(jnp.where(bd, _dot(attn, vc), 0.0))
            o_ref[rows, :] = o_intra + _dot(qp.astype(BF16), st.astype(BF16))
            for h in range(4):
                st_ref[c, :, 128 * h:128 * (h + 1)] = st[64 * h:64 * (h + 1), 128 * h:128 * (h + 1)]
            kv = jnp.where(bd, _dot_tn(kk.astype(BF16), vc), 0.0)
            state[...] = st * decb + kv
            return carry

        lax.fori_loop(0, nc, chunk, 0)

    return pl.pallas_call(
        body, grid=(s // t,),
        in_specs=[pl.BlockSpec((t, 256), lambda i: (i, C_GQ // 256)), pl.BlockSpec((t, 256), lambda i: (i, C_GK // 256)),
                  pl.BlockSpec((t, 512), lambda i: (i, C_GV // 512)), pl.BlockSpec((t, PE_W), lambda i: (i, 0)),
                  pl.BlockSpec((PE_W, 256), lambda i: (0, 0)), pl.BlockSpec((1, 256), lambda i: (0, 0))],
        out_specs=[pl.BlockSpec((t, 512), lambda i: (i, 0)), pl.BlockSpec((nc, CHUNK, 512), lambda i: (i, 0, 0))],
        out_shape=[jax.ShapeDtypeStruct((s, 512), F32), jax.ShapeDtypeStruct((s // CHUNK, CHUNK, 512), F32)],
        scratch_shapes=[pltpu.VMEM((256, 512), F32), pltpu.VMEM((t, 256), F32)],
        name="gla_fwd", compiler_params=_cp("arbitrary"))(pm, pm, pm, pe, wau_p, b_alpha)


def _gla_bwd(pm, pe, wau_p, wau_pt, b_alpha, do, states):
    s = pm.shape[0]
    t = _row_tile(s)
    nc = t // CHUNK
    nb = s // t

    def body(q_ref, k_ref, v_ref, e_ref, wau_ref, waut_ref, ba_ref, do_ref, st_ref,
             dq_ref, dk_ref, dv_ref, de_ref, dwau_ref, dba_ref, gstate, la_scr, dla_scr):
        @pl.when(pl.program_id(0) == 0)
        def _():
            gstate[...] = jnp.zeros_like(gstate)
            dwau_ref[...] = jnp.zeros_like(dwau_ref)
            dba_ref[...] = jnp.zeros_like(dba_ref)

        eb = e_ref[...].astype(BF16)
        z = _dot(eb, wau_ref[...]) + ba_ref[...]
        la_scr[...] = _log_sigmoid(z) * (1.0 / GLA_TAU)
        lmask, hmask, bd = _gla_consts()
        triu = (_iota((CHUNK, CHUNK), 0) <= _iota((CHUNK, CHUNK), 1)).astype(F32)
        last_row = _iota((CHUNK, 256), 0) == CHUNK - 1

        def chunk(cc, carry):
            c = nc - 1 - cc
            rows = pl.ds(pl.multiple_of(c * CHUNK, CHUNK), CHUNK)
            qc = q_ref[rows, :].astype(F32) * 0.125
            kc = k_ref[rows, :].astype(F32)
            vc = v_ref[rows, :]
            dob = do_ref[rows, :]
            bl, ep, en, ek, decb, qp, qn, kn, kp, kk = _gla_chunk(la_scr[rows, :], qc, kc)
            qs = jnp.where(hmask, jnp.concatenate([qp] * 4, axis=0), 0.0).astype(BF16)
            qns = jnp.where(hmask, jnp.concatenate([qn] * 4, axis=0), 0.0).astype(BF16)
            knb, kpb = kn.astype(BF16), kp.astype(BF16)
            attn = jnp.where(lmask, _dot_nt(qs, knb), _dot_nt(qns, kpb)).astype(BF16)
            st = jnp.where(bd, jnp.concatenate([st_ref[c]] * 4, axis=0), 0.0)
            g = gstate[...]
            gb = g.astype(BF16)
            do_s = jnp.where(bd, jnp.concatenate([dob] * 4, axis=0), jnp.zeros((), BF16))
            dattn = _dot_nt(do_s, vc)
            dv_ref[rows, :] = (_dot_tn(attn, do_s) + _dot(kk.astype(BF16), gb)).astype(BF16)
            dac = jnp.where(lmask, dattn, 0.0).astype(BF16)
            daa = jnp.where(lmask, 0.0, dattn).astype(BF16)
            dqp = _fold_heads(jnp.where(hmask, _dot(dac, knb), 0.0)) + _dot_nt(dob, st.astype(BF16))
            dqn = _fold_heads(jnp.where(hmask, _dot(daa, kpb), 0.0))
            dkn = _dot_tn(dac, qs)
            dkp = _dot_tn(daa, qns)
            dkk = _dot_nt(vc, gb)
            ddec = _dot_nt(jnp.ones((8, 512), F32), g * st, precision=HI)[0:1, :]
            gstate[...] = decb * g + jnp.where(bd, _dot_tn(qp.astype(BF16), dob), 0.0)
            dq_ref[rows, :] = ((dqp * ep + dqn * en) * 0.125).astype(BF16)
            dk_ref[rows, :] = (dkn * en + dkp * ep + dkk * ek).astype(BF16)
            dek = dkk * kc * ek
            db = (dqp * qc + dkp * kc) * ep - (dqn * qc + dkn * kc) * en - dek
            dbl = jnp.sum(dek, axis=0, keepdims=True) + ddec * jnp.exp(bl)
            db = db + jnp.where(last_row, dbl, 0.0)
            dla_scr[rows, :] = _dot(triu, db, precision=HI)
            return carry

        lax.fori_loop(0, nc, chunk, 0)
        dz = dla_scr[...] * (1.0 / GLA_TAU) * _sigmoid(-z)
        dzb = dz.astype(BF16)
        dwau_ref[...] += _dot_tn(eb, dzb)
        dba_ref[...] += _fold8(dz)
        de_ref[...] = _dot(dzb, waut_ref[...])

    rev = lambda i: nb - 1 - i
    return pl.pallas_call(
        body, grid=(nb,),
        in_specs=[pl.BlockSpec((t, 256), lambda i: (rev(i), C_GQ // 256)), pl.BlockSpec((t, 256), lambda i: (rev(i), C_GK // 256)),
                  pl.BlockSpec((t, 512), lambda i: (rev(i), C_GV // 512)), pl.BlockSpec((t, PE_W), lambda i: (rev(i), 0)),
                  pl.BlockSpec((PE_W, 256), lambda i: (0, 0)), pl.BlockSpec((256, PE_W), lambda i: (0, 0)),
                  pl.BlockSpec((1, 256), lambda i: (0, 0)), pl.BlockSpec((t, 512), lambda i: (rev(i), 0)),
                  pl.BlockSpec((nc, CHUNK, 512), lambda i: (rev(i), 0, 0))],
        out_specs=[pl.BlockSpec((t, 256), lambda i: (rev(i), 0)), pl.BlockSpec((t, 256), lambda i: (rev(i), 0)),
                   pl.BlockSpec((t, 512), lambda i: (rev(i), 0)), pl.BlockSpec((t, PE_W), lambda i: (rev(i), 0)),
                   pl.BlockSpec((PE_W, 256), lambda i: (0, 0)), pl.BlockSpec((8, 256), lambda i: (0, 0))],
        out_shape=[jax.ShapeDtypeStruct((s, 256), BF16), jax.ShapeDtypeStruct((s, 256), BF16),
                   jax.ShapeDtypeStruct((s, 512), BF16), jax.ShapeDtypeStruct((s, PE_W), F32),
                   jax.ShapeDtypeStruct((PE_W, 256), F32), jax.ShapeDtypeStruct((8, 256), F32)],
        scratch_shapes=[pltpu.VMEM((256, 512), F32), pltpu.VMEM((t, 256), F32), pltpu.VMEM((t, 256), F32)],
        name="gla_bwd", compiler_params=_cp("arbitrary"))(pm, pm, pm, pe, wau_p, wau_pt, b_alpha, do, states)


def _fcum_fwd(pe, bias):
    s = pe.shape[0]
    t = min(s, 256)

    def body(e_ref, b_ref, f_ref, carry):
        @pl.when(pl.program_id(0) == 0)
        def _():
            carry[...] = jnp.zeros_like(carry)

        lf = _log_sigmoid(e_ref[...] + b_ref[...])
        tri = (_iota((t, t), 0) >= _iota((t, t), 1)).astype(F32)
        f = _dot(tri, lf, precision=HI) + carry[0:1, :]
        f_ref[...] = f
        carry[...] = jnp.broadcast_to(f[t - 1:t, :], carry.shape)

    return pl.pallas_call(
        body, grid=(s // t,),
        in_specs=[pl.BlockSpec((t, PE_W), lambda i: (i, 0)), pl.BlockSpec((1, PE_W), lambda i: (0, 0))],
        out_specs=pl.BlockSpec((t, PE_W), lambda i: (i, 0)),
        out_shape=jax.ShapeDtypeStruct((s, PE_W), F32), scratch_shapes=[pltpu.VMEM((8, PE_W), F32)],
        name="fcum_fwd", compiler_params=_cp("arbitrary"))(pe, bias)


def _fcum_bwd(pe, bias, df):
    s = pe.shape[0]
    t = min(s, 256)
    nb = s // t

    def body(e_ref, b_ref, df_ref, de_ref, db_ref, carry):
        @pl.when(pl.program_id(0) == 0)
        def _():
            carry[...] = jnp.zeros_like(carry)
            db_ref[...] = jnp.zeros_like(db_ref)

        triu = (_iota((t, t), 0) <= _iota((t, t), 1)).astype(F32)
        dlf = _dot(triu, df_ref[...], precision=HI) + carry[0:1, :]
        carry[...] = jnp.broadcast_to(dlf[0:1, :], carry.shape)
        lane = _iota((t, PE_W), 1)
        dff = jnp.where((lane >= FF_LANE) & (lane < FF_LANE + 8), dlf * _sigmoid(-(e_ref[...] + b_ref[...])), 0.0)
        de_ref[...] = dff
        db_ref[...] += _fold8(dff)

    rev = lambda i: (nb - 1 - i, 0)
    return pl.pallas_call(
        body, grid=(nb,),
        in_specs=[pl.BlockSpec((t, PE_W), rev), pl.BlockSpec((1, PE_W), lambda i: (0, 0)), pl.BlockSpec((t, PE_W), rev)],
        out_specs=[pl.BlockSpec((t, PE_W), rev), pl.BlockSpec((8, PE_W), lambda i: (0, 0))],
        out_shape=[jax.ShapeDtypeStruct((s, PE_W), F32), jax.ShapeDtypeStruct((8, PE_W), F32)],
        scratch_shapes=[pltpu.VMEM((8, PE_W), F32)],
        name="fcum_bwd", compiler_params=_cp("arbitrary"))(pe, bias, df)


FOX_WIDE = 1024


def _split3(x):
    hi = x.astype(BF16)
    r = x - hi.astype(F32)
    mid = r.astype(BF16)
    lo = (r - mid.astype(F32)).astype(BF16)
    return jnp.concatenate([hi, mid, lo], axis=1)


def _fox_tables():
    heads, lane = np.arange(8), np.arange(64)
    spread = np.zeros((512, 1024), np.float32)
    spread[(64 * heads[:, None] + lane).ravel(), (128 * heads[:, None] + lane).ravel()] = 1.0
    def place(src_lane0, dst_off, val):
        t = np.zeros((384, 1024), np.float32)
        for p in range(3):
            t[128 * p + src_lane0 + heads, 128 * heads + dst_off + p] = val
        return t
    def const(off, val):
        c = np.zeros((1, 1024), np.float32)
        for p in range(3):
            c[0, 128 * heads + off + p] = val
        return c
    pick = np.zeros((1024, 128), np.float32)
    pick[128 * heads + 64, FF_LANE + heads] = 1.0
    rows = np.zeros((8, 128), np.float32)
    rows[heads, FF_LANE + heads] = 1.0
    bf = lambda a: jnp.asarray(a, BF16)
    return dict(spread=bf(spread), spread_q=bf(0.125 * spread), compact=bf(spread.T),
                f_to_q=bf(place(FF_LANE, 64, 1.0)), f_to_k=bf(place(FF_LANE, 67, -1.0)), d_to_do=bf(place(0, 64, 1.0)),
                ones_q=jnp.asarray(const(67, 1.0)), ones_k=jnp.asarray(const(64, 1.0)), ones_v=jnp.asarray(const(64, -1.0)),
                pick=bf(pick), rows=jnp.asarray(rows))


def _fox_prep(pm, f128, lse8, tb, *, backward):
    s = pm.shape[0]
    tm = _row_tile(s)

    def body(*refs):
        if backward:
            q_ref, f_ref, lse_ref, spq_ref, fq_ref, cq_ref, rows_ref, qa_ref = refs
            f = f_ref[...] - _dot_tn(lse_ref[...], rows_ref[...], precision=HI)
            qa_ref[...] = (_dot(q_ref[...], spq_ref[...]) + _dot(_split3(f), fq_ref[...]) + cq_ref[...]).astype(BF16)
            return
        (q_ref, k_ref, v_ref, f_ref, spq_ref, sp_ref, fq_ref, fk_ref, cq_ref, ck_ref, cv_ref,
         qa_ref, ka_ref, va_ref, vt_ref, qt_ref) = refs
        f3 = _split3(f_ref[...])
        q, v = q_ref[...], v_ref[...]
        qa_ref[...] = (_dot(q, spq_ref[...]) + _dot(f3, fq_ref[...]) + cq_ref[...]).astype(BF16)
        ka_ref[...] = (_dot(k_ref[...], sp_ref[...]) + _dot(f3, fk_ref[...]) + ck_ref[...]).astype(BF16)
        va_ref[...] = (_dot(v, sp_ref[...]) + cv_ref[...]).astype(BF16)
        vt_ref[...] = v.T
        qt_ref[...] = (q.astype(F32) * 0.125).astype(BF16).T

    row = lambda i: (i, 0)
    const = lambda i: (0, 0)
    blk = lambda c: pl.BlockSpec((tm, 512), lambda i: (i, c // 512))
    wide = pl.BlockSpec((tm, 1024), row)
    mat = lambda a: pl.BlockSpec(a.shape, const)
    if backward:
        ins = [pm, f128, lse8, tb["spread_q"], tb["f_to_q"], tb["ones_q"], tb["rows"]]
        in_specs = [blk(C_FQ), pl.BlockSpec((tm, 128), row), pl.BlockSpec((8, tm), lambda i: (0, i))] + [mat(a) for a in ins[3:]]
        out_specs, out_shape = wide, jax.ShapeDtypeStruct((s, 1024), BF16)
    else:
        ins = [pm, pm, pm, f128, tb["spread_q"], tb["spread"], tb["f_to_q"], tb["f_to_k"], tb["ones_q"], tb["ones_k"], tb["ones_v"]]
        in_specs = [blk(C_FQ), blk(C_FK), blk(C_FV), pl.BlockSpec((tm, 128), row)] + [mat(a) for a in ins[4:]]
        tr = pl.BlockSpec((512, tm), lambda i: (0, i))
        out_specs = [wide, wide, wide, tr, tr]
        out_shape = [jax.ShapeDtypeStruct((s, 1024), BF16)] * 3 + [jax.ShapeDtypeStruct((512, s), BF16)] * 2
    return pl.pallas_call(body, grid=(s // tm,), in_specs=in_specs, out_specs=out_specs, out_shape=out_shape,
                          name="fox_prep_bwd" if backward else "fox_prep", compiler_params=_cp("parallel"))(*ins)


def _fox_post(dq, dkt, dvt, dfcol8, tb):
    s = dq.shape[0]
    tm = _row_tile(s)

    def body(dq_ref, dkt_ref, dvt_ref, dfc_ref, cmp_ref, pick_ref, rows_ref, dfq_ref, dfk_ref, dfv_ref, df_ref):
        d = dq_ref[...]
        dfq_ref[...] = _dot(d.astype(BF16), cmp_ref[...]).astype(BF16)
        d3 = _split3(d)
        pick = pick_ref[...]
        rsum = _dot(d3[:, 0:1024], pick) + _dot(d3[:, 1024:2048], pick) + _dot(d3[:, 2048:3072], pick)
        df_ref[...] = rsum - _dot_tn(dfc_ref[...], rows_ref[...], precision=HI)
        dfk_ref[...] = dkt_ref[...].T
        dfv_ref[...] = dvt_ref[...].T

    row = lambda i: (i, 0)
    const = lambda i: (0, 0)
    tr = pl.BlockSpec((512, tm), lambda i: (0, i))
    out = pl.BlockSpec((tm, 512), row)
    return pl.pallas_call(
        body, grid=(s // tm,),
        in_specs=[pl.BlockSpec((tm, 1024), row), tr, tr, pl.BlockSpec((8, tm), lambda i: (0, i)),
                  pl.BlockSpec((1024, 512), const), pl.BlockSpec((1024, 128), const), pl.BlockSpec((8, 128), const)],
        out_specs=[out, out, out, pl.BlockSpec((tm, 128), row)],
        out_shape=[jax.ShapeDtypeStruct((s, 512), BF16)] * 3 + [jax.ShapeDtypeStruct((s, 128), F32)],
        name="fox_post", compiler_params=_cp("parallel"))(dq, dkt, dvt, dfcol8, tb["compact"], tb["pick"], tb["rows"])


def _fox_fwd(k_aug, q_aug, vt):
    s = k_aug.shape[0]
    nh = 8
    tk = _row_tile(s)
    tq = min(s, FOX_WIDE)
    per = tq // tk

    def body(k_ref, q_ref, v_ref, o_ref, lse_ref, sbuf):
        i = pl.program_id(1)
        qa = q_ref[...]

        def scores(j):
            return _dot_nt(k_ref[pl.ds(pl.multiple_of(j * tk, tk), tk), :], qa)

        def update(st, j, carry):
            m, l, acc = carry
            m2 = jnp.maximum(m, jnp.max(st, axis=0, keepdims=True))
            p = jnp.exp(st - m2)
            a = jnp.exp(m - m2)
            vj = v_ref[:, pl.ds(pl.multiple_of(j * tk, tk), tk)]
            return m2, a * l + jnp.sum(p, axis=0, keepdims=True), a * acc + _dot(vj, p.astype(BF16))

        def step(a, carry):
            sbuf[1] = scores(2 * a + 1)
            carry = update(sbuf[0], 2 * a, carry)
            sbuf[0] = scores(2 * a + 2)
            return update(sbuf[1], 2 * a + 1, carry)

        n = i * per
        sbuf[0] = scores(0)
        carry = (jnp.full((1, tq), -1e30, F32), jnp.zeros((1, tq), F32), jnp.zeros((64, tq), F32))
        carry = lax.fori_loop(0, n // 2, step, carry)
        for r in range(per):
            j = n + r
            st = sbuf[0] if r == 0 else scores(j)
            st = jnp.where(j * tk + _iota((tk, tq), 0) <= i * tq + _iota((tk, tq), 1), st, -1e30)
            carry = update(st, j, carry)
        m, l, acc = carry
        o_ref[...] = (acc / l).astype(BF16)
        lse_ref[0] = m + jnp.log(l)

    return pl.pallas_call(
        body, grid=(nh, s // tq),
        in_specs=[pl.BlockSpec((s, 128), lambda h, i: (0, h)), pl.BlockSpec((tq, 128), lambda h, i: (i, h)),
                  pl.BlockSpec((64, s), lambda h, i: (h, 0))],
        out_specs=[pl.BlockSpec((64, tq), lambda h, i: (h, i)), pl.BlockSpec((1, 1, tq), lambda h, i: (h, 0, i))],
        out_shape=[jax.ShapeDtypeStruct((512, s), BF16), jax.ShapeDtypeStruct((nh, 1, s), F32)],
        scratch_shapes=[pltpu.VMEM((2, tk, tq), F32)],
        name="fox_fwd", compiler_params=_cp("parallel", "arbitrary"))(k_aug, q_aug, vt)


def _fox_bwd(q_aug, do_aug, qt, dot_, k_aug, v_aug):
    s = q_aug.shape[0]
    nh = 8
    tq = _row_tile(s)
    tk = min(s, FOX_WIDE)
    per = tk // tq
    nqb = s // tq

    def body(qa_ref, da_ref, qt_ref, dt_ref, ka_ref, va_ref, dq_ref, dk_ref, dv_ref, dfk_ref):
        j = pl.program_id(1)

        @pl.when(j == 0)
        def _():
            dq_ref[...] = jnp.zeros_like(dq_ref)

        ka, va = ka_ref[...], va_ref[...]
        ks = jnp.where(_iota((tk, 128), 1) < 64, ka.astype(F32) * 0.125, 0.0).astype(BF16)
        lane64 = _iota((tq, 128), 1) == 64

        def tile(i, masked, carry):
            dk, dv, dfk = carry
            rows = pl.ds(pl.multiple_of(i * tq, tq), tq)
            sp = _dot_nt(qa_ref[rows, :], ka)
            if masked:
                sp = jnp.where(i * tq + _iota((tq, tk), 0) >= j * tk + _iota((tq, tk), 1), sp, -1e30)
            p = jnp.exp(sp)
            ds = p * _dot_nt(da_ref[rows, :], va)
            dsb = ds.astype(BF16)
            dq_ref[rows, :] += _dot(dsb, ks) + jnp.where(lane64, jnp.sum(ds, axis=1, keepdims=True), 0.0)
            return (dk + _dot(qt_ref[:, rows], dsb), dv + _dot(dt_ref[:, rows], p.astype(BF16)),
                    dfk + jnp.sum(ds, axis=0, keepdims=True))

        carry = (jnp.zeros((64, tk), F32), jnp.zeros((64, tk), F32), jnp.zeros((1, tk), F32))
        for r in range(per):
            carry = tile(j * per + r, True, carry)
        dk, dv, dfk = lax.fori_loop((j + 1) * per, nqb, lambda i, c: tile(i, False, c), carry)
        dk_ref[...] = dk.astype(BF16)
        dv_ref[...] = dv.astype(BF16)
        dfk_ref[0] = dfk

    head_cols = lambda h, j: (0, h)
    head_rows = lambda h, j: (h, 0)
    once = dict(pipeline_mode=pl.Buffered(1))
    return pl.pallas_call(
        body, grid=(nh, s // tk),
        in_specs=[pl.BlockSpec((s, 128), head_cols, **once), pl.BlockSpec((s, 128), head_cols, **once),
                  pl.BlockSpec((64, s), head_rows, **once), pl.BlockSpec((64, s), head_rows, **once),
                  pl.BlockSpec((tk, 128), lambda h, j: (j, h)), pl.BlockSpec((tk, 128), lambda h, j: (j, h))],
        out_specs=[pl.BlockSpec((s, 128), head_cols), pl.BlockSpec((64, tk), lambda h, j: (h, j)),
                   pl.BlockSpec((64, tk), lambda h, j: (h, j)), pl.BlockSpec((1, 1, tk), lambda h, j: (h, 0, j))],
        out_shape=[jax.ShapeDtypeStruct((s, 1024), F32), jax.ShapeDtypeStruct((512, s), BF16),
                   jax.ShapeDtypeStruct((512, s), BF16), jax.ShapeDtypeStruct((nh, 1, s), F32)],
        name="fox_bwd", compiler_params=_cp("parallel", "arbitrary"))(q_aug, do_aug, qt, dot_, k_aug, v_aug)


MEM_SCALE = 128 ** -0.5


def _mem_attn_fwd(pm, mkv):
    s = pm.shape[0]
    t = _row_tile(s)
    nm = mkv.shape[0]

    def body(q_ref, mk_ref, mv_ref, o_ref):
        for h in range(4):
            cols = slice(128 * h, 128 * (h + 1))
            sc = _dot_nt(q_ref[:, cols], mk_ref[:, cols]) * MEM_SCALE
            p = jnp.exp(sc - jnp.max(sc, axis=-1, keepdims=True))
            p = p / jnp.sum(p, axis=-1, keepdims=True)
            o_ref[:, cols] = _dot(p.astype(BF16), mv_ref[:, cols]).astype(BF16)

    return pl.pallas_call(
        body, grid=(s // t,),
        in_specs=[pl.BlockSpec((t, 512), lambda i: (i, C_MQ // 512)), pl.BlockSpec((nm, 512), lambda i: (0, 0)),
                  pl.BlockSpec((nm, 512), lambda i: (0, 1))],
        out_specs=pl.BlockSpec((t, 512), lambda i: (i, 0)),
        out_shape=jax.ShapeDtypeStruct((s, 512), BF16),
        name="mem_attn_fwd", compiler_params=_cp("parallel"))(pm, mkv, mkv)


def _mem_attn_bwd(pm, mkv, do):
    s = pm.shape[0]
    t = _row_tile(s)
    nm = mkv.shape[0]

    def body(q_ref, mk_ref, mv_ref, do_ref, dq_ref, dmk_ref, dmv_ref):
        @pl.when(pl.program_id(0) == 0)
        def _():
            dmk_ref[...] = jnp.zeros_like(dmk_ref)
            dmv_ref[...] = jnp.zeros_like(dmv_ref)

        for h in range(4):
            cols = slice(128 * h, 128 * (h + 1))
            qh, kh, vh, doh = q_ref[:, cols], mk_ref[:, cols], mv_ref[:, cols], do_ref[:, cols]
            sc = _dot_nt(qh, kh) * MEM_SCALE
            p = jnp.exp(sc - jnp.max(sc, axis=-1, keepdims=True))
            p = p / jnp.sum(p, axis=-1, keepdims=True)
            pb = p.astype(BF16)
            dp = _dot_nt(doh, vh)
            ds = (p * (dp - jnp.sum(p * dp, axis=-1, keepdims=True)) * MEM_SCALE).astype(BF16)
            dq_ref[:, cols] = _dot(ds, kh).astype(BF16)
            dmk_ref[:, cols] += _dot_tn(ds, qh)
            dmv_ref[:, cols] += _dot_tn(pb, doh)

    return pl.pallas_call(
        body, grid=(s // t,),
        in_specs=[pl.BlockSpec((t, 512), lambda i: (i, C_MQ // 512)), pl.BlockSpec((nm, 512), lambda i: (0, 0)),
                  pl.BlockSpec((nm, 512), lambda i: (0, 1)), pl.BlockSpec((t, 512), lambda i: (i, 0))],
        out_specs=[pl.BlockSpec((t, 512), lambda i: (i, 0)), pl.BlockSpec((nm, 512), lambda i: (0, 0)),
                   pl.BlockSpec((nm, 512), lambda i: (0, 0))],
        out_shape=[jax.ShapeDtypeStruct((s, 512), BF16), jax.ShapeDtypeStruct((nm, 512), F32),
                   jax.ShapeDtypeStruct((nm, 512), F32)],
        name="mem_attn_bwd", compiler_params=_cp("arbitrary"))(pm, mkv, mkv, do)


def _gain_grad(dxn_g, x, r, name):
    m, d = x.shape

    def body(d_ref, x_ref, r_ref, o_ref):
        o_ref[...] = _fold8(d_ref[...] * (x_ref[...] * r_ref[...]))

    return pl.pallas_call(body, out_shape=jax.ShapeDtypeStruct((8, d), F32), name=name,
                          compiler_params=pltpu.CompilerParams(vmem_limit_bytes=VMEM_LIMIT_BYTES))(dxn_g, x, r)


def _head_norm(o, gh):
    xs, rs = [], []
    for h in range(4):
        oh = o[:, 128 * h:128 * (h + 1)]
        r = lax.rsqrt(jnp.mean(oh * oh, axis=-1, keepdims=True) + EPS)
        xs.append(oh * r)
        rs.append(r)
    return xs, rs


def _merge_fwd(x, pm, o_gla, o_fox_t, o_mem, g_head, wg, wf, wm, wo, g_ffn):
    s = x.shape[0]
    t = min(s, 256)

    def body(x_ref, g0_ref, g1_ref, g2_ref, gg_ref, og_ref, of_ref, om_ref, gh_ref, wg_ref, wf_ref, wm_ref, wo_ref, gf_ref,
             mg_ref, h1_ref, u2_ref, r2_ref):
        xs, _ = _head_norm(og_ref[...], None)
        gg = gg_ref[...].astype(F32)
        sil = gg * _sigmoid(gg)
        ogn = jnp.concatenate(xs, axis=1) * gh_ref[...] * sil
        merged = (_sigmoid(g0_ref[...].astype(F32)) * _dot(ogn.astype(BF16), wg_ref[...])
                  + _sigmoid(g1_ref[...].astype(F32)) * _dot(of_ref[...].T, wf_ref[...])
                  + _sigmoid(g2_ref[...].astype(F32)) * _dot(om_ref[...], wm_ref[...]))
        mb = merged.astype(BF16)
        mg_ref[...] = mb
        h1 = x_ref[...] + _dot(mb, wo_ref[...])
        h1_ref[...] = h1
        r = lax.rsqrt(jnp.mean(h1 * h1, axis=-1, keepdims=True) + EPS)
        u2_ref[...] = ((h1 * r) * gf_ref[...]).astype(BF16)
        r2_ref[...] = r

    row = lambda i: (i, 0)
    const = lambda i: (0, 0)
    return pl.pallas_call(
        body, grid=(s // t,),
        in_specs=[pl.BlockSpec((t, D), row), pl.BlockSpec((t, D), lambda i: (i, 0)), pl.BlockSpec((t, D), lambda i: (i, 1)),
                  pl.BlockSpec((t, D), lambda i: (i, 2)), pl.BlockSpec((t, 512), lambda i: (i, C_GG // 512)),
                  pl.BlockSpec((t, 512), row), pl.BlockSpec((512, t), lambda i: (0, i)), pl.BlockSpec((t, 512), row),
                  pl.BlockSpec((1, 512), const), pl.BlockSpec((512, D), const), pl.BlockSpec((512, D), const),
                  pl.BlockSpec((512, D), const), pl.BlockSpec((D, D), const), pl.BlockSpec((1, D), const)],
        out_specs=[pl.BlockSpec((t, D), row), pl.BlockSpec((t, D), row), pl.BlockSpec((t, D), row), pl.BlockSpec((t, 1), row)],
        out_shape=[jax.ShapeDtypeStruct((s, D), BF16), jax.ShapeDtypeStruct((s, D), F32),
                   jax.ShapeDtypeStruct((s, D), BF16), jax.ShapeDtypeStruct((s, 1), F32)],
        name="merge_fwd", compiler_params=_cp("parallel"))(x, pm, pm, pm, pm, o_gla, o_fox_t, o_mem, g_head, wg, wf, wm, wo, g_ffn)


def _merge_bwd(dh1b, pm, o_gla, o_fox_t, o_mem, g_head, wg, wf, wm, wgt, wft, wmt, wot, spread, d_to_do):
    s = dh1b.shape[0]
    t = min(s, 256)

    def body(dh_ref, g0_ref, g1_ref, g2_ref, gg_ref, og_ref, of_ref, om_ref, gh_ref, wg_ref, wf_ref, wm_ref,
             wgt_ref, wft_ref, wmt_ref, wot_ref, sp_ref, dd_ref,
             dgt_ref, dgg_ref, dog_ref, da_ref, dot_ref, dom_ref, dwg_ref, dwf_ref, dwm_ref, dgh_ref):
        @pl.when(pl.program_id(0) == 0)
        def _():
            dwg_ref[...] = jnp.zeros_like(dwg_ref)
            dwf_ref[...] = jnp.zeros_like(dwf_ref)
            dwm_ref[...] = jnp.zeros_like(dwm_ref)
            dgh_ref[...] = jnp.zeros_like(dgh_ref)

        dmerged = _dot(dh_ref[...], wot_ref[...])
        og = og_ref[...]
        xs, rs = _head_norm(og, None)
        on = jnp.concatenate(xs, axis=1)
        gg = gg_ref[...].astype(F32)
        sg = _sigmoid(gg)
        sil = gg * sg
        gh = gh_ref[...]
        ognb = (on * gh * sil).astype(BF16)
        ofb, omb = of_ref[...].T, om_ref[...]
        douts = []
        for idx, (gref, ob, w_ref, wt_ref, dw_ref) in enumerate((
                (g0_ref, ognb, wg_ref, wgt_ref, dwg_ref), (g1_ref, ofb, wf_ref, wft_ref, dwf_ref),
                (g2_ref, omb, wm_ref, wmt_ref, dwm_ref))):
            gt = _sigmoid(gref[...].astype(F32))
            y = _dot(ob, w_ref[...])
            dgt_ref[:, D * idx:D * (idx + 1)] = (dmerged * y * gt * (1.0 - gt)).astype(BF16)
            dy = (gt * dmerged).astype(BF16)
            dw_ref[...] += _dot_tn(ob, dy)
            douts.append(_dot(dy, wt_ref[...]))
        dogn, dof, dom = douts
        dofb = dof.astype(BF16)
        dom_ref[...] = dom.astype(BF16)
        ind = (_iota((512, 128), 0) // 64 == _iota((512, 128), 1)).astype(F32)
        delta = _dot(dofb.astype(F32) * ofb.astype(F32), ind, precision=HI)
        da_ref[...] = (_dot(dofb, sp_ref[...]) + _dot(_split3(delta), dd_ref[...])).astype(BF16)
        dot_ref[...] = dofb.T
        dgg_ref[...] = (dogn * on * gh * (sg * (1.0 + gg * (1.0 - sg)))).astype(BF16)
        d_on = dogn * sil
        dgh_ref[...] += _fold8(d_on * on)
        dxn = d_on * gh
        outs = []
        for h in range(4):
            cols = slice(128 * h, 128 * (h + 1))
            dh_, xh = dxn[:, cols], xs[h]
            outs.append(rs[h] * (dh_ - xh * jnp.mean(dh_ * xh, axis=-1, keepdims=True)))
        dog_ref[...] = jnp.concatenate(outs, axis=1).astype(BF16)

    row = lambda i: (i, 0)
    const = lambda i: (0, 0)
    return pl.pallas_call(
        body, grid=(s // t,),
        in_specs=[pl.BlockSpec((t, D), row), pl.BlockSpec((t, D), lambda i: (i, 0)), pl.BlockSpec((t, D), lambda i: (i, 1)),
                  pl.BlockSpec((t, D), lambda i: (i, 2)), pl.BlockSpec((t, 512), lambda i: (i, C_GG // 512)),
                  pl.BlockSpec((t, 512), row), pl.BlockSpec((512, t), lambda i: (0, i)), pl.BlockSpec((t, 512), row),
                  pl.BlockSpec((1, 512), const), pl.BlockSpec((512, D), const), pl.BlockSpec((512, D), const),
                  pl.BlockSpec((512, D), const), pl.BlockSpec((D, 512), const), pl.BlockSpec((D, 512), const),
                  pl.BlockSpec((D, 512), const), pl.BlockSpec((D, D), const),
                  pl.BlockSpec((512, 1024), const), pl.BlockSpec((384, 1024), const)],
        out_specs=[pl.BlockSpec((t, 3 * D), row), pl.BlockSpec((t, 512), row), pl.BlockSpec((t, 512), row),
                   pl.BlockSpec((t, 1024), row), pl.BlockSpec((512, t), lambda i: (0, i)), pl.BlockSpec((t, 512), row),
                   pl.BlockSpec((512, D), const), pl.BlockSpec((512, D), const), pl.BlockSpec((512, D), const),
                   pl.BlockSpec((8, 512), const)],
        out_shape=[jax.ShapeDtypeStruct((s, 3 * D), BF16), jax.ShapeDtypeStruct((s, 512), BF16),
                   jax.ShapeDtypeStruct((s, 512), BF16), jax.ShapeDtypeStruct((s, 1024), BF16),
                   jax.ShapeDtypeStruct((512, s), BF16), jax.ShapeDtypeStruct((s, 512), BF16),
                   jax.ShapeDtypeStruct((512, D), F32), jax.ShapeDtypeStruct((512, D), F32),
                   jax.ShapeDtypeStruct((512, D), F32), jax.ShapeDtypeStruct((8, 512), F32)],
        name="merge_bwd", compiler_params=_cp("arbitrary"))(
            dh1b, pm, pm, pm, pm, o_gla, o_fox_t, o_mem, g_head, wg, wf, wm, wgt, wft, wmt, wot, spread, d_to_do)


def _ff2_loss(a, w2, h1, g_final, target):
    s, k = a.shape
    tm = min(s, 256)

    def body(a_ref, w_ref, h1_ref, g_ref, t_ref, dh_ref, dhb_ref, loss_ref, dg_ref):
        @pl.when(pl.program_id(0) == 0)
        def _():
            loss_ref[...] = jnp.zeros_like(loss_ref)
            dg_ref[...] = jnp.zeros_like(dg_ref)

        h2 = h1_ref[...] + _dot(_relu2_bf16(a_ref[...]), w_ref[...])
        r = lax.rsqrt(jnp.mean(h2 * h2, axis=-1, keepdims=True) + EPS)
        xn = h2 * r
        g = g_ref[...]
        err = xn * g - t_ref[...]
        e2 = _fold8(err * err)
        part = e2[:, 0:128]
        for c in range(1, D // 128):
            part = part + e2[:, 128 * c:128 * (c + 1)]
        loss_ref[...] += part
        dy = err * (1.0 / D)
        dg_ref[...] += _fold8(dy * xn)
        dxn = dy * g
        dh = r * (dxn - xn * jnp.mean(dxn * xn, axis=-1, keepdims=True))
        dh_ref[...] = dh
        dhb_ref[...] = dh.astype(BF16)

    row = lambda i: (i, 0)
    const = lambda i: (0, 0)
    return pl.pallas_call(
        body, grid=(s // tm,),
        in_specs=[pl.BlockSpec((tm, k), row), pl.BlockSpec((k, D), const, pipeline_mode=pl.Buffered(1)),
                  pl.BlockSpec((tm, D), row), pl.BlockSpec((1, D), const), pl.BlockSpec((tm, D), row)],
        out_specs=[pl.BlockSpec((tm, D), row), pl.BlockSpec((tm, D), row), pl.BlockSpec((8, 128), const),
                   pl.BlockSpec((8, D), const)],
        out_shape=[jax.ShapeDtypeStruct((s, D), F32), jax.ShapeDtypeStruct((s, D), BF16),
                   jax.ShapeDtypeStruct((8, 128), F32), jax.ShapeDtypeStruct((8, D), F32)],
        name="ff2_loss", compiler_params=_cp("arbitrary"))(a, w2, h1, g_final, target)


def _adam(w, g, m, v, name):
    r, c = w.shape
    tr = r
    for cand in (512, 256, 128, 64, 32, 16, 8):
        if r % cand == 0 and cand * c * 4 <= (1 << 20):
            tr = cand
            break
    c1 = 1.0 - ADAM_B1 ** ADAM_STEP
    c2 = 1.0 - ADAM_B2 ** ADAM_STEP

    def body(w_ref, g_ref, m_ref, v_ref, d_ref, nm_ref, nv_ref):
        gv = g_ref[...]
        nm = ADAM_B1 * m_ref[...] + (1.0 - ADAM_B1) * gv
        nv = ADAM_B2 * v_ref[...] + (1.0 - ADAM_B2) * (gv * gv)
        d_ref[...] = -ADAM_LR * ((nm / c1) / (jnp.sqrt(nv / c2) + ADAM_EPS) + ADAM_WD * w_ref[...])
        nm_ref[...] = nm
        nv_ref[...] = nv

    spec = pl.BlockSpec((tr, c), lambda i: (i, 0))
    return pl.pallas_call(
        body, grid=(r // tr,), in_specs=[spec] * 4, out_specs=[spec] * 3,
        out_shape=[jax.ShapeDtypeStruct((r, c), F32)] * 3, name=name, compiler_params=_cp("parallel"))(w, g, m, v)


def _add2(a, b, name):
    n, r, c = a.shape
    tr = 464 if r % 464 == 0 else r

    def body(a_ref, b_ref, o_ref):
        o_ref[...] = (a_ref[...].astype(F32) + b_ref[...].astype(F32)).astype(BF16)

    spec = pl.BlockSpec((1, tr, c), lambda k, i: (k, i, 0))
    return pl.pallas_call(body, grid=(n, r // tr), in_specs=[spec, spec], out_specs=spec,
                          out_shape=jax.ShapeDtypeStruct((n, r, c), BF16), name=name,
                          compiler_params=_cp("parallel", "parallel"))(a, b)


def _sum4(a, name):
    _, r, c = a.shape
    tr = 464 if r % 464 == 0 else r

    def body(a_ref, o_ref):
        o_ref[...] = ((a_ref[0].astype(F32) + a_ref[1].astype(F32)) + a_ref[2].astype(F32)) + a_ref[3].astype(F32)

    return pl.pallas_call(body, grid=(r // tr,), in_specs=[pl.BlockSpec((4, tr, c), lambda i: (0, i, 0))],
                          out_specs=pl.BlockSpec((tr, c), lambda i: (i, 0)),
                          out_shape=jax.ShapeDtypeStruct((r, c), F32), name=name, compiler_params=_cp("parallel"))(a)


def _adam_small(w, gathered, m, v):
    c1 = 1.0 - ADAM_B1 ** ADAM_STEP
    c2 = 1.0 - ADAM_B2 ** ADAM_STEP

    def body(w_ref, g_ref, m_ref, v_ref, gs_ref, d_ref, nm_ref, nv_ref):
        gv = g_ref[0]
        for dev in range(1, N_DEV):
            gv = gv + g_ref[dev]
        gs_ref[...] = gv
        nm = ADAM_B1 * m_ref[...] + (1.0 - ADAM_B1) * gv
        nv = ADAM_B2 * v_ref[...] + (1.0 - ADAM_B2) * (gv * gv)
        d_ref[...] = -ADAM_LR * ((nm / c1) / (jnp.sqrt(nv / c2) + ADAM_EPS) + ADAM_WD * w_ref[...])
        nm_ref[...] = nm
        nv_ref[...] = nv

    return pl.pallas_call(body, out_shape=[jax.ShapeDtypeStruct((8, D), F32)] * 4, name="adam_small")(w, gathered, m, v)


def _place():
    return lax.axis_index("x"), lax.axis_index("y"), lax.axis_index("c")


def _other_chips(x, y):
    return [(1 - x, y), (x, 1 - y), (1 - x, 1 - y)]


def _gather_shards(p):
    r, c = p.shape
    hr = r // 2

    def body(p_ref, out_ref, send_sems, recv_sems, local_sem):
        x, y, cc = _place()
        sibling = (x, y, 1 - cc)
        chips = _other_chips(x, y)

        def half(chip, core):
            return out_ref.at[2 * chip[0] + chip[1], pl.ds(core * hr, hr), :]

        def copy(k, chip, core, to, src=None):
            return pltpu.make_async_remote_copy(
                src_ref=half(chip, core) if src is None else src, dst_ref=half(chip, core),
                send_sem=send_sems.at[k], recv_sem=recv_sems.at[k], device_id=to, device_id_type=MESH)

        mine = pltpu.make_async_copy(p_ref, out_ref.at[2 * x + y], local_sem)
        mine.start()
        my_half = p_ref.at[pl.ds(cc * hr, hr), :]
        first = [copy(j, (x, y), cc, (*chip, cc), src=my_half) for j, chip in enumerate(chips)]
        for cp in first:
            cp.start()
        passed = [copy(3 + j, chip, cc, sibling) for j, chip in enumerate(chips)]
        for j, chip in enumerate(chips):
            copy(j, chip, cc, (x, y, cc)).wait_recv()
            passed[j].start()
        for j, chip in enumerate(chips):
            copy(3 + j, chip, 1 - cc, (x, y, cc)).wait_recv()
        for cp in first + passed:
            cp.wait_send()
        mine.wait()

    any_spec = pl.BlockSpec(memory_space=pl.ANY)
    return pl.pallas_call(
        body, out_shape=jax.ShapeDtypeStruct((N_CHIPS, r, c), p.dtype), in_specs=[any_spec], out_specs=any_spec,
        scratch_shapes=[pltpu.SemaphoreType.DMA((6,)), pltpu.SemaphoreType.DMA((6,)), pltpu.SemaphoreType.DMA],
        name="gather_shards")(p)


def _swap_halves(g):
    n, r, c = g.shape
    hr = r // 2

    def body(g_ref, out_ref, send_sem, recv_sem):
        x, y, cc = _place()
        cp = pltpu.make_async_remote_copy(
            src_ref=g_ref.at[:, pl.ds((1 - cc) * hr, hr), :], dst_ref=out_ref,
            send_sem=send_sem, recv_sem=recv_sem, device_id=(x, y, 1 - cc), device_id_type=MESH)
        cp.start()
        cp.wait()

    any_spec = pl.BlockSpec(memory_space=pl.ANY)
    return pl.pallas_call(
        body, out_shape=jax.ShapeDtypeStruct((n, hr, c), g.dtype), in_specs=[any_spec], out_specs=any_spec,
        scratch_shapes=[pltpu.SemaphoreType.DMA, pltpu.SemaphoreType.DMA], name="swap_halves")(g)


def _scatter_partials(p):
    n, hr, c = p.shape

    def body(p_ref, out_ref, send_sems, recv_sems, local_sem):
        x, y, cc = _place()
        me = 2 * x + y
        sibling = (x, y, 1 - cc)
        chips = _other_chips(x, y)
        ids = [2 * chip[0] + chip[1] for chip in chips]

        def land(src, core):
            return out_ref.at[src, pl.ds(core * hr, hr), :]

        def copy(k, src_ref, dst_ref, to):
            return pltpu.make_async_remote_copy(src_ref=src_ref, dst_ref=dst_ref, send_sem=send_sems.at[k],
                                                recv_sem=recv_sems.at[k], device_id=to, device_id_type=MESH)

        mine = pltpu.make_async_copy(p_ref.at[me], land(me, cc), local_sem)
        mine.start()
        sends = [copy(j, p_ref.at[ids[j]], land(me, cc), (*chip, cc)) for j, chip in enumerate(chips)]
        sends.append(copy(3, p_ref.at[me], land(me, cc), sibling))
        for cp in sends:
            cp.start()
        passed = [copy(4 + j, land(ids[j], cc), land(ids[j], cc), sibling) for j in range(3)]
        for j in range(3):
            copy(j, p_ref.at[me], land(ids[j], cc), (x, y, cc)).wait_recv()
            passed[j].start()
        copy(3, p_ref.at[me], land(me, 1 - cc), (x, y, cc)).wait_recv()
        for j in range(3):
            copy(4 + j, p_ref.at[me], land(ids[j], 1 - cc), (x, y, cc)).wait_recv()
        for cp in sends + passed:
            cp.wait_send()
        mine.wait()

    any_spec = pl.BlockSpec(memory_space=pl.ANY)
    return pl.pallas_call(
        body, out_shape=jax.ShapeDtypeStruct((n, 2 * hr, c), p.dtype), in_specs=[any_spec], out_specs=any_spec,
        scratch_shapes=[pltpu.SemaphoreType.DMA((7,)), pltpu.SemaphoreType.DMA((7,)), pltpu.SemaphoreType.DMA],
        name="scatter_partials")(p)


def _gather_small(blk):
    m, n = blk.shape

    def body(x_ref, out_ref, send_sems, recv_sems, local_sem):
        x, y, cc = _place()
        me, sibling = (x, y, cc), (x, y, 1 - cc)
        chips = _other_chips(x, y)

        def slot(px, py, pc):
            return out_ref.at[4 * px + 2 * py + pc]

        def copy(k, block, to, src=None):
            return pltpu.make_async_remote_copy(
                src_ref=slot(*block) if src is None else src, dst_ref=slot(*block),
                send_sem=send_sems.at[k], recv_sem=recv_sems.at[k], device_id=to, device_id_type=MESH)

        mine = pltpu.make_async_copy(x_ref, slot(*me), local_sem)
        mine.start()
        first = [copy(0, me, sibling, src=x_ref)]
        first += [copy(1 + j, me, (*chip, cc), src=x_ref) for j, chip in enumerate(chips)]
        for cp in first:
            cp.start()
        passed = [copy(4 + j, (*chip, cc), sibling) for j, chip in enumerate(chips)]
        for j, chip in enumerate(chips):
            copy(1 + j, (*chip, cc), me).wait_recv()
            passed[j].start()
        copy(0, sibling, me).wait_recv()
        for j, chip in enumerate(chips):
            copy(4 + j, (*chip, 1 - cc), me).wait_recv()
        for cp in first + passed:
            cp.wait_send()
        mine.wait()

    vmem = pl.BlockSpec(memory_space=pltpu.VMEM)
    return pl.pallas_call(
        body, out_shape=jax.ShapeDtypeStruct((N_DEV, m, n), blk.dtype), in_specs=[vmem], out_specs=vmem,
        scratch_shapes=[pltpu.SemaphoreType.DMA((7,)), pltpu.SemaphoreType.DMA((7,)), pltpu.SemaphoreType.DMA],
        name="gather_small")(blk)


def _shard_shape(name, full_shape):
    shp = list(full_shape)
    shp[SHARD_AXIS[name]] //= N_CHIPS
    return tuple(shp)


FULL_SHAPES = {"w_in": (D, 6680), "w_alpha_up": (16, 256), "w_mem_kv": (D, D), "w_gla_o": (512, D), "w_fox_o": (512, D),
               "w_mem_o": (512, D), "w_out": (D, D), "w_ff1": (D, 4 * D), "w_ff2": (4 * D, D)}


def _pack(shards, dtype):
    flat = [shards[n].astype(dtype).reshape(-1) for n in WEIGHTS]
    total = sum(f.shape[0] for f in flat)
    flat.append(jnp.zeros((PACK_ROWS * PACK_W - total,), dtype))
    return jnp.concatenate(flat).reshape(PACK_ROWS, PACK_W)


def _unpack(packed):
    flat = packed.reshape(-1)
    out, off = {}, 0
    for n in WEIGHTS:
        shp = _shard_shape(n, FULL_SHAPES[n])
        size = shp[0] * shp[1]
        out[n] = flat[off:off + size].reshape(shp)
        off += size
    return out


def _split_shards(name, full):
    return jnp.split(full, N_CHIPS, axis=SHARD_AXIS[name])


def _pack_small(vals, scalar=None):
    row4 = jnp.concatenate([vals["b_alpha"].reshape(-1), vals["b_forget"].reshape(-1), jnp.zeros((D - 264,), F32)])
    row5 = jnp.concatenate([vals["g_gla_head"].reshape(-1), jnp.zeros((D - 512,), F32)])
    row6 = jnp.zeros((D,), F32) if scalar is None else jnp.broadcast_to(scalar, (D,))
    rows = [vals["g_mix"].reshape(-1), vals["g_mem"].reshape(-1), vals["g_ffn"].reshape(-1), vals["g_final"].reshape(-1),
            row4, row5, row6, jnp.zeros((D,), F32)]
    return jnp.stack(rows)


def _unpack_small(blk):
    return {"g_mix": blk[0].reshape(1, D), "g_mem": blk[1].reshape(1, D), "g_ffn": blk[2].reshape(1, D),
            "g_final": blk[3].reshape(D), "b_alpha": blk[4, 0:256].reshape(1, 256), "b_forget": blk[4, 256:264].reshape(1, 8),
            "g_gla_head": blk[5, 0:512].reshape(1, 4, 128)}


def _local_step(x, mem, target, wb, small):
    s = x.shape[0]
    nm = mem.shape[0]
    t = _row_tile(s)
    nb = s // t
    w_in = wb["w_in"]
    w_main = jnp.concatenate([w_in[:, 3608:6680], w_in[:, 0:1536], w_in[:, 1552:3088], w_in[:, 3096:3608]], axis=1)
    w_e = jnp.concatenate([w_in[:, 1536:1552], w_in[:, 3088:3096], jnp.zeros((D, PE_W - 24), BF16)], axis=1)
    w_in_pt = jnp.concatenate([w_main, w_e, jnp.zeros((D, DP_W - PM_W - PE_W), BF16)], axis=1).T
    wau_p = jnp.concatenate([wb["w_alpha_up"], jnp.zeros((PE_W - 16, 256), BF16)], axis=0)
    b_alpha = small["b_alpha"].reshape(1, 256)
    bias_e = jnp.concatenate([jnp.zeros((FF_LANE,), F32), small["b_forget"].reshape(-1),
                              jnp.zeros((PE_W - FF_LANE - 8,), F32)]).reshape(1, PE_W)
    g_mix, g_mem, g_ffn = small["g_mix"].reshape(1, D), small["g_mem"].reshape(1, D), small["g_ffn"].reshape(1, D)
    g_final = small["g_final"].reshape(1, D)
    g_head = small["g_gla_head"].reshape(1, 512)

    u, r1 = _rms_fwd(x, g_mix, "norm_mix")
    big = min(s, 1024)
    pm = _mm_nn(u, w_main, out_dtype=BF16, tm=big, tn=PM_W // 4, tk=D, name="proj_main")
    pe = _mm_nn(u, w_e, out_dtype=F32, tm=t, tn=PE_W, tk=D, name="proj_narrow")
    o_gla, states = _gla_fwd(pm, pe, wau_p, b_alpha)
    fcum = _fcum_fwd(pe, bias_e)
    tb = _fox_tables()
    qf_aug, k_aug, v_aug, vt, qt = _fox_prep(pm, fcum, None, tb, backward=False)
    o_fox, lse = _fox_fwd(k_aug, qf_aug, vt)
    mn, rm = _rms_fwd(mem, g_mem, "norm_mem")
    mkv = _mm_nn(mn, wb["w_mem_kv"], out_dtype=BF16, tm=nm, tn=512, tk=D, name="mem_kv")
    o_mem = _mem_attn_fwd(pm, mkv)
    merged, h1, u2, r2 = _merge_fwd(x, pm, o_gla, o_fox, o_mem, g_head, wb["w_gla_o"], wb["w_fox_o"], wb["w_mem_o"],
                                    wb["w_out"], g_ffn)
    a = _mm_nn(u2, wb["w_ff1"], out_dtype=BF16, tm=big, tn=1024, tk=D, name="ff1")
    dh2, dh2b, loss8, dgfin8 = _ff2_loss(a, wb["w_ff2"], h1, g_final, target)
    loss = 0.5 * jnp.sum(loss8) / D

    da = _mm_nn(dh2b, wb["w_ff2"].T, out_dtype=BF16, tm=t, tn=1024, tk=D, name="d_act",
                epi=lambda acc, at: acc * (2.0 * jnp.maximum(at.astype(F32), 0.0)), extra=a)
    gw = {}
    gw["w_ff2"] = _mm_tn(a, dh2b, tm=1024, tn=D, ts=t, name="dw_ff2", a_fn=_relu2_bf16)
    gw["w_ff1"] = _mm_tn(u2, da, tm=D, tn=1024, ts=t, name="dw_ff1")
    dh1, dh1b, dgffn8 = _mm_norm_bwd(da, wb["w_ff1"].T, h1, r2, g_ffn, dh2, name="d_h1", want_bf16=True)
    gw["w_out"] = _mm_tn(merged, dh1b, tm=D, tn=D, ts=t, name="dw_out")
    (dgates, dgg, do_gla, do_aug, do_t, do_mem, gw["w_gla_o"], gw["w_fox_o"], gw["w_mem_o"], dgh8) = _merge_bwd(
        dh1b, pm, o_gla, o_fox, o_mem, g_head, wb["w_gla_o"], wb["w_fox_o"], wb["w_mem_o"],
        wb["w_gla_o"].T, wb["w_fox_o"].T, wb["w_mem_o"].T, wb["w_out"].T, tb["spread"], tb["d_to_do"])
    dgq, dgk, dgv, de_gla, dwau_p, dba8 = _gla_bwd(pm, pe, wau_p, wau_p.T, b_alpha, do_gla, states)
    gw["w_alpha_up"] = dwau_p[0:16, :]
    q_aug = _fox_prep(pm, fcum, lse.reshape(8, s), tb, backward=True)
    dq_aug, dfk_t, dfv_t, dfcol = _fox_bwd(q_aug, do_aug, qt, do_t, k_aug, v_aug)
    dfq, dfk, dfv, df = _fox_post(dq_aug, dfk_t, dfv_t, dfcol.reshape(8, s), tb)
    de_fox, dbf8 = _fcum_bwd(pe, bias_e, df)
    dmq, dmk, dmv = _mem_attn_bwd(pm, mkv, do_mem)
    dmkv = jnp.concatenate([dmk, dmv], axis=1).astype(BF16)
    gw["w_mem_kv"] = _mm_tn(mn, dmkv, tm=D, tn=D, ts=nm, name="dw_mem_kv")
    dmn_g = _mm_nn(dmkv, wb["w_mem_kv"].T, out_dtype=F32, tm=nm, tn=D, tk=D, name="d_mem_norm")
    dgmem8 = _gain_grad(dmn_g, mem, rm, "dg_mem")
    dproj = jnp.concatenate(
        [dgates, dgq, dgk, dgv, dgg, dfq, dfk, dfv, dmq,
         (de_gla + de_fox).astype(BF16), jnp.zeros((s, DP_W - PM_W - PE_W), BF16)], axis=1)
    dwp = _mm_tn(u, dproj, tm=D, tn=1024, ts=t, name="dw_in")
    gw["w_in"] = jnp.concatenate([dwp[:, 3072:4608], dwp[:, PM_W:PM_W + 16], dwp[:, 4608:6144],
                                  dwp[:, PM_W + 16:PM_W + 24], dwp[:, 6144:6656], dwp[:, 0:3072]], axis=1)
    grad_x, dgmix8 = _mm_norm_bwd(dproj, w_in_pt, x, r1, g_mix, dh1, name="d_x", want_bf16=False)
    gs = {"g_mix": dgmix8.sum(0), "g_mem": dgmem8.sum(0), "g_ffn": dgffn8.sum(0), "g_final": dgfin8.sum(0),
          "b_alpha": dba8.sum(0), "b_forget": dbf8.sum(0)[FF_LANE:FF_LANE + 8], "g_gla_head": dgh8.sum(0)}
    return loss, grad_x, gw, gs


def kernel(x, mem, g_mix, w_in, w_alpha_up, b_alpha, b_forget, g_gla_head, g_mem, w_mem_kv, w_gla_o, w_fox_o, w_mem_o, w_out, g_ffn, w_ff1, w_ff2, g_final, loss_target, m_g_mix, m_w_in, m_w_alpha_up, m_b_alpha, m_b_forget, m_g_gla_head, m_g_mem, m_w_mem_kv, m_w_gla_o, m_w_fox_o, m_w_mem_o, m_w_out, m_g_ffn, m_w_ff1, m_w_ff2, m_g_final, v_g_mix, v_w_in, v_w_alpha_up, v_b_alpha, v_b_forget, v_g_gla_head, v_g_mem, v_w_mem_kv, v_w_gla_o, v_w_fox_o, v_w_mem_o, v_w_out, v_g_ffn, v_w_ff1, v_w_ff2, v_g_final):
    args = dict(locals())
    w_sh = {n: args[n][0] for n in WEIGHTS}
    m_sh = {n: args["m_" + n][0] for n in WEIGHTS}
    v_sh = {n: args["v_" + n][0] for n in WEIGHTS}
    small = {n: args[n] for n in SMALL}

    gathered = _gather_shards(_pack(w_sh, BF16))
    parts = [_unpack(gathered[k]) for k in range(N_CHIPS)]
    wb = {n: jnp.concatenate([parts[k][n] for k in range(N_CHIPS)], axis=SHARD_AXIS[n]) for n in WEIGHTS}

    loss, grad_x, gw, gs = _local_step(x[0], mem[0], loss_target[0], wb, small)

    g_by_chip = [_split_shards(n, gw[n]) for n in WEIGHTS]
    g_packed = jnp.stack([_pack({n: g_by_chip[i][k] for i, n in enumerate(WEIGHTS)}, BF16) for k in range(N_CHIPS)])
    hr = PACK_ROWS // 2
    cc = lax.axis_index("c")
    from_sibling = _swap_halves(g_packed)
    my_half = lax.dynamic_slice_in_dim(g_packed, cc * hr, hr, axis=1)
    chip_sum = _add2(my_half, from_sibling, "chip_sum")
    g_shard = _sum4(_scatter_partials(chip_sum), "shard_sum")

    g_out = _unpack(g_shard)
    d_out, m_out, v_out = {}, {}, {}
    for n in WEIGHTS:
        d_out[n], m_out[n], v_out[n] = _adam(w_sh[n], g_out[n], m_sh[n], v_sh[n], "adam_" + n)

    small_all = _gather_small(_pack_small(gs, loss))
    sm = {n: args["m_" + n] for n in SMALL}
    sv = {n: args["v_" + n] for n in SMALL}
    gs_sum, sd, snm, snv = _adam_small(_pack_small(small), small_all, _pack_small(sm), _pack_small(sv))
    gs_o, sd_o, snm_o, snv_o = _unpack_small(gs_sum), _unpack_small(sd), _unpack_small(snm), _unpack_small(snv)

    names = ["g_mix", "w_in", "w_alpha_up", "b_alpha", "b_forget", "g_gla_head", "g_mem", "w_mem_kv", "w_gla_o", "w_fox_o",
             "w_mem_o", "w_out", "g_ffn", "w_ff1", "w_ff2", "g_final"]

    def pick(big, sml, n):
        return big[n][None] if n in big else sml[n]

    outs = [gs_sum[6, 0], grad_x[None]]
    for big, sml in ((g_out, gs_o), (d_out, sd_o), (m_out, snm_o), (v_out, snv_o)):
        outs += [pick(big, sml, n) for n in names]
    return tuple(outs)
```

```python
import functools

import numpy as np
import jax
import jax.numpy as jnp
from jax import lax
from jax.experimental import pallas as pl
from jax.experimental.pallas import tpu as pltpu

F32 = jnp.float32
BF16 = jnp.bfloat16
HI = lax.Precision.HIGHEST
MESH = pl.DeviceIdType.MESH

EPS = 1e-6
D = 1024
CHUNK = 64
GLA_TAU = 16.0
N_CHIPS = 4
N_DEV = 8
VMEM_LIMIT_BYTES = 56 * 1024 * 1024

ADAM_LR, ADAM_B1, ADAM_B2, ADAM_EPS, ADAM_WD, ADAM_STEP = 0.001, 0.9, 0.999, 1e-08, 0.01, 10

PM_W = 6656
PE_W = 128
DP_W = 7168
C_GQ, C_GK, C_GV, C_GG, C_FQ, C_FK, C_FV, C_MQ = 3072, 3328, 3584, 4096, 4608, 5120, 5632, 6144
FF_LANE = 16

WEIGHTS = ("w_in", "w_alpha_up", "w_mem_kv", "w_gla_o", "w_fox_o", "w_mem_o", "w_out", "w_ff1", "w_ff2")
SHARD_AXIS = {"w_in": 1, "w_alpha_up": 1, "w_mem_kv": 0, "w_gla_o": 1, "w_fox_o": 1, "w_mem_o": 1, "w_out": 0,
              "w_ff1": 1, "w_ff2": 0}
SMALL = ("g_mix", "g_mem", "g_ffn", "g_final", "b_alpha", "b_forget", "g_gla_head")
PACK_W = 1024
PACK_ROWS_A = 2048
PACK_ROWS_B = 3104
PACK_ROWS = PACK_ROWS_A + PACK_ROWS_B


def _cp(*sem):
    return pltpu.CompilerParams(dimension_semantics=sem, vmem_limit_bytes=VMEM_LIMIT_BYTES)


def _dot(a, b, **kw):
    return jnp.dot(a, b, preferred_element_type=F32, **kw)


def _dot_nt(a, b, **kw):
    return lax.dot_general(a, b, (((1,), (1,)), ((), ())), preferred_element_type=F32, **kw)


def _dot_tn(a, b, **kw):
    return lax.dot_general(a, b, (((0,), (0,)), ((), ())), preferred_element_type=F32, **kw)


def _sigmoid(x):
    return 1.0 / (1.0 + jnp.exp(-x))


def _log_sigmoid(x):
    return -(jnp.maximum(-x, 0.0) + jnp.log1p(jnp.exp(-jnp.abs(x))))


def _fold8(x):
    m, n = x.shape
    return x.reshape(m // 8, 8, n).sum(axis=0)


def _iota(shape, dim):
    return lax.broadcasted_iota(jnp.int32, shape, dim)


def _row_tile(s):
    return min(s, 512)


class _Side:
    def __init__(self, inputs, out_shape, scratch, ops):
        self.inputs, self.out_shape, self.scratch, self.ops = list(inputs), list(out_shape), list(scratch), ops


ANY_SPEC = pl.BlockSpec(memory_space=pl.ANY)


def _mm_nn(a, b, *, out_dtype, tm, tn, tk, name, a_fn=None, epi=None, extra=None, side=None):
    m, k = a.shape
    _, n = b.shape
    nk = k // tk
    n_in = 2 + (extra is not None)
    n_sin = 0 if side is None else len(side.inputs)
    n_sout = 0 if side is None else len(side.out_shape)

    def body_one(*refs):
        a_ref, b_ref = refs[0], refs[1]
        o_ref = refs[n_in + n_sin]
        if side is not None:
            start, finish = side.ops(refs[n_in:n_in + n_sin], refs[n_in + n_sin + 1:n_in + n_sin + 1 + n_sout],
                                     refs[n_in + n_sin + 1 + n_sout:])
            pl.when((pl.program_id(0) == 0) & (pl.program_id(1) == 0))(start)
        at = a_ref[...] if a_fn is None else a_fn(a_ref[...])
        r = _dot(at, b_ref[...])
        if epi is not None:
            r = epi(r, None if extra is None else refs[2][...])
        o_ref[...] = r.astype(out_dtype)
        if side is not None:
            pl.when((pl.program_id(0) == m // tm - 1) & (pl.program_id(1) == n // tn - 1))(finish)

    if nk == 1:
        in_specs = [pl.BlockSpec((tm, k), lambda i, j: (i, 0)), pl.BlockSpec((k, tn), lambda i, j: (0, j))]
        args = [a, b]
        if extra is not None:
            in_specs.append(pl.BlockSpec((tm, tn), lambda i, j: (i, j)))
            args.append(extra)
        out_specs = pl.BlockSpec((tm, tn), lambda i, j: (i, j))
        out_shape = jax.ShapeDtypeStruct((m, n), out_dtype)
        if side is None:
            return pl.pallas_call(
                body_one, grid=(m // tm, n // tn), in_specs=in_specs, out_specs=out_specs, out_shape=out_shape,
                name=name, compiler_params=_cp("parallel", "parallel"))(*args)
        return pl.pallas_call(
            body_one, grid=(m // tm, n // tn), in_specs=in_specs + [ANY_SPEC] * n_sin,
            out_specs=[out_specs] + [ANY_SPEC] * n_sout, out_shape=[out_shape] + side.out_shape,
            scratch_shapes=side.scratch, name=name, compiler_params=_cp("arbitrary", "arbitrary"))(*args, *side.inputs)
    assert side is None

    def body(*refs):
        if extra is None:
            a_ref, b_ref, o_ref, acc = refs
            x_ref = None
        else:
            a_ref, b_ref, x_ref, o_ref, acc = refs
        kk = pl.program_id(2)

        @pl.when(kk == 0)
        def _():
            acc[...] = jnp.zeros_like(acc)

        at = a_ref[...]
        if a_fn is not None:
            at = a_fn(at)
        acc[...] += _dot(at, b_ref[...])

        @pl.when(kk == nk - 1)
        def _():
            r = acc[...]
            if epi is not None:
                r = epi(r, None if x_ref is None else x_ref[...])
            o_ref[...] = r.astype(out_dtype)

    in_specs = [pl.BlockSpec((tm, tk), lambda i, j, kk: (i, kk)), pl.BlockSpec((tk, tn), lambda i, j, kk: (kk, j))]
    args = [a, b]
    if extra is not None:
        in_specs.append(pl.BlockSpec((tm, tn), lambda i, j, kk: (i, j)))
        args.append(extra)
    return pl.pallas_call(
        body, grid=(m // tm, n // tn, nk), in_specs=in_specs,
        out_specs=pl.BlockSpec((tm, tn), lambda i, j, kk: (i, j)),
        out_shape=jax.ShapeDtypeStruct((m, n), out_dtype),
        scratch_shapes=[pltpu.VMEM((tm, tn), F32)], name=name,
        compiler_params=_cp("parallel", "parallel", "arbitrary"))(*args)


def _mm_tn(a, b, *, tm, tn, ts, name, a_fn=None):
    s, m = a.shape
    _, n = b.shape
    ns = s // ts

    def body(a_ref, b_ref, o_ref, acc):
        kk = pl.program_id(2)

        @pl.when(kk == 0)
        def _():
            acc[...] = jnp.zeros_like(acc)

        at = a_ref[...]
        if a_fn is not None:
            at = a_fn(at)
        acc[...] += _dot_tn(at, b_ref[...])

        @pl.when(kk == ns - 1)
        def _():
            o_ref[...] = acc[...]

    return pl.pallas_call(
        body, grid=(m // tm, n // tn, ns),
        in_specs=[pl.BlockSpec((ts, tm), lambda i, j, kk: (kk, i)), pl.BlockSpec((ts, tn), lambda i, j, kk: (kk, j))],
        out_specs=pl.BlockSpec((tm, tn), lambda i, j, kk: (i, j)),
        out_shape=jax.ShapeDtypeStruct((m, n), F32),
        scratch_shapes=[pltpu.VMEM((tm, tn), F32)], name=name,
        compiler_params=_cp("parallel", "parallel", "arbitrary"))(a, b)


def _relu2_bf16(t):
    r = jnp.maximum(t.astype(F32), 0.0)
    return (r * r).astype(BF16)


def _rms_fwd(x, g, name):
    s, d = x.shape
    tm = min(s, 512)

    def body(x_ref, g_ref, u_ref, r_ref):
        xv = x_ref[...]
        r = lax.rsqrt(jnp.mean(xv * xv, axis=-1, keepdims=True) + EPS)
        u_ref[...] = ((xv * r) * g_ref[...]).astype(BF16)
        r_ref[...] = r

    return pl.pallas_call(
        body, grid=(s // tm,),
        in_specs=[pl.BlockSpec((tm, d), lambda i: (i, 0)), pl.BlockSpec((1, d), lambda i: (0, 0))],
        out_specs=[pl.BlockSpec((tm, d), lambda i: (i, 0)), pl.BlockSpec((tm, 1), lambda i: (i, 0))],
        out_shape=[jax.ShapeDtypeStruct((s, d), BF16), jax.ShapeDtypeStruct((s, 1), F32)],
        name=name, compiler_params=_cp("parallel"))(x, g)


def _mm_norm_bwd(a, b, xin, r, g, dres, *, name, want_bf16, side=None):
    s, k = a.shape
    tm = min(s, 256)
    n_out = 3 if want_bf16 else 2
    n_sin = 0 if side is None else len(side.inputs)
    n_sout = 0 if side is None else len(side.out_shape)

    def body(a_ref, b_ref, x_ref, r_ref, g_ref, dres_ref, *rest):
        outs = rest[n_sin:n_sin + n_out]
        dx_ref, dg_ref = outs[0], outs[-1]
        if side is not None:
            start, finish = side.ops(rest[:n_sin], rest[n_sin + n_out:n_sin + n_out + n_sout], rest[n_sin + n_out + n_sout:])
            pl.when(pl.program_id(0) == 0)(start)

        @pl.when(pl.program_id(0) == 0)
        def _():
            dg_ref[...] = jnp.zeros_like(dg_ref)

        du = _dot(a_ref[...], b_ref[...])
        xn = x_ref[...] * r_ref[...]
        dg_ref[...] += _fold8(du * xn)
        dxn = du * g_ref[...]
        dx = dres_ref[...] + r_ref[...] * (dxn - xn * jnp.mean(dxn * xn, axis=-1, keepdims=True))
        dx_ref[...] = dx
        if want_bf16:
            outs[1][...] = dx.astype(BF16)
        if side is not None:
            pl.when(pl.program_id(0) == s // tm - 1)(finish)

    row = lambda i: (i, 0)
    const = lambda i: (0, 0)
    out_specs = [pl.BlockSpec((tm, D), row)]
    out_shape = [jax.ShapeDtypeStruct((s, D), F32)]
    if want_bf16:
        out_specs.append(pl.BlockSpec((tm, D), row))
        out_shape.append(jax.ShapeDtypeStruct((s, D), BF16))
    out_specs.append(pl.BlockSpec((8, D), const))
    out_shape.append(jax.ShapeDtypeStruct((8, D), F32))
    side_in = [] if side is None else side.inputs
    return pl.pallas_call(
        body, grid=(s // tm,),
        in_specs=[pl.BlockSpec((tm, k), row), pl.BlockSpec((k, D), const, pipeline_mode=pl.Buffered(1)),
                  pl.BlockSpec((tm, D), row), pl.BlockSpec((tm, 1), row), pl.BlockSpec((1, D), const),
                  pl.BlockSpec((tm, D), row)] + [ANY_SPEC] * n_sin,
        out_specs=out_specs + [ANY_SPEC] * n_sout, out_shape=out_shape + ([] if side is None else side.out_shape),
        scratch_shapes=[] if side is None else side.scratch,
        name=name, compiler_params=_cp("arbitrary"))(a, b, xin, r, g, dres, *side_in)


def _gla_consts():
    lmask = _iota((4 * CHUNK, CHUNK), 0) % CHUNK >= _iota((4 * CHUNK, CHUNK), 1)
    hmask = _iota((256, 256), 0) // CHUNK == _iota((256, 256), 1) // CHUNK
    bd = _iota((256, 512), 0) // CHUNK == _iota((256, 512), 1) // 128
    return lmask, hmask, bd


def _fold_heads(x):
    return x[0:64] + x[64:128] + x[128:192] + x[192:256]


def _gla_chunk(lac, qc, kc):
    tri = (_iota((CHUNK, CHUNK), 0) >= _iota((CHUNK, CHUNK), 1)).astype(F32)
    b = _dot(tri, lac, precision=HI)
    bl = b[CHUNK - 1:CHUNK, :]
    ep, en, ek = jnp.exp(b), jnp.exp(-b), jnp.exp(bl - b)
    decb = jnp.exp(_dot_tn(lac, jnp.ones((CHUNK, 128), F32), precision=HI))
    decb = jnp.concatenate([decb] * 4, axis=1)
    return bl, ep, en, ek, decb, qc * ep, qc * en, kc * en, kc * ep, kc * ek


def _gla_fwd(pm, pe, wau_p, b_alpha):
    s = pm.shape[0]
    t = _row_tile(s)
    nc = t // CHUNK

    def body(q_ref, k_ref, v_ref, e_ref, wau_ref, ba_ref, o_ref, st_ref, state, la_scr):
        @pl.when(pl.program_id(0) == 0)
        def _():
            state[...] = jnp.zeros_like(state)

        z = _dot(e_ref[...].astype(BF16), wau_ref[...]) + ba_ref[...]
        la_scr[...] = _log_sigmoid(z) * (1.0 / GLA_TAU)
        lmask, hmask, bd = _gla_consts()

        def chunk(c, carry):
            rows = pl.ds(pl.multiple_of(c * CHUNK, CHUNK), CHUNK)
            qc = q_ref[rows, :].astype(F32) * 0.125
            kc = k_ref[rows, :].astype(F32)
            vc = v_ref[rows, :]
            _, _, _, _, decb, qp, qn, kn, kp, kk = _gla_chunk(la_scr[rows, :], qc, kc)
            qs = jnp.where(hmask, jnp.concatenate([qp] * 4, axis=0), 0.0).astype(BF16)
            qns = jnp.where(hmask, jnp.concatenate([qn] * 4, axis=0), 0.0).astype(BF16)
            attn = jnp.where(lmask, _dot_nt(qs, kn.astype(BF16)), _dot_nt(qns, kp.astype(BF16))).astype(BF16)
            st = state[...]
            o_intra = _fold_heads(jnp.where(bd, _dot(attn, vc), 0.0))
            o_ref[rows, :] = o_intra + _dot(qp.astype(BF16), st.astype(BF16))
            for h in range(4):
                st_ref[c, :, 128 * h:128 * (h + 1)] = st[64 * h:64 * (h + 1), 128 * h:128 * (h + 1)]
            kv = jnp.where(bd, _dot_tn(kk.astype(BF16), vc), 0.0)
            state[...] = st * decb + kv
            return carry

        lax.fori_loop(0, nc, chunk, 0)

    return pl.pallas_call(
        body, grid=(s // t,),
        in_specs=[pl.BlockSpec((t, 256), lambda i: (i, C_GQ // 256)), pl.BlockSpec((t, 256), lambda i: (i, C_GK // 256)),
                  pl.BlockSpec((t, 512), lambda i: (i, C_GV // 512)), pl.BlockSpec((t, PE_W), lambda i: (i, 0)),
                  pl.BlockSpec((PE_W, 256), lambda i: (0, 0)), pl.BlockSpec((1, 256), lambda i: (0, 0))],
        out_specs=[pl.BlockSpec((t, 512), lambda i: (i, 0)), pl.BlockSpec((nc, CHUNK, 512), lambda i: (i, 0, 0))],
        out_shape=[jax.ShapeDtypeStruct((s, 512), F32), jax.ShapeDtypeStruct((s // CHUNK, CHUNK, 512), F32)],
        scratch_shapes=[pltpu.VMEM((256, 512), F32), pltpu.VMEM((t, 256), F32)],
        name="gla_fwd", compiler_params=_cp("arbitrary"))(pm, pm, pm, pe, wau_p, b_alpha)


def _gla_bwd(pm, pe, wau_p, wau_pt, b_alpha, do, states):
    s = pm.shape[0]
    t = _row_tile(s)
    nc = t // CHUNK
    nb = s // t

    def body(q_ref, k_ref, v_ref, e_ref, wau_ref, waut_ref, ba_ref, do_ref, st_ref,
             dq_ref, dk_ref, dv_ref, de_ref, dwau_ref, dba_ref, gstate, la_scr, dla_scr):
        @pl.when(pl.program_id(0) == 0)
        def _():
            gstate[...] = jnp.zeros_like(gstate)
            dwau_ref[...] = jnp.zeros_like(dwau_ref)
            dba_ref[...] = jnp.zeros_like(dba_ref)

        eb = e_ref[...].astype(BF16)
        z = _dot(eb, wau_ref[...]) + ba_ref[...]
        la_scr[...] = _log_sigmoid(z) * (1.0 / GLA_TAU)
        lmask, hmask, bd = _gla_consts()
        triu = (_iota((CHUNK, CHUNK), 0) <= _iota((CHUNK, CHUNK), 1)).astype(F32)
        last_row = _iota((CHUNK, 256), 0) == CHUNK - 1

        def chunk(cc, carry):
            c = nc - 1 - cc
            rows = pl.ds(pl.multiple_of(c * CHUNK, CHUNK), CHUNK)
            qc = q_ref[rows, :].astype(F32) * 0.125
            kc = k_ref[rows, :].astype(F32)
            vc = v_ref[rows, :]
            dob = do_ref[rows, :]
            bl, ep, en, ek, decb, qp, qn, kn, kp, kk = _gla_chunk(la_scr[rows, :], qc, kc)
            qs = jnp.where(hmask, jnp.concatenate([qp] * 4, axis=0), 0.0).astype(BF16)
            qns = jnp.where(hmask, jnp.concatenate([qn] * 4, axis=0), 0.0).astype(BF16)
            knb, kpb = kn.astype(BF16), kp.astype(BF16)
            attn = jnp.where(lmask, _dot_nt(qs, knb), _dot_nt(qns, kpb)).astype(BF16)
            st = jnp.where(bd, jnp.concatenate([st_ref[c]] * 4, axis=0), 0.0)
            g = gstate[...]
            gb = g.astype(BF16)
            do_s = jnp.where(bd, jnp.concatenate([dob] * 4, axis=0), jnp.zeros((), BF16))
            dattn = _dot_nt(do_s, vc)
            dv_ref[rows, :] = (_dot_tn(attn, do_s) + _dot(kk.astype(BF16), gb)).astype(BF16)
            dac = jnp.where(lmask, dattn, 0.0).astype(BF16)
            daa = jnp.where(lmask, 0.0, dattn).astype(BF16)
            dqp = _fold_heads(jnp.where(hmask, _dot(dac, knb), 0.0)) + _dot_nt(dob, st.astype(BF16))
            dqn = _fold_heads(jnp.where(hmask, _dot(daa, kpb), 0.0))
            dkn = _dot_tn(dac, qs)
            dkp = _dot_tn(daa, qns)
            dkk = _dot_nt(vc, gb)
            ddec = _dot_nt(jnp.ones((8, 512), F32), g * st, precision=HI)[0:1, :]
            gstate[...] = decb * g + jnp.where(bd, _dot_tn(qp.astype(BF16), dob), 0.0)
            dq_ref[rows, :] = ((dqp * ep + dqn * en) * 0.125).astype(BF16)
            dk_ref[rows, :] = (dkn * en + dkp * ep + dkk * ek).astype(BF16)
            dek = dkk * kc * ek
            db = (dqp * qc + dkp * kc) * ep - (dqn * qc + dkn * kc) * en - dek
            dbl = jnp.sum(dek, axis=0, keepdims=True) + ddec * jnp.exp(bl)
            db = db + jnp.where(last_row, dbl, 0.0)
            dla_scr[rows, :] = _dot(triu, db, precision=HI)
            return carry

        lax.fori_loop(0, nc, chunk, 0)
        dz = dla_scr[...] * (1.0 / GLA_TAU) * _sigmoid(-z)
        dzb = dz.astype(BF16)
        dwau_ref[...] += _dot_tn(eb, dzb)
        dba_ref[...] += _fold8(dz)
        de_ref[...] = _dot(dzb, waut_ref[...])

    rev = lambda i: nb - 1 - i
    return pl.pallas_call(
        body, grid=(nb,),
        in_specs=[pl.BlockSpec((t, 256), lambda i: (rev(i), C_GQ // 256)), pl.BlockSpec((t, 256), lambda i: (rev(i), C_GK // 256)),
                  pl.BlockSpec((t, 512), lambda i: (rev(i), C_GV // 512)), pl.BlockSpec((t, PE_W), lambda i: (rev(i), 0)),
                  pl.BlockSpec((PE_W, 256), lambda i: (0, 0)), pl.BlockSpec((256, PE_W), lambda i: (0, 0)),
                  pl.BlockSpec((1, 256), lambda i: (0, 0)), pl.BlockSpec((t, 512), lambda i: (rev(i), 0)),
                  pl.BlockSpec((nc, CHUNK, 512), lambda i: (rev(i), 0, 0))],
        out_specs=[pl.BlockSpec((t, 256), lambda i: (rev(i), 0)), pl.BlockSpec((t, 256), lambda i: (rev(i), 0)),
                   pl.BlockSpec((t, 512), lambda i: (rev(i), 0)), pl.BlockSpec((t, PE_W), lambda i: (rev(i), 0)),
                   pl.BlockSpec((PE_W, 256), lambda i: (0, 0)), pl.BlockSpec((8, 256), lambda i: (0, 0))],
        out_shape=[jax.ShapeDtypeStruct((s, 256), BF16), jax.ShapeDtypeStruct((s, 256), BF16),
                   jax.ShapeDtypeStruct((s, 512), BF16), jax.ShapeDtypeStruct((s, PE_W), F32),
                   jax.ShapeDtypeStruct((PE_W, 256), F32), jax.ShapeDtypeStruct((8, 256), F32)],
        scratch_shapes=[pltpu.VMEM((256, 512), F32), pltpu.VMEM((t, 256), F32), pltpu.VMEM((t, 256), F32)],
        name="gla_bwd", compiler_params=_cp("arbitrary"))(pm, pm, pm, pe, wau_p, wau_pt, b_alpha, do, states)


def _fcum_fwd(pe, bias):
    s = pe.shape[0]
    t = min(s, 256)

    def body(e_ref, b_ref, f_ref, carry):
        @pl.when(pl.program_id(0) == 0)
        def _():
            carry[...] = jnp.zeros_like(carry)

        lf = _log_sigmoid(e_ref[...] + b_ref[...])
        tri = (_iota((t, t), 0) >= _iota((t, t), 1)).astype(F32)
        f = _dot(tri, lf, precision=HI) + carry[0:1, :]
        f_ref[...] = f
        carry[...] = jnp.broadcast_to(f[t - 1:t, :], carry.shape)

    return pl.pallas_call(
        body, grid=(s // t,),
        in_specs=[pl.BlockSpec((t, PE_W), lambda i: (i, 0)), pl.BlockSpec((1, PE_W), lambda i: (0, 0))],
        out_specs=pl.BlockSpec((t, PE_W), lambda i: (i, 0)),
        out_shape=jax.ShapeDtypeStruct((s, PE_W), F32), scratch_shapes=[pltpu.VMEM((8, PE_W), F32)],
        name="fcum_fwd", compiler_params=_cp("arbitrary"))(pe, bias)


def _fcum_bwd(pe, bias, df):
    s = pe.shape[0]
    t = min(s, 256)
    nb = s // t

    def body(e_ref, b_ref, df_ref, de_ref, db_ref, carry):
        @pl.when(pl.program_id(0) == 0)
        def _():
            carry[...] = jnp.zeros_like(carry)
            db_ref[...] = jnp.zeros_like(db_ref)

        triu = (_iota((t, t), 0) <= _iota((t, t), 1)).astype(F32)
        dlf = _dot(triu, df_ref[...], precision=HI) + carry[0:1, :]
        carry[...] = jnp.broadcast_to(dlf[0:1, :], carry.shape)
        lane = _iota((t, PE_W), 1)
        dff = jnp.where((lane >= FF_LANE) & (lane < FF_LANE + 8), dlf * _sigmoid(-(e_ref[...] + b_ref[...])), 0.0)
        de_ref[...] = dff
        db_ref[...] += _fold8(dff)

    rev = lambda i: (nb - 1 - i, 0)
    return pl.pallas_call(
        body, grid=(nb,),
        in_specs=[pl.BlockSpec((t, PE_W), rev), pl.BlockSpec((1, PE_W), lambda i: (0, 0)), pl.BlockSpec((t, PE_W), rev)],
        out_specs=[pl.BlockSpec((t, PE_W), rev), pl.BlockSpec((8, PE_W), lambda i: (0, 0))],
        out_shape=[jax.ShapeDtypeStruct((s, PE_W), F32), jax.ShapeDtypeStruct((8, PE_W), F32)],
        scratch_shapes=[pltpu.VMEM((8, PE_W), F32)],
        name="fcum_bwd", compiler_params=_cp("arbitrary"))(pe, bias, df)


FOX_WIDE = 1024


def _split3(x):
    hi = x.astype(BF16)
    r = x - hi.astype(F32)
    mid = r.astype(BF16)
    lo = (r - mid.astype(F32)).astype(BF16)
    return jnp.concatenate([hi, mid, lo], axis=1)


def _fox_tables():
    heads, lane = np.arange(8), np.arange(64)
    spread = np.zeros((512, 1024), np.float32)
    spread[(64 * heads[:, None] + lane).ravel(), (128 * heads[:, None] + lane).ravel()] = 1.0
    def place(src_lane0, dst_off, val):
        t = np.zeros((384, 1024), np.float32)
        for p in range(3):
            t[128 * p + src_lane0 + heads, 128 * heads + dst_off + p] = val
        return t
    def const(off, val):
        c = np.zeros((1, 1024), np.float32)
        for p in range(3):
            c[0, 128 * heads + off + p] = val
        return c
    pick = np.zeros((1024, 128), np.float32)
    pick[128 * heads + 64, FF_LANE + heads] = 1.0
    rows = np.zeros((8, 128), np.float32)
    rows[heads, FF_LANE + heads] = 1.0
    bf = lambda a: jnp.asarray(a, BF16)
    return dict(spread=bf(spread), spread_q=bf(0.125 * spread), compact=bf(spread.T),
                f_to_q=bf(place(FF_LANE, 64, 1.0)), f_to_k=bf(place(FF_LANE, 67, -1.0)), d_to_do=bf(place(0, 64, 1.0)),
                ones_q=jnp.asarray(const(67, 1.0)), ones_k=jnp.asarray(const(64, 1.0)), ones_v=jnp.asarray(const(64, -1.0)),
                pick=bf(pick), rows=jnp.asarray(rows))


def _fox_prep(pm, f128, lse8, tb, *, backward):
    s = pm.shape[0]
    tm = _row_tile(s)

    def body(*refs):
        if backward:
            q_ref, f_ref, lse_ref, spq_ref, fq_ref, cq_ref, rows_ref, qa_ref = refs
            f = f_ref[...] - _dot_tn(lse_ref[...], rows_ref[...], precision=HI)
            qa_ref[...] = (_dot(q_ref[...], spq_ref[...]) + _dot(_split3(f), fq_ref[...]) + cq_ref[...]).astype(BF16)
            return
        (q_ref, k_ref, v_ref, f_ref, spq_ref, sp_ref, fq_ref, fk_ref, cq_ref, ck_ref, cv_ref,
         qa_ref, ka_ref, va_ref, vt_ref, qt_ref) = refs
        f3 = _split3(f_ref[...])
        q, v = q_ref[...], v_ref[...]
        qa_ref[...] = (_dot(q, spq_ref[...]) + _dot(f3, fq_ref[...]) + cq_ref[...]).astype(BF16)
        ka_ref[...] = (_dot(k_ref[...], sp_ref[...]) + _dot(f3, fk_ref[...]) + ck_ref[...]).astype(BF16)
        va_ref[...] = (_dot(v, sp_ref[...]) + cv_ref[...]).astype(BF16)
        vt_ref[...] = v.T
        qt_ref[...] = (q.astype(F32) * 0.125).astype(BF16).T

    row = lambda i: (i, 0)
    const = lambda i: (0, 0)
    blk = lambda c: pl.BlockSpec((tm, 512), lambda i: (i, c // 512))
    wide = pl.BlockSpec((tm, 1024), row)
    mat = lambda a: pl.BlockSpec(a.shape, const)
    if backward:
        ins = [pm, f128, lse8, tb["spread_q"], tb["f_to_q"], tb["ones_q"], tb["rows"]]
        in_specs = [blk(C_FQ), pl.BlockSpec((tm, 128), row), pl.BlockSpec((8, tm), lambda i: (0, i))] + [mat(a) for a in ins[3:]]
        out_specs, out_shape = wide, jax.ShapeDtypeStruct((s, 1024), BF16)
    else:
        ins = [pm, pm, pm, f128, tb["spread_q"], tb["spread"], tb["f_to_q"], tb["f_to_k"], tb["ones_q"], tb["ones_k"], tb["ones_v"]]
        in_specs = [blk(C_FQ), blk(C_FK), blk(C_FV), pl.BlockSpec((tm, 128), row)] + [mat(a) for a in ins[4:]]
        tr = pl.BlockSpec((512, tm), lambda i: (0, i))
        out_specs = [wide, wide, wide, tr, tr]
        out_shape = [jax.ShapeDtypeStruct((s, 1024), BF16)] * 3 + [jax.ShapeDtypeStruct((512, s), BF16)] * 2
    return pl.pallas_call(body, grid=(s // tm,), in_specs=in_specs, out_specs=out_specs, out_shape=out_shape,
                          name="fox_prep_bwd" if backward else "fox_prep", compiler_params=_cp("parallel"))(*ins)


def _fox_post(dq, dkt, dvt, dfcol8, tb):
    s = dq.shape[0]
    tm = _row_tile(s)

    def body(dq_ref, dkt_ref, dvt_ref, dfc_ref, cmp_ref, pick_ref, rows_ref, dfq_ref, dfk_ref, dfv_ref, df_ref):
        d = dq_ref[...]
        dfq_ref[...] = _dot(d.astype(BF16), cmp_ref[...]).astype(BF16)
        d3 = _split3(d)
        pick = pick_ref[...]
        rsum = _dot(d3[:, 0:1024], pick) + _dot(d3[:, 1024:2048], pick) + _dot(d3[:, 2048:3072], pick)
        df_ref[...] = rsum - _dot_tn(dfc_ref[...], rows_ref[...], precision=HI)
        dfk_ref[...] = dkt_ref[...].T
        dfv_ref[...] = dvt_ref[...].T

    row = lambda i: (i, 0)
    const = lambda i: (0, 0)
    tr = pl.BlockSpec((512, tm), lambda i: (0, i))
    out = pl.BlockSpec((tm, 512), row)
    return pl.pallas_call(
        body, grid=(s // tm,),
        in_specs=[pl.BlockSpec((tm, 1024), row), tr, tr, pl.BlockSpec((8, tm), lambda i: (0, i)),
                  pl.BlockSpec((1024, 512), const), pl.BlockSpec((1024, 128), const), pl.BlockSpec((8, 128), const)],
        out_specs=[out, out, out, pl.BlockSpec((tm, 128), row)],
        out_shape=[jax.ShapeDtypeStruct((s, 512), BF16)] * 3 + [jax.ShapeDtypeStruct((s, 128), F32)],
        name="fox_post", compiler_params=_cp("parallel"))(dq, dkt, dvt, dfcol8, tb["compact"], tb["pick"], tb["rows"])


def _fox_fwd(k_aug, q_aug, vt):
    s = k_aug.shape[0]
    nh = 8
    tk = _row_tile(s)
    tq = min(s, FOX_WIDE)
    per = tq // tk

    def body(k_ref, q_ref, v_ref, o_ref, lse_ref, sbuf):
        i = pl.program_id(1)
        qa = q_ref[...]

        def scores(j):
            return _dot_nt(k_ref[pl.ds(pl.multiple_of(j * tk, tk), tk), :], qa)

        def update(st, j, carry):
            m, l, acc = carry
            m2 = jnp.maximum(m, jnp.max(st, axis=0, keepdims=True))
            p = jnp.exp(st - m2)
            a = jnp.exp(m - m2)
            vj = v_ref[:, pl.ds(pl.multiple_of(j * tk, tk), tk)]
            return m2, a * l + jnp.sum(p, axis=0, keepdims=True), a * acc + _dot(vj, p.astype(BF16))

        def step(a, carry):
            sbuf[1] = scores(2 * a + 1)
            carry = update(sbuf[0], 2 * a, carry)
            sbuf[0] = scores(2 * a + 2)
            return update(sbuf[1], 2 * a + 1, carry)

        n = i * per
        sbuf[0] = scores(0)
        carry = (jnp.full((1, tq), -1e30, F32), jnp.zeros((1, tq), F32), jnp.zeros((64, tq), F32))
        carry = lax.fori_loop(0, n // 2, step, carry)
        for r in range(per):
            j = n + r
            st = sbuf[0] if r == 0 else scores(j)
            st = jnp.where(j * tk + _iota((tk, tq), 0) <= i * tq + _iota((tk, tq), 1), st, -1e30)
            carry = update(st, j, carry)
        m, l, acc = carry
        o_ref[...] = (acc / l).astype(BF16)
        lse_ref[0] = m + jnp.log(l)

    return pl.pallas_call(
        body, grid=(nh, s // tq),
        in_specs=[pl.BlockSpec((s, 128), lambda h, i: (0, h)), pl.BlockSpec((tq, 128), lambda h, i: (i, h)),
                  pl.BlockSpec((64, s), lambda h, i: (h, 0))],
        out_specs=[pl.BlockSpec((64, tq), lambda h, i: (h, i)), pl.BlockSpec((1, 1, tq), lambda h, i: (h, 0, i))],
        out_shape=[jax.ShapeDtypeStruct((512, s), BF16), jax.ShapeDtypeStruct((nh, 1, s), F32)],
        scratch_shapes=[pltpu.VMEM((2, tk, tq), F32)],
        name="fox_fwd", compiler_params=_cp("parallel", "arbitrary"))(k_aug, q_aug, vt)


def _fox_bwd(q_aug, do_aug, qt, dot_, k_aug, v_aug):
    s = q_aug.shape[0]
    nh = 8
    tq = _row_tile(s)
    tk = min(s, FOX_WIDE)
    per = tk // tq
    nqb = s // tq

    def body(qa_ref, da_ref, qt_ref, dt_ref, ka_ref, va_ref, dq_ref, dk_ref, dv_ref, dfk_ref):
        j = pl.program_id(1)

        @pl.when(j == 0)
        def _():
            dq_ref[...] = jnp.zeros_like(dq_ref)

        ka, va = ka_ref[...], va_ref[...]
        ks = jnp.where(_iota((tk, 128), 1) < 64, ka.astype(F32) * 0.125, 0.0).astype(BF16)
        lane64 = _iota((tq, 128), 1) == 64

        def tile(i, masked, carry):
            dk, dv, dfk = carry
            rows = pl.ds(pl.multiple_of(i * tq, tq), tq)
            sp = _dot_nt(qa_ref[rows, :], ka)
            if masked:
                sp = jnp.where(i * tq + _iota((tq, tk), 0) >= j * tk + _iota((tq, tk), 1), sp, -1e30)
            p = jnp.exp(sp)
            ds = p * _dot_nt(da_ref[rows, :], va)
            dsb = ds.astype(BF16)
            dq_ref[rows, :] += _dot(dsb, ks) + jnp.where(lane64, jnp.sum(ds, axis=1, keepdims=True), 0.0)
            return (dk + _dot(qt_ref[:, rows], dsb), dv + _dot(dt_ref[:, rows], p.astype(BF16)),
                    dfk + jnp.sum(ds, axis=0, keepdims=True))

        carry = (jnp.zeros((64, tk), F32), jnp.zeros((64, tk), F32), jnp.zeros((1, tk), F32))
        for r in range(per):
            carry = tile(j * per + r, True, carry)
        dk, dv, dfk = lax.fori_loop((j + 1) * per, nqb, lambda i, c: tile(i, False, c), carry)
        dk_ref[...] = dk.astype(BF16)
        dv_ref[...] = dv.astype(BF16)
        dfk_ref[0] = dfk

    head_cols = lambda h, j: (0, h)
    head_rows = lambda h, j: (h, 0)
    once = dict(pipeline_mode=pl.Buffered(1))
    return pl.pallas_call(
        body, grid=(nh, s // tk),
        in_specs=[pl.BlockSpec((s, 128), head_cols, **once), pl.BlockSpec((s, 128), head_cols, **once),
                  pl.BlockSpec((64, s), head_rows, **once), pl.BlockSpec((64, s), head_rows, **once),
                  pl.BlockSpec((tk, 128), lambda h, j: (j, h)), pl.BlockSpec((tk, 128), lambda h, j: (j, h))],
        out_specs=[pl.BlockSpec((s, 128), head_cols), pl.BlockSpec((64, tk), lambda h, j: (h, j)),
                   pl.BlockSpec((64, tk), lambda h, j: (h, j)), pl.BlockSpec((1, 1, tk), lambda h, j: (h, 0, j))],
        out_shape=[jax.ShapeDtypeStruct((s, 1024), F32), jax.ShapeDtypeStruct((512, s), BF16),
                   jax.ShapeDtypeStruct((512, s), BF16), jax.ShapeDtypeStruct((nh, 1, s), F32)],
        name="fox_bwd", compiler_params=_cp("parallel", "arbitrary"))(q_aug, do_aug, qt, dot_, k_aug, v_aug)


MEM_SCALE = 128 ** -0.5


def _mem_attn_fwd(pm, mkv):
    s = pm.shape[0]
    t = _row_tile(s)
    nm = mkv.shape[0]

    def body(q_ref, mk_ref, mv_ref, o_ref):
        for h in range(4):
            cols = slice(128 * h, 128 * (h + 1))
            sc = _dot_nt(q_ref[:, cols], mk_ref[:, cols]) * MEM_SCALE
            p = jnp.exp(sc - jnp.max(sc, axis=-1, keepdims=True))
            p = p / jnp.sum(p, axis=-1, keepdims=True)
            o_ref[:, cols] = _dot(p.astype(BF16), mv_ref[:, cols]).astype(BF16)

    return pl.pallas_call(
        body, grid=(s // t,),
        in_specs=[pl.BlockSpec((t, 512), lambda i: (i, C_MQ // 512)), pl.BlockSpec((nm, 512), lambda i: (0, 0)),
                  pl.BlockSpec((nm, 512), lambda i: (0, 1))],
        out_specs=pl.BlockSpec((t, 512), lambda i: (i, 0)),
        out_shape=jax.ShapeDtypeStruct((s, 512), BF16),
        name="mem_attn_fwd", compiler_params=_cp("parallel"))(pm, mkv, mkv)


def _mem_attn_bwd(pm, mkv, do):
    s = pm.shape[0]
    t = _row_tile(s)
    nm = mkv.shape[0]

    def body(q_ref, mk_ref, mv_ref, do_ref, dq_ref, dmk_ref, dmv_ref):
        @pl.when(pl.program_id(0) == 0)
        def _():
            dmk_ref[...] = jnp.zeros_like(dmk_ref)
            dmv_ref[...] = jnp.zeros_like(dmv_ref)

        for h in range(4):
            cols = slice(128 * h, 128 * (h + 1))
            qh, kh, vh, doh = q_ref[:, cols], mk_ref[:, cols], mv_ref[:, cols], do_ref[:, cols]
            sc = _dot_nt(qh, kh) * MEM_SCALE
            p = jnp.exp(sc - jnp.max(sc, axis=-1, keepdims=True))
            p = p / jnp.sum(p, axis=-1, keepdims=True)
            pb = p.astype(BF16)
            dp = _dot_nt(doh, vh)
            ds = (p * (dp - jnp.sum(p * dp, axis=-1, keepdims=True)) * MEM_SCALE).astype(BF16)
            dq_ref[:, cols] = _dot(ds, kh).astype(BF16)
            dmk_ref[:, cols] += _dot_tn(ds, qh)
            dmv_ref[:, cols] += _dot_tn(pb, doh)

    return pl.pallas_call(
        body, grid=(s // t,),
        in_specs=[pl.BlockSpec((t, 512), lambda i: (i, C_MQ // 512)), pl.BlockSpec((nm, 512), lambda i: (0, 0)),
                  pl.BlockSpec((nm, 512), lambda i: (0, 1)), pl.BlockSpec((t, 512), lambda i: (i, 0))],
        out_specs=[pl.BlockSpec((t, 512), lambda i: (i, 0)), pl.BlockSpec((nm, 512), lambda i: (0, 0)),
                   pl.BlockSpec((nm, 512), lambda i: (0, 0))],
        out_shape=[jax.ShapeDtypeStruct((s, 512), BF16), jax.ShapeDtypeStruct((nm, 512), F32),
                   jax.ShapeDtypeStruct((nm, 512), F32)],
        name="mem_attn_bwd", compiler_params=_cp("arbitrary"))(pm, mkv, mkv, do)


def _gain_grad(dxn_g, x, r, name):
    m, d = x.shape

    def body(d_ref, x_ref, r_ref, o_ref):
        o_ref[...] = _fold8(d_ref[...] * (x_ref[...] * r_ref[...]))

    return pl.pallas_call(body, out_shape=jax.ShapeDtypeStruct((8, d), F32), name=name,
                          compiler_params=pltpu.CompilerParams(vmem_limit_bytes=VMEM_LIMIT_BYTES))(dxn_g, x, r)


def _head_norm(o, gh):
    xs, rs = [], []
    for h in range(4):
        oh = o[:, 128 * h:128 * (h + 1)]
        r = lax.rsqrt(jnp.mean(oh * oh, axis=-1, keepdims=True) + EPS)
        xs.append(oh * r)
        rs.append(r)
    return xs, rs


def _merge_fwd(x, pm, o_gla, o_fox_t, o_mem, g_head, wg, wf, wm, wo, g_ffn):
    s = x.shape[0]
    t = min(s, 256)

    def body(x_ref, g0_ref, g1_ref, g2_ref, gg_ref, og_ref, of_ref, om_ref, gh_ref, wg_ref, wf_ref, wm_ref, wo_ref, gf_ref,
             mg_ref, h1_ref, u2_ref, r2_ref):
        xs, _ = _head_norm(og_ref[...], None)
        gg = gg_ref[...].astype(F32)
        sil = gg * _sigmoid(gg)
        ogn = jnp.concatenate(xs, axis=1) * gh_ref[...] * sil
        merged = (_sigmoid(g0_ref[...].astype(F32)) * _dot(ogn.astype(BF16), wg_ref[...])
                  + _sigmoid(g1_ref[...].astype(F32)) * _dot(of_ref[...].T, wf_ref[...])
                  + _sigmoid(g2_ref[...].astype(F32)) * _dot(om_ref[...], wm_ref[...]))
        mb = merged.astype(BF16)
        mg_ref[...] = mb
        h1 = x_ref[...] + _dot(mb, wo_ref[...])
        h1_ref[...] = h1
        r = lax.rsqrt(jnp.mean(h1 * h1, axis=-1, keepdims=True) + EPS)
        u2_ref[...] = ((h1 * r) * gf_ref[...]).astype(BF16)
        r2_ref[...] = r

    row = lambda i: (i, 0)
    const = lambda i: (0, 0)
    return pl.pallas_call(
        body, grid=(s // t,),
        in_specs=[pl.BlockSpec((t, D), row), pl.BlockSpec((t, D), lambda i: (i, 0)), pl.BlockSpec((t, D), lambda i: (i, 1)),
                  pl.BlockSpec((t, D), lambda i: (i, 2)), pl.BlockSpec((t, 512), lambda i: (i, C_GG // 512)),
                  pl.BlockSpec((t, 512), row), pl.BlockSpec((512, t), lambda i: (0, i)), pl.BlockSpec((t, 512), row),
                  pl.BlockSpec((1, 512), const), pl.BlockSpec((512, D), const), pl.BlockSpec((512, D), const),
                  pl.BlockSpec((512, D), const), pl.BlockSpec((D, D), const), pl.BlockSpec((1, D), const)],
        out_specs=[pl.BlockSpec((t, D), row), pl.BlockSpec((t, D), row), pl.BlockSpec((t, D), row), pl.BlockSpec((t, 1), row)],
        out_shape=[jax.ShapeDtypeStruct((s, D), BF16), jax.ShapeDtypeStruct((s, D), F32),
                   jax.ShapeDtypeStruct((s, D), BF16), jax.ShapeDtypeStruct((s, 1), F32)],
        name="merge_fwd", compiler_params=_cp("parallel"))(x, pm, pm, pm, pm, o_gla, o_fox_t, o_mem, g_head, wg, wf, wm, wo, g_ffn)


def _merge_bwd(dh1b, pm, o_gla, o_fox_t, o_mem, g_head, wg, wf, wm, wgt, wft, wmt, wot, spread, d_to_do):
    s = dh1b.shape[0]
    t = min(s, 256)

    def body(dh_ref, g0_ref, g1_ref, g2_ref, gg_ref, og_ref, of_ref, om_ref, gh_ref, wg_ref, wf_ref, wm_ref,
             wgt_ref, wft_ref, wmt_ref, wot_ref, sp_ref, dd_ref,
             dgt_ref, dgg_ref, dog_ref, da_ref, dot_ref, dom_ref, dwg_ref, dwf_ref, dwm_ref, dgh_ref):
        @pl.when(pl.program_id(0) == 0)
        def _():
            dwg_ref[...] = jnp.zeros_like(dwg_ref)
            dwf_ref[...] = jnp.zeros_like(dwf_ref)
            dwm_ref[...] = jnp.zeros_like(dwm_ref)
            dgh_ref[...] = jnp.zeros_like(dgh_ref)

        dmerged = _dot(dh_ref[...], wot_ref[...])
        og = og_ref[...]
        xs, rs = _head_norm(og, None)
        on = jnp.concatenate(xs, axis=1)
        gg = gg_ref[...].astype(F32)
        sg = _sigmoid(gg)
        sil = gg * sg
        gh = gh_ref[...]
        ognb = (on * gh * sil).astype(BF16)
        ofb, omb = of_ref[...].T, om_ref[...]
        douts = []
        for idx, (gref, ob, w_ref, wt_ref, dw_ref) in enumerate((
                (g0_ref, ognb, wg_ref, wgt_ref, dwg_ref), (g1_ref, ofb, wf_ref, wft_ref, dwf_ref),
                (g2_ref, omb, wm_ref, wmt_ref, dwm_ref))):
            gt = _sigmoid(gref[...].astype(F32))
            y = _dot(ob, w_ref[...])
            dgt_ref[:, D * idx:D * (idx + 1)] = (dmerged * y * gt * (1.0 - gt)).astype(BF16)
            dy = (gt * dmerged).astype(BF16)
            dw_ref[...] += _dot_tn(ob, dy)
            douts.append(_dot(dy, wt_ref[...]))
        dogn, dof, dom = douts
        dofb = dof.astype(BF16)
        dom_ref[...] = dom.astype(BF16)
        ind = (_iota((512, 128), 0) // 64 == _iota((512, 128), 1)).astype(F32)
        delta = _dot(dofb.astype(F32) * ofb.astype(F32), ind, precision=HI)
        da_ref[...] = (_dot(dofb, sp_ref[...]) + _dot(_split3(delta), dd_ref[...])).astype(BF16)
        dot_ref[...] = dofb.T
        dgg_ref[...] = (dogn * on * gh * (sg * (1.0 + gg * (1.0 - sg)))).astype(BF16)
        d_on = dogn * sil
        dgh_ref[...] += _fold8(d_on * on)
        dxn = d_on * gh
        outs = []
        for h in range(4):
            cols = slice(128 * h, 128 * (h + 1))
            dh_, xh = dxn[:, cols], xs[h]
            outs.append(rs[h] * (dh_ - xh * jnp.mean(dh_ * xh, axis=-1, keepdims=True)))
        dog_ref[...] = jnp.concatenate(outs, axis=1).astype(BF16)

    row = lambda i: (i, 0)
    const = lambda i: (0, 0)
    return pl.pallas_call(
        body, grid=(s // t,),
        in_specs=[pl.BlockSpec((t, D), row), pl.BlockSpec((t, D), lambda i: (i, 0)), pl.BlockSpec((t, D), lambda i: (i, 1)),
                  pl.BlockSpec((t, D), lambda i: (i, 2)), pl.BlockSpec((t, 512), lambda i: (i, C_GG // 512)),
                  pl.BlockSpec((t, 512), row), pl.BlockSpec((512, t), lambda i: (0, i)), pl.BlockSpec((t, 512), row),
                  pl.BlockSpec((1, 512), const), pl.BlockSpec((512, D), const), pl.BlockSpec((512, D), const),
                  pl.BlockSpec((512, D), const), pl.BlockSpec((D, 512), const), pl.BlockSpec((D, 512), const),
                  pl.BlockSpec((D, 512), const), pl.BlockSpec((D, D), const),
                  pl.BlockSpec((512, 1024), const), pl.BlockSpec((384, 1024), const)],
        out_specs=[pl.BlockSpec((t, 3 * D), row), pl.BlockSpec((t, 512), row), pl.BlockSpec((t, 512), row),
                   pl.BlockSpec((t, 1024), row), pl.BlockSpec((512, t), lambda i: (0, i)), pl.BlockSpec((t, 512), row),
                   pl.BlockSpec((512, D), const), pl.BlockSpec((512, D), const), pl.BlockSpec((512, D), const),
                   pl.BlockSpec((8, 512), const)],
        out_shape=[jax.ShapeDtypeStruct((s, 3 * D), BF16), jax.ShapeDtypeStruct((s, 512), BF16),
                   jax.ShapeDtypeStruct((s, 512), BF16), jax.ShapeDtypeStruct((s, 1024), BF16),
                   jax.ShapeDtypeStruct((512, s), BF16), jax.ShapeDtypeStruct((s, 512), BF16),
                   jax.ShapeDtypeStruct((512, D), F32), jax.ShapeDtypeStruct((512, D), F32),
                   jax.ShapeDtypeStruct((512, D), F32), jax.ShapeDtypeStruct((8, 512), F32)],
        name="merge_bwd", compiler_params=_cp("arbitrary"))(
            dh1b, pm, pm, pm, pm, o_gla, o_fox_t, o_mem, g_head, wg, wf, wm, wgt, wft, wmt, wot, spread, d_to_do)


def _ff2_loss(a, w2, h1, g_final, target):
    s, k = a.shape
    tm = min(s, 256)

    def body(a_ref, w_ref, h1_ref, g_ref, t_ref, dh_ref, dhb_ref, loss_ref, dg_ref):
        @pl.when(pl.program_id(0) == 0)
        def _():
            loss_ref[...] = jnp.zeros_like(loss_ref)
            dg_ref[...] = jnp.zeros_like(dg_ref)

        h2 = h1_ref[...] + _dot(_relu2_bf16(a_ref[...]), w_ref[...])
        r = lax.rsqrt(jnp.mean(h2 * h2, axis=-1, keepdims=True) + EPS)
        xn = h2 * r
        g = g_ref[...]
        err = xn * g - t_ref[...]
        e2 = _fold8(err * err)
        part = e2[:, 0:128]
        for c in range(1, D // 128):
            part = part + e2[:, 128 * c:128 * (c + 1)]
        loss_ref[...] += part
        dy = err * (1.0 / D)
        dg_ref[...] += _fold8(dy * xn)
        dxn = dy * g
        dh = r * (dxn - xn * jnp.mean(dxn * xn, axis=-1, keepdims=True))
        dh_ref[...] = dh
        dhb_ref[...] = dh.astype(BF16)

    row = lambda i: (i, 0)
    const = lambda i: (0, 0)
    return pl.pallas_call(
        body, grid=(s // tm,),
        in_specs=[pl.BlockSpec((tm, k), row), pl.BlockSpec((k, D), const, pipeline_mode=pl.Buffered(1)),
                  pl.BlockSpec((tm, D), row), pl.BlockSpec((1, D), const), pl.BlockSpec((tm, D), row)],
        out_specs=[pl.BlockSpec((tm, D), row), pl.BlockSpec((tm, D), row), pl.BlockSpec((8, 128), const),
                   pl.BlockSpec((8, D), const)],
        out_shape=[jax.ShapeDtypeStruct((s, D), F32), jax.ShapeDtypeStruct((s, D), BF16),
                   jax.ShapeDtypeStruct((8, 128), F32), jax.ShapeDtypeStruct((8, D), F32)],
        name="ff2_loss", compiler_params=_cp("arbitrary"))(a, w2, h1, g_final, target)


def _adam(w, g, m, v, name):
    r, c = w.shape
    tr = r
    for cand in (512, 256, 128, 64, 32, 16, 8):
        if r % cand == 0 and cand * c * 4 <= (1 << 20):
            tr = cand
            break
    c1 = 1.0 - ADAM_B1 ** ADAM_STEP
    c2 = 1.0 - ADAM_B2 ** ADAM_STEP

    def body(w_ref, g_ref, m_ref, v_ref, d_ref, nm_ref, nv_ref):
        gv = g_ref[...]
        nm = ADAM_B1 * m_ref[...] + (1.0 - ADAM_B1) * gv
        nv = ADAM_B2 * v_ref[...] + (1.0 - ADAM_B2) * (gv * gv)
        d_ref[...] = -ADAM_LR * ((nm / c1) / (jnp.sqrt(nv / c2) + ADAM_EPS) + ADAM_WD * w_ref[...])
        nm_ref[...] = nm
        nv_ref[...] = nv

    spec = pl.BlockSpec((tr, c), lambda i: (i, 0))
    return pl.pallas_call(
        body, grid=(r // tr,), in_specs=[spec] * 4, out_specs=[spec] * 3,
        out_shape=[jax.ShapeDtypeStruct((r, c), F32)] * 3, name=name, compiler_params=_cp("parallel"))(w, g, m, v)


def _row_block(r):
    return max(d for d in range(16, 513, 16) if r % d == 0)


def _add2(a, b, name):
    n, r, c = a.shape
    tr = _row_block(r)

    def body(a_ref, b_ref, o_ref):
        o_ref[...] = (a_ref[...].astype(F32) + b_ref[...].astype(F32)).astype(BF16)

    spec = pl.BlockSpec((1, tr, c), lambda k, i: (k, i, 0))
    return pl.pallas_call(body, grid=(n, r // tr), in_specs=[spec, spec], out_specs=spec,
                          out_shape=jax.ShapeDtypeStruct((n, r, c), BF16), name=name,
                          compiler_params=_cp("parallel", "parallel"))(a, b)


def _sum4(a, name):
    _, r, c = a.shape
    tr = _row_block(r)

    def body(a_ref, o_ref):
        o_ref[...] = ((a_ref[0].astype(F32) + a_ref[1].astype(F32)) + a_ref[2].astype(F32)) + a_ref[3].astype(F32)

    return pl.pallas_call(body, grid=(r // tr,), in_specs=[pl.BlockSpec((4, tr, c), lambda i: (0, i, 0))],
                          out_specs=pl.BlockSpec((tr, c), lambda i: (i, 0)),
                          out_shape=jax.ShapeDtypeStruct((r, c), F32), name=name, compiler_params=_cp("parallel"))(a)


def _adam_small(w, gathered, m, v):
    c1 = 1.0 - ADAM_B1 ** ADAM_STEP
    c2 = 1.0 - ADAM_B2 ** ADAM_STEP

    def body(w_ref, g_ref, m_ref, v_ref, gs_ref, d_ref, nm_ref, nv_ref):
        gv = g_ref[0]
        for dev in range(1, N_DEV):
            gv = gv + g_ref[dev]
        gs_ref[...] = gv
        nm = ADAM_B1 * m_ref[...] + (1.0 - ADAM_B1) * gv
        nv = ADAM_B2 * v_ref[...] + (1.0 - ADAM_B2) * (gv * gv)
        d_ref[...] = -ADAM_LR * ((nm / c1) / (jnp.sqrt(nv / c2) + ADAM_EPS) + ADAM_WD * w_ref[...])
        nm_ref[...] = nm
        nv_ref[...] = nv

    return pl.pallas_call(body, out_shape=[jax.ShapeDtypeStruct((8, D), F32)] * 4, name="adam_small")(w, gathered, m, v)


def _place():
    return lax.axis_index("x"), lax.axis_index("y"), lax.axis_index("c")


def _other_chips(x, y):
    return [(1 - x, y), (x, 1 - y), (1 - x, 1 - y)]


def _gather_shards(p):
    def body(p_ref, out_ref, *sems):
        start, finish = _gather_ops((p_ref,), (out_ref,), sems)
        start()
        finish()

    return pl.pallas_call(
        body, out_shape=jax.ShapeDtypeStruct((N_CHIPS,) + p.shape, p.dtype), in_specs=[ANY_SPEC], out_specs=ANY_SPEC,
        scratch_shapes=GATHER_SEMS, name="gather_shards")(p)


GATHER_SEMS = [pltpu.SemaphoreType.DMA((6,)), pltpu.SemaphoreType.DMA((6,)), pltpu.SemaphoreType.DMA]


def _gather_ops(in_refs, out_refs, sems):
    (p_ref,), (out_ref,) = in_refs, out_refs
    send_sems, recv_sems, local_sem = sems
    hr = p_ref.shape[0] // 2
    x, y, cc = _place()
    sibling = (x, y, 1 - cc)
    chips = _other_chips(x, y)

    def half(chip, core):
        return out_ref.at[2 * chip[0] + chip[1], pl.ds(core * hr, hr), :]

    def copy(k, chip, core, to, src=None):
        return pltpu.make_async_remote_copy(
            src_ref=half(chip, core) if src is None else src, dst_ref=half(chip, core),
            send_sem=send_sems.at[k], recv_sem=recv_sems.at[k], device_id=to, device_id_type=MESH)

    mine = pltpu.make_async_copy(p_ref, out_ref.at[2 * x + y], local_sem)
    my_half = p_ref.at[pl.ds(cc * hr, hr), :]
    first = [copy(j, (x, y), cc, (*chip, cc), src=my_half) for j, chip in enumerate(chips)]
    passed = [copy(3 + j, chip, cc, sibling) for j, chip in enumerate(chips)]

    def start():
        mine.start()
        for cp in first:
            cp.start()

    def finish():
        for j, chip in enumerate(chips):
            copy(j, chip, cc, (x, y, cc)).wait_recv()
            passed[j].start()
        for j, chip in enumerate(chips):
            copy(3 + j, chip, 1 - cc, (x, y, cc)).wait_recv()
        for cp in first + passed:
            cp.wait_send()
        mine.wait()

    return start, finish


def _gather_side(p):
    return _Side([p], [jax.ShapeDtypeStruct((N_CHIPS,) + p.shape, p.dtype)], GATHER_SEMS, _gather_ops)


def _swap_halves(g):
    n, r, c = g.shape
    hr = r // 2

    def body(g_ref, out_ref, send_sem, recv_sem):
        x, y, cc = _place()
        cp = pltpu.make_async_remote_copy(
            src_ref=g_ref.at[:, pl.ds((1 - cc) * hr, hr), :], dst_ref=out_ref,
            send_sem=send_sem, recv_sem=recv_sem, device_id=(x, y, 1 - cc), device_id_type=MESH)
        cp.start()
        cp.wait()

    any_spec = pl.BlockSpec(memory_space=pl.ANY)
    return pl.pallas_call(
        body, out_shape=jax.ShapeDtypeStruct((n, hr, c), g.dtype), in_specs=[any_spec], out_specs=any_spec,
        scratch_shapes=[pltpu.SemaphoreType.DMA, pltpu.SemaphoreType.DMA], name="swap_halves")(g)


SCATTER_SEMS = [pltpu.SemaphoreType.DMA((7,)), pltpu.SemaphoreType.DMA((7,)), pltpu.SemaphoreType.DMA]


def _scatter_ops(in_refs, out_refs, sems):
    (p_ref,), (out_ref,) = in_refs, out_refs
    send_sems, recv_sems, local_sem = sems
    hr = p_ref.shape[1]
    x, y, cc = _place()
    me = 2 * x + y
    sibling = (x, y, 1 - cc)
    chips = _other_chips(x, y)
    ids = [2 * chip[0] + chip[1] for chip in chips]

    def land(src, core):
        return out_ref.at[src, pl.ds(core * hr, hr), :]

    def copy(k, src_ref, dst_ref, to):
        return pltpu.make_async_remote_copy(src_ref=src_ref, dst_ref=dst_ref, send_sem=send_sems.at[k],
                                            recv_sem=recv_sems.at[k], device_id=to, device_id_type=MESH)

    mine = pltpu.make_async_copy(p_ref.at[me], land(me, cc), local_sem)
    sends = [copy(j, p_ref.at[ids[j]], land(me, cc), (*chip, cc)) for j, chip in enumerate(chips)]
    sends.append(copy(3, p_ref.at[me], land(me, cc), sibling))
    passed = [copy(4 + j, land(ids[j], cc), land(ids[j], cc), sibling) for j in range(3)]

    def start():
        mine.start()
        for cp in sends:
            cp.start()

    def finish():
        for j in range(3):
            copy(j, p_ref.at[me], land(ids[j], cc), (x, y, cc)).wait_recv()
            passed[j].start()
        copy(3, p_ref.at[me], land(me, 1 - cc), (x, y, cc)).wait_recv()
        for j in range(3):
            copy(4 + j, p_ref.at[me], land(ids[j], 1 - cc), (x, y, cc)).wait_recv()
        for cp in sends + passed:
            cp.wait_send()
        mine.wait()

    return start, finish


def _scatter_side(p):
    n, hr, c = p.shape
    return _Side([p], [jax.ShapeDtypeStruct((n, 2 * hr, c), p.dtype)], SCATTER_SEMS, _scatter_ops)


def _gather_small(blk):
    m, n = blk.shape

    def body(x_ref, out_ref, send_sems, recv_sems, local_sem):
        x, y, cc = _place()
        me, sibling = (x, y, cc), (x, y, 1 - cc)
        chips = _other_chips(x, y)

        def slot(px, py, pc):
            return out_ref.at[4 * px + 2 * py + pc]

        def copy(k, block, to, src=None):
            return pltpu.make_async_remote_copy(
                src_ref=slot(*block) if src is None else src, dst_ref=slot(*block),
                send_sem=send_sems.at[k], recv_sem=recv_sems.at[k], device_id=to, device_id_type=MESH)

        mine = pltpu.make_async_copy(x_ref, slot(*me), local_sem)
        mine.start()
        first = [copy(0, me, sibling, src=x_ref)]
        first += [copy(1 + j, me, (*chip, cc), src=x_ref) for j, chip in enumerate(chips)]
        for cp in first:
            cp.start()
        passed = [copy(4 + j, (*chip, cc), sibling) for j, chip in enumerate(chips)]
        for j, chip in enumerate(chips):
            copy(1 + j, (*chip, cc), me).wait_recv()
            passed[j].start()
        copy(0, sibling, me).wait_recv()
        for j, chip in enumerate(chips):
            copy(4 + j, (*chip, 1 - cc), me).wait_recv()
        for cp in first + passed:
            cp.wait_send()
        mine.wait()

    vmem = pl.BlockSpec(memory_space=pltpu.VMEM)
    return pl.pallas_call(
        body, out_shape=jax.ShapeDtypeStruct((N_DEV, m, n), blk.dtype), in_specs=[vmem], out_specs=vmem,
        scratch_shapes=[pltpu.SemaphoreType.DMA((7,)), pltpu.SemaphoreType.DMA((7,)), pltpu.SemaphoreType.DMA],
        name="gather_small")(blk)


def _shard_shape(name, full_shape):
    shp = list(full_shape)
    shp[SHARD_AXIS[name]] //= N_CHIPS
    return tuple(shp)


FULL_SHAPES = {"w_in": (D, 6680), "w_alpha_up": (16, 256), "w_mem_kv": (D, D), "w_gla_o": (512, D), "w_fox_o": (512, D),
               "w_mem_o": (512, D), "w_out": (D, D), "w_ff1": (D, 4 * D), "w_ff2": (4 * D, D)}


def _pack_a(sh, dtype):
    w = sh["w_in"].astype(dtype)
    return jnp.concatenate([w[:, 0:PACK_W], jnp.pad(w[:, PACK_W:], ((0, 0), (0, 2 * PACK_W - w.shape[1])))], axis=0)


def _pack_b(sh, dtype):
    o3 = jnp.concatenate([sh["w_gla_o"], sh["w_fox_o"], sh["w_mem_o"], jnp.zeros((512, 256), sh["w_gla_o"].dtype)], axis=1)
    au = jnp.pad(sh["w_alpha_up"], ((0, PACK_ROWS_B - 3072 - 16), (0, PACK_W - 64)))
    return jnp.concatenate([sh["w_ff1"], sh["w_ff2"], sh["w_mem_kv"], sh["w_out"], o3, au], axis=0).astype(dtype)


def _unpack_a(pa):
    return {"w_in": jnp.concatenate([pa[0:1024], pa[1024:2048, 0:1670 - PACK_W]], axis=1)}


def _unpack_b(pb):
    return {"w_ff1": pb[0:1024], "w_ff2": pb[1024:2048], "w_mem_kv": pb[2048:2304], "w_out": pb[2304:2560],
            "w_gla_o": pb[2560:3072, 0:256], "w_fox_o": pb[2560:3072, 256:512], "w_mem_o": pb[2560:3072, 512:768],
            "w_alpha_up": pb[3072:3088, 0:64]}


def _unpack(packed):
    return {**_unpack_a(packed[0:PACK_ROWS_A]), **_unpack_b(packed[PACK_ROWS_A:])}


def _split_shards(name, full):
    return jnp.split(full, N_CHIPS, axis=SHARD_AXIS[name])


def _pack_small(vals, scalar=None):
    row4 = jnp.concatenate([vals["b_alpha"].reshape(-1), vals["b_forget"].reshape(-1), jnp.zeros((D - 264,), F32)])
    row5 = jnp.concatenate([vals["g_gla_head"].reshape(-1), jnp.zeros((D - 512,), F32)])
    row6 = jnp.zeros((D,), F32) if scalar is None else jnp.broadcast_to(scalar, (D,))
    rows = [vals["g_mix"].reshape(-1), vals["g_mem"].reshape(-1), vals["g_ffn"].reshape(-1), vals["g_final"].reshape(-1),
            row4, row5, row6, jnp.zeros((D,), F32)]
    return jnp.stack(rows)


def _unpack_small(blk):
    return {"g_mix": blk[0].reshape(1, D), "g_mem": blk[1].reshape(1, D), "g_ffn": blk[2].reshape(1, D),
            "g_final": blk[3].reshape(D), "b_alpha": blk[4, 0:256].reshape(1, 256), "b_forget": blk[4, 256:264].reshape(1, 8),
            "g_gla_head": blk[5, 0:512].reshape(1, 4, 128)}


def _local_step(x, mem, target, wb, small, exchange=None):
    s = x.shape[0]
    nm = mem.shape[0]
    t = _row_tile(s)
    nb = s // t
    w_in = wb["w_in"]
    w_main = jnp.concatenate([w_in[:, 3608:6680], w_in[:, 0:1536], w_in[:, 1552:3088], w_in[:, 3096:3608]], axis=1)
    w_e = jnp.concatenate([w_in[:, 1536:1552], w_in[:, 3088:3096], jnp.zeros((D, PE_W - 24), BF16)], axis=1)
    w_in_pt = jnp.concatenate([w_main, w_e, jnp.zeros((D, DP_W - PM_W - PE_W), BF16)], axis=1).T
    b_alpha = small["b_alpha"].reshape(1, 256)
    bias_e = jnp.concatenate([jnp.zeros((FF_LANE,), F32), small["b_forget"].reshape(-1),
                              jnp.zeros((PE_W - FF_LANE - 8,), F32)]).reshape(1, PE_W)
    g_mix, g_mem, g_ffn = small["g_mix"].reshape(1, D), small["g_mem"].reshape(1, D), small["g_ffn"].reshape(1, D)
    g_final = small["g_final"].reshape(1, D)
    g_head = small["g_gla_head"].reshape(1, 512)

    u, r1 = _rms_fwd(x, g_mix, "norm_mix")
    big = min(s, 1024)
    if exchange is None:
        pm = _mm_nn(u, w_main, out_dtype=BF16, tm=big, tn=PM_W // 4, tk=D, name="proj_main")
    else:
        pm, gathered = _mm_nn(u, w_main, out_dtype=BF16, tm=big, tn=PM_W // 4, tk=D, name="proj_main", side=exchange.gather)
        wb = {**wb, **exchange.weights(gathered)}
    wau_p = jnp.concatenate([wb["w_alpha_up"], jnp.zeros((PE_W - 16, 256), BF16)], axis=0)
    pe = _mm_nn(u, w_e, out_dtype=F32, tm=t, tn=PE_W, tk=D, name="proj_narrow")
    o_gla, states = _gla_fwd(pm, pe, wau_p, b_alpha)
    fcum = _fcum_fwd(pe, bias_e)
    tb = _fox_tables()
    qf_aug, k_aug, v_aug, vt, qt = _fox_prep(pm, fcum, None, tb, backward=False)
    o_fox, lse = _fox_fwd(k_aug, qf_aug, vt)
    mn, rm = _rms_fwd(mem, g_mem, "norm_mem")
    mkv = _mm_nn(mn, wb["w_mem_kv"], out_dtype=BF16, tm=nm, tn=512, tk=D, name="mem_kv")
    o_mem = _mem_attn_fwd(pm, mkv)
    merged, h1, u2, r2 = _merge_fwd(x, pm, o_gla, o_fox, o_mem, g_head, wb["w_gla_o"], wb["w_fox_o"], wb["w_mem_o"],
                                    wb["w_out"], g_ffn)
    a = _mm_nn(u2, wb["w_ff1"], out_dtype=BF16, tm=big, tn=1024, tk=D, name="ff1")
    dh2, dh2b, loss8, dgfin8 = _ff2_loss(a, wb["w_ff2"], h1, g_final, target)
    loss = 0.5 * jnp.sum(loss8) / D

    da = _mm_nn(dh2b, wb["w_ff2"].T, out_dtype=BF16, tm=t, tn=1024, tk=D, name="d_act",
                epi=lambda acc, at: acc * (2.0 * jnp.maximum(at.astype(F32), 0.0)), extra=a)
    gw = {}
    gw["w_ff2"] = _mm_tn(a, dh2b, tm=1024, tn=D, ts=t, name="dw_ff2", a_fn=_relu2_bf16)
    gw["w_ff1"] = _mm_tn(u2, da, tm=D, tn=1024, ts=t, name="dw_ff1")
    dh1, dh1b, dgffn8 = _mm_norm_bwd(da, wb["w_ff1"].T, h1, r2, g_ffn, dh2, name="d_h1", want_bf16=True)
    gw["w_out"] = _mm_tn(merged, dh1b, tm=D, tn=D, ts=t, name="dw_out")
    (dgates, dgg, do_gla, do_aug, do_t, do_mem, gw["w_gla_o"], gw["w_fox_o"], gw["w_mem_o"], dgh8) = _merge_bwd(
        dh1b, pm, o_gla, o_fox, o_mem, g_head, wb["w_gla_o"], wb["w_fox_o"], wb["w_mem_o"],
        wb["w_gla_o"].T, wb["w_fox_o"].T, wb["w_mem_o"].T, wb["w_out"].T, tb["spread"], tb["d_to_do"])
    dgq, dgk, dgv, de_gla, dwau_p, dba8 = _gla_bwd(pm, pe, wau_p, wau_p.T, b_alpha, do_gla, states)
    gw["w_alpha_up"] = dwau_p[0:16, :]
    q_aug = _fox_prep(pm, fcum, lse.reshape(8, s), tb, backward=True)
    dq_aug, dfk_t, dfv_t, dfcol = _fox_bwd(q_aug, do_aug, qt, do_t, k_aug, v_aug)
    dfq, dfk, dfv, df = _fox_post(dq_aug, dfk_t, dfv_t, dfcol.reshape(8, s), tb)
    de_fox, dbf8 = _fcum_bwd(pe, bias_e, df)
    dmq, dmk, dmv = _mem_attn_bwd(pm, mkv, do_mem)
    dmkv = jnp.concatenate([dmk, dmv], axis=1).astype(BF16)
    gw["w_mem_kv"] = _mm_tn(mn, dmkv, tm=D, tn=D, ts=nm, name="dw_mem_kv")
    dmn_g = _mm_nn(dmkv, wb["w_mem_kv"].T, out_dtype=F32, tm=nm, tn=D, tk=D, name="d_mem_norm")
    dgmem8 = _gain_grad(dmn_g, mem, rm, "dg_mem")
    dproj = jnp.concatenate(
        [dgates, dgq, dgk, dgv, dgg, dfq, dfk, dfv, dmq,
         (de_gla + de_fox).astype(BF16), jnp.zeros((s, DP_W - PM_W - PE_W), BF16)], axis=1)
    dwp = _mm_tn(u, dproj, tm=D, tn=1024, ts=t, name="dw_in")
    gw["w_in"] = jnp.concatenate([dwp[:, 3072:4608], dwp[:, PM_W:PM_W + 16], dwp[:, 4608:6144],
                                  dwp[:, PM_W + 16:PM_W + 24], dwp[:, 6144:6656], dwp[:, 0:3072]], axis=1)
    if exchange is None:
        grad_x, dgmix8 = _mm_norm_bwd(dproj, w_in_pt, x, r1, g_mix, dh1, name="d_x", want_bf16=False)
        exchanged = None
    else:
        grad_x, dgmix8, exchanged = _mm_norm_bwd(dproj, w_in_pt, x, r1, g_mix, dh1, name="d_x", want_bf16=False,
                                                 side=exchange.scatter(gw))
    gs = {"g_mix": dgmix8.sum(0), "g_mem": dgmem8.sum(0), "g_ffn": dgffn8.sum(0), "g_final": dgfin8.sum(0),
          "b_alpha": dba8.sum(0), "b_forget": dbf8.sum(0)[FF_LANE:FF_LANE + 8], "g_gla_head": dgh8.sum(0)}
    return loss, grad_x, gw, gs, exchanged


def kernel(x, mem, g_mix, w_in, w_alpha_up, b_alpha, b_forget, g_gla_head, g_mem, w_mem_kv, w_gla_o, w_fox_o, w_mem_o, w_out, g_ffn, w_ff1, w_ff2, g_final, loss_target, m_g_mix, m_w_in, m_w_alpha_up, m_b_alpha, m_b_forget, m_g_gla_head, m_g_mem, m_w_mem_kv, m_w_gla_o, m_w_fox_o, m_w_mem_o, m_w_out, m_g_ffn, m_w_ff1, m_w_ff2, m_g_final, v_g_mix, v_w_in, v_w_alpha_up, v_b_alpha, v_b_forget, v_g_gla_head, v_g_mem, v_w_mem_kv, v_w_gla_o, v_w_fox_o, v_w_mem_o, v_w_out, v_g_ffn, v_w_ff1, v_w_ff2, v_g_final):
    args = dict(locals())
    w_sh = {n: args[n][0] for n in WEIGHTS}
    m_sh = {n: args["m_" + n][0] for n in WEIGHTS}
    v_sh = {n: args["v_" + n][0] for n in WEIGHTS}
    small = {n: args[n] for n in SMALL}

    def whole(parts):
        return {n: jnp.concatenate([p[n] for p in parts], axis=SHARD_AXIS[n]) for n in parts[0]}

    class Exchange:
        gather = _gather_side(_pack_b(w_sh, BF16))

        @staticmethod
        def weights(gathered):
            return whole([_unpack_b(gathered[k]) for k in range(N_CHIPS)])

        @staticmethod
        def scatter(gw):
            by_chip = {n: _split_shards(n, gw[n]) for n in WEIGHTS}
            packed = jnp.stack([jnp.concatenate([_pack_a({n: by_chip[n][k] for n in WEIGHTS}, BF16),
                                                 _pack_b({n: by_chip[n][k] for n in WEIGHTS}, BF16)], axis=0)
                                for k in range(N_CHIPS)])
            hr = PACK_ROWS // 2
            mine = lax.dynamic_slice_in_dim(packed, lax.axis_index("c") * hr, hr, axis=1)
            return _scatter_side(_add2(mine, _swap_halves(packed), "chip_sum"))

    gathered_a = _gather_shards(_pack_a(w_sh, BF16))
    wb = whole([_unpack_a(gathered_a[k]) for k in range(N_CHIPS)])
    loss, grad_x, gw, gs, by_chip = _local_step(x[0], mem[0], loss_target[0], wb, small, Exchange)
    g_out = _unpack(_sum4(by_chip, "shard_sum"))
    d_out, m_out, v_out = {}, {}, {}
    for n in WEIGHTS:
        d_out[n], m_out[n], v_out[n] = _adam(w_sh[n], g_out[n], m_sh[n], v_sh[n], "adam_" + n)

    small_all = _gather_small(_pack_small(gs, loss))
    sm = {n: args["m_" + n] for n in SMALL}
    sv = {n: args["v_" + n] for n in SMALL}
    gs_sum, sd, snm, snv = _adam_small(_pack_small(small), small_all, _pack_small(sm), _pack_small(sv))
    gs_o, sd_o, snm_o, snv_o = _unpack_small(gs_sum), _unpack_small(sd), _unpack_small(snm), _unpack_small(snv)

    names = ["g_mix", "w_in", "w_alpha_up", "b_alpha", "b_forget", "g_gla_head", "g_mem", "w_mem_kv", "w_gla_o", "w_fox_o",
             "w_mem_o", "w_out", "g_ffn", "w_ff1", "w_ff2", "g_final"]

    def pick(big, sml, n):
        return big[n][None] if n in big else sml[n]

    outs = [gs_sum[6, 0], grad_x[None]]
    for big, sml in ((g_out, gs_o), (d_out, sd_o), (m_out, snm_o), (v_out, snv_o)):
        outs += [pick(big, sml, n) for n in names]
    return tuple(outs)
```

```python
import functools

import numpy as np
import jax
import jax.numpy as jnp
from jax import lax
from jax.experimental import pallas as pl
from jax.experimental.pallas import tpu as pltpu

F32 = jnp.float32
BF16 = jnp.bfloat16
HI = lax.Precision.HIGHEST
MESH = pl.DeviceIdType.MESH

EPS = 1e-6
D = 1024
CHUNK = 64
GLA_TAU = 16.0
N_CHIPS = 4
N_DEV = 8
VMEM_LIMIT_BYTES = 56 * 1024 * 1024

ADAM_LR, ADAM_B1, ADAM_B2, ADAM_EPS, ADAM_WD, ADAM_STEP = 0.001, 0.9, 0.999, 1e-08, 0.01, 10

PM_W = 6656
PE_W = 128
DP_W = 7168
C_GQ, C_GK, C_GV, C_GG, C_FQ, C_FK, C_FV, C_MQ = 3072, 3328, 3584, 4096, 4608, 5120, 5632, 6144
FF_LANE = 16

WEIGHTS = ("w_in", "w_alpha_up", "w_mem_kv", "w_gla_o", "w_fox_o", "w_mem_o", "w_out", "w_ff1", "w_ff2")
SHARD_AXIS = {"w_in": 1, "w_alpha_up": 1, "w_mem_kv": 0, "w_gla_o": 1, "w_fox_o": 1, "w_mem_o": 1, "w_out": 0,
              "w_ff1": 1, "w_ff2": 0}
SMALL = ("g_mix", "g_mem", "g_ffn", "g_final", "b_alpha", "b_forget", "g_gla_head")
PACK_W = 1024
PACK_ROWS_A = 2048
PACK_ROWS_B = 3104
PACK_ROWS = PACK_ROWS_A + PACK_ROWS_B


def _cp(*sem):
    return pltpu.CompilerParams(dimension_semantics=sem, vmem_limit_bytes=VMEM_LIMIT_BYTES)


def _dot(a, b, **kw):
    return jnp.dot(a, b, preferred_element_type=F32, **kw)


def _dot_nt(a, b, **kw):
    return lax.dot_general(a, b, (((1,), (1,)), ((), ())), preferred_element_type=F32, **kw)


def _dot_tn(a, b, **kw):
    return lax.dot_general(a, b, (((0,), (0,)), ((), ())), preferred_element_type=F32, **kw)


def _sigmoid(x):
    return 1.0 / (1.0 + jnp.exp(-x))


def _log_sigmoid(x):
    return -(jnp.maximum(-x, 0.0) + jnp.log1p(jnp.exp(-jnp.abs(x))))


def _fold8(x):
    m, n = x.shape
    return x.reshape(m // 8, 8, n).sum(axis=0)


def _iota(shape, dim):
    return lax.broadcasted_iota(jnp.int32, shape, dim)


def _row_tile(s):
    return min(s, 512)


class _Side:
    def __init__(self, inputs, out_shape, scratch, ops):
        self.inputs, self.out_shape, self.scratch, self.ops = list(inputs), list(out_shape), list(scratch), ops


ANY_SPEC = pl.BlockSpec(memory_space=pl.ANY)


def _mm_nn(a, b, *, out_dtype, tm, tn, tk, name, a_fn=None, epi=None, extra=None, side=None):
    m, k = a.shape
    _, n = b.shape
    nk = k // tk
    n_in = 2 + (extra is not None)
    n_sin = 0 if side is None else len(side.inputs)
    n_sout = 0 if side is None else len(side.out_shape)

    def body_one(*refs):
        a_ref, b_ref = refs[0], refs[1]
        o_ref = refs[n_in + n_sin]
        if side is not None:
            start, finish = side.ops(refs[n_in:n_in + n_sin], refs[n_in + n_sin + 1:n_in + n_sin + 1 + n_sout],
                                     refs[n_in + n_sin + 1 + n_sout:])
            pl.when((pl.program_id(0) == 0) & (pl.program_id(1) == 0))(start)
        at = a_ref[...] if a_fn is None else a_fn(a_ref[...])
        r = _dot(at, b_ref[...])
        if epi is not None:
            r = epi(r, None if extra is None else refs[2][...])
        o_ref[...] = r.astype(out_dtype)
        if side is not None:
            pl.when((pl.program_id(0) == m // tm - 1) & (pl.program_id(1) == n // tn - 1))(finish)

    if nk == 1:
        in_specs = [pl.BlockSpec((tm, k), lambda i, j: (i, 0)), pl.BlockSpec((k, tn), lambda i, j: (0, j))]
        args = [a, b]
        if extra is not None:
            in_specs.append(pl.BlockSpec((tm, tn), lambda i, j: (i, j)))
            args.append(extra)
        out_specs = pl.BlockSpec((tm, tn), lambda i, j: (i, j))
        out_shape = jax.ShapeDtypeStruct((m, n), out_dtype)
        if side is None:
            return pl.pallas_call(
                body_one, grid=(m // tm, n // tn), in_specs=in_specs, out_specs=out_specs, out_shape=out_shape,
                name=name, compiler_params=_cp("parallel", "parallel"))(*args)
        return pl.pallas_call(
            body_one, grid=(m // tm, n // tn), in_specs=in_specs + [ANY_SPEC] * n_sin,
            out_specs=[out_specs] + [ANY_SPEC] * n_sout, out_shape=[out_shape] + side.out_shape,
            scratch_shapes=side.scratch, name=name, compiler_params=_cp("arbitrary", "arbitrary"))(*args, *side.inputs)
    assert side is None

    def body(*refs):
        if extra is None:
            a_ref, b_ref, o_ref, acc = refs
            x_ref = None
        else:
            a_ref, b_ref, x_ref, o_ref, acc = refs
        kk = pl.program_id(2)

        @pl.when(kk == 0)
        def _():
            acc[...] = jnp.zeros_like(acc)

        at = a_ref[...]
        if a_fn is not None:
            at = a_fn(at)
        acc[...] += _dot(at, b_ref[...])

        @pl.when(kk == nk - 1)
        def _():
            r = acc[...]
            if epi is not None:
                r = epi(r, None if x_ref is None else x_ref[...])
            o_ref[...] = r.astype(out_dtype)

    in_specs = [pl.BlockSpec((tm, tk), lambda i, j, kk: (i, kk)), pl.BlockSpec((tk, tn), lambda i, j, kk: (kk, j))]
    args = [a, b]
    if extra is not None:
        in_specs.append(pl.BlockSpec((tm, tn), lambda i, j, kk: (i, j)))
        args.append(extra)
    return pl.pallas_call(
        body, grid=(m // tm, n // tn, nk), in_specs=in_specs,
        out_specs=pl.BlockSpec((tm, tn), lambda i, j, kk: (i, j)),
        out_shape=jax.ShapeDtypeStruct((m, n), out_dtype),
        scratch_shapes=[pltpu.VMEM((tm, tn), F32)], name=name,
        compiler_params=_cp("parallel", "parallel", "arbitrary"))(*args)


def _mm_tn(a, b, *, tm, tn, ts, name, a_fn=None):
    s, m = a.shape
    _, n = b.shape
    ns = s // ts

    def body(a_ref, b_ref, o_ref, acc):
        kk = pl.program_id(2)

        @pl.when(kk == 0)
        def _():
            acc[...] = jnp.zeros_like(acc)

        at = a_ref[...]
        if a_fn is not None:
            at = a_fn(at)
        acc[...] += _dot_tn(at, b_ref[...])

        @pl.when(kk == ns - 1)
        def _():
            o_ref[...] = acc[...]

    return pl.pallas_call(
        body, grid=(m // tm, n // tn, ns),
        in_specs=[pl.BlockSpec((ts, tm), lambda i, j, kk: (kk, i)), pl.BlockSpec((ts, tn), lambda i, j, kk: (kk, j))],
        out_specs=pl.BlockSpec((tm, tn), lambda i, j, kk: (i, j)),
        out_shape=jax.ShapeDtypeStruct((m, n), F32),
        scratch_shapes=[pltpu.VMEM((tm, tn), F32)], name=name,
        compiler_params=_cp("parallel", "parallel", "arbitrary"))(a, b)


def _relu2_bf16(t):
    r = jnp.maximum(t.astype(F32), 0.0)
    return (r * r).astype(BF16)


def _rms_fwd(x, g, name):
    s, d = x.shape
    tm = min(s, 512)

    def body(x_ref, g_ref, u_ref, r_ref):
        xv = x_ref[...]
        r = lax.rsqrt(jnp.mean(xv * xv, axis=-1, keepdims=True) + EPS)
        u_ref[...] = ((xv * r) * g_ref[...]).astype(BF16)
        r_ref[...] = r

    return pl.pallas_call(
        body, grid=(s // tm,),
        in_specs=[pl.BlockSpec((tm, d), lambda i: (i, 0)), pl.BlockSpec((1, d), lambda i: (0, 0))],
        out_specs=[pl.BlockSpec((tm, d), lambda i: (i, 0)), pl.BlockSpec((tm, 1), lambda i: (i, 0))],
        out_shape=[jax.ShapeDtypeStruct((s, d), BF16), jax.ShapeDtypeStruct((s, 1), F32)],
        name=name, compiler_params=_cp("parallel"))(x, g)


def _mm_norm_bwd(a, b, xin, r, g, dres, *, name, want_bf16, side=None):
    s, k = a.shape
    tm = min(s, 256)
    n_out = 3 if want_bf16 else 2
    n_sin = 0 if side is None else len(side.inputs)
    n_sout = 0 if side is None else len(side.out_shape)

    def body(a_ref, b_ref, x_ref, r_ref, g_ref, dres_ref, *rest):
        outs = rest[n_sin:n_sin + n_out]
        dx_ref, dg_ref = outs[0], outs[-1]
        if side is not None:
            start, finish = side.ops(rest[:n_sin], rest[n_sin + n_out:n_sin + n_out + n_sout], rest[n_sin + n_out + n_sout:])
            pl.when(pl.program_id(0) == 0)(start)

        @pl.when(pl.program_id(0) == 0)
        def _():
            dg_ref[...] = jnp.zeros_like(dg_ref)

        du = _dot(a_ref[...], b_ref[...])
        xn = x_ref[...] * r_ref[...]
        dg_ref[...] += _fold8(du * xn)
        dxn = du * g_ref[...]
        dx = dres_ref[...] + r_ref[...] * (dxn - xn * jnp.mean(dxn * xn, axis=-1, keepdims=True))
        dx_ref[...] = dx
        if want_bf16:
            outs[1][...] = dx.astype(BF16)
        if side is not None:
            pl.when(pl.program_id(0) == s // tm - 1)(finish)

    row = lambda i: (i, 0)
    const = lambda i: (0, 0)
    out_specs = [pl.BlockSpec((tm, D), row)]
    out_shape = [jax.ShapeDtypeStruct((s, D), F32)]
    if want_bf16:
        out_specs.append(pl.BlockSpec((tm, D), row))
        out_shape.append(jax.ShapeDtypeStruct((s, D), BF16))
    out_specs.append(pl.BlockSpec((8, D), const))
    out_shape.append(jax.ShapeDtypeStruct((8, D), F32))
    side_in = [] if side is None else side.inputs
    return pl.pallas_call(
        body, grid=(s // tm,),
        in_specs=[pl.BlockSpec((tm, k), row), pl.BlockSpec((k, D), const, pipeline_mode=pl.Buffered(1)),
                  pl.BlockSpec((tm, D), row), pl.BlockSpec((tm, 1), row), pl.BlockSpec((1, D), const),
                  pl.BlockSpec((tm, D), row)] + [ANY_SPEC] * n_sin,
        out_specs=out_specs + [ANY_SPEC] * n_sout, out_shape=out_shape + ([] if side is None else side.out_shape),
        scratch_shapes=[] if side is None else side.scratch,
        name=name, compiler_params=_cp("arbitrary"))(a, b, xin, r, g, dres, *side_in)


def _gla_consts():
    lmask = _iota((4 * CHUNK, CHUNK), 0) % CHUNK >= _iota((4 * CHUNK, CHUNK), 1)
    hmask = _iota((256, 256), 0) // CHUNK == _iota((256, 256), 1) // CHUNK
    bd = _iota((256, 512), 0) // CHUNK == _iota((256, 512), 1) // 128
    return lmask, hmask, bd


def _fold_heads(x):
    return x[0:64] + x[64:128] + x[128:192] + x[192:256]


def _gla_chunk(lac, qc, kc):
    tri = (_iota((CHUNK, CHUNK), 0) >= _iota((CHUNK, CHUNK), 1)).astype(F32)
    b = _dot(tri, lac, precision=HI)
    bl = b[CHUNK - 1:CHUNK, :]
    ep, en, ek = jnp.exp(b), jnp.exp(-b), jnp.exp(bl - b)
    decb = jnp.exp(_dot_tn(lac, jnp.ones((CHUNK, 128), F32), precision=HI))
    decb = jnp.concatenate([decb] * 4, axis=1)
    return bl, ep, en, ek, decb, qc * ep, qc * en, kc * en, kc * ep, kc * ek


def _gla_fwd(pm, pe, wau_p, b_alpha):
    s = pm.shape[0]
    t = _row_tile(s)
    nc = t // CHUNK

    def body(q_ref, k_ref, v_ref, e_ref, wau_ref, ba_ref, o_ref, st_ref, state, la_scr):
        @pl.when(pl.program_id(0) == 0)
        def _():
            state[...] = jnp.zeros_like(state)

        z = _dot(e_ref[...].astype(BF16), wau_ref[...]) + ba_ref[...]
        la_scr[...] = _log_sigmoid(z) * (1.0 / GLA_TAU)
        lmask, hmask, bd = _gla_consts()

        def chunk(c, carry):
            rows = pl.ds(pl.multiple_of(c * CHUNK, CHUNK), CHUNK)
            qc = q_ref[rows, :].astype(F32) * 0.125
            kc = k_ref[rows, :].astype(F32)
            vc = v_ref[rows, :]
            _, _, _, _, decb, qp, qn, kn, kp, kk = _gla_chunk(la_scr[rows, :], qc, kc)
            qs = jnp.where(hmask, jnp.concatenate([qp] * 4, axis=0), 0.0).astype(BF16)
            qns = jnp.where(hmask, jnp.concatenate([qn] * 4, axis=0), 0.0).astype(BF16)
            attn = jnp.where(lmask, _dot_nt(qs, kn.astype(BF16)), _dot_nt(qns, kp.astype(BF16))).astype(BF16)
            st = state[...]
            o_intra = _fold_heads(jnp.where(bd, _dot(attn, vc), 0.0))
            o_ref[rows, :] = o_intra + _dot(qp.astype(BF16), st.astype(BF16))
            for h in range(4):
                st_ref[c, :, 128 * h:128 * (h + 1)] = st[64 * h:64 * (h + 1), 128 * h:128 * (h + 1)]
            kv = jnp.where(bd, _dot_tn(kk.astype(BF16), vc), 0.0)
            state[...] = st * decb + kv
            return carry

        lax.fori_loop(0, nc, chunk, 0)

    return pl.pallas_call(
        body, grid=(s // t,),
        in_specs=[pl.BlockSpec((t, 256), lambda i: (i, C_GQ // 256)), pl.BlockSpec((t, 256), lambda i: (i, C_GK // 256)),
                  pl.BlockSpec((t, 512), lambda i: (i, C_GV // 512)), pl.BlockSpec((t, PE_W), lambda i: (i, 0)),
                  pl.BlockSpec((PE_W, 256), lambda i: (0, 0)), pl.BlockSpec((1, 256), lambda i: (0, 0))],
        out_specs=[pl.BlockSpec((t, 512), lambda i: (i, 0)), pl.BlockSpec((nc, CHUNK, 512), lambda i: (i, 0, 0))],
        out_shape=[jax.ShapeDtypeStruct((s, 512), F32), jax.ShapeDtypeStruct((s // CHUNK, CHUNK, 512), F32)],
        scratch_shapes=[pltpu.VMEM((256, 512), F32), pltpu.VMEM((t, 256), F32)],
        name="gla_fwd", compiler_params=_cp("arbitrary"))(pm, pm, pm, pe, wau_p, b_alpha)


def _gla_bwd(pm, pe, wau_p, wau_pt, b_alpha, do, states):
    s = pm.shape[0]
    t = _row_tile(s)
    nc = t // CHUNK
    nb = s // t

    def body(q_ref, k_ref, v_ref, e_ref, wau_ref, waut_ref, ba_ref, do_ref, st_ref,
             dq_ref, dk_ref, dv_ref, de_ref, dwau_ref, dba_ref, gstate, la_scr, dla_scr):
        @pl.when(pl.program_id(0) == 0)
        def _():
            gstate[...] = jnp.zeros_like(gstate)
            dwau_ref[...] = jnp.zeros_like(dwau_ref)
            dba_ref[...] = jnp.zeros_like(dba_ref)

        eb = e_ref[...].astype(BF16)
        z = _dot(eb, wau_ref[...]) + ba_ref[...]
        la_scr[...] = _log_sigmoid(z) * (1.0 / GLA_TAU)
        lmask, hmask, bd = _gla_consts()
        triu = (_iota((CHUNK, CHUNK), 0) <= _iota((CHUNK, CHUNK), 1)).astype(F32)
        last_row = _iota((CHUNK, 256), 0) == CHUNK - 1

        def chunk(cc, carry):
            c = nc - 1 - cc
            rows = pl.ds(pl.multiple_of(c * CHUNK, CHUNK), CHUNK)
            qc = q_ref[rows, :].astype(F32) * 0.125
            kc = k_ref[rows, :].astype(F32)
            vc = v_ref[rows, :]
            dob = do_ref[rows, :]
            bl, ep, en, ek, decb, qp, qn, kn, kp, kk = _gla_chunk(la_scr[rows, :], qc, kc)
            qs = jnp.where(hmask, jnp.concatenate([qp] * 4, axis=0), 0.0).astype(BF16)
            qns = jnp.where(hmask, jnp.concatenate([qn] * 4, axis=0), 0.0).astype(BF16)
            knb, kpb = kn.astype(BF16), kp.astype(BF16)
            attn = jnp.where(lmask, _dot_nt(qs, knb), _dot_nt(qns, kpb)).astype(BF16)
            st = jnp.where(bd, jnp.concatenate([st_ref[c]] * 4, axis=0), 0.0)
            g = gstate[...]
            gb = g.astype(BF16)
            do_s = jnp.where(bd, jnp.concatenate([dob] * 4, axis=0), jnp.zeros((), BF16))
            dattn = _dot_nt(do_s, vc)
            dv_ref[rows, :] = (_dot_tn(attn, do_s) + _dot(kk.astype(BF16), gb)).astype(BF16)
            dac = jnp.where(lmask, dattn, 0.0).astype(BF16)
            daa = jnp.where(lmask, 0.0, dattn).astype(BF16)
            dqp = _fold_heads(jnp.where(hmask, _dot(dac, knb), 0.0)) + _dot_nt(dob, st.astype(BF16))
            dqn = _fold_heads(jnp.where(hmask, _dot(daa, kpb), 0.0))
            dkn = _dot_tn(dac, qs)
            dkp = _dot_tn(daa, qns)
            dkk = _dot_nt(vc, gb)
            ddec = _dot_nt(jnp.ones((8, 512), F32), g * st, precision=HI)[0:1, :]
            gstate[...] = decb * g + jnp.where(bd, _dot_tn(qp.astype(BF16), dob), 0.0)
            dq_ref[rows, :] = ((dqp * ep + dqn * en) * 0.125).astype(BF16)
            dk_ref[rows, :] = (dkn * en + dkp * ep + dkk * ek).astype(BF16)
            dek = dkk * kc * ek
            db = (dqp * qc + dkp * kc) * ep - (dqn * qc + dkn * kc) * en - dek
            dbl = jnp.sum(dek, axis=0, keepdims=True) + ddec * jnp.exp(bl)
            db = db + jnp.where(last_row, dbl, 0.0)
            dla_scr[rows, :] = _dot(triu, db, precision=HI)
            return carry

        lax.fori_loop(0, nc, chunk, 0)
        dz = dla_scr[...] * (1.0 / GLA_TAU) * _sigmoid(-z)
        dzb = dz.astype(BF16)
        dwau_ref[...] += _dot_tn(eb, dzb)
        dba_ref[...] += _fold8(dz)
        de_ref[...] = _dot(dzb, waut_ref[...])

    rev = lambda i: nb - 1 - i
    return pl.pallas_call(
        body, grid=(nb,),
        in_specs=[pl.BlockSpec((t, 256), lambda i: (rev(i), C_GQ // 256)), pl.BlockSpec((t, 256), lambda i: (rev(i), C_GK // 256)),
                  pl.BlockSpec((t, 512), lambda i: (rev(i), C_GV // 512)), pl.BlockSpec((t, PE_W), lambda i: (rev(i), 0)),
                  pl.BlockSpec((PE_W, 256), lambda i: (0, 0)), pl.BlockSpec((256, PE_W), lambda i: (0, 0)),
                  pl.BlockSpec((1, 256), lambda i: (0, 0)), pl.BlockSpec((t, 512), lambda i: (rev(i), 0)),
                  pl.BlockSpec((nc, CHUNK, 512), lambda i: (rev(i), 0, 0))],
        out_specs=[pl.BlockSpec((t, 256), lambda i: (rev(i), 0)), pl.BlockSpec((t, 256), lambda i: (rev(i), 0)),
                   pl.BlockSpec((t, 512), lambda i: (rev(i), 0)), pl.BlockSpec((t, PE_W), lambda i: (rev(i), 0)),
                   pl.BlockSpec((PE_W, 256), lambda i: (0, 0)), pl.BlockSpec((8, 256), lambda i: (0, 0))],
        out_shape=[jax.ShapeDtypeStruct((s, 256), BF16), jax.ShapeDtypeStruct((s, 256), BF16),
                   jax.ShapeDtypeStruct((s, 512), BF16), jax.ShapeDtypeStruct((s, PE_W), F32),
                   jax.ShapeDtypeStruct((PE_W, 256), F32), jax.ShapeDtypeStruct((8, 256), F32)],
        scratch_shapes=[pltpu.VMEM((256, 512), F32), pltpu.VMEM((t, 256), F32), pltpu.VMEM((t, 256), F32)],
        name="gla_bwd", compiler_params=_cp("arbitrary"))(pm, pm, pm, pe, wau_p, wau_pt, b_alpha, do, states)


def _fcum_fwd(pe, bias):
    s = pe.shape[0]
    t = min(s, 256)

    def body(e_ref, b_ref, f_ref, carry):
        @pl.when(pl.program_id(0) == 0)
        def _():
            carry[...] = jnp.zeros_like(carry)

        lf = _log_sigmoid(e_ref[...] + b_ref[...])
        tri = (_iota((t, t), 0) >= _iota((t, t), 1)).astype(F32)
        f = _dot(tri, lf, precision=HI) + carry[0:1, :]
        f_ref[...] = f
        carry[...] = jnp.broadcast_to(f[t - 1:t, :], carry.shape)

    return pl.pallas_call(
        body, grid=(s // t,),
        in_specs=[pl.BlockSpec((t, PE_W), lambda i: (i, 0)), pl.BlockSpec((1, PE_W), lambda i: (0, 0))],
        out_specs=pl.BlockSpec((t, PE_W), lambda i: (i, 0)),
        out_shape=jax.ShapeDtypeStruct((s, PE_W), F32), scratch_shapes=[pltpu.VMEM((8, PE_W), F32)],
        name="fcum_fwd", compiler_params=_cp("arbitrary"))(pe, bias)


def _fcum_bwd(pe, bias, df):
    s = pe.shape[0]
    t = min(s, 256)
    nb = s // t

    def body(e_ref, b_ref, df_ref, de_ref, db_ref, carry):
        @pl.when(pl.program_id(0) == 0)
        def _():
            carry[...] = jnp.zeros_like(carry)
            db_ref[...] = jnp.zeros_like(db_ref)

        triu = (_iota((t, t), 0) <= _iota((t, t), 1)).astype(F32)
        dlf = _dot(triu, df_ref[...], precision=HI) + carry[0:1, :]
        carry[...] = jnp.broadcast_to(dlf[0:1, :], carry.shape)
        lane = _iota((t, PE_W), 1)
        dff = jnp.where((lane >= FF_LANE) & (lane < FF_LANE + 8), dlf * _sigmoid(-(e_ref[...] + b_ref[...])), 0.0)
        de_ref[...] = dff
        db_ref[...] += _fold8(dff)

    rev = lambda i: (nb - 1 - i, 0)
    return pl.pallas_call(
        body, grid=(nb,),
        in_specs=[pl.BlockSpec((t, PE_W), rev), pl.BlockSpec((1, PE_W), lambda i: (0, 0)), pl.BlockSpec((t, PE_W), rev)],
        out_specs=[pl.BlockSpec((t, PE_W), rev), pl.BlockSpec((8, PE_W), lambda i: (0, 0))],
        out_shape=[jax.ShapeDtypeStruct((s, PE_W), F32), jax.ShapeDtypeStruct((8, PE_W), F32)],
        scratch_shapes=[pltpu.VMEM((8, PE_W), F32)],
        name="fcum_bwd", compiler_params=_cp("arbitrary"))(pe, bias, df)


FOX_WIDE = 1024


def _split3(x):
    hi = x.astype(BF16)
    r = x - hi.astype(F32)
    mid = r.astype(BF16)
    lo = (r - mid.astype(F32)).astype(BF16)
    return jnp.concatenate([hi, mid, lo], axis=1)


def _fox_tables():
    heads, lane = np.arange(8), np.arange(64)
    spread = np.zeros((512, 1024), np.float32)
    spread[(64 * heads[:, None] + lane).ravel(), (128 * heads[:, None] + lane).ravel()] = 1.0
    def place(src_lane0, dst_off, val):
        t = np.zeros((384, 1024), np.float32)
        for p in range(3):
            t[128 * p + src_lane0 + heads, 128 * heads + dst_off + p] = val
        return t
    def const(off, val):
        c = np.zeros((1, 1024), np.float32)
        for p in range(3):
            c[0, 128 * heads + off + p] = val
        return c
    rows = np.zeros((8, 128), np.float32)
    rows[heads, FF_LANE + heads] = 1.0
    bf = lambda a: jnp.asarray(a, BF16)
    return dict(spread=bf(spread),
                f_to_q=bf(place(FF_LANE, 64, 1.0)), f_to_k=bf(place(FF_LANE, 67, -1.0)), d_to_do=bf(place(0, 64, 1.0)),
                ones_q=jnp.asarray(const(67, 1.0)), ones_k=jnp.asarray(const(64, 1.0)), ones_v=jnp.asarray(const(64, -1.0)),
                rows=jnp.asarray(rows))


LOG2E = 1.4426950408889634


def _fox_prep(pm, f128, lse8, tb, *, backward):
    s = pm.shape[0]
    tm = _row_tile(s)

    def body(*refs):
        if backward:
            q_ref, f_ref, lse_ref, sp_ref, fq_ref, cq_ref, rows_ref, qa_ref = refs
            f = f_ref[...] * LOG2E - _dot_tn(lse_ref[...], rows_ref[...], precision=HI)
            q2 = (q_ref[...].astype(F32) * (0.125 * LOG2E)).astype(BF16)
            qa_ref[...] = (_dot(q2, sp_ref[...]) + _dot(_split3(f), fq_ref[...]) + cq_ref[...]).astype(BF16)
            return
        (q_ref, k_ref, v_ref, f_ref, sp_ref, fq_ref, fk_ref, cq_ref, ck_ref, cv_ref,
         qa_ref, ka_ref, va_ref, vt_ref, qt_ref, kt_ref) = refs
        f3 = _split3(f_ref[...] * LOG2E)
        q, k, v = q_ref[...].astype(F32), k_ref[...], v_ref[...]
        sp = sp_ref[...]
        qa_ref[...] = (_dot((q * (0.125 * LOG2E)).astype(BF16), sp) + _dot(f3, fq_ref[...]) + cq_ref[...]).astype(BF16)
        ka_ref[...] = (_dot(k, sp) + _dot(f3, fk_ref[...]) + ck_ref[...]).astype(BF16)
        va_ref[...] = (_dot(v, sp) + cv_ref[...]).astype(BF16)
        vt_ref[...] = v.T
        qt_ref[...] = (q * 0.125).astype(BF16).T
        kt_ref[...] = (k.astype(F32) * 0.125).astype(BF16).T

    row = lambda i: (i, 0)
    const = lambda i: (0, 0)
    blk = lambda c: pl.BlockSpec((tm, 512), lambda i: (i, c // 512))
    wide = pl.BlockSpec((tm, 1024), row)
    mat = lambda a: pl.BlockSpec(a.shape, const)
    if backward:
        ins = [pm, f128, lse8, tb["spread"], tb["f_to_q"], tb["ones_q"], tb["rows"]]
        in_specs = [blk(C_FQ), pl.BlockSpec((tm, 128), row), pl.BlockSpec((8, tm), lambda i: (0, i))] + [mat(a) for a in ins[3:]]
        out_specs, out_shape = wide, jax.ShapeDtypeStruct((s, 1024), BF16)
    else:
        ins = [pm, pm, pm, f128, tb["spread"], tb["f_to_q"], tb["f_to_k"], tb["ones_q"], tb["ones_k"], tb["ones_v"]]
        in_specs = [blk(C_FQ), blk(C_FK), blk(C_FV), pl.BlockSpec((tm, 128), row)] + [mat(a) for a in ins[4:]]
        tr = pl.BlockSpec((512, tm), lambda i: (0, i))
        out_specs = [wide, wide, wide, tr, tr, tr]
        out_shape = [jax.ShapeDtypeStruct((s, 1024), BF16)] * 3 + [jax.ShapeDtypeStruct((512, s), BF16)] * 3
    return pl.pallas_call(body, grid=(s // tm,), in_specs=in_specs, out_specs=out_specs, out_shape=out_shape,
                          name="fox_prep_bwd" if backward else "fox_prep", compiler_params=_cp("parallel"))(*ins)


def _fox_post(dqt, dkt, dvt, rowsum8, colsum8, tb):
    s = dqt.shape[1]
    tm = _row_tile(s)

    def body(dqt_ref, dkt_ref, dvt_ref, rs_ref, cs_ref, rows_ref, dfq_ref, dfk_ref, dfv_ref, df_ref):
        dfq_ref[...] = dqt_ref[...].T.astype(BF16)
        dfk_ref[...] = dkt_ref[...].T
        dfv_ref[...] = dvt_ref[...].T
        df_ref[...] = _dot_tn(rs_ref[...] - cs_ref[...], rows_ref[...], precision=HI)

    row = lambda i: (i, 0)
    tr = pl.BlockSpec((512, tm), lambda i: (0, i))
    out = pl.BlockSpec((tm, 512), row)
    heads = pl.BlockSpec((8, tm), lambda i: (0, i))
    return pl.pallas_call(
        body, grid=(s // tm,),
        in_specs=[tr, tr, tr, heads, heads, pl.BlockSpec((8, 128), lambda i: (0, 0))],
        out_specs=[out, out, out, pl.BlockSpec((tm, 128), row)],
        out_shape=[jax.ShapeDtypeStruct((s, 512), BF16)] * 3 + [jax.ShapeDtypeStruct((s, 128), F32)],
        name="fox_post", compiler_params=_cp("parallel"))(dqt, dkt, dvt, rowsum8, colsum8, tb["rows"])


def _fox_fwd(k_aug, q_aug, vt):
    s = k_aug.shape[0]
    nh = 8
    tk = _row_tile(s)
    tq = min(s, FOX_WIDE)
    per = tq // tk

    def body(k_ref, q_ref, v_ref, o_ref, lse_ref, sbuf):
        i = pl.program_id(1)
        qa = q_ref[...]

        def scores(j):
            return _dot_nt(k_ref[pl.ds(pl.multiple_of(j * tk, tk), tk), :], qa)

        def update(st, j, carry):
            m, l, acc = carry
            m2 = jnp.maximum(m, jnp.max(st, axis=0, keepdims=True))
            p = jnp.exp2(st - m2)
            a = jnp.exp2(m - m2)
            vj = v_ref[:, pl.ds(pl.multiple_of(j * tk, tk), tk)]
            return m2, a * l + jnp.sum(p, axis=0, keepdims=True), a * acc + _dot(vj, p.astype(BF16))

        def step(a, carry):
            sbuf[1] = scores(2 * a + 1)
            carry = update(sbuf[0], 2 * a, carry)
            sbuf[0] = scores(2 * a + 2)
            return update(sbuf[1], 2 * a + 1, carry)

        n = i * per
        sbuf[0] = scores(0)
        carry = (jnp.full((1, tq), -1e30, F32), jnp.zeros((1, tq), F32), jnp.zeros((64, tq), F32))
        carry = lax.fori_loop(0, n // 2, step, carry)
        for r in range(per):
            j = n + r
            st = sbuf[0] if r == 0 else scores(j)
            st = jnp.where(j * tk + _iota((tk, tq), 0) <= i * tq + _iota((tk, tq), 1), st, -1e30)
            carry = update(st, j, carry)
        m, l, acc = carry
        o_ref[...] = (acc / l).astype(BF16)
        lse_ref[0] = m + jnp.log2(l)

    return pl.pallas_call(
        body, grid=(nh, s // tq),
        in_specs=[pl.BlockSpec((s, 128), lambda h, i: (0, h)), pl.BlockSpec((tq, 128), lambda h, i: (i, h)),
                  pl.BlockSpec((64, s), lambda h, i: (h, 0))],
        out_specs=[pl.BlockSpec((64, tq), lambda h, i: (h, i)), pl.BlockSpec((1, 1, tq), lambda h, i: (h, 0, i))],
        out_shape=[jax.ShapeDtypeStruct((512, s), BF16), jax.ShapeDtypeStruct((nh, 1, s), F32)],
        scratch_shapes=[pltpu.VMEM((2, tk, tq), F32)],
        name="fox_fwd", compiler_params=_cp("parallel", "arbitrary"))(k_aug, q_aug, vt)


def _fox_bwd(q_aug, do_aug, qt, dot_, k_aug, v_aug, kt):
    s = q_aug.shape[0]
    nh = 8
    tq = _row_tile(s)
    tk = min(s, FOX_WIDE)
    per = tk // tq
    nqb = s // tq

    def body(qa_ref, da_ref, qt_ref, dt_ref, ka_ref, va_ref, kt_ref, dq_ref, rs_ref, dk_ref, dv_ref, dfk_ref):
        j = pl.program_id(1)

        @pl.when(j == 0)
        def _():
            dq_ref[...] = jnp.zeros_like(dq_ref)
            rs_ref[...] = jnp.zeros_like(rs_ref)

        ka, va, ks = ka_ref[...], va_ref[...], kt_ref[...]

        def tile(i, masked, carry):
            dk, dv, dfk = carry
            rows = pl.ds(pl.multiple_of(i * tq, tq), tq)
            sp = _dot_nt(qa_ref[rows, :], ka)
            if masked:
                sp = jnp.where(i * tq + _iota((tq, tk), 0) >= j * tk + _iota((tq, tk), 1), sp, -1e30)
            p = jnp.exp2(sp)
            ds = p * _dot_nt(da_ref[rows, :], va)
            dsb = ds.astype(BF16)
            dq_ref[:, rows] += _dot_nt(ks, dsb)
            rs_ref[0, :, rows] += jnp.broadcast_to(jnp.sum(ds, axis=1, keepdims=True), (tq, 128)).T[0:8]
            return (dk + _dot(qt_ref[:, rows], dsb), dv + _dot(dt_ref[:, rows], p.astype(BF16)),
                    dfk + jnp.sum(ds, axis=0, keepdims=True))

        carry = (jnp.zeros((64, tk), F32), jnp.zeros((64, tk), F32), jnp.zeros((1, tk), F32))
        for r in range(per):
            carry = tile(j * per + r, True, carry)
        dk, dv, dfk = lax.fori_loop((j + 1) * per, nqb, lambda i, c: tile(i, False, c), carry)
        dk_ref[...] = dk.astype(BF16)
        dv_ref[...] = dv.astype(BF16)
        dfk_ref[0] = dfk

    head_cols = lambda h, j: (0, h)
    head_rows = lambda h, j: (h, 0)
    once = dict(pipeline_mode=pl.Buffered(1))
    return pl.pallas_call(
        body, grid=(nh, s // tk),
        in_specs=[pl.BlockSpec((s, 128), head_cols, **once), pl.BlockSpec((s, 128), head_cols, **once),
                  pl.BlockSpec((64, s), head_rows, **once), pl.BlockSpec((64, s), head_rows, **once),
                  pl.BlockSpec((tk, 128), lambda h, j: (j, h)), pl.BlockSpec((tk, 128), lambda h, j: (j, h)),
                  pl.BlockSpec((64, tk), lambda h, j: (h, j))],
        out_specs=[pl.BlockSpec((64, s), head_rows), pl.BlockSpec((1, 8, s), lambda h, j: (h, 0, 0)),
                   pl.BlockSpec((64, tk), lambda h, j: (h, j)),
                   pl.BlockSpec((64, tk), lambda h, j: (h, j)), pl.BlockSpec((1, 1, tk), lambda h, j: (h, 0, j))],
        out_shape=[jax.ShapeDtypeStruct((512, s), F32), jax.ShapeDtypeStruct((nh, 8, s), F32),
                   jax.ShapeDtypeStruct((512, s), BF16),
                   jax.ShapeDtypeStruct((512, s), BF16), jax.ShapeDtypeStruct((nh, 1, s), F32)],
        name="fox_bwd", compiler_params=_cp("parallel", "arbitrary"))(q_aug, do_aug, qt, dot_, k_aug, v_aug, kt)


MEM_SCALE = 128 ** -0.5


def _mem_attn_fwd(pm, mkv):
    s = pm.shape[0]
    t = _row_tile(s)
    nm = mkv.shape[0]

    def body(q_ref, mk_ref, mv_ref, o_ref):
        for h in range(4):
            cols = slice(128 * h, 128 * (h + 1))
            sc = _dot_nt(q_ref[:, cols], mk_ref[:, cols]) * MEM_SCALE
            p = jnp.exp(sc - jnp.max(sc, axis=-1, keepdims=True))
            p = p / jnp.sum(p, axis=-1, keepdims=True)
            o_ref[:, cols] = _dot(p.astype(BF16), mv_ref[:, cols]).astype(BF16)

    return pl.pallas_call(
        body, grid=(s // t,),
        in_specs=[pl.BlockSpec((t, 512), lambda i: (i, C_MQ // 512)), pl.BlockSpec((nm, 512), lambda i: (0, 0)),
                  pl.BlockSpec((nm, 512), lambda i: (0, 1))],
        out_specs=pl.BlockSpec((t, 512), lambda i: (i, 0)),
        out_shape=jax.ShapeDtypeStruct((s, 512), BF16),
        name="mem_attn_fwd", compiler_params=_cp("parallel"))(pm, mkv, mkv)


def _mem_attn_bwd(pm, mkv, do):
    s = pm.shape[0]
    t = _row_tile(s)
    nm = mkv.shape[0]

    def body(q_ref, mk_ref, mv_ref, do_ref, dq_ref, dmk_ref, dmv_ref):
        @pl.when(pl.program_id(0) == 0)
        def _():
            dmk_ref[...] = jnp.zeros_like(dmk_ref)
            dmv_ref[...] = jnp.zeros_like(dmv_ref)

        for h in range(4):
            cols = slice(128 * h, 128 * (h + 1))
            qh, kh, vh, doh = q_ref[:, cols], mk_ref[:, cols], mv_ref[:, cols], do_ref[:, cols]
            sc = _dot_nt(qh, kh) * MEM_SCALE
            p = jnp.exp(sc - jnp.max(sc, axis=-1, keepdims=True))
            p = p / jnp.sum(p, axis=-1, keepdims=True)
            pb = p.astype(BF16)
            dp = _dot_nt(doh, vh)
            ds = (p * (dp - jnp.sum(p * dp, axis=-1, keepdims=True)) * MEM_SCALE).astype(BF16)
            dq_ref[:, cols] = _dot(ds, kh).astype(BF16)
            dmk_ref[:, cols] += _dot_tn(ds, qh)
            dmv_ref[:, cols] += _dot_tn(pb, doh)

    return pl.pallas_call(
        body, grid=(s // t,),
        in_specs=[pl.BlockSpec((t, 512), lambda i: (i, C_MQ // 512)), pl.BlockSpec((nm, 512), lambda i: (0, 0)),
                  pl.BlockSpec((nm, 512), lambda i: (0, 1)), pl.BlockSpec((t, 512), lambda i: (i, 0))],
        out_specs=[pl.BlockSpec((t, 512), lambda i: (i, 0)), pl.BlockSpec((nm, 512), lambda i: (0, 0)),
                   pl.BlockSpec((nm, 512), lambda i: (0, 0))],
        out_shape=[jax.ShapeDtypeStruct((s, 512), BF16), jax.ShapeDtypeStruct((nm, 512), F32),
                   jax.ShapeDtypeStruct((nm, 512), F32)],
        name="mem_attn_bwd", compiler_params=_cp("arbitrary"))(pm, mkv, mkv, do)


def _gain_grad(dxn_g, x, r, name):
    m, d = x.shape

    def body(d_ref, x_ref, r_ref, o_ref):
        o_ref[...] = _fold8(d_ref[...] * (x_ref[...] * r_ref[...]))

    return pl.pallas_call(body, out_shape=jax.ShapeDtypeStruct((8, d), F32), name=name,
                          compiler_params=pltpu.CompilerParams(vmem_limit_bytes=VMEM_LIMIT_BYTES))(dxn_g, x, r)


def _head_norm(o, gh):
    xs, rs = [], []
    for h in range(4):
        oh = o[:, 128 * h:128 * (h + 1)]
        r = lax.rsqrt(jnp.mean(oh * oh, axis=-1, keepdims=True) + EPS)
        xs.append(oh * r)
        rs.append(r)
    return xs, rs


def _merge_fwd(x, pm, o_gla, o_fox_t, o_mem, g_head, wg, wf, wm, wo, g_ffn):
    s = x.shape[0]
    t = min(s, 256)

    def body(x_ref, g0_ref, g1_ref, g2_ref, gg_ref, og_ref, of_ref, om_ref, gh_ref, wg_ref, wf_ref, wm_ref, wo_ref, gf_ref,
             mg_ref, h1_ref, u2_ref, r2_ref):
        xs, _ = _head_norm(og_ref[...], None)
        gg = gg_ref[...].astype(F32)
        sil = gg * _sigmoid(gg)
        ogn = jnp.concatenate(xs, axis=1) * gh_ref[...] * sil
        merged = (_sigmoid(g0_ref[...].astype(F32)) * _dot(ogn.astype(BF16), wg_ref[...])
                  + _sigmoid(g1_ref[...].astype(F32)) * _dot(of_ref[...].T, wf_ref[...])
                  + _sigmoid(g2_ref[...].astype(F32)) * _dot(om_ref[...], wm_ref[...]))
        mb = merged.astype(BF16)
        mg_ref[...] = mb
        h1 = x_ref[...] + _dot(mb, wo_ref[...])
        h1_ref[...] = h1
        r = lax.rsqrt(jnp.mean(h1 * h1, axis=-1, keepdims=True) + EPS)
        u2_ref[...] = ((h1 * r) * gf_ref[...]).astype(BF16)
        r2_ref[...] = r

    row = lambda i: (i, 0)
    const = lambda i: (0, 0)
    return pl.pallas_call(
        body, grid=(s // t,),
        in_specs=[pl.BlockSpec((t, D), row), pl.BlockSpec((t, D), lambda i: (i, 0)), pl.BlockSpec((t, D), lambda i: (i, 1)),
                  pl.BlockSpec((t, D), lambda i: (i, 2)), pl.BlockSpec((t, 512), lambda i: (i, C_GG // 512)),
                  pl.BlockSpec((t, 512), row), pl.BlockSpec((512, t), lambda i: (0, i)), pl.BlockSpec((t, 512), row),
                  pl.BlockSpec((1, 512), const), pl.BlockSpec((512, D), const), pl.BlockSpec((512, D), const),
                  pl.BlockSpec((512, D), const), pl.BlockSpec((D, D), const), pl.BlockSpec((1, D), const)],
        out_specs=[pl.BlockSpec((t, D), row), pl.BlockSpec((t, D), row), pl.BlockSpec((t, D), row), pl.BlockSpec((t, 1), row)],
        out_shape=[jax.ShapeDtypeStruct((s, D), BF16), jax.ShapeDtypeStruct((s, D), F32),
                   jax.ShapeDtypeStruct((s, D), BF16), jax.ShapeDtypeStruct((s, 1), F32)],
        name="merge_fwd", compiler_params=_cp("parallel"))(x, pm, pm, pm, pm, o_gla, o_fox_t, o_mem, g_head, wg, wf, wm, wo, g_ffn)


def _merge_bwd(dh1b, pm, o_gla, o_fox_t, o_mem, g_head, wg, wf, wm, wgt, wft, wmt, wot, spread, d_to_do):
    s = dh1b.shape[0]
    t = min(s, 256)

    def body(dh_ref, g0_ref, g1_ref, g2_ref, gg_ref, og_ref, of_ref, om_ref, gh_ref, wg_ref, wf_ref, wm_ref,
             wgt_ref, wft_ref, wmt_ref, wot_ref, sp_ref, dd_ref,
             dgt_ref, dgg_ref, dog_ref, da_ref, dot_ref, dom_ref, dwg_ref, dwf_ref, dwm_ref, dgh_ref):
        @pl.when(pl.program_id(0) == 0)
        def _():
            dwg_ref[...] = jnp.zeros_like(dwg_ref)
            dwf_ref[...] = jnp.zeros_like(dwf_ref)
            dwm_ref[...] = jnp.zeros_like(dwm_ref)
            dgh_ref[...] = jnp.zeros_like(dgh_ref)

        dmerged = _dot(dh_ref[...], wot_ref[...])
        og = og_ref[...]
        xs, rs = _head_norm(og, None)
        on = jnp.concatenate(xs, axis=1)
        gg = gg_ref[...].astype(F32)
        sg = _sigmoid(gg)
        sil = gg * sg
        gh = gh_ref[...]
        ognb = (on * gh * sil).astype(BF16)
        ofb, omb = of_ref[...].T, om_ref[...]
        douts = []
        for idx, (gref, ob, w_ref, wt_ref, dw_ref) in enumerate((
                (g0_ref, ognb, wg_ref, wgt_ref, dwg_ref), (g1_ref, ofb, wf_ref, wft_ref, dwf_ref),
                (g2_ref, omb, wm_ref, wmt_ref, dwm_ref))):
            gt = _sigmoid(gref[...].astype(F32))
            y = _dot(ob, w_ref[...])
            dgt_ref[:, D * idx:D * (idx + 1)] = (dmerged * y * gt * (1.0 - gt)).astype(BF16)
            dy = (gt * dmerged).astype(BF16)
            dw_ref[...] += _dot_tn(ob, dy)
            douts.append(_dot(dy, wt_ref[...]))
        dogn, dof, dom = douts
        dofb = dof.astype(BF16)
        dom_ref[...] = dom.astype(BF16)
        ind = (_iota((512, 128), 0) // 64 == _iota((512, 128), 1)).astype(F32)
        delta = _dot(dofb.astype(F32) * ofb.astype(F32), ind, precision=HI)
        da_ref[...] = (_dot(dofb, sp_ref[...]) + _dot(_split3(delta), dd_ref[...])).astype(BF16)
        dot_ref[...] = dofb.T
        dgg_ref[...] = (dogn * on * gh * (sg * (1.0 + gg * (1.0 - sg)))).astype(BF16)
        d_on = dogn * sil
        dgh_ref[...] += _fold8(d_on * on)
        dxn = d_on * gh
        outs = []
        for h in range(4):
            cols = slice(128 * h, 128 * (h + 1))
            dh_, xh = dxn[:, cols], xs[h]
            outs.append(rs[h] * (dh_ - xh * jnp.mean(dh_ * xh, axis=-1, keepdims=True)))
        dog_ref[...] = jnp.concatenate(outs, axis=1).astype(BF16)

    row = lambda i: (i, 0)
    const = lambda i: (0, 0)
    return pl.pallas_call(
        body, grid=(s // t,),
        in_specs=[pl.BlockSpec((t, D), row), pl.BlockSpec((t, D), lambda i: (i, 0)), pl.BlockSpec((t, D), lambda i: (i, 1)),
                  pl.BlockSpec((t, D), lambda i: (i, 2)), pl.BlockSpec((t, 512), lambda i: (i, C_GG // 512)),
                  pl.BlockSpec((t, 512), row), pl.BlockSpec((512, t), lambda i: (0, i)), pl.BlockSpec((t, 512), row),
                  pl.BlockSpec((1, 512), const), pl.BlockSpec((512, D), const), pl.BlockSpec((512, D), const),
                  pl.BlockSpec((512, D), const), pl.BlockSpec((D, 512), const), pl.BlockSpec((D, 512), const),
                  pl.BlockSpec((D, 512), const), pl.BlockSpec((D, D), const),
                  pl.BlockSpec((512, 1024), const), pl.BlockSpec((384, 1024), const)],
        out_specs=[pl.BlockSpec((t, 3 * D), row), pl.BlockSpec((t, 512), row), pl.BlockSpec((t, 512), row),
                   pl.BlockSpec((t, 1024), row), pl.BlockSpec((512, t), lambda i: (0, i)), pl.BlockSpec((t, 512), row),
                   pl.BlockSpec((512, D), const), pl.BlockSpec((512, D), const), pl.BlockSpec((512, D), const),
                   pl.BlockSpec((8, 512), const)],
        out_shape=[jax.ShapeDtypeStruct((s, 3 * D), BF16), jax.ShapeDtypeStruct((s, 512), BF16),
                   jax.ShapeDtypeStruct((s, 512), BF16), jax.ShapeDtypeStruct((s, 1024), BF16),
                   jax.ShapeDtypeStruct((512, s), BF16), jax.ShapeDtypeStruct((s, 512), BF16),
                   jax.ShapeDtypeStruct((512, D), F32), jax.ShapeDtypeStruct((512, D), F32),
                   jax.ShapeDtypeStruct((512, D), F32), jax.ShapeDtypeStruct((8, 512), F32)],
        name="merge_bwd", compiler_params=_cp("arbitrary"))(
            dh1b, pm, pm, pm, pm, o_gla, o_fox_t, o_mem, g_head, wg, wf, wm, wgt, wft, wmt, wot, spread, d_to_do)


def _ff2_loss(a, w2, h1, g_final, target):
    s, k = a.shape
    tm = min(s, 256)

    def body(a_ref, w_ref, h1_ref, g_ref, t_ref, dh_ref, dhb_ref, loss_ref, dg_ref):
        @pl.when(pl.program_id(0) == 0)
        def _():
            loss_ref[...] = jnp.zeros_like(loss_ref)
            dg_ref[...] = jnp.zeros_like(dg_ref)

        h2 = h1_ref[...] + _dot(_relu2_bf16(a_ref[...]), w_ref[...])
        r = lax.rsqrt(jnp.mean(h2 * h2, axis=-1, keepdims=True) + EPS)
        xn = h2 * r
        g = g_ref[...]
        err = xn * g - t_ref[...]
        e2 = _fold8(err * err)
        part = e2[:, 0:128]
        for c in range(1, D // 128):
            part = part + e2[:, 128 * c:128 * (c + 1)]
        loss_ref[...] += part
        dy = err * (1.0 / D)
        dg_ref[...] += _fold8(dy * xn)
        dxn = dy * g
        dh = r * (dxn - xn * jnp.mean(dxn * xn, axis=-1, keepdims=True))
        dh_ref[...] = dh
        dhb_ref[...] = dh.astype(BF16)

    row = lambda i: (i, 0)
    const = lambda i: (0, 0)
    return pl.pallas_call(
        body, grid=(s // tm,),
        in_specs=[pl.BlockSpec((tm, k), row), pl.BlockSpec((k, D), const, pipeline_mode=pl.Buffered(1)),
                  pl.BlockSpec((tm, D), row), pl.BlockSpec((1, D), const), pl.BlockSpec((tm, D), row)],
        out_specs=[pl.BlockSpec((tm, D), row), pl.BlockSpec((tm, D), row), pl.BlockSpec((8, 128), const),
                   pl.BlockSpec((8, D), const)],
        out_shape=[jax.ShapeDtypeStruct((s, D), F32), jax.ShapeDtypeStruct((s, D), BF16),
                   jax.ShapeDtypeStruct((8, 128), F32), jax.ShapeDtypeStruct((8, D), F32)],
        name="ff2_loss", compiler_params=_cp("arbitrary"))(a, w2, h1, g_final, target)


def _adam(w, g, m, v, name):
    r, c = w.shape
    tr = r
    for cand in (512, 256, 128, 64, 32, 16, 8):
        if r % cand == 0 and cand * c * 4 <= (1 << 20):
            tr = cand
            break
    c1 = 1.0 - ADAM_B1 ** ADAM_STEP
    c2 = 1.0 - ADAM_B2 ** ADAM_STEP

    def body(w_ref, g_ref, m_ref, v_ref, d_ref, nm_ref, nv_ref):
        gv = g_ref[...]
        nm = ADAM_B1 * m_ref[...] + (1.0 - ADAM_B1) * gv
        nv = ADAM_B2 * v_ref[...] + (1.0 - ADAM_B2) * (gv * gv)
        d_ref[...] = -ADAM_LR * ((nm / c1) / (jnp.sqrt(nv / c2) + ADAM_EPS) + ADAM_WD * w_ref[...])
        nm_ref[...] = nm
        nv_ref[...] = nv

    spec = pl.BlockSpec((tr, c), lambda i: (i, 0))
    return pl.pallas_call(
        body, grid=(r // tr,), in_specs=[spec] * 4, out_specs=[spec] * 3,
        out_shape=[jax.ShapeDtypeStruct((r, c), F32)] * 3, name=name, compiler_params=_cp("parallel"))(w, g, m, v)


def _row_block(r):
    return max(d for d in range(16, 513, 16) if r % d == 0)


def _add2(a, b, name):
    n, r, c = a.shape
    tr = _row_block(r)

    def body(a_ref, b_ref, o_ref):
        o_ref[...] = (a_ref[...].astype(F32) + b_ref[...].astype(F32)).astype(BF16)

    spec = pl.BlockSpec((1, tr, c), lambda k, i: (k, i, 0))
    return pl.pallas_call(body, grid=(n, r // tr), in_specs=[spec, spec], out_specs=spec,
                          out_shape=jax.ShapeDtypeStruct((n, r, c), BF16), name=name,
                          compiler_params=_cp("parallel", "parallel"))(a, b)


def _sum4(a, name):
    _, r, c = a.shape
    tr = _row_block(r)

    def body(a_ref, o_ref):
        o_ref[...] = ((a_ref[0].astype(F32) + a_ref[1].astype(F32)) + a_ref[2].astype(F32)) + a_ref[3].astype(F32)

    return pl.pallas_call(body, grid=(r // tr,), in_specs=[pl.BlockSpec((4, tr, c), lambda i: (0, i, 0))],
                          out_specs=pl.BlockSpec((tr, c), lambda i: (i, 0)),
                          out_shape=jax.ShapeDtypeStruct((r, c), F32), name=name, compiler_params=_cp("parallel"))(a)


def _adam_small(w, gathered, m, v):
    c1 = 1.0 - ADAM_B1 ** ADAM_STEP
    c2 = 1.0 - ADAM_B2 ** ADAM_STEP

    def body(w_ref, g_ref, m_ref, v_ref, gs_ref, d_ref, nm_ref, nv_ref):
        gv = g_ref[0]
        for dev in range(1, N_DEV):
            gv = gv + g_ref[dev]
        gs_ref[...] = gv
        nm = ADAM_B1 * m_ref[...] + (1.0 - ADAM_B1) * gv
        nv = ADAM_B2 * v_ref[...] + (1.0 - ADAM_B2) * (gv * gv)
        d_ref[...] = -ADAM_LR * ((nm / c1) / (jnp.sqrt(nv / c2) + ADAM_EPS) + ADAM_WD * w_ref[...])
        nm_ref[...] = nm
        nv_ref[...] = nv

    return pl.pallas_call(body, out_shape=[jax.ShapeDtypeStruct((8, D), F32)] * 4, name="adam_small")(w, gathered, m, v)


def _place():
    return lax.axis_index("x"), lax.axis_index("y"), lax.axis_index("c")


def _other_chips(x, y):
    return [(1 - x, y), (x, 1 - y), (1 - x, 1 - y)]


def _gather_shards(p):
    def body(p_ref, out_ref, *sems):
        start, finish = _gather_ops((p_ref,), (out_ref,), sems)
        start()
        finish()

    return pl.pallas_call(
        body, out_shape=jax.ShapeDtypeStruct((N_CHIPS,) + p.shape, p.dtype), in_specs=[ANY_SPEC], out_specs=ANY_SPEC,
        scratch_shapes=GATHER_SEMS, name="gather_shards")(p)


GATHER_SEMS = [pltpu.SemaphoreType.DMA((6,)), pltpu.SemaphoreType.DMA((6,)), pltpu.SemaphoreType.DMA]


def _gather_ops(in_refs, out_refs, sems):
    (p_ref,), (out_ref,) = in_refs, out_refs
    send_sems, recv_sems, local_sem = sems
    hr = p_ref.shape[0] // 2
    x, y, cc = _place()
    sibling = (x, y, 1 - cc)
    chips = _other_chips(x, y)

    def half(chip, core):
        return out_ref.at[2 * chip[0] + chip[1], pl.ds(core * hr, hr), :]

    def copy(k, chip, core, to, src=None):
        return pltpu.make_async_remote_copy(
            src_ref=half(chip, core) if src is None else src, dst_ref=half(chip, core),
            send_sem=send_sems.at[k], recv_sem=recv_sems.at[k], device_id=to, device_id_type=MESH)

    mine = pltpu.make_async_copy(p_ref, out_ref.at[2 * x + y], local_sem)
    my_half = p_ref.at[pl.ds(cc * hr, hr), :]
    first = [copy(j, (x, y), cc, (*chip, cc), src=my_half) for j, chip in enumerate(chips)]
    passed = [copy(3 + j, chip, cc, sibling) for j, chip in enumerate(chips)]

    def start():
        mine.start()
        for cp in first:
            cp.start()

    def finish():
        for j, chip in enumerate(chips):
            copy(j, chip, cc, (x, y, cc)).wait_recv()
            passed[j].start()
        for j, chip in enumerate(chips):
            copy(3 + j, chip, 1 - cc, (x, y, cc)).wait_recv()
        for cp in first + passed:
            cp.wait_send()
        mine.wait()

    return start, finish


def _gather_side(p):
    return _Side([p], [jax.ShapeDtypeStruct((N_CHIPS,) + p.shape, p.dtype)], GATHER_SEMS, _gather_ops)


def _swap_halves(g):
    n, r, c = g.shape
    hr = r // 2

    def body(g_ref, out_ref, send_sem, recv_sem):
        x, y, cc = _place()
        cp = pltpu.make_async_remote_copy(
            src_ref=g_ref.at[:, pl.ds((1 - cc) * hr, hr), :], dst_ref=out_ref,
            send_sem=send_sem, recv_sem=recv_sem, device_id=(x, y, 1 - cc), device_id_type=MESH)
        cp.start()
        cp.wait()

    any_spec = pl.BlockSpec(memory_space=pl.ANY)
    return pl.pallas_call(
        body, out_shape=jax.ShapeDtypeStruct((n, hr, c), g.dtype), in_specs=[any_spec], out_specs=any_spec,
        scratch_shapes=[pltpu.SemaphoreType.DMA, pltpu.SemaphoreType.DMA], name="swap_halves")(g)


SCATTER_SEMS = [pltpu.SemaphoreType.DMA((7,)), pltpu.SemaphoreType.DMA((7,)), pltpu.SemaphoreType.DMA]


def _scatter_ops(in_refs, out_refs, sems):
    (p_ref,), (out_ref,) = in_refs, out_refs
    send_sems, recv_sems, local_sem = sems
    hr = p_ref.shape[1]
    x, y, cc = _place()
    me = 2 * x + y
    sibling = (x, y, 1 - cc)
    chips = _other_chips(x, y)
    ids = [2 * chip[0] + chip[1] for chip in chips]

    def land(src, core):
        return out_ref.at[src, pl.ds(core * hr, hr), :]

    def copy(k, src_ref, dst_ref, to):
        return pltpu.make_async_remote_copy(src_ref=src_ref, dst_ref=dst_ref, send_sem=send_sems.at[k],
                                            recv_sem=recv_sems.at[k], device_id=to, device_id_type=MESH)

    mine = pltpu.make_async_copy(p_ref.at[me], land(me, cc), local_sem)
    sends = [copy(j, p_ref.at[ids[j]], land(me, cc), (*chip, cc)) for j, chip in enumerate(chips)]
    sends.append(copy(3, p_ref.at[me], land(me, cc), sibling))
    passed = [copy(4 + j, land(ids[j], cc), land(ids[j], cc), sibling) for j in range(3)]

    def start():
        mine.start()
        for cp in sends:
            cp.start()

    def finish():
        for j in range(3):
            copy(j, p_ref.at[me], land(ids[j], cc), (x, y, cc)).wait_recv()
            passed[j].start()
        copy(3, p_ref.at[me], land(me, 1 - cc), (x, y, cc)).wait_recv()
        for j in range(3):
            copy(4 + j, p_ref.at[me], land(ids[j], 1 - cc), (x, y, cc)).wait_recv()
        for cp in sends + passed:
            cp.wait_send()
        mine.wait()

    return start, finish


def _scatter_side(p):
    n, hr, c = p.shape
    return _Side([p], [jax.ShapeDtypeStruct((n, 2 * hr, c), p.dtype)], SCATTER_SEMS, _scatter_ops)


def _gather_small(blk):
    m, n = blk.shape

    def body(x_ref, out_ref, send_sems, recv_sems, local_sem):
        x, y, cc = _place()
        me, sibling = (x, y, cc), (x, y, 1 - cc)
        chips = _other_chips(x, y)

        def slot(px, py, pc):
            return out_ref.at[4 * px + 2 * py + pc]

        def copy(k, block, to, src=None):
            return pltpu.make_async_remote_copy(
                src_ref=slot(*block) if src is None else src, dst_ref=slot(*block),
                send_sem=send_sems.at[k], recv_sem=recv_sems.at[k], device_id=to, device_id_type=MESH)

        mine = pltpu.make_async_copy(x_ref, slot(*me), local_sem)
        mine.start()
        first = [copy(0, me, sibling, src=x_ref)]
        first += [copy(1 + j, me, (*chip, cc), src=x_ref) for j, chip in enumerate(chips)]
        for cp in first:
            cp.start()
        passed = [copy(4 + j, (*chip, cc), sibling) for j, chip in enumerate(chips)]
        for j, chip in enumerate(chips):
            copy(1 + j, (*chip, cc), me).wait_recv()
            passed[j].start()
        copy(0, sibling, me).wait_recv()
        for j, chip in enumerate(chips):
            copy(4 + j, (*chip, 1 - cc), me).wait_recv()
        for cp in first + passed:
            cp.wait_send()
        mine.wait()

    vmem = pl.BlockSpec(memory_space=pltpu.VMEM)
    return pl.pallas_call(
        body, out_shape=jax.ShapeDtypeStruct((N_DEV, m, n), blk.dtype), in_specs=[vmem], out_specs=vmem,
        scratch_shapes=[pltpu.SemaphoreType.DMA((7,)), pltpu.SemaphoreType.DMA((7,)), pltpu.SemaphoreType.DMA],
        name="gather_small")(blk)


def _shard_shape(name, full_shape):
    shp = list(full_shape)
    shp[SHARD_AXIS[name]] //= N_CHIPS
    return tuple(shp)


FULL_SHAPES = {"w_in": (D, 6680), "w_alpha_up": (16, 256), "w_mem_kv": (D, D), "w_gla_o": (512, D), "w_fox_o": (512, D),
               "w_mem_o": (512, D), "w_out": (D, D), "w_ff1": (D, 4 * D), "w_ff2": (4 * D, D)}


def _pack_a(sh, dtype):
    w = sh["w_in"].astype(dtype)
    return jnp.concatenate([w[:, 0:PACK_W], jnp.pad(w[:, PACK_W:], ((0, 0), (0, 2 * PACK_W - w.shape[1])))], axis=0)


def _pack_b(sh, dtype):
    o3 = jnp.concatenate([sh["w_gla_o"], sh["w_fox_o"], sh["w_mem_o"], jnp.zeros((512, 256), sh["w_gla_o"].dtype)], axis=1)
    au = jnp.pad(sh["w_alpha_up"], ((0, PACK_ROWS_B - 3072 - 16), (0, PACK_W - 64)))
    return jnp.concatenate([sh["w_ff1"], sh["w_ff2"], sh["w_mem_kv"], sh["w_out"], o3, au], axis=0).astype(dtype)


def _unpack_a(pa):
    return {"w_in": jnp.concatenate([pa[0:1024], pa[1024:2048, 0:1670 - PACK_W]], axis=1)}


def _unpack_b(pb):
    return {"w_ff1": pb[0:1024], "w_ff2": pb[1024:2048], "w_mem_kv": pb[2048:2304], "w_out": pb[2304:2560],
            "w_gla_o": pb[2560:3072, 0:256], "w_fox_o": pb[2560:3072, 256:512], "w_mem_o": pb[2560:3072, 512:768],
            "w_alpha_up": pb[3072:3088, 0:64]}


def _unpack(packed):
    return {**_unpack_a(packed[0:PACK_ROWS_A]), **_unpack_b(packed[PACK_ROWS_A:])}


def _split_shards(name, full):
    return jnp.split(full, N_CHIPS, axis=SHARD_AXIS[name])


def _pack_small(vals, scalar=None):
    row4 = jnp.concatenate([vals["b_alpha"].reshape(-1), vals["b_forget"].reshape(-1), jnp.zeros((D - 264,), F32)])
    row5 = jnp.concatenate([vals["g_gla_head"].reshape(-1), jnp.zeros((D - 512,), F32)])
    row6 = jnp.zeros((D,), F32) if scalar is None else jnp.broadcast_to(scalar, (D,))
    rows = [vals["g_mix"].reshape(-1), vals["g_mem"].reshape(-1), vals["g_ffn"].reshape(-1), vals["g_final"].reshape(-1),
            row4, row5, row6, jnp.zeros((D,), F32)]
    return jnp.stack(rows)


def _unpack_small(blk):
    return {"g_mix": blk[0].reshape(1, D), "g_mem": blk[1].reshape(1, D), "g_ffn": blk[2].reshape(1, D),
            "g_final": blk[3].reshape(D), "b_alpha": blk[4, 0:256].reshape(1, 256), "b_forget": blk[4, 256:264].reshape(1, 8),
            "g_gla_head": blk[5, 0:512].reshape(1, 4, 128)}


def _local_step(x, mem, target, wb, small, exchange=None):
    s = x.shape[0]
    nm = mem.shape[0]
    t = _row_tile(s)
    nb = s // t
    w_in = wb["w_in"]
    w_main = jnp.concatenate([w_in[:, 3608:6680], w_in[:, 0:1536], w_in[:, 1552:3088], w_in[:, 3096:3608]], axis=1)
    w_e = jnp.concatenate([w_in[:, 1536:1552], w_in[:, 3088:3096], jnp.zeros((D, PE_W - 24), BF16)], axis=1)
    w_in_pt = jnp.concatenate([w_main, w_e, jnp.zeros((D, DP_W - PM_W - PE_W), BF16)], axis=1).T
    b_alpha = small["b_alpha"].reshape(1, 256)
    bias_e = jnp.concatenate([jnp.zeros((FF_LANE,), F32), small["b_forget"].reshape(-1),
                              jnp.zeros((PE_W - FF_LANE - 8,), F32)]).reshape(1, PE_W)
    g_mix, g_mem, g_ffn = small["g_mix"].reshape(1, D), small["g_mem"].reshape(1, D), small["g_ffn"].reshape(1, D)
    g_final = small["g_final"].reshape(1, D)
    g_head = small["g_gla_head"].reshape(1, 512)

    u, r1 = _rms_fwd(x, g_mix, "norm_mix")
    big = min(s, 1024)
    if exchange is None:
        pm = _mm_nn(u, w_main, out_dtype=BF16, tm=big, tn=PM_W // 4, tk=D, name="proj_main")
    else:
        pm, gathered = _mm_nn(u, w_main, out_dtype=BF16, tm=big, tn=PM_W // 4, tk=D, name="proj_main", side=exchange.gather)
        wb = {**wb, **exchange.weights(gathered)}
    wau_p = jnp.concatenate([wb["w_alpha_up"], jnp.zeros((PE_W - 16, 256), BF16)], axis=0)
    pe = _mm_nn(u, w_e, out_dtype=F32, tm=t, tn=PE_W, tk=D, name="proj_narrow")
    o_gla, states = _gla_fwd(pm, pe, wau_p, b_alpha)
    fcum = _fcum_fwd(pe, bias_e)
    tb = _fox_tables()
    qf_aug, k_aug, v_aug, vt, qt, kt = _fox_prep(pm, fcum, None, tb, backward=False)
    o_fox, lse = _fox_fwd(k_aug, qf_aug, vt)
    mn, rm = _rms_fwd(mem, g_mem, "norm_mem")
    mkv = _mm_nn(mn, wb["w_mem_kv"], out_dtype=BF16, tm=nm, tn=512, tk=D, name="mem_kv")
    o_mem = _mem_attn_fwd(pm, mkv)
    merged, h1, u2, r2 = _merge_fwd(x, pm, o_gla, o_fox, o_mem, g_head, wb["w_gla_o"], wb["w_fox_o"], wb["w_mem_o"],
                                    wb["w_out"], g_ffn)
    a = _mm_nn(u2, wb["w_ff1"], out_dtype=BF16, tm=big, tn=1024, tk=D, name="ff1")
    dh2, dh2b, loss8, dgfin8 = _ff2_loss(a, wb["w_ff2"], h1, g_final, target)
    loss = 0.5 * jnp.sum(loss8) / D

    da = _mm_nn(dh2b, wb["w_ff2"].T, out_dtype=BF16, tm=t, tn=1024, tk=D, name="d_act",
                epi=lambda acc, at: acc * (2.0 * jnp.maximum(at.astype(F32), 0.0)), extra=a)
    gw = {}
    gw["w_ff2"] = _mm_tn(a, dh2b, tm=1024, tn=D, ts=t, name="dw_ff2", a_fn=_relu2_bf16)
    gw["w_ff1"] = _mm_tn(u2, da, tm=D, tn=1024, ts=t, name="dw_ff1")
    dh1, dh1b, dgffn8 = _mm_norm_bwd(da, wb["w_ff1"].T, h1, r2, g_ffn, dh2, name="d_h1", want_bf16=True)
    gw["w_out"] = _mm_tn(merged, dh1b, tm=D, tn=D, ts=t, name="dw_out")
    (dgates, dgg, do_gla, do_aug, do_t, do_mem, gw["w_gla_o"], gw["w_fox_o"], gw["w_mem_o"], dgh8) = _merge_bwd(
        dh1b, pm, o_gla, o_fox, o_mem, g_head, wb["w_gla_o"], wb["w_fox_o"], wb["w_mem_o"],
        wb["w_gla_o"].T, wb["w_fox_o"].T, wb["w_mem_o"].T, wb["w_out"].T, tb["spread"], tb["d_to_do"])
    dgq, dgk, dgv, de_gla, dwau_p, dba8 = _gla_bwd(pm, pe, wau_p, wau_p.T, b_alpha, do_gla, states)
    gw["w_alpha_up"] = dwau_p[0:16, :]
    q_aug = _fox_prep(pm, fcum, lse.reshape(8, s), tb, backward=True)
    dfq_t, dfrow, dfk_t, dfv_t, dfcol = _fox_bwd(q_aug, do_aug, qt, do_t, k_aug, v_aug, kt)
    dfq, dfk, dfv, df = _fox_post(dfq_t, dfk_t, dfv_t, dfrow[:, 0, :], dfcol.reshape(8, s), tb)
    de_fox, dbf8 = _fcum_bwd(pe, bias_e, df)
    dmq, dmk, dmv = _mem_attn_bwd(pm, mkv, do_mem)
    dmkv = jnp.concatenate([dmk, dmv], axis=1).astype(BF16)
    gw["w_mem_kv"] = _mm_tn(mn, dmkv, tm=D, tn=D, ts=nm, name="dw_mem_kv")
    dmn_g = _mm_nn(dmkv, wb["w_mem_kv"].T, out_dtype=F32, tm=nm, tn=D, tk=D, name="d_mem_norm")
    dgmem8 = _gain_grad(dmn_g, mem, rm, "dg_mem")
    dproj = jnp.concatenate(
        [dgates, dgq, dgk, dgv, dgg, dfq, dfk, dfv, dmq,
         (de_gla + de_fox).astype(BF16), jnp.zeros((s, DP_W - PM_W - PE_W), BF16)], axis=1)
    dwp = _mm_tn(u, dproj, tm=D, tn=1024, ts=t, name="dw_in")
    gw["w_in"] = jnp.concatenate([dwp[:, 3072:4608], dwp[:, PM_W:PM_W + 16], dwp[:, 4608:6144],
                                  dwp[:, PM_W + 16:PM_W + 24], dwp[:, 6144:6656], dwp[:, 0:3072]], axis=1)
    if exchange is None:
        grad_x, dgmix8 = _mm_norm_bwd(dproj, w_in_pt, x, r1, g_mix, dh1, name="d_x", want_bf16=False)
        exchanged = None
    else:
        grad_x, dgmix8, exchanged = _mm_norm_bwd(dproj, w_in_pt, x, r1, g_mix, dh1, name="d_x", want_bf16=False,
                                                 side=exchange.scatter(gw))
    gs = {"g_mix": dgmix8.sum(0), "g_mem": dgmem8.sum(0), "g_ffn": dgffn8.sum(0), "g_final": dgfin8.sum(0),
          "b_alpha": dba8.sum(0), "b_forget": dbf8.sum(0)[FF_LANE:FF_LANE + 8], "g_gla_head": dgh8.sum(0)}
    return loss, grad_x, gw, gs, exchanged


def kernel(x, mem, g_mix, w_in, w_alpha_up, b_alpha, b_forget, g_gla_head, g_mem, w_mem_kv, w_gla_o, w_fox_o, w_mem_o, w_out, g_ffn, w_ff1, w_ff2, g_final, loss_target, m_g_mix, m_w_in, m_w_alpha_up, m_b_alpha, m_b_forget, m_g_gla_head, m_g_mem, m_w_mem_kv, m_w_gla_o, m_w_fox_o, m_w_mem_o, m_w_out, m_g_ffn, m_w_ff1, m_w_ff2, m_g_final, v_g_mix, v_w_in, v_w_alpha_up, v_b_alpha, v_b_forget, v_g_gla_head, v_g_mem, v_w_mem_kv, v_w_gla_o, v_w_fox_o, v_w_mem_o, v_w_out, v_g_ffn, v_w_ff1, v_w_ff2, v_g_final):
    args = dict(locals())
    w_sh = {n: args[n][0] for n in WEIGHTS}
    m_sh = {n: args["m_" + n][0] for n in WEIGHTS}
    v_sh = {n: args["v_" + n][0] for n in WEIGHTS}
    small = {n: args[n] for n in SMALL}

    def whole(parts):
        return {n: jnp.concatenate([p[n] for p in parts], axis=SHARD_AXIS[n]) for n in parts[0]}

    class Exchange:
        gather = _gather_side(_pack_b(w_sh, BF16))

        @staticmethod
        def weights(gathered):
            return whole([_unpack_b(gathered[k]) for k in range(N_CHIPS)])

        @staticmethod
        def scatter(gw):
            by_chip = {n: _split_shards(n, gw[n]) for n in WEIGHTS}
            packed = jnp.stack([jnp.concatenate([_pack_a({n: by_chip[n][k] for n in WEIGHTS}, BF16),
                                                 _pack_b({n: by_chip[n][k] for n in WEIGHTS}, BF16)], axis=0)
                                for k in range(N_CHIPS)])
            hr = PACK_ROWS // 2
            mine = lax.dynamic_slice_in_dim(packed, lax.axis_index("c") * hr, hr, axis=1)
            return _scatter_side(_add2(mine, _swap_halves(packed), "chip_sum"))

    gathered_a = _gather_shards(_pack_a(w_sh, BF16))
    wb = whole([_unpack_a(gathered_a[k]) for k in range(N_CHIPS)])
    loss, grad_x, gw, gs, by_chip = _local_step(x[0], mem[0], loss_target[0], wb, small, Exchange)
    g_out = _unpack(_sum4(by_chip, "shard_sum"))
    d_out, m_out, v_out = {}, {}, {}
    for n in WEIGHTS:
        d_out[n], m_out[n], v_out[n] = _adam(w_sh[n], g_out[n], m_sh[n], v_sh[n], "adam_" + n)

    small_all = _gather_small(_pack_small(gs, loss))
    sm = {n: args["m_" + n] for n in SMALL}
    sv = {n: args["v_" + n] for n in SMALL}
    gs_sum, sd, snm, snv = _adam_small(_pack_small(small), small_all, _pack_small(sm), _pack_small(sv))
    gs_o, sd_o, snm_o, snv_o = _unpack_small(gs_sum), _unpack_small(sd), _unpack_small(snm), _unpack_small(snv)

    names = ["g_mix", "w_in", "w_alpha_up", "b_alpha", "b_forget", "g_gla_head", "g_mem", "w_mem_kv", "w_gla_o", "w_fox_o",
             "w_mem_o", "w_out", "g_ffn", "w_ff1", "w_ff2", "g_final"]

    def pick(big, sml, n):
        return big[n][None] if n in big else sml[n]

    outs = [gs_sum[6, 0], grad_x[None]]
    for big, sml in ((g_out, gs_o), (d_out, sd_o), (m_out, snm_o), (v_out, snv_o)):
        outs += [pick(big, sml, n) for n in names]
    return tuple(outs)
```

```python
import functools

import numpy as np
import jax
import jax.numpy as jnp
from jax import lax
from jax.experimental import pallas as pl
from jax.experimental.pallas import tpu as pltpu

F32 = jnp.float32
BF16 = jnp.bfloat16
HI = lax.Precision.HIGHEST
MESH = pl.DeviceIdType.MESH

EPS = 1e-6
D = 1024
CHUNK = 64
GLA_TAU = 16.0
N_CHIPS = 4
N_DEV = 8
VMEM_LIMIT_BYTES = 56 * 1024 * 1024

ADAM_LR, ADAM_B1, ADAM_B2, ADAM_EPS, ADAM_WD, ADAM_STEP = 0.001, 0.9, 0.999, 1e-08, 0.01, 10

PM_W = 6656
PE_W = 128
DP_W = 7168
C_GQ, C_GK, C_GV, C_GG, C_FQ, C_FK, C_FV, C_MQ = 3072, 3328, 3584, 4096, 4608, 5120, 5632, 6144
FF_LANE = 16

WEIGHTS = ("w_in", "w_alpha_up", "w_mem_kv", "w_gla_o", "w_fox_o", "w_mem_o", "w_out", "w_ff1", "w_ff2")
SHARD_AXIS = {"w_in": 1, "w_alpha_up": 1, "w_mem_kv": 0, "w_gla_o": 1, "w_fox_o": 1, "w_mem_o": 1, "w_out": 0,
              "w_ff1": 1, "w_ff2": 0}
SMALL = ("g_mix", "g_mem", "g_ffn", "g_final", "b_alpha", "b_forget", "g_gla_head")
PACK_W = 1024
PACK_ROWS_A = 2048
PACK_ROWS_B = 3104
PACK_ROWS = PACK_ROWS_A + PACK_ROWS_B


def _cp(*sem):
    return pltpu.CompilerParams(dimension_semantics=sem, vmem_limit_bytes=VMEM_LIMIT_BYTES)


def _dot(a, b, **kw):
    return jnp.dot(a, b, preferred_element_type=F32, **kw)


def _dot_nt(a, b, **kw):
    return lax.dot_general(a, b, (((1,), (1,)), ((), ())), preferred_element_type=F32, **kw)


def _dot_tn(a, b, **kw):
    return lax.dot_general(a, b, (((0,), (0,)), ((), ())), preferred_element_type=F32, **kw)


def _sigmoid(x):
    return 1.0 / (1.0 + jnp.exp(-x))


def _log_sigmoid(x):
    return -(jnp.maximum(-x, 0.0) + jnp.log1p(jnp.exp(-jnp.abs(x))))


def _fold8(x):
    m, n = x.shape
    return x.reshape(m // 8, 8, n).sum(axis=0)


def _iota(shape, dim):
    return lax.broadcasted_iota(jnp.int32, shape, dim)


def _row_tile(s):
    return min(s, 512)


class _Side:
    def __init__(self, inputs, out_shape, scratch, ops):
        self.inputs, self.out_shape, self.scratch, self.ops = list(inputs), list(out_shape), list(scratch), ops


ANY_SPEC = pl.BlockSpec(memory_space=pl.ANY)


def _mm_nn(a, b, *, out_dtype, tm, tn, tk, name, a_fn=None, epi=None, extra=None, side=None):
    m, k = a.shape
    _, n = b.shape
    nk = k // tk
    n_in = 2 + (extra is not None)
    n_sin = 0 if side is None else len(side.inputs)
    n_sout = 0 if side is None else len(side.out_shape)

    def body_one(*refs):
        a_ref, b_ref = refs[0], refs[1]
        o_ref = refs[n_in + n_sin]
        if side is not None:
            start, finish = side.ops(refs[n_in:n_in + n_sin], refs[n_in + n_sin + 1:n_in + n_sin + 1 + n_sout],
                                     refs[n_in + n_sin + 1 + n_sout:])
            pl.when((pl.program_id(0) == 0) & (pl.program_id(1) == 0))(start)
        at = a_ref[...] if a_fn is None else a_fn(a_ref[...])
        r = _dot(at, b_ref[...])
        if epi is not None:
            r = epi(r, None if extra is None else refs[2][...])
        o_ref[...] = r.astype(out_dtype)
        if side is not None:
            pl.when((pl.program_id(0) == m // tm - 1) & (pl.program_id(1) == n // tn - 1))(finish)

    if nk == 1:
        in_specs = [pl.BlockSpec((tm, k), lambda i, j: (i, 0)), pl.BlockSpec((k, tn), lambda i, j: (0, j))]
        args = [a, b]
        if extra is not None:
            in_specs.append(pl.BlockSpec((tm, tn), lambda i, j: (i, j)))
            args.append(extra)
        out_specs = pl.BlockSpec((tm, tn), lambda i, j: (i, j))
        out_shape = jax.ShapeDtypeStruct((m, n), out_dtype)
        if side is None:
            return pl.pallas_call(
                body_one, grid=(m // tm, n // tn), in_specs=in_specs, out_specs=out_specs, out_shape=out_shape,
                name=name, compiler_params=_cp("parallel", "parallel"))(*args)
        return pl.pallas_call(
            body_one, grid=(m // tm, n // tn), in_specs=in_specs + [ANY_SPEC] * n_sin,
            out_specs=[out_specs] + [ANY_SPEC] * n_sout, out_shape=[out_shape] + side.out_shape,
            scratch_shapes=side.scratch, name=name, compiler_params=_cp("arbitrary", "arbitrary"))(*args, *side.inputs)
    assert side is None

    def body(*refs):
        if extra is None:
            a_ref, b_ref, o_ref, acc = refs
            x_ref = None
        else:
            a_ref, b_ref, x_ref, o_ref, acc = refs
        kk = pl.program_id(2)

        @pl.when(kk == 0)
        def _():
            acc[...] = jnp.zeros_like(acc)

        at = a_ref[...]
        if a_fn is not None:
            at = a_fn(at)
        acc[...] += _dot(at, b_ref[...])

        @pl.when(kk == nk - 1)
        def _():
            r = acc[...]
            if epi is not None:
                r = epi(r, None if x_ref is None else x_ref[...])
            o_ref[...] = r.astype(out_dtype)

    in_specs = [pl.BlockSpec((tm, tk), lambda i, j, kk: (i, kk)), pl.BlockSpec((tk, tn), lambda i, j, kk: (kk, j))]
    args = [a, b]
    if extra is not None:
        in_specs.append(pl.BlockSpec((tm, tn), lambda i, j, kk: (i, j)))
        args.append(extra)
    return pl.pallas_call(
        body, grid=(m // tm, n // tn, nk), in_specs=in_specs,
        out_specs=pl.BlockSpec((tm, tn), lambda i, j, kk: (i, j)),
        out_shape=jax.ShapeDtypeStruct((m, n), out_dtype),
        scratch_shapes=[pltpu.VMEM((tm, tn), F32)], name=name,
        compiler_params=_cp("parallel", "parallel", "arbitrary"))(*args)


def _mm_tn(a, b, *, tm, tn, ts, name, a_fn=None):
    s, m = a.shape
    _, n = b.shape
    ns = s // ts

    def body(a_ref, b_ref, o_ref, acc):
        kk = pl.program_id(2)

        @pl.when(kk == 0)
        def _():
            acc[...] = jnp.zeros_like(acc)

        at = a_ref[...]
        if a_fn is not None:
            at = a_fn(at)
        acc[...] += _dot_tn(at, b_ref[...])

        @pl.when(kk == ns - 1)
        def _():
            o_ref[...] = acc[...]

    return pl.pallas_call(
        body, grid=(m // tm, n // tn, ns),
        in_specs=[pl.BlockSpec((ts, tm), lambda i, j, kk: (kk, i)), pl.BlockSpec((ts, tn), lambda i, j, kk: (kk, j))],
        out_specs=pl.BlockSpec((tm, tn), lambda i, j, kk: (i, j)),
        out_shape=jax.ShapeDtypeStruct((m, n), F32),
        scratch_shapes=[pltpu.VMEM((tm, tn), F32)], name=name,
        compiler_params=_cp("parallel", "parallel", "arbitrary"))(a, b)


def _relu2_bf16(t):
    r = jnp.maximum(t.astype(F32), 0.0)
    return (r * r).astype(BF16)


def _rms_fwd(x, g, name):
    s, d = x.shape
    tm = min(s, 512)

    def body(x_ref, g_ref, u_ref, r_ref):
        xv = x_ref[...]
        r = lax.rsqrt(jnp.mean(xv * xv, axis=-1, keepdims=True) + EPS)
        u_ref[...] = ((xv * r) * g_ref[...]).astype(BF16)
        r_ref[...] = r

    return pl.pallas_call(
        body, grid=(s // tm,),
        in_specs=[pl.BlockSpec((tm, d), lambda i: (i, 0)), pl.BlockSpec((1, d), lambda i: (0, 0))],
        out_specs=[pl.BlockSpec((tm, d), lambda i: (i, 0)), pl.BlockSpec((tm, 1), lambda i: (i, 0))],
        out_shape=[jax.ShapeDtypeStruct((s, d), BF16), jax.ShapeDtypeStruct((s, 1), F32)],
        name=name, compiler_params=_cp("parallel"))(x, g)


def _mm_norm_bwd(a, b, xin, r, g, dres, *, name, want_bf16, side=None):
    s, k = a.shape
    tm = min(s, 256)
    n_out = 3 if want_bf16 else 2
    n_sin = 0 if side is None else len(side.inputs)
    n_sout = 0 if side is None else len(side.out_shape)

    def body(a_ref, b_ref, x_ref, r_ref, g_ref, dres_ref, *rest):
        outs = rest[n_sin:n_sin + n_out]
        dx_ref, dg_ref = outs[0], outs[-1]
        if side is not None:
            start, finish = side.ops(rest[:n_sin], rest[n_sin + n_out:n_sin + n_out + n_sout], rest[n_sin + n_out + n_sout:])
            pl.when(pl.program_id(0) == 0)(start)

        @pl.when(pl.program_id(0) == 0)
        def _():
            dg_ref[...] = jnp.zeros_like(dg_ref)

        du = _dot(a_ref[...], b_ref[...])
        xn = x_ref[...] * r_ref[...]
        dg_ref[...] += _fold8(du * xn)
        dxn = du * g_ref[...]
        dx = dres_ref[...] + r_ref[...] * (dxn - xn * jnp.mean(dxn * xn, axis=-1, keepdims=True))
        dx_ref[...] = dx
        if want_bf16:
            outs[1][...] = dx.astype(BF16)
        if side is not None:
            pl.when(pl.program_id(0) == s // tm - 1)(finish)

    row = lambda i: (i, 0)
    const = lambda i: (0, 0)
    out_specs = [pl.BlockSpec((tm, D), row)]
    out_shape = [jax.ShapeDtypeStruct((s, D), F32)]
    if want_bf16:
        out_specs.append(pl.BlockSpec((tm, D), row))
        out_shape.append(jax.ShapeDtypeStruct((s, D), BF16))
    out_specs.append(pl.BlockSpec((8, D), const))
    out_shape.append(jax.ShapeDtypeStruct((8, D), F32))
    side_in = [] if side is None else side.inputs
    return pl.pallas_call(
        body, grid=(s // tm,),
        in_specs=[pl.BlockSpec((tm, k), row), pl.BlockSpec((k, D), const, pipeline_mode=pl.Buffered(1)),
                  pl.BlockSpec((tm, D), row), pl.BlockSpec((tm, 1), row), pl.BlockSpec((1, D), const),
                  pl.BlockSpec((tm, D), row)] + [ANY_SPEC] * n_sin,
        out_specs=out_specs + [ANY_SPEC] * n_sout, out_shape=out_shape + ([] if side is None else side.out_shape),
        scratch_shapes=[] if side is None else side.scratch,
        name=name, compiler_params=_cp("arbitrary"))(a, b, xin, r, g, dres, *side_in)


def _gla_consts():
    lmask = _iota((4 * CHUNK, CHUNK), 0) % CHUNK >= _iota((4 * CHUNK, CHUNK), 1)
    hmask = _iota((256, 256), 0) // CHUNK == _iota((256, 256), 1) // CHUNK
    bd = _iota((256, 512), 0) // CHUNK == _iota((256, 512), 1) // 128
    return lmask, hmask, bd


def _fold_heads(x):
    return x[0:64] + x[64:128] + x[128:192] + x[192:256]


def _gla_chunk(lac, qc, kc):
    tri = (_iota((CHUNK, CHUNK), 0) >= _iota((CHUNK, CHUNK), 1)).astype(F32)
    b = _dot(tri, lac, precision=HI)
    bl = b[CHUNK - 1:CHUNK, :]
    ep, en, ek = jnp.exp(b), jnp.exp(-b), jnp.exp(bl - b)
    decb = jnp.exp(_dot_tn(lac, jnp.ones((CHUNK, 128), F32), precision=HI))
    decb = jnp.concatenate([decb] * 4, axis=1)
    return bl, ep, en, ek, decb, qc * ep, qc * en, kc * en, kc * ep, kc * ek


def _gla_fwd(pm, pe, wau_p, b_alpha):
    s = pm.shape[0]
    t = _row_tile(s)
    nc = t // CHUNK

    def body(q_ref, k_ref, v_ref, e_ref, wau_ref, ba_ref, o_ref, st_ref, state, la_scr):
        @pl.when(pl.program_id(0) == 0)
        def _():
            state[...] = jnp.zeros_like(state)

        z = _dot(e_ref[...].astype(BF16), wau_ref[...]) + ba_ref[...]
        la_scr[...] = _log_sigmoid(z) * (1.0 / GLA_TAU)
        lmask, hmask, bd = _gla_consts()

        def chunk(c, carry):
            rows = pl.ds(pl.multiple_of(c * CHUNK, CHUNK), CHUNK)
            qc = q_ref[rows, :].astype(F32) * 0.125
            kc = k_ref[rows, :].astype(F32)
            vc = v_ref[rows, :]
            _, _, _, _, decb, qp, qn, kn, kp, kk = _gla_chunk(la_scr[rows, :], qc, kc)
            qs = jnp.where(hmask, jnp.concatenate([qp] * 4, axis=0), 0.0).astype(BF16)
            qns = jnp.where(hmask, jnp.concatenate([qn] * 4, axis=0), 0.0).astype(BF16)
            attn = jnp.where(lmask, _dot_nt(qs, kn.astype(BF16)), _dot_nt(qns, kp.astype(BF16))).astype(BF16)
            st = state[...]
            o_intra = _fold_heads(jnp.where(bd, _dot(attn, vc), 0.0))
            o_ref[rows, :] = o_intra + _dot(qp.astype(BF16), st.astype(BF16))
            for h in range(4):
                st_ref[c, :, 128 * h:128 * (h + 1)] = st[64 * h:64 * (h + 1), 128 * h:128 * (h + 1)]
            kv = jnp.where(bd, _dot_tn(kk.astype(BF16), vc), 0.0)
            state[...] = st * decb + kv
            return carry

        lax.fori_loop(0, nc, chunk, 0)

    return pl.pallas_call(
        body, grid=(s // t,),
        in_specs=[pl.BlockSpec((t, 256), lambda i: (i, C_GQ // 256)), pl.BlockSpec((t, 256), lambda i: (i, C_GK // 256)),
                  pl.BlockSpec((t, 512), lambda i: (i, C_GV // 512)), pl.BlockSpec((t, PE_W), lambda i: (i, 0)),
                  pl.BlockSpec((PE_W, 256), lambda i: (0, 0)), pl.BlockSpec((1, 256), lambda i: (0, 0))],
        out_specs=[pl.BlockSpec((t, 512), lambda i: (i, 0)), pl.BlockSpec((nc, CHUNK, 512), lambda i: (i, 0, 0))],
        out_shape=[jax.ShapeDtypeStruct((s, 512), F32), jax.ShapeDtypeStruct((s // CHUNK, CHUNK, 512), F32)],
        scratch_shapes=[pltpu.VMEM((256, 512), F32), pltpu.VMEM((t, 256), F32)],
        name="gla_fwd", compiler_params=_cp("arbitrary"))(pm, pm, pm, pe, wau_p, b_alpha)


def _gla_bwd(pm, pe, wau_p, wau_pt, b_alpha, do, states):
    s = pm.shape[0]
    t = _row_tile(s)
    nc = t // CHUNK
    nb = s // t

    def body(q_ref, k_ref, v_ref, e_ref, wau_ref, waut_ref, ba_ref, do_ref, st_ref,
             dq_ref, dk_ref, dv_ref, de_ref, dwau_ref, dba_ref, gstate, la_scr, dla_scr):
        @pl.when(pl.program_id(0) == 0)
        def _():
            gstate[...] = jnp.zeros_like(gstate)
            dwau_ref[...] = jnp.zeros_like(dwau_ref)
            dba_ref[...] = jnp.zeros_like(dba_ref)

        eb = e_ref[...].astype(BF16)
        z = _dot(eb, wau_ref[...]) + ba_ref[...]
        la_scr[...] = _log_sigmoid(z) * (1.0 / GLA_TAU)
        lmask, hmask, bd = _gla_consts()
        triu = (_iota((CHUNK, CHUNK), 0) <= _iota((CHUNK, CHUNK), 1)).astype(F32)
        last_row = _iota((CHUNK, 256), 0) == CHUNK - 1

        def chunk(cc, carry):
            c = nc - 1 - cc
            rows = pl.ds(pl.multiple_of(c * CHUNK, CHUNK), CHUNK)
            qc = q_ref[rows, :].astype(F32) * 0.125
            kc = k_ref[rows, :].astype(F32)
            vc = v_ref[rows, :]
            dob = do_ref[rows, :]
            bl, ep, en, ek, decb, qp, qn, kn, kp, kk = _gla_chunk(la_scr[rows, :], qc, kc)
            qs = jnp.where(hmask, jnp.concatenate([qp] * 4, axis=0), 0.0).astype(BF16)
            qns = jnp.where(hmask, jnp.concatenate([qn] * 4, axis=0), 0.0).astype(BF16)
            knb, kpb = kn.astype(BF16), kp.astype(BF16)
            attn = jnp.where(lmask, _dot_nt(qs, knb), _dot_nt(qns, kpb)).astype(BF16)
            st = jnp.where(bd, jnp.concatenate([st_ref[c]] * 4, axis=0), 0.0)
            g = gstate[...]
            gb = g.astype(BF16)
            do_s = jnp.where(bd, jnp.concatenate([dob] * 4, axis=0), jnp.zeros((), BF16))
            dattn = _dot_nt(do_s, vc)
            dv_ref[rows, :] = (_dot_tn(attn, do_s) + _dot(kk.astype(BF16), gb)).astype(BF16)
            dac = jnp.where(lmask, dattn, 0.0).astype(BF16)
            daa = jnp.where(lmask, 0.0, dattn).astype(BF16)
            dqp = _fold_heads(jnp.where(hmask, _dot(dac, knb), 0.0)) + _dot_nt(dob, st.astype(BF16))
            dqn = _fold_heads(jnp.where(hmask, _dot(daa, kpb), 0.0))
            dkn = _dot_tn(dac, qs)
            dkp = _dot_tn(daa, qns)
            dkk = _dot_nt(vc, gb)
            ddec = _dot_nt(jnp.ones((8, 512), F32), g * st, precision=HI)[0:1, :]
            gstate[...] = decb * g + jnp.where(bd, _dot_tn(qp.astype(BF16), dob), 0.0)
            dq_ref[rows, :] = ((dqp * ep + dqn * en) * 0.125).astype(BF16)
            dk_ref[rows, :] = (dkn * en + dkp * ep + dkk * ek).astype(BF16)
            dek = dkk * kc * ek
            db = (dqp * qc + dkp * kc) * ep - (dqn * qc + dkn * kc) * en - dek
            dbl = jnp.sum(dek, axis=0, keepdims=True) + ddec * jnp.exp(bl)
            db = db + jnp.where(last_row, dbl, 0.0)
            dla_scr[rows, :] = _dot(triu, db, precision=HI)
            return carry

        lax.fori_loop(0, nc, chunk, 0)
        dz = dla_scr[...] * (1.0 / GLA_TAU) * _sigmoid(-z)
        dzb = dz.astype(BF16)
        dwau_ref[...] += _dot_tn(eb, dzb)
        dba_ref[...] += _fold8(dz)
        de_ref[...] = _dot(dzb, waut_ref[...])

    rev = lambda i: nb - 1 - i
    return pl.pallas_call(
        body, grid=(nb,),
        in_specs=[pl.BlockSpec((t, 256), lambda i: (rev(i), C_GQ // 256)), pl.BlockSpec((t, 256), lambda i: (rev(i), C_GK // 256)),
                  pl.BlockSpec((t, 512), lambda i: (rev(i), C_GV // 512)), pl.BlockSpec((t, PE_W), lambda i: (rev(i), 0)),
                  pl.BlockSpec((PE_W, 256), lambda i: (0, 0)), pl.BlockSpec((256, PE_W), lambda i: (0, 0)),
                  pl.BlockSpec((1, 256), lambda i: (0, 0)), pl.BlockSpec((t, 512), lambda i: (rev(i), 0)),
                  pl.BlockSpec((nc, CHUNK, 512), lambda i: (rev(i), 0, 0))],
        out_specs=[pl.BlockSpec((t, 256), lambda i: (rev(i), 0)), pl.BlockSpec((t, 256), lambda i: (rev(i), 0)),
                   pl.BlockSpec((t, 512), lambda i: (rev(i), 0)), pl.BlockSpec((t, PE_W), lambda i: (rev(i), 0)),
                   pl.BlockSpec((PE_W, 256), lambda i: (0, 0)), pl.BlockSpec((8, 256), lambda i: (0, 0))],
        out_shape=[jax.ShapeDtypeStruct((s, 256), BF16), jax.ShapeDtypeStruct((s, 256), BF16),
                   jax.ShapeDtypeStruct((s, 512), BF16), jax.ShapeDtypeStruct((s, PE_W), F32),
                   jax.ShapeDtypeStruct((PE_W, 256), F32), jax.ShapeDtypeStruct((8, 256), F32)],
        scratch_shapes=[pltpu.VMEM((256, 512), F32), pltpu.VMEM((t, 256), F32), pltpu.VMEM((t, 256), F32)],
        name="gla_bwd", compiler_params=_cp("arbitrary"))(pm, pm, pm, pe, wau_p, wau_pt, b_alpha, do, states)


def _fcum_fwd(pe, bias):
    s = pe.shape[0]
    t = min(s, 256)

    def body(e_ref, b_ref, f_ref, carry):
        @pl.when(pl.program_id(0) == 0)
        def _():
            carry[...] = jnp.zeros_like(carry)

        lf = _log_sigmoid(e_ref[...] + b_ref[...])
        tri = (_iota((t, t), 0) >= _iota((t, t), 1)).astype(F32)
        f = _dot(tri, lf, precision=HI) + carry[0:1, :]
        f_ref[...] = f
        carry[...] = jnp.broadcast_to(f[t - 1:t, :], carry.shape)

    return pl.pallas_call(
        body, grid=(s // t,),
        in_specs=[pl.BlockSpec((t, PE_W), lambda i: (i, 0)), pl.BlockSpec((1, PE_W), lambda i: (0, 0))],
        out_specs=pl.BlockSpec((t, PE_W), lambda i: (i, 0)),
        out_shape=jax.ShapeDtypeStruct((s, PE_W), F32), scratch_shapes=[pltpu.VMEM((8, PE_W), F32)],
        name="fcum_fwd", compiler_params=_cp("arbitrary"))(pe, bias)


def _fcum_bwd(pe, bias, df):
    s = pe.shape[0]
    t = min(s, 256)
    nb = s // t

    def body(e_ref, b_ref, df_ref, de_ref, db_ref, carry):
        @pl.when(pl.program_id(0) == 0)
        def _():
            carry[...] = jnp.zeros_like(carry)
            db_ref[...] = jnp.zeros_like(db_ref)

        triu = (_iota((t, t), 0) <= _iota((t, t), 1)).astype(F32)
        dlf = _dot(triu, df_ref[...], precision=HI) + carry[0:1, :]
        carry[...] = jnp.broadcast_to(dlf[0:1, :], carry.shape)
        lane = _iota((t, PE_W), 1)
        dff = jnp.where((lane >= FF_LANE) & (lane < FF_LANE + 8), dlf * _sigmoid(-(e_ref[...] + b_ref[...])), 0.0)
        de_ref[...] = dff
        db_ref[...] += _fold8(dff)

    rev = lambda i: (nb - 1 - i, 0)
    return pl.pallas_call(
        body, grid=(nb,),
        in_specs=[pl.BlockSpec((t, PE_W), rev), pl.BlockSpec((1, PE_W), lambda i: (0, 0)), pl.BlockSpec((t, PE_W), rev)],
        out_specs=[pl.BlockSpec((t, PE_W), rev), pl.BlockSpec((8, PE_W), lambda i: (0, 0))],
        out_shape=[jax.ShapeDtypeStruct((s, PE_W), F32), jax.ShapeDtypeStruct((8, PE_W), F32)],
        scratch_shapes=[pltpu.VMEM((8, PE_W), F32)],
        name="fcum_bwd", compiler_params=_cp("arbitrary"))(pe, bias, df)


FOX_WIDE = 1024


def _split3(x):
    hi = x.astype(BF16)
    r = x - hi.astype(F32)
    mid = r.astype(BF16)
    lo = (r - mid.astype(F32)).astype(BF16)
    return jnp.concatenate([hi, mid, lo], axis=1)


def _fox_tables():
    heads, lane = np.arange(8), np.arange(64)
    spread = np.zeros((512, 1024), np.float32)
    spread[(64 * heads[:, None] + lane).ravel(), (128 * heads[:, None] + lane).ravel()] = 1.0
    def place(src_lane0, dst_off, val):
        t = np.zeros((384, 1024), np.float32)
        for p in range(3):
            t[128 * p + src_lane0 + heads, 128 * heads + dst_off + p] = val
        return t
    def const(off, val):
        c = np.zeros((1, 1024), np.float32)
        for p in range(3):
            c[0, 128 * heads + off + p] = val
        return c
    rows = np.zeros((8, 128), np.float32)
    rows[heads, FF_LANE + heads] = 1.0
    bf = lambda a: jnp.asarray(a, BF16)
    return dict(spread=bf(spread),
                f_to_q=bf(place(FF_LANE, 64, 1.0)), f_to_k=bf(place(FF_LANE, 67, -1.0)), d_to_do=bf(place(0, 64, 1.0)),
                ones_q=jnp.asarray(const(67, 1.0)), ones_k=jnp.asarray(const(64, 1.0)), ones_v=jnp.asarray(const(64, -1.0)),
                rows=jnp.asarray(rows))


LOG2E = 1.4426950408889634


def _fox_prep(pm, f128, lse8, tb, *, backward):
    s = pm.shape[0]
    tm = _row_tile(s)

    def body(*refs):
        if backward:
            q_ref, f_ref, lse_ref, sp_ref, fq_ref, cq_ref, rows_ref, qa_ref = refs
            f = f_ref[...] * LOG2E - _dot_tn(lse_ref[...], rows_ref[...], precision=HI)
            q2 = (q_ref[...].astype(F32) * (0.125 * LOG2E)).astype(BF16)
            qa_ref[...] = (_dot(q2, sp_ref[...]) + _dot(_split3(f), fq_ref[...]) + cq_ref[...]).astype(BF16)
            return
        (q_ref, k_ref, v_ref, f_ref, sp_ref, fq_ref, fk_ref, cq_ref, ck_ref, cv_ref,
         qa_ref, ka_ref, va_ref, vt_ref, qt_ref, kt_ref) = refs
        f3 = _split3(f_ref[...] * LOG2E)
        q, k, v = q_ref[...].astype(F32), k_ref[...], v_ref[...]
        sp = sp_ref[...]
        qa_ref[...] = (_dot((q * (0.125 * LOG2E)).astype(BF16), sp) + _dot(f3, fq_ref[...]) + cq_ref[...]).astype(BF16)
        ka_ref[...] = (_dot(k, sp) + _dot(f3, fk_ref[...]) + ck_ref[...]).astype(BF16)
        va_ref[...] = (_dot(v, sp) + cv_ref[...]).astype(BF16)
        vt_ref[...] = v.T
        qt_ref[...] = (q * 0.125).astype(BF16).T
        kt_ref[...] = (k.astype(F32) * 0.125).astype(BF16).T

    row = lambda i: (i, 0)
    const = lambda i: (0, 0)
    blk = lambda c: pl.BlockSpec((tm, 512), lambda i: (i, c // 512))
    wide = pl.BlockSpec((tm, 1024), row)
    mat = lambda a: pl.BlockSpec(a.shape, const)
    if backward:
        ins = [pm, f128, lse8, tb["spread"], tb["f_to_q"], tb["ones_q"], tb["rows"]]
        in_specs = [blk(C_FQ), pl.BlockSpec((tm, 128), row), pl.BlockSpec((8, tm), lambda i: (0, i))] + [mat(a) for a in ins[3:]]
        out_specs, out_shape = wide, jax.ShapeDtypeStruct((s, 1024), BF16)
    else:
        ins = [pm, pm, pm, f128, tb["spread"], tb["f_to_q"], tb["f_to_k"], tb["ones_q"], tb["ones_k"], tb["ones_v"]]
        in_specs = [blk(C_FQ), blk(C_FK), blk(C_FV), pl.BlockSpec((tm, 128), row)] + [mat(a) for a in ins[4:]]
        tr = pl.BlockSpec((512, tm), lambda i: (0, i))
        out_specs = [wide, wide, wide, tr, tr, tr]
        out_shape = [jax.ShapeDtypeStruct((s, 1024), BF16)] * 3 + [jax.ShapeDtypeStruct((512, s), BF16)] * 3
    return pl.pallas_call(body, grid=(s // tm,), in_specs=in_specs, out_specs=out_specs, out_shape=out_shape,
                          name="fox_prep_bwd" if backward else "fox_prep", compiler_params=_cp("parallel"))(*ins)


def _fox_post(dqt, dkt, dvt, rowsum8, colsum8, tb):
    s = dqt.shape[1]
    tm = _row_tile(s)

    def body(dqt_ref, dkt_ref, dvt_ref, rs_ref, cs_ref, rows_ref, dfq_ref, dfk_ref, dfv_ref, df_ref):
        dfq_ref[...] = dqt_ref[...].T.astype(BF16)
        dfk_ref[...] = dkt_ref[...].T
        dfv_ref[...] = dvt_ref[...].T
        df_ref[...] = _dot_tn(rs_ref[...] - cs_ref[...], rows_ref[...], precision=HI)

    row = lambda i: (i, 0)
    tr = pl.BlockSpec((512, tm), lambda i: (0, i))
    out = pl.BlockSpec((tm, 512), row)
    heads = pl.BlockSpec((8, tm), lambda i: (0, i))
    return pl.pallas_call(
        body, grid=(s // tm,),
        in_specs=[tr, tr, tr, heads, heads, pl.BlockSpec((8, 128), lambda i: (0, 0))],
        out_specs=[out, out, out, pl.BlockSpec((tm, 128), row)],
        out_shape=[jax.ShapeDtypeStruct((s, 512), BF16)] * 3 + [jax.ShapeDtypeStruct((s, 128), F32)],
        name="fox_post", compiler_params=_cp("parallel"))(dqt, dkt, dvt, rowsum8, colsum8, tb["rows"])


def _fox_fwd(k_aug, q_aug, vt):
    s = k_aug.shape[0]
    nh = 8
    tk = _row_tile(s)
    tq = min(s, FOX_WIDE)
    per = tq // tk

    def body(k_ref, q_ref, v_ref, o_ref, lse_ref, sbuf):
        i = pl.program_id(1)
        qa = q_ref[...]

        def scores(j):
            return _dot_nt(k_ref[pl.ds(pl.multiple_of(j * tk, tk), tk), :], qa)

        def update(st, j, carry):
            m, l, acc = carry
            m2 = jnp.maximum(m, jnp.max(st, axis=0, keepdims=True))
            p = jnp.exp2(st - m2)
            a = jnp.exp2(m - m2)
            vj = v_ref[:, pl.ds(pl.multiple_of(j * tk, tk), tk)]
            return m2, a * l + jnp.sum(p, axis=0, keepdims=True), a * acc + _dot(vj, p.astype(BF16))

        def step(a, carry):
            sbuf[1] = scores(2 * a + 1)
            carry = update(sbuf[0], 2 * a, carry)
            sbuf[0] = scores(2 * a + 2)
            return update(sbuf[1], 2 * a + 1, carry)

        n = i * per
        sbuf[0] = scores(0)
        carry = (jnp.full((1, tq), -1e30, F32), jnp.zeros((1, tq), F32), jnp.zeros((64, tq), F32))
        carry = lax.fori_loop(0, n // 2, step, carry)
        tri = _iota((tk, tk), 0) <= _iota((tk, tk), 1)
        late = [_dot_nt(k_ref[pl.ds(pl.multiple_of((n + r) * tk, tk), tk), :], qa[r * tk:, :]) for r in range(1, per)]
        for r in range(per):
            st = sbuf[0] if r == 0 else late[r - 1]
            head = jnp.where(tri, st[:, :tk], -1e30)
            st = head if st.shape[1] == tk else jnp.concatenate([head, st[:, tk:]], axis=1)
            part = update(st, n + r, tuple(c[:, r * tk:] for c in carry))
            carry = part if r == 0 else tuple(jnp.concatenate([old[:, :r * tk], new], axis=1) for old, new in zip(carry, part))
        m, l, acc = carry
        o_ref[...] = (acc / l).astype(BF16)
        lse_ref[0] = m + jnp.log2(l)

    return pl.pallas_call(
        body, grid=(nh, s // tq),
        in_specs=[pl.BlockSpec((s, 128), lambda h, i: (0, h)), pl.BlockSpec((tq, 128), lambda h, i: (i, h)),
                  pl.BlockSpec((64, s), lambda h, i: (h, 0))],
        out_specs=[pl.BlockSpec((64, tq), lambda h, i: (h, i)), pl.BlockSpec((1, 1, tq), lambda h, i: (h, 0, i))],
        out_shape=[jax.ShapeDtypeStruct((512, s), BF16), jax.ShapeDtypeStruct((nh, 1, s), F32)],
        scratch_shapes=[pltpu.VMEM((2, tk, tq), F32)],
        name="fox_fwd", compiler_params=_cp("parallel", "arbitrary"))(k_aug, q_aug, vt)


def _fox_bwd(q_aug, do_aug, qt, dot_, k_aug, v_aug, kt):
    s = q_aug.shape[0]
    nh = 8
    tq = _row_tile(s)
    tk = min(s, FOX_WIDE)
    per = tk // tq
    nqb = s // tq

    def body(qa_ref, da_ref, qt_ref, dt_ref, ka_ref, va_ref, kt_ref, dq_ref, rs_ref, dk_ref, dv_ref, dfk_ref):
        j = pl.program_id(1)

        @pl.when(j == 0)
        def _():
            dq_ref[...] = jnp.zeros_like(dq_ref)
            rs_ref[...] = jnp.zeros_like(rs_ref)

        ka, va, ks = ka_ref[...], va_ref[...], kt_ref[...]

        tri = _iota((tq, tq), 0) >= _iota((tq, tq), 1)

        def tile(i, w, carry):
            masked = w is not None
            w = tk if w is None else w
            rows = pl.ds(pl.multiple_of(i * tq, tq), tq)
            sp = _dot_nt(qa_ref[rows, :], ka[:w])
            if masked:
                last = jnp.where(tri, sp[:, w - tq:], -1e30)
                sp = last if w == tq else jnp.concatenate([sp[:, :w - tq], last], axis=1)
            p = jnp.exp2(sp)
            ds = p * _dot_nt(da_ref[rows, :], va[:w])
            dsb = ds.astype(BF16)
            dq_ref[:, rows] += _dot_nt(ks[:, :w], dsb)
            rs_ref[0, :, rows] += jnp.broadcast_to(jnp.sum(ds, axis=1, keepdims=True), (tq, 128)).T[0:8]
            new = (_dot(qt_ref[:, rows], dsb), _dot(dt_ref[:, rows], p.astype(BF16)), jnp.sum(ds, axis=0, keepdims=True))
            if w == tk:
                return tuple(c + d for c, d in zip(carry, new))
            return tuple(jnp.concatenate([c[:, :w] + d, c[:, w:]], axis=1) for c, d in zip(carry, new))

        carry = (jnp.zeros((64, tk), F32), jnp.zeros((64, tk), F32), jnp.zeros((1, tk), F32))
        for r in range(per):
            carry = tile(j * per + r, (r + 1) * tq, carry)
        dk, dv, dfk = lax.fori_loop((j + 1) * per, nqb, lambda i, c: tile(i, None, c), carry)
        dk_ref[...] = dk.astype(BF16)
        dv_ref[...] = dv.astype(BF16)
        dfk_ref[0] = dfk

    head_cols = lambda h, j: (0, h)
    head_rows = lambda h, j: (h, 0)
    once = dict(pipeline_mode=pl.Buffered(1))
    return pl.pallas_call(
        body, grid=(nh, s // tk),
        in_specs=[pl.BlockSpec((s, 128), head_cols, **once), pl.BlockSpec((s, 128), head_cols, **once),
                  pl.BlockSpec((64, s), head_rows, **once), pl.BlockSpec((64, s), head_rows, **once),
                  pl.BlockSpec((tk, 128), lambda h, j: (j, h)), pl.BlockSpec((tk, 128), lambda h, j: (j, h)),
                  pl.BlockSpec((64, tk), lambda h, j: (h, j))],
        out_specs=[pl.BlockSpec((64, s), head_rows), pl.BlockSpec((1, 8, s), lambda h, j: (h, 0, 0)),
                   pl.BlockSpec((64, tk), lambda h, j: (h, j)),
                   pl.BlockSpec((64, tk), lambda h, j: (h, j)), pl.BlockSpec((1, 1, tk), lambda h, j: (h, 0, j))],
        out_shape=[jax.ShapeDtypeStruct((512, s), F32), jax.ShapeDtypeStruct((nh, 8, s), F32),
                   jax.ShapeDtypeStruct((512, s), BF16),
                   jax.ShapeDtypeStruct((512, s), BF16), jax.ShapeDtypeStruct((nh, 1, s), F32)],
        name="fox_bwd", compiler_params=_cp("parallel", "arbitrary"))(q_aug, do_aug, qt, dot_, k_aug, v_aug, kt)


MEM_SCALE = 128 ** -0.5


def _mem_attn_fwd(pm, mkv):
    s = pm.shape[0]
    t = _row_tile(s)
    nm = mkv.shape[0]

    def body(q_ref, mk_ref, mv_ref, o_ref):
        for h in range(4):
            cols = slice(128 * h, 128 * (h + 1))
            sc = _dot_nt(q_ref[:, cols], mk_ref[:, cols]) * MEM_SCALE
            p = jnp.exp(sc - jnp.max(sc, axis=-1, keepdims=True))
            p = p / jnp.sum(p, axis=-1, keepdims=True)
            o_ref[:, cols] = _dot(p.astype(BF16), mv_ref[:, cols]).astype(BF16)

    return pl.pallas_call(
        body, grid=(s // t,),
        in_specs=[pl.BlockSpec((t, 512), lambda i: (i, C_MQ // 512)), pl.BlockSpec((nm, 512), lambda i: (0, 0)),
                  pl.BlockSpec((nm, 512), lambda i: (0, 1))],
        out_specs=pl.BlockSpec((t, 512), lambda i: (i, 0)),
        out_shape=jax.ShapeDtypeStruct((s, 512), BF16),
        name="mem_attn_fwd", compiler_params=_cp("parallel"))(pm, mkv, mkv)


def _mem_attn_bwd(pm, mkv, do):
    s = pm.shape[0]
    t = _row_tile(s)
    nm = mkv.shape[0]

    def body(q_ref, mk_ref, mv_ref, do_ref, dq_ref, dmk_ref, dmv_ref):
        @pl.when(pl.program_id(0) == 0)
        def _():
            dmk_ref[...] = jnp.zeros_like(dmk_ref)
            dmv_ref[...] = jnp.zeros_like(dmv_ref)

        for h in range(4):
            cols = slice(128 * h, 128 * (h + 1))
            qh, kh, vh, doh = q_ref[:, cols], mk_ref[:, cols], mv_ref[:, cols], do_ref[:, cols]
            sc = _dot_nt(qh, kh) * MEM_SCALE
            p = jnp.exp(sc - jnp.max(sc, axis=-1, keepdims=True))
            p = p / jnp.sum(p, axis=-1, keepdims=True)
            pb = p.astype(BF16)
            dp = _dot_nt(doh, vh)
            ds = (p * (dp - jnp.sum(p * dp, axis=-1, keepdims=True)) * MEM_SCALE).astype(BF16)
            dq_ref[:, cols] = _dot(ds, kh).astype(BF16)
            dmk_ref[:, cols] += _dot_tn(ds, qh)
            dmv_ref[:, cols] += _dot_tn(pb, doh)

    return pl.pallas_call(
        body, grid=(s // t,),
        in_specs=[pl.BlockSpec((t, 512), lambda i: (i, C_MQ // 512)), pl.BlockSpec((nm, 512), lambda i: (0, 0)),
                  pl.BlockSpec((nm, 512), lambda i: (0, 1)), pl.BlockSpec((t, 512), lambda i: (i, 0))],
        out_specs=[pl.BlockSpec((t, 512), lambda i: (i, 0)), pl.BlockSpec((nm, 512), lambda i: (0, 0)),
                   pl.BlockSpec((nm, 512), lambda i: (0, 0))],
        out_shape=[jax.ShapeDtypeStruct((s, 512), BF16), jax.ShapeDtypeStruct((nm, 512), F32),
                   jax.ShapeDtypeStruct((nm, 512), F32)],
        name="mem_attn_bwd", compiler_params=_cp("arbitrary"))(pm, mkv, mkv, do)


def _gain_grad(dxn_g, x, r, name):
    m, d = x.shape

    def body(d_ref, x_ref, r_ref, o_ref):
        o_ref[...] = _fold8(d_ref[...] * (x_ref[...] * r_ref[...]))

    return pl.pallas_call(body, out_shape=jax.ShapeDtypeStruct((8, d), F32), name=name,
                          compiler_params=pltpu.CompilerParams(vmem_limit_bytes=VMEM_LIMIT_BYTES))(dxn_g, x, r)


def _head_norm(o, gh):
    xs, rs = [], []
    for h in range(4):
        oh = o[:, 128 * h:128 * (h + 1)]
        r = lax.rsqrt(jnp.mean(oh * oh, axis=-1, keepdims=True) + EPS)
        xs.append(oh * r)
        rs.append(r)
    return xs, rs


def _merge_fwd(x, pm, o_gla, o_fox_t, o_mem, g_head, wg, wf, wm, wo, g_ffn):
    s = x.shape[0]
    t = min(s, 256)

    def body(x_ref, g0_ref, g1_ref, g2_ref, gg_ref, og_ref, of_ref, om_ref, gh_ref, wg_ref, wf_ref, wm_ref, wo_ref, gf_ref,
             mg_ref, h1_ref, u2_ref, r2_ref):
        xs, _ = _head_norm(og_ref[...], None)
        gg = gg_ref[...].astype(F32)
        sil = gg * _sigmoid(gg)
        ogn = jnp.concatenate(xs, axis=1) * gh_ref[...] * sil
        merged = (_sigmoid(g0_ref[...].astype(F32)) * _dot(ogn.astype(BF16), wg_ref[...])
                  + _sigmoid(g1_ref[...].astype(F32)) * _dot(of_ref[...].T, wf_ref[...])
                  + _sigmoid(g2_ref[...].astype(F32)) * _dot(om_ref[...], wm_ref[...]))
        mb = merged.astype(BF16)
        mg_ref[...] = mb
        h1 = x_ref[...] + _dot(mb, wo_ref[...])
        h1_ref[...] = h1
        r = lax.rsqrt(jnp.mean(h1 * h1, axis=-1, keepdims=True) + EPS)
        u2_ref[...] = ((h1 * r) * gf_ref[...]).astype(BF16)
        r2_ref[...] = r

    row = lambda i: (i, 0)
    const = lambda i: (0, 0)
    return pl.pallas_call(
        body, grid=(s // t,),
        in_specs=[pl.BlockSpec((t, D), row), pl.BlockSpec((t, D), lambda i: (i, 0)), pl.BlockSpec((t, D), lambda i: (i, 1)),
                  pl.BlockSpec((t, D), lambda i: (i, 2)), pl.BlockSpec((t, 512), lambda i: (i, C_GG // 512)),
                  pl.BlockSpec((t, 512), row), pl.BlockSpec((512, t), lambda i: (0, i)), pl.BlockSpec((t, 512), row),
                  pl.BlockSpec((1, 512), const), pl.BlockSpec((512, D), const), pl.BlockSpec((512, D), const),
                  pl.BlockSpec((512, D), const), pl.BlockSpec((D, D), const), pl.BlockSpec((1, D), const)],
        out_specs=[pl.BlockSpec((t, D), row), pl.BlockSpec((t, D), row), pl.BlockSpec((t, D), row), pl.BlockSpec((t, 1), row)],
        out_shape=[jax.ShapeDtypeStruct((s, D), BF16), jax.ShapeDtypeStruct((s, D), F32),
                   jax.ShapeDtypeStruct((s, D), BF16), jax.ShapeDtypeStruct((s, 1), F32)],
        name="merge_fwd", compiler_params=_cp("parallel"))(x, pm, pm, pm, pm, o_gla, o_fox_t, o_mem, g_head, wg, wf, wm, wo, g_ffn)


def _merge_bwd(dh1b, pm, o_gla, o_fox_t, o_mem, g_head, wg, wf, wm, wgt, wft, wmt, wot, spread, d_to_do):
    s = dh1b.shape[0]
    t = min(s, 256)

    def body(dh_ref, g0_ref, g1_ref, g2_ref, gg_ref, og_ref, of_ref, om_ref, gh_ref, wg_ref, wf_ref, wm_ref,
             wgt_ref, wft_ref, wmt_ref, wot_ref, sp_ref, dd_ref,
             dgt_ref, dgg_ref, dog_ref, da_ref, dot_ref, dom_ref, dwg_ref, dwf_ref, dwm_ref, dgh_ref):
        @pl.when(pl.program_id(0) == 0)
        def _():
            dwg_ref[...] = jnp.zeros_like(dwg_ref)
            dwf_ref[...] = jnp.zeros_like(dwf_ref)
            dwm_ref[...] = jnp.zeros_like(dwm_ref)
            dgh_ref[...] = jnp.zeros_like(dgh_ref)

        dmerged = _dot(dh_ref[...], wot_ref[...])
        og = og_ref[...]
        xs, rs = _head_norm(og, None)
        on = jnp.concatenate(xs, axis=1)
        gg = gg_ref[...].astype(F32)
        sg = _sigmoid(gg)
        sil = gg * sg
        gh = gh_ref[...]
        ognb = (on * gh * sil).astype(BF16)
        ofb, omb = of_ref[...].T, om_ref[...]
        douts = []
        for idx, (gref, ob, w_ref, wt_ref, dw_ref) in enumerate((
                (g0_ref, ognb, wg_ref, wgt_ref, dwg_ref), (g1_ref, ofb, wf_ref, wft_ref, dwf_ref),
                (g2_ref, omb, wm_ref, wmt_ref, dwm_ref))):
            gt = _sigmoid(gref[...].astype(F32))
            y = _dot(ob, w_ref[...])
            dgt_ref[:, D * idx:D * (idx + 1)] = (dmerged * y * gt * (1.0 - gt)).astype(BF16)
            dy = (gt * dmerged).astype(BF16)
            dw_ref[...] += _dot_tn(ob, dy)
            douts.append(_dot(dy, wt_ref[...]))
        dogn, dof, dom = douts
        dofb = dof.astype(BF16)
        dom_ref[...] = dom.astype(BF16)
        ind = (_iota((512, 128), 0) // 64 == _iota((512, 128), 1)).astype(F32)
        delta = _dot(dofb.astype(F32) * ofb.astype(F32), ind, precision=HI)
        da_ref[...] = (_dot(dofb, sp_ref[...]) + _dot(_split3(delta), dd_ref[...])).astype(BF16)
        dot_ref[...] = dofb.T
        dgg_ref[...] = (dogn * on * gh * (sg * (1.0 + gg * (1.0 - sg)))).astype(BF16)
        d_on = dogn * sil
        dgh_ref[...] += _fold8(d_on * on)
        dxn = d_on * gh
        outs = []
        for h in range(4):
            cols = slice(128 * h, 128 * (h + 1))
            dh_, xh = dxn[:, cols], xs[h]
            outs.append(rs[h] * (dh_ - xh * jnp.mean(dh_ * xh, axis=-1, keepdims=True)))
        dog_ref[...] = jnp.concatenate(outs, axis=1).astype(BF16)

    row = lambda i: (i, 0)
    const = lambda i: (0, 0)
    return pl.pallas_call(
        body, grid=(s // t,),
        in_specs=[pl.BlockSpec((t, D), row), pl.BlockSpec((t, D), lambda i: (i, 0)), pl.BlockSpec((t, D), lambda i: (i, 1)),
                  pl.BlockSpec((t, D), lambda i: (i, 2)), pl.BlockSpec((t, 512), lambda i: (i, C_GG // 512)),
                  pl.BlockSpec((t, 512), row), pl.BlockSpec((512, t), lambda i: (0, i)), pl.BlockSpec((t, 512), row),
                  pl.BlockSpec((1, 512), const), pl.BlockSpec((512, D), const), pl.BlockSpec((512, D), const),
                  pl.BlockSpec((512, D), const), pl.BlockSpec((D, 512), const), pl.BlockSpec((D, 512), const),
                  pl.BlockSpec((D, 512), const), pl.BlockSpec((D, D), const),
                  pl.BlockSpec((512, 1024), const), pl.BlockSpec((384, 1024), const)],
        out_specs=[pl.BlockSpec((t, 3 * D), row), pl.BlockSpec((t, 512), row), pl.BlockSpec((t, 512), row),
                   pl.BlockSpec((t, 1024), row), pl.BlockSpec((512, t), lambda i: (0, i)), pl.BlockSpec((t, 512), row),
                   pl.BlockSpec((512, D), const), pl.BlockSpec((512, D), const), pl.BlockSpec((512, D), const),
                   pl.BlockSpec((8, 512), const)],
        out_shape=[jax.ShapeDtypeStruct((s, 3 * D), BF16), jax.ShapeDtypeStruct((s, 512), BF16),
                   jax.ShapeDtypeStruct((s, 512), BF16), jax.ShapeDtypeStruct((s, 1024), BF16),
                   jax.ShapeDtypeStruct((512, s), BF16), jax.ShapeDtypeStruct((s, 512), BF16),
                   jax.ShapeDtypeStruct((512, D), F32), jax.ShapeDtypeStruct((512, D), F32),
                   jax.ShapeDtypeStruct((512, D), F32), jax.ShapeDtypeStruct((8, 512), F32)],
        name="merge_bwd", compiler_params=_cp("arbitrary"))(
            dh1b, pm, pm, pm, pm, o_gla, o_fox_t, o_mem, g_head, wg, wf, wm, wgt, wft, wmt, wot, spread, d_to_do)


def _ff2_loss(a, w2, h1, g_final, target):
    s, k = a.shape
    tm = min(s, 256)

    def body(a_ref, w_ref, h1_ref, g_ref, t_ref, dh_ref, dhb_ref, loss_ref, dg_ref):
        @pl.when(pl.program_id(0) == 0)
        def _():
            loss_ref[...] = jnp.zeros_like(loss_ref)
            dg_ref[...] = jnp.zeros_like(dg_ref)

        h2 = h1_ref[...] + _dot(_relu2_bf16(a_ref[...]), w_ref[...])
        r = lax.rsqrt(jnp.mean(h2 * h2, axis=-1, keepdims=True) + EPS)
        xn = h2 * r
        g = g_ref[...]
        err = xn * g - t_ref[...]
        e2 = _fold8(err * err)
        part = e2[:, 0:128]
        for c in range(1, D // 128):
            part = part + e2[:, 128 * c:128 * (c + 1)]
        loss_ref[...] += part
        dy = err * (1.0 / D)
        dg_ref[...] += _fold8(dy * xn)
        dxn = dy * g
        dh = r * (dxn - xn * jnp.mean(dxn * xn, axis=-1, keepdims=True))
        dh_ref[...] = dh
        dhb_ref[...] = dh.astype(BF16)

    row = lambda i: (i, 0)
    const = lambda i: (0, 0)
    return pl.pallas_call(
        body, grid=(s // tm,),
        in_specs=[pl.BlockSpec((tm, k), row), pl.BlockSpec((k, D), const, pipeline_mode=pl.Buffered(1)),
                  pl.BlockSpec((tm, D), row), pl.BlockSpec((1, D), const), pl.BlockSpec((tm, D), row)],
        out_specs=[pl.BlockSpec((tm, D), row), pl.BlockSpec((tm, D), row), pl.BlockSpec((8, 128), const),
                   pl.BlockSpec((8, D), const)],
        out_shape=[jax.ShapeDtypeStruct((s, D), F32), jax.ShapeDtypeStruct((s, D), BF16),
                   jax.ShapeDtypeStruct((8, 128), F32), jax.ShapeDtypeStruct((8, D), F32)],
        name="ff2_loss", compiler_params=_cp("arbitrary"))(a, w2, h1, g_final, target)


def _adam(w, g, m, v, name):
    r, c = w.shape
    tr = r
    for cand in (512, 256, 128, 64, 32, 16, 8):
        if r % cand == 0 and cand * c * 4 <= (1 << 20):
            tr = cand
            break
    c1 = 1.0 - ADAM_B1 ** ADAM_STEP
    c2 = 1.0 - ADAM_B2 ** ADAM_STEP

    def body(w_ref, g_ref, m_ref, v_ref, d_ref, nm_ref, nv_ref):
        gv = g_ref[...]
        nm = ADAM_B1 * m_ref[...] + (1.0 - ADAM_B1) * gv
        nv = ADAM_B2 * v_ref[...] + (1.0 - ADAM_B2) * (gv * gv)
        d_ref[...] = -ADAM_LR * ((nm / c1) / (jnp.sqrt(nv / c2) + ADAM_EPS) + ADAM_WD * w_ref[...])
        nm_ref[...] = nm
        nv_ref[...] = nv

    spec = pl.BlockSpec((tr, c), lambda i: (i, 0))
    return pl.pallas_call(
        body, grid=(r // tr,), in_specs=[spec] * 4, out_specs=[spec] * 3,
        out_shape=[jax.ShapeDtypeStruct((r, c), F32)] * 3, name=name, compiler_params=_cp("parallel"))(w, g, m, v)


def _row_block(r):
    return max(d for d in range(16, 513, 16) if r % d == 0)


def _add2(a, b, name):
    n, r, c = a.shape
    tr = _row_block(r)

    def body(a_ref, b_ref, o_ref):
        o_ref[...] = (a_ref[...].astype(F32) + b_ref[...].astype(F32)).astype(BF16)

    spec = pl.BlockSpec((1, tr, c), lambda k, i: (k, i, 0))
    return pl.pallas_call(body, grid=(n, r // tr), in_specs=[spec, spec], out_specs=spec,
                          out_shape=jax.ShapeDtypeStruct((n, r, c), BF16), name=name,
                          compiler_params=_cp("parallel", "parallel"))(a, b)


def _sum4(a, name):
    _, r, c = a.shape
    tr = _row_block(r)

    def body(a_ref, o_ref):
        o_ref[...] = ((a_ref[0].astype(F32) + a_ref[1].astype(F32)) + a_ref[2].astype(F32)) + a_ref[3].astype(F32)

    return pl.pallas_call(body, grid=(r // tr,), in_specs=[pl.BlockSpec((4, tr, c), lambda i: (0, i, 0))],
                          out_specs=pl.BlockSpec((tr, c), lambda i: (i, 0)),
                          out_shape=jax.ShapeDtypeStruct((r, c), F32), name=name, compiler_params=_cp("parallel"))(a)


def _adam_small(w, gathered, m, v):
    c1 = 1.0 - ADAM_B1 ** ADAM_STEP
    c2 = 1.0 - ADAM_B2 ** ADAM_STEP

    def body(w_ref, g_ref, m_ref, v_ref, gs_ref, d_ref, nm_ref, nv_ref):
        gv = g_ref[0]
        for dev in range(1, N_DEV):
            gv = gv + g_ref[dev]
        gs_ref[...] = gv
        nm = ADAM_B1 * m_ref[...] + (1.0 - ADAM_B1) * gv
        nv = ADAM_B2 * v_ref[...] + (1.0 - ADAM_B2) * (gv * gv)
        d_ref[...] = -ADAM_LR * ((nm / c1) / (jnp.sqrt(nv / c2) + ADAM_EPS) + ADAM_WD * w_ref[...])
        nm_ref[...] = nm
        nv_ref[...] = nv

    return pl.pallas_call(body, out_shape=[jax.ShapeDtypeStruct((8, D), F32)] * 4, name="adam_small")(w, gathered, m, v)


def _place():
    return lax.axis_index("x"), lax.axis_index("y"), lax.axis_index("c")


def _other_chips(x, y):
    return [(1 - x, y), (x, 1 - y), (1 - x, 1 - y)]


def _gather_shards(p):
    def body(p_ref, out_ref, *sems):
        start, finish = _gather_ops((p_ref,), (out_ref,), sems)
        start()
        finish()

    return pl.pallas_call(
        body, out_shape=jax.ShapeDtypeStruct((N_CHIPS,) + p.shape, p.dtype), in_specs=[ANY_SPEC], out_specs=ANY_SPEC,
        scratch_shapes=GATHER_SEMS, name="gather_shards")(p)


GATHER_SEMS = [pltpu.SemaphoreType.DMA((6,)), pltpu.SemaphoreType.DMA((6,)), pltpu.SemaphoreType.DMA]


def _gather_ops(in_refs, out_refs, sems):
    (p_ref,), (out_ref,) = in_refs, out_refs
    send_sems, recv_sems, local_sem = sems
    hr = p_ref.shape[0] // 2
    x, y, cc = _place()
    sibling = (x, y, 1 - cc)
    chips = _other_chips(x, y)

    def half(chip, core):
        return out_ref.at[2 * chip[0] + chip[1], pl.ds(core * hr, hr), :]

    def copy(k, chip, core, to, src=None):
        return pltpu.make_async_remote_copy(
            src_ref=half(chip, core) if src is None else src, dst_ref=half(chip, core),
            send_sem=send_sems.at[k], recv_sem=recv_sems.at[k], device_id=to, device_id_type=MESH)

    mine = pltpu.make_async_copy(p_ref, out_ref.at[2 * x + y], local_sem)
    my_half = p_ref.at[pl.ds(cc * hr, hr), :]
    first = [copy(j, (x, y), cc, (*chip, cc), src=my_half) for j, chip in enumerate(chips)]
    passed = [copy(3 + j, chip, cc, sibling) for j, chip in enumerate(chips)]

    def start():
        mine.start()
        for cp in first:
            cp.start()

    def finish():
        for j, chip in enumerate(chips):
            copy(j, chip, cc, (x, y, cc)).wait_recv()
            passed[j].start()
        for j, chip in enumerate(chips):
            copy(3 + j, chip, 1 - cc, (x, y, cc)).wait_recv()
        for cp in first + passed:
            cp.wait_send()
        mine.wait()

    return start, finish


def _gather_side(p):
    return _Side([p], [jax.ShapeDtypeStruct((N_CHIPS,) + p.shape, p.dtype)], GATHER_SEMS, _gather_ops)


def _swap_halves(g):
    n, r, c = g.shape
    hr = r // 2

    def body(g_ref, out_ref, send_sem, recv_sem):
        x, y, cc = _place()
        cp = pltpu.make_async_remote_copy(
            src_ref=g_ref.at[:, pl.ds((1 - cc) * hr, hr), :], dst_ref=out_ref,
            send_sem=send_sem, recv_sem=recv_sem, device_id=(x, y, 1 - cc), device_id_type=MESH)
        cp.start()
        cp.wait()

    any_spec = pl.BlockSpec(memory_space=pl.ANY)
    return pl.pallas_call(
        body, out_shape=jax.ShapeDtypeStruct((n, hr, c), g.dtype), in_specs=[any_spec], out_specs=any_spec,
        scratch_shapes=[pltpu.SemaphoreType.DMA, pltpu.SemaphoreType.DMA], name="swap_halves")(g)


SCATTER_SEMS = [pltpu.SemaphoreType.DMA((7,)), pltpu.SemaphoreType.DMA((7,)), pltpu.SemaphoreType.DMA]


def _scatter_ops(in_refs, out_refs, sems):
    (p_ref,), (out_ref,) = in_refs, out_refs
    send_sems, recv_sems, local_sem = sems
    hr = p_ref.shape[1]
    x, y, cc = _place()
    me = 2 * x + y
    sibling = (x, y, 1 - cc)
    chips = _other_chips(x, y)
    ids = [2 * chip[0] + chip[1] for chip in chips]

    def land(src, core):
        return out_ref.at[src, pl.ds(core * hr, hr), :]

    def copy(k, src_ref, dst_ref, to):
        return pltpu.make_async_remote_copy(src_ref=src_ref, dst_ref=dst_ref, send_sem=send_sems.at[k],
                                            recv_sem=recv_sems.at[k], device_id=to, device_id_type=MESH)

    mine = pltpu.make_async_copy(p_ref.at[me], land(me, cc), local_sem)
    sends = [copy(j, p_ref.at[ids[j]], land(me, cc), (*chip, cc)) for j, chip in enumerate(chips)]
    sends.append(copy(3, p_ref.at[me], land(me, cc), sibling))
    passed = [copy(4 + j, land(ids[j], cc), land(ids[j], cc), sibling) for j in range(3)]

    def start():
        mine.start()
        for cp in sends:
            cp.start()

    def finish():
        for j in range(3):
            copy(j, p_ref.at[me], land(ids[j], cc), (x, y, cc)).wait_recv()
            passed[j].start()
        copy(3, p_ref.at[me], land(me, 1 - cc), (x, y, cc)).wait_recv()
        for j in range(3):
            copy(4 + j, p_ref.at[me], land(ids[j], 1 - cc), (x, y, cc)).wait_recv()
        for cp in sends + passed:
            cp.wait_send()
        mine.wait()

    return start, finish


def _scatter_side(p):
    n, hr, c = p.shape
    return _Side([p], [jax.ShapeDtypeStruct((n, 2 * hr, c), p.dtype)], SCATTER_SEMS, _scatter_ops)


def _gather_small(blk):
    m, n = blk.shape

    def body(x_ref, out_ref, send_sems, recv_sems, local_sem):
        x, y, cc = _place()
        me, sibling = (x, y, cc), (x, y, 1 - cc)
        chips = _other_chips(x, y)

        def slot(px, py, pc):
            return out_ref.at[4 * px + 2 * py + pc]

        def copy(k, block, to, src=None):
            return pltpu.make_async_remote_copy(
                src_ref=slot(*block) if src is None else src, dst_ref=slot(*block),
                send_sem=send_sems.at[k], recv_sem=recv_sems.at[k], device_id=to, device_id_type=MESH)

        mine = pltpu.make_async_copy(x_ref, slot(*me), local_sem)
        mine.start()
        first = [copy(0, me, sibling, src=x_ref)]
        first += [copy(1 + j, me, (*chip, cc), src=x_ref) for j, chip in enumerate(chips)]
        for cp in first:
            cp.start()
        passed = [copy(4 + j, (*chip, cc), sibling) for j, chip in enumerate(chips)]
        for j, chip in enumerate(chips):
            copy(1 + j, (*chip, cc), me).wait_recv()
            passed[j].start()
        copy(0, sibling, me).wait_recv()
        for j, chip in enumerate(chips):
            copy(4 + j, (*chip, 1 - cc), me).wait_recv()
        for cp in first + passed:
            cp.wait_send()
        mine.wait()

    vmem = pl.BlockSpec(memory_space=pltpu.VMEM)
    return pl.pallas_call(
        body, out_shape=jax.ShapeDtypeStruct((N_DEV, m, n), blk.dtype), in_specs=[vmem], out_specs=vmem,
        scratch_shapes=[pltpu.SemaphoreType.DMA((7,)), pltpu.SemaphoreType.DMA((7,)), pltpu.SemaphoreType.DMA],
        name="gather_small")(blk)


def _shard_shape(name, full_shape):
    shp = list(full_shape)
    shp[SHARD_AXIS[name]] //= N_CHIPS
    return tuple(shp)


FULL_SHAPES = {"w_in": (D, 6680), "w_alpha_up": (16, 256), "w_mem_kv": (D, D), "w_gla_o": (512, D), "w_fox_o": (512, D),
               "w_mem_o": (512, D), "w_out": (D, D), "w_ff1": (D, 4 * D), "w_ff2": (4 * D, D)}


def _pack_a(sh, dtype):
    w = sh["w_in"].astype(dtype)
    return jnp.concatenate([w[:, 0:PACK_W], jnp.pad(w[:, PACK_W:], ((0, 0), (0, 2 * PACK_W - w.shape[1])))], axis=0)


def _pack_b(sh, dtype):
    o3 = jnp.concatenate([sh["w_gla_o"], sh["w_fox_o"], sh["w_mem_o"], jnp.zeros((512, 256), sh["w_gla_o"].dtype)], axis=1)
    au = jnp.pad(sh["w_alpha_up"], ((0, PACK_ROWS_B - 3072 - 16), (0, PACK_W - 64)))
    return jnp.concatenate([sh["w_ff1"], sh["w_ff2"], sh["w_mem_kv"], sh["w_out"], o3, au], axis=0).astype(dtype)


def _unpack_a(pa):
    return {"w_in": jnp.concatenate([pa[0:1024], pa[1024:2048, 0:1670 - PACK_W]], axis=1)}


def _unpack_b(pb):
    return {"w_ff1": pb[0:1024], "w_ff2": pb[1024:2048], "w_mem_kv": pb[2048:2304], "w_out": pb[2304:2560],
            "w_gla_o": pb[2560:3072, 0:256], "w_fox_o": pb[2560:3072, 256:512], "w_mem_o": pb[2560:3072, 512:768],
            "w_alpha_up": pb[3072:3088, 0:64]}


def _unpack(packed):
    return {**_unpack_a(packed[0:PACK_ROWS_A]), **_unpack_b(packed[PACK_ROWS_A:])}


def _split_shards(name, full):
    return jnp.split(full, N_CHIPS, axis=SHARD_AXIS[name])


def _pack_small(vals, scalar=None):
    row4 = jnp.concatenate([vals["b_alpha"].reshape(-1), vals["b_forget"].reshape(-1), jnp.zeros((D - 264,), F32)])
    row5 = jnp.concatenate([vals["g_gla_head"].reshape(-1), jnp.zeros((D - 512,), F32)])
    row6 = jnp.zeros((D,), F32) if scalar is None else jnp.broadcast_to(scalar, (D,))
    rows = [vals["g_mix"].reshape(-1), vals["g_mem"].reshape(-1), vals["g_ffn"].reshape(-1), vals["g_final"].reshape(-1),
            row4, row5, row6, jnp.zeros((D,), F32)]
    return jnp.stack(rows)


def _unpack_small(blk):
    return {"g_mix": blk[0].reshape(1, D), "g_mem": blk[1].reshape(1, D), "g_ffn": blk[2].reshape(1, D),
            "g_final": blk[3].reshape(D), "b_alpha": blk[4, 0:256].reshape(1, 256), "b_forget": blk[4, 256:264].reshape(1, 8),
            "g_gla_head": blk[5, 0:512].reshape(1, 4, 128)}


def _local_step(x, mem, target, wb, small, exchange=None):
    s = x.shape[0]
    nm = mem.shape[0]
    t = _row_tile(s)
    nb = s // t
    w_in = wb["w_in"]
    w_main = jnp.concatenate([w_in[:, 3608:6680], w_in[:, 0:1536], w_in[:, 1552:3088], w_in[:, 3096:3608]], axis=1)
    w_e = jnp.concatenate([w_in[:, 1536:1552], w_in[:, 3088:3096], jnp.zeros((D, PE_W - 24), BF16)], axis=1)
    w_in_pt = jnp.concatenate([w_main, w_e, jnp.zeros((D, DP_W - PM_W - PE_W), BF16)], axis=1).T
    b_alpha = small["b_alpha"].reshape(1, 256)
    bias_e = jnp.concatenate([jnp.zeros((FF_LANE,), F32), small["b_forget"].reshape(-1),
                              jnp.zeros((PE_W - FF_LANE - 8,), F32)]).reshape(1, PE_W)
    g_mix, g_mem, g_ffn = small["g_mix"].reshape(1, D), small["g_mem"].reshape(1, D), small["g_ffn"].reshape(1, D)
    g_final = small["g_final"].reshape(1, D)
    g_head = small["g_gla_head"].reshape(1, 512)

    u, r1 = _rms_fwd(x, g_mix, "norm_mix")
    big = min(s, 1024)
    if exchange is None:
        pm = _mm_nn(u, w_main, out_dtype=BF16, tm=big, tn=PM_W // 4, tk=D, name="proj_main")
    else:
        pm, gathered = _mm_nn(u, w_main, out_dtype=BF16, tm=big, tn=PM_W // 4, tk=D, name="proj_main", side=exchange.gather)
        wb = {**wb, **exchange.weights(gathered)}
    wau_p = jnp.concatenate([wb["w_alpha_up"], jnp.zeros((PE_W - 16, 256), BF16)], axis=0)
    pe = _mm_nn(u, w_e, out_dtype=F32, tm=t, tn=PE_W, tk=D, name="proj_narrow")
    o_gla, states = _gla_fwd(pm, pe, wau_p, b_alpha)
    fcum = _fcum_fwd(pe, bias_e)
    tb = _fox_tables()
    qf_aug, k_aug, v_aug, vt, qt, kt = _fox_prep(pm, fcum, None, tb, backward=False)
    o_fox, lse = _fox_fwd(k_aug, qf_aug, vt)
    mn, rm = _rms_fwd(mem, g_mem, "norm_mem")
    mkv = _mm_nn(mn, wb["w_mem_kv"], out_dtype=BF16, tm=nm, tn=512, tk=D, name="mem_kv")
    o_mem = _mem_attn_fwd(pm, mkv)
    merged, h1, u2, r2 = _merge_fwd(x, pm, o_gla, o_fox, o_mem, g_head, wb["w_gla_o"], wb["w_fox_o"], wb["w_mem_o"],
                                    wb["w_out"], g_ffn)
    a = _mm_nn(u2, wb["w_ff1"], out_dtype=BF16, tm=big, tn=1024, tk=D, name="ff1")
    dh2, dh2b, loss8, dgfin8 = _ff2_loss(a, wb["w_ff2"], h1, g_final, target)
    loss = 0.5 * jnp.sum(loss8) / D

    da = _mm_nn(dh2b, wb["w_ff2"].T, out_dtype=BF16, tm=t, tn=1024, tk=D, name="d_act",
                epi=lambda acc, at: acc * (2.0 * jnp.maximum(at.astype(F32), 0.0)), extra=a)
    gw = {}
    gw["w_ff2"] = _mm_tn(a, dh2b, tm=1024, tn=D, ts=t, name="dw_ff2", a_fn=_relu2_bf16)
    gw["w_ff1"] = _mm_tn(u2, da, tm=D, tn=1024, ts=t, name="dw_ff1")
    dh1, dh1b, dgffn8 = _mm_norm_bwd(da, wb["w_ff1"].T, h1, r2, g_ffn, dh2, name="d_h1", want_bf16=True)
    gw["w_out"] = _mm_tn(merged, dh1b, tm=D, tn=D, ts=t, name="dw_out")
    (dgates, dgg, do_gla, do_aug, do_t, do_mem, gw["w_gla_o"], gw["w_fox_o"], gw["w_mem_o"], dgh8) = _merge_bwd(
        dh1b, pm, o_gla, o_fox, o_mem, g_head, wb["w_gla_o"], wb["w_fox_o"], wb["w_mem_o"],
        wb["w_gla_o"].T, wb["w_fox_o"].T, wb["w_mem_o"].T, wb["w_out"].T, tb["spread"], tb["d_to_do"])
    dgq, dgk, dgv, de_gla, dwau_p, dba8 = _gla_bwd(pm, pe, wau_p, wau_p.T, b_alpha, do_gla, states)
    gw["w_alpha_up"] = dwau_p[0:16, :]
    q_aug = _fox_prep(pm, fcum, lse.reshape(8, s), tb, backward=True)
    dfq_t, dfrow, dfk_t, dfv_t, dfcol = _fox_bwd(q_aug, do_aug, qt, do_t, k_aug, v_aug, kt)
    dfq, dfk, dfv, df = _fox_post(dfq_t, dfk_t, dfv_t, dfrow[:, 0, :], dfcol.reshape(8, s), tb)
    de_fox, dbf8 = _fcum_bwd(pe, bias_e, df)
    dmq, dmk, dmv = _mem_attn_bwd(pm, mkv, do_mem)
    dmkv = jnp.concatenate([dmk, dmv], axis=1).astype(BF16)
    gw["w_mem_kv"] = _mm_tn(mn, dmkv, tm=D, tn=D, ts=nm, name="dw_mem_kv")
    dmn_g = _mm_nn(dmkv, wb["w_mem_kv"].T, out_dtype=F32, tm=nm, tn=D, tk=D, name="d_mem_norm")
    dgmem8 = _gain_grad(dmn_g, mem, rm, "dg_mem")
    dproj = jnp.concatenate(
        [dgates, dgq, dgk, dgv, dgg, dfq, dfk, dfv, dmq,
         (de_gla + de_fox).astype(BF16), jnp.zeros((s, DP_W - PM_W - PE_W), BF16)], axis=1)
    dwp = _mm_tn(u, dproj, tm=D, tn=1024, ts=t, name="dw_in")
    gw["w_in"] = jnp.concatenate([dwp[:, 3072:4608], dwp[:, PM_W:PM_W + 16], dwp[:, 4608:6144],
                                  dwp[:, PM_W + 16:PM_W + 24], dwp[:, 6144:6656], dwp[:, 0:3072]], axis=1)
    if exchange is None:
        grad_x, dgmix8 = _mm_norm_bwd(dproj, w_in_pt, x, r1, g_mix, dh1, name="d_x", want_bf16=False)
        exchanged = None
    else:
        grad_x, dgmix8, exchanged = _mm_norm_bwd(dproj, w_in_pt, x, r1, g_mix, dh1, name="d_x", want_bf16=False,
                                                 side=exchange.scatter(gw))
    gs = {"g_mix": dgmix8.sum(0), "g_mem": dgmem8.sum(0), "g_ffn": dgffn8.sum(0), "g_final": dgfin8.sum(0),
          "b_alpha": dba8.sum(0), "b_forget": dbf8.sum(0)[FF_LANE:FF_LANE + 8], "g_gla_head": dgh8.sum(0)}
    return loss, grad_x, gw, gs, exchanged


def kernel(x, mem, g_mix, w_in, w_alpha_up, b_alpha, b_forget, g_gla_head, g_mem, w_mem_kv, w_gla_o, w_fox_o, w_mem_o, w_out, g_ffn, w_ff1, w_ff2, g_final, loss_target, m_g_mix, m_w_in, m_w_alpha_up, m_b_alpha, m_b_forget, m_g_gla_head, m_g_mem, m_w_mem_kv, m_w_gla_o, m_w_fox_o, m_w_mem_o, m_w_out, m_g_ffn, m_w_ff1, m_w_ff2, m_g_final, v_g_mix, v_w_in, v_w_alpha_up, v_b_alpha, v_b_forget, v_g_gla_head, v_g_mem, v_w_mem_kv, v_w_gla_o, v_w_fox_o, v_w_mem_o, v_w_out, v_g_ffn, v_w_ff1, v_w_ff2, v_g_final):
    args = dict(locals())
    w_sh = {n: args[n][0] for n in WEIGHTS}
    m_sh = {n: args["m_" + n][0] for n in WEIGHTS}
    v_sh = {n: args["v_" + n][0] for n in WEIGHTS}
    small = {n: args[n] for n in SMALL}

    def whole(parts):
        return {n: jnp.concatenate([p[n] for p in parts], axis=SHARD_AXIS[n]) for n in parts[0]}

    class Exchange:
        gather = _gather_side(_pack_b(w_sh, BF16))

        @staticmethod
        def weights(gathered):
            return whole([_unpack_b(gathered[k]) for k in range(N_CHIPS)])

        @staticmethod
        def scatter(gw):
            by_chip = {n: _split_shards(n, gw[n]) for n in WEIGHTS}
            packed = jnp.stack([jnp.concatenate([_pack_a({n: by_chip[n][k] for n in WEIGHTS}, BF16),
                                                 _pack_b({n: by_chip[n][k] for n in WEIGHTS}, BF16)], axis=0)
                                for k in range(N_CHIPS)])
            hr = PACK_ROWS // 2
            mine = lax.dynamic_slice_in_dim(packed, lax.axis_index("c") * hr, hr, axis=1)
            return _scatter_side(_add2(mine, _swap_halves(packed), "chip_sum"))

    gathered_a = _gather_shards(_pack_a(w_sh, BF16))
    wb = whole([_unpack_a(gathered_a[k]) for k in range(N_CHIPS)])
    loss, grad_x, gw, gs, by_chip = _local_step(x[0], mem[0], loss_target[0], wb, small, Exchange)
    g_out = _unpack(_sum4(by_chip, "shard_sum"))
    d_out, m_out, v_out = {}, {}, {}
    for n in WEIGHTS:
        d_out[n], m_out[n], v_out[n] = _adam(w_sh[n], g_out[n], m_sh[n], v_sh[n], "adam_" + n)

    small_all = _gather_small(_pack_small(gs, loss))
    sm = {n: args["m_" + n] for n in SMALL}
    sv = {n: args["v_" + n] for n in SMALL}
    gs_sum, sd, snm, snv = _adam_small(_pack_small(small), small_all, _pack_small(sm), _pack_small(sv))
    gs_o, sd_o, snm_o, snv_o = _unpack_small(gs_sum), _unpack_small(sd), _unpack_small(snm), _unpack_small(snv)

    names = ["g_mix", "w_in", "w_alpha_up", "b_alpha", "b_forget", "g_gla_head", "g_mem", "w_mem_kv", "w_gla_o", "w_fox_o",
             "w_mem_o", "w_out", "g_ffn", "w_ff1", "w_ff2", "g_final"]

    def pick(big, sml, n):
        return big[n][None] if n in big else sml[n]

    outs = [gs_sum[6, 0], grad_x[None]]
    for big, sml in ((g_out, gs_o), (d_out, sd_o), (m_out, snm_o), (v_out, snv_o)):
        outs += [pick(big, sml, n) for n in names]
    return tuple(outs)
```

```python
import functools

import numpy as np
import jax
import jax.numpy as jnp
from jax import lax
from jax.experimental import pallas as pl
from jax.experimental.pallas import tpu as pltpu

F32 = jnp.float32
BF16 = jnp.bfloat16
HI = lax.Precision.HIGHEST
MESH = pl.DeviceIdType.MESH

EPS = 1e-6
D = 1024
CHUNK = 64
GLA_TAU = 16.0
N_CHIPS = 4
N_DEV = 8
VMEM_LIMIT_BYTES = 56 * 1024 * 1024

ADAM_LR, ADAM_B1, ADAM_B2, ADAM_EPS, ADAM_WD, ADAM_STEP = 0.001, 0.9, 0.999, 1e-08, 0.01, 10

PM_W = 6656
PE_W = 128
DP_W = 7168
C_GQ, C_GK, C_GV, C_GG, C_FQ, C_FK, C_FV, C_MQ = 3072, 3328, 3584, 4096, 4608, 5120, 5632, 6144
FF_LANE = 16

WEIGHTS = ("w_in", "w_alpha_up", "w_mem_kv", "w_gla_o", "w_fox_o", "w_mem_o", "w_out", "w_ff1", "w_ff2")
SHARD_AXIS = {"w_in": 1, "w_alpha_up": 1, "w_mem_kv": 0, "w_gla_o": 1, "w_fox_o": 1, "w_mem_o": 1, "w_out": 0,
              "w_ff1": 1, "w_ff2": 0}
SMALL = ("g_mix", "g_mem", "g_ffn", "g_final", "b_alpha", "b_forget", "g_gla_head")
PACK_W = 1024
PACK_ROWS_A = 2048
PACK_ROWS_B = 3104
PACK_ROWS = PACK_ROWS_A + PACK_ROWS_B


def _cp(*sem):
    return pltpu.CompilerParams(dimension_semantics=sem, vmem_limit_bytes=VMEM_LIMIT_BYTES)


def _dot(a, b, **kw):
    return jnp.dot(a, b, preferred_element_type=F32, **kw)


def _dot_nt(a, b, **kw):
    return lax.dot_general(a, b, (((1,), (1,)), ((), ())), preferred_element_type=F32, **kw)


def _dot_tn(a, b, **kw):
    return lax.dot_general(a, b, (((0,), (0,)), ((), ())), preferred_element_type=F32, **kw)


def _sigmoid(x):
    return 1.0 / (1.0 + jnp.exp(-x))


def _log_sigmoid(x):
    return -(jnp.maximum(-x, 0.0) + jnp.log1p(jnp.exp(-jnp.abs(x))))


def _fold8(x):
    m, n = x.shape
    return x.reshape(m // 8, 8, n).sum(axis=0)


def _iota(shape, dim):
    return lax.broadcasted_iota(jnp.int32, shape, dim)


def _row_tile(s):
    return min(s, 512)


class _Side:
    def __init__(self, inputs, out_shape, scratch, ops):
        self.inputs, self.out_shape, self.scratch, self.ops = list(inputs), list(out_shape), list(scratch), ops


ANY_SPEC = pl.BlockSpec(memory_space=pl.ANY)


def _mm_nn(a, b, *, out_dtype, tm, tn, tk, name, a_fn=None, epi=None, extra=None, side=None):
    m, k = a.shape
    _, n = b.shape
    nk = k // tk
    n_in = 2 + (extra is not None)
    n_sin = 0 if side is None else len(side.inputs)
    n_sout = 0 if side is None else len(side.out_shape)

    def body_one(*refs):
        a_ref, b_ref = refs[0], refs[1]
        o_ref = refs[n_in + n_sin]
        if side is not None:
            start, finish = side.ops(refs[n_in:n_in + n_sin], refs[n_in + n_sin + 1:n_in + n_sin + 1 + n_sout],
                                     refs[n_in + n_sin + 1 + n_sout:])
            pl.when((pl.program_id(0) == 0) & (pl.program_id(1) == 0))(start)
        at = a_ref[...] if a_fn is None else a_fn(a_ref[...])
        r = _dot(at, b_ref[...])
        if epi is not None:
            r = epi(r, None if extra is None else refs[2][...])
        o_ref[...] = r.astype(out_dtype)
        if side is not None:
            pl.when((pl.program_id(0) == m // tm - 1) & (pl.program_id(1) == n // tn - 1))(finish)

    if nk == 1:
        in_specs = [pl.BlockSpec((tm, k), lambda i, j: (i, 0)), pl.BlockSpec((k, tn), lambda i, j: (0, j))]
        args = [a, b]
        if extra is not None:
            in_specs.append(pl.BlockSpec((tm, tn), lambda i, j: (i, j)))
            args.append(extra)
        out_specs = pl.BlockSpec((tm, tn), lambda i, j: (i, j))
        out_shape = jax.ShapeDtypeStruct((m, n), out_dtype)
        if side is None:
            return pl.pallas_call(
                body_one, grid=(m // tm, n // tn), in_specs=in_specs, out_specs=out_specs, out_shape=out_shape,
                name=name, compiler_params=_cp("parallel", "parallel"))(*args)
        return pl.pallas_call(
            body_one, grid=(m // tm, n // tn), in_specs=in_specs + [ANY_SPEC] * n_sin,
            out_specs=[out_specs] + [ANY_SPEC] * n_sout, out_shape=[out_shape] + side.out_shape,
            scratch_shapes=side.scratch, name=name, compiler_params=_cp("arbitrary", "arbitrary"))(*args, *side.inputs)
    assert side is None

    def body(*refs):
        if extra is None:
            a_ref, b_ref, o_ref, acc = refs
            x_ref = None
        else:
            a_ref, b_ref, x_ref, o_ref, acc = refs
        kk = pl.program_id(2)

        @pl.when(kk == 0)
        def _():
            acc[...] = jnp.zeros_like(acc)

        at = a_ref[...]
        if a_fn is not None:
            at = a_fn(at)
        acc[...] += _dot(at, b_ref[...])

        @pl.when(kk == nk - 1)
        def _():
            r = acc[...]
            if epi is not None:
                r = epi(r, None if x_ref is None else x_ref[...])
            o_ref[...] = r.astype(out_dtype)

    in_specs = [pl.BlockSpec((tm, tk), lambda i, j, kk: (i, kk)), pl.BlockSpec((tk, tn), lambda i, j, kk: (kk, j))]
    args = [a, b]
    if extra is not None:
        in_specs.append(pl.BlockSpec((tm, tn), lambda i, j, kk: (i, j)))
        args.append(extra)
    return pl.pallas_call(
        body, grid=(m // tm, n // tn, nk), in_specs=in_specs,
        out_specs=pl.BlockSpec((tm, tn), lambda i, j, kk: (i, j)),
        out_shape=jax.ShapeDtypeStruct((m, n), out_dtype),
        scratch_shapes=[pltpu.VMEM((tm, tn), F32)], name=name,
        compiler_params=_cp("parallel", "parallel", "arbitrary"))(*args)


def _mm_tn(a, b, *, tm, tn, ts, name, a_fn=None):
    s, m = a.shape
    _, n = b.shape
    ns = s // ts

    def body(a_ref, b_ref, o_ref, acc):
        kk = pl.program_id(2)

        @pl.when(kk == 0)
        def _():
            acc[...] = jnp.zeros_like(acc)

        at = a_ref[...]
        if a_fn is not None:
            at = a_fn(at)
        acc[...] += _dot_tn(at, b_ref[...])

        @pl.when(kk == ns - 1)
        def _():
            o_ref[...] = acc[...]

    return pl.pallas_call(
        body, grid=(m // tm, n // tn, ns),
        in_specs=[pl.BlockSpec((ts, tm), lambda i, j, kk: (kk, i)), pl.BlockSpec((ts, tn), lambda i, j, kk: (kk, j))],
        out_specs=pl.BlockSpec((tm, tn), lambda i, j, kk: (i, j)),
        out_shape=jax.ShapeDtypeStruct((m, n), F32),
        scratch_shapes=[pltpu.VMEM((tm, tn), F32)], name=name,
        compiler_params=_cp("parallel", "parallel", "arbitrary"))(a, b)


def _relu2_bf16(t):
    r = jnp.maximum(t.astype(F32), 0.0)
    return (r * r).astype(BF16)


def _rms_fwd(x, g, name):
    s, d = x.shape
    tm = min(s, 512)

    def body(x_ref, g_ref, u_ref, r_ref):
        xv = x_ref[...]
        r = lax.rsqrt(jnp.mean(xv * xv, axis=-1, keepdims=True) + EPS)
        u_ref[...] = ((xv * r) * g_ref[...]).astype(BF16)
        r_ref[...] = r

    return pl.pallas_call(
        body, grid=(s // tm,),
        in_specs=[pl.BlockSpec((tm, d), lambda i: (i, 0)), pl.BlockSpec((1, d), lambda i: (0, 0))],
        out_specs=[pl.BlockSpec((tm, d), lambda i: (i, 0)), pl.BlockSpec((tm, 1), lambda i: (i, 0))],
        out_shape=[jax.ShapeDtypeStruct((s, d), BF16), jax.ShapeDtypeStruct((s, 1), F32)],
        name=name, compiler_params=_cp("parallel"))(x, g)


def _mm_norm_bwd(a, b, xin, r, g, dres, *, name, want_bf16, side=None):
    s, k = a.shape
    tm = min(s, 256)
    n_out = 3 if want_bf16 else 2
    n_sin = 0 if side is None else len(side.inputs)
    n_sout = 0 if side is None else len(side.out_shape)

    def body(a_ref, b_ref, x_ref, r_ref, g_ref, dres_ref, *rest):
        outs = rest[n_sin:n_sin + n_out]
        dx_ref, dg_ref = outs[0], outs[-1]
        if side is not None:
            start, finish = side.ops(rest[:n_sin], rest[n_sin + n_out:n_sin + n_out + n_sout], rest[n_sin + n_out + n_sout:])
            pl.when(pl.program_id(0) == 0)(start)

        @pl.when(pl.program_id(0) == 0)
        def _():
            dg_ref[...] = jnp.zeros_like(dg_ref)

        du = _dot(a_ref[...], b_ref[...])
        xn = x_ref[...] * r_ref[...]
        dg_ref[...] += _fold8(du * xn)
        dxn = du * g_ref[...]
        dx = dres_ref[...] + r_ref[...] * (dxn - xn * jnp.mean(dxn * xn, axis=-1, keepdims=True))
        dx_ref[...] = dx
        if want_bf16:
            outs[1][...] = dx.astype(BF16)
        if side is not None:
            pl.when(pl.program_id(0) == s // tm - 1)(finish)

    row = lambda i: (i, 0)
    const = lambda i: (0, 0)
    out_specs = [pl.BlockSpec((tm, D), row)]
    out_shape = [jax.ShapeDtypeStruct((s, D), F32)]
    if want_bf16:
        out_specs.append(pl.BlockSpec((tm, D), row))
        out_shape.append(jax.ShapeDtypeStruct((s, D), BF16))
    out_specs.append(pl.BlockSpec((8, D), const))
    out_shape.append(jax.ShapeDtypeStruct((8, D), F32))
    side_in = [] if side is None else side.inputs
    return pl.pallas_call(
        body, grid=(s // tm,),
        in_specs=[pl.BlockSpec((tm, k), row), pl.BlockSpec((k, D), const, pipeline_mode=pl.Buffered(1)),
                  pl.BlockSpec((tm, D), row), pl.BlockSpec((tm, 1), row), pl.BlockSpec((1, D), const),
                  pl.BlockSpec((tm, D), row)] + [ANY_SPEC] * n_sin,
        out_specs=out_specs + [ANY_SPEC] * n_sout, out_shape=out_shape + ([] if side is None else side.out_shape),
        scratch_shapes=[] if side is None else side.scratch,
        name=name, compiler_params=_cp("arbitrary"))(a, b, xin, r, g, dres, *side_in)


def _gla_consts():
    lmask = _iota((4 * CHUNK, CHUNK), 0) % CHUNK >= _iota((4 * CHUNK, CHUNK), 1)
    hmask = _iota((256, 256), 0) // CHUNK == _iota((256, 256), 1) // CHUNK
    bd = _iota((256, 512), 0) // CHUNK == _iota((256, 512), 1) // 128
    return lmask, hmask, bd


def _fold_heads(x):
    return x[0:64] + x[64:128] + x[128:192] + x[192:256]


def _gla_decays(la, b_scr, dec_scr):
    tri = (_iota((CHUNK, CHUNK), 0) >= _iota((CHUNK, CHUNK), 1)).astype(BF16)
    ones = jnp.ones((CHUNK, 128), BF16)
    for c in range(la.shape[0] // CHUNK):
        la3 = _split3(la[CHUNK * c:CHUNK * (c + 1)])
        b_scr[CHUNK * c:CHUNK * (c + 1), :] = _sum3(_dot(tri, la3), 1)
        dec_scr[c] = jnp.exp(_sum3(_dot_tn(la3, ones), 0))


def _gla_chunk(b, qc, kc):
    bl = b[CHUNK - 1:CHUNK, :]
    ep, en, ek = jnp.exp(b), jnp.exp(-b), jnp.exp(bl - b)
    return bl, ep, en, ek, qc * ep, qc * en, kc * en, kc * ep, kc * ek


def _gla_fwd(pm, pe, wau_p, b_alpha):
    s = pm.shape[0]
    t = _row_tile(s)
    nc = t // CHUNK

    def body(q_ref, k_ref, v_ref, e_ref, wau_ref, ba_ref, o_ref, st_ref, state, b_scr, dec_scr):
        @pl.when(pl.program_id(0) == 0)
        def _():
            state[...] = jnp.zeros_like(state)

        z = _dot(e_ref[...].astype(BF16), wau_ref[...]) + ba_ref[...]
        _gla_decays(_log_sigmoid(z) * (1.0 / GLA_TAU), b_scr, dec_scr)
        lmask, hmask, bd = _gla_consts()

        def chunk(c, carry):
            rows = pl.ds(pl.multiple_of(c * CHUNK, CHUNK), CHUNK)
            qc = q_ref[rows, :].astype(F32) * 0.125
            kc = k_ref[rows, :].astype(F32)
            vc = v_ref[rows, :]
            _, _, _, _, qp, qn, kn, kp, kk = _gla_chunk(b_scr[rows, :], qc, kc)
            decb = jnp.concatenate([dec_scr[c]] * 4, axis=1)
            qs = jnp.where(hmask, jnp.concatenate([qp] * 4, axis=0), 0.0).astype(BF16)
            qns = jnp.where(hmask, jnp.concatenate([qn] * 4, axis=0), 0.0).astype(BF16)
            attn = jnp.where(lmask, _dot_nt(qs, kn.astype(BF16)), _dot_nt(qns, kp.astype(BF16))).astype(BF16)
            st = state[...]
            o_intra = _fold_heads(jnp.where(bd, _dot(attn, vc), 0.0))
            o_ref[rows, :] = o_intra + _dot(qp.astype(BF16), st.astype(BF16))
            for h in range(4):
                st_ref[c, :, 128 * h:128 * (h + 1)] = st[64 * h:64 * (h + 1), 128 * h:128 * (h + 1)]
            kv = jnp.where(bd, _dot_tn(kk.astype(BF16), vc), 0.0)
            state[...] = st * decb + kv
            return carry

        lax.fori_loop(0, nc, chunk, 0)

    return pl.pallas_call(
        body, grid=(s // t,),
        in_specs=[pl.BlockSpec((t, 256), lambda i: (i, C_GQ // 256)), pl.BlockSpec((t, 256), lambda i: (i, C_GK // 256)),
                  pl.BlockSpec((t, 512), lambda i: (i, C_GV // 512)), pl.BlockSpec((t, PE_W), lambda i: (i, 0)),
                  pl.BlockSpec((PE_W, 256), lambda i: (0, 0)), pl.BlockSpec((1, 256), lambda i: (0, 0))],
        out_specs=[pl.BlockSpec((t, 512), lambda i: (i, 0)), pl.BlockSpec((nc, CHUNK, 512), lambda i: (i, 0, 0))],
        out_shape=[jax.ShapeDtypeStruct((s, 512), F32), jax.ShapeDtypeStruct((s // CHUNK, CHUNK, 512), F32)],
        scratch_shapes=[pltpu.VMEM((256, 512), F32), pltpu.VMEM((t, 256), F32), pltpu.VMEM((nc, 256, 128), F32)],
        name="gla_fwd", compiler_params=_cp("arbitrary"))(pm, pm, pm, pe, wau_p, b_alpha)


def _gla_bwd(pm, pe, wau_p, wau_pt, b_alpha, do, states):
    s = pm.shape[0]
    t = _row_tile(s)
    nc = t // CHUNK
    nb = s // t

    def body(q_ref, k_ref, v_ref, e_ref, wau_ref, waut_ref, ba_ref, do_ref, st_ref,
             dq_ref, dk_ref, dv_ref, de_ref, dwau_ref, dba_ref, gstate, b_scr, db_scr, dec_scr):
        @pl.when(pl.program_id(0) == 0)
        def _():
            gstate[...] = jnp.zeros_like(gstate)
            dwau_ref[...] = jnp.zeros_like(dwau_ref)
            dba_ref[...] = jnp.zeros_like(dba_ref)

        eb = e_ref[...].astype(BF16)
        z = _dot(eb, wau_ref[...]) + ba_ref[...]
        _gla_decays(_log_sigmoid(z) * (1.0 / GLA_TAU), b_scr, dec_scr)
        lmask, hmask, bd = _gla_consts()
        last_row = _iota((CHUNK, 256), 0) == CHUNK - 1

        def chunk(cc, carry):
            c = nc - 1 - cc
            rows = pl.ds(pl.multiple_of(c * CHUNK, CHUNK), CHUNK)
            qc = q_ref[rows, :].astype(F32) * 0.125
            kc = k_ref[rows, :].astype(F32)
            vc = v_ref[rows, :]
            dob = do_ref[rows, :]
            bl, ep, en, ek, qp, qn, kn, kp, kk = _gla_chunk(b_scr[rows, :], qc, kc)
            decb = jnp.concatenate([dec_scr[c]] * 4, axis=1)
            qs = jnp.where(hmask, jnp.concatenate([qp] * 4, axis=0), 0.0).astype(BF16)
            qns = jnp.where(hmask, jnp.concatenate([qn] * 4, axis=0), 0.0).astype(BF16)
            knb, kpb = kn.astype(BF16), kp.astype(BF16)
            attn = jnp.where(lmask, _dot_nt(qs, knb), _dot_nt(qns, kpb)).astype(BF16)
            st = jnp.where(bd, jnp.concatenate([st_ref[c]] * 4, axis=0), 0.0)
            g = gstate[...]
            gb = g.astype(BF16)
            do_s = jnp.where(bd, jnp.concatenate([dob] * 4, axis=0), jnp.zeros((), BF16))
            dattn = _dot_nt(do_s, vc)
            dv_ref[rows, :] = (_dot_tn(attn, do_s) + _dot(kk.astype(BF16), gb)).astype(BF16)
            dac = jnp.where(lmask, dattn, 0.0).astype(BF16)
            daa = jnp.where(lmask, 0.0, dattn).astype(BF16)
            dqp = _fold_heads(jnp.where(hmask, _dot(dac, knb), 0.0)) + _dot_nt(dob, st.astype(BF16))
            dqn = _fold_heads(jnp.where(hmask, _dot(daa, kpb), 0.0))
            dkn = _dot_tn(dac, qs)
            dkp = _dot_tn(daa, qns)
            dkk = _dot_nt(vc, gb)
            ddec = _dot_nt(jnp.ones((8, 1536), BF16), _split3(g * st))[0:1, :]
            gstate[...] = decb * g + jnp.where(bd, _dot_tn(qp.astype(BF16), dob), 0.0)
            dq_ref[rows, :] = ((dqp * ep + dqn * en) * 0.125).astype(BF16)
            dk_ref[rows, :] = (dkn * en + dkp * ep + dkk * ek).astype(BF16)
            dek = dkk * kc * ek
            db = (dqp * qc + dkp * kc) * ep - (dqn * qc + dkn * kc) * en - dek
            dbl = jnp.sum(dek, axis=0, keepdims=True) + ddec * jnp.exp(bl)
            db_scr[rows, :] = db + jnp.where(last_row, dbl, 0.0)
            return carry

        lax.fori_loop(0, nc, chunk, 0)
        triu = (_iota((CHUNK, CHUNK), 0) <= _iota((CHUNK, CHUNK), 1)).astype(BF16)
        dla = jnp.concatenate([_sum3(_dot(triu, _split3(db_scr[CHUNK * c:CHUNK * (c + 1), :])), 1) for c in range(nc)], axis=0)
        dz = dla * (1.0 / GLA_TAU) * _sigmoid(-z)
        dzb = dz.astype(BF16)
        dwau_ref[...] += _dot_tn(eb, dzb)
        dba_ref[...] += _fold8(dz)
        de_ref[...] = _dot(dzb, waut_ref[...])

    rev = lambda i: nb - 1 - i
    return pl.pallas_call(
        body, grid=(nb,),
        in_specs=[pl.BlockSpec((t, 256), lambda i: (rev(i), C_GQ // 256)), pl.BlockSpec((t, 256), lambda i: (rev(i), C_GK // 256)),
                  pl.BlockSpec((t, 512), lambda i: (rev(i), C_GV // 512)), pl.BlockSpec((t, PE_W), lambda i: (rev(i), 0)),
                  pl.BlockSpec((PE_W, 256), lambda i: (0, 0)), pl.BlockSpec((256, PE_W), lambda i: (0, 0)),
                  pl.BlockSpec((1, 256), lambda i: (0, 0)), pl.BlockSpec((t, 512), lambda i: (rev(i), 0)),
                  pl.BlockSpec((nc, CHUNK, 512), lambda i: (rev(i), 0, 0))],
        out_specs=[pl.BlockSpec((t, 256), lambda i: (rev(i), 0)), pl.BlockSpec((t, 256), lambda i: (rev(i), 0)),
                   pl.BlockSpec((t, 512), lambda i: (rev(i), 0)), pl.BlockSpec((t, PE_W), lambda i: (rev(i), 0)),
                   pl.BlockSpec((PE_W, 256), lambda i: (0, 0)), pl.BlockSpec((8, 256), lambda i: (0, 0))],
        out_shape=[jax.ShapeDtypeStruct((s, 256), BF16), jax.ShapeDtypeStruct((s, 256), BF16),
                   jax.ShapeDtypeStruct((s, 512), BF16), jax.ShapeDtypeStruct((s, PE_W), F32),
                   jax.ShapeDtypeStruct((PE_W, 256), F32), jax.ShapeDtypeStruct((8, 256), F32)],
        scratch_shapes=[pltpu.VMEM((256, 512), F32), pltpu.VMEM((t, 256), F32), pltpu.VMEM((t, 256), F32),
                        pltpu.VMEM((nc, 256, 128), F32)],
        name="gla_bwd", compiler_params=_cp("arbitrary"))(pm, pm, pm, pe, wau_p, wau_pt, b_alpha, do, states)


def _fcum_fwd(pe, bias):
    s = pe.shape[0]
    t = min(s, 256)

    def body(e_ref, b_ref, f_ref, carry):
        @pl.when(pl.program_id(0) == 0)
        def _():
            carry[...] = jnp.zeros_like(carry)

        lf = _log_sigmoid(e_ref[...] + b_ref[...])
        tri = (_iota((t, t), 0) >= _iota((t, t), 1)).astype(BF16)
        f = _sum3(_dot(tri, _split3(lf)), 1) + carry[0:1, :]
        f_ref[...] = f
        carry[...] = jnp.broadcast_to(f[t - 1:t, :], carry.shape)

    return pl.pallas_call(
        body, grid=(s // t,),
        in_specs=[pl.BlockSpec((t, PE_W), lambda i: (i, 0)), pl.BlockSpec((1, PE_W), lambda i: (0, 0))],
        out_specs=pl.BlockSpec((t, PE_W), lambda i: (i, 0)),
        out_shape=jax.ShapeDtypeStruct((s, PE_W), F32), scratch_shapes=[pltpu.VMEM((8, PE_W), F32)],
        name="fcum_fwd", compiler_params=_cp("arbitrary"))(pe, bias)


def _fcum_bwd(pe, bias, df):
    s = pe.shape[0]
    t = min(s, 256)
    nb = s // t

    def body(e_ref, b_ref, df_ref, de_ref, db_ref, carry):
        @pl.when(pl.program_id(0) == 0)
        def _():
            carry[...] = jnp.zeros_like(carry)
            db_ref[...] = jnp.zeros_like(db_ref)

        triu = (_iota((t, t), 0) <= _iota((t, t), 1)).astype(BF16)
        dlf = _sum3(_dot(triu, _split3(df_ref[...])), 1) + carry[0:1, :]
        carry[...] = jnp.broadcast_to(dlf[0:1, :], carry.shape)
        lane = _iota((t, PE_W), 1)
        dff = jnp.where((lane >= FF_LANE) & (lane < FF_LANE + 8), dlf * _sigmoid(-(e_ref[...] + b_ref[...])), 0.0)
        de_ref[...] = dff
        db_ref[...] += _fold8(dff)

    rev = lambda i: (nb - 1 - i, 0)
    return pl.pallas_call(
        body, grid=(nb,),
        in_specs=[pl.BlockSpec((t, PE_W), rev), pl.BlockSpec((1, PE_W), lambda i: (0, 0)), pl.BlockSpec((t, PE_W), rev)],
        out_specs=[pl.BlockSpec((t, PE_W), rev), pl.BlockSpec((8, PE_W), lambda i: (0, 0))],
        out_shape=[jax.ShapeDtypeStruct((s, PE_W), F32), jax.ShapeDtypeStruct((8, PE_W), F32)],
        scratch_shapes=[pltpu.VMEM((8, PE_W), F32)],
        name="fcum_bwd", compiler_params=_cp("arbitrary"))(pe, bias, df)


FOX_WIDE = 1024


def _split3(x):
    hi = x.astype(BF16)
    r = x - hi.astype(F32)
    mid = r.astype(BF16)
    lo = (r - mid.astype(F32)).astype(BF16)
    return jnp.concatenate([hi, mid, lo], axis=1)


def _sum3(x, axis):
    n = x.shape[axis] // 3
    parts = [lax.slice_in_dim(x, n * p, n * (p + 1), axis=axis) for p in range(3)]
    return (parts[0] + parts[1]) + parts[2]


def _fox_tables():
    heads, lane = np.arange(8), np.arange(64)
    spread = np.zeros((512, 1024), np.float32)
    spread[(64 * heads[:, None] + lane).ravel(), (128 * heads[:, None] + lane).ravel()] = 1.0
    def place(src_lane0, dst_off, val):
        t = np.zeros((384, 1024), np.float32)
        for p in range(3):
            t[128 * p + src_lane0 + heads, 128 * heads + dst_off + p] = val
        return t
    def const(off, val):
        c = np.zeros((1, 1024), np.float32)
        for p in range(3):
            c[0, 128 * heads + off + p] = val
        return c
    rows = np.zeros((8, 128), np.float32)
    rows[heads, FF_LANE + heads] = 1.0
    bf = lambda a: jnp.asarray(a, BF16)
    return dict(spread=bf(spread),
                f_to_q=bf(place(FF_LANE, 64, 1.0)), f_to_k=bf(place(FF_LANE, 67, -1.0)), d_to_do=bf(place(0, 64, 1.0)),
                ones_q=jnp.asarray(const(67, 1.0)), ones_k=jnp.asarray(const(64, 1.0)), ones_v=jnp.asarray(const(64, -1.0)),
                rows=jnp.asarray(rows))


LOG2E = 1.4426950408889634


def _fox_prep(pm, f128, lse8, tb, *, backward):
    s = pm.shape[0]
    tm = _row_tile(s)

    def body(*refs):
        if backward:
            q_ref, f_ref, lse_ref, sp_ref, fq_ref, cq_ref, rows_ref, qa_ref = refs
            f = f_ref[...] * LOG2E - _dot_tn(lse_ref[...], rows_ref[...], precision=HI)
            q2 = (q_ref[...].astype(F32) * (0.125 * LOG2E)).astype(BF16)
            qa_ref[...] = (_dot(q2, sp_ref[...]) + _dot(_split3(f), fq_ref[...]) + cq_ref[...]).astype(BF16)
            return
        (q_ref, k_ref, v_ref, f_ref, sp_ref, fq_ref, fk_ref, cq_ref, ck_ref, cv_ref,
         qa_ref, ka_ref, va_ref, vt_ref, qt_ref, kt_ref) = refs
        f3 = _split3(f_ref[...] * LOG2E)
        q, k, v = q_ref[...].astype(F32), k_ref[...], v_ref[...]
        sp = sp_ref[...]
        qa_ref[...] = (_dot((q * (0.125 * LOG2E)).astype(BF16), sp) + _dot(f3, fq_ref[...]) + cq_ref[...]).astype(BF16)
        ka_ref[...] = (_dot(k, sp) + _dot(f3, fk_ref[...]) + ck_ref[...]).astype(BF16)
        va_ref[...] = (_dot(v, sp) + cv_ref[...]).astype(BF16)
        vt_ref[...] = v.T
        qt_ref[...] = (q * 0.125).astype(BF16).T
        kt_ref[...] = (k.astype(F32) * 0.125).astype(BF16).T

    row = lambda i: (i, 0)
    const = lambda i: (0, 0)
    blk = lambda c: pl.BlockSpec((tm, 512), lambda i: (i, c // 512))
    wide = pl.BlockSpec((tm, 1024), row)
    mat = lambda a: pl.BlockSpec(a.shape, const)
    if backward:
        ins = [pm, f128, lse8, tb["spread"], tb["f_to_q"], tb["ones_q"], tb["rows"]]
        in_specs = [blk(C_FQ), pl.BlockSpec((tm, 128), row), pl.BlockSpec((8, tm), lambda i: (0, i))] + [mat(a) for a in ins[3:]]
        out_specs, out_shape = wide, jax.ShapeDtypeStruct((s, 1024), BF16)
    else:
        ins = [pm, pm, pm, f128, tb["spread"], tb["f_to_q"], tb["f_to_k"], tb["ones_q"], tb["ones_k"], tb["ones_v"]]
        in_specs = [blk(C_FQ), blk(C_FK), blk(C_FV), pl.BlockSpec((tm, 128), row)] + [mat(a) for a in ins[4:]]
        tr = pl.BlockSpec((512, tm), lambda i: (0, i))
        out_specs = [wide, wide, wide, tr, tr, tr]
        out_shape = [jax.ShapeDtypeStruct((s, 1024), BF16)] * 3 + [jax.ShapeDtypeStruct((512, s), BF16)] * 3
    return pl.pallas_call(body, grid=(s // tm,), in_specs=in_specs, out_specs=out_specs, out_shape=out_shape,
                          name="fox_prep_bwd" if backward else "fox_prep", compiler_params=_cp("parallel"))(*ins)


def _fox_post(dqt, dkt, dvt, rowsum8, colsum8, tb):
    s = dqt.shape[1]
    tm = _row_tile(s)

    def body(dqt_ref, dkt_ref, dvt_ref, rs_ref, cs_ref, rows_ref, dfq_ref, dfk_ref, dfv_ref, df_ref):
        dfq_ref[...] = dqt_ref[...].T.astype(BF16)
        dfk_ref[...] = dkt_ref[...].T
        dfv_ref[...] = dvt_ref[...].T
        df_ref[...] = _dot_tn(rs_ref[...] - cs_ref[...], rows_ref[...], precision=HI)

    row = lambda i: (i, 0)
    tr = pl.BlockSpec((512, tm), lambda i: (0, i))
    out = pl.BlockSpec((tm, 512), row)
    heads = pl.BlockSpec((8, tm), lambda i: (0, i))
    return pl.pallas_call(
        body, grid=(s // tm,),
        in_specs=[tr, tr, tr, heads, heads, pl.BlockSpec((8, 128), lambda i: (0, 0))],
        out_specs=[out, out, out, pl.BlockSpec((tm, 128), row)],
        out_shape=[jax.ShapeDtypeStruct((s, 512), BF16)] * 3 + [jax.ShapeDtypeStruct((s, 128), F32)],
        name="fox_post", compiler_params=_cp("parallel"))(dqt, dkt, dvt, rowsum8, colsum8, tb["rows"])


def _fox_fwd(k_aug, q_aug, vt):
    s = k_aug.shape[0]
    nh = 8
    tk = _row_tile(s)
    tq = min(s, FOX_WIDE)
    per = tq // tk

    def body(k_ref, q_ref, v_ref, o_ref, lse_ref, sbuf):
        i = pl.program_id(1)
        qa = q_ref[...]

        def scores(j):
            return _dot_nt(k_ref[pl.ds(pl.multiple_of(j * tk, tk), tk), :], qa)

        ones_row = (_iota((16, tk), 0) == 0).astype(BF16)

        def update(st, j, carry):
            m, acc = carry
            m2 = jnp.maximum(m, jnp.max(st, axis=0, keepdims=True))
            p = jnp.exp2(st - m2)
            vj = jnp.concatenate([v_ref[:, pl.ds(pl.multiple_of(j * tk, tk), tk)], ones_row], axis=0)
            return m2, jnp.exp2(m - m2) * acc + _dot(vj, p.astype(BF16))

        def step(a, carry):
            sbuf[1] = scores(2 * a + 1)
            carry = update(sbuf[0], 2 * a, carry)
            sbuf[0] = scores(2 * a + 2)
            return update(sbuf[1], 2 * a + 1, carry)

        n = i * per
        sbuf[0] = scores(0)
        carry = (jnp.full((1, tq), -1e30, F32), jnp.zeros((80, tq), F32))
        carry = lax.fori_loop(0, n // 2, step, carry)
        tri = _iota((tk, tk), 0) <= _iota((tk, tk), 1)
        late = [_dot_nt(k_ref[pl.ds(pl.multiple_of((n + r) * tk, tk), tk), :], qa[r * tk:, :]) for r in range(1, per)]
        for r in range(per):
            st = sbuf[0] if r == 0 else late[r - 1]
            head = jnp.where(tri, st[:, :tk], -1e30)
            st = head if st.shape[1] == tk else jnp.concatenate([head, st[:, tk:]], axis=1)
            part = update(st, n + r, tuple(c[:, r * tk:] for c in carry))
            carry = part if r == 0 else tuple(jnp.concatenate([old[:, :r * tk], new], axis=1) for old, new in zip(carry, part))
        m, acc = carry
        l = acc[64:65]
        o_ref[...] = (acc[0:64] / l).astype(BF16)
        lse_ref[0] = m + jnp.log2(l)

    return pl.pallas_call(
        body, grid=(nh, s // tq),
        in_specs=[pl.BlockSpec((s, 128), lambda h, i: (0, h)), pl.BlockSpec((tq, 128), lambda h, i: (i, h)),
                  pl.BlockSpec((64, s), lambda h, i: (h, 0))],
        out_specs=[pl.BlockSpec((64, tq), lambda h, i: (h, i)), pl.BlockSpec((1, 1, tq), lambda h, i: (h, 0, i))],
        out_shape=[jax.ShapeDtypeStruct((512, s), BF16), jax.ShapeDtypeStruct((nh, 1, s), F32)],
        scratch_shapes=[pltpu.VMEM((2, tk, tq), F32)],
        name="fox_fwd", compiler_params=_cp("parallel", "arbitrary"))(k_aug, q_aug, vt)


def _fox_bwd(q_aug, do_aug, qt, dot_, k_aug, v_aug, kt):
    s = q_aug.shape[0]
    nh = 8
    tq = _row_tile(s)
    tk = min(s, FOX_WIDE)
    per = tk // tq
    nqb = s // tq

    def body(qa_ref, da_ref, qt_ref, dt_ref, ka_ref, va_ref, kt_ref, dq_ref, rs_ref, dk_ref, dv_ref, dfk_ref):
        j = pl.program_id(1)

        @pl.when(j == 0)
        def _():
            dq_ref[...] = jnp.zeros_like(dq_ref)
            rs_ref[...] = jnp.zeros_like(rs_ref)

        ones_row = (_iota((16, tk), 0) == 0).astype(BF16)
        ka, va = ka_ref[...], va_ref[...]
        ks = jnp.concatenate([kt_ref[...], ones_row], axis=0)
        tri = _iota((tq, tq), 0) >= _iota((tq, tq), 1)

        def tile(i, w, carry):
            masked = w is not None
            w = tk if w is None else w
            rows = pl.ds(pl.multiple_of(i * tq, tq), tq)
            sp = _dot_nt(qa_ref[rows, :], ka[:w])
            if masked:
                last = jnp.where(tri, sp[:, w - tq:], -1e30)
                sp = last if w == tq else jnp.concatenate([sp[:, :w - tq], last], axis=1)
            p = jnp.exp2(sp)
            dsb = (p * _dot_nt(da_ref[rows, :], va[:w])).astype(BF16)
            dq = _dot_nt(ks[:, :w], dsb)
            dq_ref[:, rows] += dq[0:64]
            rs_ref[0, :, rows] += dq[64:72]
            new = (_dot(jnp.concatenate([qt_ref[:, rows], ones_row[:, :tq]], axis=0), dsb), _dot(dt_ref[:, rows], p.astype(BF16)))
            if w == tk:
                return tuple(c + d for c, d in zip(carry, new))
            return tuple(jnp.concatenate([c[:, :w] + d, c[:, w:]], axis=1) for c, d in zip(carry, new))

        carry = (jnp.zeros((80, tk), F32), jnp.zeros((64, tk), F32))
        for r in range(per):
            carry = tile(j * per + r, (r + 1) * tq, carry)
        dk, dv = lax.fori_loop((j + 1) * per, nqb, lambda i, c: tile(i, None, c), carry)
        dk_ref[...] = dk[0:64].astype(BF16)
        dv_ref[...] = dv.astype(BF16)
        dfk_ref[0] = dk[64:65]

    head_cols = lambda h, j: (0, h)
    head_rows = lambda h, j: (h, 0)
    once = dict(pipeline_mode=pl.Buffered(1))
    return pl.pallas_call(
        body, grid=(nh, s // tk),
        in_specs=[pl.BlockSpec((s, 128), head_cols, **once), pl.BlockSpec((s, 128), head_cols, **once),
                  pl.BlockSpec((64, s), head_rows, **once), pl.BlockSpec((64, s), head_rows, **once),
                  pl.BlockSpec((tk, 128), lambda h, j: (j, h)), pl.BlockSpec((tk, 128), lambda h, j: (j, h)),
                  pl.BlockSpec((64, tk), lambda h, j: (h, j))],
        out_specs=[pl.BlockSpec((64, s), head_rows), pl.BlockSpec((1, 8, s), lambda h, j: (h, 0, 0)),
                   pl.BlockSpec((64, tk), lambda h, j: (h, j)),
                   pl.BlockSpec((64, tk), lambda h, j: (h, j)), pl.BlockSpec((1, 1, tk), lambda h, j: (h, 0, j))],
        out_shape=[jax.ShapeDtypeStruct((512, s), F32), jax.ShapeDtypeStruct((nh, 8, s), F32),
                   jax.ShapeDtypeStruct((512, s), BF16),
                   jax.ShapeDtypeStruct((512, s), BF16), jax.ShapeDtypeStruct((nh, 1, s), F32)],
        name="fox_bwd", compiler_params=_cp("parallel", "arbitrary"))(q_aug, do_aug, qt, dot_, k_aug, v_aug, kt)


MEM_SCALE = 128 ** -0.5


def _mem_attn_fwd(pm, mkv):
    s = pm.shape[0]
    t = _row_tile(s)
    nm = mkv.shape[0]

    def body(q_ref, mk_ref, mv_ref, o_ref):
        for h in range(4):
            cols = slice(128 * h, 128 * (h + 1))
            sc = _dot_nt(q_ref[:, cols], mk_ref[:, cols]) * MEM_SCALE
            p = jnp.exp(sc - jnp.max(sc, axis=-1, keepdims=True))
            p = p / jnp.sum(p, axis=-1, keepdims=True)
            o_ref[:, cols] = _dot(p.astype(BF16), mv_ref[:, cols]).astype(BF16)

    return pl.pallas_call(
        body, grid=(s // t,),
        in_specs=[pl.BlockSpec((t, 512), lambda i: (i, C_MQ // 512)), pl.BlockSpec((nm, 512), lambda i: (0, 0)),
                  pl.BlockSpec((nm, 512), lambda i: (0, 1))],
        out_specs=pl.BlockSpec((t, 512), lambda i: (i, 0)),
        out_shape=jax.ShapeDtypeStruct((s, 512), BF16),
        name="mem_attn_fwd", compiler_params=_cp("parallel"))(pm, mkv, mkv)


def _mem_attn_bwd(pm, mkv, do):
    s = pm.shape[0]
    t = _row_tile(s)
    nm = mkv.shape[0]

    def body(q_ref, mk_ref, mv_ref, do_ref, dq_ref, dmk_ref, dmv_ref):
        @pl.when(pl.program_id(0) == 0)
        def _():
            dmk_ref[...] = jnp.zeros_like(dmk_ref)
            dmv_ref[...] = jnp.zeros_like(dmv_ref)

        for h in range(4):
            cols = slice(128 * h, 128 * (h + 1))
            qh, kh, vh, doh = q_ref[:, cols], mk_ref[:, cols], mv_ref[:, cols], do_ref[:, cols]
            sc = _dot_nt(qh, kh) * MEM_SCALE
            p = jnp.exp(sc - jnp.max(sc, axis=-1, keepdims=True))
            p = p / jnp.sum(p, axis=-1, keepdims=True)
            pb = p.astype(BF16)
            dp = _dot_nt(doh, vh)
            ds = (p * (dp - jnp.sum(p * dp, axis=-1, keepdims=True)) * MEM_SCALE).astype(BF16)
            dq_ref[:, cols] = _dot(ds, kh).astype(BF16)
            dmk_ref[:, cols] += _dot_tn(ds, qh)
            dmv_ref[:, cols] += _dot_tn(pb, doh)

    return pl.pallas_call(
        body, grid=(s // t,),
        in_specs=[pl.BlockSpec((t, 512), lambda i: (i, C_MQ // 512)), pl.BlockSpec((nm, 512), lambda i: (0, 0)),
                  pl.BlockSpec((nm, 512), lambda i: (0, 1)), pl.BlockSpec((t, 512), lambda i: (i, 0))],
        out_specs=[pl.BlockSpec((t, 512), lambda i: (i, 0)), pl.BlockSpec((nm, 512), lambda i: (0, 0)),
                   pl.BlockSpec((nm, 512), lambda i: (0, 0))],
        out_shape=[jax.ShapeDtypeStruct((s, 512), BF16), jax.ShapeDtypeStruct((nm, 512), F32),
                   jax.ShapeDtypeStruct((nm, 512), F32)],
        name="mem_attn_bwd", compiler_params=_cp("arbitrary"))(pm, mkv, mkv, do)


def _gain_grad(dxn_g, x, r, name):
    m, d = x.shape

    def body(d_ref, x_ref, r_ref, o_ref):
        o_ref[...] = _fold8(d_ref[...] * (x_ref[...] * r_ref[...]))

    return pl.pallas_call(body, out_shape=jax.ShapeDtypeStruct((8, d), F32), name=name,
                          compiler_params=pltpu.CompilerParams(vmem_limit_bytes=VMEM_LIMIT_BYTES))(dxn_g, x, r)


def _head_norm(o, gh):
    xs, rs = [], []
    for h in range(4):
        oh = o[:, 128 * h:128 * (h + 1)]
        r = lax.rsqrt(jnp.mean(oh * oh, axis=-1, keepdims=True) + EPS)
        xs.append(oh * r)
        rs.append(r)
    return xs, rs


def _merge_fwd(x, pm, o_gla, o_fox_t, o_mem, g_head, wg, wf, wm, wo, g_ffn):
    s = x.shape[0]
    t = min(s, 256)

    def body(x_ref, g0_ref, g1_ref, g2_ref, gg_ref, og_ref, of_ref, om_ref, gh_ref, wg_ref, wf_ref, wm_ref, wo_ref, gf_ref,
             mg_ref, h1_ref, u2_ref, r2_ref):
        xs, _ = _head_norm(og_ref[...], None)
        gg = gg_ref[...].astype(F32)
        sil = gg * _sigmoid(gg)
        ogn = jnp.concatenate(xs, axis=1) * gh_ref[...] * sil
        merged = (_sigmoid(g0_ref[...].astype(F32)) * _dot(ogn.astype(BF16), wg_ref[...])
                  + _sigmoid(g1_ref[...].astype(F32)) * _dot(of_ref[...].T, wf_ref[...])
                  + _sigmoid(g2_ref[...].astype(F32)) * _dot(om_ref[...], wm_ref[...]))
        mb = merged.astype(BF16)
        mg_ref[...] = mb
        h1 = x_ref[...] + _dot(mb, wo_ref[...])
        h1_ref[...] = h1
        r = lax.rsqrt(jnp.mean(h1 * h1, axis=-1, keepdims=True) + EPS)
        u2_ref[...] = ((h1 * r) * gf_ref[...]).astype(BF16)
        r2_ref[...] = r

    row = lambda i: (i, 0)
    const = lambda i: (0, 0)
    return pl.pallas_call(
        body, grid=(s // t,),
        in_specs=[pl.BlockSpec((t, D), row), pl.BlockSpec((t, D), lambda i: (i, 0)), pl.BlockSpec((t, D), lambda i: (i, 1)),
                  pl.BlockSpec((t, D), lambda i: (i, 2)), pl.BlockSpec((t, 512), lambda i: (i, C_GG // 512)),
                  pl.BlockSpec((t, 512), row), pl.BlockSpec((512, t), lambda i: (0, i)), pl.BlockSpec((t, 512), row),
                  pl.BlockSpec((1, 512), const), pl.BlockSpec((512, D), const), pl.BlockSpec((512, D), const),
                  pl.BlockSpec((512, D), const), pl.BlockSpec((D, D), const), pl.BlockSpec((1, D), const)],
        out_specs=[pl.BlockSpec((t, D), row), pl.BlockSpec((t, D), row), pl.BlockSpec((t, D), row), pl.BlockSpec((t, 1), row)],
        out_shape=[jax.ShapeDtypeStruct((s, D), BF16), jax.ShapeDtypeStruct((s, D), F32),
                   jax.ShapeDtypeStruct((s, D), BF16), jax.ShapeDtypeStruct((s, 1), F32)],
        name="merge_fwd", compiler_params=_cp("parallel"))(x, pm, pm, pm, pm, o_gla, o_fox_t, o_mem, g_head, wg, wf, wm, wo, g_ffn)


def _merge_bwd(dh1b, pm, o_gla, o_fox_t, o_mem, g_head, wg, wf, wm, wgt, wft, wmt, wot, spread, d_to_do):
    s = dh1b.shape[0]
    t = min(s, 256)

    def body(dh_ref, g0_ref, g1_ref, g2_ref, gg_ref, og_ref, of_ref, om_ref, gh_ref, wg_ref, wf_ref, wm_ref,
             wgt_ref, wft_ref, wmt_ref, wot_ref, sp_ref, dd_ref,
             dgt_ref, dgg_ref, dog_ref, da_ref, dot_ref, dom_ref, dwg_ref, dwf_ref, dwm_ref, dgh_ref):
        @pl.when(pl.program_id(0) == 0)
        def _():
            dwg_ref[...] = jnp.zeros_like(dwg_ref)
            dwf_ref[...] = jnp.zeros_like(dwf_ref)
            dwm_ref[...] = jnp.zeros_like(dwm_ref)
            dgh_ref[...] = jnp.zeros_like(dgh_ref)

        dmerged = _dot(dh_ref[...], wot_ref[...])
        og = og_ref[...]
        xs, rs = _head_norm(og, None)
        on = jnp.concatenate(xs, axis=1)
        gg = gg_ref[...].astype(F32)
        sg = _sigmoid(gg)
        sil = gg * sg
        gh = gh_ref[...]
        ognb = (on * gh * sil).astype(BF16)
        ofb, omb = of_ref[...].T, om_ref[...]
        douts = []
        for idx, (gref, ob, w_ref, wt_ref, dw_ref) in enumerate((
                (g0_ref, ognb, wg_ref, wgt_ref, dwg_ref), (g1_ref, ofb, wf_ref, wft_ref, dwf_ref),
                (g2_ref, omb, wm_ref, wmt_ref, dwm_ref))):
            gt = _sigmoid(gref[...].astype(F32))
            y = _dot(ob, w_ref[...])
            dgt_ref[:, D * idx:D * (idx + 1)] = (dmerged * y * gt * (1.0 - gt)).astype(BF16)
            dy = (gt * dmerged).astype(BF16)
            dw_ref[...] += _dot_tn(ob, dy)
            douts.append(_dot(dy, wt_ref[...]))
        dogn, dof, dom = douts
        dofb = dof.astype(BF16)
        dom_ref[...] = dom.astype(BF16)
        ind = (_iota((1536, 128), 0) % 512 // 64 == _iota((1536, 128), 1)).astype(BF16)
        delta = _dot(_split3(dofb.astype(F32) * ofb.astype(F32)), ind)
        da_ref[...] = (_dot(dofb, sp_ref[...]) + _dot(_split3(delta), dd_ref[...])).astype(BF16)
        dot_ref[...] = dofb.T
        dgg_ref[...] = (dogn * on * gh * (sg * (1.0 + gg * (1.0 - sg)))).astype(BF16)
        d_on = dogn * sil
        dgh_ref[...] += _fold8(d_on * on)
        dxn = d_on * gh
        outs = []
        for h in range(4):
            cols = slice(128 * h, 128 * (h + 1))
            dh_, xh = dxn[:, cols], xs[h]
            outs.append(rs[h] * (dh_ - xh * jnp.mean(dh_ * xh, axis=-1, keepdims=True)))
        dog_ref[...] = jnp.concatenate(outs, axis=1).astype(BF16)

    row = lambda i: (i, 0)
    const = lambda i: (0, 0)
    return pl.pallas_call(
        body, grid=(s // t,),
        in_specs=[pl.BlockSpec((t, D), row), pl.BlockSpec((t, D), lambda i: (i, 0)), pl.BlockSpec((t, D), lambda i: (i, 1)),
                  pl.BlockSpec((t, D), lambda i: (i, 2)), pl.BlockSpec((t, 512), lambda i: (i, C_GG // 512)),
                  pl.BlockSpec((t, 512), row), pl.BlockSpec((512, t), lambda i: (0, i)), pl.BlockSpec((t, 512), row),
                  pl.BlockSpec((1, 512), const), pl.BlockSpec((512, D), const), pl.BlockSpec((512, D), const),
                  pl.BlockSpec((512, D), const), pl.BlockSpec((D, 512), const), pl.BlockSpec((D, 512), const),
                  pl.BlockSpec((D, 512), const), pl.BlockSpec((D, D), const),
                  pl.BlockSpec((512, 1024), const), pl.BlockSpec((384, 1024), const)],
        out_specs=[pl.BlockSpec((t, 3 * D), row), pl.BlockSpec((t, 512), row), pl.BlockSpec((t, 512), row),
                   pl.BlockSpec((t, 1024), row), pl.BlockSpec((512, t), lambda i: (0, i)), pl.BlockSpec((t, 512), row),
                   pl.BlockSpec((512, D), const), pl.BlockSpec((512, D), const), pl.BlockSpec((512, D), const),
                   pl.BlockSpec((8, 512), const)],
        out_shape=[jax.ShapeDtypeStruct((s, 3 * D), BF16), jax.ShapeDtypeStruct((s, 512), BF16),
                   jax.ShapeDtypeStruct((s, 512), BF16), jax.ShapeDtypeStruct((s, 1024), BF16),
                   jax.ShapeDtypeStruct((512, s), BF16), jax.ShapeDtypeStruct((s, 512), BF16),
                   jax.ShapeDtypeStruct((512, D), F32), jax.ShapeDtypeStruct((512, D), F32),
                   jax.ShapeDtypeStruct((512, D), F32), jax.ShapeDtypeStruct((8, 512), F32)],
        name="merge_bwd", compiler_params=_cp("arbitrary"))(
            dh1b, pm, pm, pm, pm, o_gla, o_fox_t, o_mem, g_head, wg, wf, wm, wgt, wft, wmt, wot, spread, d_to_do)


def _ff2_loss(a, w2, h1, g_final, target):
    s, k = a.shape
    tm = min(s, 256)

    def body(a_ref, w_ref, h1_ref, g_ref, t_ref, dh_ref, dhb_ref, loss_ref, dg_ref):
        @pl.when(pl.program_id(0) == 0)
        def _():
            loss_ref[...] = jnp.zeros_like(loss_ref)
            dg_ref[...] = jnp.zeros_like(dg_ref)

        h2 = h1_ref[...] + _dot(_relu2_bf16(a_ref[...]), w_ref[...])
        r = lax.rsqrt(jnp.mean(h2 * h2, axis=-1, keepdims=True) + EPS)
        xn = h2 * r
        g = g_ref[...]
        err = xn * g - t_ref[...]
        e2 = _fold8(err * err)
        part = e2[:, 0:128]
        for c in range(1, D // 128):
            part = part + e2[:, 128 * c:128 * (c + 1)]
        loss_ref[...] += part
        dy = err * (1.0 / D)
        dg_ref[...] += _fold8(dy * xn)
        dxn = dy * g
        dh = r * (dxn - xn * jnp.mean(dxn * xn, axis=-1, keepdims=True))
        dh_ref[...] = dh
        dhb_ref[...] = dh.astype(BF16)

    row = lambda i: (i, 0)
    const = lambda i: (0, 0)
    return pl.pallas_call(
        body, grid=(s // tm,),
        in_specs=[pl.BlockSpec((tm, k), row), pl.BlockSpec((k, D), const, pipeline_mode=pl.Buffered(1)),
                  pl.BlockSpec((tm, D), row), pl.BlockSpec((1, D), const), pl.BlockSpec((tm, D), row)],
        out_specs=[pl.BlockSpec((tm, D), row), pl.BlockSpec((tm, D), row), pl.BlockSpec((8, 128), const),
                   pl.BlockSpec((8, D), const)],
        out_shape=[jax.ShapeDtypeStruct((s, D), F32), jax.ShapeDtypeStruct((s, D), BF16),
                   jax.ShapeDtypeStruct((8, 128), F32), jax.ShapeDtypeStruct((8, D), F32)],
        name="ff2_loss", compiler_params=_cp("arbitrary"))(a, w2, h1, g_final, target)


def _adam(w, g, m, v, name):
    r, c = w.shape
    tr = r
    for cand in (512, 256, 128, 64, 32, 16, 8):
        if r % cand == 0 and cand * c * 4 <= (1 << 20):
            tr = cand
            break
    c1 = 1.0 - ADAM_B1 ** ADAM_STEP
    c2 = 1.0 - ADAM_B2 ** ADAM_STEP

    def body(w_ref, g_ref, m_ref, v_ref, d_ref, nm_ref, nv_ref):
        gv = g_ref[...]
        nm = ADAM_B1 * m_ref[...] + (1.0 - ADAM_B1) * gv
        nv = ADAM_B2 * v_ref[...] + (1.0 - ADAM_B2) * (gv * gv)
        d_ref[...] = -ADAM_LR * ((nm / c1) / (jnp.sqrt(nv / c2) + ADAM_EPS) + ADAM_WD * w_ref[...])
        nm_ref[...] = nm
        nv_ref[...] = nv

    spec = pl.BlockSpec((tr, c), lambda i: (i, 0))
    return pl.pallas_call(
        body, grid=(r // tr,), in_specs=[spec] * 4, out_specs=[spec] * 3,
        out_shape=[jax.ShapeDtypeStruct((r, c), F32)] * 3, name=name, compiler_params=_cp("parallel"))(w, g, m, v)


def _row_block(r):
    return max(d for d in range(16, 513, 16) if r % d == 0)


def _add2(a, b, name):
    n, r, c = a.shape
    tr = _row_block(r)

    def body(a_ref, b_ref, o_ref):
        o_ref[...] = (a_ref[...].astype(F32) + b_ref[...].astype(F32)).astype(BF16)

    spec = pl.BlockSpec((1, tr, c), lambda k, i: (k, i, 0))
    return pl.pallas_call(body, grid=(n, r // tr), in_specs=[spec, spec], out_specs=spec,
                          out_shape=jax.ShapeDtypeStruct((n, r, c), BF16), name=name,
                          compiler_params=_cp("parallel", "parallel"))(a, b)


def _sum4(a, name):
    _, r, c = a.shape
    tr = _row_block(r)

    def body(a_ref, o_ref):
        o_ref[...] = ((a_ref[0].astype(F32) + a_ref[1].astype(F32)) + a_ref[2].astype(F32)) + a_ref[3].astype(F32)

    return pl.pallas_call(body, grid=(r // tr,), in_specs=[pl.BlockSpec((4, tr, c), lambda i: (0, i, 0))],
                          out_specs=pl.BlockSpec((tr, c), lambda i: (i, 0)),
                          out_shape=jax.ShapeDtypeStruct((r, c), F32), name=name, compiler_params=_cp("parallel"))(a)


def _adam_small(w, gathered, m, v):
    c1 = 1.0 - ADAM_B1 ** ADAM_STEP
    c2 = 1.0 - ADAM_B2 ** ADAM_STEP

    def body(w_ref, g_ref, m_ref, v_ref, gs_ref, d_ref, nm_ref, nv_ref):
        gv = g_ref[0]
        for dev in range(1, N_DEV):
            gv = gv + g_ref[dev]
        gs_ref[...] = gv
        nm = ADAM_B1 * m_ref[...] + (1.0 - ADAM_B1) * gv
        nv = ADAM_B2 * v_ref[...] + (1.0 - ADAM_B2) * (gv * gv)
        d_ref[...] = -ADAM_LR * ((nm / c1) / (jnp.sqrt(nv / c2) + ADAM_EPS) + ADAM_WD * w_ref[...])
        nm_ref[...] = nm
        nv_ref[...] = nv

    return pl.pallas_call(body, out_shape=[jax.ShapeDtypeStruct((8, D), F32)] * 4, name="adam_small")(w, gathered, m, v)


def _place():
    return lax.axis_index("x"), lax.axis_index("y"), lax.axis_index("c")


def _other_chips(x, y):
    return [(1 - x, y), (x, 1 - y), (1 - x, 1 - y)]


def _gather_shards(p):
    def body(p_ref, out_ref, *sems):
        start, finish = _gather_ops((p_ref,), (out_ref,), sems)
        start()
        finish()

    return pl.pallas_call(
        body, out_shape=jax.ShapeDtypeStruct((N_CHIPS,) + p.shape, p.dtype), in_specs=[ANY_SPEC], out_specs=ANY_SPEC,
        scratch_shapes=GATHER_SEMS, name="gather_shards")(p)


GATHER_SEMS = [pltpu.SemaphoreType.DMA((6,)), pltpu.SemaphoreType.DMA((6,)), pltpu.SemaphoreType.DMA]


def _gather_ops(in_refs, out_refs, sems):
    (p_ref,), (out_ref,) = in_refs, out_refs
    send_sems, recv_sems, local_sem = sems
    hr = p_ref.shape[0] // 2
    x, y, cc = _place()
    sibling = (x, y, 1 - cc)
    chips = _other_chips(x, y)

    def half(chip, core):
        return out_ref.at[2 * chip[0] + chip[1], pl.ds(core * hr, hr), :]

    def copy(k, chip, core, to, src=None):
        return pltpu.make_async_remote_copy(
            src_ref=half(chip, core) if src is None else src, dst_ref=half(chip, core),
            send_sem=send_sems.at[k], recv_sem=recv_sems.at[k], device_id=to, device_id_type=MESH)

    mine = pltpu.make_async_copy(p_ref, out_ref.at[2 * x + y], local_sem)
    my_half = p_ref.at[pl.ds(cc * hr, hr), :]
    first = [copy(j, (x, y), cc, (*chip, cc), src=my_half) for j, chip in enumerate(chips)]
    passed = [copy(3 + j, chip, cc, sibling) for j, chip in enumerate(chips)]

    def start():
        mine.start()
        for cp in first:
            cp.start()

    def finish():
        for j, chip in enumerate(chips):
            copy(j, chip, cc, (x, y, cc)).wait_recv()
            passed[j].start()
        for j, chip in enumerate(chips):
            copy(3 + j, chip, 1 - cc, (x, y, cc)).wait_recv()
        for cp in first + passed:
            cp.wait_send()
        mine.wait()

    return start, finish


def _gather_side(p):
    return _Side([p], [jax.ShapeDtypeStruct((N_CHIPS,) + p.shape, p.dtype)], GATHER_SEMS, _gather_ops)


def _swap_halves(g):
    n, r, c = g.shape
    hr = r // 2

    def body(g_ref, out_ref, send_sem, recv_sem):
        x, y, cc = _place()
        cp = pltpu.make_async_remote_copy(
            src_ref=g_ref.at[:, pl.ds((1 - cc) * hr, hr), :], dst_ref=out_ref,
            send_sem=send_sem, recv_sem=recv_sem, device_id=(x, y, 1 - cc), device_id_type=MESH)
        cp.start()
        cp.wait()

    any_spec = pl.BlockSpec(memory_space=pl.ANY)
    return pl.pallas_call(
        body, out_shape=jax.ShapeDtypeStruct((n, hr, c), g.dtype), in_specs=[any_spec], out_specs=any_spec,
        scratch_shapes=[pltpu.SemaphoreType.DMA, pltpu.SemaphoreType.DMA], name="swap_halves")(g)


SCATTER_SEMS = [pltpu.SemaphoreType.DMA((7,)), pltpu.SemaphoreType.DMA((7,)), pltpu.SemaphoreType.DMA]


def _scatter_ops(in_refs, out_refs, sems):
    (p_ref,), (out_ref,) = in_refs, out_refs
    send_sems, recv_sems, local_sem = sems
    hr = p_ref.shape[1]
    x, y, cc = _place()
    me = 2 * x + y
    sibling = (x, y, 1 - cc)
    chips = _other_chips(x, y)
    ids = [2 * chip[0] + chip[1] for chip in chips]

    def land(src, core):
        return out_ref.at[src, pl.ds(core * hr, hr), :]

    def copy(k, src_ref, dst_ref, to):
        return pltpu.make_async_remote_copy(src_ref=src_ref, dst_ref=dst_ref, send_sem=send_sems.at[k],
                                            recv_sem=recv_sems.at[k], device_id=to, device_id_type=MESH)

    mine = pltpu.make_async_copy(p_ref.at[me], land(me, cc), local_sem)
    sends = [copy(j, p_ref.at[ids[j]], land(me, cc), (*chip, cc)) for j, chip in enumerate(chips)]
    sends.append(copy(3, p_ref.at[me], land(me, cc), sibling))
    passed = [copy(4 + j, land(ids[j], cc), land(ids[j], cc), sibling) for j in range(3)]

    def start():
        mine.start()
        for cp in sends:
            cp.start()

    def finish():
        for j in range(3):
            copy(j, p_ref.at[me], land(ids[j], cc), (x, y, cc)).wait_recv()
            passed[j].start()
        copy(3, p_ref.at[me], land(me, 1 - cc), (x, y, cc)).wait_recv()
        for j in range(3):
            copy(4 + j, p_ref.at[me], land(ids[j], 1 - cc), (x, y, cc)).wait_recv()
        for cp in sends + passed:
            cp.wait_send()
        mine.wait()

    return start, finish


def _scatter_side(p):
    n, hr, c = p.shape
    return _Side([p], [jax.ShapeDtypeStruct((n, 2 * hr, c), p.dtype)], SCATTER_SEMS, _scatter_ops)


def _gather_small(blk):
    m, n = blk.shape

    def body(x_ref, out_ref, send_sems, recv_sems, local_sem):
        x, y, cc = _place()
        me, sibling = (x, y, cc), (x, y, 1 - cc)
        chips = _other_chips(x, y)

        def slot(px, py, pc):
            return out_ref.at[4 * px + 2 * py + pc]

        def copy(k, block, to, src=None):
            return pltpu.make_async_remote_copy(
                src_ref=slot(*block) if src is None else src, dst_ref=slot(*block),
                send_sem=send_sems.at[k], recv_sem=recv_sems.at[k], device_id=to, device_id_type=MESH)

        mine = pltpu.make_async_copy(x_ref, slot(*me), local_sem)
        mine.start()
        first = [copy(0, me, sibling, src=x_ref)]
        first += [copy(1 + j, me, (*chip, cc), src=x_ref) for j, chip in enumerate(chips)]
        for cp in first:
            cp.start()
        passed = [copy(4 + j, (*chip, cc), sibling) for j, chip in enumerate(chips)]
        for j, chip in enumerate(chips):
            copy(1 + j, (*chip, cc), me).wait_recv()
            passed[j].start()
        copy(0, sibling, me).wait_recv()
        for j, chip in enumerate(chips):
            copy(4 + j, (*chip, 1 - cc), me).wait_recv()
        for cp in first + passed:
            cp.wait_send()
        mine.wait()

    vmem = pl.BlockSpec(memory_space=pltpu.VMEM)
    return pl.pallas_call(
        body, out_shape=jax.ShapeDtypeStruct((N_DEV, m, n), blk.dtype), in_specs=[vmem], out_specs=vmem,
        scratch_shapes=[pltpu.SemaphoreType.DMA((7,)), pltpu.SemaphoreType.DMA((7,)), pltpu.SemaphoreType.DMA],
        name="gather_small")(blk)


def _shard_shape(name, full_shape):
    shp = list(full_shape)
    shp[SHARD_AXIS[name]] //= N_CHIPS
    return tuple(shp)


FULL_SHAPES = {"w_in": (D, 6680), "w_alpha_up": (16, 256), "w_mem_kv": (D, D), "w_gla_o": (512, D), "w_fox_o": (512, D),
               "w_mem_o": (512, D), "w_out": (D, D), "w_ff1": (D, 4 * D), "w_ff2": (4 * D, D)}


def _pack_a(sh, dtype):
    w = sh["w_in"].astype(dtype)
    return jnp.concatenate([w[:, 0:PACK_W], jnp.pad(w[:, PACK_W:], ((0, 0), (0, 2 * PACK_W - w.shape[1])))], axis=0)


def _pack_b(sh, dtype):
    o3 = jnp.concatenate([sh["w_gla_o"], sh["w_fox_o"], sh["w_mem_o"], jnp.zeros((512, 256), sh["w_gla_o"].dtype)], axis=1)
    au = jnp.pad(sh["w_alpha_up"], ((0, PACK_ROWS_B - 3072 - 16), (0, PACK_W - 64)))
    return jnp.concatenate([sh["w_ff1"], sh["w_ff2"], sh["w_mem_kv"], sh["w_out"], o3, au], axis=0).astype(dtype)


def _unpack_a(pa):
    return {"w_in": jnp.concatenate([pa[0:1024], pa[1024:2048, 0:1670 - PACK_W]], axis=1)}


def _unpack_b(pb):
    return {"w_ff1": pb[0:1024], "w_ff2": pb[1024:2048], "w_mem_kv": pb[2048:2304], "w_out": pb[2304:2560],
            "w_gla_o": pb[2560:3072, 0:256], "w_fox_o": pb[2560:3072, 256:512], "w_mem_o": pb[2560:3072, 512:768],
            "w_alpha_up": pb[3072:3088, 0:64]}


def _unpack(packed):
    return {**_unpack_a(packed[0:PACK_ROWS_A]), **_unpack_b(packed[PACK_ROWS_A:])}


def _split_shards(name, full):
    return jnp.split(full, N_CHIPS, axis=SHARD_AXIS[name])


def _pack_small(vals, scalar=None):
    row4 = jnp.concatenate([vals["b_alpha"].reshape(-1), vals["b_forget"].reshape(-1), jnp.zeros((D - 264,), F32)])
    row5 = jnp.concatenate([vals["g_gla_head"].reshape(-1), jnp.zeros((D - 512,), F32)])
    row6 = jnp.zeros((D,), F32) if scalar is None else jnp.broadcast_to(scalar, (D,))
    rows = [vals["g_mix"].reshape(-1), vals["g_mem"].reshape(-1), vals["g_ffn"].reshape(-1), vals["g_final"].reshape(-1),
            row4, row5, row6, jnp.zeros((D,), F32)]
    return jnp.stack(rows)


def _unpack_small(blk):
    return {"g_mix": blk[0].reshape(1, D), "g_mem": blk[1].reshape(1, D), "g_ffn": blk[2].reshape(1, D),
            "g_final": blk[3].reshape(D), "b_alpha": blk[4, 0:256].reshape(1, 256), "b_forget": blk[4, 256:264].reshape(1, 8),
            "g_gla_head": blk[5, 0:512].reshape(1, 4, 128)}


def _local_step(x, mem, target, wb, small, exchange=None):
    s = x.shape[0]
    nm = mem.shape[0]
    t = _row_tile(s)
    nb = s // t
    w_in = wb["w_in"]
    w_main = jnp.concatenate([w_in[:, 3608:6680], w_in[:, 0:1536], w_in[:, 1552:3088], w_in[:, 3096:3608]], axis=1)
    w_e = jnp.concatenate([w_in[:, 1536:1552], w_in[:, 3088:3096], jnp.zeros((D, PE_W - 24), BF16)], axis=1)
    w_in_pt = jnp.concatenate([w_main, w_e, jnp.zeros((D, DP_W - PM_W - PE_W), BF16)], axis=1).T
    b_alpha = small["b_alpha"].reshape(1, 256)
    bias_e = jnp.concatenate([jnp.zeros((FF_LANE,), F32), small["b_forget"].reshape(-1),
                              jnp.zeros((PE_W - FF_LANE - 8,), F32)]).reshape(1, PE_W)
    g_mix, g_mem, g_ffn = small["g_mix"].reshape(1, D), small["g_mem"].reshape(1, D), small["g_ffn"].reshape(1, D)
    g_final = small["g_final"].reshape(1, D)
    g_head = small["g_gla_head"].reshape(1, 512)

    u, r1 = _rms_fwd(x, g_mix, "norm_mix")
    big = min(s, 1024)
    if exchange is None:
        pm = _mm_nn(u, w_main, out_dtype=BF16, tm=big, tn=PM_W // 4, tk=D, name="proj_main")
    else:
        pm, gathered = _mm_nn(u, w_main, out_dtype=BF16, tm=big, tn=PM_W // 4, tk=D, name="proj_main", side=exchange.gather)
        wb = {**wb, **exchange.weights(gathered)}
    wau_p = jnp.concatenate([wb["w_alpha_up"], jnp.zeros((PE_W - 16, 256), BF16)], axis=0)
    pe = _mm_nn(u, w_e, out_dtype=F32, tm=t, tn=PE_W, tk=D, name="proj_narrow")
    o_gla, states = _gla_fwd(pm, pe, wau_p, b_alpha)
    fcum = _fcum_fwd(pe, bias_e)
    tb = _fox_tables()
    qf_aug, k_aug, v_aug, vt, qt, kt = _fox_prep(pm, fcum, None, tb, backward=False)
    o_fox, lse = _fox_fwd(k_aug, qf_aug, vt)
    mn, rm = _rms_fwd(mem, g_mem, "norm_mem")
    mkv = _mm_nn(mn, wb["w_mem_kv"], out_dtype=BF16, tm=nm, tn=512, tk=D, name="mem_kv")
    o_mem = _mem_attn_fwd(pm, mkv)
    merged, h1, u2, r2 = _merge_fwd(x, pm, o_gla, o_fox, o_mem, g_head, wb["w_gla_o"], wb["w_fox_o"], wb["w_mem_o"],
                                    wb["w_out"], g_ffn)
    a = _mm_nn(u2, wb["w_ff1"], out_dtype=BF16, tm=big, tn=1024, tk=D, name="ff1")
    dh2, dh2b, loss8, dgfin8 = _ff2_loss(a, wb["w_ff2"], h1, g_final, target)
    loss = 0.5 * jnp.sum(loss8) / D

    da = _mm_nn(dh2b, wb["w_ff2"].T, out_dtype=BF16, tm=t, tn=1024, tk=D, name="d_act",
                epi=lambda acc, at: acc * (2.0 * jnp.maximum(at.astype(F32), 0.0)), extra=a)
    gw = {}
    gw["w_ff2"] = _mm_tn(a, dh2b, tm=1024, tn=D, ts=t, name="dw_ff2", a_fn=_relu2_bf16)
    gw["w_ff1"] = _mm_tn(u2, da, tm=D, tn=1024, ts=t, name="dw_ff1")
    dh1, dh1b, dgffn8 = _mm_norm_bwd(da, wb["w_ff1"].T, h1, r2, g_ffn, dh2, name="d_h1", want_bf16=True)
    gw["w_out"] = _mm_tn(merged, dh1b, tm=D, tn=D, ts=t, name="dw_out")
    (dgates, dgg, do_gla, do_aug, do_t, do_mem, gw["w_gla_o"], gw["w_fox_o"], gw["w_mem_o"], dgh8) = _merge_bwd(
        dh1b, pm, o_gla, o_fox, o_mem, g_head, wb["w_gla_o"], wb["w_fox_o"], wb["w_mem_o"],
        wb["w_gla_o"].T, wb["w_fox_o"].T, wb["w_mem_o"].T, wb["w_out"].T, tb["spread"], tb["d_to_do"])
    dgq, dgk, dgv, de_gla, dwau_p, dba8 = _gla_bwd(pm, pe, wau_p, wau_p.T, b_alpha, do_gla, states)
    gw["w_alpha_up"] = dwau_p[0:16, :]
    q_aug = _fox_prep(pm, fcum, lse.reshape(8, s), tb, backward=True)
    dfq_t, dfrow, dfk_t, dfv_t, dfcol = _fox_bwd(q_aug, do_aug, qt, do_t, k_aug, v_aug, kt)
    dfq, dfk, dfv, df = _fox_post(dfq_t, dfk_t, dfv_t, dfrow[:, 0, :], dfcol.reshape(8, s), tb)
    de_fox, dbf8 = _fcum_bwd(pe, bias_e, df)
    dmq, dmk, dmv = _mem_attn_bwd(pm, mkv, do_mem)
    dmkv = jnp.concatenate([dmk, dmv], axis=1).astype(BF16)
    gw["w_mem_kv"] = _mm_tn(mn, dmkv, tm=D, tn=D, ts=nm, name="dw_mem_kv")
    dmn_g = _mm_nn(dmkv, wb["w_mem_kv"].T, out_dtype=F32, tm=nm, tn=D, tk=D, name="d_mem_norm")
    dgmem8 = _gain_grad(dmn_g, mem, rm, "dg_mem")
    dproj = jnp.concatenate(
        [dgates, dgq, dgk, dgv, dgg, dfq, dfk, dfv, dmq,
         (de_gla + de_fox).astype(BF16), jnp.zeros((s, DP_W - PM_W - PE_W), BF16)], axis=1)
    dwp = _mm_tn(u, dproj, tm=D, tn=1024, ts=t, name="dw_in")
    gw["w_in"] = jnp.concatenate([dwp[:, 3072:4608], dwp[:, PM_W:PM_W + 16], dwp[:, 4608:6144],
                                  dwp[:, PM_W + 16:PM_W + 24], dwp[:, 6144:6656], dwp[:, 0:3072]], axis=1)
    if exchange is None:
        grad_x, dgmix8 = _mm_norm_bwd(dproj, w_in_pt, x, r1, g_mix, dh1, name="d_x", want_bf16=False)
        exchanged = None
    else:
        grad_x, dgmix8, exchanged = _mm_norm_bwd(dproj, w_in_pt, x, r1, g_mix, dh1, name="d_x", want_bf16=False,
                                                 side=exchange.scatter(gw))
    gs = {"g_mix": dgmix8.sum(0), "g_mem": dgmem8.sum(0), "g_ffn": dgffn8.sum(0), "g_final": dgfin8.sum(0),
          "b_alpha": dba8.sum(0), "b_forget": dbf8.sum(0)[FF_LANE:FF_LANE + 8], "g_gla_head": dgh8.sum(0)}
    return loss, grad_x, gw, gs, exchanged


def kernel(x, mem, g_mix, w_in, w_alpha_up, b_alpha, b_forget, g_gla_head, g_mem, w_mem_kv, w_gla_o, w_fox_o, w_mem_o, w_out, g_ffn, w_ff1, w_ff2, g_final, loss_target, m_g_mix, m_w_in, m_w_alpha_up, m_b_alpha, m_b_forget, m_g_gla_head, m_g_mem, m_w_mem_kv, m_w_gla_o, m_w_fox_o, m_w_mem_o, m_w_out, m_g_ffn, m_w_ff1, m_w_ff2, m_g_final, v_g_mix, v_w_in, v_w_alpha_up, v_b_alpha, v_b_forget, v_g_gla_head, v_g_mem, v_w_mem_kv, v_w_gla_o, v_w_fox_o, v_w_mem_o, v_w_out, v_g_ffn, v_w_ff1, v_w_ff2, v_g_final):
    args = dict(locals())
    w_sh = {n: args[n][0] for n in WEIGHTS}
    m_sh = {n: args["m_" + n][0] for n in WEIGHTS}
    v_sh = {n: args["v_" + n][0] for n in WEIGHTS}
    small = {n: args[n] for n in SMALL}

    def whole(parts):
        return {n: jnp.concatenate([p[n] for p in parts], axis=SHARD_AXIS[n]) for n in parts[0]}

    class Exchange:
        gather = _gather_side(_pack_b(w_sh, BF16))

        @staticmethod
        def weights(gathered):
            return whole([_unpack_b(gathered[k]) for k in range(N_CHIPS)])

        @staticmethod
        def scatter(gw):
            by_chip = {n: _split_shards(n, gw[n]) for n in WEIGHTS}
            packed = jnp.stack([jnp.concatenate([_pack_a({n: by_chip[n][k] for n in WEIGHTS}, BF16),
                                                 _pack_b({n: by_chip[n][k] for n in WEIGHTS}, BF16)], axis=0)
                                for k in range(N_CHIPS)])
            hr = PACK_ROWS // 2
            mine = lax.dynamic_slice_in_dim(packed, lax.axis_index("c") * hr, hr, axis=1)
            return _scatter_side(_add2(mine, _swap_halves(packed), "chip_sum"))

    gathered_a = _gather_shards(_pack_a(w_sh, BF16))
    wb = whole([_unpack_a(gathered_a[k]) for k in range(N_CHIPS)])
    loss, grad_x, gw, gs, by_chip = _local_step(x[0], mem[0], loss_target[0], wb, small, Exchange)
    g_out = _unpack(_sum4(by_chip, "shard_sum"))
    d_out, m_out, v_out = {}, {}, {}
    for n in WEIGHTS:
        d_out[n], m_out[n], v_out[n] = _adam(w_sh[n], g_out[n], m_sh[n], v_sh[n], "adam_" + n)

    small_all = _gather_small(_pack_small(gs, loss))
    sm = {n: args["m_" + n] for n in SMALL}
    sv = {n: args["v_" + n] for n in SMALL}
    gs_sum, sd, snm, snv = _adam_small(_pack_small(small), small_all, _pack_small(sm), _pack_small(sv))
    gs_o, sd_o, snm_o, snv_o = _unpack_small(gs_sum), _unpack_small(sd), _unpack_small(snm), _unpack_small(snv)

    names = ["g_mix", "w_in", "w_alpha_up", "b_alpha", "b_forget", "g_gla_head", "g_mem", "w_mem_kv", "w_gla_o", "w_fox_o",
             "w_mem_o", "w_out", "g_ffn", "w_ff1", "w_ff2", "g_final"]

    def pick(big, sml, n):
        return big[n][None] if n in big else sml[n]

    outs = [gs_sum[6, 0], grad_x[None]]
    for big, sml in ((g_out, gs_o), (d_out, sd_o), (m_out, snm_o), (v_out, snv_o)):
        outs += [pick(big, sml, n) for n in names]
    return tuple(outs)
```

```python
import functools

import numpy as np
import jax
import jax.numpy as jnp
from jax import lax
from jax.experimental import pallas as pl
from jax.experimental.pallas import tpu as pltpu

F32 = jnp.float32
BF16 = jnp.bfloat16
HI = lax.Precision.HIGHEST
MESH = pl.DeviceIdType.MESH

EPS = 1e-6
D = 1024
CHUNK = 64
GLA_TAU = 16.0
N_CHIPS = 4
N_DEV = 8
VMEM_LIMIT_BYTES = 56 * 1024 * 1024

ADAM_LR, ADAM_B1, ADAM_B2, ADAM_EPS, ADAM_WD, ADAM_STEP = 0.001, 0.9, 0.999, 1e-08, 0.01, 10

PM_W = 6656
PE_W = 128
C_GQ, C_GK, C_GV, C_GG, C_FQ, C_FK, C_FV, C_MQ = 3072, 3328, 3584, 4096, 4608, 5120, 5632, 6144
FF_LANE = 16

WEIGHTS = ("w_in", "w_alpha_up", "w_mem_kv", "w_gla_o", "w_fox_o", "w_mem_o", "w_out", "w_ff1", "w_ff2")
SHARD_AXIS = {"w_in": 1, "w_alpha_up": 1, "w_mem_kv": 0, "w_gla_o": 1, "w_fox_o": 1, "w_mem_o": 1, "w_out": 0,
              "w_ff1": 1, "w_ff2": 0}
SMALL = ("g_mix", "g_mem", "g_ffn", "g_final", "b_alpha", "b_forget", "g_gla_head")
PACK_W = 1024
PACK_ROWS_A = 2048
PACK_ROWS_B = 3104
PACK_ROWS = PACK_ROWS_A + PACK_ROWS_B


def _cp(*sem):
    return pltpu.CompilerParams(dimension_semantics=sem, vmem_limit_bytes=VMEM_LIMIT_BYTES)


def _dot(a, b, **kw):
    return jnp.dot(a, b, preferred_element_type=F32, **kw)


def _dot_nt(a, b, **kw):
    return lax.dot_general(a, b, (((1,), (1,)), ((), ())), preferred_element_type=F32, **kw)


def _dot_tn(a, b, **kw):
    return lax.dot_general(a, b, (((0,), (0,)), ((), ())), preferred_element_type=F32, **kw)


def _sigmoid(x):
    return 1.0 / (1.0 + jnp.exp(-x))


def _log_sigmoid(x):
    return -(jnp.maximum(-x, 0.0) + jnp.log1p(jnp.exp(-jnp.abs(x))))


def _fold8(x):
    m, n = x.shape
    return x.reshape(m // 8, 8, n).sum(axis=0)


def _iota(shape, dim):
    return lax.broadcasted_iota(jnp.int32, shape, dim)


def _row_tile(s):
    return min(s, 512)


class _Side:
    def __init__(self, inputs, out_shape, scratch, ops):
        self.inputs, self.out_shape, self.scratch, self.ops = list(inputs), list(out_shape), list(scratch), ops


ANY_SPEC = pl.BlockSpec(memory_space=pl.ANY)


def _mm_nn(a, b, *, out_dtype, tm, tn, tk, name, a_fn=None, epi=None, extra=None, side=None):
    m, k = a.shape
    _, n = b.shape
    nk = k // tk
    n_in = 2 + (extra is not None)
    n_sin = 0 if side is None else len(side.inputs)
    n_sout = 0 if side is None else len(side.out_shape)

    def body_one(*refs):
        a_ref, b_ref = refs[0], refs[1]
        o_ref = refs[n_in + n_sin]
        if side is not None:
            start, finish = side.ops(refs[n_in:n_in + n_sin], refs[n_in + n_sin + 1:n_in + n_sin + 1 + n_sout],
                                     refs[n_in + n_sin + 1 + n_sout:])
            pl.when((pl.program_id(0) == 0) & (pl.program_id(1) == 0))(start)
        at = a_ref[...] if a_fn is None else a_fn(a_ref[...])
        r = _dot(at, b_ref[...])
        if epi is not None:
            r = epi(r, None if extra is None else refs[2][...])
        o_ref[...] = r.astype(out_dtype)
        if side is not None:
            pl.when((pl.program_id(0) == m // tm - 1) & (pl.program_id(1) == n // tn - 1))(finish)

    if nk == 1:
        in_specs = [pl.BlockSpec((tm, k), lambda i, j: (i, 0)), pl.BlockSpec((k, tn), lambda i, j: (0, j))]
        args = [a, b]
        if extra is not None:
            in_specs.append(pl.BlockSpec((tm, tn), lambda i, j: (i, j)))
            args.append(extra)
        out_specs = pl.BlockSpec((tm, tn), lambda i, j: (i, j))
        out_shape = jax.ShapeDtypeStruct((m, n), out_dtype)
        if side is None:
            return pl.pallas_call(
                body_one, grid=(m // tm, n // tn), in_specs=in_specs, out_specs=out_specs, out_shape=out_shape,
                name=name, compiler_params=_cp("parallel", "parallel"))(*args)
        return pl.pallas_call(
            body_one, grid=(m // tm, n // tn), in_specs=in_specs + [ANY_SPEC] * n_sin,
            out_specs=[out_specs] + [ANY_SPEC] * n_sout, out_shape=[out_shape] + side.out_shape,
            scratch_shapes=side.scratch, name=name, compiler_params=_cp("arbitrary", "arbitrary"))(*args, *side.inputs)
    assert side is None

    def body(*refs):
        if extra is None:
            a_ref, b_ref, o_ref, acc = refs
            x_ref = None
        else:
            a_ref, b_ref, x_ref, o_ref, acc = refs
        kk = pl.program_id(2)

        @pl.when(kk == 0)
        def _():
            acc[...] = jnp.zeros_like(acc)

        at = a_ref[...]
        if a_fn is not None:
            at = a_fn(at)
        acc[...] += _dot(at, b_ref[...])

        @pl.when(kk == nk - 1)
        def _():
            r = acc[...]
            if epi is not None:
                r = epi(r, None if x_ref is None else x_ref[...])
            o_ref[...] = r.astype(out_dtype)

    in_specs = [pl.BlockSpec((tm, tk), lambda i, j, kk: (i, kk)), pl.BlockSpec((tk, tn), lambda i, j, kk: (kk, j))]
    args = [a, b]
    if extra is not None:
        in_specs.append(pl.BlockSpec((tm, tn), lambda i, j, kk: (i, j)))
        args.append(extra)
    return pl.pallas_call(
        body, grid=(m // tm, n // tn, nk), in_specs=in_specs,
        out_specs=pl.BlockSpec((tm, tn), lambda i, j, kk: (i, j)),
        out_shape=jax.ShapeDtypeStruct((m, n), out_dtype),
        scratch_shapes=[pltpu.VMEM((tm, tn), F32)], name=name,
        compiler_params=_cp("parallel", "parallel", "arbitrary"))(*args)


def _mm_tn(a, b, *, tm, tn, ts, name, a_fn=None):
    s, m = a.shape
    _, n = b.shape
    ns = s // ts

    def body(a_ref, b_ref, o_ref, acc):
        kk = pl.program_id(2)

        @pl.when(kk == 0)
        def _():
            acc[...] = jnp.zeros_like(acc)

        at = a_ref[...]
        if a_fn is not None:
            at = a_fn(at)
        acc[...] += _dot_tn(at, b_ref[...])

        @pl.when(kk == ns - 1)
        def _():
            o_ref[...] = acc[...]

    return pl.pallas_call(
        body, grid=(m // tm, n // tn, ns),
        in_specs=[pl.BlockSpec((ts, tm), lambda i, j, kk: (kk, i)), pl.BlockSpec((ts, tn), lambda i, j, kk: (kk, j))],
        out_specs=pl.BlockSpec((tm, tn), lambda i, j, kk: (i, j)),
        out_shape=jax.ShapeDtypeStruct((m, n), F32),
        scratch_shapes=[pltpu.VMEM((tm, tn), F32)], name=name,
        compiler_params=_cp("parallel", "parallel", "arbitrary"))(a, b)


def _mm_tn_cat(a, bs, *, ts, name):
    s, m = a.shape
    n = sum(b.shape[1] for b in bs)
    ns = s // ts
    nb = len(bs)

    def body(*refs):
        a_ref, b_refs, o_ref, acc = refs[0], refs[1:1 + nb], refs[1 + nb], refs[2 + nb]
        kk = pl.program_id(0)

        @pl.when(kk == 0)
        def _():
            acc[...] = jnp.zeros_like(acc)

        bt = b_refs[0][...] if nb == 1 else jnp.concatenate([r[...] for r in b_refs], axis=1)
        acc[...] += _dot_tn(a_ref[...], bt)

        @pl.when(kk == ns - 1)
        def _():
            o_ref[...] = acc[...]

    return pl.pallas_call(
        body, grid=(ns,),
        in_specs=[pl.BlockSpec((ts, m), lambda kk: (kk, 0))] + [pl.BlockSpec((ts, b.shape[1]), lambda kk: (kk, 0)) for b in bs],
        out_specs=pl.BlockSpec((m, n), lambda kk: (0, 0)), out_shape=jax.ShapeDtypeStruct((m, n), F32),
        scratch_shapes=[pltpu.VMEM((m, n), F32)], name=name, compiler_params=_cp("arbitrary"))(a, *bs)


def _relu2_bf16(t):
    r = jnp.maximum(t.astype(F32), 0.0)
    return (r * r).astype(BF16)


def _rms_fwd(x, g, name):
    s, d = x.shape
    tm = min(s, 512)

    def body(x_ref, g_ref, u_ref, r_ref):
        xv = x_ref[...]
        r = lax.rsqrt(jnp.mean(xv * xv, axis=-1, keepdims=True) + EPS)
        u_ref[...] = ((xv * r) * g_ref[...]).astype(BF16)
        r_ref[...] = r

    return pl.pallas_call(
        body, grid=(s // tm,),
        in_specs=[pl.BlockSpec((tm, d), lambda i: (i, 0)), pl.BlockSpec((1, d), lambda i: (0, 0))],
        out_specs=[pl.BlockSpec((tm, d), lambda i: (i, 0)), pl.BlockSpec((tm, 1), lambda i: (i, 0))],
        out_shape=[jax.ShapeDtypeStruct((s, d), BF16), jax.ShapeDtypeStruct((s, 1), F32)],
        name=name, compiler_params=_cp("parallel"))(x, g)


def _mm_norm_bwd(a_parts, b, xin, r, g, dres, *, name, want_bf16, side=None):
    s = a_parts[0].shape[0]
    k = b.shape[0]
    na = len(a_parts)
    offs = [sum(p.shape[1] for p in a_parts[:i]) for i in range(na)]
    assert offs[-1] + a_parts[-1].shape[1] == k
    tm = min(s, 256)
    n_out = 3 if want_bf16 else 2
    n_sin = 0 if side is None else len(side.inputs)
    n_sout = 0 if side is None else len(side.out_shape)

    def body(*refs):
        a_refs = refs[:na]
        b_ref, x_ref, r_ref, g_ref, dres_ref = refs[na:na + 5]
        rest = refs[na + 5:]
        outs = rest[n_sin:n_sin + n_out]
        dx_ref, dg_ref = outs[0], outs[-1]
        if side is not None:
            start, finish = side.ops(rest[:n_sin], rest[n_sin + n_out:n_sin + n_out + n_sout], rest[n_sin + n_out + n_sout:])
            pl.when(pl.program_id(0) == 0)(start)

        @pl.when(pl.program_id(0) == 0)
        def _():
            dg_ref[...] = jnp.zeros_like(dg_ref)

        du = _dot(a_refs[0][...], b_ref[0:a_parts[0].shape[1], :])
        for a_ref, off, part in zip(a_refs[1:], offs[1:], a_parts[1:]):
            du = du + _dot(a_ref[...], b_ref[off:off + part.shape[1], :])
        xn = x_ref[...] * r_ref[...]
        dg_ref[...] += _fold8(du * xn)
        dxn = du * g_ref[...]
        dx = dres_ref[...] + r_ref[...] * (dxn - xn * jnp.mean(dxn * xn, axis=-1, keepdims=True))
        dx_ref[...] = dx
        if want_bf16:
            outs[1][...] = dx.astype(BF16)
        if side is not None:
            pl.when(pl.program_id(0) == s // tm - 1)(finish)

    row = lambda i: (i, 0)
    const = lambda i: (0, 0)
    out_specs = [pl.BlockSpec((tm, D), row)]
    out_shape = [jax.ShapeDtypeStruct((s, D), F32)]
    if want_bf16:
        out_specs.append(pl.BlockSpec((tm, D), row))
        out_shape.append(jax.ShapeDtypeStruct((s, D), BF16))
    out_specs.append(pl.BlockSpec((8, D), const))
    out_shape.append(jax.ShapeDtypeStruct((8, D), F32))
    side_in = [] if side is None else side.inputs
    return pl.pallas_call(
        body, grid=(s // tm,),
        in_specs=[pl.BlockSpec((tm, p.shape[1]), row) for p in a_parts]
        + [pl.BlockSpec((k, D), const, pipeline_mode=pl.Buffered(1)),
           pl.BlockSpec((tm, D), row), pl.BlockSpec((tm, 1), row), pl.BlockSpec((1, D), const),
           pl.BlockSpec((tm, D), row)] + [ANY_SPEC] * n_sin,
        out_specs=out_specs + [ANY_SPEC] * n_sout, out_shape=out_shape + ([] if side is None else side.out_shape),
        scratch_shapes=[] if side is None else side.scratch,
        name=name, compiler_params=_cp("arbitrary"))(*a_parts, b, xin, r, g, dres, *side_in)


def _gla_consts():
    lmask = _iota((4 * CHUNK, CHUNK), 0) % CHUNK >= _iota((4 * CHUNK, CHUNK), 1)
    hmask = _iota((256, 256), 0) // CHUNK == _iota((256, 256), 1) // CHUNK
    bd = _iota((256, 512), 0) // CHUNK == _iota((256, 512), 1) // 128
    return lmask, hmask, bd


def _fold_heads(x):
    return x[0:64] + x[64:128] + x[128:192] + x[192:256]


def _gla_decays(la, b_scr, dec_scr):
    tri = (_iota((CHUNK, CHUNK), 0) >= _iota((CHUNK, CHUNK), 1)).astype(BF16)
    ones = jnp.ones((CHUNK, 128), BF16)
    for c in range(la.shape[0] // CHUNK):
        la3 = _split3(la[CHUNK * c:CHUNK * (c + 1)])
        b_scr[CHUNK * c:CHUNK * (c + 1), :] = _sum3(_dot(tri, la3), 1)
        dec_scr[c] = jnp.exp(_sum3(_dot_tn(la3, ones), 0))


def _gla_chunk(b, qc, kc):
    bl = b[CHUNK - 1:CHUNK, :]
    ep, en, ek = jnp.exp(b), jnp.exp(-b), jnp.exp(bl - b)
    return bl, ep, en, ek, qc * ep, qc * en, kc * en, kc * ep, kc * ek


def _gla_fwd(pm, pe, wau_p, b_alpha):
    s = pm.shape[0]
    t = _row_tile(s)
    nc = t // CHUNK

    def body(q_ref, k_ref, v_ref, e_ref, wau_ref, ba_ref, o_ref, st_ref, state, b_scr, dec_scr):
        @pl.when(pl.program_id(0) == 0)
        def _():
            state[...] = jnp.zeros_like(state)

        z = _dot(e_ref[...].astype(BF16), wau_ref[...]) + ba_ref[...]
        _gla_decays(_log_sigmoid(z) * (1.0 / GLA_TAU), b_scr, dec_scr)
        lmask, hmask, bd = _gla_consts()

        def chunk(c, carry):
            rows = pl.ds(pl.multiple_of(c * CHUNK, CHUNK), CHUNK)
            qc = q_ref[rows, :].astype(F32) * 0.125
            kc = k_ref[rows, :].astype(F32)
            vc = v_ref[rows, :]
            _, _, _, _, qp, qn, kn, kp, kk = _gla_chunk(b_scr[rows, :], qc, kc)
            decb = jnp.concatenate([dec_scr[c]] * 4, axis=1)
            qs = jnp.where(hmask, jnp.concatenate([qp] * 4, axis=0), 0.0).astype(BF16)
            qns = jnp.where(hmask, jnp.concatenate([qn] * 4, axis=0), 0.0).astype(BF16)
            attn = jnp.where(lmask, _dot_nt(qs, kn.astype(BF16)), _dot_nt(qns, kp.astype(BF16))).astype(BF16)
            st = state[...]
            o_intra = _fold_heads(jnp.where(bd, _dot(attn, vc), 0.0))
            o_ref[rows, :] = o_intra + _dot(qp.astype(BF16), st.astype(BF16))
            for h in range(4):
                st_ref[c, :, 128 * h:128 * (h + 1)] = st[64 * h:64 * (h + 1), 128 * h:128 * (h + 1)]
            kv = jnp.where(bd, _dot_tn(kk.astype(BF16), vc), 0.0)
            state[...] = st * decb + kv
            return carry

        lax.fori_loop(0, nc, chunk, 0)

    return pl.pallas_call(
        body, grid=(s // t,),
        in_specs=[pl.BlockSpec((t, 256), lambda i: (i, C_GQ // 256)), pl.BlockSpec((t, 256), lambda i: (i, C_GK // 256)),
                  pl.BlockSpec((t, 512), lambda i: (i, C_GV // 512)), pl.BlockSpec((t, PE_W), lambda i: (i, 0)),
                  pl.BlockSpec((PE_W, 256), lambda i: (0, 0)), pl.BlockSpec((1, 256), lambda i: (0, 0))],
        out_specs=[pl.BlockSpec((t, 512), lambda i: (i, 0)), pl.BlockSpec((nc, CHUNK, 512), lambda i: (i, 0, 0))],
        out_shape=[jax.ShapeDtypeStruct((s, 512), F32), jax.ShapeDtypeStruct((s // CHUNK, CHUNK, 512), F32)],
        scratch_shapes=[pltpu.VMEM((256, 512), F32), pltpu.VMEM((t, 256), F32), pltpu.VMEM((nc, 256, 128), F32)],
        name="gla_fwd", compiler_params=_cp("arbitrary"))(pm, pm, pm, pe, wau_p, b_alpha)


def _gla_bwd(pm, pe, wau_p, wau_pt, b_alpha, do, states):
    s = pm.shape[0]
    t = _row_tile(s)
    nc = t // CHUNK
    nb = s // t

    def body(q_ref, k_ref, v_ref, e_ref, wau_ref, waut_ref, ba_ref, do_ref, st_ref,
             dq_ref, dk_ref, dv_ref, de_ref, dwau_ref, dba_ref, gstate, b_scr, db_scr, dec_scr):
        @pl.when(pl.program_id(0) == 0)
        def _():
            gstate[...] = jnp.zeros_like(gstate)
            dwau_ref[...] = jnp.zeros_like(dwau_ref)
            dba_ref[...] = jnp.zeros_like(dba_ref)

        eb = e_ref[...].astype(BF16)
        z = _dot(eb, wau_ref[...]) + ba_ref[...]
        _gla_decays(_log_sigmoid(z) * (1.0 / GLA_TAU), b_scr, dec_scr)
        lmask, hmask, bd = _gla_consts()
        last_row = _iota((CHUNK, 256), 0) == CHUNK - 1

        def chunk(cc, carry):
            c = nc - 1 - cc
            rows = pl.ds(pl.multiple_of(c * CHUNK, CHUNK), CHUNK)
            qc = q_ref[rows, :].astype(F32) * 0.125
            kc = k_ref[rows, :].astype(F32)
            vc = v_ref[rows, :]
            dob = do_ref[rows, :]
            bl, ep, en, ek, qp, qn, kn, kp, kk = _gla_chunk(b_scr[rows, :], qc, kc)
            decb = jnp.concatenate([dec_scr[c]] * 4, axis=1)
            qs = jnp.where(hmask, jnp.concatenate([qp] * 4, axis=0), 0.0).astype(BF16)
            qns = jnp.where(hmask, jnp.concatenate([qn] * 4, axis=0), 0.0).astype(BF16)
            knb, kpb = kn.astype(BF16), kp.astype(BF16)
            attn = jnp.where(lmask, _dot_nt(qs, knb), _dot_nt(qns, kpb)).astype(BF16)
            st = jnp.where(bd, jnp.concatenate([st_ref[c]] * 4, axis=0), 0.0)
            g = gstate[...]
            gb = g.astype(BF16)
            do_s = jnp.where(bd, jnp.concatenate([dob] * 4, axis=0), jnp.zeros((), BF16))
            dattn = _dot_nt(do_s, vc)
            dv_ref[rows, :] = (_dot_tn(attn, do_s) + _dot(kk.astype(BF16), gb)).astype(BF16)
            dac = jnp.where(lmask, dattn, 0.0).astype(BF16)
            daa = jnp.where(lmask, 0.0, dattn).astype(BF16)
            dqp = _fold_heads(jnp.where(hmask, _dot(dac, knb), 0.0)) + _dot_nt(dob, st.astype(BF16))
            dqn = _fold_heads(jnp.where(hmask, _dot(daa, kpb), 0.0))
            dkn = _dot_tn(dac, qs)
            dkp = _dot_tn(daa, qns)
            dkk = _dot_nt(vc, gb)
            ddec = _dot_nt(jnp.ones((8, 1536), BF16), _split3(g * st))[0:1, :]
            gstate[...] = decb * g + jnp.where(bd, _dot_tn(qp.astype(BF16), dob), 0.0)
            dq_ref[rows, :] = ((dqp * ep + dqn * en) * 0.125).astype(BF16)
            dk_ref[rows, :] = (dkn * en + dkp * ep + dkk * ek).astype(BF16)
            dek = dkk * kc * ek
            db = (dqp * qc + dkp * kc) * ep - (dqn * qc + dkn * kc) * en - dek
            dbl = jnp.sum(dek, axis=0, keepdims=True) + ddec * jnp.exp(bl)
            db_scr[rows, :] = db + jnp.where(last_row, dbl, 0.0)
            return carry

        lax.fori_loop(0, nc, chunk, 0)
        triu = (_iota((CHUNK, CHUNK), 0) <= _iota((CHUNK, CHUNK), 1)).astype(BF16)
        dla = jnp.concatenate([_sum3(_dot(triu, _split3(db_scr[CHUNK * c:CHUNK * (c + 1), :])), 1) for c in range(nc)], axis=0)
        dz = dla * (1.0 / GLA_TAU) * _sigmoid(-z)
        dzb = dz.astype(BF16)
        dwau_ref[...] += _dot_tn(eb, dzb)
        dba_ref[...] += _fold8(dz)
        de_ref[...] = _dot(dzb, waut_ref[...])

    rev = lambda i: nb - 1 - i
    return pl.pallas_call(
        body, grid=(nb,),
        in_specs=[pl.BlockSpec((t, 256), lambda i: (rev(i), C_GQ // 256)), pl.BlockSpec((t, 256), lambda i: (rev(i), C_GK // 256)),
                  pl.BlockSpec((t, 512), lambda i: (rev(i), C_GV // 512)), pl.BlockSpec((t, PE_W), lambda i: (rev(i), 0)),
                  pl.BlockSpec((PE_W, 256), lambda i: (0, 0)), pl.BlockSpec((256, PE_W), lambda i: (0, 0)),
                  pl.BlockSpec((1, 256), lambda i: (0, 0)), pl.BlockSpec((t, 512), lambda i: (rev(i), 0)),
                  pl.BlockSpec((nc, CHUNK, 512), lambda i: (rev(i), 0, 0))],
        out_specs=[pl.BlockSpec((t, 256), lambda i: (rev(i), 0)), pl.BlockSpec((t, 256), lambda i: (rev(i), 0)),
                   pl.BlockSpec((t, 512), lambda i: (rev(i), 0)), pl.BlockSpec((t, PE_W), lambda i: (rev(i), 0)),
                   pl.BlockSpec((PE_W, 256), lambda i: (0, 0)), pl.BlockSpec((8, 256), lambda i: (0, 0))],
        out_shape=[jax.ShapeDtypeStruct((s, 256), BF16), jax.ShapeDtypeStruct((s, 256), BF16),
                   jax.ShapeDtypeStruct((s, 512), BF16), jax.ShapeDtypeStruct((s, PE_W), F32),
                   jax.ShapeDtypeStruct((PE_W, 256), F32), jax.ShapeDtypeStruct((8, 256), F32)],
        scratch_shapes=[pltpu.VMEM((256, 512), F32), pltpu.VMEM((t, 256), F32), pltpu.VMEM((t, 256), F32),
                        pltpu.VMEM((nc, 256, 128), F32)],
        name="gla_bwd", compiler_params=_cp("arbitrary"))(pm, pm, pm, pe, wau_p, wau_pt, b_alpha, do, states)


def _fcum_fwd(pe, bias):
    s = pe.shape[0]
    t = min(s, 256)

    def body(e_ref, b_ref, f_ref, carry):
        @pl.when(pl.program_id(0) == 0)
        def _():
            carry[...] = jnp.zeros_like(carry)

        lf = _log_sigmoid(e_ref[...] + b_ref[...])
        tri = (_iota((t, t), 0) >= _iota((t, t), 1)).astype(BF16)
        f = _sum3(_dot(tri, _split3(lf)), 1) + carry[0:1, :]
        f_ref[...] = f
        carry[...] = jnp.broadcast_to(f[t - 1:t, :], carry.shape)

    return pl.pallas_call(
        body, grid=(s // t,),
        in_specs=[pl.BlockSpec((t, PE_W), lambda i: (i, 0)), pl.BlockSpec((1, PE_W), lambda i: (0, 0))],
        out_specs=pl.BlockSpec((t, PE_W), lambda i: (i, 0)),
        out_shape=jax.ShapeDtypeStruct((s, PE_W), F32), scratch_shapes=[pltpu.VMEM((8, PE_W), F32)],
        name="fcum_fwd", compiler_params=_cp("arbitrary"))(pe, bias)


def _fcum_bwd(pe, bias, df):
    s = pe.shape[0]
    t = min(s, 256)
    nb = s // t

    def body(e_ref, b_ref, df_ref, de_ref, db_ref, carry):
        @pl.when(pl.program_id(0) == 0)
        def _():
            carry[...] = jnp.zeros_like(carry)
            db_ref[...] = jnp.zeros_like(db_ref)

        triu = (_iota((t, t), 0) <= _iota((t, t), 1)).astype(BF16)
        dlf = _sum3(_dot(triu, _split3(df_ref[...])), 1) + carry[0:1, :]
        carry[...] = jnp.broadcast_to(dlf[0:1, :], carry.shape)
        lane = _iota((t, PE_W), 1)
        dff = jnp.where((lane >= FF_LANE) & (lane < FF_LANE + 8), dlf * _sigmoid(-(e_ref[...] + b_ref[...])), 0.0)
        de_ref[...] = dff
        db_ref[...] += _fold8(dff)

    rev = lambda i: (nb - 1 - i, 0)
    return pl.pallas_call(
        body, grid=(nb,),
        in_specs=[pl.BlockSpec((t, PE_W), rev), pl.BlockSpec((1, PE_W), lambda i: (0, 0)), pl.BlockSpec((t, PE_W), rev)],
        out_specs=[pl.BlockSpec((t, PE_W), rev), pl.BlockSpec((8, PE_W), lambda i: (0, 0))],
        out_shape=[jax.ShapeDtypeStruct((s, PE_W), F32), jax.ShapeDtypeStruct((8, PE_W), F32)],
        scratch_shapes=[pltpu.VMEM((8, PE_W), F32)],
        name="fcum_bwd", compiler_params=_cp("arbitrary"))(pe, bias, df)


FOX_WIDE = 1024


def _split3(x):
    hi = x.astype(BF16)
    r = x - hi.astype(F32)
    mid = r.astype(BF16)
    lo = (r - mid.astype(F32)).astype(BF16)
    return jnp.concatenate([hi, mid, lo], axis=1)


def _sum3(x, axis):
    n = x.shape[axis] // 3
    parts = [lax.slice_in_dim(x, n * p, n * (p + 1), axis=axis) for p in range(3)]
    return (parts[0] + parts[1]) + parts[2]


def _fox_tables():
    heads, lane = np.arange(8), np.arange(64)
    spread = np.zeros((512, 1024), np.float32)
    spread[(64 * heads[:, None] + lane).ravel(), (128 * heads[:, None] + lane).ravel()] = 1.0
    def place(src_lane0, dst_off, val):
        t = np.zeros((384, 1024), np.float32)
        for p in range(3):
            t[128 * p + src_lane0 + heads, 128 * heads + dst_off + p] = val
        return t
    def const(off, val):
        c = np.zeros((1, 1024), np.float32)
        for p in range(3):
            c[0, 128 * heads + off + p] = val
        return c
    rows = np.zeros((8, 128), np.float32)
    rows[heads, FF_LANE + heads] = 1.0
    bf = lambda a: jnp.asarray(a, BF16)
    return dict(spread=bf(spread),
                f_to_q=bf(place(FF_LANE, 64, 1.0)), f_to_k=bf(place(FF_LANE, 67, -1.0)), d_to_do=bf(place(0, 64, 1.0)),
                ones_q=jnp.asarray(const(67, 1.0)), ones_k=jnp.asarray(const(64, 1.0)), ones_v=jnp.asarray(const(64, -1.0)),
                rows=jnp.asarray(rows))


LOG2E = 1.4426950408889634


def _fox_prep(pm, f128, lse8, tb, *, backward):
    s = pm.shape[0]
    tm = _row_tile(s)

    def body(*refs):
        if backward:
            q_ref, f_ref, lse_ref, sp_ref, fq_ref, cq_ref, rows_ref, qa_ref = refs
            f = f_ref[...] * LOG2E - _dot_tn(lse_ref[...], rows_ref[...], precision=HI)
            q2 = (q_ref[...].astype(F32) * (0.125 * LOG2E)).astype(BF16)
            qa_ref[...] = (_dot(q2, sp_ref[...]) + _dot(_split3(f), fq_ref[...]) + cq_ref[...]).astype(BF16)
            return
        (q_ref, k_ref, v_ref, f_ref, sp_ref, fq_ref, fk_ref, cq_ref, ck_ref, cv_ref,
         qa_ref, ka_ref, va_ref, vt_ref, qt_ref, kt_ref) = refs
        f3 = _split3(f_ref[...] * LOG2E)
        q, k, v = q_ref[...].astype(F32), k_ref[...], v_ref[...]
        sp = sp_ref[...]
        qa_ref[...] = (_dot((q * (0.125 * LOG2E)).astype(BF16), sp) + _dot(f3, fq_ref[...]) + cq_ref[...]).astype(BF16)
        ka_ref[...] = (_dot(k, sp) + _dot(f3, fk_ref[...]) + ck_ref[...]).astype(BF16)
        va_ref[...] = (_dot(v, sp) + cv_ref[...]).astype(BF16)
        vt_ref[...] = v.T
        qt_ref[...] = (q * 0.125).astype(BF16).T
        kt_ref[...] = (k.astype(F32) * 0.125).astype(BF16).T

    row = lambda i: (i, 0)
    const = lambda i: (0, 0)
    blk = lambda c: pl.BlockSpec((tm, 512), lambda i: (i, c // 512))
    wide = pl.BlockSpec((tm, 1024), row)
    mat = lambda a: pl.BlockSpec(a.shape, const)
    if backward:
        ins = [pm, f128, lse8, tb["spread"], tb["f_to_q"], tb["ones_q"], tb["rows"]]
        in_specs = [blk(C_FQ), pl.BlockSpec((tm, 128), row), pl.BlockSpec((8, tm), lambda i: (0, i))] + [mat(a) for a in ins[3:]]
        out_specs, out_shape = wide, jax.ShapeDtypeStruct((s, 1024), BF16)
    else:
        ins = [pm, pm, pm, f128, tb["spread"], tb["f_to_q"], tb["f_to_k"], tb["ones_q"], tb["ones_k"], tb["ones_v"]]
        in_specs = [blk(C_FQ), blk(C_FK), blk(C_FV), pl.BlockSpec((tm, 128), row)] + [mat(a) for a in ins[4:]]
        tr = pl.BlockSpec((512, tm), lambda i: (0, i))
        out_specs = [wide, wide, wide, tr, tr, tr]
        out_shape = [jax.ShapeDtypeStruct((s, 1024), BF16)] * 3 + [jax.ShapeDtypeStruct((512, s), BF16)] * 3
    return pl.pallas_call(body, grid=(s // tm,), in_specs=in_specs, out_specs=out_specs, out_shape=out_shape,
                          name="fox_prep_bwd" if backward else "fox_prep", compiler_params=_cp("parallel"))(*ins)


def _fox_post(dqt, dkt, dvt, rowsum8, colsum8, tb):
    s = dqt.shape[1]
    tm = _row_tile(s)

    def body(dqt_ref, dkt_ref, dvt_ref, rs_ref, cs_ref, rows_ref, dfq_ref, dfk_ref, dfv_ref, df_ref):
        dfq_ref[...] = dqt_ref[...].T.astype(BF16)
        dfk_ref[...] = dkt_ref[...].T
        dfv_ref[...] = dvt_ref[...].T
        df_ref[...] = _dot_tn(rs_ref[...] - cs_ref[...], rows_ref[...], precision=HI)

    row = lambda i: (i, 0)
    tr = pl.BlockSpec((512, tm), lambda i: (0, i))
    out = pl.BlockSpec((tm, 512), row)
    heads = pl.BlockSpec((8, tm), lambda i: (0, i))
    return pl.pallas_call(
        body, grid=(s // tm,),
        in_specs=[tr, tr, tr, heads, heads, pl.BlockSpec((8, 128), lambda i: (0, 0))],
        out_specs=[out, out, out, pl.BlockSpec((tm, 128), row)],
        out_shape=[jax.ShapeDtypeStruct((s, 512), BF16)] * 3 + [jax.ShapeDtypeStruct((s, 128), F32)],
        name="fox_post", compiler_params=_cp("parallel"))(dqt, dkt, dvt, rowsum8, colsum8, tb["rows"])


def _fox_fwd(k_aug, q_aug, vt):
    s = k_aug.shape[0]
    nh = 8
    tk = _row_tile(s)
    tq = min(s, FOX_WIDE)
    per = tq // tk

    def body(k_ref, q_ref, v_ref, o_ref, lse_ref, sbuf):
        i = pl.program_id(1)
        qa = q_ref[...]

        def scores(j):
            return _dot_nt(k_ref[pl.ds(pl.multiple_of(j * tk, tk), tk), :], qa)

        ones_row = (_iota((16, tk), 0) == 0).astype(BF16)

        def update(st, j, carry):
            m, acc = carry
            m2 = jnp.maximum(m, jnp.max(st, axis=0, keepdims=True))
            p = jnp.exp2(st - m2)
            vj = jnp.concatenate([v_ref[:, pl.ds(pl.multiple_of(j * tk, tk), tk)], ones_row], axis=0)
            return m2, jnp.exp2(m - m2) * acc + _dot(vj, p.astype(BF16))

        def step(a, carry):
            sbuf[1] = scores(2 * a + 1)
            carry = update(sbuf[0], 2 * a, carry)
            sbuf[0] = scores(2 * a + 2)
            return update(sbuf[1], 2 * a + 1, carry)

        n = i * per
        sbuf[0] = scores(0)
        carry = (jnp.full((1, tq), -1e30, F32), jnp.zeros((80, tq), F32))
        carry = lax.fori_loop(0, n // 2, step, carry)
        tri = _iota((tk, tk), 0) <= _iota((tk, tk), 1)
        late = [_dot_nt(k_ref[pl.ds(pl.multiple_of((n + r) * tk, tk), tk), :], qa[r * tk:, :]) for r in range(1, per)]
        for r in range(per):
            st = sbuf[0] if r == 0 else late[r - 1]
            head = jnp.where(tri, st[:, :tk], -1e30)
            st = head if st.shape[1] == tk else jnp.concatenate([head, st[:, tk:]], axis=1)
            part = update(st, n + r, tuple(c[:, r * tk:] for c in carry))
            carry = part if r == 0 else tuple(jnp.concatenate([old[:, :r * tk], new], axis=1) for old, new in zip(carry, part))
        m, acc = carry
        l = acc[64:65]
        o_ref[...] = (acc[0:64] / l).astype(BF16)
        lse_ref[0] = m + jnp.log2(l)

    return pl.pallas_call(
        body, grid=(nh, s // tq),
        in_specs=[pl.BlockSpec((s, 128), lambda h, i: (0, h)), pl.BlockSpec((tq, 128), lambda h, i: (i, h)),
                  pl.BlockSpec((64, s), lambda h, i: (h, 0))],
        out_specs=[pl.BlockSpec((64, tq), lambda h, i: (h, i)), pl.BlockSpec((1, 1, tq), lambda h, i: (h, 0, i))],
        out_shape=[jax.ShapeDtypeStruct((512, s), BF16), jax.ShapeDtypeStruct((nh, 1, s), F32)],
        scratch_shapes=[pltpu.VMEM((2, tk, tq), F32)],
        name="fox_fwd", compiler_params=_cp("parallel", "arbitrary"))(k_aug, q_aug, vt)


def _fox_bwd(q_aug, do_aug, qt, dot_, k_aug, v_aug, kt):
    s = q_aug.shape[0]
    nh = 8
    tq = _row_tile(s)
    tk = min(s, FOX_WIDE)
    per = tk // tq
    nqb = s // tq

    def body(qa_ref, da_ref, qt_ref, dt_ref, ka_ref, va_ref, kt_ref, dq_ref, rs_ref, dk_ref, dv_ref, dfk_ref):
        j = pl.program_id(1)

        @pl.when(j == 0)
        def _():
            dq_ref[...] = jnp.zeros_like(dq_ref)
            rs_ref[...] = jnp.zeros_like(rs_ref)

        ones_row = (_iota((16, tk), 0) == 0).astype(BF16)
        ka, va = ka_ref[...], va_ref[...]
        ks = jnp.concatenate([kt_ref[...], ones_row], axis=0)
        tri = _iota((tq, tq), 0) >= _iota((tq, tq), 1)

        def tile(i, w, carry):
            masked = w is not None
            w = tk if w is None else w
            rows = pl.ds(pl.multiple_of(i * tq, tq), tq)
            sp = _dot_nt(qa_ref[rows, :], ka[:w])
            if masked:
                last = jnp.where(tri, sp[:, w - tq:], -1e30)
                sp = last if w == tq else jnp.concatenate([sp[:, :w - tq], last], axis=1)
            p = jnp.exp2(sp)
            dsb = (p * _dot_nt(da_ref[rows, :], va[:w])).astype(BF16)
            dq = _dot_nt(ks[:, :w], dsb)
            dq_ref[:, rows] += dq[0:64]
            rs_ref[0, :, rows] += dq[64:72]
            new = (_dot(jnp.concatenate([qt_ref[:, rows], ones_row[:, :tq]], axis=0), dsb), _dot(dt_ref[:, rows], p.astype(BF16)))
            if w == tk:
                return tuple(c + d for c, d in zip(carry, new))
            return tuple(jnp.concatenate([c[:, :w] + d, c[:, w:]], axis=1) for c, d in zip(carry, new))

        carry = (jnp.zeros((80, tk), F32), jnp.zeros((64, tk), F32))
        for r in range(per):
            carry = tile(j * per + r, (r + 1) * tq, carry)
        dk, dv = lax.fori_loop((j + 1) * per, nqb, lambda i, c: tile(i, None, c), carry)
        dk_ref[...] = dk[0:64].astype(BF16)
        dv_ref[...] = dv.astype(BF16)
        dfk_ref[0] = dk[64:65]

    head_cols = lambda h, j: (0, h)
    head_rows = lambda h, j: (h, 0)
    once = dict(pipeline_mode=pl.Buffered(1))
    return pl.pallas_call(
        body, grid=(nh, s // tk),
        in_specs=[pl.BlockSpec((s, 128), head_cols, **once), pl.BlockSpec((s, 128), head_cols, **once),
                  pl.BlockSpec((64, s), head_rows, **once), pl.BlockSpec((64, s), head_rows, **once),
                  pl.BlockSpec((tk, 128), lambda h, j: (j, h)), pl.BlockSpec((tk, 128), lambda h, j: (j, h)),
                  pl.BlockSpec((64, tk), lambda h, j: (h, j))],
        out_specs=[pl.BlockSpec((64, s), head_rows), pl.BlockSpec((1, 8, s), lambda h, j: (h, 0, 0)),
                   pl.BlockSpec((64, tk), lambda h, j: (h, j)),
                   pl.BlockSpec((64, tk), lambda h, j: (h, j)), pl.BlockSpec((1, 1, tk), lambda h, j: (h, 0, j))],
        out_shape=[jax.ShapeDtypeStruct((512, s), F32), jax.ShapeDtypeStruct((nh, 8, s), F32),
                   jax.ShapeDtypeStruct((512, s), BF16),
                   jax.ShapeDtypeStruct((512, s), BF16), jax.ShapeDtypeStruct((nh, 1, s), F32)],
        name="fox_bwd", compiler_params=_cp("parallel", "arbitrary"))(q_aug, do_aug, qt, dot_, k_aug, v_aug, kt)


MEM_SCALE = 128 ** -0.5


def _mem_attn_fwd(pm, mkv):
    s = pm.shape[0]
    t = _row_tile(s)
    nm = mkv.shape[0]

    def body(q_ref, mk_ref, mv_ref, o_ref):
        for h in range(4):
            cols = slice(128 * h, 128 * (h + 1))
            sc = _dot_nt(q_ref[:, cols], mk_ref[:, cols]) * MEM_SCALE
            p = jnp.exp(sc - jnp.max(sc, axis=-1, keepdims=True))
            p = p / jnp.sum(p, axis=-1, keepdims=True)
            o_ref[:, cols] = _dot(p.astype(BF16), mv_ref[:, cols]).astype(BF16)

    return pl.pallas_call(
        body, grid=(s // t,),
        in_specs=[pl.BlockSpec((t, 512), lambda i: (i, C_MQ // 512)), pl.BlockSpec((nm, 512), lambda i: (0, 0)),
                  pl.BlockSpec((nm, 512), lambda i: (0, 1))],
        out_specs=pl.BlockSpec((t, 512), lambda i: (i, 0)),
        out_shape=jax.ShapeDtypeStruct((s, 512), BF16),
        name="mem_attn_fwd", compiler_params=_cp("parallel"))(pm, mkv, mkv)


def _mem_attn_bwd(pm, mkv, do):
    s = pm.shape[0]
    t = _row_tile(s)
    nm = mkv.shape[0]

    def body(q_ref, mk_ref, mv_ref, do_ref, dq_ref, dmk_ref, dmv_ref):
        @pl.when(pl.program_id(0) == 0)
        def _():
            dmk_ref[...] = jnp.zeros_like(dmk_ref)
            dmv_ref[...] = jnp.zeros_like(dmv_ref)

        for h in range(4):
            cols = slice(128 * h, 128 * (h + 1))
            qh, kh, vh, doh = q_ref[:, cols], mk_ref[:, cols], mv_ref[:, cols], do_ref[:, cols]
            sc = _dot_nt(qh, kh) * MEM_SCALE
            p = jnp.exp(sc - jnp.max(sc, axis=-1, keepdims=True))
            p = p / jnp.sum(p, axis=-1, keepdims=True)
            pb = p.astype(BF16)
            dp = _dot_nt(doh, vh)
            ds = (p * (dp - jnp.sum(p * dp, axis=-1, keepdims=True)) * MEM_SCALE).astype(BF16)
            dq_ref[:, cols] = _dot(ds, kh).astype(BF16)
            dmk_ref[:, cols] += _dot_tn(ds, qh)
            dmv_ref[:, cols] += _dot_tn(pb, doh)

    return pl.pallas_call(
        body, grid=(s // t,),
        in_specs=[pl.BlockSpec((t, 512), lambda i: (i, C_MQ // 512)), pl.BlockSpec((nm, 512), lambda i: (0, 0)),
                  pl.BlockSpec((nm, 512), lambda i: (0, 1)), pl.BlockSpec((t, 512), lambda i: (i, 0))],
        out_specs=[pl.BlockSpec((t, 512), lambda i: (i, 0)), pl.BlockSpec((nm, 512), lambda i: (0, 0)),
                   pl.BlockSpec((nm, 512), lambda i: (0, 0))],
        out_shape=[jax.ShapeDtypeStruct((s, 512), BF16), jax.ShapeDtypeStruct((nm, 512), F32),
                   jax.ShapeDtypeStruct((nm, 512), F32)],
        name="mem_attn_bwd", compiler_params=_cp("arbitrary"))(pm, mkv, mkv, do)


def _gain_grad(dxn_g, x, r, name):
    m, d = x.shape

    def body(d_ref, x_ref, r_ref, o_ref):
        o_ref[...] = _fold8(d_ref[...] * (x_ref[...] * r_ref[...]))

    return pl.pallas_call(body, out_shape=jax.ShapeDtypeStruct((8, d), F32), name=name,
                          compiler_params=pltpu.CompilerParams(vmem_limit_bytes=VMEM_LIMIT_BYTES))(dxn_g, x, r)


def _head_norm(o, gh):
    xs, rs = [], []
    for h in range(4):
        oh = o[:, 128 * h:128 * (h + 1)]
        r = lax.rsqrt(jnp.mean(oh * oh, axis=-1, keepdims=True) + EPS)
        xs.append(oh * r)
        rs.append(r)
    return xs, rs


def _merge_fwd(x, pm, o_gla, o_fox_t, o_mem, g_head, wg, wf, wm, wo, g_ffn):
    s = x.shape[0]
    t = min(s, 256)

    def body(x_ref, g0_ref, g1_ref, g2_ref, gg_ref, og_ref, of_ref, om_ref, gh_ref, wg_ref, wf_ref, wm_ref, wo_ref, gf_ref,
             mg_ref, h1_ref, u2_ref, r2_ref):
        xs, _ = _head_norm(og_ref[...], None)
        gg = gg_ref[...].astype(F32)
        sil = gg * _sigmoid(gg)
        ogn = jnp.concatenate(xs, axis=1) * gh_ref[...] * sil
        merged = (_sigmoid(g0_ref[...].astype(F32)) * _dot(ogn.astype(BF16), wg_ref[...])
                  + _sigmoid(g1_ref[...].astype(F32)) * _dot(of_ref[...].T, wf_ref[...])
                  + _sigmoid(g2_ref[...].astype(F32)) * _dot(om_ref[...], wm_ref[...]))
        mb = merged.astype(BF16)
        mg_ref[...] = mb
        h1 = x_ref[...] + _dot(mb, wo_ref[...])
        h1_ref[...] = h1
        r = lax.rsqrt(jnp.mean(h1 * h1, axis=-1, keepdims=True) + EPS)
        u2_ref[...] = ((h1 * r) * gf_ref[...]).astype(BF16)
        r2_ref[...] = r

    row = lambda i: (i, 0)
    const = lambda i: (0, 0)
    return pl.pallas_call(
        body, grid=(s // t,),
        in_specs=[pl.BlockSpec((t, D), row), pl.BlockSpec((t, D), lambda i: (i, 0)), pl.BlockSpec((t, D), lambda i: (i, 1)),
                  pl.BlockSpec((t, D), lambda i: (i, 2)), pl.BlockSpec((t, 512), lambda i: (i, C_GG // 512)),
                  pl.BlockSpec((t, 512), row), pl.BlockSpec((512, t), lambda i: (0, i)), pl.BlockSpec((t, 512), row),
                  pl.BlockSpec((1, 512), const), pl.BlockSpec((512, D), const), pl.BlockSpec((512, D), const),
                  pl.BlockSpec((512, D), const), pl.BlockSpec((D, D), const), pl.BlockSpec((1, D), const)],
        out_specs=[pl.BlockSpec((t, D), row), pl.BlockSpec((t, D), row), pl.BlockSpec((t, D), row), pl.BlockSpec((t, 1), row)],
        out_shape=[jax.ShapeDtypeStruct((s, D), BF16), jax.ShapeDtypeStruct((s, D), F32),
                   jax.ShapeDtypeStruct((s, D), BF16), jax.ShapeDtypeStruct((s, 1), F32)],
        name="merge_fwd", compiler_params=_cp("parallel"))(x, pm, pm, pm, pm, o_gla, o_fox_t, o_mem, g_head, wg, wf, wm, wo, g_ffn)


def _merge_bwd(dh1b, pm, o_gla, o_fox_t, o_mem, g_head, wg, wf, wm, wgt, wft, wmt, wot, spread, d_to_do):
    s = dh1b.shape[0]
    t = min(s, 256)

    def body(dh_ref, g0_ref, g1_ref, g2_ref, gg_ref, og_ref, of_ref, om_ref, gh_ref, wg_ref, wf_ref, wm_ref,
             wgt_ref, wft_ref, wmt_ref, wot_ref, sp_ref, dd_ref,
             dgt_ref, dgg_ref, dog_ref, da_ref, dot_ref, dom_ref, dwg_ref, dwf_ref, dwm_ref, dgh_ref):
        @pl.when(pl.program_id(0) == 0)
        def _():
            dwg_ref[...] = jnp.zeros_like(dwg_ref)
            dwf_ref[...] = jnp.zeros_like(dwf_ref)
            dwm_ref[...] = jnp.zeros_like(dwm_ref)
            dgh_ref[...] = jnp.zeros_like(dgh_ref)

        dmerged = _dot(dh_ref[...], wot_ref[...])
        og = og_ref[...]
        xs, rs = _head_norm(og, None)
        on = jnp.concatenate(xs, axis=1)
        gg = gg_ref[...].astype(F32)
        sg = _sigmoid(gg)
        sil = gg * sg
        gh = gh_ref[...]
        ognb = (on * gh * sil).astype(BF16)
        ofb, omb = of_ref[...].T, om_ref[...]
        douts = []
        for idx, (gref, ob, w_ref, wt_ref, dw_ref) in enumerate((
                (g0_ref, ognb, wg_ref, wgt_ref, dwg_ref), (g1_ref, ofb, wf_ref, wft_ref, dwf_ref),
                (g2_ref, omb, wm_ref, wmt_ref, dwm_ref))):
            gt = _sigmoid(gref[...].astype(F32))
            y = _dot(ob, w_ref[...])
            dgt_ref[:, D * idx:D * (idx + 1)] = (dmerged * y * gt * (1.0 - gt)).astype(BF16)
            dy = (gt * dmerged).astype(BF16)
            dw_ref[...] += _dot_tn(ob, dy)
            douts.append(_dot(dy, wt_ref[...]))
        dogn, dof, dom = douts
        dofb = dof.astype(BF16)
        dom_ref[...] = dom.astype(BF16)
        ind = (_iota((1536, 128), 0) % 512 // 64 == _iota((1536, 128), 1)).astype(BF16)
        delta = _dot(_split3(dofb.astype(F32) * ofb.astype(F32)), ind)
        da_ref[...] = (_dot(dofb, sp_ref[...]) + _dot(_split3(delta), dd_ref[...])).astype(BF16)
        dot_ref[...] = dofb.T
        dgg_ref[...] = (dogn * on * gh * (sg * (1.0 + gg * (1.0 - sg)))).astype(BF16)
        d_on = dogn * sil
        dgh_ref[...] += _fold8(d_on * on)
        dxn = d_on * gh
        outs = []
        for h in range(4):
            cols = slice(128 * h, 128 * (h + 1))
            dh_, xh = dxn[:, cols], xs[h]
            outs.append(rs[h] * (dh_ - xh * jnp.mean(dh_ * xh, axis=-1, keepdims=True)))
        dog_ref[...] = jnp.concatenate(outs, axis=1).astype(BF16)

    row = lambda i: (i, 0)
    const = lambda i: (0, 0)
    return pl.pallas_call(
        body, grid=(s // t,),
        in_specs=[pl.BlockSpec((t, D), row), pl.BlockSpec((t, D), lambda i: (i, 0)), pl.BlockSpec((t, D), lambda i: (i, 1)),
                  pl.BlockSpec((t, D), lambda i: (i, 2)), pl.BlockSpec((t, 512), lambda i: (i, C_GG // 512)),
                  pl.BlockSpec((t, 512), row), pl.BlockSpec((512, t), lambda i: (0, i)), pl.BlockSpec((t, 512), row),
                  pl.BlockSpec((1, 512), const), pl.BlockSpec((512, D), const), pl.BlockSpec((512, D), const),
                  pl.BlockSpec((512, D), const), pl.BlockSpec((D, 512), const), pl.BlockSpec((D, 512), const),
                  pl.BlockSpec((D, 512), const), pl.BlockSpec((D, D), const),
                  pl.BlockSpec((512, 1024), const), pl.BlockSpec((384, 1024), const)],
        out_specs=[pl.BlockSpec((t, 3 * D), row), pl.BlockSpec((t, 512), row), pl.BlockSpec((t, 512), row),
                   pl.BlockSpec((t, 1024), row), pl.BlockSpec((512, t), lambda i: (0, i)), pl.BlockSpec((t, 512), row),
                   pl.BlockSpec((512, D), const), pl.BlockSpec((512, D), const), pl.BlockSpec((512, D), const),
                   pl.BlockSpec((8, 512), const)],
        out_shape=[jax.ShapeDtypeStruct((s, 3 * D), BF16), jax.ShapeDtypeStruct((s, 512), BF16),
                   jax.ShapeDtypeStruct((s, 512), BF16), jax.ShapeDtypeStruct((s, 1024), BF16),
                   jax.ShapeDtypeStruct((512, s), BF16), jax.ShapeDtypeStruct((s, 512), BF16),
                   jax.ShapeDtypeStruct((512, D), F32), jax.ShapeDtypeStruct((512, D), F32),
                   jax.ShapeDtypeStruct((512, D), F32), jax.ShapeDtypeStruct((8, 512), F32)],
        name="merge_bwd", compiler_params=_cp("arbitrary"))(
            dh1b, pm, pm, pm, pm, o_gla, o_fox_t, o_mem, g_head, wg, wf, wm, wgt, wft, wmt, wot, spread, d_to_do)


def _ff2_loss(a, w2, h1, g_final, target):
    s, k = a.shape
    tm = min(s, 256)

    def body(a_ref, w_ref, h1_ref, g_ref, t_ref, dh_ref, dhb_ref, loss_ref, dg_ref):
        @pl.when(pl.program_id(0) == 0)
        def _():
            loss_ref[...] = jnp.zeros_like(loss_ref)
            dg_ref[...] = jnp.zeros_like(dg_ref)

        h2 = h1_ref[...] + _dot(_relu2_bf16(a_ref[...]), w_ref[...])
        r = lax.rsqrt(jnp.mean(h2 * h2, axis=-1, keepdims=True) + EPS)
        xn = h2 * r
        g = g_ref[...]
        err = xn * g - t_ref[...]
        e2 = _fold8(err * err)
        part = e2[:, 0:128]
        for c in range(1, D // 128):
            part = part + e2[:, 128 * c:128 * (c + 1)]
        loss_ref[...] += part
        dy = err * (1.0 / D)
        dg_ref[...] += _fold8(dy * xn)
        dxn = dy * g
        dh = r * (dxn - xn * jnp.mean(dxn * xn, axis=-1, keepdims=True))
        dh_ref[...] = dh
        dhb_ref[...] = dh.astype(BF16)

    row = lambda i: (i, 0)
    const = lambda i: (0, 0)
    return pl.pallas_call(
        body, grid=(s // tm,),
        in_specs=[pl.BlockSpec((tm, k), row), pl.BlockSpec((k, D), const, pipeline_mode=pl.Buffered(1)),
                  pl.BlockSpec((tm, D), row), pl.BlockSpec((1, D), const), pl.BlockSpec((tm, D), row)],
        out_specs=[pl.BlockSpec((tm, D), row), pl.BlockSpec((tm, D), row), pl.BlockSpec((8, 128), const),
                   pl.BlockSpec((8, D), const)],
        out_shape=[jax.ShapeDtypeStruct((s, D), F32), jax.ShapeDtypeStruct((s, D), BF16),
                   jax.ShapeDtypeStruct((8, 128), F32), jax.ShapeDtypeStruct((8, D), F32)],
        name="ff2_loss", compiler_params=_cp("arbitrary"))(a, w2, h1, g_final, target)


def _adam(w, g, m, v, name):
    _, r, c = w.shape
    tr = r
    for cand in (512, 256, 128, 64, 32, 16, 8):
        if r % cand == 0 and cand * c * 4 <= (1 << 20):
            tr = cand
            break
    c1 = 1.0 - ADAM_B1 ** ADAM_STEP
    c2 = 1.0 - ADAM_B2 ** ADAM_STEP

    def body(w_ref, g_ref, m_ref, v_ref, d_ref, nm_ref, nv_ref):
        gv = g_ref[...]
        nm = ADAM_B1 * m_ref[...] + (1.0 - ADAM_B1) * gv
        nv = ADAM_B2 * v_ref[...] + (1.0 - ADAM_B2) * (gv * gv)
        d_ref[...] = -ADAM_LR * ((nm / c1) / (jnp.sqrt(nv / c2) + ADAM_EPS) + ADAM_WD * w_ref[...])
        nm_ref[...] = nm
        nv_ref[...] = nv

    spec = pl.BlockSpec((1, tr, c), lambda i: (0, i, 0))
    return pl.pallas_call(
        body, grid=(r // tr,), in_specs=[spec] * 4, out_specs=[spec] * 3,
        out_shape=[jax.ShapeDtypeStruct((1, r, c), F32)] * 3, name=name, compiler_params=_cp("parallel"))(w, g, m, v)


def _row_block(r):
    return max(d for d in range(16, 513, 16) if r % d == 0)


def _add2(a, b, name):
    n, r, c = a.shape
    tr = _row_block(r)

    def body(a_ref, b_ref, o_ref):
        o_ref[...] = (a_ref[...].astype(F32) + b_ref[...].astype(F32)).astype(BF16)

    spec = pl.BlockSpec((1, tr, c), lambda k, i: (k, i, 0))
    return pl.pallas_call(body, grid=(n, r // tr), in_specs=[spec, spec], out_specs=spec,
                          out_shape=jax.ShapeDtypeStruct((n, r, c), BF16), name=name,
                          compiler_params=_cp("parallel", "parallel"))(a, b)


def _sum4(a, name):
    _, r, c = a.shape
    tr = _row_block(r)

    def body(a_ref, o_ref):
        o_ref[...] = ((a_ref[0].astype(F32) + a_ref[1].astype(F32)) + a_ref[2].astype(F32)) + a_ref[3].astype(F32)

    return pl.pallas_call(body, grid=(r // tr,), in_specs=[pl.BlockSpec((4, tr, c), lambda i: (0, i, 0))],
                          out_specs=pl.BlockSpec((tr, c), lambda i: (i, 0)),
                          out_shape=jax.ShapeDtypeStruct((r, c), F32), name=name, compiler_params=_cp("parallel"))(a)


def _adam_small(w, gathered, m, v):
    c1 = 1.0 - ADAM_B1 ** ADAM_STEP
    c2 = 1.0 - ADAM_B2 ** ADAM_STEP

    def body(w_ref, g_ref, m_ref, v_ref, gs_ref, d_ref, nm_ref, nv_ref):
        gv = g_ref[0]
        for dev in range(1, N_DEV):
            gv = gv + g_ref[dev]
        gs_ref[...] = gv
        nm = ADAM_B1 * m_ref[...] + (1.0 - ADAM_B1) * gv
        nv = ADAM_B2 * v_ref[...] + (1.0 - ADAM_B2) * (gv * gv)
        d_ref[...] = -ADAM_LR * ((nm / c1) / (jnp.sqrt(nv / c2) + ADAM_EPS) + ADAM_WD * w_ref[...])
        nm_ref[...] = nm
        nv_ref[...] = nv

    return pl.pallas_call(body, out_shape=[jax.ShapeDtypeStruct((8, D), F32)] * 4, name="adam_small")(w, gathered, m, v)


def _place():
    return lax.axis_index("x"), lax.axis_index("y"), lax.axis_index("c")


def _other_chips(x, y):
    return [(1 - x, y), (x, 1 - y), (1 - x, 1 - y)]


def _gather_shards(p):
    def body(p_ref, out_ref, *sems):
        start, finish = _gather_ops((p_ref,), (out_ref,), sems)
        start()
        finish()

    return pl.pallas_call(
        body, out_shape=jax.ShapeDtypeStruct((N_CHIPS,) + p.shape, p.dtype), in_specs=[ANY_SPEC], out_specs=ANY_SPEC,
        scratch_shapes=GATHER_SEMS, name="gather_shards")(p)


GATHER_SEMS = [pltpu.SemaphoreType.DMA((6,)), pltpu.SemaphoreType.DMA((6,)), pltpu.SemaphoreType.DMA]


def _gather_ops(in_refs, out_refs, sems):
    (p_ref,), (out_ref,) = in_refs, out_refs
    send_sems, recv_sems, local_sem = sems
    hr = p_ref.shape[0] // 2
    x, y, cc = _place()
    sibling = (x, y, 1 - cc)
    chips = _other_chips(x, y)

    def half(chip, core):
        return out_ref.at[2 * chip[0] + chip[1], pl.ds(core * hr, hr), :]

    def copy(k, chip, core, to, src=None):
        return pltpu.make_async_remote_copy(
            src_ref=half(chip, core) if src is None else src, dst_ref=half(chip, core),
            send_sem=send_sems.at[k], recv_sem=recv_sems.at[k], device_id=to, device_id_type=MESH)

    mine = pltpu.make_async_copy(p_ref, out_ref.at[2 * x + y], local_sem)
    my_half = p_ref.at[pl.ds(cc * hr, hr), :]
    first = [copy(j, (x, y), cc, (*chip, cc), src=my_half) for j, chip in enumerate(chips)]
    passed = [copy(3 + j, chip, cc, sibling) for j, chip in enumerate(chips)]

    def start():
        mine.start()
        for cp in first:
            cp.start()

    def finish():
        for j, chip in enumerate(chips):
            copy(j, chip, cc, (x, y, cc)).wait_recv()
            passed[j].start()
        for j, chip in enumerate(chips):
            copy(3 + j, chip, 1 - cc, (x, y, cc)).wait_recv()
        for cp in first + passed:
            cp.wait_send()
        mine.wait()

    return start, finish


def _gather_side(p):
    return _Side([p], [jax.ShapeDtypeStruct((N_CHIPS,) + p.shape, p.dtype)], GATHER_SEMS, _gather_ops)


def _swap_halves(g):
    n, r, c = g.shape
    hr = r // 2

    def body(g_ref, out_ref, send_sem, recv_sem):
        x, y, cc = _place()
        cp = pltpu.make_async_remote_copy(
            src_ref=g_ref.at[:, pl.ds((1 - cc) * hr, hr), :], dst_ref=out_ref,
            send_sem=send_sem, recv_sem=recv_sem, device_id=(x, y, 1 - cc), device_id_type=MESH)
        cp.start()
        cp.wait()

    any_spec = pl.BlockSpec(memory_space=pl.ANY)
    return pl.pallas_call(
        body, out_shape=jax.ShapeDtypeStruct((n, hr, c), g.dtype), in_specs=[any_spec], out_specs=any_spec,
        scratch_shapes=[pltpu.SemaphoreType.DMA, pltpu.SemaphoreType.DMA], name="swap_halves")(g)


SCATTER_SEMS = [pltpu.SemaphoreType.DMA((7,)), pltpu.SemaphoreType.DMA((7,)), pltpu.SemaphoreType.DMA]


def _scatter_ops(in_refs, out_refs, sems):
    (p_ref,), (out_ref,) = in_refs, out_refs
    send_sems, recv_sems, local_sem = sems
    hr = p_ref.shape[1]
    x, y, cc = _place()
    me = 2 * x + y
    sibling = (x, y, 1 - cc)
    chips = _other_chips(x, y)
    ids = [2 * chip[0] + chip[1] for chip in chips]

    def land(src, core):
        return out_ref.at[src, pl.ds(core * hr, hr), :]

    def copy(k, src_ref, dst_ref, to):
        return pltpu.make_async_remote_copy(src_ref=src_ref, dst_ref=dst_ref, send_sem=send_sems.at[k],
                                            recv_sem=recv_sems.at[k], device_id=to, device_id_type=MESH)

    mine = pltpu.make_async_copy(p_ref.at[me], land(me, cc), local_sem)
    sends = [copy(j, p_ref.at[ids[j]], land(me, cc), (*chip, cc)) for j, chip in enumerate(chips)]
    sends.append(copy(3, p_ref.at[me], land(me, cc), sibling))
    passed = [copy(4 + j, land(ids[j], cc), land(ids[j], cc), sibling) for j in range(3)]

    def start():
        mine.start()
        for cp in sends:
            cp.start()

    def finish():
        for j in range(3):
            copy(j, p_ref.at[me], land(ids[j], cc), (x, y, cc)).wait_recv()
            passed[j].start()
        copy(3, p_ref.at[me], land(me, 1 - cc), (x, y, cc)).wait_recv()
        for j in range(3):
            copy(4 + j, p_ref.at[me], land(ids[j], 1 - cc), (x, y, cc)).wait_recv()
        for cp in sends + passed:
            cp.wait_send()
        mine.wait()

    return start, finish


def _scatter_side(p):
    n, hr, c = p.shape
    return _Side([p], [jax.ShapeDtypeStruct((n, 2 * hr, c), p.dtype)], SCATTER_SEMS, _scatter_ops)


def _gather_small(blk):
    m, n = blk.shape

    def body(x_ref, out_ref, send_sems, recv_sems, local_sem):
        x, y, cc = _place()
        me, sibling = (x, y, cc), (x, y, 1 - cc)
        chips = _other_chips(x, y)

        def slot(px, py, pc):
            return out_ref.at[4 * px + 2 * py + pc]

        def copy(k, block, to, src=None):
            return pltpu.make_async_remote_copy(
                src_ref=slot(*block) if src is None else src, dst_ref=slot(*block),
                send_sem=send_sems.at[k], recv_sem=recv_sems.at[k], device_id=to, device_id_type=MESH)

        mine = pltpu.make_async_copy(x_ref, slot(*me), local_sem)
        mine.start()
        first = [copy(0, me, sibling, src=x_ref)]
        first += [copy(1 + j, me, (*chip, cc), src=x_ref) for j, chip in enumerate(chips)]
        for cp in first:
            cp.start()
        passed = [copy(4 + j, (*chip, cc), sibling) for j, chip in enumerate(chips)]
        for j, chip in enumerate(chips):
            copy(1 + j, (*chip, cc), me).wait_recv()
            passed[j].start()
        copy(0, sibling, me).wait_recv()
        for j, chip in enumerate(chips):
            copy(4 + j, (*chip, 1 - cc), me).wait_recv()
        for cp in first + passed:
            cp.wait_send()
        mine.wait()

    vmem = pl.BlockSpec(memory_space=pltpu.VMEM)
    return pl.pallas_call(
        body, out_shape=jax.ShapeDtypeStruct((N_DEV, m, n), blk.dtype), in_specs=[vmem], out_specs=vmem,
        scratch_shapes=[pltpu.SemaphoreType.DMA((7,)), pltpu.SemaphoreType.DMA((7,)), pltpu.SemaphoreType.DMA],
        name="gather_small")(blk)


def _shard_shape(name, full_shape):
    shp = list(full_shape)
    shp[SHARD_AXIS[name]] //= N_CHIPS
    return tuple(shp)


FULL_SHAPES = {"w_in": (D, 6680), "w_alpha_up": (16, 256), "w_mem_kv": (D, D), "w_gla_o": (512, D), "w_fox_o": (512, D),
               "w_mem_o": (512, D), "w_out": (D, D), "w_ff1": (D, 4 * D), "w_ff2": (4 * D, D)}


def _pack_a(sh, dtype):
    w = sh["w_in"].astype(dtype)
    return jnp.concatenate([w[:, 0:PACK_W], jnp.pad(w[:, PACK_W:], ((0, 0), (0, 2 * PACK_W - w.shape[1])))], axis=0)


def _pack_b(sh, dtype):
    o3 = jnp.concatenate([sh["w_gla_o"], sh["w_fox_o"], sh["w_mem_o"], jnp.zeros((512, 256), sh["w_gla_o"].dtype)], axis=1)
    au = jnp.pad(sh["w_alpha_up"], ((0, PACK_ROWS_B - 3072 - 16), (0, PACK_W - 64)))
    return jnp.concatenate([sh["w_ff1"], sh["w_ff2"], sh["w_mem_kv"], sh["w_out"], o3, au], axis=0).astype(dtype)


def _unpack_a(pa):
    return {"w_in": jnp.concatenate([pa[0:1024], pa[1024:2048, 0:1670 - PACK_W]], axis=1)}


def _unpack_b(pb):
    return {"w_ff1": pb[0:1024], "w_ff2": pb[1024:2048], "w_mem_kv": pb[2048:2304], "w_out": pb[2304:2560],
            "w_gla_o": pb[2560:3072, 0:256], "w_fox_o": pb[2560:3072, 256:512], "w_mem_o": pb[2560:3072, 512:768],
            "w_alpha_up": pb[3072:3088, 0:64]}


def _unpack(packed):
    return {**_unpack_a(packed[0:PACK_ROWS_A]), **_unpack_b(packed[PACK_ROWS_A:])}


def _split_shards(name, full):
    return jnp.split(full, N_CHIPS, axis=SHARD_AXIS[name])


def _pack_small(vals, scalar=None):
    row4 = jnp.concatenate([vals["b_alpha"].reshape(-1), vals["b_forget"].reshape(-1), jnp.zeros((D - 264,), F32)])
    row5 = jnp.concatenate([vals["g_gla_head"].reshape(-1), jnp.zeros((D - 512,), F32)])
    row6 = jnp.zeros((D,), F32) if scalar is None else jnp.broadcast_to(scalar, (D,))
    rows = [vals["g_mix"].reshape(-1), vals["g_mem"].reshape(-1), vals["g_ffn"].reshape(-1), vals["g_final"].reshape(-1),
            row4, row5, row6, jnp.zeros((D,), F32)]
    return jnp.stack(rows)


def _unpack_small(blk):
    return {"g_mix": blk[0].reshape(1, D), "g_mem": blk[1].reshape(1, D), "g_ffn": blk[2].reshape(1, D),
            "g_final": blk[3].reshape(D), "b_alpha": blk[4, 0:256].reshape(1, 256), "b_forget": blk[4, 256:264].reshape(1, 8),
            "g_gla_head": blk[5, 0:512].reshape(1, 4, 128)}


def _local_step(x, mem, target, wb, small, exchange=None):
    s = x.shape[0]
    nm = mem.shape[0]
    t = _row_tile(s)
    nb = s // t
    w_in = wb["w_in"]
    w_main = jnp.concatenate([w_in[:, 3608:6680], w_in[:, 0:1536], w_in[:, 1552:3088], w_in[:, 3096:3608]], axis=1)
    w_e = jnp.concatenate([w_in[:, 1536:1552], w_in[:, 3088:3096], jnp.zeros((D, PE_W - 24), BF16)], axis=1)
    w_in_pt = jnp.concatenate([w_main, w_e], axis=1).T
    b_alpha = small["b_alpha"].reshape(1, 256)
    bias_e = jnp.concatenate([jnp.zeros((FF_LANE,), F32), small["b_forget"].reshape(-1),
                              jnp.zeros((PE_W - FF_LANE - 8,), F32)]).reshape(1, PE_W)
    g_mix, g_mem, g_ffn = small["g_mix"].reshape(1, D), small["g_mem"].reshape(1, D), small["g_ffn"].reshape(1, D)
    g_final = small["g_final"].reshape(1, D)
    g_head = small["g_gla_head"].reshape(1, 512)

    u, r1 = _rms_fwd(x, g_mix, "norm_mix")
    big = min(s, 1024)
    if exchange is None:
        pm = _mm_nn(u, w_main, out_dtype=BF16, tm=big, tn=PM_W // 4, tk=D, name="proj_main")
    else:
        pm, gathered = _mm_nn(u, w_main, out_dtype=BF16, tm=big, tn=PM_W // 4, tk=D, name="proj_main", side=exchange.gather)
        wb = {**wb, **exchange.weights(gathered)}
    wau_p = jnp.concatenate([wb["w_alpha_up"], jnp.zeros((PE_W - 16, 256), BF16)], axis=0)
    pe = _mm_nn(u, w_e, out_dtype=F32, tm=t, tn=PE_W, tk=D, name="proj_narrow")
    o_gla, states = _gla_fwd(pm, pe, wau_p, b_alpha)
    fcum = _fcum_fwd(pe, bias_e)
    tb = _fox_tables()
    qf_aug, k_aug, v_aug, vt, qt, kt = _fox_prep(pm, fcum, None, tb, backward=False)
    o_fox, lse = _fox_fwd(k_aug, qf_aug, vt)
    mn, rm = _rms_fwd(mem, g_mem, "norm_mem")
    mkv = _mm_nn(mn, wb["w_mem_kv"], out_dtype=BF16, tm=nm, tn=512, tk=D, name="mem_kv")
    o_mem = _mem_attn_fwd(pm, mkv)
    merged, h1, u2, r2 = _merge_fwd(x, pm, o_gla, o_fox, o_mem, g_head, wb["w_gla_o"], wb["w_fox_o"], wb["w_mem_o"],
                                    wb["w_out"], g_ffn)
    a = _mm_nn(u2, wb["w_ff1"], out_dtype=BF16, tm=big, tn=1024, tk=D, name="ff1")
    dh2, dh2b, loss8, dgfin8 = _ff2_loss(a, wb["w_ff2"], h1, g_final, target)
    loss = 0.5 * jnp.sum(loss8) / D

    da = _mm_nn(dh2b, wb["w_ff2"].T, out_dtype=BF16, tm=big, tn=1024, tk=D, name="d_act",
                epi=lambda acc, at: acc * (2.0 * jnp.maximum(at.astype(F32), 0.0)), extra=a)
    gw = {}
    gw["w_ff2"] = _mm_tn(a, dh2b, tm=1024, tn=D, ts=big, name="dw_ff2", a_fn=_relu2_bf16)
    gw["w_ff1"] = _mm_tn(u2, da, tm=D, tn=1024, ts=big, name="dw_ff1")
    dh1, dh1b, dgffn8 = _mm_norm_bwd([da], wb["w_ff1"].T, h1, r2, g_ffn, dh2, name="d_h1", want_bf16=True)
    gw["w_out"] = _mm_tn(merged, dh1b, tm=D, tn=D, ts=big, name="dw_out")
    (dgates, dgg, do_gla, do_aug, do_t, do_mem, gw["w_gla_o"], gw["w_fox_o"], gw["w_mem_o"], dgh8) = _merge_bwd(
        dh1b, pm, o_gla, o_fox, o_mem, g_head, wb["w_gla_o"], wb["w_fox_o"], wb["w_mem_o"],
        wb["w_gla_o"].T, wb["w_fox_o"].T, wb["w_mem_o"].T, wb["w_out"].T, tb["spread"], tb["d_to_do"])
    dgq, dgk, dgv, de_gla, dwau_p, dba8 = _gla_bwd(pm, pe, wau_p, wau_p.T, b_alpha, do_gla, states)
    gw["w_alpha_up"] = dwau_p[0:16, :]
    q_aug = _fox_prep(pm, fcum, lse.reshape(8, s), tb, backward=True)
    dfq_t, dfrow, dfk_t, dfv_t, dfcol = _fox_bwd(q_aug, do_aug, qt, do_t, k_aug, v_aug, kt)
    dfq, dfk, dfv, df = _fox_post(dfq_t, dfk_t, dfv_t, dfrow[:, 0, :], dfcol.reshape(8, s), tb)
    de_fox, dbf8 = _fcum_bwd(pe, bias_e, df)
    dmq, dmk, dmv = _mem_attn_bwd(pm, mkv, do_mem)
    dmkv = jnp.concatenate([dmk, dmv], axis=1).astype(BF16)
    gw["w_mem_kv"] = _mm_tn(mn, dmkv, tm=D, tn=D, ts=nm, name="dw_mem_kv")
    dmn_g = _mm_nn(dmkv, wb["w_mem_kv"].T, out_dtype=F32, tm=nm, tn=D, tk=D, name="d_mem_norm")
    dgmem8 = _gain_grad(dmn_g, mem, rm, "dg_mem")
    de = (de_gla + de_fox).astype(BF16)
    dproj = [dgates, dgq, dgk, dgv, dgg, dfq, dfk, dfv, dmq, de]
    dw_gates = _mm_tn(u, dgates, tm=D, tn=1024, ts=big, name="dw_in_gates")
    dw_g = _mm_tn_cat(u, [dgq, dgk, dgv], ts=big, name="dw_in_gla")
    dw_gf = _mm_tn_cat(u, [dgg, dfq], ts=big, name="dw_in_gg_fq")
    dw_f = _mm_tn_cat(u, [dfk, dfv], ts=big, name="dw_in_fk_fv")
    dw_m = _mm_tn_cat(u, [dmq, de], ts=big, name="dw_in_mq_narrow")
    gw["w_in"] = jnp.concatenate([dw_g, dw_gf[:, 0:512], dw_m[:, 512:528], dw_gf[:, 512:1024], dw_f,
                                  dw_m[:, 528:536], dw_m[:, 0:512], dw_gates], axis=1)
    if exchange is None:
        grad_x, dgmix8 = _mm_norm_bwd(dproj, w_in_pt, x, r1, g_mix, dh1, name="d_x", want_bf16=False)
        exchanged = None
    else:
        grad_x, dgmix8, exchanged = _mm_norm_bwd(dproj, w_in_pt, x, r1, g_mix, dh1, name="d_x", want_bf16=False,
                                                 side=exchange.scatter(gw))
    gs = {"g_mix": dgmix8.sum(0), "g_mem": dgmem8.sum(0), "g_ffn": dgffn8.sum(0), "g_final": dgfin8.sum(0),
          "b_alpha": dba8.sum(0), "b_forget": dbf8.sum(0)[FF_LANE:FF_LANE + 8], "g_gla_head": dgh8.sum(0)}
    return loss, grad_x, gw, gs, exchanged


def kernel(x, mem, g_mix, w_in, w_alpha_up, b_alpha, b_forget, g_gla_head, g_mem, w_mem_kv, w_gla_o, w_fox_o, w_mem_o, w_out, g_ffn, w_ff1, w_ff2, g_final, loss_target, m_g_mix, m_w_in, m_w_alpha_up, m_b_alpha, m_b_forget, m_g_gla_head, m_g_mem, m_w_mem_kv, m_w_gla_o, m_w_fox_o, m_w_mem_o, m_w_out, m_g_ffn, m_w_ff1, m_w_ff2, m_g_final, v_g_mix, v_w_in, v_w_alpha_up, v_b_alpha, v_b_forget, v_g_gla_head, v_g_mem, v_w_mem_kv, v_w_gla_o, v_w_fox_o, v_w_mem_o, v_w_out, v_g_ffn, v_w_ff1, v_w_ff2, v_g_final):
    args = dict(locals())
    w_sh = {n: args[n][0] for n in WEIGHTS}
    small = {n: args[n] for n in SMALL}

    def whole(parts):
        return {n: jnp.concatenate([p[n] for p in parts], axis=SHARD_AXIS[n]) for n in parts[0]}

    class Exchange:
        gather = _gather_side(_pack_b(w_sh, BF16))

        @staticmethod
        def weights(gathered):
            return whole([_unpack_b(gathered[k]) for k in range(N_CHIPS)])

        @staticmethod
        def scatter(gw):
            by_chip = {n: _split_shards(n, gw[n]) for n in WEIGHTS}
            packed = jnp.stack([jnp.concatenate([_pack_a({n: by_chip[n][k] for n in WEIGHTS}, BF16),
                                                 _pack_b({n: by_chip[n][k] for n in WEIGHTS}, BF16)], axis=0)
                                for k in range(N_CHIPS)])
            hr = PACK_ROWS // 2
            mine = lax.dynamic_slice_in_dim(packed, lax.axis_index("c") * hr, hr, axis=1)
            return _scatter_side(_add2(mine, _swap_halves(packed), "chip_sum"))

    gathered_a = _gather_shards(_pack_a(w_sh, BF16))
    wb = whole([_unpack_a(gathered_a[k]) for k in range(N_CHIPS)])
    loss, grad_x, gw, gs, by_chip = _local_step(x[0], mem[0], loss_target[0], wb, small, Exchange)
    g_out = {n: g[None] for n, g in _unpack(_sum4(by_chip, "shard_sum")).items()}
    d_out, m_out, v_out = {}, {}, {}
    for n in WEIGHTS:
        d_out[n], m_out[n], v_out[n] = _adam(args[n], g_out[n], args["m_" + n], args["v_" + n], "adam_" + n)

    small_all = _gather_small(_pack_small(gs, loss))
    sm = {n: args["m_" + n] for n in SMALL}
    sv = {n: args["v_" + n] for n in SMALL}
    gs_sum, sd, snm, snv = _adam_small(_pack_small(small), small_all, _pack_small(sm), _pack_small(sv))
    gs_o, sd_o, snm_o, snv_o = _unpack_small(gs_sum), _unpack_small(sd), _unpack_small(snm), _unpack_small(snv)

    names = ["g_mix", "w_in", "w_alpha_up", "b_alpha", "b_forget", "g_gla_head", "g_mem", "w_mem_kv", "w_gla_o", "w_fox_o",
             "w_mem_o", "w_out", "g_ffn", "w_ff1", "w_ff2", "g_final"]

    def pick(big, sml, n):
        return big[n] if n in big else sml[n]

    outs = [gs_sum[6, 0], grad_x[None]]
    for big, sml in ((g_out, gs_o), (d_out, sd_o), (m_out, snm_o), (v_out, snv_o)):
        outs += [pick(big, sml, n) for n in names]
    return tuple(outs)
```

```python
import functools

import numpy as np
import jax
import jax.numpy as jnp
from jax import lax
from jax.experimental import pallas as pl
from jax.experimental.pallas import tpu as pltpu

F32 = jnp.float32
BF16 = jnp.bfloat16
HI = lax.Precision.HIGHEST
MESH = pl.DeviceIdType.MESH

EPS = 1e-6
D = 1024
CHUNK = 64
GLA_TAU = 16.0
N_CHIPS = 4
N_DEV = 8
VMEM_LIMIT_BYTES = 56 * 1024 * 1024

ADAM_LR, ADAM_B1, ADAM_B2, ADAM_EPS, ADAM_WD, ADAM_STEP = 0.001, 0.9, 0.999, 1e-08, 0.01, 10

PM_W = 6656
PE_W = 128
C_GQ, C_GK, C_GV, C_GG, C_FQ, C_FK, C_FV, C_MQ = 3072, 3328, 3584, 4096, 4608, 5120, 5632, 6144
FF_LANE = 16

WEIGHTS = ("w_in", "w_alpha_up", "w_mem_kv", "w_gla_o", "w_fox_o", "w_mem_o", "w_out", "w_ff1", "w_ff2")
SHARD_AXIS = {"w_in": 1, "w_alpha_up": 1, "w_mem_kv": 0, "w_gla_o": 1, "w_fox_o": 1, "w_mem_o": 1, "w_out": 0,
              "w_ff1": 1, "w_ff2": 0}
SMALL = ("g_mix", "g_mem", "g_ffn", "g_final", "b_alpha", "b_forget", "g_gla_head")
PACK_W = 1024
PACK_ROWS_A = 2048
PACK_ROWS_B = 3104
PACK_ROWS = PACK_ROWS_A + PACK_ROWS_B


def _cp(*sem):
    return pltpu.CompilerParams(dimension_semantics=sem, vmem_limit_bytes=VMEM_LIMIT_BYTES)


def _dot(a, b, **kw):
    return jnp.dot(a, b, preferred_element_type=F32, **kw)


def _dot_nt(a, b, **kw):
    return lax.dot_general(a, b, (((1,), (1,)), ((), ())), preferred_element_type=F32, **kw)


def _dot_tn(a, b, **kw):
    return lax.dot_general(a, b, (((0,), (0,)), ((), ())), preferred_element_type=F32, **kw)


def _sigmoid(x):
    return 1.0 / (1.0 + jnp.exp(-x))


def _log_sigmoid(x):
    return -(jnp.maximum(-x, 0.0) + jnp.log1p(jnp.exp(-jnp.abs(x))))


def _fold8(x):
    m, n = x.shape
    return x.reshape(m // 8, 8, n).sum(axis=0)


def _iota(shape, dim):
    return lax.broadcasted_iota(jnp.int32, shape, dim)


def _row_tile(s):
    return min(s, 512)


class _Side:
    def __init__(self, inputs, out_shape, scratch, ops):
        self.inputs, self.out_shape, self.scratch, self.ops = list(inputs), list(out_shape), list(scratch), ops


ANY_SPEC = pl.BlockSpec(memory_space=pl.ANY)


def _mm_nn(a, b, *, out_dtype, tm, tn, tk, name, a_fn=None, epi=None, extra=None, side=None):
    m, k = a.shape
    _, n = b.shape
    nk = k // tk
    n_in = 2 + (extra is not None)
    n_sin = 0 if side is None else len(side.inputs)
    n_sout = 0 if side is None else len(side.out_shape)

    def body_one(*refs):
        a_ref, b_ref = refs[0], refs[1]
        o_ref = refs[n_in + n_sin]
        if side is not None:
            start, finish = side.ops(refs[n_in:n_in + n_sin], refs[n_in + n_sin + 1:n_in + n_sin + 1 + n_sout],
                                     refs[n_in + n_sin + 1 + n_sout:])
            pl.when((pl.program_id(0) == 0) & (pl.program_id(1) == 0))(start)
        at = a_ref[...] if a_fn is None else a_fn(a_ref[...])
        r = _dot(at, b_ref[...])
        if epi is not None:
            r = epi(r, None if extra is None else refs[2][...])
        o_ref[...] = r.astype(out_dtype)
        if side is not None:
            pl.when((pl.program_id(0) == m // tm - 1) & (pl.program_id(1) == n // tn - 1))(finish)

    if nk == 1:
        in_specs = [pl.BlockSpec((tm, k), lambda i, j: (i, 0)), pl.BlockSpec((k, tn), lambda i, j: (0, j))]
        args = [a, b]
        if extra is not None:
            in_specs.append(pl.BlockSpec((tm, tn), lambda i, j: (i, j)))
            args.append(extra)
        out_specs = pl.BlockSpec((tm, tn), lambda i, j: (i, j))
        out_shape = jax.ShapeDtypeStruct((m, n), out_dtype)
        if side is None:
            return pl.pallas_call(
                body_one, grid=(m // tm, n // tn), in_specs=in_specs, out_specs=out_specs, out_shape=out_shape,
                name=name, compiler_params=_cp("parallel", "parallel"))(*args)
        return pl.pallas_call(
            body_one, grid=(m // tm, n // tn), in_specs=in_specs + [ANY_SPEC] * n_sin,
            out_specs=[out_specs] + [ANY_SPEC] * n_sout, out_shape=[out_shape] + side.out_shape,
            scratch_shapes=side.scratch, name=name, compiler_params=_cp("arbitrary", "arbitrary"))(*args, *side.inputs)
    assert side is None

    def body(*refs):
        if extra is None:
            a_ref, b_ref, o_ref, acc = refs
            x_ref = None
        else:
            a_ref, b_ref, x_ref, o_ref, acc = refs
        kk = pl.program_id(2)

        @pl.when(kk == 0)
        def _():
            acc[...] = jnp.zeros_like(acc)

        at = a_ref[...]
        if a_fn is not None:
            at = a_fn(at)
        acc[...] += _dot(at, b_ref[...])

        @pl.when(kk == nk - 1)
        def _():
            r = acc[...]
            if epi is not None:
                r = epi(r, None if x_ref is None else x_ref[...])
            o_ref[...] = r.astype(out_dtype)

    in_specs = [pl.BlockSpec((tm, tk), lambda i, j, kk: (i, kk)), pl.BlockSpec((tk, tn), lambda i, j, kk: (kk, j))]
    args = [a, b]
    if extra is not None:
        in_specs.append(pl.BlockSpec((tm, tn), lambda i, j, kk: (i, j)))
        args.append(extra)
    return pl.pallas_call(
        body, grid=(m // tm, n // tn, nk), in_specs=in_specs,
        out_specs=pl.BlockSpec((tm, tn), lambda i, j, kk: (i, j)),
        out_shape=jax.ShapeDtypeStruct((m, n), out_dtype),
        scratch_shapes=[pltpu.VMEM((tm, tn), F32)], name=name,
        compiler_params=_cp("parallel", "parallel", "arbitrary"))(*args)


def _mm_tn(a, b, *, tm, tn, ts, name, a_fn=None):
    s, m = a.shape
    _, n = b.shape
    ns = s // ts

    def body(a_ref, b_ref, o_ref, acc):
        kk = pl.program_id(2)

        @pl.when(kk == 0)
        def _():
            acc[...] = jnp.zeros_like(acc)

        at = a_ref[...]
        if a_fn is not None:
            at = a_fn(at)
        acc[...] += _dot_tn(at, b_ref[...])

        @pl.when(kk == ns - 1)
        def _():
            o_ref[...] = acc[...]

    return pl.pallas_call(
        body, grid=(m // tm, n // tn, ns),
        in_specs=[pl.BlockSpec((ts, tm), lambda i, j, kk: (kk, i)), pl.BlockSpec((ts, tn), lambda i, j, kk: (kk, j))],
        out_specs=pl.BlockSpec((tm, tn), lambda i, j, kk: (i, j)),
        out_shape=jax.ShapeDtypeStruct((m, n), F32),
        scratch_shapes=[pltpu.VMEM((tm, tn), F32)], name=name,
        compiler_params=_cp("parallel", "parallel", "arbitrary"))(a, b)


def _mm_tn_cat(a, bs, *, ts, name):
    s, m = a.shape
    n = sum(b.shape[1] for b in bs)
    ns = s // ts
    nb = len(bs)

    def body(*refs):
        a_ref, b_refs, o_ref, acc = refs[0], refs[1:1 + nb], refs[1 + nb], refs[2 + nb]
        kk = pl.program_id(0)

        @pl.when(kk == 0)
        def _():
            acc[...] = jnp.zeros_like(acc)

        bt = b_refs[0][...] if nb == 1 else jnp.concatenate([r[...] for r in b_refs], axis=1)
        acc[...] += _dot_tn(a_ref[...], bt)

        @pl.when(kk == ns - 1)
        def _():
            o_ref[...] = acc[...]

    return pl.pallas_call(
        body, grid=(ns,),
        in_specs=[pl.BlockSpec((ts, m), lambda kk: (kk, 0))] + [pl.BlockSpec((ts, b.shape[1]), lambda kk: (kk, 0)) for b in bs],
        out_specs=pl.BlockSpec((m, n), lambda kk: (0, 0)), out_shape=jax.ShapeDtypeStruct((m, n), F32),
        scratch_shapes=[pltpu.VMEM((m, n), F32)], name=name, compiler_params=_cp("arbitrary"))(a, *bs)


def _relu2_bf16(t):
    r = jnp.maximum(t.astype(F32), 0.0)
    return (r * r).astype(BF16)


def _rms_fwd(x, g, name, side=None):
    s, d = x.shape
    tm = min(s, 512)
    n_sin = 0 if side is None else len(side.inputs)
    n_sout = 0 if side is None else len(side.out_shape)

    def body(x_ref, g_ref, *rest):
        u_ref, r_ref = rest[n_sin:n_sin + 2]
        if side is not None:
            start, finish = side.ops(rest[:n_sin], rest[n_sin + 2:n_sin + 2 + n_sout], rest[n_sin + 2 + n_sout:])
            pl.when(pl.program_id(0) == 0)(start)
        xv = x_ref[...]
        r = lax.rsqrt(jnp.mean(xv * xv, axis=-1, keepdims=True) + EPS)
        u_ref[...] = ((xv * r) * g_ref[...]).astype(BF16)
        r_ref[...] = r
        if side is not None:
            pl.when(pl.program_id(0) == s // tm - 1)(finish)

    side_in = [] if side is None else side.inputs
    return pl.pallas_call(
        body, grid=(s // tm,),
        in_specs=[pl.BlockSpec((tm, d), lambda i: (i, 0)), pl.BlockSpec((1, d), lambda i: (0, 0))] + [ANY_SPEC] * n_sin,
        out_specs=[pl.BlockSpec((tm, d), lambda i: (i, 0)), pl.BlockSpec((tm, 1), lambda i: (i, 0))] + [ANY_SPEC] * n_sout,
        out_shape=[jax.ShapeDtypeStruct((s, d), BF16), jax.ShapeDtypeStruct((s, 1), F32)]
        + ([] if side is None else side.out_shape),
        scratch_shapes=[] if side is None else side.scratch,
        name=name, compiler_params=_cp("parallel" if side is None else "arbitrary"))(x, g, *side_in)


def _mm_norm_bwd(a_parts, b, xin, r, g, dres, *, name, want_bf16, side=None):
    s = a_parts[0].shape[0]
    k = b.shape[0]
    na = len(a_parts)
    offs = [sum(p.shape[1] for p in a_parts[:i]) for i in range(na)]
    assert offs[-1] + a_parts[-1].shape[1] == k
    tm = min(s, 256)
    n_out = 3 if want_bf16 else 2
    n_sin = 0 if side is None else len(side.inputs)
    n_sout = 0 if side is None else len(side.out_shape)

    def body(*refs):
        a_refs = refs[:na]
        b_ref, x_ref, r_ref, g_ref, dres_ref = refs[na:na + 5]
        rest = refs[na + 5:]
        outs = rest[n_sin:n_sin + n_out]
        dx_ref, dg_ref = outs[0], outs[-1]
        if side is not None:
            start, finish = side.ops(rest[:n_sin], rest[n_sin + n_out:n_sin + n_out + n_sout], rest[n_sin + n_out + n_sout:])
            pl.when(pl.program_id(0) == 0)(start)

        @pl.when(pl.program_id(0) == 0)
        def _():
            dg_ref[...] = jnp.zeros_like(dg_ref)

        du = _dot(a_refs[0][...], b_ref[0:a_parts[0].shape[1], :])
        for a_ref, off, part in zip(a_refs[1:], offs[1:], a_parts[1:]):
            du = du + _dot(a_ref[...], b_ref[off:off + part.shape[1], :])
        xn = x_ref[...] * r_ref[...]
        dg_ref[...] += _fold8(du * xn)
        dxn = du * g_ref[...]
        dx = dres_ref[...] + r_ref[...] * (dxn - xn * jnp.mean(dxn * xn, axis=-1, keepdims=True))
        dx_ref[...] = dx
        if want_bf16:
            outs[1][...] = dx.astype(BF16)
        if side is not None:
            pl.when(pl.program_id(0) == s // tm - 1)(finish)

    row = lambda i: (i, 0)
    const = lambda i: (0, 0)
    out_specs = [pl.BlockSpec((tm, D), row)]
    out_shape = [jax.ShapeDtypeStruct((s, D), F32)]
    if want_bf16:
        out_specs.append(pl.BlockSpec((tm, D), row))
        out_shape.append(jax.ShapeDtypeStruct((s, D), BF16))
    out_specs.append(pl.BlockSpec((8, D), const))
    out_shape.append(jax.ShapeDtypeStruct((8, D), F32))
    side_in = [] if side is None else side.inputs
    return pl.pallas_call(
        body, grid=(s // tm,),
        in_specs=[pl.BlockSpec((tm, p.shape[1]), row) for p in a_parts]
        + [pl.BlockSpec((k, D), const, pipeline_mode=pl.Buffered(1)),
           pl.BlockSpec((tm, D), row), pl.BlockSpec((tm, 1), row), pl.BlockSpec((1, D), const),
           pl.BlockSpec((tm, D), row)] + [ANY_SPEC] * n_sin,
        out_specs=out_specs + [ANY_SPEC] * n_sout, out_shape=out_shape + ([] if side is None else side.out_shape),
        scratch_shapes=[] if side is None else side.scratch,
        name=name, compiler_params=_cp("arbitrary"))(*a_parts, b, xin, r, g, dres, *side_in)


def _gla_consts():
    lmask = _iota((4 * CHUNK, CHUNK), 0) % CHUNK >= _iota((4 * CHUNK, CHUNK), 1)
    hmask = _iota((256, 256), 0) // CHUNK == _iota((256, 256), 1) // CHUNK
    bd = _iota((256, 512), 0) // CHUNK == _iota((256, 512), 1) // 128
    return lmask, hmask, bd


def _fold_heads(x):
    return x[0:64] + x[64:128] + x[128:192] + x[192:256]


def _gla_decays(la, b_scr, dec_scr):
    tri = (_iota((CHUNK, CHUNK), 0) >= _iota((CHUNK, CHUNK), 1)).astype(BF16)
    ones = jnp.ones((CHUNK, 128), BF16)
    for c in range(la.shape[0] // CHUNK):
        la3 = _split3(la[CHUNK * c:CHUNK * (c + 1)])
        b_scr[CHUNK * c:CHUNK * (c + 1), :] = _sum3(_dot(tri, la3), 1)
        dec_scr[c] = jnp.exp(_sum3(_dot_tn(la3, ones), 0))


def _gla_chunk(b, qc, kc):
    bl = b[CHUNK - 1:CHUNK, :]
    ep, en, ek = jnp.exp(b), jnp.exp(-b), jnp.exp(bl - b)
    return bl, ep, en, ek, qc * ep, qc * en, kc * en, kc * ep, kc * ek


def _gla_fwd(pm, pe, wau_p, b_alpha):
    s = pm.shape[0]
    t = _row_tile(s)
    nc = t // CHUNK

    def body(q_ref, k_ref, v_ref, e_ref, wau_ref, ba_ref, o_ref, st_ref, state, b_scr, dec_scr):
        @pl.when(pl.program_id(0) == 0)
        def _():
            state[...] = jnp.zeros_like(state)

        z = _dot(e_ref[...].astype(BF16), wau_ref[...]) + ba_ref[...]
        _gla_decays(_log_sigmoid(z) * (1.0 / GLA_TAU), b_scr, dec_scr)
        lmask, hmask, bd = _gla_consts()

        def chunk(c, carry):
            rows = pl.ds(pl.multiple_of(c * CHUNK, CHUNK), CHUNK)
            qc = q_ref[rows, :].astype(F32) * 0.125
            kc = k_ref[rows, :].astype(F32)
            vc = v_ref[rows, :]
            _, _, _, _, qp, qn, kn, kp, kk = _gla_chunk(b_scr[rows, :], qc, kc)
            decb = jnp.concatenate([dec_scr[c]] * 4, axis=1)
            qs = jnp.where(hmask, jnp.concatenate([qp] * 4, axis=0), 0.0).astype(BF16)
            qns = jnp.where(hmask, jnp.concatenate([qn] * 4, axis=0), 0.0).astype(BF16)
            attn = jnp.where(lmask, _dot_nt(qs, kn.astype(BF16)), _dot_nt(qns, kp.astype(BF16))).astype(BF16)
            st = state[...]
            o_intra = _fold_heads(jnp.where(bd, _dot(attn, vc), 0.0))
            o_ref[rows, :] = o_intra + _dot(qp.astype(BF16), st.astype(BF16))
            for h in range(4):
                st_ref[c, :, 128 * h:128 * (h + 1)] = st[64 * h:64 * (h + 1), 128 * h:128 * (h + 1)]
            kv = jnp.where(bd, _dot_tn(kk.astype(BF16), vc), 0.0)
            state[...] = st * decb + kv
            return carry

        lax.fori_loop(0, nc, chunk, 0)

    return pl.pallas_call(
        body, grid=(s // t,),
        in_specs=[pl.BlockSpec((t, 256), lambda i: (i, C_GQ // 256)), pl.BlockSpec((t, 256), lambda i: (i, C_GK // 256)),
                  pl.BlockSpec((t, 512), lambda i: (i, C_GV // 512)), pl.BlockSpec((t, PE_W), lambda i: (i, 0)),
                  pl.BlockSpec((PE_W, 256), lambda i: (0, 0)), pl.BlockSpec((1, 256), lambda i: (0, 0))],
        out_specs=[pl.BlockSpec((t, 512), lambda i: (i, 0)), pl.BlockSpec((nc, CHUNK, 512), lambda i: (i, 0, 0))],
        out_shape=[jax.ShapeDtypeStruct((s, 512), F32), jax.ShapeDtypeStruct((s // CHUNK, CHUNK, 512), F32)],
        scratch_shapes=[pltpu.VMEM((256, 512), F32), pltpu.VMEM((t, 256), F32), pltpu.VMEM((nc, 256, 128), F32)],
        name="gla_fwd", compiler_params=_cp("arbitrary"))(pm, pm, pm, pe, wau_p, b_alpha)


def _gla_bwd(pm, pe, wau_p, wau_pt, b_alpha, do, states):
    s = pm.shape[0]
    t = _row_tile(s)
    nc = t // CHUNK
    nb = s // t

    def body(q_ref, k_ref, v_ref, e_ref, wau_ref, waut_ref, ba_ref, do_ref, st_ref,
             dq_ref, dk_ref, dv_ref, de_ref, dwau_ref, dba_ref, gstate, b_scr, db_scr, dec_scr):
        @pl.when(pl.program_id(0) == 0)
        def _():
            gstate[...] = jnp.zeros_like(gstate)
            dwau_ref[...] = jnp.zeros_like(dwau_ref)
            dba_ref[...] = jnp.zeros_like(dba_ref)

        eb = e_ref[...].astype(BF16)
        z = _dot(eb, wau_ref[...]) + ba_ref[...]
        _gla_decays(_log_sigmoid(z) * (1.0 / GLA_TAU), b_scr, dec_scr)
        lmask, hmask, bd = _gla_consts()
        last_row = _iota((CHUNK, 256), 0) == CHUNK - 1

        def chunk(cc, carry):
            c = nc - 1 - cc
            rows = pl.ds(pl.multiple_of(c * CHUNK, CHUNK), CHUNK)
            qc = q_ref[rows, :].astype(F32) * 0.125
            kc = k_ref[rows, :].astype(F32)
            vc = v_ref[rows, :]
            dob = do_ref[rows, :]
            bl, ep, en, ek, qp, qn, kn, kp, kk = _gla_chunk(b_scr[rows, :], qc, kc)
            decb = jnp.concatenate([dec_scr[c]] * 4, axis=1)
            qs = jnp.where(hmask, jnp.concatenate([qp] * 4, axis=0), 0.0).astype(BF16)
            qns = jnp.where(hmask, jnp.concatenate([qn] * 4, axis=0), 0.0).astype(BF16)
            knb, kpb = kn.astype(BF16), kp.astype(BF16)
            attn = jnp.where(lmask, _dot_nt(qs, knb), _dot_nt(qns, kpb)).astype(BF16)
            st = jnp.where(bd, jnp.concatenate([st_ref[c]] * 4, axis=0), 0.0)
            g = gstate[...]
            gb = g.astype(BF16)
            do_s = jnp.where(bd, jnp.concatenate([dob] * 4, axis=0), jnp.zeros((), BF16))
            dattn = _dot_nt(do_s, vc)
            dv_ref[rows, :] = (_dot_tn(attn, do_s) + _dot(kk.astype(BF16), gb)).astype(BF16)
            dac = jnp.where(lmask, dattn, 0.0).astype(BF16)
            daa = jnp.where(lmask, 0.0, dattn).astype(BF16)
            dqp = _fold_heads(jnp.where(hmask, _dot(dac, knb), 0.0)) + _dot_nt(dob, st.astype(BF16))
            dqn = _fold_heads(jnp.where(hmask, _dot(daa, kpb), 0.0))
            dkn = _dot_tn(dac, qs)
            dkp = _dot_tn(daa, qns)
            dkk = _dot_nt(vc, gb)
            ddec = _dot_nt(jnp.ones((8, 1536), BF16), _split3(g * st))[0:1, :]
            gstate[...] = decb * g + jnp.where(bd, _dot_tn(qp.astype(BF16), dob), 0.0)
            dq_ref[rows, :] = ((dqp * ep + dqn * en) * 0.125).astype(BF16)
            dk_ref[rows, :] = (dkn * en + dkp * ep + dkk * ek).astype(BF16)
            dek = dkk * kc * ek
            db = (dqp * qc + dkp * kc) * ep - (dqn * qc + dkn * kc) * en - dek
            dbl = jnp.sum(dek, axis=0, keepdims=True) + ddec * jnp.exp(bl)
            db_scr[rows, :] = db + jnp.where(last_row, dbl, 0.0)
            return carry

        lax.fori_loop(0, nc, chunk, 0)
        triu = (_iota((CHUNK, CHUNK), 0) <= _iota((CHUNK, CHUNK), 1)).astype(BF16)
        dla = jnp.concatenate([_sum3(_dot(triu, _split3(db_scr[CHUNK * c:CHUNK * (c + 1), :])), 1) for c in range(nc)], axis=0)
        dz = dla * (1.0 / GLA_TAU) * _sigmoid(-z)
        dzb = dz.astype(BF16)
        dwau_ref[...] += _dot_tn(eb, dzb)
        dba_ref[...] += _fold8(dz)
        de_ref[...] = _dot(dzb, waut_ref[...])

    rev = lambda i: nb - 1 - i
    return pl.pallas_call(
        body, grid=(nb,),
        in_specs=[pl.BlockSpec((t, 256), lambda i: (rev(i), C_GQ // 256)), pl.BlockSpec((t, 256), lambda i: (rev(i), C_GK // 256)),
                  pl.BlockSpec((t, 512), lambda i: (rev(i), C_GV // 512)), pl.BlockSpec((t, PE_W), lambda i: (rev(i), 0)),
                  pl.BlockSpec((PE_W, 256), lambda i: (0, 0)), pl.BlockSpec((256, PE_W), lambda i: (0, 0)),
                  pl.BlockSpec((1, 256), lambda i: (0, 0)), pl.BlockSpec((t, 512), lambda i: (rev(i), 0)),
                  pl.BlockSpec((nc, CHUNK, 512), lambda i: (rev(i), 0, 0))],
        out_specs=[pl.BlockSpec((t, 256), lambda i: (rev(i), 0)), pl.BlockSpec((t, 256), lambda i: (rev(i), 0)),
                   pl.BlockSpec((t, 512), lambda i: (rev(i), 0)), pl.BlockSpec((t, PE_W), lambda i: (rev(i), 0)),
                   pl.BlockSpec((PE_W, 256), lambda i: (0, 0)), pl.BlockSpec((8, 256), lambda i: (0, 0))],
        out_shape=[jax.ShapeDtypeStruct((s, 256), BF16), jax.ShapeDtypeStruct((s, 256), BF16),
                   jax.ShapeDtypeStruct((s, 512), BF16), jax.ShapeDtypeStruct((s, PE_W), F32),
                   jax.ShapeDtypeStruct((PE_W, 256), F32), jax.ShapeDtypeStruct((8, 256), F32)],
        scratch_shapes=[pltpu.VMEM((256, 512), F32), pltpu.VMEM((t, 256), F32), pltpu.VMEM((t, 256), F32),
                        pltpu.VMEM((nc, 256, 128), F32)],
        name="gla_bwd", compiler_params=_cp("arbitrary"))(pm, pm, pm, pe, wau_p, wau_pt, b_alpha, do, states)


def _fcum_fwd(pe, bias):
    s = pe.shape[0]
    t = min(s, 256)

    def body(e_ref, b_ref, f_ref, carry):
        @pl.when(pl.program_id(0) == 0)
        def _():
            carry[...] = jnp.zeros_like(carry)

        lf = _log_sigmoid(e_ref[...] + b_ref[...])
        tri = (_iota((t, t), 0) >= _iota((t, t), 1)).astype(BF16)
        f = _sum3(_dot(tri, _split3(lf)), 1) + carry[0:1, :]
        f_ref[...] = f
        carry[...] = jnp.broadcast_to(f[t - 1:t, :], carry.shape)

    return pl.pallas_call(
        body, grid=(s // t,),
        in_specs=[pl.BlockSpec((t, PE_W), lambda i: (i, 0)), pl.BlockSpec((1, PE_W), lambda i: (0, 0))],
        out_specs=pl.BlockSpec((t, PE_W), lambda i: (i, 0)),
        out_shape=jax.ShapeDtypeStruct((s, PE_W), F32), scratch_shapes=[pltpu.VMEM((8, PE_W), F32)],
        name="fcum_fwd", compiler_params=_cp("arbitrary"))(pe, bias)


def _fcum_bwd(pe, bias, df):
    s = pe.shape[0]
    t = min(s, 256)
    nb = s // t

    def body(e_ref, b_ref, df_ref, de_ref, db_ref, carry):
        @pl.when(pl.program_id(0) == 0)
        def _():
            carry[...] = jnp.zeros_like(carry)
            db_ref[...] = jnp.zeros_like(db_ref)

        triu = (_iota((t, t), 0) <= _iota((t, t), 1)).astype(BF16)
        dlf = _sum3(_dot(triu, _split3(df_ref[...])), 1) + carry[0:1, :]
        carry[...] = jnp.broadcast_to(dlf[0:1, :], carry.shape)
        lane = _iota((t, PE_W), 1)
        dff = jnp.where((lane >= FF_LANE) & (lane < FF_LANE + 8), dlf * _sigmoid(-(e_ref[...] + b_ref[...])), 0.0)
        de_ref[...] = dff
        db_ref[...] += _fold8(dff)

    rev = lambda i: (nb - 1 - i, 0)
    return pl.pallas_call(
        body, grid=(nb,),
        in_specs=[pl.BlockSpec((t, PE_W), rev), pl.BlockSpec((1, PE_W), lambda i: (0, 0)), pl.BlockSpec((t, PE_W), rev)],
        out_specs=[pl.BlockSpec((t, PE_W), rev), pl.BlockSpec((8, PE_W), lambda i: (0, 0))],
        out_shape=[jax.ShapeDtypeStruct((s, PE_W), F32), jax.ShapeDtypeStruct((8, PE_W), F32)],
        scratch_shapes=[pltpu.VMEM((8, PE_W), F32)],
        name="fcum_bwd", compiler_params=_cp("arbitrary"))(pe, bias, df)


FOX_WIDE = 1024


def _split3(x):
    hi = x.astype(BF16)
    r = x - hi.astype(F32)
    mid = r.astype(BF16)
    lo = (r - mid.astype(F32)).astype(BF16)
    return jnp.concatenate([hi, mid, lo], axis=1)


def _sum3(x, axis):
    n = x.shape[axis] // 3
    parts = [lax.slice_in_dim(x, n * p, n * (p + 1), axis=axis) for p in range(3)]
    return (parts[0] + parts[1]) + parts[2]


def _fox_tables():
    heads, lane = np.arange(8), np.arange(64)
    spread = np.zeros((512, 1024), np.float32)
    spread[(64 * heads[:, None] + lane).ravel(), (128 * heads[:, None] + lane).ravel()] = 1.0
    def place(src_lane0, dst_off, val):
        t = np.zeros((384, 1024), np.float32)
        for p in range(3):
            t[128 * p + src_lane0 + heads, 128 * heads + dst_off + p] = val
        return t
    def const(off, val):
        c = np.zeros((1, 1024), np.float32)
        for p in range(3):
            c[0, 128 * heads + off + p] = val
        return c
    rows = np.zeros((8, 128), np.float32)
    rows[heads, FF_LANE + heads] = 1.0
    bf = lambda a: jnp.asarray(a, BF16)
    return dict(spread=bf(spread),
                f_to_q=bf(place(FF_LANE, 64, 1.0)), f_to_k=bf(place(FF_LANE, 67, -1.0)), d_to_do=bf(place(0, 64, 1.0)),
                ones_q=jnp.asarray(const(67, 1.0)), ones_k=jnp.asarray(const(64, 1.0)), ones_v=jnp.asarray(const(64, -1.0)),
                rows=jnp.asarray(rows))


LOG2E = 1.4426950408889634


def _fox_prep(pm, f128, lse8, tb, *, backward):
    s = pm.shape[0]
    tm = _row_tile(s)

    def body(*refs):
        if backward:
            q_ref, f_ref, lse_ref, sp_ref, fq_ref, cq_ref, rows_ref, qa_ref = refs
            f = f_ref[...] * LOG2E - _dot_tn(lse_ref[...], rows_ref[...], precision=HI)
            q2 = (q_ref[...].astype(F32) * (0.125 * LOG2E)).astype(BF16)
            qa_ref[...] = (_dot(q2, sp_ref[...]) + _dot(_split3(f), fq_ref[...]) + cq_ref[...]).astype(BF16)
            return
        (q_ref, k_ref, v_ref, f_ref, sp_ref, fq_ref, fk_ref, cq_ref, ck_ref, cv_ref,
         qa_ref, ka_ref, va_ref, vt_ref, qt_ref, kt_ref) = refs
        f3 = _split3(f_ref[...] * LOG2E)
        q, k, v = q_ref[...].astype(F32), k_ref[...], v_ref[...]
        sp = sp_ref[...]
        qa_ref[...] = (_dot((q * (0.125 * LOG2E)).astype(BF16), sp) + _dot(f3, fq_ref[...]) + cq_ref[...]).astype(BF16)
        ka_ref[...] = (_dot(k, sp) + _dot(f3, fk_ref[...]) + ck_ref[...]).astype(BF16)
        va_ref[...] = (_dot(v, sp) + cv_ref[...]).astype(BF16)
        vt_ref[...] = v.T
        qt_ref[...] = (q * 0.125).astype(BF16).T
        kt_ref[...] = (k.astype(F32) * 0.125).astype(BF16).T

    row = lambda i: (i, 0)
    const = lambda i: (0, 0)
    blk = lambda c: pl.BlockSpec((tm, 512), lambda i: (i, c // 512))
    wide = pl.BlockSpec((tm, 1024), row)
    mat = lambda a: pl.BlockSpec(a.shape, const)
    if backward:
        ins = [pm, f128, lse8, tb["spread"], tb["f_to_q"], tb["ones_q"], tb["rows"]]
        in_specs = [blk(C_FQ), pl.BlockSpec((tm, 128), row), pl.BlockSpec((8, tm), lambda i: (0, i))] + [mat(a) for a in ins[3:]]
        out_specs, out_shape = wide, jax.ShapeDtypeStruct((s, 1024), BF16)
    else:
        ins = [pm, pm, pm, f128, tb["spread"], tb["f_to_q"], tb["f_to_k"], tb["ones_q"], tb["ones_k"], tb["ones_v"]]
        in_specs = [blk(C_FQ), blk(C_FK), blk(C_FV), pl.BlockSpec((tm, 128), row)] + [mat(a) for a in ins[4:]]
        tr = pl.BlockSpec((512, tm), lambda i: (0, i))
        out_specs = [wide, wide, wide, tr, tr, tr]
        out_shape = [jax.ShapeDtypeStruct((s, 1024), BF16)] * 3 + [jax.ShapeDtypeStruct((512, s), BF16)] * 3
    return pl.pallas_call(body, grid=(s // tm,), in_specs=in_specs, out_specs=out_specs, out_shape=out_shape,
                          name="fox_prep_bwd" if backward else "fox_prep", compiler_params=_cp("parallel"))(*ins)


def _fox_post(dqt, dkt, dvt, rowsum8, colsum8, tb):
    s = dqt.shape[1]
    tm = _row_tile(s)

    def body(dqt_ref, dkt_ref, dvt_ref, rs_ref, cs_ref, rows_ref, dfq_ref, dfk_ref, dfv_ref, df_ref):
        dfq_ref[...] = dqt_ref[...].T.astype(BF16)
        dfk_ref[...] = dkt_ref[...].T
        dfv_ref[...] = dvt_ref[...].T
        df_ref[...] = _dot_tn(rs_ref[...] - cs_ref[...], rows_ref[...], precision=HI)

    row = lambda i: (i, 0)
    tr = pl.BlockSpec((512, tm), lambda i: (0, i))
    out = pl.BlockSpec((tm, 512), row)
    heads = pl.BlockSpec((8, tm), lambda i: (0, i))
    return pl.pallas_call(
        body, grid=(s // tm,),
        in_specs=[tr, tr, tr, heads, heads, pl.BlockSpec((8, 128), lambda i: (0, 0))],
        out_specs=[out, out, out, pl.BlockSpec((tm, 128), row)],
        out_shape=[jax.ShapeDtypeStruct((s, 512), BF16)] * 3 + [jax.ShapeDtypeStruct((s, 128), F32)],
        name="fox_post", compiler_params=_cp("parallel"))(dqt, dkt, dvt, rowsum8, colsum8, tb["rows"])


def _fox_fwd(k_aug, q_aug, vt):
    s = k_aug.shape[0]
    nh = 8
    tk = _row_tile(s)
    tq = min(s, FOX_WIDE)
    per = tq // tk

    def body(k_ref, q_ref, v_ref, o_ref, lse_ref, sbuf):
        i = pl.program_id(1)
        qa = q_ref[...]

        def scores(j):
            return _dot_nt(k_ref[pl.ds(pl.multiple_of(j * tk, tk), tk), :], qa)

        ones_row = (_iota((16, tk), 0) == 0).astype(BF16)

        def update(st, j, carry):
            m, acc = carry
            m2 = jnp.maximum(m, jnp.max(st, axis=0, keepdims=True))
            p = jnp.exp2(st - m2)
            vj = jnp.concatenate([v_ref[:, pl.ds(pl.multiple_of(j * tk, tk), tk)], ones_row], axis=0)
            return m2, jnp.exp2(m - m2) * acc + _dot(vj, p.astype(BF16))

        def step(a, carry):
            sbuf[1] = scores(2 * a + 1)
            carry = update(sbuf[0], 2 * a, carry)
            sbuf[0] = scores(2 * a + 2)
            return update(sbuf[1], 2 * a + 1, carry)

        n = i * per
        sbuf[0] = scores(0)
        carry = (jnp.full((1, tq), -1e30, F32), jnp.zeros((80, tq), F32))
        carry = lax.fori_loop(0, n // 2, step, carry)
        tri = _iota((tk, tk), 0) <= _iota((tk, tk), 1)
        late = [_dot_nt(k_ref[pl.ds(pl.multiple_of((n + r) * tk, tk), tk), :], qa[r * tk:, :]) for r in range(1, per)]
        for r in range(per):
            st = sbuf[0] if r == 0 else late[r - 1]
            head = jnp.where(tri, st[:, :tk], -1e30)
            st = head if st.shape[1] == tk else jnp.concatenate([head, st[:, tk:]], axis=1)
            part = update(st, n + r, tuple(c[:, r * tk:] for c in carry))
            carry = part if r == 0 else tuple(jnp.concatenate([old[:, :r * tk], new], axis=1) for old, new in zip(carry, part))
        m, acc = carry
        l = acc[64:65]
        o_ref[...] = (acc[0:64] / l).astype(BF16)
        lse_ref[0] = m + jnp.log2(l)

    return pl.pallas_call(
        body, grid=(nh, s // tq),
        in_specs=[pl.BlockSpec((s, 128), lambda h, i: (0, h)), pl.BlockSpec((tq, 128), lambda h, i: (i, h)),
                  pl.BlockSpec((64, s), lambda h, i: (h, 0))],
        out_specs=[pl.BlockSpec((64, tq), lambda h, i: (h, i)), pl.BlockSpec((1, 1, tq), lambda h, i: (h, 0, i))],
        out_shape=[jax.ShapeDtypeStruct((512, s), BF16), jax.ShapeDtypeStruct((nh, 1, s), F32)],
        scratch_shapes=[pltpu.VMEM((2, tk, tq), F32)],
        name="fox_fwd", compiler_params=_cp("parallel", "arbitrary"))(k_aug, q_aug, vt)


def _fox_bwd(q_aug, do_aug, qt, dot_, k_aug, v_aug, kt):
    s = q_aug.shape[0]
    nh = 8
    tq = _row_tile(s)
    tk = min(s, FOX_WIDE)
    per = tk // tq
    nqb = s // tq

    def body(qa_ref, da_ref, qt_ref, dt_ref, ka_ref, va_ref, kt_ref, dq_ref, rs_ref, dk_ref, dv_ref, dfk_ref):
        j = pl.program_id(1)

        @pl.when(j == 0)
        def _():
            dq_ref[...] = jnp.zeros_like(dq_ref)
            rs_ref[...] = jnp.zeros_like(rs_ref)

        ones_row = (_iota((16, tk), 0) == 0).astype(BF16)
        ka, va = ka_ref[...], va_ref[...]
        ks = jnp.concatenate([kt_ref[...], ones_row], axis=0)
        tri = _iota((tq, tq), 0) >= _iota((tq, tq), 1)

        def tile(i, w, carry):
            masked = w is not None
            w = tk if w is None else w
            rows = pl.ds(pl.multiple_of(i * tq, tq), tq)
            sp = _dot_nt(qa_ref[rows, :], ka[:w])
            if masked:
                last = jnp.where(tri, sp[:, w - tq:], -1e30)
                sp = last if w == tq else jnp.concatenate([sp[:, :w - tq], last], axis=1)
            p = jnp.exp2(sp)
            dsb = (p * _dot_nt(da_ref[rows, :], va[:w])).astype(BF16)
            dq = _dot_nt(ks[:, :w], dsb)
            dq_ref[:, rows] += dq[0:64]
            rs_ref[0, :, rows] += dq[64:72]
            new = (_dot(jnp.concatenate([qt_ref[:, rows], ones_row[:, :tq]], axis=0), dsb), _dot(dt_ref[:, rows], p.astype(BF16)))
            if w == tk:
                return tuple(c + d for c, d in zip(carry, new))
            return tuple(jnp.concatenate([c[:, :w] + d, c[:, w:]], axis=1) for c, d in zip(carry, new))

        carry = (jnp.zeros((80, tk), F32), jnp.zeros((64, tk), F32))
        for r in range(per):
            carry = tile(j * per + r, (r + 1) * tq, carry)
        dk, dv = lax.fori_loop((j + 1) * per, nqb, lambda i, c: tile(i, None, c), carry)
        dk_ref[...] = dk[0:64].astype(BF16)
        dv_ref[...] = dv.astype(BF16)
        dfk_ref[0] = dk[64:65]

    head_cols = lambda h, j: (0, h)
    head_rows = lambda h, j: (h, 0)
    once = dict(pipeline_mode=pl.Buffered(1))
    return pl.pallas_call(
        body, grid=(nh, s // tk),
        in_specs=[pl.BlockSpec((s, 128), head_cols, **once), pl.BlockSpec((s, 128), head_cols, **once),
                  pl.BlockSpec((64, s), head_rows, **once), pl.BlockSpec((64, s), head_rows, **once),
                  pl.BlockSpec((tk, 128), lambda h, j: (j, h)), pl.BlockSpec((tk, 128), lambda h, j: (j, h)),
                  pl.BlockSpec((64, tk), lambda h, j: (h, j))],
        out_specs=[pl.BlockSpec((64, s), head_rows), pl.BlockSpec((1, 8, s), lambda h, j: (h, 0, 0)),
                   pl.BlockSpec((64, tk), lambda h, j: (h, j)),
                   pl.BlockSpec((64, tk), lambda h, j: (h, j)), pl.BlockSpec((1, 1, tk), lambda h, j: (h, 0, j))],
        out_shape=[jax.ShapeDtypeStruct((512, s), F32), jax.ShapeDtypeStruct((nh, 8, s), F32),
                   jax.ShapeDtypeStruct((512, s), BF16),
                   jax.ShapeDtypeStruct((512, s), BF16), jax.ShapeDtypeStruct((nh, 1, s), F32)],
        name="fox_bwd", compiler_params=_cp("parallel", "arbitrary"))(q_aug, do_aug, qt, dot_, k_aug, v_aug, kt)


MEM_SCALE = 128 ** -0.5


def _mem_attn_fwd(pm, mkv):
    s = pm.shape[0]
    t = _row_tile(s)
    nm = mkv.shape[0]

    def body(q_ref, mk_ref, mv_ref, o_ref):
        for h in range(4):
            cols = slice(128 * h, 128 * (h + 1))
            sc = _dot_nt(q_ref[:, cols], mk_ref[:, cols]) * MEM_SCALE
            p = jnp.exp(sc - jnp.max(sc, axis=-1, keepdims=True))
            p = p / jnp.sum(p, axis=-1, keepdims=True)
            o_ref[:, cols] = _dot(p.astype(BF16), mv_ref[:, cols]).astype(BF16)

    return pl.pallas_call(
        body, grid=(s // t,),
        in_specs=[pl.BlockSpec((t, 512), lambda i: (i, C_MQ // 512)), pl.BlockSpec((nm, 512), lambda i: (0, 0)),
                  pl.BlockSpec((nm, 512), lambda i: (0, 1))],
        out_specs=pl.BlockSpec((t, 512), lambda i: (i, 0)),
        out_shape=jax.ShapeDtypeStruct((s, 512), BF16),
        name="mem_attn_fwd", compiler_params=_cp("parallel"))(pm, mkv, mkv)


def _mem_attn_bwd(pm, mkv, do):
    s = pm.shape[0]
    t = _row_tile(s)
    nm = mkv.shape[0]

    def body(q_ref, mk_ref, mv_ref, do_ref, dq_ref, dmk_ref, dmv_ref):
        @pl.when(pl.program_id(0) == 0)
        def _():
            dmk_ref[...] = jnp.zeros_like(dmk_ref)
            dmv_ref[...] = jnp.zeros_like(dmv_ref)

        for h in range(4):
            cols = slice(128 * h, 128 * (h + 1))
            qh, kh, vh, doh = q_ref[:, cols], mk_ref[:, cols], mv_ref[:, cols], do_ref[:, cols]
            sc = _dot_nt(qh, kh) * MEM_SCALE
            p = jnp.exp(sc - jnp.max(sc, axis=-1, keepdims=True))
            p = p / jnp.sum(p, axis=-1, keepdims=True)
            pb = p.astype(BF16)
            dp = _dot_nt(doh, vh)
            ds = (p * (dp - jnp.sum(p * dp, axis=-1, keepdims=True)) * MEM_SCALE).astype(BF16)
            dq_ref[:, cols] = _dot(ds, kh).astype(BF16)
            dmk_ref[:, cols] += _dot_tn(ds, qh)
            dmv_ref[:, cols] += _dot_tn(pb, doh)

    return pl.pallas_call(
        body, grid=(s // t,),
        in_specs=[pl.BlockSpec((t, 512), lambda i: (i, C_MQ // 512)), pl.BlockSpec((nm, 512), lambda i: (0, 0)),
                  pl.BlockSpec((nm, 512), lambda i: (0, 1)), pl.BlockSpec((t, 512), lambda i: (i, 0))],
        out_specs=[pl.BlockSpec((t, 512), lambda i: (i, 0)), pl.BlockSpec((nm, 512), lambda i: (0, 0)),
                   pl.BlockSpec((nm, 512), lambda i: (0, 0))],
        out_shape=[jax.ShapeDtypeStruct((s, 512), BF16), jax.ShapeDtypeStruct((nm, 512), F32),
                   jax.ShapeDtypeStruct((nm, 512), F32)],
        name="mem_attn_bwd", compiler_params=_cp("arbitrary"))(pm, mkv, mkv, do)


def _gain_grad(dxn_g, x, r, name):
    m, d = x.shape

    def body(d_ref, x_ref, r_ref, o_ref):
        o_ref[...] = _fold8(d_ref[...] * (x_ref[...] * r_ref[...]))

    return pl.pallas_call(body, out_shape=jax.ShapeDtypeStruct((8, d), F32), name=name,
                          compiler_params=pltpu.CompilerParams(vmem_limit_bytes=VMEM_LIMIT_BYTES))(dxn_g, x, r)


def _head_norm(o, gh):
    xs, rs = [], []
    for h in range(4):
        oh = o[:, 128 * h:128 * (h + 1)]
        r = lax.rsqrt(jnp.mean(oh * oh, axis=-1, keepdims=True) + EPS)
        xs.append(oh * r)
        rs.append(r)
    return xs, rs


def _merge_fwd(x, pm, o_gla, o_fox_t, o_mem, g_head, wg, wf, wm, wo, g_ffn):
    s = x.shape[0]
    t = min(s, 256)

    def body(x_ref, g0_ref, g1_ref, g2_ref, gg_ref, og_ref, of_ref, om_ref, gh_ref, wg_ref, wf_ref, wm_ref, wo_ref, gf_ref,
             mg_ref, h1_ref, u2_ref, r2_ref):
        xs, _ = _head_norm(og_ref[...], None)
        gg = gg_ref[...].astype(F32)
        sil = gg * _sigmoid(gg)
        ogn = jnp.concatenate(xs, axis=1) * gh_ref[...] * sil
        merged = (_sigmoid(g0_ref[...].astype(F32)) * _dot(ogn.astype(BF16), wg_ref[...])
                  + _sigmoid(g1_ref[...].astype(F32)) * _dot(of_ref[...].T, wf_ref[...])
                  + _sigmoid(g2_ref[...].astype(F32)) * _dot(om_ref[...], wm_ref[...]))
        mb = merged.astype(BF16)
        mg_ref[...] = mb
        h1 = x_ref[...] + _dot(mb, wo_ref[...])
        h1_ref[...] = h1
        r = lax.rsqrt(jnp.mean(h1 * h1, axis=-1, keepdims=True) + EPS)
        u2_ref[...] = ((h1 * r) * gf_ref[...]).astype(BF16)
        r2_ref[...] = r

    row = lambda i: (i, 0)
    const = lambda i: (0, 0)
    return pl.pallas_call(
        body, grid=(s // t,),
        in_specs=[pl.BlockSpec((t, D), row), pl.BlockSpec((t, D), lambda i: (i, 0)), pl.BlockSpec((t, D), lambda i: (i, 1)),
                  pl.BlockSpec((t, D), lambda i: (i, 2)), pl.BlockSpec((t, 512), lambda i: (i, C_GG // 512)),
                  pl.BlockSpec((t, 512), row), pl.BlockSpec((512, t), lambda i: (0, i)), pl.BlockSpec((t, 512), row),
                  pl.BlockSpec((1, 512), const), pl.BlockSpec((512, D), const), pl.BlockSpec((512, D), const),
                  pl.BlockSpec((512, D), const), pl.BlockSpec((D, D), const), pl.BlockSpec((1, D), const)],
        out_specs=[pl.BlockSpec((t, D), row), pl.BlockSpec((t, D), row), pl.BlockSpec((t, D), row), pl.BlockSpec((t, 1), row)],
        out_shape=[jax.ShapeDtypeStruct((s, D), BF16), jax.ShapeDtypeStruct((s, D), F32),
                   jax.ShapeDtypeStruct((s, D), BF16), jax.ShapeDtypeStruct((s, 1), F32)],
        name="merge_fwd", compiler_params=_cp("parallel"))(x, pm, pm, pm, pm, o_gla, o_fox_t, o_mem, g_head, wg, wf, wm, wo, g_ffn)


def _merge_bwd(dh1b, pm, o_gla, o_fox_t, o_mem, g_head, wg, wf, wm, wgt, wft, wmt, wot, spread, d_to_do):
    s = dh1b.shape[0]
    t = min(s, 256)

    def body(dh_ref, g0_ref, g1_ref, g2_ref, gg_ref, og_ref, of_ref, om_ref, gh_ref, wg_ref, wf_ref, wm_ref,
             wgt_ref, wft_ref, wmt_ref, wot_ref, sp_ref, dd_ref,
             dgt_ref, dgg_ref, dog_ref, da_ref, dot_ref, dom_ref, dwg_ref, dwf_ref, dwm_ref, dgh_ref):
        @pl.when(pl.program_id(0) == 0)
        def _():
            dwg_ref[...] = jnp.zeros_like(dwg_ref)
            dwf_ref[...] = jnp.zeros_like(dwf_ref)
            dwm_ref[...] = jnp.zeros_like(dwm_ref)
            dgh_ref[...] = jnp.zeros_like(dgh_ref)

        dmerged = _dot(dh_ref[...], wot_ref[...])
        og = og_ref[...]
        xs, rs = _head_norm(og, None)
        on = jnp.concatenate(xs, axis=1)
        gg = gg_ref[...].astype(F32)
        sg = _sigmoid(gg)
        sil = gg * sg
        gh = gh_ref[...]
        ognb = (on * gh * sil).astype(BF16)
        ofb, omb = of_ref[...].T, om_ref[...]
        douts = []
        for idx, (gref, ob, w_ref, wt_ref, dw_ref) in enumerate((
                (g0_ref, ognb, wg_ref, wgt_ref, dwg_ref), (g1_ref, ofb, wf_ref, wft_ref, dwf_ref),
                (g2_ref, omb, wm_ref, wmt_ref, dwm_ref))):
            gt = _sigmoid(gref[...].astype(F32))
            y = _dot(ob, w_ref[...])
            dgt_ref[:, D * idx:D * (idx + 1)] = (dmerged * y * gt * (1.0 - gt)).astype(BF16)
            dy = (gt * dmerged).astype(BF16)
            dw_ref[...] += _dot_tn(ob, dy)
            douts.append(_dot(dy, wt_ref[...]))
        dogn, dof, dom = douts
        dofb = dof.astype(BF16)
        dom_ref[...] = dom.astype(BF16)
        ind = (_iota((1536, 128), 0) % 512 // 64 == _iota((1536, 128), 1)).astype(BF16)
        delta = _dot(_split3(dofb.astype(F32) * ofb.astype(F32)), ind)
        da_ref[...] = (_dot(dofb, sp_ref[...]) + _dot(_split3(delta), dd_ref[...])).astype(BF16)
        dot_ref[...] = dofb.T
        dgg_ref[...] = (dogn * on * gh * (sg * (1.0 + gg * (1.0 - sg)))).astype(BF16)
        d_on = dogn * sil
        dgh_ref[...] += _fold8(d_on * on)
        dxn = d_on * gh
        outs = []
        for h in range(4):
            cols = slice(128 * h, 128 * (h + 1))
            dh_, xh = dxn[:, cols], xs[h]
            outs.append(rs[h] * (dh_ - xh * jnp.mean(dh_ * xh, axis=-1, keepdims=True)))
        dog_ref[...] = jnp.concatenate(outs, axis=1).astype(BF16)

    row = lambda i: (i, 0)
    const = lambda i: (0, 0)
    return pl.pallas_call(
        body, grid=(s // t,),
        in_specs=[pl.BlockSpec((t, D), row), pl.BlockSpec((t, D), lambda i: (i, 0)), pl.BlockSpec((t, D), lambda i: (i, 1)),
                  pl.BlockSpec((t, D), lambda i: (i, 2)), pl.BlockSpec((t, 512), lambda i: (i, C_GG // 512)),
                  pl.BlockSpec((t, 512), row), pl.BlockSpec((512, t), lambda i: (0, i)), pl.BlockSpec((t, 512), row),
                  pl.BlockSpec((1, 512), const), pl.BlockSpec((512, D), const), pl.BlockSpec((512, D), const),
                  pl.BlockSpec((512, D), const), pl.BlockSpec((D, 512), const), pl.BlockSpec((D, 512), const),
                  pl.BlockSpec((D, 512), const), pl.BlockSpec((D, D), const),
                  pl.BlockSpec((512, 1024), const), pl.BlockSpec((384, 1024), const)],
        out_specs=[pl.BlockSpec((t, 3 * D), row), pl.BlockSpec((t, 512), row), pl.BlockSpec((t, 512), row),
                   pl.BlockSpec((t, 1024), row), pl.BlockSpec((512, t), lambda i: (0, i)), pl.BlockSpec((t, 512), row),
                   pl.BlockSpec((512, D), const), pl.BlockSpec((512, D), const), pl.BlockSpec((512, D), const),
                   pl.BlockSpec((8, 512), const)],
        out_shape=[jax.ShapeDtypeStruct((s, 3 * D), BF16), jax.ShapeDtypeStruct((s, 512), BF16),
                   jax.ShapeDtypeStruct((s, 512), BF16), jax.ShapeDtypeStruct((s, 1024), BF16),
                   jax.ShapeDtypeStruct((512, s), BF16), jax.ShapeDtypeStruct((s, 512), BF16),
                   jax.ShapeDtypeStruct((512, D), F32), jax.ShapeDtypeStruct((512, D), F32),
                   jax.ShapeDtypeStruct((512, D), F32), jax.ShapeDtypeStruct((8, 512), F32)],
        name="merge_bwd", compiler_params=_cp("arbitrary"))(
            dh1b, pm, pm, pm, pm, o_gla, o_fox_t, o_mem, g_head, wg, wf, wm, wgt, wft, wmt, wot, spread, d_to_do)


def _ff2_loss(a, w2, h1, g_final, target):
    s, k = a.shape
    tm = min(s, 256)

    def body(a_ref, w_ref, h1_ref, g_ref, t_ref, dh_ref, dhb_ref, loss_ref, dg_ref):
        @pl.when(pl.program_id(0) == 0)
        def _():
            loss_ref[...] = jnp.zeros_like(loss_ref)
            dg_ref[...] = jnp.zeros_like(dg_ref)

        h2 = h1_ref[...] + _dot(_relu2_bf16(a_ref[...]), w_ref[...])
        r = lax.rsqrt(jnp.mean(h2 * h2, axis=-1, keepdims=True) + EPS)
        xn = h2 * r
        g = g_ref[...]
        err = xn * g - t_ref[...]
        e2 = _fold8(err * err)
        part = e2[:, 0:128]
        for c in range(1, D // 128):
            part = part + e2[:, 128 * c:128 * (c + 1)]
        loss_ref[...] += part
        dy = err * (1.0 / D)
        dg_ref[...] += _fold8(dy * xn)
        dxn = dy * g
        dh = r * (dxn - xn * jnp.mean(dxn * xn, axis=-1, keepdims=True))
        dh_ref[...] = dh
        dhb_ref[...] = dh.astype(BF16)

    row = lambda i: (i, 0)
    const = lambda i: (0, 0)
    return pl.pallas_call(
        body, grid=(s // tm,),
        in_specs=[pl.BlockSpec((tm, k), row), pl.BlockSpec((k, D), const, pipeline_mode=pl.Buffered(1)),
                  pl.BlockSpec((tm, D), row), pl.BlockSpec((1, D), const), pl.BlockSpec((tm, D), row)],
        out_specs=[pl.BlockSpec((tm, D), row), pl.BlockSpec((tm, D), row), pl.BlockSpec((8, 128), const),
                   pl.BlockSpec((8, D), const)],
        out_shape=[jax.ShapeDtypeStruct((s, D), F32), jax.ShapeDtypeStruct((s, D), BF16),
                   jax.ShapeDtypeStruct((8, 128), F32), jax.ShapeDtypeStruct((8, D), F32)],
        name="ff2_loss", compiler_params=_cp("arbitrary"))(a, w2, h1, g_final, target)


def _adam(w, g, m, v, name):
    _, r, c = w.shape
    tr = r
    for cand in (512, 256, 128, 64, 32, 16, 8):
        if r % cand == 0 and cand * c * 4 <= (1 << 20):
            tr = cand
            break
    c1 = 1.0 - ADAM_B1 ** ADAM_STEP
    c2 = 1.0 - ADAM_B2 ** ADAM_STEP

    def body(w_ref, g_ref, m_ref, v_ref, d_ref, nm_ref, nv_ref):
        gv = g_ref[...]
        nm = ADAM_B1 * m_ref[...] + (1.0 - ADAM_B1) * gv
        nv = ADAM_B2 * v_ref[...] + (1.0 - ADAM_B2) * (gv * gv)
        d_ref[...] = -ADAM_LR * ((nm / c1) / (jnp.sqrt(nv / c2) + ADAM_EPS) + ADAM_WD * w_ref[...])
        nm_ref[...] = nm
        nv_ref[...] = nv

    spec = pl.BlockSpec((1, tr, c), lambda i: (0, i, 0))
    return pl.pallas_call(
        body, grid=(r // tr,), in_specs=[spec] * 4, out_specs=[spec] * 3,
        out_shape=[jax.ShapeDtypeStruct((1, r, c), F32)] * 3, name=name, compiler_params=_cp("parallel"))(w, g, m, v)


def _row_block(r):
    return max(d for d in range(16, 513, 16) if r % d == 0)


def _add2(a, b, name):
    n, r, c = a.shape
    tr = _row_block(r)

    def body(a_ref, b_ref, o_ref):
        o_ref[...] = (a_ref[...].astype(F32) + b_ref[...].astype(F32)).astype(BF16)

    spec = pl.BlockSpec((1, tr, c), lambda k, i: (k, i, 0))
    return pl.pallas_call(body, grid=(n, r // tr), in_specs=[spec, spec], out_specs=spec,
                          out_shape=jax.ShapeDtypeStruct((n, r, c), BF16), name=name,
                          compiler_params=_cp("parallel", "parallel"))(a, b)


def _sum4(a, name):
    _, r, c = a.shape
    tr = _row_block(r)

    def body(a_ref, o_ref):
        o_ref[...] = ((a_ref[0].astype(F32) + a_ref[1].astype(F32)) + a_ref[2].astype(F32)) + a_ref[3].astype(F32)

    return pl.pallas_call(body, grid=(r // tr,), in_specs=[pl.BlockSpec((4, tr, c), lambda i: (0, i, 0))],
                          out_specs=pl.BlockSpec((tr, c), lambda i: (i, 0)),
                          out_shape=jax.ShapeDtypeStruct((r, c), F32), name=name, compiler_params=_cp("parallel"))(a)


def _adam_small(w, gathered, m, v):
    c1 = 1.0 - ADAM_B1 ** ADAM_STEP
    c2 = 1.0 - ADAM_B2 ** ADAM_STEP

    def body(w_ref, g_ref, m_ref, v_ref, gs_ref, d_ref, nm_ref, nv_ref):
        gv = g_ref[0]
        for dev in range(1, N_DEV):
            gv = gv + g_ref[dev]
        gs_ref[...] = gv
        nm = ADAM_B1 * m_ref[...] + (1.0 - ADAM_B1) * gv
        nv = ADAM_B2 * v_ref[...] + (1.0 - ADAM_B2) * (gv * gv)
        d_ref[...] = -ADAM_LR * ((nm / c1) / (jnp.sqrt(nv / c2) + ADAM_EPS) + ADAM_WD * w_ref[...])
        nm_ref[...] = nm
        nv_ref[...] = nv

    return pl.pallas_call(body, out_shape=[jax.ShapeDtypeStruct((8, D), F32)] * 4, name="adam_small")(w, gathered, m, v)


def _place():
    return lax.axis_index("x"), lax.axis_index("y"), lax.axis_index("c")


def _other_chips(x, y):
    return [(1 - x, y), (x, 1 - y), (1 - x, 1 - y)]


GATHER_SEMS = [pltpu.SemaphoreType.DMA((6,)), pltpu.SemaphoreType.DMA((6,)), pltpu.SemaphoreType.DMA]


def _gather_ops(in_refs, out_refs, sems):
    (p_ref,), (out_ref,) = in_refs, out_refs
    send_sems, recv_sems, local_sem = sems
    hr = p_ref.shape[0] // 2
    x, y, cc = _place()
    sibling = (x, y, 1 - cc)
    chips = _other_chips(x, y)

    def half(chip, core):
        return out_ref.at[2 * chip[0] + chip[1], pl.ds(core * hr, hr), :]

    def copy(k, chip, core, to, src=None):
        return pltpu.make_async_remote_copy(
            src_ref=half(chip, core) if src is None else src, dst_ref=half(chip, core),
            send_sem=send_sems.at[k], recv_sem=recv_sems.at[k], device_id=to, device_id_type=MESH)

    mine = pltpu.make_async_copy(p_ref, out_ref.at[2 * x + y], local_sem)
    my_half = p_ref.at[pl.ds(cc * hr, hr), :]
    first = [copy(j, (x, y), cc, (*chip, cc), src=my_half) for j, chip in enumerate(chips)]
    passed = [copy(3 + j, chip, cc, sibling) for j, chip in enumerate(chips)]

    def start():
        mine.start()
        for cp in first:
            cp.start()

    def finish():
        for j, chip in enumerate(chips):
            copy(j, chip, cc, (x, y, cc)).wait_recv()
            passed[j].start()
        for j, chip in enumerate(chips):
            copy(3 + j, chip, 1 - cc, (x, y, cc)).wait_recv()
        for cp in first + passed:
            cp.wait_send()
        mine.wait()

    return start, finish


def _gather_side(p):
    return _Side([p], [jax.ShapeDtypeStruct((N_CHIPS,) + p.shape, p.dtype)], GATHER_SEMS, _gather_ops)


def _swap_halves(g):
    n, r, c = g.shape
    hr = r // 2

    def body(g_ref, out_ref, send_sem, recv_sem):
        x, y, cc = _place()
        cp = pltpu.make_async_remote_copy(
            src_ref=g_ref.at[:, pl.ds((1 - cc) * hr, hr), :], dst_ref=out_ref,
            send_sem=send_sem, recv_sem=recv_sem, device_id=(x, y, 1 - cc), device_id_type=MESH)
        cp.start()
        cp.wait()

    any_spec = pl.BlockSpec(memory_space=pl.ANY)
    return pl.pallas_call(
        body, out_shape=jax.ShapeDtypeStruct((n, hr, c), g.dtype), in_specs=[any_spec], out_specs=any_spec,
        scratch_shapes=[pltpu.SemaphoreType.DMA, pltpu.SemaphoreType.DMA], name="swap_halves")(g)


SCATTER_SEMS = [pltpu.SemaphoreType.DMA((7,)), pltpu.SemaphoreType.DMA((7,)), pltpu.SemaphoreType.DMA]


def _scatter_ops(in_refs, out_refs, sems):
    (p_ref,), (out_ref,) = in_refs, out_refs
    send_sems, recv_sems, local_sem = sems
    hr = p_ref.shape[1]
    x, y, cc = _place()
    me = 2 * x + y
    sibling = (x, y, 1 - cc)
    chips = _other_chips(x, y)
    ids = [2 * chip[0] + chip[1] for chip in chips]

    def land(src, core):
        return out_ref.at[src, pl.ds(core * hr, hr), :]

    def copy(k, src_ref, dst_ref, to):
        return pltpu.make_async_remote_copy(src_ref=src_ref, dst_ref=dst_ref, send_sem=send_sems.at[k],
                                            recv_sem=recv_sems.at[k], device_id=to, device_id_type=MESH)

    mine = pltpu.make_async_copy(p_ref.at[me], land(me, cc), local_sem)
    sends = [copy(j, p_ref.at[ids[j]], land(me, cc), (*chip, cc)) for j, chip in enumerate(chips)]
    sends.append(copy(3, p_ref.at[me], land(me, cc), sibling))
    passed = [copy(4 + j, land(ids[j], cc), land(ids[j], cc), sibling) for j in range(3)]

    def start():
        mine.start()
        for cp in sends:
            cp.start()

    def finish():
        for j in range(3):
            copy(j, p_ref.at[me], land(ids[j], cc), (x, y, cc)).wait_recv()
            passed[j].start()
        copy(3, p_ref.at[me], land(me, 1 - cc), (x, y, cc)).wait_recv()
        for j in range(3):
            copy(4 + j, p_ref.at[me], land(ids[j], 1 - cc), (x, y, cc)).wait_recv()
        for cp in sends + passed:
            cp.wait_send()
        mine.wait()

    return start, finish


def _scatter_side(p):
    n, hr, c = p.shape
    return _Side([p], [jax.ShapeDtypeStruct((n, 2 * hr, c), p.dtype)], SCATTER_SEMS, _scatter_ops)


def _gather_small(blk):
    m, n = blk.shape

    def body(x_ref, out_ref, send_sems, recv_sems, local_sem):
        x, y, cc = _place()
        me, sibling = (x, y, cc), (x, y, 1 - cc)
        chips = _other_chips(x, y)

        def slot(px, py, pc):
            return out_ref.at[4 * px + 2 * py + pc]

        def copy(k, block, to, src=None):
            return pltpu.make_async_remote_copy(
                src_ref=slot(*block) if src is None else src, dst_ref=slot(*block),
                send_sem=send_sems.at[k], recv_sem=recv_sems.at[k], device_id=to, device_id_type=MESH)

        mine = pltpu.make_async_copy(x_ref, slot(*me), local_sem)
        mine.start()
        first = [copy(0, me, sibling, src=x_ref)]
        first += [copy(1 + j, me, (*chip, cc), src=x_ref) for j, chip in enumerate(chips)]
        for cp in first:
            cp.start()
        passed = [copy(4 + j, (*chip, cc), sibling) for j, chip in enumerate(chips)]
        for j, chip in enumerate(chips):
            copy(1 + j, (*chip, cc), me).wait_recv()
            passed[j].start()
        copy(0, sibling, me).wait_recv()
        for j, chip in enumerate(chips):
            copy(4 + j, (*chip, 1 - cc), me).wait_recv()
        for cp in first + passed:
            cp.wait_send()
        mine.wait()

    vmem = pl.BlockSpec(memory_space=pltpu.VMEM)
    return pl.pallas_call(
        body, out_shape=jax.ShapeDtypeStruct((N_DEV, m, n), blk.dtype), in_specs=[vmem], out_specs=vmem,
        scratch_shapes=[pltpu.SemaphoreType.DMA((7,)), pltpu.SemaphoreType.DMA((7,)), pltpu.SemaphoreType.DMA],
        name="gather_small")(blk)


def _shard_shape(name, full_shape):
    shp = list(full_shape)
    shp[SHARD_AXIS[name]] //= N_CHIPS
    return tuple(shp)


FULL_SHAPES = {"w_in": (D, 6680), "w_alpha_up": (16, 256), "w_mem_kv": (D, D), "w_gla_o": (512, D), "w_fox_o": (512, D),
               "w_mem_o": (512, D), "w_out": (D, D), "w_ff1": (D, 4 * D), "w_ff2": (4 * D, D)}


def _pack_a(sh, dtype):
    w = sh["w_in"].astype(dtype)
    return jnp.concatenate([w[:, 0:PACK_W], jnp.pad(w[:, PACK_W:], ((0, 0), (0, 2 * PACK_W - w.shape[1])))], axis=0)


def _pack_b(sh, dtype):
    o3 = jnp.concatenate([sh["w_gla_o"], sh["w_fox_o"], sh["w_mem_o"], jnp.zeros((512, 256), sh["w_gla_o"].dtype)], axis=1)
    au = jnp.pad(sh["w_alpha_up"], ((0, PACK_ROWS_B - 3072 - 16), (0, PACK_W - 64)))
    return jnp.concatenate([sh["w_ff1"], sh["w_ff2"], sh["w_mem_kv"], sh["w_out"], o3, au], axis=0).astype(dtype)


def _unpack_a(pa):
    return {"w_in": jnp.concatenate([pa[0:1024], pa[1024:2048, 0:1670 - PACK_W]], axis=1)}


def _unpack_b(pb):
    return {"w_ff1": pb[0:1024], "w_ff2": pb[1024:2048], "w_mem_kv": pb[2048:2304], "w_out": pb[2304:2560],
            "w_gla_o": pb[2560:3072, 0:256], "w_fox_o": pb[2560:3072, 256:512], "w_mem_o": pb[2560:3072, 512:768],
            "w_alpha_up": pb[3072:3088, 0:64]}


def _unpack(packed):
    return {**_unpack_a(packed[0:PACK_ROWS_A]), **_unpack_b(packed[PACK_ROWS_A:])}


def _split_shards(name, full):
    return jnp.split(full, N_CHIPS, axis=SHARD_AXIS[name])


def _pack_small(vals, scalar=None):
    row4 = jnp.concatenate([vals["b_alpha"].reshape(-1), vals["b_forget"].reshape(-1), jnp.zeros((D - 264,), F32)])
    row5 = jnp.concatenate([vals["g_gla_head"].reshape(-1), jnp.zeros((D - 512,), F32)])
    row6 = jnp.zeros((D,), F32) if scalar is None else jnp.broadcast_to(scalar, (D,))
    rows = [vals["g_mix"].reshape(-1), vals["g_mem"].reshape(-1), vals["g_ffn"].reshape(-1), vals["g_final"].reshape(-1),
            row4, row5, row6, jnp.zeros((D,), F32)]
    return jnp.stack(rows)


def _unpack_small(blk):
    return {"g_mix": blk[0].reshape(1, D), "g_mem": blk[1].reshape(1, D), "g_ffn": blk[2].reshape(1, D),
            "g_final": blk[3].reshape(D), "b_alpha": blk[4, 0:256].reshape(1, 256), "b_forget": blk[4, 256:264].reshape(1, 8),
            "g_gla_head": blk[5, 0:512].reshape(1, 4, 128)}


def _local_step(x, mem, target, wb, small, exchange=None):
    s = x.shape[0]
    nm = mem.shape[0]
    t = _row_tile(s)
    nb = s // t
    b_alpha = small["b_alpha"].reshape(1, 256)
    bias_e = jnp.concatenate([jnp.zeros((FF_LANE,), F32), small["b_forget"].reshape(-1),
                              jnp.zeros((PE_W - FF_LANE - 8,), F32)]).reshape(1, PE_W)
    g_mix, g_mem, g_ffn = small["g_mix"].reshape(1, D), small["g_mem"].reshape(1, D), small["g_ffn"].reshape(1, D)
    g_final = small["g_final"].reshape(1, D)
    g_head = small["g_gla_head"].reshape(1, 512)

    if exchange is None:
        u, r1 = _rms_fwd(x, g_mix, "norm_mix")
    else:
        u, r1, gathered = _rms_fwd(x, g_mix, "norm_mix", side=exchange.gather_a)
        wb = exchange.weights_a(gathered)
    w_in = wb["w_in"]
    w_main = jnp.concatenate([w_in[:, 3608:6680], w_in[:, 0:1536], w_in[:, 1552:3088], w_in[:, 3096:3608]], axis=1)
    w_e = jnp.concatenate([w_in[:, 1536:1552], w_in[:, 3088:3096], jnp.zeros((D, PE_W - 24), BF16)], axis=1)
    w_in_pt = jnp.concatenate([w_main, w_e], axis=1).T
    big = min(s, 1024)
    if exchange is None:
        pm = _mm_nn(u, w_main, out_dtype=BF16, tm=big, tn=PM_W // 4, tk=D, name="proj_main")
    else:
        pm, gathered = _mm_nn(u, w_main, out_dtype=BF16, tm=big, tn=PM_W // 4, tk=D, name="proj_main", side=exchange.gather_b)
        wb = {**wb, **exchange.weights_b(gathered)}
    wau_p = jnp.concatenate([wb["w_alpha_up"], jnp.zeros((PE_W - 16, 256), BF16)], axis=0)
    pe = _mm_nn(u, w_e, out_dtype=F32, tm=t, tn=PE_W, tk=D, name="proj_narrow")
    o_gla, states = _gla_fwd(pm, pe, wau_p, b_alpha)
    fcum = _fcum_fwd(pe, bias_e)
    tb = _fox_tables()
    qf_aug, k_aug, v_aug, vt, qt, kt = _fox_prep(pm, fcum, None, tb, backward=False)
    o_fox, lse = _fox_fwd(k_aug, qf_aug, vt)
    mn, rm = _rms_fwd(mem, g_mem, "norm_mem")
    mkv = _mm_nn(mn, wb["w_mem_kv"], out_dtype=BF16, tm=nm, tn=512, tk=D, name="mem_kv")
    o_mem = _mem_attn_fwd(pm, mkv)
    merged, h1, u2, r2 = _merge_fwd(x, pm, o_gla, o_fox, o_mem, g_head, wb["w_gla_o"], wb["w_fox_o"], wb["w_mem_o"],
                                    wb["w_out"], g_ffn)
    a = _mm_nn(u2, wb["w_ff1"], out_dtype=BF16, tm=big, tn=1024, tk=D, name="ff1")
    dh2, dh2b, loss8, dgfin8 = _ff2_loss(a, wb["w_ff2"], h1, g_final, target)
    loss = 0.5 * jnp.sum(loss8) / D

    da = _mm_nn(dh2b, wb["w_ff2"].T, out_dtype=BF16, tm=big, tn=1024, tk=D, name="d_act",
                epi=lambda acc, at: acc * (2.0 * jnp.maximum(at.astype(F32), 0.0)), extra=a)
    gw = {}
    gw["w_ff2"] = _mm_tn(a, dh2b, tm=1024, tn=D, ts=big, name="dw_ff2", a_fn=_relu2_bf16)
    gw["w_ff1"] = _mm_tn(u2, da, tm=D, tn=1024, ts=big, name="dw_ff1")
    dh1, dh1b, dgffn8 = _mm_norm_bwd([da], wb["w_ff1"].T, h1, r2, g_ffn, dh2, name="d_h1", want_bf16=True)
    gw["w_out"] = _mm_tn(merged, dh1b, tm=D, tn=D, ts=big, name="dw_out")
    (dgates, dgg, do_gla, do_aug, do_t, do_mem, gw["w_gla_o"], gw["w_fox_o"], gw["w_mem_o"], dgh8) = _merge_bwd(
        dh1b, pm, o_gla, o_fox, o_mem, g_head, wb["w_gla_o"], wb["w_fox_o"], wb["w_mem_o"],
        wb["w_gla_o"].T, wb["w_fox_o"].T, wb["w_mem_o"].T, wb["w_out"].T, tb["spread"], tb["d_to_do"])
    dgq, dgk, dgv, de_gla, dwau_p, dba8 = _gla_bwd(pm, pe, wau_p, wau_p.T, b_alpha, do_gla, states)
    gw["w_alpha_up"] = dwau_p[0:16, :]
    q_aug = _fox_prep(pm, fcum, lse.reshape(8, s), tb, backward=True)
    dfq_t, dfrow, dfk_t, dfv_t, dfcol = _fox_bwd(q_aug, do_aug, qt, do_t, k_aug, v_aug, kt)
    dfq, dfk, dfv, df = _fox_post(dfq_t, dfk_t, dfv_t, dfrow[:, 0, :], dfcol.reshape(8, s), tb)
    de_fox, dbf8 = _fcum_bwd(pe, bias_e, df)
    dmq, dmk, dmv = _mem_attn_bwd(pm, mkv, do_mem)
    dmkv = jnp.concatenate([dmk, dmv], axis=1).astype(BF16)
    gw["w_mem_kv"] = _mm_tn(mn, dmkv, tm=D, tn=D, ts=nm, name="dw_mem_kv")
    dmn_g = _mm_nn(dmkv, wb["w_mem_kv"].T, out_dtype=F32, tm=nm, tn=D, tk=D, name="d_mem_norm")
    dgmem8 = _gain_grad(dmn_g, mem, rm, "dg_mem")
    de = (de_gla + de_fox).astype(BF16)
    dproj = [dgates, dgq, dgk, dgv, dgg, dfq, dfk, dfv, dmq, de]
    dw_gates = _mm_tn(u, dgates, tm=D, tn=1024, ts=big, name="dw_in_gates")
    dw_g = _mm_tn_cat(u, [dgq, dgk, dgv], ts=big, name="dw_in_gla")
    dw_gf = _mm_tn_cat(u, [dgg, dfq], ts=big, name="dw_in_gg_fq")
    dw_f = _mm_tn_cat(u, [dfk, dfv], ts=big, name="dw_in_fk_fv")
    dw_m = _mm_tn_cat(u, [dmq, de], ts=big, name="dw_in_mq_narrow")
    gw["w_in"] = jnp.concatenate([dw_g, dw_gf[:, 0:512], dw_m[:, 512:528], dw_gf[:, 512:1024], dw_f,
                                  dw_m[:, 528:536], dw_m[:, 0:512], dw_gates], axis=1)
    if exchange is None:
        grad_x, dgmix8 = _mm_norm_bwd(dproj, w_in_pt, x, r1, g_mix, dh1, name="d_x", want_bf16=False)
        exchanged = None
    else:
        grad_x, dgmix8, exchanged = _mm_norm_bwd(dproj, w_in_pt, x, r1, g_mix, dh1, name="d_x", want_bf16=False,
                                                 side=exchange.scatter(gw))
    gs = {"g_mix": dgmix8.sum(0), "g_mem": dgmem8.sum(0), "g_ffn": dgffn8.sum(0), "g_final": dgfin8.sum(0),
          "b_alpha": dba8.sum(0), "b_forget": dbf8.sum(0)[FF_LANE:FF_LANE + 8], "g_gla_head": dgh8.sum(0)}
    return loss, grad_x, gw, gs, exchanged


def kernel(x, mem, g_mix, w_in, w_alpha_up, b_alpha, b_forget, g_gla_head, g_mem, w_mem_kv, w_gla_o, w_fox_o, w_mem_o, w_out, g_ffn, w_ff1, w_ff2, g_final, loss_target, m_g_mix, m_w_in, m_w_alpha_up, m_b_alpha, m_b_forget, m_g_gla_head, m_g_mem, m_w_mem_kv, m_w_gla_o, m_w_fox_o, m_w_mem_o, m_w_out, m_g_ffn, m_w_ff1, m_w_ff2, m_g_final, v_g_mix, v_w_in, v_w_alpha_up, v_b_alpha, v_b_forget, v_g_gla_head, v_g_mem, v_w_mem_kv, v_w_gla_o, v_w_fox_o, v_w_mem_o, v_w_out, v_g_ffn, v_w_ff1, v_w_ff2, v_g_final):
    args = dict(locals())
    w_sh = {n: args[n][0] for n in WEIGHTS}
    small = {n: args[n] for n in SMALL}

    def whole(parts):
        return {n: jnp.concatenate([p[n] for p in parts], axis=SHARD_AXIS[n]) for n in parts[0]}

    class Exchange:
        gather_a = _gather_side(_pack_a(w_sh, BF16))
        gather_b = _gather_side(_pack_b(w_sh, BF16))

        @staticmethod
        def weights_a(gathered):
            return whole([_unpack_a(gathered[k]) for k in range(N_CHIPS)])

        @staticmethod
        def weights_b(gathered):
            return whole([_unpack_b(gathered[k]) for k in range(N_CHIPS)])

        @staticmethod
        def scatter(gw):
            by_chip = {n: _split_shards(n, gw[n]) for n in WEIGHTS}
            packed = jnp.stack([jnp.concatenate([_pack_a({n: by_chip[n][k] for n in WEIGHTS}, BF16),
                                                 _pack_b({n: by_chip[n][k] for n in WEIGHTS}, BF16)], axis=0)
                                for k in range(N_CHIPS)])
            hr = PACK_ROWS // 2
            mine = lax.dynamic_slice_in_dim(packed, lax.axis_index("c") * hr, hr, axis=1)
            return _scatter_side(_add2(mine, _swap_halves(packed), "chip_sum"))

    loss, grad_x, gw, gs, by_chip = _local_step(x[0], mem[0], loss_target[0], None, small, Exchange)
    g_out = {n: g[None] for n, g in _unpack(_sum4(by_chip, "shard_sum")).items()}
    d_out, m_out, v_out = {}, {}, {}
    for n in WEIGHTS:
        d_out[n], m_out[n], v_out[n] = _adam(args[n], g_out[n], args["m_" + n], args["v_" + n], "adam_" + n)

    small_all = _gather_small(_pack_small(gs, loss))
    sm = {n: args["m_" + n] for n in SMALL}
    sv = {n: args["v_" + n] for n in SMALL}
    gs_sum, sd, snm, snv = _adam_small(_pack_small(small), small_all, _pack_small(sm), _pack_small(sv))
    gs_o, sd_o, snm_o, snv_o = _unpack_small(gs_sum), _unpack_small(sd), _unpack_small(snm), _unpack_small(snv)

    names = ["g_mix", "w_in", "w_alpha_up", "b_alpha", "b_forget", "g_gla_head", "g_mem", "w_mem_kv", "w_gla_o", "w_fox_o",
             "w_mem_o", "w_out", "g_ffn", "w_ff1", "w_ff2", "g_final"]

    def pick(big, sml, n):
        return big[n] if n in big else sml[n]

    outs = [gs_sum[6, 0], grad_x[None]]
    for big, sml in ((g_out, gs_o), (d_out, sd_o), (m_out, snm_o), (v_out, snv_o)):
        outs += [pick(big, sml, n) for n in names]
    return tuple(outs)
```

```python
import functools

import numpy as np
import jax
import jax.numpy as jnp
from jax import lax
from jax.experimental import pallas as pl
from jax.experimental.pallas import tpu as pltpu

F32 = jnp.float32
BF16 = jnp.bfloat16
HI = lax.Precision.HIGHEST
MESH = pl.DeviceIdType.MESH

EPS = 1e-6
D = 1024
CHUNK = 64
GLA_TAU = 16.0
N_CHIPS = 4
N_DEV = 8
VMEM_LIMIT_BYTES = 56 * 1024 * 1024

ADAM_LR, ADAM_B1, ADAM_B2, ADAM_EPS, ADAM_WD, ADAM_STEP = 0.001, 0.9, 0.999, 1e-08, 0.01, 10

PM_W = 6656
PE_W = 128
C_GQ, C_GK, C_GV, C_GG, C_FQ, C_FK, C_FV, C_MQ = 3072, 3328, 3584, 4096, 4608, 5120, 5632, 6144
FF_LANE = 16

WEIGHTS = ("w_in", "w_alpha_up", "w_mem_kv", "w_gla_o", "w_fox_o", "w_mem_o", "w_out", "w_ff1", "w_ff2")
SHARD_AXIS = {"w_in": 1, "w_alpha_up": 1, "w_mem_kv": 0, "w_gla_o": 1, "w_fox_o": 1, "w_mem_o": 1, "w_out": 0,
              "w_ff1": 1, "w_ff2": 0}
SMALL = ("g_mix", "g_mem", "g_ffn", "g_final", "b_alpha", "b_forget", "g_gla_head")
PACK_W = 1024
PACK_ROWS_A = 2048
PACK_ROWS_B = 3104
PACK_ROWS = PACK_ROWS_A + PACK_ROWS_B


def _cp(*sem):
    return pltpu.CompilerParams(dimension_semantics=sem, vmem_limit_bytes=VMEM_LIMIT_BYTES)


def _dot(a, b, **kw):
    return jnp.dot(a, b, preferred_element_type=F32, **kw)


def _dot_nt(a, b, **kw):
    return lax.dot_general(a, b, (((1,), (1,)), ((), ())), preferred_element_type=F32, **kw)


def _dot_tn(a, b, **kw):
    return lax.dot_general(a, b, (((0,), (0,)), ((), ())), preferred_element_type=F32, **kw)


def _sigmoid(x):
    return 1.0 / (1.0 + jnp.exp(-x))


def _log_sigmoid(x):
    return -(jnp.maximum(-x, 0.0) + jnp.log1p(jnp.exp(-jnp.abs(x))))


def _fold8(x):
    m, n = x.shape
    return x.reshape(m // 8, 8, n).sum(axis=0)


def _iota(shape, dim):
    return lax.broadcasted_iota(jnp.int32, shape, dim)


def _row_tile(s):
    return min(s, 512)


class _Side:
    def __init__(self, inputs, out_shape, scratch, ops):
        self.inputs, self.out_shape, self.scratch, self.ops = list(inputs), list(out_shape), list(scratch), ops


ANY_SPEC = pl.BlockSpec(memory_space=pl.ANY)


def _mm_nn(a, b, *, out_dtype, tm, tn, tk, name, a_fn=None, epi=None, extra=None):
    m, k = a.shape
    _, n = b.shape
    nk = k // tk

    def body_one(*refs):
        a_ref, b_ref = refs[0], refs[1]
        at = a_ref[...] if a_fn is None else a_fn(a_ref[...])
        r = _dot(at, b_ref[...])
        if epi is not None:
            r = epi(r, None if extra is None else refs[2][...])
        refs[-1][...] = r.astype(out_dtype)

    if nk == 1:
        in_specs = [pl.BlockSpec((tm, k), lambda i, j: (i, 0)), pl.BlockSpec((k, tn), lambda i, j: (0, j))]
        args = [a, b]
        if extra is not None:
            in_specs.append(pl.BlockSpec((tm, tn), lambda i, j: (i, j)))
            args.append(extra)
        return pl.pallas_call(
            body_one, grid=(m // tm, n // tn), in_specs=in_specs, out_specs=pl.BlockSpec((tm, tn), lambda i, j: (i, j)),
            out_shape=jax.ShapeDtypeStruct((m, n), out_dtype), name=name, compiler_params=_cp("parallel", "parallel"))(*args)

    def body(*refs):
        if extra is None:
            a_ref, b_ref, o_ref, acc = refs
            x_ref = None
        else:
            a_ref, b_ref, x_ref, o_ref, acc = refs
        kk = pl.program_id(2)

        @pl.when(kk == 0)
        def _():
            acc[...] = jnp.zeros_like(acc)

        at = a_ref[...]
        if a_fn is not None:
            at = a_fn(at)
        acc[...] += _dot(at, b_ref[...])

        @pl.when(kk == nk - 1)
        def _():
            r = acc[...]
            if epi is not None:
                r = epi(r, None if x_ref is None else x_ref[...])
            o_ref[...] = r.astype(out_dtype)

    in_specs = [pl.BlockSpec((tm, tk), lambda i, j, kk: (i, kk)), pl.BlockSpec((tk, tn), lambda i, j, kk: (kk, j))]
    args = [a, b]
    if extra is not None:
        in_specs.append(pl.BlockSpec((tm, tn), lambda i, j, kk: (i, j)))
        args.append(extra)
    return pl.pallas_call(
        body, grid=(m // tm, n // tn, nk), in_specs=in_specs,
        out_specs=pl.BlockSpec((tm, tn), lambda i, j, kk: (i, j)),
        out_shape=jax.ShapeDtypeStruct((m, n), out_dtype),
        scratch_shapes=[pltpu.VMEM((tm, tn), F32)], name=name,
        compiler_params=_cp("parallel", "parallel", "arbitrary"))(*args)


def _mm_tn(a, b, *, tm, tn, ts, name, a_fn=None):
    s, m = a.shape
    _, n = b.shape
    ns = s // ts

    def body(a_ref, b_ref, o_ref, acc):
        kk = pl.program_id(2)

        @pl.when(kk == 0)
        def _():
            acc[...] = jnp.zeros_like(acc)

        at = a_ref[...]
        if a_fn is not None:
            at = a_fn(at)
        acc[...] += _dot_tn(at, b_ref[...])

        @pl.when(kk == ns - 1)
        def _():
            o_ref[...] = acc[...]

    return pl.pallas_call(
        body, grid=(m // tm, n // tn, ns),
        in_specs=[pl.BlockSpec((ts, tm), lambda i, j, kk: (kk, i)), pl.BlockSpec((ts, tn), lambda i, j, kk: (kk, j))],
        out_specs=pl.BlockSpec((tm, tn), lambda i, j, kk: (i, j)),
        out_shape=jax.ShapeDtypeStruct((m, n), F32),
        scratch_shapes=[pltpu.VMEM((tm, tn), F32)], name=name,
        compiler_params=_cp("parallel", "parallel", "arbitrary"))(a, b)


def _mm_tn_cat(a, bs, *, ts, name):
    s, m = a.shape
    n = sum(b.shape[1] for b in bs)
    ns = s // ts
    nb = len(bs)

    def body(*refs):
        a_ref, b_refs, o_ref, acc = refs[0], refs[1:1 + nb], refs[1 + nb], refs[2 + nb]
        kk = pl.program_id(0)

        @pl.when(kk == 0)
        def _():
            acc[...] = jnp.zeros_like(acc)

        bt = b_refs[0][...] if nb == 1 else jnp.concatenate([r[...] for r in b_refs], axis=1)
        acc[...] += _dot_tn(a_ref[...], bt)

        @pl.when(kk == ns - 1)
        def _():
            o_ref[...] = acc[...]

    return pl.pallas_call(
        body, grid=(ns,),
        in_specs=[pl.BlockSpec((ts, m), lambda kk: (kk, 0))] + [pl.BlockSpec((ts, b.shape[1]), lambda kk: (kk, 0)) for b in bs],
        out_specs=pl.BlockSpec((m, n), lambda kk: (0, 0)), out_shape=jax.ShapeDtypeStruct((m, n), F32),
        scratch_shapes=[pltpu.VMEM((m, n), F32)], name=name, compiler_params=_cp("arbitrary"))(a, *bs)


def _proj(u, w_main, w_e, side=None):
    s, k = u.shape
    n = w_main.shape[1]
    tm, tn = min(s, 1024), n // 4
    n_sin = 0 if side is None else len(side.inputs)
    n_sout = 0 if side is None else len(side.out_shape)

    def body(u_ref, w_ref, we_ref, *rest):
        pm_ref, pe_ref = rest[n_sin:n_sin + 2]
        i, j = pl.program_id(0), pl.program_id(1)
        if side is not None:
            start, finish = side.ops(rest[:n_sin], rest[n_sin + 2:n_sin + 2 + n_sout], rest[n_sin + 2 + n_sout:])
            pl.when((i == 0) & (j == 0))(start)
        ut = u_ref[...]
        pm_ref[...] = _dot(ut, w_ref[...]).astype(BF16)

        @pl.when(j == 0)
        def _():
            pe_ref[...] = _dot(ut, we_ref[...])

        if side is not None:
            pl.when((i == s // tm - 1) & (j == n // tn - 1))(finish)

    side_in = [] if side is None else side.inputs
    return pl.pallas_call(
        body, grid=(s // tm, n // tn),
        in_specs=[pl.BlockSpec((tm, k), lambda i, j: (i, 0)), pl.BlockSpec((k, tn), lambda i, j: (0, j)),
                  pl.BlockSpec((k, PE_W), lambda i, j: (0, 0))] + [ANY_SPEC] * n_sin,
        out_specs=[pl.BlockSpec((tm, tn), lambda i, j: (i, j)), pl.BlockSpec((tm, PE_W), lambda i, j: (i, 0))]
        + [ANY_SPEC] * n_sout,
        out_shape=[jax.ShapeDtypeStruct((s, n), BF16), jax.ShapeDtypeStruct((s, PE_W), F32)]
        + ([] if side is None else side.out_shape),
        scratch_shapes=[] if side is None else side.scratch,
        name="proj_main", compiler_params=_cp("arbitrary", "arbitrary"))(u, w_main, w_e, *side_in)


def _transpose(w, name):
    r, c = w.shape
    tr = min(r, 256)

    def body(w_ref, o_ref):
        o_ref[...] = w_ref[...].T

    return pl.pallas_call(body, grid=(r // tr,), in_specs=[pl.BlockSpec((tr, c), lambda i: (i, 0))],
                          out_specs=pl.BlockSpec((c, tr), lambda i: (0, i)),
                          out_shape=jax.ShapeDtypeStruct((c, r), w.dtype), name=name, compiler_params=_cp("parallel"))(w)


def _relu2_bf16(t):
    r = jnp.maximum(t.astype(F32), 0.0)
    return (r * r).astype(BF16)


def _rms_fwd(x, g, name, side=None):
    s, d = x.shape
    tm = min(s, 512)
    n_sin = 0 if side is None else len(side.inputs)
    n_sout = 0 if side is None else len(side.out_shape)

    def body(x_ref, g_ref, *rest):
        u_ref, r_ref = rest[n_sin:n_sin + 2]
        if side is not None:
            start, finish = side.ops(rest[:n_sin], rest[n_sin + 2:n_sin + 2 + n_sout], rest[n_sin + 2 + n_sout:])
            pl.when(pl.program_id(0) == 0)(start)
        xv = x_ref[...]
        r = lax.rsqrt(jnp.mean(xv * xv, axis=-1, keepdims=True) + EPS)
        u_ref[...] = ((xv * r) * g_ref[...]).astype(BF16)
        r_ref[...] = r
        if side is not None:
            pl.when(pl.program_id(0) == s // tm - 1)(finish)

    side_in = [] if side is None else side.inputs
    return pl.pallas_call(
        body, grid=(s // tm,),
        in_specs=[pl.BlockSpec((tm, d), lambda i: (i, 0)), pl.BlockSpec((1, d), lambda i: (0, 0))] + [ANY_SPEC] * n_sin,
        out_specs=[pl.BlockSpec((tm, d), lambda i: (i, 0)), pl.BlockSpec((tm, 1), lambda i: (i, 0))] + [ANY_SPEC] * n_sout,
        out_shape=[jax.ShapeDtypeStruct((s, d), BF16), jax.ShapeDtypeStruct((s, 1), F32)]
        + ([] if side is None else side.out_shape),
        scratch_shapes=[] if side is None else side.scratch,
        name=name, compiler_params=_cp("parallel" if side is None else "arbitrary"))(x, g, *side_in)


def _mm_norm_bwd(a_parts, b, xin, r, g, dres, *, name, want_bf16, side=None):
    s = a_parts[0].shape[0]
    k = b.shape[0]
    na = len(a_parts)
    offs = [sum(p.shape[1] for p in a_parts[:i]) for i in range(na)]
    assert offs[-1] + a_parts[-1].shape[1] == k
    tm = min(s, 256)
    n_out = 3 if want_bf16 else 2
    n_sin = 0 if side is None else len(side.inputs)
    n_sout = 0 if side is None else len(side.out_shape)

    def body(*refs):
        a_refs = refs[:na]
        b_ref, x_ref, r_ref, g_ref, dres_ref = refs[na:na + 5]
        rest = refs[na + 5:]
        outs = rest[n_sin:n_sin + n_out]
        dx_ref, dg_ref = outs[0], outs[-1]
        if side is not None:
            start, finish = side.ops(rest[:n_sin], rest[n_sin + n_out:n_sin + n_out + n_sout], rest[n_sin + n_out + n_sout:])
            pl.when(pl.program_id(0) == 0)(start)

        @pl.when(pl.program_id(0) == 0)
        def _():
            dg_ref[...] = jnp.zeros_like(dg_ref)

        du = _dot(a_refs[0][...], b_ref[0:a_parts[0].shape[1], :])
        for a_ref, off, part in zip(a_refs[1:], offs[1:], a_parts[1:]):
            du = du + _dot(a_ref[...], b_ref[off:off + part.shape[1], :])
        xn = x_ref[...] * r_ref[...]
        dg_ref[...] += _fold8(du * xn)
        dxn = du * g_ref[...]
        dx = dres_ref[...] + r_ref[...] * (dxn - xn * jnp.mean(dxn * xn, axis=-1, keepdims=True))
        dx_ref[...] = dx
        if want_bf16:
            outs[1][...] = dx.astype(BF16)
        if side is not None:
            pl.when(pl.program_id(0) == s // tm - 1)(finish)

    row = lambda i: (i, 0)
    const = lambda i: (0, 0)
    out_specs = [pl.BlockSpec((tm, D), row)]
    out_shape = [jax.ShapeDtypeStruct((s, D), F32)]
    if want_bf16:
        out_specs.append(pl.BlockSpec((tm, D), row))
        out_shape.append(jax.ShapeDtypeStruct((s, D), BF16))
    out_specs.append(pl.BlockSpec((8, D), const))
    out_shape.append(jax.ShapeDtypeStruct((8, D), F32))
    side_in = [] if side is None else side.inputs
    return pl.pallas_call(
        body, grid=(s // tm,),
        in_specs=[pl.BlockSpec((tm, p.shape[1]), row) for p in a_parts]
        + [pl.BlockSpec((k, D), const, pipeline_mode=pl.Buffered(1)),
           pl.BlockSpec((tm, D), row), pl.BlockSpec((tm, 1), row), pl.BlockSpec((1, D), const),
           pl.BlockSpec((tm, D), row)] + [ANY_SPEC] * n_sin,
        out_specs=out_specs + [ANY_SPEC] * n_sout, out_shape=out_shape + ([] if side is None else side.out_shape),
        scratch_shapes=[] if side is None else side.scratch,
        name=name, compiler_params=_cp("arbitrary"))(*a_parts, b, xin, r, g, dres, *side_in)


def _gla_consts():
    lmask = _iota((4 * CHUNK, CHUNK), 0) % CHUNK >= _iota((4 * CHUNK, CHUNK), 1)
    hmask = _iota((256, 256), 0) // CHUNK == _iota((256, 256), 1) // CHUNK
    bd = _iota((256, 512), 0) // CHUNK == _iota((256, 512), 1) // 128
    return lmask, hmask, bd


def _fold_heads(x):
    return x[0:64] + x[64:128] + x[128:192] + x[192:256]


def _gla_decays(la, b_scr, dec_scr):
    tri = (_iota((CHUNK, CHUNK), 0) >= _iota((CHUNK, CHUNK), 1)).astype(BF16)
    ones = jnp.ones((CHUNK, 128), BF16)
    for c in range(la.shape[0] // CHUNK):
        la3 = _split3(la[CHUNK * c:CHUNK * (c + 1)])
        b_scr[CHUNK * c:CHUNK * (c + 1), :] = _sum3(_dot(tri, la3), 1)
        dec_scr[c] = jnp.exp(_sum3(_dot_tn(la3, ones), 0))


def _gla_chunk(b, qc, kc):
    bl = b[CHUNK - 1:CHUNK, :]
    ep, en, ek = jnp.exp(b), jnp.exp(-b), jnp.exp(bl - b)
    return bl, ep, en, ek, qc * ep, qc * en, kc * en, kc * ep, kc * ek


def _gla_fwd(pm, pe, wau_p, b_alpha):
    s = pm.shape[0]
    t = _row_tile(s)
    nc = t // CHUNK

    def body(q_ref, k_ref, v_ref, e_ref, wau_ref, ba_ref, o_ref, st_ref, state, b_scr, dec_scr):
        @pl.when(pl.program_id(0) == 0)
        def _():
            state[...] = jnp.zeros_like(state)

        z = _dot(e_ref[...].astype(BF16), wau_ref[...]) + ba_ref[...]
        _gla_decays(_log_sigmoid(z) * (1.0 / GLA_TAU), b_scr, dec_scr)
        lmask, hmask, bd = _gla_consts()

        def chunk(c, carry):
            rows = pl.ds(pl.multiple_of(c * CHUNK, CHUNK), CHUNK)
            qc = q_ref[rows, :].astype(F32) * 0.125
            kc = k_ref[rows, :].astype(F32)
            vc = v_ref[rows, :]
            _, _, _, _, qp, qn, kn, kp, kk = _gla_chunk(b_scr[rows, :], qc, kc)
            decb = jnp.concatenate([dec_scr[c]] * 4, axis=1)
            qs = jnp.where(hmask, jnp.concatenate([qp] * 4, axis=0), 0.0).astype(BF16)
            qns = jnp.where(hmask, jnp.concatenate([qn] * 4, axis=0), 0.0).astype(BF16)
            attn = jnp.where(lmask, _dot_nt(qs, kn.astype(BF16)), _dot_nt(qns, kp.astype(BF16))).astype(BF16)
            st = state[...]
            o_intra = _fold_heads(jnp.where(bd, _dot(attn, vc), 0.0))
            o_ref[rows, :] = o_intra + _dot(qp.astype(BF16), st.astype(BF16))
            for h in range(4):
                st_ref[c, :, 128 * h:128 * (h + 1)] = st[64 * h:64 * (h + 1), 128 * h:128 * (h + 1)]
            kv = jnp.where(bd, _dot_tn(kk.astype(BF16), vc), 0.0)
            state[...] = st * decb + kv
            return carry

        lax.fori_loop(0, nc, chunk, 0)

    return pl.pallas_call(
        body, grid=(s // t,),
        in_specs=[pl.BlockSpec((t, 256), lambda i: (i, C_GQ // 256)), pl.BlockSpec((t, 256), lambda i: (i, C_GK // 256)),
                  pl.BlockSpec((t, 512), lambda i: (i, C_GV // 512)), pl.BlockSpec((t, PE_W), lambda i: (i, 0)),
                  pl.BlockSpec((PE_W, 256), lambda i: (0, 0)), pl.BlockSpec((1, 256), lambda i: (0, 0))],
        out_specs=[pl.BlockSpec((t, 512), lambda i: (i, 0)), pl.BlockSpec((nc, CHUNK, 512), lambda i: (i, 0, 0))],
        out_shape=[jax.ShapeDtypeStruct((s, 512), F32), jax.ShapeDtypeStruct((s // CHUNK, CHUNK, 512), F32)],
        scratch_shapes=[pltpu.VMEM((256, 512), F32), pltpu.VMEM((t, 256), F32), pltpu.VMEM((nc, 256, 128), F32)],
        name="gla_fwd", compiler_params=_cp("arbitrary"))(pm, pm, pm, pe, wau_p, b_alpha)


def _gla_bwd(pm, pe, wau_p, wau_pt, b_alpha, do, states):
    s = pm.shape[0]
    t = _row_tile(s)
    nc = t // CHUNK
    nb = s // t

    def body(q_ref, k_ref, v_ref, e_ref, wau_ref, waut_ref, ba_ref, do_ref, st_ref,
             dq_ref, dk_ref, dv_ref, de_ref, dwau_ref, dba_ref, gstate, b_scr, db_scr, dec_scr):
        @pl.when(pl.program_id(0) == 0)
        def _():
            gstate[...] = jnp.zeros_like(gstate)
            dwau_ref[...] = jnp.zeros_like(dwau_ref)
            dba_ref[...] = jnp.zeros_like(dba_ref)

        eb = e_ref[...].astype(BF16)
        z = _dot(eb, wau_ref[...]) + ba_ref[...]
        _gla_decays(_log_sigmoid(z) * (1.0 / GLA_TAU), b_scr, dec_scr)
        lmask, hmask, bd = _gla_consts()
        last_row = _iota((CHUNK, 256), 0) == CHUNK - 1

        def chunk(cc, carry):
            c = nc - 1 - cc
            rows = pl.ds(pl.multiple_of(c * CHUNK, CHUNK), CHUNK)
            qc = q_ref[rows, :].astype(F32) * 0.125
            kc = k_ref[rows, :].astype(F32)
            vc = v_ref[rows, :]
            dob = do_ref[rows, :]
            bl, ep, en, ek, qp, qn, kn, kp, kk = _gla_chunk(b_scr[rows, :], qc, kc)
            decb = jnp.concatenate([dec_scr[c]] * 4, axis=1)
            qs = jnp.where(hmask, jnp.concatenate([qp] * 4, axis=0), 0.0).astype(BF16)
            qns = jnp.where(hmask, jnp.concatenate([qn] * 4, axis=0), 0.0).astype(BF16)
            knb, kpb = kn.astype(BF16), kp.astype(BF16)
            attn = jnp.where(lmask, _dot_nt(qs, knb), _dot_nt(qns, kpb)).astype(BF16)
            st = jnp.where(bd, jnp.concatenate([st_ref[c]] * 4, axis=0), 0.0)
            g = gstate[...]
            gb = g.astype(BF16)
            do_s = jnp.where(bd, jnp.concatenate([dob] * 4, axis=0), jnp.zeros((), BF16))
            dattn = _dot_nt(do_s, vc)
            dv_ref[rows, :] = (_dot_tn(attn, do_s) + _dot(kk.astype(BF16), gb)).astype(BF16)
            dac = jnp.where(lmask, dattn, 0.0).astype(BF16)
            daa = jnp.where(lmask, 0.0, dattn).astype(BF16)
            dqp = _fold_heads(jnp.where(hmask, _dot(dac, knb), 0.0)) + _dot_nt(dob, st.astype(BF16))
            dqn = _fold_heads(jnp.where(hmask, _dot(daa, kpb), 0.0))
            dkn = _dot_tn(dac, qs)
            dkp = _dot_tn(daa, qns)
            dkk = _dot_nt(vc, gb)
            ddec = _dot_nt(jnp.ones((8, 1536), BF16), _split3(g * st))[0:1, :]
            gstate[...] = decb * g + jnp.where(bd, _dot_tn(qp.astype(BF16), dob), 0.0)
            dq_ref[rows, :] = ((dqp * ep + dqn * en) * 0.125).astype(BF16)
            dk_ref[rows, :] = (dkn * en + dkp * ep + dkk * ek).astype(BF16)
            dek = dkk * kc * ek
            db = (dqp * qc + dkp * kc) * ep - (dqn * qc + dkn * kc) * en - dek
            dbl = jnp.sum(dek, axis=0, keepdims=True) + ddec * jnp.exp(bl)
            db_scr[rows, :] = db + jnp.where(last_row, dbl, 0.0)
            return carry

        lax.fori_loop(0, nc, chunk, 0)
        triu = (_iota((CHUNK, CHUNK), 0) <= _iota((CHUNK, CHUNK), 1)).astype(BF16)
        dla = jnp.concatenate([_sum3(_dot(triu, _split3(db_scr[CHUNK * c:CHUNK * (c + 1), :])), 1) for c in range(nc)], axis=0)
        dz = dla * (1.0 / GLA_TAU) * _sigmoid(-z)
        dzb = dz.astype(BF16)
        dwau_ref[...] += _dot_tn(eb, dzb)
        dba_ref[...] += _fold8(dz)
        de_ref[...] = _dot(dzb, waut_ref[...])

    rev = lambda i: nb - 1 - i
    return pl.pallas_call(
        body, grid=(nb,),
        in_specs=[pl.BlockSpec((t, 256), lambda i: (rev(i), C_GQ // 256)), pl.BlockSpec((t, 256), lambda i: (rev(i), C_GK // 256)),
                  pl.BlockSpec((t, 512), lambda i: (rev(i), C_GV // 512)), pl.BlockSpec((t, PE_W), lambda i: (rev(i), 0)),
                  pl.BlockSpec((PE_W, 256), lambda i: (0, 0)), pl.BlockSpec((256, PE_W), lambda i: (0, 0)),
                  pl.BlockSpec((1, 256), lambda i: (0, 0)), pl.BlockSpec((t, 512), lambda i: (rev(i), 0)),
                  pl.BlockSpec((nc, CHUNK, 512), lambda i: (rev(i), 0, 0))],
        out_specs=[pl.BlockSpec((t, 256), lambda i: (rev(i), 0)), pl.BlockSpec((t, 256), lambda i: (rev(i), 0)),
                   pl.BlockSpec((t, 512), lambda i: (rev(i), 0)), pl.BlockSpec((t, PE_W), lambda i: (rev(i), 0)),
                   pl.BlockSpec((PE_W, 256), lambda i: (0, 0)), pl.BlockSpec((8, 256), lambda i: (0, 0))],
        out_shape=[jax.ShapeDtypeStruct((s, 256), BF16), jax.ShapeDtypeStruct((s, 256), BF16),
                   jax.ShapeDtypeStruct((s, 512), BF16), jax.ShapeDtypeStruct((s, PE_W), F32),
                   jax.ShapeDtypeStruct((PE_W, 256), F32), jax.ShapeDtypeStruct((8, 256), F32)],
        scratch_shapes=[pltpu.VMEM((256, 512), F32), pltpu.VMEM((t, 256), F32), pltpu.VMEM((t, 256), F32),
                        pltpu.VMEM((nc, 256, 128), F32)],
        name="gla_bwd", compiler_params=_cp("arbitrary"))(pm, pm, pm, pe, wau_p, wau_pt, b_alpha, do, states)


def _fcum_fwd(pe, bias):
    s = pe.shape[0]
    t = min(s, 256)

    def body(e_ref, b_ref, f_ref, carry):
        @pl.when(pl.program_id(0) == 0)
        def _():
            carry[...] = jnp.zeros_like(carry)

        lf = _log_sigmoid(e_ref[...] + b_ref[...])
        tri = (_iota((t, t), 0) >= _iota((t, t), 1)).astype(BF16)
        f = _sum3(_dot(tri, _split3(lf)), 1) + carry[0:1, :]
        f_ref[...] = f
        carry[...] = jnp.broadcast_to(f[t - 1:t, :], carry.shape)

    return pl.pallas_call(
        body, grid=(s // t,),
        in_specs=[pl.BlockSpec((t, PE_W), lambda i: (i, 0)), pl.BlockSpec((1, PE_W), lambda i: (0, 0))],
        out_specs=pl.BlockSpec((t, PE_W), lambda i: (i, 0)),
        out_shape=jax.ShapeDtypeStruct((s, PE_W), F32), scratch_shapes=[pltpu.VMEM((8, PE_W), F32)],
        name="fcum_fwd", compiler_params=_cp("arbitrary"))(pe, bias)


def _fcum_bwd(pe, bias, df):
    s = pe.shape[0]
    t = min(s, 256)
    nb = s // t

    def body(e_ref, b_ref, df_ref, de_ref, db_ref, carry):
        @pl.when(pl.program_id(0) == 0)
        def _():
            carry[...] = jnp.zeros_like(carry)
            db_ref[...] = jnp.zeros_like(db_ref)

        triu = (_iota((t, t), 0) <= _iota((t, t), 1)).astype(BF16)
        dlf = _sum3(_dot(triu, _split3(df_ref[...])), 1) + carry[0:1, :]
        carry[...] = jnp.broadcast_to(dlf[0:1, :], carry.shape)
        lane = _iota((t, PE_W), 1)
        dff = jnp.where((lane >= FF_LANE) & (lane < FF_LANE + 8), dlf * _sigmoid(-(e_ref[...] + b_ref[...])), 0.0)
        de_ref[...] = dff
        db_ref[...] += _fold8(dff)

    rev = lambda i: (nb - 1 - i, 0)
    return pl.pallas_call(
        body, grid=(nb,),
        in_specs=[pl.BlockSpec((t, PE_W), rev), pl.BlockSpec((1, PE_W), lambda i: (0, 0)), pl.BlockSpec((t, PE_W), rev)],
        out_specs=[pl.BlockSpec((t, PE_W), rev), pl.BlockSpec((8, PE_W), lambda i: (0, 0))],
        out_shape=[jax.ShapeDtypeStruct((s, PE_W), F32), jax.ShapeDtypeStruct((8, PE_W), F32)],
        scratch_shapes=[pltpu.VMEM((8, PE_W), F32)],
        name="fcum_bwd", compiler_params=_cp("arbitrary"))(pe, bias, df)


FOX_WIDE = 1024


def _split3(x):
    hi = x.astype(BF16)
    r = x - hi.astype(F32)
    mid = r.astype(BF16)
    lo = (r - mid.astype(F32)).astype(BF16)
    return jnp.concatenate([hi, mid, lo], axis=1)


def _sum3(x, axis):
    n = x.shape[axis] // 3
    parts = [lax.slice_in_dim(x, n * p, n * (p + 1), axis=axis) for p in range(3)]
    return (parts[0] + parts[1]) + parts[2]


def _fox_tables():
    heads, lane = np.arange(8), np.arange(64)
    spread = np.zeros((512, 1024), np.float32)
    spread[(64 * heads[:, None] + lane).ravel(), (128 * heads[:, None] + lane).ravel()] = 1.0
    def place(src_lane0, dst_off, val):
        t = np.zeros((384, 1024), np.float32)
        for p in range(3):
            t[128 * p + src_lane0 + heads, 128 * heads + dst_off + p] = val
        return t
    def const(off, val):
        c = np.zeros((1, 1024), np.float32)
        for p in range(3):
            c[0, 128 * heads + off + p] = val
        return c
    rows = np.zeros((8, 128), np.float32)
    rows[heads, FF_LANE + heads] = 1.0
    bf = lambda a: jnp.asarray(a, BF16)
    return dict(spread=bf(spread),
                f_to_q=bf(place(FF_LANE, 64, 1.0)), f_to_k=bf(place(FF_LANE, 67, -1.0)), d_to_do=bf(place(0, 64, 1.0)),
                ones_q=jnp.asarray(const(67, 1.0)), ones_k=jnp.asarray(const(64, 1.0)), ones_v=jnp.asarray(const(64, -1.0)),
                rows=jnp.asarray(rows))


LOG2E = 1.4426950408889634


def _fox_prep(pm, f128, lse8, tb, *, backward):
    s = pm.shape[0]
    tm = _row_tile(s)

    def body(*refs):
        if backward:
            q_ref, f_ref, lse_ref, sp_ref, fq_ref, cq_ref, rows_ref, qa_ref = refs
            f = f_ref[...] * LOG2E - _dot_tn(lse_ref[...], rows_ref[...], precision=HI)
            q2 = (q_ref[...].astype(F32) * (0.125 * LOG2E)).astype(BF16)
            qa_ref[...] = (_dot(q2, sp_ref[...]) + _dot(_split3(f), fq_ref[...]) + cq_ref[...]).astype(BF16)
            return
        (q_ref, k_ref, v_ref, f_ref, sp_ref, fq_ref, fk_ref, cq_ref, ck_ref, cv_ref,
         qa_ref, ka_ref, va_ref, vt_ref, qt_ref, kt_ref) = refs
        f3 = _split3(f_ref[...] * LOG2E)
        q, k, v = q_ref[...].astype(F32), k_ref[...], v_ref[...]
        sp = sp_ref[...]
        qa_ref[...] = (_dot((q * (0.125 * LOG2E)).astype(BF16), sp) + _dot(f3, fq_ref[...]) + cq_ref[...]).astype(BF16)
        ka_ref[...] = (_dot(k, sp) + _dot(f3, fk_ref[...]) + ck_ref[...]).astype(BF16)
        va_ref[...] = (_dot(v, sp) + cv_ref[...]).astype(BF16)
        vt_ref[...] = v.T
        qt_ref[...] = (q * 0.125).astype(BF16).T
        kt_ref[...] = (k.astype(F32) * 0.125).astype(BF16).T

    row = lambda i: (i, 0)
    const = lambda i: (0, 0)
    blk = lambda c: pl.BlockSpec((tm, 512), lambda i: (i, c // 512))
    wide = pl.BlockSpec((tm, 1024), row)
    mat = lambda a: pl.BlockSpec(a.shape, const)
    if backward:
        ins = [pm, f128, lse8, tb["spread"], tb["f_to_q"], tb["ones_q"], tb["rows"]]
        in_specs = [blk(C_FQ), pl.BlockSpec((tm, 128), row), pl.BlockSpec((8, tm), lambda i: (0, i))] + [mat(a) for a in ins[3:]]
        out_specs, out_shape = wide, jax.ShapeDtypeStruct((s, 1024), BF16)
    else:
        ins = [pm, pm, pm, f128, tb["spread"], tb["f_to_q"], tb["f_to_k"], tb["ones_q"], tb["ones_k"], tb["ones_v"]]
        in_specs = [blk(C_FQ), blk(C_FK), blk(C_FV), pl.BlockSpec((tm, 128), row)] + [mat(a) for a in ins[4:]]
        tr = pl.BlockSpec((512, tm), lambda i: (0, i))
        out_specs = [wide, wide, wide, tr, tr, tr]
        out_shape = [jax.ShapeDtypeStruct((s, 1024), BF16)] * 3 + [jax.ShapeDtypeStruct((512, s), BF16)] * 3
    return pl.pallas_call(body, grid=(s // tm,), in_specs=in_specs, out_specs=out_specs, out_shape=out_shape,
                          name="fox_prep_bwd" if backward else "fox_prep", compiler_params=_cp("parallel"))(*ins)


def _fox_post(dqt, dkt, dvt, rowsum8, colsum8, tb):
    s = dqt.shape[1]
    tm = _row_tile(s)

    def body(dqt_ref, dkt_ref, dvt_ref, rs_ref, cs_ref, rows_ref, dfq_ref, dfk_ref, dfv_ref, df_ref):
        dfq_ref[...] = dqt_ref[...].T.astype(BF16)
        dfk_ref[...] = dkt_ref[...].T
        dfv_ref[...] = dvt_ref[...].T
        df_ref[...] = _dot_tn(rs_ref[...] - cs_ref[...], rows_ref[...], precision=HI)

    row = lambda i: (i, 0)
    tr = pl.BlockSpec((512, tm), lambda i: (0, i))
    out = pl.BlockSpec((tm, 512), row)
    heads = pl.BlockSpec((8, tm), lambda i: (0, i))
    return pl.pallas_call(
        body, grid=(s // tm,),
        in_specs=[tr, tr, tr, heads, heads, pl.BlockSpec((8, 128), lambda i: (0, 0))],
        out_specs=[out, out, out, pl.BlockSpec((tm, 128), row)],
        out_shape=[jax.ShapeDtypeStruct((s, 512), BF16)] * 3 + [jax.ShapeDtypeStruct((s, 128), F32)],
        name="fox_post", compiler_params=_cp("parallel"))(dqt, dkt, dvt, rowsum8, colsum8, tb["rows"])


def _fox_fwd(k_aug, q_aug, vt):
    s = k_aug.shape[0]
    nh = 8
    tk = _row_tile(s)
    tq = min(s, FOX_WIDE)
    per = tq // tk

    def body(k_ref, q_ref, v_ref, o_ref, lse_ref, sbuf):
        i = pl.program_id(1)
        qa = q_ref[...]

        def scores(j):
            return _dot_nt(k_ref[pl.ds(pl.multiple_of(j * tk, tk), tk), :], qa)

        ones_row = (_iota((16, tk), 0) == 0).astype(BF16)

        def update(st, j, carry):
            m, acc = carry
            m2 = jnp.maximum(m, jnp.max(st, axis=0, keepdims=True))
            p = jnp.exp2(st - m2)
            vj = jnp.concatenate([v_ref[:, pl.ds(pl.multiple_of(j * tk, tk), tk)], ones_row], axis=0)
            return m2, jnp.exp2(m - m2) * acc + _dot(vj, p.astype(BF16))

        def step(a, carry):
            sbuf[1] = scores(2 * a + 1)
            carry = update(sbuf[0], 2 * a, carry)
            sbuf[0] = scores(2 * a + 2)
            return update(sbuf[1], 2 * a + 1, carry)

        n = i * per
        sbuf[0] = scores(0)
        carry = (jnp.full((1, tq), -1e30, F32), jnp.zeros((80, tq), F32))
        carry = lax.fori_loop(0, n // 2, step, carry)
        tri = _iota((tk, tk), 0) <= _iota((tk, tk), 1)
        late = [_dot_nt(k_ref[pl.ds(pl.multiple_of((n + r) * tk, tk), tk), :], qa[r * tk:, :]) for r in range(1, per)]
        for r in range(per):
            st = sbuf[0] if r == 0 else late[r - 1]
            head = jnp.where(tri, st[:, :tk], -1e30)
            st = head if st.shape[1] == tk else jnp.concatenate([head, st[:, tk:]], axis=1)
            part = update(st, n + r, tuple(c[:, r * tk:] for c in carry))
            carry = part if r == 0 else tuple(jnp.concatenate([old[:, :r * tk], new], axis=1) for old, new in zip(carry, part))
        m, acc = carry
        l = acc[64:65]
        o_ref[...] = (acc[0:64] / l).astype(BF16)
        lse_ref[0] = m + jnp.log2(l)

    return pl.pallas_call(
        body, grid=(nh, s // tq),
        in_specs=[pl.BlockSpec((s, 128), lambda h, i: (0, h)), pl.BlockSpec((tq, 128), lambda h, i: (i, h)),
                  pl.BlockSpec((64, s), lambda h, i: (h, 0))],
        out_specs=[pl.BlockSpec((64, tq), lambda h, i: (h, i)), pl.BlockSpec((1, 1, tq), lambda h, i: (h, 0, i))],
        out_shape=[jax.ShapeDtypeStruct((512, s), BF16), jax.ShapeDtypeStruct((nh, 1, s), F32)],
        scratch_shapes=[pltpu.VMEM((2, tk, tq), F32)],
        name="fox_fwd", compiler_params=_cp("parallel", "arbitrary"))(k_aug, q_aug, vt)


def _fox_bwd(q_aug, do_aug, qt, dot_, k_aug, v_aug, kt):
    s = q_aug.shape[0]
    nh = 8
    tq = _row_tile(s)
    tk = min(s, FOX_WIDE)
    per = tk // tq
    nqb = s // tq

    def body(qa_ref, da_ref, qt_ref, dt_ref, ka_ref, va_ref, kt_ref, dq_ref, rs_ref, dk_ref, dv_ref, dfk_ref):
        j = pl.program_id(1)

        @pl.when(j == 0)
        def _():
            dq_ref[...] = jnp.zeros_like(dq_ref)
            rs_ref[...] = jnp.zeros_like(rs_ref)

        ones_row = (_iota((16, tk), 0) == 0).astype(BF16)
        ka, va = ka_ref[...], va_ref[...]
        ks = jnp.concatenate([kt_ref[...], ones_row], axis=0)
        tri = _iota((tq, tq), 0) >= _iota((tq, tq), 1)

        def tile(i, w, carry):
            masked = w is not None
            w = tk if w is None else w
            rows = pl.ds(pl.multiple_of(i * tq, tq), tq)
            sp = _dot_nt(qa_ref[rows, :], ka[:w])
            if masked:
                last = jnp.where(tri, sp[:, w - tq:], -1e30)
                sp = last if w == tq else jnp.concatenate([sp[:, :w - tq], last], axis=1)
            p = jnp.exp2(sp)
            dsb = (p * _dot_nt(da_ref[rows, :], va[:w])).astype(BF16)
            dq = _dot_nt(ks[:, :w], dsb)
            dq_ref[:, rows] += dq[0:64]
            rs_ref[0, :, rows] += dq[64:72]
            new = (_dot(jnp.concatenate([qt_ref[:, rows], ones_row[:, :tq]], axis=0), dsb), _dot(dt_ref[:, rows], p.astype(BF16)))
            if w == tk:
                return tuple(c + d for c, d in zip(carry, new))
            return tuple(jnp.concatenate([c[:, :w] + d, c[:, w:]], axis=1) for c, d in zip(carry, new))

        carry = (jnp.zeros((80, tk), F32), jnp.zeros((64, tk), F32))
        for r in range(per):
            carry = tile(j * per + r, (r + 1) * tq, carry)
        dk, dv = lax.fori_loop((j + 1) * per, nqb, lambda i, c: tile(i, None, c), carry)
        dk_ref[...] = dk[0:64].astype(BF16)
        dv_ref[...] = dv.astype(BF16)
        dfk_ref[0] = dk[64:65]

    head_cols = lambda h, j: (0, h)
    head_rows = lambda h, j: (h, 0)
    once = dict(pipeline_mode=pl.Buffered(1))
    return pl.pallas_call(
        body, grid=(nh, s // tk),
        in_specs=[pl.BlockSpec((s, 128), head_cols, **once), pl.BlockSpec((s, 128), head_cols, **once),
                  pl.BlockSpec((64, s), head_rows, **once), pl.BlockSpec((64, s), head_rows, **once),
                  pl.BlockSpec((tk, 128), lambda h, j: (j, h)), pl.BlockSpec((tk, 128), lambda h, j: (j, h)),
                  pl.BlockSpec((64, tk), lambda h, j: (h, j))],
        out_specs=[pl.BlockSpec((64, s), head_rows), pl.BlockSpec((1, 8, s), lambda h, j: (h, 0, 0)),
                   pl.BlockSpec((64, tk), lambda h, j: (h, j)),
                   pl.BlockSpec((64, tk), lambda h, j: (h, j)), pl.BlockSpec((1, 1, tk), lambda h, j: (h, 0, j))],
        out_shape=[jax.ShapeDtypeStruct((512, s), F32), jax.ShapeDtypeStruct((nh, 8, s), F32),
                   jax.ShapeDtypeStruct((512, s), BF16),
                   jax.ShapeDtypeStruct((512, s), BF16), jax.ShapeDtypeStruct((nh, 1, s), F32)],
        name="fox_bwd", compiler_params=_cp("parallel", "arbitrary"))(q_aug, do_aug, qt, dot_, k_aug, v_aug, kt)


MEM_SCALE = 128 ** -0.5


def _mem_attn_fwd(pm, mkv):
    s = pm.shape[0]
    t = _row_tile(s)
    nm = mkv.shape[0]

    def body(q_ref, mk_ref, mv_ref, o_ref):
        for h in range(4):
            cols = slice(128 * h, 128 * (h + 1))
            sc = _dot_nt(q_ref[:, cols], mk_ref[:, cols]) * MEM_SCALE
            p = jnp.exp(sc - jnp.max(sc, axis=-1, keepdims=True))
            p = p / jnp.sum(p, axis=-1, keepdims=True)
            o_ref[:, cols] = _dot(p.astype(BF16), mv_ref[:, cols]).astype(BF16)

    return pl.pallas_call(
        body, grid=(s // t,),
        in_specs=[pl.BlockSpec((t, 512), lambda i: (i, C_MQ // 512)), pl.BlockSpec((nm, 512), lambda i: (0, 0)),
                  pl.BlockSpec((nm, 512), lambda i: (0, 1))],
        out_specs=pl.BlockSpec((t, 512), lambda i: (i, 0)),
        out_shape=jax.ShapeDtypeStruct((s, 512), BF16),
        name="mem_attn_fwd", compiler_params=_cp("parallel"))(pm, mkv, mkv)


def _mem_attn_bwd(pm, mkv, do):
    s = pm.shape[0]
    t = _row_tile(s)
    nm = mkv.shape[0]

    def body(q_ref, mk_ref, mv_ref, do_ref, dq_ref, dmk_ref, dmv_ref):
        @pl.when(pl.program_id(0) == 0)
        def _():
            dmk_ref[...] = jnp.zeros_like(dmk_ref)
            dmv_ref[...] = jnp.zeros_like(dmv_ref)

        for h in range(4):
            cols = slice(128 * h, 128 * (h + 1))
            qh, kh, vh, doh = q_ref[:, cols], mk_ref[:, cols], mv_ref[:, cols], do_ref[:, cols]
            sc = _dot_nt(qh, kh) * MEM_SCALE
            p = jnp.exp(sc - jnp.max(sc, axis=-1, keepdims=True))
            p = p / jnp.sum(p, axis=-1, keepdims=True)
            pb = p.astype(BF16)
            dp = _dot_nt(doh, vh)
            ds = (p * (dp - jnp.sum(p * dp, axis=-1, keepdims=True)) * MEM_SCALE).astype(BF16)
            dq_ref[:, cols] = _dot(ds, kh).astype(BF16)
            dmk_ref[:, cols] += _dot_tn(ds, qh)
            dmv_ref[:, cols] += _dot_tn(pb, doh)

    return pl.pallas_call(
        body, grid=(s // t,),
        in_specs=[pl.BlockSpec((t, 512), lambda i: (i, C_MQ // 512)), pl.BlockSpec((nm, 512), lambda i: (0, 0)),
                  pl.BlockSpec((nm, 512), lambda i: (0, 1)), pl.BlockSpec((t, 512), lambda i: (i, 0))],
        out_specs=[pl.BlockSpec((t, 512), lambda i: (i, 0)), pl.BlockSpec((nm, 512), lambda i: (0, 0)),
                   pl.BlockSpec((nm, 512), lambda i: (0, 0))],
        out_shape=[jax.ShapeDtypeStruct((s, 512), BF16), jax.ShapeDtypeStruct((nm, 512), F32),
                   jax.ShapeDtypeStruct((nm, 512), F32)],
        name="mem_attn_bwd", compiler_params=_cp("arbitrary"))(pm, mkv, mkv, do)


def _gain_grad(dxn_g, x, r, name):
    m, d = x.shape

    def body(d_ref, x_ref, r_ref, o_ref):
        o_ref[...] = _fold8(d_ref[...] * (x_ref[...] * r_ref[...]))

    return pl.pallas_call(body, out_shape=jax.ShapeDtypeStruct((8, d), F32), name=name,
                          compiler_params=pltpu.CompilerParams(vmem_limit_bytes=VMEM_LIMIT_BYTES))(dxn_g, x, r)


def _head_norm(o, gh):
    xs, rs = [], []
    for h in range(4):
        oh = o[:, 128 * h:128 * (h + 1)]
        r = lax.rsqrt(jnp.mean(oh * oh, axis=-1, keepdims=True) + EPS)
        xs.append(oh * r)
        rs.append(r)
    return xs, rs


def _merge_fwd(x, pm, o_gla, o_fox_t, o_mem, g_head, wg, wf, wm, wo, g_ffn):
    s = x.shape[0]
    t = min(s, 256)

    def body(x_ref, g0_ref, g1_ref, g2_ref, gg_ref, og_ref, of_ref, om_ref, gh_ref, wg_ref, wf_ref, wm_ref, wo_ref, gf_ref,
             mg_ref, h1_ref, u2_ref, r2_ref):
        xs, _ = _head_norm(og_ref[...], None)
        gg = gg_ref[...].astype(F32)
        sil = gg * _sigmoid(gg)
        ogn = jnp.concatenate(xs, axis=1) * gh_ref[...] * sil
        merged = (_sigmoid(g0_ref[...].astype(F32)) * _dot(ogn.astype(BF16), wg_ref[...])
                  + _sigmoid(g1_ref[...].astype(F32)) * _dot(of_ref[...].T, wf_ref[...])
                  + _sigmoid(g2_ref[...].astype(F32)) * _dot(om_ref[...], wm_ref[...]))
        mb = merged.astype(BF16)
        mg_ref[...] = mb
        h1 = x_ref[...] + _dot(mb, wo_ref[...])
        h1_ref[...] = h1
        r = lax.rsqrt(jnp.mean(h1 * h1, axis=-1, keepdims=True) + EPS)
        u2_ref[...] = ((h1 * r) * gf_ref[...]).astype(BF16)
        r2_ref[...] = r

    row = lambda i: (i, 0)
    const = lambda i: (0, 0)
    return pl.pallas_call(
        body, grid=(s // t,),
        in_specs=[pl.BlockSpec((t, D), row), pl.BlockSpec((t, D), lambda i: (i, 0)), pl.BlockSpec((t, D), lambda i: (i, 1)),
                  pl.BlockSpec((t, D), lambda i: (i, 2)), pl.BlockSpec((t, 512), lambda i: (i, C_GG // 512)),
                  pl.BlockSpec((t, 512), row), pl.BlockSpec((512, t), lambda i: (0, i)), pl.BlockSpec((t, 512), row),
                  pl.BlockSpec((1, 512), const), pl.BlockSpec((512, D), const), pl.BlockSpec((512, D), const),
                  pl.BlockSpec((512, D), const), pl.BlockSpec((D, D), const), pl.BlockSpec((1, D), const)],
        out_specs=[pl.BlockSpec((t, D), row), pl.BlockSpec((t, D), row), pl.BlockSpec((t, D), row), pl.BlockSpec((t, 1), row)],
        out_shape=[jax.ShapeDtypeStruct((s, D), BF16), jax.ShapeDtypeStruct((s, D), F32),
                   jax.ShapeDtypeStruct((s, D), BF16), jax.ShapeDtypeStruct((s, 1), F32)],
        name="merge_fwd", compiler_params=_cp("parallel"))(x, pm, pm, pm, pm, o_gla, o_fox_t, o_mem, g_head, wg, wf, wm, wo, g_ffn)


def _merge_bwd(dh1b, pm, o_gla, o_fox_t, o_mem, g_head, wg, wf, wm, wgt, wft, wmt, wot, spread, d_to_do):
    s = dh1b.shape[0]
    t = min(s, 256)

    def body(dh_ref, g0_ref, g1_ref, g2_ref, gg_ref, og_ref, of_ref, om_ref, gh_ref, wg_ref, wf_ref, wm_ref,
             wgt_ref, wft_ref, wmt_ref, wot_ref, sp_ref, dd_ref,
             dgt_ref, dgg_ref, dog_ref, da_ref, dot_ref, dom_ref, dwg_ref, dwf_ref, dwm_ref, dgh_ref):
        @pl.when(pl.program_id(0) == 0)
        def _():
            dwg_ref[...] = jnp.zeros_like(dwg_ref)
            dwf_ref[...] = jnp.zeros_like(dwf_ref)
            dwm_ref[...] = jnp.zeros_like(dwm_ref)
            dgh_ref[...] = jnp.zeros_like(dgh_ref)

        dmerged = _dot(dh_ref[...], wot_ref[...])
        og = og_ref[...]
        xs, rs = _head_norm(og, None)
        on = jnp.concatenate(xs, axis=1)
        gg = gg_ref[...].astype(F32)
        sg = _sigmoid(gg)
        sil = gg * sg
        gh = gh_ref[...]
        ognb = (on * gh * sil).astype(BF16)
        ofb, omb = of_ref[...].T, om_ref[...]
        douts = []
        for idx, (gref, ob, w_ref, wt_ref, dw_ref) in enumerate((
                (g0_ref, ognb, wg_ref, wgt_ref, dwg_ref), (g1_ref, ofb, wf_ref, wft_ref, dwf_ref),
                (g2_ref, omb, wm_ref, wmt_ref, dwm_ref))):
            gt = _sigmoid(gref[...].astype(F32))
            y = _dot(ob, w_ref[...])
            dgt_ref[:, D * idx:D * (idx + 1)] = (dmerged * y * gt * (1.0 - gt)).astype(BF16)
            dy = (gt * dmerged).astype(BF16)
            dw_ref[...] += _dot_tn(ob, dy)
            douts.append(_dot(dy, wt_ref[...]))
        dogn, dof, dom = douts
        dofb = dof.astype(BF16)
        dom_ref[...] = dom.astype(BF16)
        ind = (_iota((1536, 128), 0) % 512 // 64 == _iota((1536, 128), 1)).astype(BF16)
        delta = _dot(_split3(dofb.astype(F32) * ofb.astype(F32)), ind)
        da_ref[...] = (_dot(dofb, sp_ref[...]) + _dot(_split3(delta), dd_ref[...])).astype(BF16)
        dot_ref[...] = dofb.T
        dgg_ref[...] = (dogn * on * gh * (sg * (1.0 + gg * (1.0 - sg)))).astype(BF16)
        d_on = dogn * sil
        dgh_ref[...] += _fold8(d_on * on)
        dxn = d_on * gh
        outs = []
        for h in range(4):
            cols = slice(128 * h, 128 * (h + 1))
            dh_, xh = dxn[:, cols], xs[h]
            outs.append(rs[h] * (dh_ - xh * jnp.mean(dh_ * xh, axis=-1, keepdims=True)))
        dog_ref[...] = jnp.concatenate(outs, axis=1).astype(BF16)

    row = lambda i: (i, 0)
    const = lambda i: (0, 0)
    return pl.pallas_call(
        body, grid=(s // t,),
        in_specs=[pl.BlockSpec((t, D), row), pl.BlockSpec((t, D), lambda i: (i, 0)), pl.BlockSpec((t, D), lambda i: (i, 1)),
                  pl.BlockSpec((t, D), lambda i: (i, 2)), pl.BlockSpec((t, 512), lambda i: (i, C_GG // 512)),
                  pl.BlockSpec((t, 512), row), pl.BlockSpec((512, t), lambda i: (0, i)), pl.BlockSpec((t, 512), row),
                  pl.BlockSpec((1, 512), const), pl.BlockSpec((512, D), const), pl.BlockSpec((512, D), const),
                  pl.BlockSpec((512, D), const), pl.BlockSpec((D, 512), const), pl.BlockSpec((D, 512), const),
                  pl.BlockSpec((D, 512), const), pl.BlockSpec((D, D), const),
                  pl.BlockSpec((512, 1024), const), pl.BlockSpec((384, 1024), const)],
        out_specs=[pl.BlockSpec((t, 3 * D), row), pl.BlockSpec((t, 512), row), pl.BlockSpec((t, 512), row),
                   pl.BlockSpec((t, 1024), row), pl.BlockSpec((512, t), lambda i: (0, i)), pl.BlockSpec((t, 512), row),
                   pl.BlockSpec((512, D), const), pl.BlockSpec((512, D), const), pl.BlockSpec((512, D), const),
                   pl.BlockSpec((8, 512), const)],
        out_shape=[jax.ShapeDtypeStruct((s, 3 * D), BF16), jax.ShapeDtypeStruct((s, 512), BF16),
                   jax.ShapeDtypeStruct((s, 512), BF16), jax.ShapeDtypeStruct((s, 1024), BF16),
                   jax.ShapeDtypeStruct((512, s), BF16), jax.ShapeDtypeStruct((s, 512), BF16),
                   jax.ShapeDtypeStruct((512, D), F32), jax.ShapeDtypeStruct((512, D), F32),
                   jax.ShapeDtypeStruct((512, D), F32), jax.ShapeDtypeStruct((8, 512), F32)],
        name="merge_bwd", compiler_params=_cp("arbitrary"))(
            dh1b, pm, pm, pm, pm, o_gla, o_fox_t, o_mem, g_head, wg, wf, wm, wgt, wft, wmt, wot, spread, d_to_do)


def _ff2_loss(a, w2, h1, g_final, target):
    s, k = a.shape
    tm = min(s, 256)

    def body(a_ref, w_ref, h1_ref, g_ref, t_ref, dh_ref, dhb_ref, loss_ref, dg_ref):
        @pl.when(pl.program_id(0) == 0)
        def _():
            loss_ref[...] = jnp.zeros_like(loss_ref)
            dg_ref[...] = jnp.zeros_like(dg_ref)

        h2 = h1_ref[...] + _dot(_relu2_bf16(a_ref[...]), w_ref[...])
        r = lax.rsqrt(jnp.mean(h2 * h2, axis=-1, keepdims=True) + EPS)
        xn = h2 * r
        g = g_ref[...]
        err = xn * g - t_ref[...]
        e2 = _fold8(err * err)
        part = e2[:, 0:128]
        for c in range(1, D // 128):
            part = part + e2[:, 128 * c:128 * (c + 1)]
        loss_ref[...] += part
        dy = err * (1.0 / D)
        dg_ref[...] += _fold8(dy * xn)
        dxn = dy * g
        dh = r * (dxn - xn * jnp.mean(dxn * xn, axis=-1, keepdims=True))
        dh_ref[...] = dh
        dhb_ref[...] = dh.astype(BF16)

    row = lambda i: (i, 0)
    const = lambda i: (0, 0)
    return pl.pallas_call(
        body, grid=(s // tm,),
        in_specs=[pl.BlockSpec((tm, k), row), pl.BlockSpec((k, D), const, pipeline_mode=pl.Buffered(1)),
                  pl.BlockSpec((tm, D), row), pl.BlockSpec((1, D), const), pl.BlockSpec((tm, D), row)],
        out_specs=[pl.BlockSpec((tm, D), row), pl.BlockSpec((tm, D), row), pl.BlockSpec((8, 128), const),
                   pl.BlockSpec((8, D), const)],
        out_shape=[jax.ShapeDtypeStruct((s, D), F32), jax.ShapeDtypeStruct((s, D), BF16),
                   jax.ShapeDtypeStruct((8, 128), F32), jax.ShapeDtypeStruct((8, D), F32)],
        name="ff2_loss", compiler_params=_cp("arbitrary"))(a, w2, h1, g_final, target)


def _adam(w, g, m, v, name):
    _, r, c = w.shape
    tr = r
    for cand in (512, 256, 128, 64, 32, 16, 8):
        if r % cand == 0 and cand * c * 4 <= (1 << 20):
            tr = cand
            break
    c1 = 1.0 - ADAM_B1 ** ADAM_STEP
    c2 = 1.0 - ADAM_B2 ** ADAM_STEP

    def body(w_ref, g_ref, m_ref, v_ref, d_ref, nm_ref, nv_ref):
        gv = g_ref[...]
        nm = ADAM_B1 * m_ref[...] + (1.0 - ADAM_B1) * gv
        nv = ADAM_B2 * v_ref[...] + (1.0 - ADAM_B2) * (gv * gv)
        d_ref[...] = -ADAM_LR * ((nm / c1) / (jnp.sqrt(nv / c2) + ADAM_EPS) + ADAM_WD * w_ref[...])
        nm_ref[...] = nm
        nv_ref[...] = nv

    spec = pl.BlockSpec((1, tr, c), lambda i: (0, i, 0))
    return pl.pallas_call(
        body, grid=(r // tr,), in_specs=[spec] * 4, out_specs=[spec] * 3,
        out_shape=[jax.ShapeDtypeStruct((1, r, c), F32)] * 3, name=name, compiler_params=_cp("parallel"))(w, g, m, v)


def _row_block(r):
    return max(d for d in range(16, 513, 16) if r % d == 0)


def _add_half(core, a, b, name):
    n, r, c = b.shape
    tr = _row_block(r)

    def body(core_ref, a_ref, b_ref, o_ref):
        o_ref[...] = (a_ref[...].astype(F32) + b_ref[...].astype(F32)).astype(BF16)

    spec = pl.BlockSpec((1, tr, c), lambda k, i, core_ref: (k, i, 0))
    half = pl.BlockSpec((1, tr, c), lambda k, i, core_ref: (k, i + core_ref[0] * (r // tr), 0))
    return pl.pallas_call(
        body, grid_spec=pltpu.PrefetchScalarGridSpec(num_scalar_prefetch=1, grid=(n, r // tr), in_specs=[half, spec],
                                                     out_specs=spec),
        out_shape=jax.ShapeDtypeStruct((n, r, c), BF16), name=name, compiler_params=_cp("parallel", "parallel"))(core, a, b)


def _sum4(a, name):
    _, r, c = a.shape
    tr = _row_block(r)

    def body(a_ref, o_ref):
        o_ref[...] = ((a_ref[0].astype(F32) + a_ref[1].astype(F32)) + a_ref[2].astype(F32)) + a_ref[3].astype(F32)

    return pl.pallas_call(body, grid=(r // tr,), in_specs=[pl.BlockSpec((4, tr, c), lambda i: (0, i, 0))],
                          out_specs=pl.BlockSpec((tr, c), lambda i: (i, 0)),
                          out_shape=jax.ShapeDtypeStruct((r, c), F32), name=name, compiler_params=_cp("parallel"))(a)


def _adam_small(w, gathered, m, v):
    c1 = 1.0 - ADAM_B1 ** ADAM_STEP
    c2 = 1.0 - ADAM_B2 ** ADAM_STEP

    def body(w_ref, g_ref, m_ref, v_ref, gs_ref, d_ref, nm_ref, nv_ref):
        gv = g_ref[0]
        for dev in range(1, N_DEV):
            gv = gv + g_ref[dev]
        gs_ref[...] = gv
        nm = ADAM_B1 * m_ref[...] + (1.0 - ADAM_B1) * gv
        nv = ADAM_B2 * v_ref[...] + (1.0 - ADAM_B2) * (gv * gv)
        d_ref[...] = -ADAM_LR * ((nm / c1) / (jnp.sqrt(nv / c2) + ADAM_EPS) + ADAM_WD * w_ref[...])
        nm_ref[...] = nm
        nv_ref[...] = nv

    return pl.pallas_call(body, out_shape=[jax.ShapeDtypeStruct((8, D), F32)] * 4, name="adam_small")(w, gathered, m, v)


def _place():
    return lax.axis_index("x"), lax.axis_index("y"), lax.axis_index("c")


def _other_chips(x, y):
    return [(1 - x, y), (x, 1 - y), (1 - x, 1 - y)]


GATHER_SEMS = [pltpu.SemaphoreType.DMA((6,)), pltpu.SemaphoreType.DMA((6,)), pltpu.SemaphoreType.DMA]


def _gather_ops(in_refs, out_refs, sems):
    (p_ref,), (out_ref,) = in_refs, out_refs
    send_sems, recv_sems, local_sem = sems
    hr = p_ref.shape[0] // 2
    x, y, cc = _place()
    sibling = (x, y, 1 - cc)
    chips = _other_chips(x, y)

    def half(chip, core):
        return out_ref.at[2 * chip[0] + chip[1], pl.ds(core * hr, hr), :]

    def copy(k, chip, core, to, src=None):
        return pltpu.make_async_remote_copy(
            src_ref=half(chip, core) if src is None else src, dst_ref=half(chip, core),
            send_sem=send_sems.at[k], recv_sem=recv_sems.at[k], device_id=to, device_id_type=MESH)

    mine = pltpu.make_async_copy(p_ref, out_ref.at[2 * x + y], local_sem)
    my_half = p_ref.at[pl.ds(cc * hr, hr), :]
    first = [copy(j, (x, y), cc, (*chip, cc), src=my_half) for j, chip in enumerate(chips)]
    passed = [copy(3 + j, chip, cc, sibling) for j, chip in enumerate(chips)]

    def start():
        mine.start()
        for cp in first:
            cp.start()

    def finish():
        for j, chip in enumerate(chips):
            copy(j, chip, cc, (x, y, cc)).wait_recv()
            passed[j].start()
        for j, chip in enumerate(chips):
            copy(3 + j, chip, 1 - cc, (x, y, cc)).wait_recv()
        for cp in first + passed:
            cp.wait_send()
        mine.wait()

    return start, finish


def _gather_side(p):
    return _Side([p], [jax.ShapeDtypeStruct((N_CHIPS,) + p.shape, p.dtype)], GATHER_SEMS, _gather_ops)


def _swap_halves(g):
    n, r, c = g.shape
    hr = r // 2

    def body(g_ref, out_ref, send_sem, recv_sem):
        x, y, cc = _place()
        cp = pltpu.make_async_remote_copy(
            src_ref=g_ref.at[:, pl.ds((1 - cc) * hr, hr), :], dst_ref=out_ref,
            send_sem=send_sem, recv_sem=recv_sem, device_id=(x, y, 1 - cc), device_id_type=MESH)
        cp.start()
        cp.wait()

    any_spec = pl.BlockSpec(memory_space=pl.ANY)
    return pl.pallas_call(
        body, out_shape=jax.ShapeDtypeStruct((n, hr, c), g.dtype), in_specs=[any_spec], out_specs=any_spec,
        scratch_shapes=[pltpu.SemaphoreType.DMA, pltpu.SemaphoreType.DMA], name="swap_halves")(g)


SCATTER_SEMS = [pltpu.SemaphoreType.DMA((7,)), pltpu.SemaphoreType.DMA((7,)), pltpu.SemaphoreType.DMA]


def _scatter_ops(in_refs, out_refs, sems):
    (p_ref,), (out_ref,) = in_refs, out_refs
    send_sems, recv_sems, local_sem = sems
    hr = p_ref.shape[1]
    x, y, cc = _place()
    me = 2 * x + y
    sibling = (x, y, 1 - cc)
    chips = _other_chips(x, y)
    ids = [2 * chip[0] + chip[1] for chip in chips]

    def land(src, core):
        return out_ref.at[src, pl.ds(core * hr, hr), :]

    def copy(k, src_ref, dst_ref, to):
        return pltpu.make_async_remote_copy(src_ref=src_ref, dst_ref=dst_ref, send_sem=send_sems.at[k],
                                            recv_sem=recv_sems.at[k], device_id=to, device_id_type=MESH)

    mine = pltpu.make_async_copy(p_ref.at[me], land(me, cc), local_sem)
    sends = [copy(j, p_ref.at[ids[j]], land(me, cc), (*chip, cc)) for j, chip in enumerate(chips)]
    sends.append(copy(3, p_ref.at[me], land(me, cc), sibling))
    passed = [copy(4 + j, land(ids[j], cc), land(ids[j], cc), sibling) for j in range(3)]

    def start():
        mine.start()
        for cp in sends:
            cp.start()

    def finish():
        for j in range(3):
            copy(j, p_ref.at[me], land(ids[j], cc), (x, y, cc)).wait_recv()
            passed[j].start()
        copy(3, p_ref.at[me], land(me, 1 - cc), (x, y, cc)).wait_recv()
        for j in range(3):
            copy(4 + j, p_ref.at[me], land(ids[j], 1 - cc), (x, y, cc)).wait_recv()
        for cp in sends + passed:
            cp.wait_send()
        mine.wait()

    return start, finish


def _scatter_side(p):
    n, hr, c = p.shape
    return _Side([p], [jax.ShapeDtypeStruct((n, 2 * hr, c), p.dtype)], SCATTER_SEMS, _scatter_ops)


def _gather_small(blk):
    m, n = blk.shape

    def body(x_ref, out_ref, send_sems, recv_sems, local_sem):
        x, y, cc = _place()
        me, sibling = (x, y, cc), (x, y, 1 - cc)
        chips = _other_chips(x, y)

        def slot(px, py, pc):
            return out_ref.at[4 * px + 2 * py + pc]

        def copy(k, block, to, src=None):
            return pltpu.make_async_remote_copy(
                src_ref=slot(*block) if src is None else src, dst_ref=slot(*block),
                send_sem=send_sems.at[k], recv_sem=recv_sems.at[k], device_id=to, device_id_type=MESH)

        mine = pltpu.make_async_copy(x_ref, slot(*me), local_sem)
        mine.start()
        first = [copy(0, me, sibling, src=x_ref)]
        first += [copy(1 + j, me, (*chip, cc), src=x_ref) for j, chip in enumerate(chips)]
        for cp in first:
            cp.start()
        passed = [copy(4 + j, (*chip, cc), sibling) for j, chip in enumerate(chips)]
        for j, chip in enumerate(chips):
            copy(1 + j, (*chip, cc), me).wait_recv()
            passed[j].start()
        copy(0, sibling, me).wait_recv()
        for j, chip in enumerate(chips):
            copy(4 + j, (*chip, 1 - cc), me).wait_recv()
        for cp in first + passed:
            cp.wait_send()
        mine.wait()

    vmem = pl.BlockSpec(memory_space=pltpu.VMEM)
    return pl.pallas_call(
        body, out_shape=jax.ShapeDtypeStruct((N_DEV, m, n), blk.dtype), in_specs=[vmem], out_specs=vmem,
        scratch_shapes=[pltpu.SemaphoreType.DMA((7,)), pltpu.SemaphoreType.DMA((7,)), pltpu.SemaphoreType.DMA],
        name="gather_small")(blk)


def _pack_a(sh, dtype):
    w = sh["w_in"].astype(dtype)
    return jnp.concatenate([w[:, 0:PACK_W], jnp.pad(w[:, PACK_W:], ((0, 0), (0, 2 * PACK_W - w.shape[1])))], axis=0)


def _pack_b(sh, dtype):
    o3 = jnp.concatenate([sh["w_gla_o"], sh["w_fox_o"], sh["w_mem_o"], jnp.zeros((512, 256), sh["w_gla_o"].dtype)], axis=1)
    au = jnp.pad(sh["w_alpha_up"], ((0, PACK_ROWS_B - 3072 - 16), (0, PACK_W - 64)))
    return jnp.concatenate([sh["w_ff1"], sh["w_ff2"], sh["w_mem_kv"], sh["w_out"], o3, au], axis=0).astype(dtype)


def _unpack_a(pa):
    return {"w_in": jnp.concatenate([pa[0:1024], pa[1024:2048, 0:1670 - PACK_W]], axis=1)}


def _unpack_b(pb):
    return {"w_ff1": pb[0:1024], "w_ff2": pb[1024:2048], "w_mem_kv": pb[2048:2304], "w_out": pb[2304:2560],
            "w_gla_o": pb[2560:3072, 0:256], "w_fox_o": pb[2560:3072, 256:512], "w_mem_o": pb[2560:3072, 512:768],
            "w_alpha_up": pb[3072:3088, 0:64]}


def _unpack(packed):
    return {**_unpack_a(packed[0:PACK_ROWS_A]), **_unpack_b(packed[PACK_ROWS_A:])}


def _split_shards(name, full):
    return jnp.split(full, N_CHIPS, axis=SHARD_AXIS[name])


def _pack_small(vals, scalar=None):
    row4 = jnp.concatenate([vals["b_alpha"].reshape(-1), vals["b_forget"].reshape(-1), jnp.zeros((D - 264,), F32)])
    row5 = jnp.concatenate([vals["g_gla_head"].reshape(-1), jnp.zeros((D - 512,), F32)])
    row6 = jnp.zeros((D,), F32) if scalar is None else jnp.broadcast_to(scalar, (D,))
    rows = [vals["g_mix"].reshape(-1), vals["g_mem"].reshape(-1), vals["g_ffn"].reshape(-1), vals["g_final"].reshape(-1),
            row4, row5, row6, jnp.zeros((D,), F32)]
    return jnp.stack(rows)


def _unpack_small(blk):
    return {"g_mix": blk[0].reshape(1, D), "g_mem": blk[1].reshape(1, D), "g_ffn": blk[2].reshape(1, D),
            "g_final": blk[3].reshape(D), "b_alpha": blk[4, 0:256].reshape(1, 256), "b_forget": blk[4, 256:264].reshape(1, 8),
            "g_gla_head": blk[5, 0:512].reshape(1, 4, 128)}


def _local_step(x, mem, target, wb, small, exchange=None):
    s = x.shape[0]
    nm = mem.shape[0]
    t = _row_tile(s)
    nb = s // t
    b_alpha = small["b_alpha"].reshape(1, 256)
    bias_e = jnp.concatenate([jnp.zeros((FF_LANE,), F32), small["b_forget"].reshape(-1),
                              jnp.zeros((PE_W - FF_LANE - 8,), F32)]).reshape(1, PE_W)
    g_mix, g_mem, g_ffn = small["g_mix"].reshape(1, D), small["g_mem"].reshape(1, D), small["g_ffn"].reshape(1, D)
    g_final = small["g_final"].reshape(1, D)
    g_head = small["g_gla_head"].reshape(1, 512)

    if exchange is None:
        u, r1 = _rms_fwd(x, g_mix, "norm_mix")
    else:
        u, r1, gathered = _rms_fwd(x, g_mix, "norm_mix", side=exchange.gather_a)
        wb = exchange.weights_a(gathered)
    w_in = wb["w_in"]
    w_main = jnp.concatenate([w_in[:, 3608:6680], w_in[:, 0:1536], w_in[:, 1552:3088], w_in[:, 3096:3608]], axis=1)
    w_e = jnp.concatenate([w_in[:, 1536:1552], w_in[:, 3088:3096], jnp.zeros((D, PE_W - 24), BF16)], axis=1)
    w_in_pt = _transpose(jnp.concatenate([w_main, w_e], axis=1), "t_w_in")
    big = min(s, 1024)
    if exchange is None:
        pm, pe = _proj(u, w_main, w_e)
    else:
        pm, pe, gathered = _proj(u, w_main, w_e, side=exchange.gather_b)
        wb = {**wb, **exchange.weights_b(gathered)}
    wau_p = jnp.concatenate([wb["w_alpha_up"], jnp.zeros((PE_W - 16, 256), BF16)], axis=0)
    o_gla, states = _gla_fwd(pm, pe, wau_p, b_alpha)
    fcum = _fcum_fwd(pe, bias_e)
    tb = _fox_tables()
    qf_aug, k_aug, v_aug, vt, qt, kt = _fox_prep(pm, fcum, None, tb, backward=False)
    o_fox, lse = _fox_fwd(k_aug, qf_aug, vt)
    mn, rm = _rms_fwd(mem, g_mem, "norm_mem")
    mkv = _mm_nn(mn, wb["w_mem_kv"], out_dtype=BF16, tm=nm, tn=512, tk=D, name="mem_kv")
    o_mem = _mem_attn_fwd(pm, mkv)
    merged, h1, u2, r2 = _merge_fwd(x, pm, o_gla, o_fox, o_mem, g_head, wb["w_gla_o"], wb["w_fox_o"], wb["w_mem_o"],
                                    wb["w_out"], g_ffn)
    a = _mm_nn(u2, wb["w_ff1"], out_dtype=BF16, tm=big, tn=1024, tk=D, name="ff1")
    dh2, dh2b, loss8, dgfin8 = _ff2_loss(a, wb["w_ff2"], h1, g_final, target)
    loss = 0.5 * jnp.sum(loss8) / D

    da = _mm_nn(dh2b, _transpose(wb["w_ff2"], "t_w_ff2"), out_dtype=BF16, tm=big, tn=1024, tk=D, name="d_act",
                epi=lambda acc, at: acc * (2.0 * jnp.maximum(at.astype(F32), 0.0)), extra=a)
    gw = {}
    gw["w_ff2"] = _mm_tn(a, dh2b, tm=1024, tn=D, ts=big, name="dw_ff2", a_fn=_relu2_bf16)
    gw["w_ff1"] = _mm_tn(u2, da, tm=D, tn=1024, ts=big, name="dw_ff1")
    dh1, dh1b, dgffn8 = _mm_norm_bwd([da], _transpose(wb["w_ff1"], "t_w_ff1"), h1, r2, g_ffn, dh2, name="d_h1", want_bf16=True)
    gw["w_out"] = _mm_tn(merged, dh1b, tm=D, tn=D, ts=big, name="dw_out")
    (dgates, dgg, do_gla, do_aug, do_t, do_mem, gw["w_gla_o"], gw["w_fox_o"], gw["w_mem_o"], dgh8) = _merge_bwd(
        dh1b, pm, o_gla, o_fox, o_mem, g_head, wb["w_gla_o"], wb["w_fox_o"], wb["w_mem_o"],
        *[_transpose(wb[n], "t_" + n) for n in ("w_gla_o", "w_fox_o", "w_mem_o", "w_out")], tb["spread"], tb["d_to_do"])
    dgq, dgk, dgv, de_gla, dwau_p, dba8 = _gla_bwd(pm, pe, wau_p, wau_p.T, b_alpha, do_gla, states)
    gw["w_alpha_up"] = dwau_p[0:16, :]
    q_aug = _fox_prep(pm, fcum, lse.reshape(8, s), tb, backward=True)
    dfq_t, dfrow, dfk_t, dfv_t, dfcol = _fox_bwd(q_aug, do_aug, qt, do_t, k_aug, v_aug, kt)
    dfq, dfk, dfv, df = _fox_post(dfq_t, dfk_t, dfv_t, dfrow[:, 0, :], dfcol.reshape(8, s), tb)
    de_fox, dbf8 = _fcum_bwd(pe, bias_e, df)
    dmq, dmk, dmv = _mem_attn_bwd(pm, mkv, do_mem)
    dmkv = jnp.concatenate([dmk, dmv], axis=1).astype(BF16)
    gw["w_mem_kv"] = _mm_tn(mn, dmkv, tm=D, tn=D, ts=nm, name="dw_mem_kv")
    dmn_g = _mm_nn(dmkv, _transpose(wb["w_mem_kv"], "t_w_mem_kv"), out_dtype=F32, tm=nm, tn=D, tk=D, name="d_mem_norm")
    dgmem8 = _gain_grad(dmn_g, mem, rm, "dg_mem")
    de = (de_gla + de_fox).astype(BF16)
    dproj = [dgates, dgq, dgk, dgv, dgg, dfq, dfk, dfv, dmq, de]
    dw_gates = _mm_tn(u, dgates, tm=D, tn=1024, ts=big, name="dw_in_gates")
    dw_g = _mm_tn_cat(u, [dgq, dgk, dgv], ts=big, name="dw_in_gla")
    dw_gf = _mm_tn_cat(u, [dgg, dfq], ts=big, name="dw_in_gg_fq")
    dw_f = _mm_tn_cat(u, [dfk, dfv], ts=big, name="dw_in_fk_fv")
    dw_m = _mm_tn_cat(u, [dmq, de], ts=big, name="dw_in_mq_narrow")
    gw["w_in"] = jnp.concatenate([dw_g, dw_gf[:, 0:512], dw_m[:, 512:528], dw_gf[:, 512:1024], dw_f,
                                  dw_m[:, 528:536], dw_m[:, 0:512], dw_gates], axis=1)
    if exchange is None:
        grad_x, dgmix8 = _mm_norm_bwd(dproj, w_in_pt, x, r1, g_mix, dh1, name="d_x", want_bf16=False)
        exchanged = None
    else:
        grad_x, dgmix8, exchanged = _mm_norm_bwd(dproj, w_in_pt, x, r1, g_mix, dh1, name="d_x", want_bf16=False,
                                                 side=exchange.scatter(gw))
    gs = {"g_mix": dgmix8.sum(0), "g_mem": dgmem8.sum(0), "g_ffn": dgffn8.sum(0), "g_final": dgfin8.sum(0),
          "b_alpha": dba8.sum(0), "b_forget": dbf8.sum(0)[FF_LANE:FF_LANE + 8], "g_gla_head": dgh8.sum(0)}
    return loss, grad_x, gw, gs, exchanged


def kernel(x, mem, g_mix, w_in, w_alpha_up, b_alpha, b_forget, g_gla_head, g_mem, w_mem_kv, w_gla_o, w_fox_o, w_mem_o, w_out, g_ffn, w_ff1, w_ff2, g_final, loss_target, m_g_mix, m_w_in, m_w_alpha_up, m_b_alpha, m_b_forget, m_g_gla_head, m_g_mem, m_w_mem_kv, m_w_gla_o, m_w_fox_o, m_w_mem_o, m_w_out, m_g_ffn, m_w_ff1, m_w_ff2, m_g_final, v_g_mix, v_w_in, v_w_alpha_up, v_b_alpha, v_b_forget, v_g_gla_head, v_g_mem, v_w_mem_kv, v_w_gla_o, v_w_fox_o, v_w_mem_o, v_w_out, v_g_ffn, v_w_ff1, v_w_ff2, v_g_final):
    args = dict(locals())
    w_sh = {n: args[n][0] for n in WEIGHTS}
    small = {n: args[n] for n in SMALL}

    def whole(parts):
        return {n: jnp.concatenate([p[n] for p in parts], axis=SHARD_AXIS[n]) for n in parts[0]}

    class Exchange:
        gather_a = _gather_side(_pack_a(w_sh, BF16))
        gather_b = _gather_side(_pack_b(w_sh, BF16))

        @staticmethod
        def weights_a(gathered):
            return whole([_unpack_a(gathered[k]) for k in range(N_CHIPS)])

        @staticmethod
        def weights_b(gathered):
            return whole([_unpack_b(gathered[k]) for k in range(N_CHIPS)])

        @staticmethod
        def scatter(gw):
            by_chip = {n: _split_shards(n, gw[n]) for n in WEIGHTS}
            packed = jnp.stack([jnp.concatenate([_pack_a({n: by_chip[n][k] for n in WEIGHTS}, BF16),
                                                 _pack_b({n: by_chip[n][k] for n in WEIGHTS}, BF16)], axis=0)
                                for k in range(N_CHIPS)])
            core = lax.axis_index("c").astype(jnp.int32).reshape(1)
            return _scatter_side(_add_half(core, packed, _swap_halves(packed), "chip_sum"))

    loss, grad_x, gw, gs, by_chip = _local_step(x[0], mem[0], loss_target[0], None, small, Exchange)
    g_out = {n: g[None] for n, g in _unpack(_sum4(by_chip, "shard_sum")).items()}
    d_out, m_out, v_out = {}, {}, {}
    for n in WEIGHTS:
        d_out[n], m_out[n], v_out[n] = _adam(args[n], g_out[n], args["m_" + n], args["v_" + n], "adam_" + n)

    small_all = _gather_small(_pack_small(gs, loss))
    sm = {n: args["m_" + n] for n in SMALL}
    sv = {n: args["v_" + n] for n in SMALL}
    gs_sum, sd, snm, snv = _adam_small(_pack_small(small), small_all, _pack_small(sm), _pack_small(sv))
    gs_o, sd_o, snm_o, snv_o = _unpack_small(gs_sum), _unpack_small(sd), _unpack_small(snm), _unpack_small(snv)

    names = ["g_mix", "w_in", "w_alpha_up", "b_alpha", "b_forget", "g_gla_head", "g_mem", "w_mem_kv", "w_gla_o", "w_fox_o",
             "w_mem_o", "w_out", "g_ffn", "w_ff1", "w_ff2", "g_final"]

    def pick(big, sml, n):
        return big[n] if n in big else sml[n]

    outs = [gs_sum[6, 0], grad_x[None]]
    for big, sml in ((g_out, gs_o), (d_out, sd_o), (m_out, snm_o), (v_out, snv_o)):
        outs += [pick(big, sml, n) for n in names]
    return tuple(outs)
```

```python
import functools

import numpy as np
import jax
import jax.numpy as jnp
from jax import lax
from jax.experimental import pallas as pl
from jax.experimental.pallas import tpu as pltpu

F32 = jnp.float32
BF16 = jnp.bfloat16
HI = lax.Precision.HIGHEST
MESH = pl.DeviceIdType.MESH

EPS = 1e-6
D = 1024
CHUNK = 64
GLA_TAU = 16.0
N_CHIPS = 4
N_DEV = 8
VMEM_LIMIT_BYTES = 56 * 1024 * 1024

ADAM_LR, ADAM_B1, ADAM_B2, ADAM_EPS, ADAM_WD, ADAM_STEP = 0.001, 0.9, 0.999, 1e-08, 0.01, 10

PM_W = 6656
PE_W = 128
C_GQ, C_GK, C_GV, C_GG, C_FQ, C_FK, C_FV, C_MQ = 3072, 3328, 3584, 4096, 4608, 5120, 5632, 6144
FF_LANE = 16

WEIGHTS = ("w_in", "w_alpha_up", "w_mem_kv", "w_gla_o", "w_fox_o", "w_mem_o", "w_out", "w_ff1", "w_ff2")
SHARD_AXIS = {"w_in": 1, "w_alpha_up": 1, "w_mem_kv": 0, "w_gla_o": 1, "w_fox_o": 1, "w_mem_o": 1, "w_out": 0,
              "w_ff1": 1, "w_ff2": 0}
SMALL = ("g_mix", "g_mem", "g_ffn", "g_final", "b_alpha", "b_forget", "g_gla_head")
PACK_W = 1024
PACK_ROWS_A = 2048
PACK_ROWS_B = 3104
PACK_ROWS = PACK_ROWS_A + PACK_ROWS_B


def _cp(*sem):
    return pltpu.CompilerParams(dimension_semantics=sem, vmem_limit_bytes=VMEM_LIMIT_BYTES)


def _dot(a, b, **kw):
    return jnp.dot(a, b, preferred_element_type=F32, **kw)


def _dot_nt(a, b, **kw):
    return lax.dot_general(a, b, (((1,), (1,)), ((), ())), preferred_element_type=F32, **kw)


def _dot_tn(a, b, **kw):
    return lax.dot_general(a, b, (((0,), (0,)), ((), ())), preferred_element_type=F32, **kw)


def _sigmoid(x):
    return 1.0 / (1.0 + jnp.exp(-x))


def _log_sigmoid(x):
    return -(jnp.maximum(-x, 0.0) + jnp.log1p(jnp.exp(-jnp.abs(x))))


def _fold8(x):
    m, n = x.shape
    return x.reshape(m // 8, 8, n).sum(axis=0)


def _iota(shape, dim):
    return lax.broadcasted_iota(jnp.int32, shape, dim)


def _row_tile(s):
    return min(s, 512)


class _Side:
    def __init__(self, inputs, out_shape, scratch, ops):
        self.inputs, self.out_shape, self.scratch, self.ops = list(inputs), list(out_shape), list(scratch), ops


ANY_SPEC = pl.BlockSpec(memory_space=pl.ANY)


def _mm_nn(a, b, *, out_dtype, tm, tn, tk, name, a_fn=None, epi=None, extra=None):
    m, k = a.shape
    _, n = b.shape
    nk = k // tk

    def body_one(*refs):
        a_ref, b_ref = refs[0], refs[1]
        at = a_ref[...] if a_fn is None else a_fn(a_ref[...])
        r = _dot(at, b_ref[...])
        if epi is not None:
            r = epi(r, None if extra is None else refs[2][...])
        refs[-1][...] = r.astype(out_dtype)

    if nk == 1:
        in_specs = [pl.BlockSpec((tm, k), lambda i, j: (i, 0)), pl.BlockSpec((k, tn), lambda i, j: (0, j))]
        args = [a, b]
        if extra is not None:
            in_specs.append(pl.BlockSpec((tm, tn), lambda i, j: (i, j)))
            args.append(extra)
        return pl.pallas_call(
            body_one, grid=(m // tm, n // tn), in_specs=in_specs, out_specs=pl.BlockSpec((tm, tn), lambda i, j: (i, j)),
            out_shape=jax.ShapeDtypeStruct((m, n), out_dtype), name=name, compiler_params=_cp("parallel", "parallel"))(*args)

    def body(*refs):
        if extra is None:
            a_ref, b_ref, o_ref, acc = refs
            x_ref = None
        else:
            a_ref, b_ref, x_ref, o_ref, acc = refs
        kk = pl.program_id(2)

        @pl.when(kk == 0)
        def _():
            acc[...] = jnp.zeros_like(acc)

        at = a_ref[...]
        if a_fn is not None:
            at = a_fn(at)
        acc[...] += _dot(at, b_ref[...])

        @pl.when(kk == nk - 1)
        def _():
            r = acc[...]
            if epi is not None:
                r = epi(r, None if x_ref is None else x_ref[...])
            o_ref[...] = r.astype(out_dtype)

    in_specs = [pl.BlockSpec((tm, tk), lambda i, j, kk: (i, kk)), pl.BlockSpec((tk, tn), lambda i, j, kk: (kk, j))]
    args = [a, b]
    if extra is not None:
        in_specs.append(pl.BlockSpec((tm, tn), lambda i, j, kk: (i, j)))
        args.append(extra)
    return pl.pallas_call(
        body, grid=(m // tm, n // tn, nk), in_specs=in_specs,
        out_specs=pl.BlockSpec((tm, tn), lambda i, j, kk: (i, j)),
        out_shape=jax.ShapeDtypeStruct((m, n), out_dtype),
        scratch_shapes=[pltpu.VMEM((tm, tn), F32)], name=name,
        compiler_params=_cp("parallel", "parallel", "arbitrary"))(*args)


def _mm_tn(a, b, *, tm, tn, ts, name, a_fn=None):
    s, m = a.shape
    _, n = b.shape
    ns = s // ts

    def body(a_ref, b_ref, o_ref, acc):
        kk = pl.program_id(2)

        @pl.when(kk == 0)
        def _():
            acc[...] = jnp.zeros_like(acc)

        at = a_ref[...]
        if a_fn is not None:
            at = a_fn(at)
        acc[...] += _dot_tn(at, b_ref[...])

        @pl.when(kk == ns - 1)
        def _():
            o_ref[...] = acc[...]

    return pl.pallas_call(
        body, grid=(m // tm, n // tn, ns),
        in_specs=[pl.BlockSpec((ts, tm), lambda i, j, kk: (kk, i)), pl.BlockSpec((ts, tn), lambda i, j, kk: (kk, j))],
        out_specs=pl.BlockSpec((tm, tn), lambda i, j, kk: (i, j)),
        out_shape=jax.ShapeDtypeStruct((m, n), F32),
        scratch_shapes=[pltpu.VMEM((tm, tn), F32)], name=name,
        compiler_params=_cp("parallel", "parallel", "arbitrary"))(a, b)


def _mm_tn_cat(a, bs, *, ts, name):
    s, m = a.shape
    n = sum(b.shape[1] for b in bs)
    ns = s // ts
    nb = len(bs)

    def body(*refs):
        a_ref, b_refs, o_ref, acc = refs[0], refs[1:1 + nb], refs[1 + nb], refs[2 + nb]
        kk = pl.program_id(0)

        @pl.when(kk == 0)
        def _():
            acc[...] = jnp.zeros_like(acc)

        bt = b_refs[0][...] if nb == 1 else jnp.concatenate([r[...] for r in b_refs], axis=1)
        acc[...] += _dot_tn(a_ref[...], bt)

        @pl.when(kk == ns - 1)
        def _():
            o_ref[...] = acc[...]

    return pl.pallas_call(
        body, grid=(ns,),
        in_specs=[pl.BlockSpec((ts, m), lambda kk: (kk, 0))] + [pl.BlockSpec((ts, b.shape[1]), lambda kk: (kk, 0)) for b in bs],
        out_specs=pl.BlockSpec((m, n), lambda kk: (0, 0)), out_shape=jax.ShapeDtypeStruct((m, n), F32),
        scratch_shapes=[pltpu.VMEM((m, n), F32)], name=name, compiler_params=_cp("arbitrary"))(a, *bs)


def _proj(u, w_main, w_e, side=None):
    s, k = u.shape
    n = w_main.shape[1]
    tm, tn = min(s, 1024), n // 4
    n_sin = 0 if side is None else len(side.inputs)
    n_sout = 0 if side is None else len(side.out_shape)

    def body(u_ref, w_ref, we_ref, *rest):
        pm_ref, pe_ref = rest[n_sin:n_sin + 2]
        i, j = pl.program_id(0), pl.program_id(1)
        if side is not None:
            start, finish = side.ops(rest[:n_sin], rest[n_sin + 2:n_sin + 2 + n_sout], rest[n_sin + 2 + n_sout:])
            pl.when((i == 0) & (j == 0))(start)
        ut = u_ref[...]
        pm_ref[...] = _dot(ut, w_ref[...]).astype(BF16)

        @pl.when(j == 0)
        def _():
            pe_ref[...] = _dot(ut, we_ref[...])

        if side is not None:
            pl.when((i == s // tm - 1) & (j == n // tn - 1))(finish)

    side_in = [] if side is None else side.inputs
    return pl.pallas_call(
        body, grid=(s // tm, n // tn),
        in_specs=[pl.BlockSpec((tm, k), lambda i, j: (i, 0)), pl.BlockSpec((k, tn), lambda i, j: (0, j)),
                  pl.BlockSpec((k, PE_W), lambda i, j: (0, 0))] + [ANY_SPEC] * n_sin,
        out_specs=[pl.BlockSpec((tm, tn), lambda i, j: (i, j)), pl.BlockSpec((tm, PE_W), lambda i, j: (i, 0))]
        + [ANY_SPEC] * n_sout,
        out_shape=[jax.ShapeDtypeStruct((s, n), BF16), jax.ShapeDtypeStruct((s, PE_W), F32)]
        + ([] if side is None else side.out_shape),
        scratch_shapes=[] if side is None else side.scratch,
        name="proj_main", compiler_params=_cp("arbitrary", "arbitrary"))(u, w_main, w_e, *side_in)


def _transpose(w, name):
    r, c = w.shape
    tr = min(r, 256)

    def body(w_ref, o_ref):
        o_ref[...] = w_ref[...].T

    return pl.pallas_call(body, grid=(r // tr,), in_specs=[pl.BlockSpec((tr, c), lambda i: (i, 0))],
                          out_specs=pl.BlockSpec((c, tr), lambda i: (0, i)),
                          out_shape=jax.ShapeDtypeStruct((c, r), w.dtype), name=name, compiler_params=_cp("parallel"))(w)


def _relu2_bf16(t):
    r = jnp.maximum(t.astype(F32), 0.0)
    return (r * r).astype(BF16)


def _rms_fwd(x, g, name, side=None):
    s, d = x.shape
    tm = min(s, 512)
    n_sin = 0 if side is None else len(side.inputs)
    n_sout = 0 if side is None else len(side.out_shape)

    def body(x_ref, g_ref, *rest):
        u_ref, r_ref = rest[n_sin:n_sin + 2]
        if side is not None:
            start, finish = side.ops(rest[:n_sin], rest[n_sin + 2:n_sin + 2 + n_sout], rest[n_sin + 2 + n_sout:])
            pl.when(pl.program_id(0) == 0)(start)
        xv = x_ref[...]
        r = lax.rsqrt(jnp.mean(xv * xv, axis=-1, keepdims=True) + EPS)
        u_ref[...] = ((xv * r) * g_ref[...]).astype(BF16)
        r_ref[...] = r
        if side is not None:
            pl.when(pl.program_id(0) == s // tm - 1)(finish)

    side_in = [] if side is None else side.inputs
    return pl.pallas_call(
        body, grid=(s // tm,),
        in_specs=[pl.BlockSpec((tm, d), lambda i: (i, 0)), pl.BlockSpec((1, d), lambda i: (0, 0))] + [ANY_SPEC] * n_sin,
        out_specs=[pl.BlockSpec((tm, d), lambda i: (i, 0)), pl.BlockSpec((tm, 1), lambda i: (i, 0))] + [ANY_SPEC] * n_sout,
        out_shape=[jax.ShapeDtypeStruct((s, d), BF16), jax.ShapeDtypeStruct((s, 1), F32)]
        + ([] if side is None else side.out_shape),
        scratch_shapes=[] if side is None else side.scratch,
        name=name, compiler_params=_cp("parallel" if side is None else "arbitrary"))(x, g, *side_in)


def _mm_norm_bwd(a_parts, b, xin, r, g, dres, *, name, want_bf16, side=None):
    s = a_parts[0].shape[0]
    k = b.shape[0]
    na = len(a_parts)
    offs = [sum(p.shape[1] for p in a_parts[:i]) for i in range(na)]
    assert offs[-1] + a_parts[-1].shape[1] == k
    tm = min(s, 256)
    n_out = 3 if want_bf16 else 2
    n_sin = 0 if side is None else len(side.inputs)
    n_sout = 0 if side is None else len(side.out_shape)

    def body(*refs):
        a_refs = refs[:na]
        b_ref, x_ref, r_ref, g_ref, dres_ref = refs[na:na + 5]
        rest = refs[na + 5:]
        outs = rest[n_sin:n_sin + n_out]
        dx_ref, dg_ref = outs[0], outs[-1]
        if side is not None:
            start, finish = side.ops(rest[:n_sin], rest[n_sin + n_out:n_sin + n_out + n_sout], rest[n_sin + n_out + n_sout:])
            pl.when(pl.program_id(0) == 0)(start)

        @pl.when(pl.program_id(0) == 0)
        def _():
            dg_ref[...] = jnp.zeros_like(dg_ref)

        du = _dot(a_refs[0][...], b_ref[0:a_parts[0].shape[1], :])
        for a_ref, off, part in zip(a_refs[1:], offs[1:], a_parts[1:]):
            du = du + _dot(a_ref[...], b_ref[off:off + part.shape[1], :])
        xn = x_ref[...] * r_ref[...]
        dg_ref[...] += _fold8(du * xn)
        dxn = du * g_ref[...]
        dx = dres_ref[...] + r_ref[...] * (dxn - xn * jnp.mean(dxn * xn, axis=-1, keepdims=True))
        dx_ref[...] = dx
        if want_bf16:
            outs[1][...] = dx.astype(BF16)
        if side is not None:
            pl.when(pl.program_id(0) == s // tm - 1)(finish)

    row = lambda i: (i, 0)
    const = lambda i: (0, 0)
    out_specs = [pl.BlockSpec((tm, D), row)]
    out_shape = [jax.ShapeDtypeStruct((s, D), F32)]
    if want_bf16:
        out_specs.append(pl.BlockSpec((tm, D), row))
        out_shape.append(jax.ShapeDtypeStruct((s, D), BF16))
    out_specs.append(pl.BlockSpec((8, D), const))
    out_shape.append(jax.ShapeDtypeStruct((8, D), F32))
    side_in = [] if side is None else side.inputs
    return pl.pallas_call(
        body, grid=(s // tm,),
        in_specs=[pl.BlockSpec((tm, p.shape[1]), row) for p in a_parts]
        + [pl.BlockSpec((k, D), const, pipeline_mode=pl.Buffered(1)),
           pl.BlockSpec((tm, D), row), pl.BlockSpec((tm, 1), row), pl.BlockSpec((1, D), const),
           pl.BlockSpec((tm, D), row)] + [ANY_SPEC] * n_sin,
        out_specs=out_specs + [ANY_SPEC] * n_sout, out_shape=out_shape + ([] if side is None else side.out_shape),
        scratch_shapes=[] if side is None else side.scratch,
        name=name, compiler_params=_cp("arbitrary"))(*a_parts, b, xin, r, g, dres, *side_in)


def _gla_consts():
    lmask = _iota((4 * CHUNK, CHUNK), 0) % CHUNK >= _iota((4 * CHUNK, CHUNK), 1)
    hmask = _iota((256, 256), 0) // CHUNK == _iota((256, 256), 1) // CHUNK
    bd = _iota((256, 512), 0) // CHUNK == _iota((256, 512), 1) // 128
    return lmask, hmask, bd


def _fold_heads(x):
    return x[0:64] + x[64:128] + x[128:192] + x[192:256]


def _gla_decays(la, b_scr, dec_scr):
    tri = (_iota((CHUNK, CHUNK), 0) >= _iota((CHUNK, CHUNK), 1)).astype(BF16)
    ones = jnp.ones((CHUNK, 128), BF16)
    for c in range(la.shape[0] // CHUNK):
        la3 = _split3(la[CHUNK * c:CHUNK * (c + 1)])
        b_scr[CHUNK * c:CHUNK * (c + 1), :] = _sum3(_dot(tri, la3), 1)
        dec_scr[c] = jnp.exp(_sum3(_dot_tn(la3, ones), 0))


def _gla_chunk(b, qc, kc):
    bl = b[CHUNK - 1:CHUNK, :]
    ep, en, ek = jnp.exp(b), jnp.exp(-b), jnp.exp(bl - b)
    return bl, ep, en, ek, qc * ep, qc * en, kc * en, kc * ep, kc * ek


def _gla_fwd(pm, pe, wau_p, b_alpha):
    s = pm.shape[0]
    t = _row_tile(s)
    nc = t // CHUNK

    def body(q_ref, k_ref, v_ref, e_ref, wau_ref, ba_ref, o_ref, st_ref, state, b_scr, dec_scr):
        @pl.when(pl.program_id(0) == 0)
        def _():
            state[...] = jnp.zeros_like(state)

        z = _dot(e_ref[...].astype(BF16), wau_ref[...]) + ba_ref[...]
        _gla_decays(_log_sigmoid(z) * (1.0 / GLA_TAU), b_scr, dec_scr)
        lmask, hmask, bd = _gla_consts()

        def chunk(c, carry):
            rows = pl.ds(pl.multiple_of(c * CHUNK, CHUNK), CHUNK)
            qc = q_ref[rows, :].astype(F32) * 0.125
            kc = k_ref[rows, :].astype(F32)
            vc = v_ref[rows, :]
            _, _, _, _, qp, qn, kn, kp, kk = _gla_chunk(b_scr[rows, :], qc, kc)
            decb = jnp.concatenate([dec_scr[c]] * 4, axis=1)
            qs = jnp.where(hmask, jnp.concatenate([qp] * 4, axis=0), 0.0).astype(BF16)
            qns = jnp.where(hmask, jnp.concatenate([qn] * 4, axis=0), 0.0).astype(BF16)
            attn = jnp.where(lmask, _dot_nt(qs, kn.astype(BF16)), _dot_nt(qns, kp.astype(BF16))).astype(BF16)
            st = state[...]
            o_intra = _fold_heads(jnp.where(bd, _dot(attn, vc), 0.0))
            o_ref[rows, :] = o_intra + _dot(qp.astype(BF16), st.astype(BF16))
            for h in range(4):
                st_ref[c, :, 128 * h:128 * (h + 1)] = st[64 * h:64 * (h + 1), 128 * h:128 * (h + 1)]
            kv = jnp.where(bd, _dot_tn(kk.astype(BF16), vc), 0.0)
            state[...] = st * decb + kv
            return carry

        lax.fori_loop(0, nc, chunk, 0)

    return pl.pallas_call(
        body, grid=(s // t,),
        in_specs=[pl.BlockSpec((t, 256), lambda i: (i, C_GQ // 256)), pl.BlockSpec((t, 256), lambda i: (i, C_GK // 256)),
                  pl.BlockSpec((t, 512), lambda i: (i, C_GV // 512)), pl.BlockSpec((t, PE_W), lambda i: (i, 0)),
                  pl.BlockSpec((PE_W, 256), lambda i: (0, 0)), pl.BlockSpec((1, 256), lambda i: (0, 0))],
        out_specs=[pl.BlockSpec((t, 512), lambda i: (i, 0)), pl.BlockSpec((nc, CHUNK, 512), lambda i: (i, 0, 0))],
        out_shape=[jax.ShapeDtypeStruct((s, 512), F32), jax.ShapeDtypeStruct((s // CHUNK, CHUNK, 512), F32)],
        scratch_shapes=[pltpu.VMEM((256, 512), F32), pltpu.VMEM((t, 256), F32), pltpu.VMEM((nc, 256, 128), F32)],
        name="gla_fwd", compiler_params=_cp("arbitrary"))(pm, pm, pm, pe, wau_p, b_alpha)


def _gla_bwd(pm, pe, wau_p, wau_pt, b_alpha, do, states):
    s = pm.shape[0]
    t = _row_tile(s)
    nc = t // CHUNK
    nb = s // t

    def body(q_ref, k_ref, v_ref, e_ref, wau_ref, waut_ref, ba_ref, do_ref, st_ref,
             dq_ref, dk_ref, dv_ref, de_ref, dwau_ref, dba_ref, gstate, b_scr, db_scr, dec_scr):
        @pl.when(pl.program_id(0) == 0)
        def _():
            gstate[...] = jnp.zeros_like(gstate)
            dwau_ref[...] = jnp.zeros_like(dwau_ref)
            dba_ref[...] = jnp.zeros_like(dba_ref)

        eb = e_ref[...].astype(BF16)
        z = _dot(eb, wau_ref[...]) + ba_ref[...]
        _gla_decays(_log_sigmoid(z) * (1.0 / GLA_TAU), b_scr, dec_scr)
        lmask, hmask, bd = _gla_consts()
        last_row = _iota((CHUNK, 256), 0) == CHUNK - 1

        def chunk(cc, carry):
            c = nc - 1 - cc
            rows = pl.ds(pl.multiple_of(c * CHUNK, CHUNK), CHUNK)
            qc = q_ref[rows, :].astype(F32) * 0.125
            kc = k_ref[rows, :].astype(F32)
            vc = v_ref[rows, :]
            dob = do_ref[rows, :]
            bl, ep, en, ek, qp, qn, kn, kp, kk = _gla_chunk(b_scr[rows, :], qc, kc)
            decb = jnp.concatenate([dec_scr[c]] * 4, axis=1)
            qs = jnp.where(hmask, jnp.concatenate([qp] * 4, axis=0), 0.0).astype(BF16)
            qns = jnp.where(hmask, jnp.concatenate([qn] * 4, axis=0), 0.0).astype(BF16)
            knb, kpb = kn.astype(BF16), kp.astype(BF16)
            attn = jnp.where(lmask, _dot_nt(qs, knb), _dot_nt(qns, kpb)).astype(BF16)
            st = jnp.where(bd, jnp.concatenate([st_ref[c]] * 4, axis=0), 0.0)
            g = gstate[...]
            gb = g.astype(BF16)
            do_s = jnp.where(bd, jnp.concatenate([dob] * 4, axis=0), jnp.zeros((), BF16))
            dattn = _dot_nt(do_s, vc)
            dv_ref[rows, :] = (_dot_tn(attn, do_s) + _dot(kk.astype(BF16), gb)).astype(BF16)
            dac = jnp.where(lmask, dattn, 0.0).astype(BF16)
            daa = jnp.where(lmask, 0.0, dattn).astype(BF16)
            dqp = _fold_heads(jnp.where(hmask, _dot(dac, knb), 0.0)) + _dot_nt(dob, st.astype(BF16))
            dqn = _fold_heads(jnp.where(hmask, _dot(daa, kpb), 0.0))
            dkn = _dot_tn(dac, qs)
            dkp = _dot_tn(daa, qns)
            dkk = _dot_nt(vc, gb)
            ddec = _dot_nt(jnp.ones((8, 1536), BF16), _split3(g * st))[0:1, :]
            gstate[...] = decb * g + jnp.where(bd, _dot_tn(qp.astype(BF16), dob), 0.0)
            dq_ref[rows, :] = ((dqp * ep + dqn * en) * 0.125).astype(BF16)
            dk_ref[rows, :] = (dkn * en + dkp * ep + dkk * ek).astype(BF16)
            dek = dkk * kc * ek
            db = (dqp * qc + dkp * kc) * ep - (dqn * qc + dkn * kc) * en - dek
            dbl = jnp.sum(dek, axis=0, keepdims=True) + ddec * jnp.exp(bl)
            db_scr[rows, :] = db + jnp.where(last_row, dbl, 0.0)
            return carry

        lax.fori_loop(0, nc, chunk, 0)
        triu = (_iota((CHUNK, CHUNK), 0) <= _iota((CHUNK, CHUNK), 1)).astype(BF16)
        dla = jnp.concatenate([_sum3(_dot(triu, _split3(db_scr[CHUNK * c:CHUNK * (c + 1), :])), 1) for c in range(nc)], axis=0)
        dz = dla * (1.0 / GLA_TAU) * _sigmoid(-z)
        dzb = dz.astype(BF16)
        dwau_ref[...] += _dot_tn(eb, dzb)
        dba_ref[...] += _fold8(dz)
        de_ref[...] = _dot(dzb, waut_ref[...])

    rev = lambda i: nb - 1 - i
    return pl.pallas_call(
        body, grid=(nb,),
        in_specs=[pl.BlockSpec((t, 256), lambda i: (rev(i), C_GQ // 256)), pl.BlockSpec((t, 256), lambda i: (rev(i), C_GK // 256)),
                  pl.BlockSpec((t, 512), lambda i: (rev(i), C_GV // 512)), pl.BlockSpec((t, PE_W), lambda i: (rev(i), 0)),
                  pl.BlockSpec((PE_W, 256), lambda i: (0, 0)), pl.BlockSpec((256, PE_W), lambda i: (0, 0)),
                  pl.BlockSpec((1, 256), lambda i: (0, 0)), pl.BlockSpec((t, 512), lambda i: (rev(i), 0)),
                  pl.BlockSpec((nc, CHUNK, 512), lambda i: (rev(i), 0, 0))],
        out_specs=[pl.BlockSpec((t, 256), lambda i: (rev(i), 0)), pl.BlockSpec((t, 256), lambda i: (rev(i), 0)),
                   pl.BlockSpec((t, 512), lambda i: (rev(i), 0)), pl.BlockSpec((t, PE_W), lambda i: (rev(i), 0)),
                   pl.BlockSpec((PE_W, 256), lambda i: (0, 0)), pl.BlockSpec((8, 256), lambda i: (0, 0))],
        out_shape=[jax.ShapeDtypeStruct((s, 256), BF16), jax.ShapeDtypeStruct((s, 256), BF16),
                   jax.ShapeDtypeStruct((s, 512), BF16), jax.ShapeDtypeStruct((s, PE_W), F32),
                   jax.ShapeDtypeStruct((PE_W, 256), F32), jax.ShapeDtypeStruct((8, 256), F32)],
        scratch_shapes=[pltpu.VMEM((256, 512), F32), pltpu.VMEM((t, 256), F32), pltpu.VMEM((t, 256), F32),
                        pltpu.VMEM((nc, 256, 128), F32)],
        name="gla_bwd", compiler_params=_cp("arbitrary"))(pm, pm, pm, pe, wau_p, wau_pt, b_alpha, do, states)


def _fcum_fwd(pe, bias):
    s = pe.shape[0]
    t = min(s, 256)

    def body(e_ref, b_ref, f_ref, carry):
        @pl.when(pl.program_id(0) == 0)
        def _():
            carry[...] = jnp.zeros_like(carry)

        lf = _log_sigmoid(e_ref[...] + b_ref[...])
        tri = (_iota((t, t), 0) >= _iota((t, t), 1)).astype(BF16)
        f = _sum3(_dot(tri, _split3(lf)), 1) + carry[0:1, :]
        f_ref[...] = f
        carry[...] = jnp.broadcast_to(f[t - 1:t, :], carry.shape)

    return pl.pallas_call(
        body, grid=(s // t,),
        in_specs=[pl.BlockSpec((t, PE_W), lambda i: (i, 0)), pl.BlockSpec((1, PE_W), lambda i: (0, 0))],
        out_specs=pl.BlockSpec((t, PE_W), lambda i: (i, 0)),
        out_shape=jax.ShapeDtypeStruct((s, PE_W), F32), scratch_shapes=[pltpu.VMEM((8, PE_W), F32)],
        name="fcum_fwd", compiler_params=_cp("arbitrary"))(pe, bias)


def _fcum_bwd(pe, bias, df):
    s = pe.shape[0]
    t = min(s, 256)
    nb = s // t

    def body(e_ref, b_ref, df_ref, de_ref, db_ref, carry):
        @pl.when(pl.program_id(0) == 0)
        def _():
            carry[...] = jnp.zeros_like(carry)
            db_ref[...] = jnp.zeros_like(db_ref)

        triu = (_iota((t, t), 0) <= _iota((t, t), 1)).astype(BF16)
        dlf = _sum3(_dot(triu, _split3(df_ref[...])), 1) + carry[0:1, :]
        carry[...] = jnp.broadcast_to(dlf[0:1, :], carry.shape)
        lane = _iota((t, PE_W), 1)
        dff = jnp.where((lane >= FF_LANE) & (lane < FF_LANE + 8), dlf * _sigmoid(-(e_ref[...] + b_ref[...])), 0.0)
        de_ref[...] = dff
        db_ref[...] += _fold8(dff)

    rev = lambda i: (nb - 1 - i, 0)
    return pl.pallas_call(
        body, grid=(nb,),
        in_specs=[pl.BlockSpec((t, PE_W), rev), pl.BlockSpec((1, PE_W), lambda i: (0, 0)), pl.BlockSpec((t, PE_W), rev)],
        out_specs=[pl.BlockSpec((t, PE_W), rev), pl.BlockSpec((8, PE_W), lambda i: (0, 0))],
        out_shape=[jax.ShapeDtypeStruct((s, PE_W), F32), jax.ShapeDtypeStruct((8, PE_W), F32)],
        scratch_shapes=[pltpu.VMEM((8, PE_W), F32)],
        name="fcum_bwd", compiler_params=_cp("arbitrary"))(pe, bias, df)


FOX_WIDE = 1024


def _split3(x):
    hi = x.astype(BF16)
    r = x - hi.astype(F32)
    mid = r.astype(BF16)
    lo = (r - mid.astype(F32)).astype(BF16)
    return jnp.concatenate([hi, mid, lo], axis=1)


def _sum3(x, axis):
    n = x.shape[axis] // 3
    parts = [lax.slice_in_dim(x, n * p, n * (p + 1), axis=axis) for p in range(3)]
    return (parts[0] + parts[1]) + parts[2]


def _fox_tables():
    heads, lane = np.arange(8), np.arange(64)
    spread = np.zeros((512, 1024), np.float32)
    spread[(64 * heads[:, None] + lane).ravel(), (128 * heads[:, None] + lane).ravel()] = 1.0
    def place(src_lane0, dst_off, val):
        t = np.zeros((384, 1024), np.float32)
        for p in range(3):
            t[128 * p + src_lane0 + heads, 128 * heads + dst_off + p] = val
        return t
    def const(off, val):
        c = np.zeros((1, 1024), np.float32)
        for p in range(3):
            c[0, 128 * heads + off + p] = val
        return c
    rows = np.zeros((8, 128), np.float32)
    rows[heads, FF_LANE + heads] = 1.0
    bf = lambda a: jnp.asarray(a, BF16)
    return dict(spread=bf(spread),
                f_to_q=bf(place(FF_LANE, 64, 1.0)), f_to_k=bf(place(FF_LANE, 67, -1.0)), d_to_do=bf(place(0, 64, 1.0)),
                ones_q=jnp.asarray(const(67, 1.0)), ones_k=jnp.asarray(const(64, 1.0)), ones_v=jnp.asarray(const(64, -1.0)),
                rows=jnp.asarray(rows))


LOG2E = 1.4426950408889634


def _fox_prep(pm, f128, lse8, tb, *, backward):
    s = pm.shape[0]
    tm = _row_tile(s)

    def body(*refs):
        if backward:
            q_ref, f_ref, lse_ref, sp_ref, fq_ref, cq_ref, rows_ref, qa_ref = refs
            f = f_ref[...] * LOG2E - _dot_tn(lse_ref[...], rows_ref[...], precision=HI)
            q2 = (q_ref[...].astype(F32) * (0.125 * LOG2E)).astype(BF16)
            qa_ref[...] = (_dot(q2, sp_ref[...]) + _dot(_split3(f), fq_ref[...]) + cq_ref[...]).astype(BF16)
            return
        (q_ref, k_ref, v_ref, f_ref, sp_ref, fq_ref, fk_ref, cq_ref, ck_ref, cv_ref,
         qa_ref, ka_ref, va_ref, vt_ref, qt_ref, kt_ref) = refs
        f3 = _split3(f_ref[...] * LOG2E)
        q, k, v = q_ref[...].astype(F32), k_ref[...], v_ref[...]
        sp = sp_ref[...]
        qa_ref[...] = (_dot((q * (0.125 * LOG2E)).astype(BF16), sp) + _dot(f3, fq_ref[...]) + cq_ref[...]).astype(BF16)
        ka_ref[...] = (_dot(k, sp) + _dot(f3, fk_ref[...]) + ck_ref[...]).astype(BF16)
        va_ref[...] = (_dot(v, sp) + cv_ref[...]).astype(BF16)
        vt_ref[...] = v.T
        qt_ref[...] = (q * 0.125).astype(BF16).T
        kt_ref[...] = (k.astype(F32) * 0.125).astype(BF16).T

    row = lambda i: (i, 0)
    const = lambda i: (0, 0)
    blk = lambda c: pl.BlockSpec((tm, 512), lambda i: (i, c // 512))
    wide = pl.BlockSpec((tm, 1024), row)
    mat = lambda a: pl.BlockSpec(a.shape, const)
    if backward:
        ins = [pm, f128, lse8, tb["spread"], tb["f_to_q"], tb["ones_q"], tb["rows"]]
        in_specs = [blk(C_FQ), pl.BlockSpec((tm, 128), row), pl.BlockSpec((8, tm), lambda i: (0, i))] + [mat(a) for a in ins[3:]]
        out_specs, out_shape = wide, jax.ShapeDtypeStruct((s, 1024), BF16)
    else:
        ins = [pm, pm, pm, f128, tb["spread"], tb["f_to_q"], tb["f_to_k"], tb["ones_q"], tb["ones_k"], tb["ones_v"]]
        in_specs = [blk(C_FQ), blk(C_FK), blk(C_FV), pl.BlockSpec((tm, 128), row)] + [mat(a) for a in ins[4:]]
        tr = pl.BlockSpec((512, tm), lambda i: (0, i))
        out_specs = [wide, wide, wide, tr, tr, tr]
        out_shape = [jax.ShapeDtypeStruct((s, 1024), BF16)] * 3 + [jax.ShapeDtypeStruct((512, s), BF16)] * 3
    return pl.pallas_call(body, grid=(s // tm,), in_specs=in_specs, out_specs=out_specs, out_shape=out_shape,
                          name="fox_prep_bwd" if backward else "fox_prep", compiler_params=_cp("parallel"))(*ins)


def _fox_post(dqt, dkt, dvt, rowsum8, colsum8, tb):
    s = dqt.shape[1]
    tm = _row_tile(s)

    def body(dqt_ref, dkt_ref, dvt_ref, rs_ref, cs_ref, rows_ref, dfq_ref, dfk_ref, dfv_ref, df_ref):
        dfq_ref[...] = dqt_ref[...].T.astype(BF16)
        dfk_ref[...] = dkt_ref[...].T
        dfv_ref[...] = dvt_ref[...].T
        df_ref[...] = _dot_tn(rs_ref[...] - cs_ref[...], rows_ref[...], precision=HI)

    row = lambda i: (i, 0)
    tr = pl.BlockSpec((512, tm), lambda i: (0, i))
    out = pl.BlockSpec((tm, 512), row)
    heads = pl.BlockSpec((8, tm), lambda i: (0, i))
    return pl.pallas_call(
        body, grid=(s // tm,),
        in_specs=[tr, tr, tr, heads, heads, pl.BlockSpec((8, 128), lambda i: (0, 0))],
        out_specs=[out, out, out, pl.BlockSpec((tm, 128), row)],
        out_shape=[jax.ShapeDtypeStruct((s, 512), BF16)] * 3 + [jax.ShapeDtypeStruct((s, 128), F32)],
        name="fox_post", compiler_params=_cp("parallel"))(dqt, dkt, dvt, rowsum8, colsum8, tb["rows"])


def _fox_fwd(k_aug, q_aug, vt):
    s = k_aug.shape[0]
    nh = 8
    tk = _row_tile(s)
    tq = min(s, 2 * FOX_WIDE)
    per = tq // tk

    def body(k_ref, q_ref, v_ref, o_ref, lse_ref, sbuf):
        i = pl.program_id(1)
        qa = q_ref[...]

        def scores(j):
            return _dot_nt(k_ref[pl.ds(pl.multiple_of(j * tk, tk), tk), :], qa)

        ones_row = (_iota((16, tk), 0) == 0).astype(BF16)

        def update(st, j, carry):
            m, acc = carry
            m2 = jnp.maximum(m, jnp.max(st, axis=0, keepdims=True))
            p = jnp.exp2(st - m2)
            vj = jnp.concatenate([v_ref[:, pl.ds(pl.multiple_of(j * tk, tk), tk)], ones_row], axis=0)
            return m2, jnp.exp2(m - m2) * acc + _dot(vj, p.astype(BF16))

        def step(a, carry):
            sbuf[1] = scores(2 * a + 1)
            carry = update(sbuf[0], 2 * a, carry)
            sbuf[0] = scores(2 * a + 2)
            return update(sbuf[1], 2 * a + 1, carry)

        n = i * per
        sbuf[0] = scores(0)
        carry = (jnp.full((1, tq), -1e30, F32), jnp.zeros((80, tq), F32))
        carry = lax.fori_loop(0, n // 2, step, carry)
        tri = _iota((tk, tk), 0) <= _iota((tk, tk), 1)
        late = [_dot_nt(k_ref[pl.ds(pl.multiple_of((n + r) * tk, tk), tk), :], qa[r * tk:, :]) for r in range(1, per)]
        for r in range(per):
            st = sbuf[0] if r == 0 else late[r - 1]
            head = jnp.where(tri, st[:, :tk], -1e30)
            st = head if st.shape[1] == tk else jnp.concatenate([head, st[:, tk:]], axis=1)
            part = update(st, n + r, tuple(c[:, r * tk:] for c in carry))
            carry = part if r == 0 else tuple(jnp.concatenate([old[:, :r * tk], new], axis=1) for old, new in zip(carry, part))
        m, acc = carry
        l = acc[64:65]
        o_ref[...] = (acc[0:64] / l).astype(BF16)
        lse_ref[0] = m + jnp.log2(l)

    return pl.pallas_call(
        body, grid=(nh, s // tq),
        in_specs=[pl.BlockSpec((s, 128), lambda h, i: (0, h)), pl.BlockSpec((tq, 128), lambda h, i: (i, h)),
                  pl.BlockSpec((64, s), lambda h, i: (h, 0))],
        out_specs=[pl.BlockSpec((64, tq), lambda h, i: (h, i)), pl.BlockSpec((1, 1, tq), lambda h, i: (h, 0, i))],
        out_shape=[jax.ShapeDtypeStruct((512, s), BF16), jax.ShapeDtypeStruct((nh, 1, s), F32)],
        scratch_shapes=[pltpu.VMEM((2, tk, tq), F32)],
        name="fox_fwd", compiler_params=_cp("parallel", "arbitrary"))(k_aug, q_aug, vt)


def _fox_bwd(q_aug, do_aug, qt, dot_, k_aug, v_aug, kt):
    s = q_aug.shape[0]
    nh = 8
    tq = _row_tile(s)
    tk = min(s, 2 * FOX_WIDE)
    per = tk // tq
    nqb = s // tq

    def body(qa_ref, da_ref, qt_ref, dt_ref, ka_ref, va_ref, kt_ref, dq_ref, rs_ref, dk_ref, dv_ref, dfk_ref):
        j = pl.program_id(1)

        @pl.when(j == 0)
        def _():
            dq_ref[...] = jnp.zeros_like(dq_ref)
            rs_ref[...] = jnp.zeros_like(rs_ref)

        ones_row = (_iota((16, tk), 0) == 0).astype(BF16)
        ka, va = ka_ref[...], va_ref[...]
        ks = jnp.concatenate([kt_ref[...], ones_row], axis=0)
        tri = _iota((tq, tq), 0) >= _iota((tq, tq), 1)

        def tile(i, w, carry):
            masked = w is not None
            w = tk if w is None else w
            rows = pl.ds(pl.multiple_of(i * tq, tq), tq)
            sp = _dot_nt(qa_ref[rows, :], ka[:w])
            if masked:
                last = jnp.where(tri, sp[:, w - tq:], -1e30)
                sp = last if w == tq else jnp.concatenate([sp[:, :w - tq], last], axis=1)
            p = jnp.exp2(sp)
            dsb = (p * _dot_nt(da_ref[rows, :], va[:w])).astype(BF16)
            dq = _dot_nt(ks[:, :w], dsb)
            dq_ref[:, rows] += dq[0:64]
            rs_ref[0, :, rows] += dq[64:72]
            new = (_dot(jnp.concatenate([qt_ref[:, rows], ones_row[:, :tq]], axis=0), dsb), _dot(dt_ref[:, rows], p.astype(BF16)))
            if w == tk:
                return tuple(c + d for c, d in zip(carry, new))
            return tuple(jnp.concatenate([c[:, :w] + d, c[:, w:]], axis=1) for c, d in zip(carry, new))

        carry = (jnp.zeros((80, tk), F32), jnp.zeros((64, tk), F32))
        for r in range(per):
            carry = tile(j * per + r, (r + 1) * tq, carry)
        dk, dv = lax.fori_loop((j + 1) * per, nqb, lambda i, c: tile(i, None, c), carry)
        dk_ref[...] = dk[0:64].astype(BF16)
        dv_ref[...] = dv.astype(BF16)
        dfk_ref[0] = dk[64:65]

    head_cols = lambda h, j: (0, h)
    head_rows = lambda h, j: (h, 0)
    once = dict(pipeline_mode=pl.Buffered(1))
    return pl.pallas_call(
        body, grid=(nh, s // tk),
        in_specs=[pl.BlockSpec((s, 128), head_cols, **once), pl.BlockSpec((s, 128), head_cols, **once),
                  pl.BlockSpec((64, s), head_rows, **once), pl.BlockSpec((64, s), head_rows, **once),
                  pl.BlockSpec((tk, 128), lambda h, j: (j, h)), pl.BlockSpec((tk, 128), lambda h, j: (j, h)),
                  pl.BlockSpec((64, tk), lambda h, j: (h, j))],
        out_specs=[pl.BlockSpec((64, s), head_rows), pl.BlockSpec((1, 8, s), lambda h, j: (h, 0, 0)),
                   pl.BlockSpec((64, tk), lambda h, j: (h, j)),
                   pl.BlockSpec((64, tk), lambda h, j: (h, j)), pl.BlockSpec((1, 1, tk), lambda h, j: (h, 0, j))],
        out_shape=[jax.ShapeDtypeStruct((512, s), F32), jax.ShapeDtypeStruct((nh, 8, s), F32),
                   jax.ShapeDtypeStruct((512, s), BF16),
                   jax.ShapeDtypeStruct((512, s), BF16), jax.ShapeDtypeStruct((nh, 1, s), F32)],
        name="fox_bwd", compiler_params=_cp("parallel", "arbitrary"))(q_aug, do_aug, qt, dot_, k_aug, v_aug, kt)


MEM_SCALE = 128 ** -0.5


def _mem_attn_fwd(pm, mkv):
    s = pm.shape[0]
    t = _row_tile(s)
    nm = mkv.shape[0]

    def body(q_ref, mk_ref, mv_ref, o_ref):
        for h in range(4):
            cols = slice(128 * h, 128 * (h + 1))
            sc = _dot_nt(q_ref[:, cols], mk_ref[:, cols]) * MEM_SCALE
            p = jnp.exp(sc - jnp.max(sc, axis=-1, keepdims=True))
            p = p / jnp.sum(p, axis=-1, keepdims=True)
            o_ref[:, cols] = _dot(p.astype(BF16), mv_ref[:, cols]).astype(BF16)

    return pl.pallas_call(
        body, grid=(s // t,),
        in_specs=[pl.BlockSpec((t, 512), lambda i: (i, C_MQ // 512)), pl.BlockSpec((nm, 512), lambda i: (0, 0)),
                  pl.BlockSpec((nm, 512), lambda i: (0, 1))],
        out_specs=pl.BlockSpec((t, 512), lambda i: (i, 0)),
        out_shape=jax.ShapeDtypeStruct((s, 512), BF16),
        name="mem_attn_fwd", compiler_params=_cp("parallel"))(pm, mkv, mkv)


def _mem_attn_bwd(pm, mkv, do):
    s = pm.shape[0]
    t = _row_tile(s)
    nm = mkv.shape[0]

    def body(q_ref, mk_ref, mv_ref, do_ref, dq_ref, dmk_ref, dmv_ref):
        @pl.when(pl.program_id(0) == 0)
        def _():
            dmk_ref[...] = jnp.zeros_like(dmk_ref)
            dmv_ref[...] = jnp.zeros_like(dmv_ref)

        for h in range(4):
            cols = slice(128 * h, 128 * (h + 1))
            qh, kh, vh, doh = q_ref[:, cols], mk_ref[:, cols], mv_ref[:, cols], do_ref[:, cols]
            sc = _dot_nt(qh, kh) * MEM_SCALE
            p = jnp.exp(sc - jnp.max(sc, axis=-1, keepdims=True))
            p = p / jnp.sum(p, axis=-1, keepdims=True)
            pb = p.astype(BF16)
            dp = _dot_nt(doh, vh)
            ds = (p * (dp - jnp.sum(p * dp, axis=-1, keepdims=True)) * MEM_SCALE).astype(BF16)
            dq_ref[:, cols] = _dot(ds, kh).astype(BF16)
            dmk_ref[:, cols] += _dot_tn(ds, qh)
            dmv_ref[:, cols] += _dot_tn(pb, doh)

    return pl.pallas_call(
        body, grid=(s // t,),
        in_specs=[pl.BlockSpec((t, 512), lambda i: (i, C_MQ // 512)), pl.BlockSpec((nm, 512), lambda i: (0, 0)),
                  pl.BlockSpec((nm, 512), lambda i: (0, 1)), pl.BlockSpec((t, 512), lambda i: (i, 0))],
        out_specs=[pl.BlockSpec((t, 512), lambda i: (i, 0)), pl.BlockSpec((nm, 512), lambda i: (0, 0)),
                   pl.BlockSpec((nm, 512), lambda i: (0, 0))],
        out_shape=[jax.ShapeDtypeStruct((s, 512), BF16), jax.ShapeDtypeStruct((nm, 512), F32),
                   jax.ShapeDtypeStruct((nm, 512), F32)],
        name="mem_attn_bwd", compiler_params=_cp("arbitrary"))(pm, mkv, mkv, do)


def _gain_grad(dxn_g, x, r, name):
    m, d = x.shape

    def body(d_ref, x_ref, r_ref, o_ref):
        o_ref[...] = _fold8(d_ref[...] * (x_ref[...] * r_ref[...]))

    return pl.pallas_call(body, out_shape=jax.ShapeDtypeStruct((8, d), F32), name=name,
                          compiler_params=pltpu.CompilerParams(vmem_limit_bytes=VMEM_LIMIT_BYTES))(dxn_g, x, r)


def _head_norm(o, gh):
    xs, rs = [], []
    for h in range(4):
        oh = o[:, 128 * h:128 * (h + 1)]
        r = lax.rsqrt(jnp.mean(oh * oh, axis=-1, keepdims=True) + EPS)
        xs.append(oh * r)
        rs.append(r)
    return xs, rs


def _merge_fwd(x, pm, o_gla, o_fox_t, o_mem, g_head, wg, wf, wm, wo, g_ffn):
    s = x.shape[0]
    t = min(s, 256)

    def body(x_ref, g0_ref, g1_ref, g2_ref, gg_ref, og_ref, of_ref, om_ref, gh_ref, wg_ref, wf_ref, wm_ref, wo_ref, gf_ref,
             mg_ref, h1_ref, u2_ref, r2_ref):
        xs, _ = _head_norm(og_ref[...], None)
        gg = gg_ref[...].astype(F32)
        sil = gg * _sigmoid(gg)
        ogn = jnp.concatenate(xs, axis=1) * gh_ref[...] * sil
        merged = (_sigmoid(g0_ref[...].astype(F32)) * _dot(ogn.astype(BF16), wg_ref[...])
                  + _sigmoid(g1_ref[...].astype(F32)) * _dot(of_ref[...].T, wf_ref[...])
                  + _sigmoid(g2_ref[...].astype(F32)) * _dot(om_ref[...], wm_ref[...]))
        mb = merged.astype(BF16)
        mg_ref[...] = mb
        h1 = x_ref[...] + _dot(mb, wo_ref[...])
        h1_ref[...] = h1
        r = lax.rsqrt(jnp.mean(h1 * h1, axis=-1, keepdims=True) + EPS)
        u2_ref[...] = ((h1 * r) * gf_ref[...]).astype(BF16)
        r2_ref[...] = r

    row = lambda i: (i, 0)
    const = lambda i: (0, 0)
    return pl.pallas_call(
        body, grid=(s // t,),
        in_specs=[pl.BlockSpec((t, D), row), pl.BlockSpec((t, D), lambda i: (i, 0)), pl.BlockSpec((t, D), lambda i: (i, 1)),
                  pl.BlockSpec((t, D), lambda i: (i, 2)), pl.BlockSpec((t, 512), lambda i: (i, C_GG // 512)),
                  pl.BlockSpec((t, 512), row), pl.BlockSpec((512, t), lambda i: (0, i)), pl.BlockSpec((t, 512), row),
                  pl.BlockSpec((1, 512), const), pl.BlockSpec((512, D), const), pl.BlockSpec((512, D), const),
                  pl.BlockSpec((512, D), const), pl.BlockSpec((D, D), const), pl.BlockSpec((1, D), const)],
        out_specs=[pl.BlockSpec((t, D), row), pl.BlockSpec((t, D), row), pl.BlockSpec((t, D), row), pl.BlockSpec((t, 1), row)],
        out_shape=[jax.ShapeDtypeStruct((s, D), BF16), jax.ShapeDtypeStruct((s, D), F32),
                   jax.ShapeDtypeStruct((s, D), BF16), jax.ShapeDtypeStruct((s, 1), F32)],
        name="merge_fwd", compiler_params=_cp("parallel"))(x, pm, pm, pm, pm, o_gla, o_fox_t, o_mem, g_head, wg, wf, wm, wo, g_ffn)


def _merge_bwd(dh1b, pm, o_gla, o_fox_t, o_mem, g_head, wg, wf, wm, wgt, wft, wmt, wot, spread, d_to_do):
    s = dh1b.shape[0]
    t = min(s, 256)

    def body(dh_ref, g0_ref, g1_ref, g2_ref, gg_ref, og_ref, of_ref, om_ref, gh_ref, wg_ref, wf_ref, wm_ref,
             wgt_ref, wft_ref, wmt_ref, wot_ref, sp_ref, dd_ref,
             dgt_ref, dgg_ref, dog_ref, da_ref, dot_ref, dom_ref, dwg_ref, dwf_ref, dwm_ref, dgh_ref):
        @pl.when(pl.program_id(0) == 0)
        def _():
            dwg_ref[...] = jnp.zeros_like(dwg_ref)
            dwf_ref[...] = jnp.zeros_like(dwf_ref)
            dwm_ref[...] = jnp.zeros_like(dwm_ref)
            dgh_ref[...] = jnp.zeros_like(dgh_ref)

        dmerged = _dot(dh_ref[...], wot_ref[...])
        og = og_ref[...]
        xs, rs = _head_norm(og, None)
        on = jnp.concatenate(xs, axis=1)
        gg = gg_ref[...].astype(F32)
        sg = _sigmoid(gg)
        sil = gg * sg
        gh = gh_ref[...]
        ognb = (on * gh * sil).astype(BF16)
        ofb, omb = of_ref[...].T, om_ref[...]
        douts = []
        for idx, (gref, ob, w_ref, wt_ref, dw_ref) in enumerate((
                (g0_ref, ognb, wg_ref, wgt_ref, dwg_ref), (g1_ref, ofb, wf_ref, wft_ref, dwf_ref),
                (g2_ref, omb, wm_ref, wmt_ref, dwm_ref))):
            gt = _sigmoid(gref[...].astype(F32))
            y = _dot(ob, w_ref[...])
            dgt_ref[:, D * idx:D * (idx + 1)] = (dmerged * y * gt * (1.0 - gt)).astype(BF16)
            dy = (gt * dmerged).astype(BF16)
            dw_ref[...] += _dot_tn(ob, dy)
            douts.append(_dot(dy, wt_ref[...]))
        dogn, dof, dom = douts
        dofb = dof.astype(BF16)
        dom_ref[...] = dom.astype(BF16)
        ind = (_iota((1536, 128), 0) % 512 // 64 == _iota((1536, 128), 1)).astype(BF16)
        delta = _dot(_split3(dofb.astype(F32) * ofb.astype(F32)), ind)
        da_ref[...] = (_dot(dofb, sp_ref[...]) + _dot(_split3(delta), dd_ref[...])).astype(BF16)
        dot_ref[...] = dofb.T
        dgg_ref[...] = (dogn * on * gh * (sg * (1.0 + gg * (1.0 - sg)))).astype(BF16)
        d_on = dogn * sil
        dgh_ref[...] += _fold8(d_on * on)
        dxn = d_on * gh
        outs = []
        for h in range(4):
            cols = slice(128 * h, 128 * (h + 1))
            dh_, xh = dxn[:, cols], xs[h]
            outs.append(rs[h] * (dh_ - xh * jnp.mean(dh_ * xh, axis=-1, keepdims=True)))
        dog_ref[...] = jnp.concatenate(outs, axis=1).astype(BF16)

    row = lambda i: (i, 0)
    const = lambda i: (0, 0)
    return pl.pallas_call(
        body, grid=(s // t,),
        in_specs=[pl.BlockSpec((t, D), row), pl.BlockSpec((t, D), lambda i: (i, 0)), pl.BlockSpec((t, D), lambda i: (i, 1)),
                  pl.BlockSpec((t, D), lambda i: (i, 2)), pl.BlockSpec((t, 512), lambda i: (i, C_GG // 512)),
                  pl.BlockSpec((t, 512), row), pl.BlockSpec((512, t), lambda i: (0, i)), pl.BlockSpec((t, 512), row),
                  pl.BlockSpec((1, 512), const), pl.BlockSpec((512, D), const), pl.BlockSpec((512, D), const),
                  pl.BlockSpec((512, D), const), pl.BlockSpec((D, 512), const), pl.BlockSpec((D, 512), const),
                  pl.BlockSpec((D, 512), const), pl.BlockSpec((D, D), const),
                  pl.BlockSpec((512, 1024), const), pl.BlockSpec((384, 1024), const)],
        out_specs=[pl.BlockSpec((t, 3 * D), row), pl.BlockSpec((t, 512), row), pl.BlockSpec((t, 512), row),
                   pl.BlockSpec((t, 1024), row), pl.BlockSpec((512, t), lambda i: (0, i)), pl.BlockSpec((t, 512), row),
                   pl.BlockSpec((512, D), const), pl.BlockSpec((512, D), const), pl.BlockSpec((512, D), const),
                   pl.BlockSpec((8, 512), const)],
        out_shape=[jax.ShapeDtypeStruct((s, 3 * D), BF16), jax.ShapeDtypeStruct((s, 512), BF16),
                   jax.ShapeDtypeStruct((s, 512), BF16), jax.ShapeDtypeStruct((s, 1024), BF16),
                   jax.ShapeDtypeStruct((512, s), BF16), jax.ShapeDtypeStruct((s, 512), BF16),
                   jax.ShapeDtypeStruct((512, D), F32), jax.ShapeDtypeStruct((512, D), F32),
                   jax.ShapeDtypeStruct((512, D), F32), jax.ShapeDtypeStruct((8, 512), F32)],
        name="merge_bwd", compiler_params=_cp("arbitrary"))(
            dh1b, pm, pm, pm, pm, o_gla, o_fox_t, o_mem, g_head, wg, wf, wm, wgt, wft, wmt, wot, spread, d_to_do)


def _ff2_loss(a, w2, h1, g_final, target):
    s, k = a.shape
    tm = min(s, 256)

    def body(a_ref, w_ref, h1_ref, g_ref, t_ref, dh_ref, dhb_ref, loss_ref, dg_ref):
        @pl.when(pl.program_id(0) == 0)
        def _():
            loss_ref[...] = jnp.zeros_like(loss_ref)
            dg_ref[...] = jnp.zeros_like(dg_ref)

        h2 = h1_ref[...] + _dot(_relu2_bf16(a_ref[...]), w_ref[...])
        r = lax.rsqrt(jnp.mean(h2 * h2, axis=-1, keepdims=True) + EPS)
        xn = h2 * r
        g = g_ref[...]
        err = xn * g - t_ref[...]
        e2 = _fold8(err * err)
        part = e2[:, 0:128]
        for c in range(1, D // 128):
            part = part + e2[:, 128 * c:128 * (c + 1)]
        loss_ref[...] += part
        dy = err * (1.0 / D)
        dg_ref[...] += _fold8(dy * xn)
        dxn = dy * g
        dh = r * (dxn - xn * jnp.mean(dxn * xn, axis=-1, keepdims=True))
        dh_ref[...] = dh
        dhb_ref[...] = dh.astype(BF16)

    row = lambda i: (i, 0)
    const = lambda i: (0, 0)
    return pl.pallas_call(
        body, grid=(s // tm,),
        in_specs=[pl.BlockSpec((tm, k), row), pl.BlockSpec((k, D), const, pipeline_mode=pl.Buffered(1)),
                  pl.BlockSpec((tm, D), row), pl.BlockSpec((1, D), const), pl.BlockSpec((tm, D), row)],
        out_specs=[pl.BlockSpec((tm, D), row), pl.BlockSpec((tm, D), row), pl.BlockSpec((8, 128), const),
                   pl.BlockSpec((8, D), const)],
        out_shape=[jax.ShapeDtypeStruct((s, D), F32), jax.ShapeDtypeStruct((s, D), BF16),
                   jax.ShapeDtypeStruct((8, 128), F32), jax.ShapeDtypeStruct((8, D), F32)],
        name="ff2_loss", compiler_params=_cp("arbitrary"))(a, w2, h1, g_final, target)


def _adam(w, g, m, v, name):
    _, r, c = w.shape
    tr = r
    for cand in (512, 256, 128, 64, 32, 16, 8):
        if r % cand == 0 and cand * c * 4 <= (1 << 20):
            tr = cand
            break
    c1 = 1.0 - ADAM_B1 ** ADAM_STEP
    c2 = 1.0 - ADAM_B2 ** ADAM_STEP

    def body(w_ref, g_ref, m_ref, v_ref, d_ref, nm_ref, nv_ref):
        gv = g_ref[...]
        nm = ADAM_B1 * m_ref[...] + (1.0 - ADAM_B1) * gv
        nv = ADAM_B2 * v_ref[...] + (1.0 - ADAM_B2) * (gv * gv)
        d_ref[...] = -ADAM_LR * ((nm / c1) / (jnp.sqrt(nv / c2) + ADAM_EPS) + ADAM_WD * w_ref[...])
        nm_ref[...] = nm
        nv_ref[...] = nv

    spec = pl.BlockSpec((1, tr, c), lambda i: (0, i, 0))
    return pl.pallas_call(
        body, grid=(r // tr,), in_specs=[spec] * 4, out_specs=[spec] * 3,
        out_shape=[jax.ShapeDtypeStruct((1, r, c), F32)] * 3, name=name, compiler_params=_cp("parallel"))(w, g, m, v)


def _row_block(r):
    return max(d for d in range(16, 513, 16) if r % d == 0)


def _add_half(core, a, b, name):
    n, r, c = b.shape
    tr = _row_block(r)

    def body(core_ref, a_ref, b_ref, o_ref):
        o_ref[...] = (a_ref[...].astype(F32) + b_ref[...].astype(F32)).astype(BF16)

    spec = pl.BlockSpec((1, tr, c), lambda k, i, core_ref: (k, i, 0))
    half = pl.BlockSpec((1, tr, c), lambda k, i, core_ref: (k, i + core_ref[0] * (r // tr), 0))
    return pl.pallas_call(
        body, grid_spec=pltpu.PrefetchScalarGridSpec(num_scalar_prefetch=1, grid=(n, r // tr), in_specs=[half, spec],
                                                     out_specs=spec),
        out_shape=jax.ShapeDtypeStruct((n, r, c), BF16), name=name, compiler_params=_cp("parallel", "parallel"))(core, a, b)


def _sum4(a, name):
    _, r, c = a.shape
    tr = _row_block(r)

    def body(a_ref, o_ref):
        o_ref[...] = ((a_ref[0].astype(F32) + a_ref[1].astype(F32)) + a_ref[2].astype(F32)) + a_ref[3].astype(F32)

    return pl.pallas_call(body, grid=(r // tr,), in_specs=[pl.BlockSpec((4, tr, c), lambda i: (0, i, 0))],
                          out_specs=pl.BlockSpec((tr, c), lambda i: (i, 0)),
                          out_shape=jax.ShapeDtypeStruct((r, c), F32), name=name, compiler_params=_cp("parallel"))(a)


def _adam_small(w, gathered, m, v):
    c1 = 1.0 - ADAM_B1 ** ADAM_STEP
    c2 = 1.0 - ADAM_B2 ** ADAM_STEP

    def body(w_ref, g_ref, m_ref, v_ref, gs_ref, d_ref, nm_ref, nv_ref):
        gv = g_ref[0]
        for dev in range(1, N_DEV):
            gv = gv + g_ref[dev]
        gs_ref[...] = gv
        nm = ADAM_B1 * m_ref[...] + (1.0 - ADAM_B1) * gv
        nv = ADAM_B2 * v_ref[...] + (1.0 - ADAM_B2) * (gv * gv)
        d_ref[...] = -ADAM_LR * ((nm / c1) / (jnp.sqrt(nv / c2) + ADAM_EPS) + ADAM_WD * w_ref[...])
        nm_ref[...] = nm
        nv_ref[...] = nv

    return pl.pallas_call(body, out_shape=[jax.ShapeDtypeStruct((8, D), F32)] * 4, name="adam_small")(w, gathered, m, v)


def _place():
    return lax.axis_index("x"), lax.axis_index("y"), lax.axis_index("c")


def _other_chips(x, y):
    return [(1 - x, y), (x, 1 - y), (1 - x, 1 - y)]


GATHER_SEMS = [pltpu.SemaphoreType.DMA((6,)), pltpu.SemaphoreType.DMA((6,)), pltpu.SemaphoreType.DMA]


def _gather_ops(in_refs, out_refs, sems):
    (p_ref,), (out_ref,) = in_refs, out_refs
    send_sems, recv_sems, local_sem = sems
    hr = p_ref.shape[0] // 2
    x, y, cc = _place()
    sibling = (x, y, 1 - cc)
    chips = _other_chips(x, y)

    def half(chip, core):
        return out_ref.at[2 * chip[0] + chip[1], pl.ds(core * hr, hr), :]

    def copy(k, chip, core, to, src=None):
        return pltpu.make_async_remote_copy(
            src_ref=half(chip, core) if src is None else src, dst_ref=half(chip, core),
            send_sem=send_sems.at[k], recv_sem=recv_sems.at[k], device_id=to, device_id_type=MESH)

    mine = pltpu.make_async_copy(p_ref, out_ref.at[2 * x + y], local_sem)
    my_half = p_ref.at[pl.ds(cc * hr, hr), :]
    first = [copy(j, (x, y), cc, (*chip, cc), src=my_half) for j, chip in enumerate(chips)]
    passed = [copy(3 + j, chip, cc, sibling) for j, chip in enumerate(chips)]

    def start():
        mine.start()
        for cp in first:
            cp.start()

    def finish():
        for j, chip in enumerate(chips):
            copy(j, chip, cc, (x, y, cc)).wait_recv()
            passed[j].start()
        for j, chip in enumerate(chips):
            copy(3 + j, chip, 1 - cc, (x, y, cc)).wait_recv()
        for cp in first + passed:
            cp.wait_send()
        mine.wait()

    return start, finish


def _gather_side(p):
    return _Side([p], [jax.ShapeDtypeStruct((N_CHIPS,) + p.shape, p.dtype)], GATHER_SEMS, _gather_ops)


def _swap_halves(g):
    n, r, c = g.shape
    hr = r // 2

    def body(g_ref, out_ref, send_sem, recv_sem):
        x, y, cc = _place()
        cp = pltpu.make_async_remote_copy(
            src_ref=g_ref.at[:, pl.ds((1 - cc) * hr, hr), :], dst_ref=out_ref,
            send_sem=send_sem, recv_sem=recv_sem, device_id=(x, y, 1 - cc), device_id_type=MESH)
        cp.start()
        cp.wait()

    any_spec = pl.BlockSpec(memory_space=pl.ANY)
    return pl.pallas_call(
        body, out_shape=jax.ShapeDtypeStruct((n, hr, c), g.dtype), in_specs=[any_spec], out_specs=any_spec,
        scratch_shapes=[pltpu.SemaphoreType.DMA, pltpu.SemaphoreType.DMA], name="swap_halves")(g)


SCATTER_SEMS = [pltpu.SemaphoreType.DMA((7,)), pltpu.SemaphoreType.DMA((7,)), pltpu.SemaphoreType.DMA]


def _scatter_ops(in_refs, out_refs, sems):
    (p_ref,), (out_ref,) = in_refs, out_refs
    send_sems, recv_sems, local_sem = sems
    hr = p_ref.shape[1]
    x, y, cc = _place()
    me = 2 * x + y
    sibling = (x, y, 1 - cc)
    chips = _other_chips(x, y)
    ids = [2 * chip[0] + chip[1] for chip in chips]

    def land(src, core):
        return out_ref.at[src, pl.ds(core * hr, hr), :]

    def copy(k, src_ref, dst_ref, to):
        return pltpu.make_async_remote_copy(src_ref=src_ref, dst_ref=dst_ref, send_sem=send_sems.at[k],
                                            recv_sem=recv_sems.at[k], device_id=to, device_id_type=MESH)

    mine = pltpu.make_async_copy(p_ref.at[me], land(me, cc), local_sem)
    sends = [copy(j, p_ref.at[ids[j]], land(me, cc), (*chip, cc)) for j, chip in enumerate(chips)]
    sends.append(copy(3, p_ref.at[me], land(me, cc), sibling))
    passed = [copy(4 + j, land(ids[j], cc), land(ids[j], cc), sibling) for j in range(3)]

    def start():
        mine.start()
        for cp in sends:
            cp.start()

    def finish():
        for j in range(3):
            copy(j, p_ref.at[me], land(ids[j], cc), (x, y, cc)).wait_recv()
            passed[j].start()
        copy(3, p_ref.at[me], land(me, 1 - cc), (x, y, cc)).wait_recv()
        for j in range(3):
            copy(4 + j, p_ref.at[me], land(ids[j], 1 - cc), (x, y, cc)).wait_recv()
        for cp in sends + passed:
            cp.wait_send()
        mine.wait()

    return start, finish


def _scatter_side(p):
    n, hr, c = p.shape
    return _Side([p], [jax.ShapeDtypeStruct((n, 2 * hr, c), p.dtype)], SCATTER_SEMS, _scatter_ops)


def _gather_small(blk):
    m, n = blk.shape

    def body(x_ref, out_ref, send_sems, recv_sems, local_sem):
        x, y, cc = _place()
        me, sibling = (x, y, cc), (x, y, 1 - cc)
        chips = _other_chips(x, y)

        def slot(px, py, pc):
            return out_ref.at[4 * px + 2 * py + pc]

        def copy(k, block, to, src=None):
            return pltpu.make_async_remote_copy(
                src_ref=slot(*block) if src is None else src, dst_ref=slot(*block),
                send_sem=send_sems.at[k], recv_sem=recv_sems.at[k], device_id=to, device_id_type=MESH)

        mine = pltpu.make_async_copy(x_ref, slot(*me), local_sem)
        mine.start()
        first = [copy(0, me, sibling, src=x_ref)]
        first += [copy(1 + j, me, (*chip, cc), src=x_ref) for j, chip in enumerate(chips)]
        for cp in first:
            cp.start()
        passed = [copy(4 + j, (*chip, cc), sibling) for j, chip in enumerate(chips)]
        for j, chip in enumerate(chips):
            copy(1 + j, (*chip, cc), me).wait_recv()
            passed[j].start()
        copy(0, sibling, me).wait_recv()
        for j, chip in enumerate(chips):
            copy(4 + j, (*chip, 1 - cc), me).wait_recv()
        for cp in first + passed:
            cp.wait_send()
        mine.wait()

    vmem = pl.BlockSpec(memory_space=pltpu.VMEM)
    return pl.pallas_call(
        body, out_shape=jax.ShapeDtypeStruct((N_DEV, m, n), blk.dtype), in_specs=[vmem], out_specs=vmem,
        scratch_shapes=[pltpu.SemaphoreType.DMA((7,)), pltpu.SemaphoreType.DMA((7,)), pltpu.SemaphoreType.DMA],
        name="gather_small")(blk)


def _pack_a(sh, dtype):
    w = sh["w_in"].astype(dtype)
    return jnp.concatenate([w[:, 0:PACK_W], jnp.pad(w[:, PACK_W:], ((0, 0), (0, 2 * PACK_W - w.shape[1])))], axis=0)


def _pack_b(sh, dtype):
    o3 = jnp.concatenate([sh["w_gla_o"], sh["w_fox_o"], sh["w_mem_o"], jnp.zeros((512, 256), sh["w_gla_o"].dtype)], axis=1)
    au = jnp.pad(sh["w_alpha_up"], ((0, PACK_ROWS_B - 3072 - 16), (0, PACK_W - 64)))
    return jnp.concatenate([sh["w_ff1"], sh["w_ff2"], sh["w_mem_kv"], sh["w_out"], o3, au], axis=0).astype(dtype)


def _unpack_a(pa):
    return {"w_in": jnp.concatenate([pa[0:1024], pa[1024:2048, 0:1670 - PACK_W]], axis=1)}


def _unpack_b(pb):
    return {"w_ff1": pb[0:1024], "w_ff2": pb[1024:2048], "w_mem_kv": pb[2048:2304], "w_out": pb[2304:2560],
            "w_gla_o": pb[2560:3072, 0:256], "w_fox_o": pb[2560:3072, 256:512], "w_mem_o": pb[2560:3072, 512:768],
            "w_alpha_up": pb[3072:3088, 0:64]}


def _unpack(packed):
    return {**_unpack_a(packed[0:PACK_ROWS_A]), **_unpack_b(packed[PACK_ROWS_A:])}


def _split_shards(name, full):
    return jnp.split(full, N_CHIPS, axis=SHARD_AXIS[name])


def _pack_small(vals, scalar=None):
    row4 = jnp.concatenate([vals["b_alpha"].reshape(-1), vals["b_forget"].reshape(-1), jnp.zeros((D - 264,), F32)])
    row5 = jnp.concatenate([vals["g_gla_head"].reshape(-1), jnp.zeros((D - 512,), F32)])
    row6 = jnp.zeros((D,), F32) if scalar is None else jnp.broadcast_to(scalar, (D,))
    rows = [vals["g_mix"].reshape(-1), vals["g_mem"].reshape(-1), vals["g_ffn"].reshape(-1), vals["g_final"].reshape(-1),
            row4, row5, row6, jnp.zeros((D,), F32)]
    return jnp.stack(rows)


def _unpack_small(blk):
    return {"g_mix": blk[0].reshape(1, D), "g_mem": blk[1].reshape(1, D), "g_ffn": blk[2].reshape(1, D),
            "g_final": blk[3].reshape(D), "b_alpha": blk[4, 0:256].reshape(1, 256), "b_forget": blk[4, 256:264].reshape(1, 8),
            "g_gla_head": blk[5, 0:512].reshape(1, 4, 128)}


def _local_step(x, mem, target, wb, small, exchange=None):
    s = x.shape[0]
    nm = mem.shape[0]
    t = _row_tile(s)
    nb = s // t
    b_alpha = small["b_alpha"].reshape(1, 256)
    bias_e = jnp.concatenate([jnp.zeros((FF_LANE,), F32), small["b_forget"].reshape(-1),
                              jnp.zeros((PE_W - FF_LANE - 8,), F32)]).reshape(1, PE_W)
    g_mix, g_mem, g_ffn = small["g_mix"].reshape(1, D), small["g_mem"].reshape(1, D), small["g_ffn"].reshape(1, D)
    g_final = small["g_final"].reshape(1, D)
    g_head = small["g_gla_head"].reshape(1, 512)

    if exchange is None:
        u, r1 = _rms_fwd(x, g_mix, "norm_mix")
    else:
        u, r1, gathered = _rms_fwd(x, g_mix, "norm_mix", side=exchange.gather_a)
        wb = exchange.weights_a(gathered)
    w_in = wb["w_in"]
    w_main = jnp.concatenate([w_in[:, 3608:6680], w_in[:, 0:1536], w_in[:, 1552:3088], w_in[:, 3096:3608]], axis=1)
    w_e = jnp.concatenate([w_in[:, 1536:1552], w_in[:, 3088:3096], jnp.zeros((D, PE_W - 24), BF16)], axis=1)
    w_in_pt = _transpose(jnp.concatenate([w_main, w_e], axis=1), "t_w_in")
    big = min(s, 1024)
    if exchange is None:
        pm, pe = _proj(u, w_main, w_e)
    else:
        pm, pe, gathered = _proj(u, w_main, w_e, side=exchange.gather_b)
        wb = {**wb, **exchange.weights_b(gathered)}
    wau_p = jnp.concatenate([wb["w_alpha_up"], jnp.zeros((PE_W - 16, 256), BF16)], axis=0)
    o_gla, states = _gla_fwd(pm, pe, wau_p, b_alpha)
    fcum = _fcum_fwd(pe, bias_e)
    tb = _fox_tables()
    qf_aug, k_aug, v_aug, vt, qt, kt = _fox_prep(pm, fcum, None, tb, backward=False)
    o_fox, lse = _fox_fwd(k_aug, qf_aug, vt)
    mn, rm = _rms_fwd(mem, g_mem, "norm_mem")
    mkv = _mm_nn(mn, wb["w_mem_kv"], out_dtype=BF16, tm=nm, tn=512, tk=D, name="mem_kv")
    o_mem = _mem_attn_fwd(pm, mkv)
    merged, h1, u2, r2 = _merge_fwd(x, pm, o_gla, o_fox, o_mem, g_head, wb["w_gla_o"], wb["w_fox_o"], wb["w_mem_o"],
                                    wb["w_out"], g_ffn)
    a = _mm_nn(u2, wb["w_ff1"], out_dtype=BF16, tm=big, tn=1024, tk=D, name="ff1")
    dh2, dh2b, loss8, dgfin8 = _ff2_loss(a, wb["w_ff2"], h1, g_final, target)
    loss = 0.5 * jnp.sum(loss8) / D

    da = _mm_nn(dh2b, _transpose(wb["w_ff2"], "t_w_ff2"), out_dtype=BF16, tm=big, tn=1024, tk=D, name="d_act",
                epi=lambda acc, at: acc * (2.0 * jnp.maximum(at.astype(F32), 0.0)), extra=a)
    gw = {}
    gw["w_ff2"] = _mm_tn(a, dh2b, tm=1024, tn=D, ts=big, name="dw_ff2", a_fn=_relu2_bf16)
    gw["w_ff1"] = _mm_tn(u2, da, tm=D, tn=1024, ts=big, name="dw_ff1")
    dh1, dh1b, dgffn8 = _mm_norm_bwd([da], _transpose(wb["w_ff1"], "t_w_ff1"), h1, r2, g_ffn, dh2, name="d_h1", want_bf16=True)
    gw["w_out"] = _mm_tn(merged, dh1b, tm=D, tn=D, ts=big, name="dw_out")
    (dgates, dgg, do_gla, do_aug, do_t, do_mem, gw["w_gla_o"], gw["w_fox_o"], gw["w_mem_o"], dgh8) = _merge_bwd(
        dh1b, pm, o_gla, o_fox, o_mem, g_head, wb["w_gla_o"], wb["w_fox_o"], wb["w_mem_o"],
        *[_transpose(wb[n], "t_" + n) for n in ("w_gla_o", "w_fox_o", "w_mem_o", "w_out")], tb["spread"], tb["d_to_do"])
    dgq, dgk, dgv, de_gla, dwau_p, dba8 = _gla_bwd(pm, pe, wau_p, wau_p.T, b_alpha, do_gla, states)
    gw["w_alpha_up"] = dwau_p[0:16, :]
    q_aug = _fox_prep(pm, fcum, lse.reshape(8, s), tb, backward=True)
    dfq_t, dfrow, dfk_t, dfv_t, dfcol = _fox_bwd(q_aug, do_aug, qt, do_t, k_aug, v_aug, kt)
    dfq, dfk, dfv, df = _fox_post(dfq_t, dfk_t, dfv_t, dfrow[:, 0, :], dfcol.reshape(8, s), tb)
    de_fox, dbf8 = _fcum_bwd(pe, bias_e, df)
    dmq, dmk, dmv = _mem_attn_bwd(pm, mkv, do_mem)
    dmkv = jnp.concatenate([dmk, dmv], axis=1).astype(BF16)
    gw["w_mem_kv"] = _mm_tn(mn, dmkv, tm=D, tn=D, ts=nm, name="dw_mem_kv")
    dmn_g = _mm_nn(dmkv, _transpose(wb["w_mem_kv"], "t_w_mem_kv"), out_dtype=F32, tm=nm, tn=D, tk=D, name="d_mem_norm")
    dgmem8 = _gain_grad(dmn_g, mem, rm, "dg_mem")
    de = (de_gla + de_fox).astype(BF16)
    dproj = [dgates, dgq, dgk, dgv, dgg, dfq, dfk, dfv, dmq, de]
    dw_gates = _mm_tn(u, dgates, tm=D, tn=1024, ts=big, name="dw_in_gates")
    dw_g = _mm_tn_cat(u, [dgq, dgk, dgv], ts=big, name="dw_in_gla")
    dw_gf = _mm_tn_cat(u, [dgg, dfq], ts=big, name="dw_in_gg_fq")
    dw_f = _mm_tn_cat(u, [dfk, dfv], ts=big, name="dw_in_fk_fv")
    dw_m = _mm_tn_cat(u, [dmq, de], ts=big, name="dw_in_mq_narrow")
    gw["w_in"] = jnp.concatenate([dw_g, dw_gf[:, 0:512], dw_m[:, 512:528], dw_gf[:, 512:1024], dw_f,
                                  dw_m[:, 528:536], dw_m[:, 0:512], dw_gates], axis=1)
    if exchange is None:
        grad_x, dgmix8 = _mm_norm_bwd(dproj, w_in_pt, x, r1, g_mix, dh1, name="d_x", want_bf16=False)
        exchanged = None
    else:
        grad_x, dgmix8, exchanged = _mm_norm_bwd(dproj, w_in_pt, x, r1, g_mix, dh1, name="d_x", want_bf16=False,
                                                 side=exchange.scatter(gw))
    gs = {"g_mix": dgmix8.sum(0), "g_mem": dgmem8.sum(0), "g_ffn": dgffn8.sum(0), "g_final": dgfin8.sum(0),
          "b_alpha": dba8.sum(0), "b_forget": dbf8.sum(0)[FF_LANE:FF_LANE + 8], "g_gla_head": dgh8.sum(0)}
    return loss, grad_x, gw, gs, exchanged


def kernel(x, mem, g_mix, w_in, w_alpha_up, b_alpha, b_forget, g_gla_head, g_mem, w_mem_kv, w_gla_o, w_fox_o, w_mem_o, w_out, g_ffn, w_ff1, w_ff2, g_final, loss_target, m_g_mix, m_w_in, m_w_alpha_up, m_b_alpha, m_b_forget, m_g_gla_head, m_g_mem, m_w_mem_kv, m_w_gla_o, m_w_fox_o, m_w_mem_o, m_w_out, m_g_ffn, m_w_ff1, m_w_ff2, m_g_final, v_g_mix, v_w_in, v_w_alpha_up, v_b_alpha, v_b_forget, v_g_gla_head, v_g_mem, v_w_mem_kv, v_w_gla_o, v_w_fox_o, v_w_mem_o, v_w_out, v_g_ffn, v_w_ff1, v_w_ff2, v_g_final):
    args = dict(locals())
    w_sh = {n: args[n][0] for n in WEIGHTS}
    small = {n: args[n] for n in SMALL}

    def whole(parts):
        return {n: jnp.concatenate([p[n] for p in parts], axis=SHARD_AXIS[n]) for n in parts[0]}

    class Exchange:
        gather_a = _gather_side(_pack_a(w_sh, BF16))
        gather_b = _gather_side(_pack_b(w_sh, BF16))

        @staticmethod
        def weights_a(gathered):
            return whole([_unpack_a(gathered[k]) for k in range(N_CHIPS)])

        @staticmethod
        def weights_b(gathered):
            return whole([_unpack_b(gathered[k]) for k in range(N_CHIPS)])

        @staticmethod
        def scatter(gw):
            by_chip = {n: _split_shards(n, gw[n]) for n in WEIGHTS}
            packed = jnp.stack([jnp.concatenate([_pack_a({n: by_chip[n][k] for n in WEIGHTS}, BF16),
                                                 _pack_b({n: by_chip[n][k] for n in WEIGHTS}, BF16)], axis=0)
                                for k in range(N_CHIPS)])
            core = lax.axis_index("c").astype(jnp.int32).reshape(1)
            return _scatter_side(_add_half(core, packed, _swap_halves(packed), "chip_sum"))

    loss, grad_x, gw, gs, by_chip = _local_step(x[0], mem[0], loss_target[0], None, small, Exchange)
    g_out = {n: g[None] for n, g in _unpack(_sum4(by_chip, "shard_sum")).items()}
    d_out, m_out, v_out = {}, {}, {}
    for n in WEIGHTS:
        d_out[n], m_out[n], v_out[n] = _adam(args[n], g_out[n], args["m_" + n], args["v_" + n], "adam_" + n)

    small_all = _gather_small(_pack_small(gs, loss))
    sm = {n: args["m_" + n] for n in SMALL}
    sv = {n: args["v_" + n] for n in SMALL}
    gs_sum, sd, snm, snv = _adam_small(_pack_small(small), small_all, _pack_small(sm), _pack_small(sv))
    gs_o, sd_o, snm_o, snv_o = _unpack_small(gs_sum), _unpack_small(sd), _unpack_small(snm), _unpack_small(snv)

    names = ["g_mix", "w_in", "w_alpha_up", "b_alpha", "b_forget", "g_gla_head", "g_mem", "w_mem_kv", "w_gla_o", "w_fox_o",
             "w_mem_o", "w_out", "g_ffn", "w_ff1", "w_ff2", "g_final"]

    def pick(big, sml, n):
        return big[n] if n in big else sml[n]

    outs = [gs_sum[6, 0], grad_x[None]]
    for big, sml in ((g_out, gs_o), (d_out, sd_o), (m_out, snm_o), (v_out, snv_o)):
        outs += [pick(big, sml, n) for n in names]
    return tuple(outs)
```

```python
import functools

import numpy as np
import jax
import jax.numpy as jnp
from jax import lax
from jax.experimental import pallas as pl
from jax.experimental.pallas import tpu as pltpu

F32 = jnp.float32
BF16 = jnp.bfloat16
HI = lax.Precision.HIGHEST
MESH = pl.DeviceIdType.MESH

EPS = 1e-6
D = 1024
CHUNK = 64
GLA_TAU = 16.0
N_CHIPS = 4
N_DEV = 8
VMEM_LIMIT_BYTES = 56 * 1024 * 1024

ADAM_LR, ADAM_B1, ADAM_B2, ADAM_EPS, ADAM_WD, ADAM_STEP = 0.001, 0.9, 0.999, 1e-08, 0.01, 10

PM_W = 6656
PE_W = 128
C_GQ, C_GK, C_GV, C_GG, C_FQ, C_FK, C_FV, C_MQ = 3072, 3328, 3584, 4096, 4608, 5120, 5632, 6144
FF_LANE = 16

WEIGHTS = ("w_in", "w_alpha_up", "w_mem_kv", "w_gla_o", "w_fox_o", "w_mem_o", "w_out", "w_ff1", "w_ff2")
SHARD_AXIS = {"w_in": 1, "w_alpha_up": 1, "w_mem_kv": 0, "w_gla_o": 1, "w_fox_o": 1, "w_mem_o": 1, "w_out": 0,
              "w_ff1": 1, "w_ff2": 0}
SMALL = ("g_mix", "g_mem", "g_ffn", "g_final", "b_alpha", "b_forget", "g_gla_head")
PACK_W = 1024
PACK_ROWS_A = 2048
PACK_ROWS_B = 3104
PACK_ROWS = PACK_ROWS_A + PACK_ROWS_B


def _cp(*sem):
    return pltpu.CompilerParams(dimension_semantics=sem, vmem_limit_bytes=VMEM_LIMIT_BYTES)


def _dot(a, b, **kw):
    return jnp.dot(a, b, preferred_element_type=F32, **kw)


def _dot_nt(a, b, **kw):
    return lax.dot_general(a, b, (((1,), (1,)), ((), ())), preferred_element_type=F32, **kw)


def _dot_tn(a, b, **kw):
    return lax.dot_general(a, b, (((0,), (0,)), ((), ())), preferred_element_type=F32, **kw)


def _sigmoid(x):
    return 1.0 / (1.0 + jnp.exp(-x))


def _log_sigmoid(x):
    return -(jnp.maximum(-x, 0.0) + jnp.log1p(jnp.exp(-jnp.abs(x))))


def _fold8(x):
    m, n = x.shape
    return x.reshape(m // 8, 8, n).sum(axis=0)


def _iota(shape, dim):
    return lax.broadcasted_iota(jnp.int32, shape, dim)


def _row_tile(s):
    return min(s, 512)


class _Side:
    def __init__(self, inputs, out_shape, scratch, ops):
        self.inputs, self.out_shape, self.scratch, self.ops = list(inputs), list(out_shape), list(scratch), ops


ANY_SPEC = pl.BlockSpec(memory_space=pl.ANY)


def _mm_nn(a, b, *, out_dtype, tm, tn, tk, name, a_fn=None, epi=None, extra=None):
    m, k = a.shape
    _, n = b.shape
    nk = k // tk

    def body_one(*refs):
        a_ref, b_ref = refs[0], refs[1]
        at = a_ref[...] if a_fn is None else a_fn(a_ref[...])
        r = _dot(at, b_ref[...])
        if epi is not None:
            r = epi(r, None if extra is None else refs[2][...])
        refs[-1][...] = r.astype(out_dtype)

    if nk == 1:
        in_specs = [pl.BlockSpec((tm, k), lambda i, j: (i, 0)), pl.BlockSpec((k, tn), lambda i, j: (0, j))]
        args = [a, b]
        if extra is not None:
            in_specs.append(pl.BlockSpec((tm, tn), lambda i, j: (i, j)))
            args.append(extra)
        return pl.pallas_call(
            body_one, grid=(m // tm, n // tn), in_specs=in_specs, out_specs=pl.BlockSpec((tm, tn), lambda i, j: (i, j)),
            out_shape=jax.ShapeDtypeStruct((m, n), out_dtype), name=name, compiler_params=_cp("parallel", "parallel"))(*args)

    def body(*refs):
        if extra is None:
            a_ref, b_ref, o_ref, acc = refs
            x_ref = None
        else:
            a_ref, b_ref, x_ref, o_ref, acc = refs
        kk = pl.program_id(2)

        @pl.when(kk == 0)
        def _():
            acc[...] = jnp.zeros_like(acc)

        at = a_ref[...]
        if a_fn is not None:
            at = a_fn(at)
        acc[...] += _dot(at, b_ref[...])

        @pl.when(kk == nk - 1)
        def _():
            r = acc[...]
            if epi is not None:
                r = epi(r, None if x_ref is None else x_ref[...])
            o_ref[...] = r.astype(out_dtype)

    in_specs = [pl.BlockSpec((tm, tk), lambda i, j, kk: (i, kk)), pl.BlockSpec((tk, tn), lambda i, j, kk: (kk, j))]
    args = [a, b]
    if extra is not None:
        in_specs.append(pl.BlockSpec((tm, tn), lambda i, j, kk: (i, j)))
        args.append(extra)
    return pl.pallas_call(
        body, grid=(m // tm, n // tn, nk), in_specs=in_specs,
        out_specs=pl.BlockSpec((tm, tn), lambda i, j, kk: (i, j)),
        out_shape=jax.ShapeDtypeStruct((m, n), out_dtype),
        scratch_shapes=[pltpu.VMEM((tm, tn), F32)], name=name,
        compiler_params=_cp("parallel", "parallel", "arbitrary"))(*args)


def _mm_tn(a, b, *, tm, tn, ts, name, a_fn=None):
    s, m = a.shape
    _, n = b.shape
    ns = s // ts

    def body(a_ref, b_ref, o_ref, acc):
        kk = pl.program_id(2)

        @pl.when(kk == 0)
        def _():
            acc[...] = jnp.zeros_like(acc)

        at = a_ref[...]
        if a_fn is not None:
            at = a_fn(at)
        acc[...] += _dot_tn(at, b_ref[...])

        @pl.when(kk == ns - 1)
        def _():
            o_ref[...] = acc[...]

    return pl.pallas_call(
        body, grid=(m // tm, n // tn, ns),
        in_specs=[pl.BlockSpec((ts, tm), lambda i, j, kk: (kk, i)), pl.BlockSpec((ts, tn), lambda i, j, kk: (kk, j))],
        out_specs=pl.BlockSpec((tm, tn), lambda i, j, kk: (i, j)),
        out_shape=jax.ShapeDtypeStruct((m, n), F32),
        scratch_shapes=[pltpu.VMEM((tm, tn), F32)], name=name,
        compiler_params=_cp("parallel", "parallel", "arbitrary"))(a, b)


def _mm_tn_cat(a, bs, *, ts, name):
    s, m = a.shape
    n = sum(b.shape[1] for b in bs)
    ns = s // ts
    nb = len(bs)

    def body(*refs):
        a_ref, b_refs, o_ref, acc = refs[0], refs[1:1 + nb], refs[1 + nb], refs[2 + nb]
        kk = pl.program_id(0)

        @pl.when(kk == 0)
        def _():
            acc[...] = jnp.zeros_like(acc)

        bt = b_refs[0][...] if nb == 1 else jnp.concatenate([r[...] for r in b_refs], axis=1)
        acc[...] += _dot_tn(a_ref[...], bt)

        @pl.when(kk == ns - 1)
        def _():
            o_ref[...] = acc[...]

    return pl.pallas_call(
        body, grid=(ns,),
        in_specs=[pl.BlockSpec((ts, m), lambda kk: (kk, 0))] + [pl.BlockSpec((ts, b.shape[1]), lambda kk: (kk, 0)) for b in bs],
        out_specs=pl.BlockSpec((m, n), lambda kk: (0, 0)), out_shape=jax.ShapeDtypeStruct((m, n), F32),
        scratch_shapes=[pltpu.VMEM((m, n), F32)], name=name, compiler_params=_cp("arbitrary"))(a, *bs)


def _proj(u, w_main, w_e, side=None):
    s, k = u.shape
    n = w_main.shape[1]
    tm, tn = min(s, 1024), n // 4
    n_sin = 0 if side is None else len(side.inputs)
    n_sout = 0 if side is None else len(side.out_shape)

    def body(u_ref, w_ref, we_ref, *rest):
        pm_ref, pe_ref = rest[n_sin:n_sin + 2]
        i, j = pl.program_id(0), pl.program_id(1)
        if side is not None:
            start, finish = side.ops(rest[:n_sin], rest[n_sin + 2:n_sin + 2 + n_sout], rest[n_sin + 2 + n_sout:])
            pl.when((i == 0) & (j == 0))(start)
        ut = u_ref[...]
        pm_ref[...] = _dot(ut, w_ref[...]).astype(BF16)

        @pl.when(j == 0)
        def _():
            pe_ref[...] = _dot(ut, we_ref[...])

        if side is not None:
            pl.when((i == s // tm - 1) & (j == n // tn - 1))(finish)

    side_in = [] if side is None else side.inputs
    return pl.pallas_call(
        body, grid=(s // tm, n // tn),
        in_specs=[pl.BlockSpec((tm, k), lambda i, j: (i, 0)), pl.BlockSpec((k, tn), lambda i, j: (0, j)),
                  pl.BlockSpec((k, PE_W), lambda i, j: (0, 0))] + [ANY_SPEC] * n_sin,
        out_specs=[pl.BlockSpec((tm, tn), lambda i, j: (i, j)), pl.BlockSpec((tm, PE_W), lambda i, j: (i, 0))]
        + [ANY_SPEC] * n_sout,
        out_shape=[jax.ShapeDtypeStruct((s, n), BF16), jax.ShapeDtypeStruct((s, PE_W), F32)]
        + ([] if side is None else side.out_shape),
        scratch_shapes=[] if side is None else side.scratch,
        name="proj_main", compiler_params=_cp("arbitrary", "arbitrary"))(u, w_main, w_e, *side_in)


def _transpose(w, name):
    r, c = w.shape
    tr = min(r, 256)

    def body(w_ref, o_ref):
        o_ref[...] = w_ref[...].T

    return pl.pallas_call(body, grid=(r // tr,), in_specs=[pl.BlockSpec((tr, c), lambda i: (i, 0))],
                          out_specs=pl.BlockSpec((c, tr), lambda i: (0, i)),
                          out_shape=jax.ShapeDtypeStruct((c, r), w.dtype), name=name, compiler_params=_cp("parallel"))(w)


def _relu2_bf16(t):
    r = jnp.maximum(t.astype(F32), 0.0)
    return (r * r).astype(BF16)


def _rms_fwd(x, g, name, side=None):
    s, d = x.shape
    tm = min(s, 512)
    n_sin = 0 if side is None else len(side.inputs)
    n_sout = 0 if side is None else len(side.out_shape)

    def body(x_ref, g_ref, *rest):
        u_ref, r_ref = rest[n_sin:n_sin + 2]
        if side is not None:
            start, finish = side.ops(rest[:n_sin], rest[n_sin + 2:n_sin + 2 + n_sout], rest[n_sin + 2 + n_sout:])
            pl.when(pl.program_id(0) == 0)(start)
        xv = x_ref[...]
        r = lax.rsqrt(jnp.mean(xv * xv, axis=-1, keepdims=True) + EPS)
        u_ref[...] = ((xv * r) * g_ref[...]).astype(BF16)
        r_ref[...] = r
        if side is not None:
            pl.when(pl.program_id(0) == s // tm - 1)(finish)

    side_in = [] if side is None else side.inputs
    return pl.pallas_call(
        body, grid=(s // tm,),
        in_specs=[pl.BlockSpec((tm, d), lambda i: (i, 0)), pl.BlockSpec((1, d), lambda i: (0, 0))] + [ANY_SPEC] * n_sin,
        out_specs=[pl.BlockSpec((tm, d), lambda i: (i, 0)), pl.BlockSpec((tm, 1), lambda i: (i, 0))] + [ANY_SPEC] * n_sout,
        out_shape=[jax.ShapeDtypeStruct((s, d), BF16), jax.ShapeDtypeStruct((s, 1), F32)]
        + ([] if side is None else side.out_shape),
        scratch_shapes=[] if side is None else side.scratch,
        name=name, compiler_params=_cp("parallel" if side is None else "arbitrary"))(x, g, *side_in)


def _mm_norm_bwd(a_parts, b, xin, r, g, dres, *, name, want_bf16, side=None):
    s = a_parts[0].shape[0]
    k = b.shape[0]
    na = len(a_parts)
    offs = [sum(p.shape[1] for p in a_parts[:i]) for i in range(na)]
    assert offs[-1] + a_parts[-1].shape[1] == k
    tm = min(s, 256)
    n_out = 3 if want_bf16 else 2
    n_sin = 0 if side is None else len(side.inputs)
    n_sout = 0 if side is None else len(side.out_shape)

    def body(*refs):
        a_refs = refs[:na]
        b_ref, x_ref, r_ref, g_ref, dres_ref = refs[na:na + 5]
        rest = refs[na + 5:]
        outs = rest[n_sin:n_sin + n_out]
        dx_ref, dg_ref = outs[0], outs[-1]
        if side is not None:
            start, finish = side.ops(rest[:n_sin], rest[n_sin + n_out:n_sin + n_out + n_sout], rest[n_sin + n_out + n_sout:])
            pl.when(pl.program_id(0) == 0)(start)

        @pl.when(pl.program_id(0) == 0)
        def _():
            dg_ref[...] = jnp.zeros_like(dg_ref)

        du = _dot(a_refs[0][...], b_ref[0:a_parts[0].shape[1], :])
        for a_ref, off, part in zip(a_refs[1:], offs[1:], a_parts[1:]):
            du = du + _dot(a_ref[...], b_ref[off:off + part.shape[1], :])
        xn = x_ref[...] * r_ref[...]
        dg_ref[...] += _fold8(du * xn)
        dxn = du * g_ref[...]
        dx = dres_ref[...] + r_ref[...] * (dxn - xn * jnp.mean(dxn * xn, axis=-1, keepdims=True))
        dx_ref[...] = dx
        if want_bf16:
            outs[1][...] = dx.astype(BF16)
        if side is not None:
            pl.when(pl.program_id(0) == s // tm - 1)(finish)

    row = lambda i: (i, 0)
    const = lambda i: (0, 0)
    out_specs = [pl.BlockSpec((tm, D), row)]
    out_shape = [jax.ShapeDtypeStruct((s, D), F32)]
    if want_bf16:
        out_specs.append(pl.BlockSpec((tm, D), row))
        out_shape.append(jax.ShapeDtypeStruct((s, D), BF16))
    out_specs.append(pl.BlockSpec((8, D), const))
    out_shape.append(jax.ShapeDtypeStruct((8, D), F32))
    side_in = [] if side is None else side.inputs
    return pl.pallas_call(
        body, grid=(s // tm,),
        in_specs=[pl.BlockSpec((tm, p.shape[1]), row) for p in a_parts]
        + [pl.BlockSpec((k, D), const, pipeline_mode=pl.Buffered(1)),
           pl.BlockSpec((tm, D), row), pl.BlockSpec((tm, 1), row), pl.BlockSpec((1, D), const),
           pl.BlockSpec((tm, D), row)] + [ANY_SPEC] * n_sin,
        out_specs=out_specs + [ANY_SPEC] * n_sout, out_shape=out_shape + ([] if side is None else side.out_shape),
        scratch_shapes=[] if side is None else side.scratch,
        name=name, compiler_params=_cp("arbitrary"))(*a_parts, b, xin, r, g, dres, *side_in)


def _gla_consts():
    lmask = _iota((4 * CHUNK, CHUNK), 0) % CHUNK >= _iota((4 * CHUNK, CHUNK), 1)
    hmask = _iota((256, 256), 0) // CHUNK == _iota((256, 256), 1) // CHUNK
    bd = _iota((256, 512), 0) // CHUNK == _iota((256, 512), 1) // 128
    return lmask, hmask, bd


def _fold_heads(x):
    return x[0:64] + x[64:128] + x[128:192] + x[192:256]


def _gla_decays(la, b_scr, dec_scr):
    tri = (_iota((CHUNK, CHUNK), 0) >= _iota((CHUNK, CHUNK), 1)).astype(BF16)
    ones = jnp.ones((CHUNK, 128), BF16)
    for c in range(la.shape[0] // CHUNK):
        la3 = _split3(la[CHUNK * c:CHUNK * (c + 1)])
        b_scr[CHUNK * c:CHUNK * (c + 1), :] = _sum3(_dot(tri, la3), 1)
        dec_scr[c] = jnp.exp(_sum3(_dot_tn(la3, ones), 0))


def _gla_chunk(b, qc, kc):
    bl = b[CHUNK - 1:CHUNK, :]
    ep, en, ek = jnp.exp(b), jnp.exp(-b), jnp.exp(bl - b)
    return bl, ep, en, ek, qc * ep, qc * en, kc * en, kc * ep, kc * ek


def _gla_fwd(pm, pe, wau_p, b_alpha):
    s = pm.shape[0]
    t = min(s, 1024)
    nc = t // CHUNK

    def body(q_ref, k_ref, v_ref, e_ref, wau_ref, ba_ref, o_ref, st_ref, state, b_scr, dec_scr):
        @pl.when(pl.program_id(0) == 0)
        def _():
            state[...] = jnp.zeros_like(state)

        z = _dot(e_ref[...].astype(BF16), wau_ref[...]) + ba_ref[...]
        _gla_decays(_log_sigmoid(z) * (1.0 / GLA_TAU), b_scr, dec_scr)
        lmask, hmask, bd = _gla_consts()

        def chunk(c, carry):
            rows = pl.ds(pl.multiple_of(c * CHUNK, CHUNK), CHUNK)
            qc = q_ref[rows, :].astype(F32) * 0.125
            kc = k_ref[rows, :].astype(F32)
            vc = v_ref[rows, :]
            _, _, _, _, qp, qn, kn, kp, kk = _gla_chunk(b_scr[rows, :], qc, kc)
            decb = jnp.concatenate([dec_scr[c]] * 4, axis=1)
            qs = jnp.where(hmask, jnp.concatenate([qp] * 4, axis=0), 0.0).astype(BF16)
            qns = jnp.where(hmask, jnp.concatenate([qn] * 4, axis=0), 0.0).astype(BF16)
            attn = jnp.where(lmask, _dot_nt(qs, kn.astype(BF16)), _dot_nt(qns, kp.astype(BF16))).astype(BF16)
            st = state[...]
            o_intra = _fold_heads(jnp.where(bd, _dot(attn, vc), 0.0))
            o_ref[rows, :] = o_intra + _dot(qp.astype(BF16), st.astype(BF16))
            for h in range(4):
                st_ref[c, :, 128 * h:128 * (h + 1)] = st[64 * h:64 * (h + 1), 128 * h:128 * (h + 1)]
            kv = jnp.where(bd, _dot_tn(kk.astype(BF16), vc), 0.0)
            state[...] = st * decb + kv
            return carry

        lax.fori_loop(0, nc, chunk, 0)

    return pl.pallas_call(
        body, grid=(s // t,),
        in_specs=[pl.BlockSpec((t, 256), lambda i: (i, C_GQ // 256)), pl.BlockSpec((t, 256), lambda i: (i, C_GK // 256)),
                  pl.BlockSpec((t, 512), lambda i: (i, C_GV // 512)), pl.BlockSpec((t, PE_W), lambda i: (i, 0)),
                  pl.BlockSpec((PE_W, 256), lambda i: (0, 0)), pl.BlockSpec((1, 256), lambda i: (0, 0))],
        out_specs=[pl.BlockSpec((t, 512), lambda i: (i, 0)), pl.BlockSpec((nc, CHUNK, 512), lambda i: (i, 0, 0))],
        out_shape=[jax.ShapeDtypeStruct((s, 512), F32), jax.ShapeDtypeStruct((s // CHUNK, CHUNK, 512), F32)],
        scratch_shapes=[pltpu.VMEM((256, 512), F32), pltpu.VMEM((t, 256), F32), pltpu.VMEM((nc, 256, 128), F32)],
        name="gla_fwd", compiler_params=_cp("arbitrary"))(pm, pm, pm, pe, wau_p, b_alpha)


def _gla_bwd(pm, pe, wau_p, wau_pt, b_alpha, do, states):
    s = pm.shape[0]
    t = min(s, 1024)
    nc = t // CHUNK
    nb = s // t

    def body(q_ref, k_ref, v_ref, e_ref, wau_ref, waut_ref, ba_ref, do_ref, st_ref,
             dq_ref, dk_ref, dv_ref, de_ref, dwau_ref, dba_ref, gstate, b_scr, db_scr, dec_scr):
        @pl.when(pl.program_id(0) == 0)
        def _():
            gstate[...] = jnp.zeros_like(gstate)
            dwau_ref[...] = jnp.zeros_like(dwau_ref)
            dba_ref[...] = jnp.zeros_like(dba_ref)

        eb = e_ref[...].astype(BF16)
        z = _dot(eb, wau_ref[...]) + ba_ref[...]
        _gla_decays(_log_sigmoid(z) * (1.0 / GLA_TAU), b_scr, dec_scr)
        lmask, hmask, bd = _gla_consts()
        last_row = _iota((CHUNK, 256), 0) == CHUNK - 1

        def chunk(cc, carry):
            c = nc - 1 - cc
            rows = pl.ds(pl.multiple_of(c * CHUNK, CHUNK), CHUNK)
            qc = q_ref[rows, :].astype(F32) * 0.125
            kc = k_ref[rows, :].astype(F32)
            vc = v_ref[rows, :]
            dob = do_ref[rows, :]
            bl, ep, en, ek, qp, qn, kn, kp, kk = _gla_chunk(b_scr[rows, :], qc, kc)
            decb = jnp.concatenate([dec_scr[c]] * 4, axis=1)
            qs = jnp.where(hmask, jnp.concatenate([qp] * 4, axis=0), 0.0).astype(BF16)
            qns = jnp.where(hmask, jnp.concatenate([qn] * 4, axis=0), 0.0).astype(BF16)
            knb, kpb = kn.astype(BF16), kp.astype(BF16)
            attn = jnp.where(lmask, _dot_nt(qs, knb), _dot_nt(qns, kpb)).astype(BF16)
            st = jnp.where(bd, jnp.concatenate([st_ref[c]] * 4, axis=0), 0.0)
            g = gstate[...]
            gb = g.astype(BF16)
            do_s = jnp.where(bd, jnp.concatenate([dob] * 4, axis=0), jnp.zeros((), BF16))
            dattn = _dot_nt(do_s, vc)
            dv_ref[rows, :] = (_dot_tn(attn, do_s) + _dot(kk.astype(BF16), gb)).astype(BF16)
            dac = jnp.where(lmask, dattn, 0.0).astype(BF16)
            daa = jnp.where(lmask, 0.0, dattn).astype(BF16)
            dqp = _fold_heads(jnp.where(hmask, _dot(dac, knb), 0.0)) + _dot_nt(dob, st.astype(BF16))
            dqn = _fold_heads(jnp.where(hmask, _dot(daa, kpb), 0.0))
            dkn = _dot_tn(dac, qs)
            dkp = _dot_tn(daa, qns)
            dkk = _dot_nt(vc, gb)
            ddec = _dot_nt(jnp.ones((8, 1536), BF16), _split3(g * st))[0:1, :]
            gstate[...] = decb * g + jnp.where(bd, _dot_tn(qp.astype(BF16), dob), 0.0)
            dq_ref[rows, :] = ((dqp * ep + dqn * en) * 0.125).astype(BF16)
            dk_ref[rows, :] = (dkn * en + dkp * ep + dkk * ek).astype(BF16)
            dek = dkk * kc * ek
            db = (dqp * qc + dkp * kc) * ep - (dqn * qc + dkn * kc) * en - dek
            dbl = jnp.sum(dek, axis=0, keepdims=True) + ddec * jnp.exp(bl)
            db_scr[rows, :] = db + jnp.where(last_row, dbl, 0.0)
            return carry

        lax.fori_loop(0, nc, chunk, 0)
        triu = (_iota((CHUNK, CHUNK), 0) <= _iota((CHUNK, CHUNK), 1)).astype(BF16)
        dla = jnp.concatenate([_sum3(_dot(triu, _split3(db_scr[CHUNK * c:CHUNK * (c + 1), :])), 1) for c in range(nc)], axis=0)
        dz = dla * (1.0 / GLA_TAU) * _sigmoid(-z)
        dzb = dz.astype(BF16)
        dwau_ref[...] += _dot_tn(eb, dzb)
        dba_ref[...] += _fold8(dz)
        de_ref[...] = _dot(dzb, waut_ref[...])

    rev = lambda i: nb - 1 - i
    return pl.pallas_call(
        body, grid=(nb,),
        in_specs=[pl.BlockSpec((t, 256), lambda i: (rev(i), C_GQ // 256)), pl.BlockSpec((t, 256), lambda i: (rev(i), C_GK // 256)),
                  pl.BlockSpec((t, 512), lambda i: (rev(i), C_GV // 512)), pl.BlockSpec((t, PE_W), lambda i: (rev(i), 0)),
                  pl.BlockSpec((PE_W, 256), lambda i: (0, 0)), pl.BlockSpec((256, PE_W), lambda i: (0, 0)),
                  pl.BlockSpec((1, 256), lambda i: (0, 0)), pl.BlockSpec((t, 512), lambda i: (rev(i), 0)),
                  pl.BlockSpec((nc, CHUNK, 512), lambda i: (rev(i), 0, 0))],
        out_specs=[pl.BlockSpec((t, 256), lambda i: (rev(i), 0)), pl.BlockSpec((t, 256), lambda i: (rev(i), 0)),
                   pl.BlockSpec((t, 512), lambda i: (rev(i), 0)), pl.BlockSpec((t, PE_W), lambda i: (rev(i), 0)),
                   pl.BlockSpec((PE_W, 256), lambda i: (0, 0)), pl.BlockSpec((8, 256), lambda i: (0, 0))],
        out_shape=[jax.ShapeDtypeStruct((s, 256), BF16), jax.ShapeDtypeStruct((s, 256), BF16),
                   jax.ShapeDtypeStruct((s, 512), BF16), jax.ShapeDtypeStruct((s, PE_W), F32),
                   jax.ShapeDtypeStruct((PE_W, 256), F32), jax.ShapeDtypeStruct((8, 256), F32)],
        scratch_shapes=[pltpu.VMEM((256, 512), F32), pltpu.VMEM((t, 256), F32), pltpu.VMEM((t, 256), F32),
                        pltpu.VMEM((nc, 256, 128), F32)],
        name="gla_bwd", compiler_params=_cp("arbitrary"))(pm, pm, pm, pe, wau_p, wau_pt, b_alpha, do, states)


def _fcum_fwd(pe, bias):
    s = pe.shape[0]
    t = min(s, 512)

    def body(e_ref, b_ref, f_ref, carry):
        @pl.when(pl.program_id(0) == 0)
        def _():
            carry[...] = jnp.zeros_like(carry)

        lf = _log_sigmoid(e_ref[...] + b_ref[...])
        tri = (_iota((t, t), 0) >= _iota((t, t), 1)).astype(BF16)
        f = _sum3(_dot(tri, _split3(lf)), 1) + carry[0:1, :]
        f_ref[...] = f
        carry[...] = jnp.broadcast_to(f[t - 1:t, :], carry.shape)

    return pl.pallas_call(
        body, grid=(s // t,),
        in_specs=[pl.BlockSpec((t, PE_W), lambda i: (i, 0)), pl.BlockSpec((1, PE_W), lambda i: (0, 0))],
        out_specs=pl.BlockSpec((t, PE_W), lambda i: (i, 0)),
        out_shape=jax.ShapeDtypeStruct((s, PE_W), F32), scratch_shapes=[pltpu.VMEM((8, PE_W), F32)],
        name="fcum_fwd", compiler_params=_cp("arbitrary"))(pe, bias)


def _fcum_bwd(pe, bias, df):
    s = pe.shape[0]
    t = min(s, 512)
    nb = s // t

    def body(e_ref, b_ref, df_ref, de_ref, db_ref, carry):
        @pl.when(pl.program_id(0) == 0)
        def _():
            carry[...] = jnp.zeros_like(carry)
            db_ref[...] = jnp.zeros_like(db_ref)

        triu = (_iota((t, t), 0) <= _iota((t, t), 1)).astype(BF16)
        dlf = _sum3(_dot(triu, _split3(df_ref[...])), 1) + carry[0:1, :]
        carry[...] = jnp.broadcast_to(dlf[0:1, :], carry.shape)
        lane = _iota((t, PE_W), 1)
        dff = jnp.where((lane >= FF_LANE) & (lane < FF_LANE + 8), dlf * _sigmoid(-(e_ref[...] + b_ref[...])), 0.0)
        de_ref[...] = dff
        db_ref[...] += _fold8(dff)

    rev = lambda i: (nb - 1 - i, 0)
    return pl.pallas_call(
        body, grid=(nb,),
        in_specs=[pl.BlockSpec((t, PE_W), rev), pl.BlockSpec((1, PE_W), lambda i: (0, 0)), pl.BlockSpec((t, PE_W), rev)],
        out_specs=[pl.BlockSpec((t, PE_W), rev), pl.BlockSpec((8, PE_W), lambda i: (0, 0))],
        out_shape=[jax.ShapeDtypeStruct((s, PE_W), F32), jax.ShapeDtypeStruct((8, PE_W), F32)],
        scratch_shapes=[pltpu.VMEM((8, PE_W), F32)],
        name="fcum_bwd", compiler_params=_cp("arbitrary"))(pe, bias, df)


FOX_WIDE = 1024


def _split3(x):
    hi = x.astype(BF16)
    r = x - hi.astype(F32)
    mid = r.astype(BF16)
    lo = (r - mid.astype(F32)).astype(BF16)
    return jnp.concatenate([hi, mid, lo], axis=1)


def _sum3(x, axis):
    n = x.shape[axis] // 3
    parts = [lax.slice_in_dim(x, n * p, n * (p + 1), axis=axis) for p in range(3)]
    return (parts[0] + parts[1]) + parts[2]


def _fox_tables():
    heads, lane = np.arange(8), np.arange(64)
    spread = np.zeros((512, 1024), np.float32)
    spread[(64 * heads[:, None] + lane).ravel(), (128 * heads[:, None] + lane).ravel()] = 1.0
    def place(src_lane0, dst_off, val):
        t = np.zeros((384, 1024), np.float32)
        for p in range(3):
            t[128 * p + src_lane0 + heads, 128 * heads + dst_off + p] = val
        return t
    def const(off, val):
        c = np.zeros((1, 1024), np.float32)
        for p in range(3):
            c[0, 128 * heads + off + p] = val
        return c
    rows = np.zeros((8, 128), np.float32)
    rows[heads, FF_LANE + heads] = 1.0
    bf = lambda a: jnp.asarray(a, BF16)
    return dict(spread=bf(spread),
                f_to_q=bf(place(FF_LANE, 64, 1.0)), f_to_k=bf(place(FF_LANE, 67, -1.0)), d_to_do=bf(place(0, 64, 1.0)),
                ones_q=jnp.asarray(const(67, 1.0)), ones_k=jnp.asarray(const(64, 1.0)), ones_v=jnp.asarray(const(64, -1.0)),
                rows=jnp.asarray(rows))


LOG2E = 1.4426950408889634


def _fox_prep(pm, f128, lse8, tb, *, backward):
    s = pm.shape[0]
    tm = _row_tile(s)

    def body(*refs):
        if backward:
            q_ref, f_ref, lse_ref, sp_ref, fq_ref, cq_ref, rows_ref, qa_ref = refs
            f = f_ref[...] * LOG2E - _dot_tn(lse_ref[...], rows_ref[...], precision=HI)
            q2 = (q_ref[...].astype(F32) * (0.125 * LOG2E)).astype(BF16)
            qa_ref[...] = (_dot(q2, sp_ref[...]) + _dot(_split3(f), fq_ref[...]) + cq_ref[...]).astype(BF16)
            return
        (q_ref, k_ref, v_ref, f_ref, sp_ref, fq_ref, fk_ref, cq_ref, ck_ref, cv_ref,
         qa_ref, ka_ref, va_ref, vt_ref, qt_ref, kt_ref) = refs
        f3 = _split3(f_ref[...] * LOG2E)
        q, k, v = q_ref[...].astype(F32), k_ref[...], v_ref[...]
        sp = sp_ref[...]
        qa_ref[...] = (_dot((q * (0.125 * LOG2E)).astype(BF16), sp) + _dot(f3, fq_ref[...]) + cq_ref[...]).astype(BF16)
        ka_ref[...] = (_dot(k, sp) + _dot(f3, fk_ref[...]) + ck_ref[...]).astype(BF16)
        va_ref[...] = (_dot(v, sp) + cv_ref[...]).astype(BF16)
        vt_ref[...] = v.T
        qt_ref[...] = (q * 0.125).astype(BF16).T
        kt_ref[...] = (k.astype(F32) * 0.125).astype(BF16).T

    row = lambda i: (i, 0)
    const = lambda i: (0, 0)
    blk = lambda c: pl.BlockSpec((tm, 512), lambda i: (i, c // 512))
    wide = pl.BlockSpec((tm, 1024), row)
    mat = lambda a: pl.BlockSpec(a.shape, const)
    if backward:
        ins = [pm, f128, lse8, tb["spread"], tb["f_to_q"], tb["ones_q"], tb["rows"]]
        in_specs = [blk(C_FQ), pl.BlockSpec((tm, 128), row), pl.BlockSpec((8, tm), lambda i: (0, i))] + [mat(a) for a in ins[3:]]
        out_specs, out_shape = wide, jax.ShapeDtypeStruct((s, 1024), BF16)
    else:
        ins = [pm, pm, pm, f128, tb["spread"], tb["f_to_q"], tb["f_to_k"], tb["ones_q"], tb["ones_k"], tb["ones_v"]]
        in_specs = [blk(C_FQ), blk(C_FK), blk(C_FV), pl.BlockSpec((tm, 128), row)] + [mat(a) for a in ins[4:]]
        tr = pl.BlockSpec((512, tm), lambda i: (0, i))
        out_specs = [wide, wide, wide, tr, tr, tr]
        out_shape = [jax.ShapeDtypeStruct((s, 1024), BF16)] * 3 + [jax.ShapeDtypeStruct((512, s), BF16)] * 3
    return pl.pallas_call(body, grid=(s // tm,), in_specs=in_specs, out_specs=out_specs, out_shape=out_shape,
                          name="fox_prep_bwd" if backward else "fox_prep", compiler_params=_cp("parallel"))(*ins)


def _fox_post(dqt, dkt, dvt, rowsum8, colsum8, tb):
    s = dqt.shape[1]
    tm = _row_tile(s)

    def body(dqt_ref, dkt_ref, dvt_ref, rs_ref, cs_ref, rows_ref, dfq_ref, dfk_ref, dfv_ref, df_ref):
        dfq_ref[...] = dqt_ref[...].T.astype(BF16)
        dfk_ref[...] = dkt_ref[...].T
        dfv_ref[...] = dvt_ref[...].T
        df_ref[...] = _dot_tn(rs_ref[...] - cs_ref[...], rows_ref[...], precision=HI)

    row = lambda i: (i, 0)
    tr = pl.BlockSpec((512, tm), lambda i: (0, i))
    out = pl.BlockSpec((tm, 512), row)
    heads = pl.BlockSpec((8, tm), lambda i: (0, i))
    return pl.pallas_call(
        body, grid=(s // tm,),
        in_specs=[tr, tr, tr, heads, heads, pl.BlockSpec((8, 128), lambda i: (0, 0))],
        out_specs=[out, out, out, pl.BlockSpec((tm, 128), row)],
        out_shape=[jax.ShapeDtypeStruct((s, 512), BF16)] * 3 + [jax.ShapeDtypeStruct((s, 128), F32)],
        name="fox_post", compiler_params=_cp("parallel"))(dqt, dkt, dvt, rowsum8, colsum8, tb["rows"])


def _fox_fwd(k_aug, q_aug, vt):
    s = k_aug.shape[0]
    nh = 8
    tk = _row_tile(s)
    tq = min(s, 2 * FOX_WIDE)
    per = tq // tk

    def body(k_ref, q_ref, v_ref, o_ref, lse_ref, sbuf):
        i = pl.program_id(1)
        qa = q_ref[...]

        def scores(j):
            return _dot_nt(k_ref[pl.ds(pl.multiple_of(j * tk, tk), tk), :], qa)

        ones_row = (_iota((16, tk), 0) == 0).astype(BF16)

        def update(st, j, carry):
            m, acc = carry
            m2 = jnp.maximum(m, jnp.max(st, axis=0, keepdims=True))
            p = jnp.exp2(st - m2)
            vj = jnp.concatenate([v_ref[:, pl.ds(pl.multiple_of(j * tk, tk), tk)], ones_row], axis=0)
            return m2, jnp.exp2(m - m2) * acc + _dot(vj, p.astype(BF16))

        def step(a, carry):
            sbuf[1] = scores(2 * a + 1)
            carry = update(sbuf[0], 2 * a, carry)
            sbuf[0] = scores(2 * a + 2)
            return update(sbuf[1], 2 * a + 1, carry)

        n = i * per
        sbuf[0] = scores(0)
        carry = (jnp.full((1, tq), -1e30, F32), jnp.zeros((80, tq), F32))
        carry = lax.fori_loop(0, n // 2, step, carry)
        tri = _iota((tk, tk), 0) <= _iota((tk, tk), 1)
        late = [_dot_nt(k_ref[pl.ds(pl.multiple_of((n + r) * tk, tk), tk), :], qa[r * tk:, :]) for r in range(1, per)]
        for r in range(per):
            st = sbuf[0] if r == 0 else late[r - 1]
            head = jnp.where(tri, st[:, :tk], -1e30)
            st = head if st.shape[1] == tk else jnp.concatenate([head, st[:, tk:]], axis=1)
            part = update(st, n + r, tuple(c[:, r * tk:] for c in carry))
            carry = part if r == 0 else tuple(jnp.concatenate([old[:, :r * tk], new], axis=1) for old, new in zip(carry, part))
        m, acc = carry
        l = acc[64:65]
        o_ref[...] = (acc[0:64] / l).astype(BF16)
        lse_ref[0] = m + jnp.log2(l)

    return pl.pallas_call(
        body, grid=(nh, s // tq),
        in_specs=[pl.BlockSpec((s, 128), lambda h, i: (0, h)), pl.BlockSpec((tq, 128), lambda h, i: (i, h)),
                  pl.BlockSpec((64, s), lambda h, i: (h, 0))],
        out_specs=[pl.BlockSpec((64, tq), lambda h, i: (h, i)), pl.BlockSpec((1, 1, tq), lambda h, i: (h, 0, i))],
        out_shape=[jax.ShapeDtypeStruct((512, s), BF16), jax.ShapeDtypeStruct((nh, 1, s), F32)],
        scratch_shapes=[pltpu.VMEM((2, tk, tq), F32)],
        name="fox_fwd", compiler_params=_cp("parallel", "arbitrary"))(k_aug, q_aug, vt)


def _fox_bwd(q_aug, do_aug, qt, dot_, k_aug, v_aug, kt):
    s = q_aug.shape[0]
    nh = 8
    tq = _row_tile(s)
    tk = min(s, 2 * FOX_WIDE)
    per = tk // tq
    nqb = s // tq

    def body(qa_ref, da_ref, qt_ref, dt_ref, ka_ref, va_ref, kt_ref, dq_ref, rs_ref, dk_ref, dv_ref, dfk_ref):
        j = pl.program_id(1)

        @pl.when(j == 0)
        def _():
            dq_ref[...] = jnp.zeros_like(dq_ref)
            rs_ref[...] = jnp.zeros_like(rs_ref)

        ones_row = (_iota((16, tk), 0) == 0).astype(BF16)
        ka, va = ka_ref[...], va_ref[...]
        ks = jnp.concatenate([kt_ref[...], ones_row], axis=0)
        tri = _iota((tq, tq), 0) >= _iota((tq, tq), 1)

        def tile(i, w, carry):
            masked = w is not None
            w = tk if w is None else w
            rows = pl.ds(pl.multiple_of(i * tq, tq), tq)
            sp = _dot_nt(qa_ref[rows, :], ka[:w])
            if masked:
                last = jnp.where(tri, sp[:, w - tq:], -1e30)
                sp = last if w == tq else jnp.concatenate([sp[:, :w - tq], last], axis=1)
            p = jnp.exp2(sp)
            dsb = (p * _dot_nt(da_ref[rows, :], va[:w])).astype(BF16)
            dq = _dot_nt(ks[:, :w], dsb)
            dq_ref[:, rows] += dq[0:64]
            rs_ref[0, :, rows] += dq[64:72]
            new = (_dot(jnp.concatenate([qt_ref[:, rows], ones_row[:, :tq]], axis=0), dsb), _dot(dt_ref[:, rows], p.astype(BF16)))
            if w == tk:
                return tuple(c + d for c, d in zip(carry, new))
            return tuple(jnp.concatenate([c[:, :w] + d, c[:, w:]], axis=1) for c, d in zip(carry, new))

        carry = (jnp.zeros((80, tk), F32), jnp.zeros((64, tk), F32))
        for r in range(per):
            carry = tile(j * per + r, (r + 1) * tq, carry)
        dk, dv = lax.fori_loop((j + 1) * per, nqb, lambda i, c: tile(i, None, c), carry)
        dk_ref[...] = dk[0:64].astype(BF16)
        dv_ref[...] = dv.astype(BF16)
        dfk_ref[0] = dk[64:65]

    head_cols = lambda h, j: (0, h)
    head_rows = lambda h, j: (h, 0)
    once = dict(pipeline_mode=pl.Buffered(1))
    return pl.pallas_call(
        body, grid=(nh, s // tk),
        in_specs=[pl.BlockSpec((s, 128), head_cols, **once), pl.BlockSpec((s, 128), head_cols, **once),
                  pl.BlockSpec((64, s), head_rows, **once), pl.BlockSpec((64, s), head_rows, **once),
                  pl.BlockSpec((tk, 128), lambda h, j: (j, h)), pl.BlockSpec((tk, 128), lambda h, j: (j, h)),
                  pl.BlockSpec((64, tk), lambda h, j: (h, j))],
        out_specs=[pl.BlockSpec((64, s), head_rows), pl.BlockSpec((1, 8, s), lambda h, j: (h, 0, 0)),
                   pl.BlockSpec((64, tk), lambda h, j: (h, j)),
                   pl.BlockSpec((64, tk), lambda h, j: (h, j)), pl.BlockSpec((1, 1, tk), lambda h, j: (h, 0, j))],
        out_shape=[jax.ShapeDtypeStruct((512, s), F32), jax.ShapeDtypeStruct((nh, 8, s), F32),
                   jax.ShapeDtypeStruct((512, s), BF16),
                   jax.ShapeDtypeStruct((512, s), BF16), jax.ShapeDtypeStruct((nh, 1, s), F32)],
        name="fox_bwd", compiler_params=_cp("parallel", "arbitrary"))(q_aug, do_aug, qt, dot_, k_aug, v_aug, kt)


MEM_SCALE = 128 ** -0.5


def _mem_attn_fwd(pm, mkv):
    s = pm.shape[0]
    t = _row_tile(s)
    nm = mkv.shape[0]

    def body(q_ref, mk_ref, mv_ref, o_ref):
        for h in range(4):
            cols = slice(128 * h, 128 * (h + 1))
            sc = _dot_nt(q_ref[:, cols], mk_ref[:, cols]) * MEM_SCALE
            p = jnp.exp(sc - jnp.max(sc, axis=-1, keepdims=True))
            p = p / jnp.sum(p, axis=-1, keepdims=True)
            o_ref[:, cols] = _dot(p.astype(BF16), mv_ref[:, cols]).astype(BF16)

    return pl.pallas_call(
        body, grid=(s // t,),
        in_specs=[pl.BlockSpec((t, 512), lambda i: (i, C_MQ // 512)), pl.BlockSpec((nm, 512), lambda i: (0, 0)),
                  pl.BlockSpec((nm, 512), lambda i: (0, 1))],
        out_specs=pl.BlockSpec((t, 512), lambda i: (i, 0)),
        out_shape=jax.ShapeDtypeStruct((s, 512), BF16),
        name="mem_attn_fwd", compiler_params=_cp("parallel"))(pm, mkv, mkv)


def _mem_attn_bwd(pm, mkv, do):
    s = pm.shape[0]
    t = _row_tile(s)
    nm = mkv.shape[0]

    def body(q_ref, mk_ref, mv_ref, do_ref, dq_ref, dmk_ref, dmv_ref):
        @pl.when(pl.program_id(0) == 0)
        def _():
            dmk_ref[...] = jnp.zeros_like(dmk_ref)
            dmv_ref[...] = jnp.zeros_like(dmv_ref)

        for h in range(4):
            cols = slice(128 * h, 128 * (h + 1))
            qh, kh, vh, doh = q_ref[:, cols], mk_ref[:, cols], mv_ref[:, cols], do_ref[:, cols]
            sc = _dot_nt(qh, kh) * MEM_SCALE
            p = jnp.exp(sc - jnp.max(sc, axis=-1, keepdims=True))
            p = p / jnp.sum(p, axis=-1, keepdims=True)
            pb = p.astype(BF16)
            dp = _dot_nt(doh, vh)
            ds = (p * (dp - jnp.sum(p * dp, axis=-1, keepdims=True)) * MEM_SCALE).astype(BF16)
            dq_ref[:, cols] = _dot(ds, kh).astype(BF16)
            dmk_ref[:, cols] += _dot_tn(ds, qh)
            dmv_ref[:, cols] += _dot_tn(pb, doh)

    return pl.pallas_call(
        body, grid=(s // t,),
        in_specs=[pl.BlockSpec((t, 512), lambda i: (i, C_MQ // 512)), pl.BlockSpec((nm, 512), lambda i: (0, 0)),
                  pl.BlockSpec((nm, 512), lambda i: (0, 1)), pl.BlockSpec((t, 512), lambda i: (i, 0))],
        out_specs=[pl.BlockSpec((t, 512), lambda i: (i, 0)), pl.BlockSpec((nm, 512), lambda i: (0, 0)),
                   pl.BlockSpec((nm, 512), lambda i: (0, 0))],
        out_shape=[jax.ShapeDtypeStruct((s, 512), BF16), jax.ShapeDtypeStruct((nm, 512), F32),
                   jax.ShapeDtypeStruct((nm, 512), F32)],
        name="mem_attn_bwd", compiler_params=_cp("arbitrary"))(pm, mkv, mkv, do)


def _gain_grad(dxn_g, x, r, name):
    m, d = x.shape

    def body(d_ref, x_ref, r_ref, o_ref):
        o_ref[...] = _fold8(d_ref[...] * (x_ref[...] * r_ref[...]))

    return pl.pallas_call(body, out_shape=jax.ShapeDtypeStruct((8, d), F32), name=name,
                          compiler_params=pltpu.CompilerParams(vmem_limit_bytes=VMEM_LIMIT_BYTES))(dxn_g, x, r)


def _head_norm(o, gh):
    xs, rs = [], []
    for h in range(4):
        oh = o[:, 128 * h:128 * (h + 1)]
        r = lax.rsqrt(jnp.mean(oh * oh, axis=-1, keepdims=True) + EPS)
        xs.append(oh * r)
        rs.append(r)
    return xs, rs


def _merge_fwd(x, pm, o_gla, o_fox_t, o_mem, g_head, wg, wf, wm, wo, g_ffn):
    s = x.shape[0]
    t = min(s, 256)

    def body(x_ref, g0_ref, g1_ref, g2_ref, gg_ref, og_ref, of_ref, om_ref, gh_ref, wg_ref, wf_ref, wm_ref, wo_ref, gf_ref,
             mg_ref, h1_ref, u2_ref, r2_ref):
        xs, _ = _head_norm(og_ref[...], None)
        gg = gg_ref[...].astype(F32)
        sil = gg * _sigmoid(gg)
        ogn = jnp.concatenate(xs, axis=1) * gh_ref[...] * sil
        merged = (_sigmoid(g0_ref[...].astype(F32)) * _dot(ogn.astype(BF16), wg_ref[...])
                  + _sigmoid(g1_ref[...].astype(F32)) * _dot(of_ref[...].T, wf_ref[...])
                  + _sigmoid(g2_ref[...].astype(F32)) * _dot(om_ref[...], wm_ref[...]))
        mb = merged.astype(BF16)
        mg_ref[...] = mb
        h1 = x_ref[...] + _dot(mb, wo_ref[...])
        h1_ref[...] = h1
        r = lax.rsqrt(jnp.mean(h1 * h1, axis=-1, keepdims=True) + EPS)
        u2_ref[...] = ((h1 * r) * gf_ref[...]).astype(BF16)
        r2_ref[...] = r

    row = lambda i: (i, 0)
    const = lambda i: (0, 0)
    return pl.pallas_call(
        body, grid=(s // t,),
        in_specs=[pl.BlockSpec((t, D), row), pl.BlockSpec((t, D), lambda i: (i, 0)), pl.BlockSpec((t, D), lambda i: (i, 1)),
                  pl.BlockSpec((t, D), lambda i: (i, 2)), pl.BlockSpec((t, 512), lambda i: (i, C_GG // 512)),
                  pl.BlockSpec((t, 512), row), pl.BlockSpec((512, t), lambda i: (0, i)), pl.BlockSpec((t, 512), row),
                  pl.BlockSpec((1, 512), const), pl.BlockSpec((512, D), const), pl.BlockSpec((512, D), const),
                  pl.BlockSpec((512, D), const), pl.BlockSpec((D, D), const), pl.BlockSpec((1, D), const)],
        out_specs=[pl.BlockSpec((t, D), row), pl.BlockSpec((t, D), row), pl.BlockSpec((t, D), row), pl.BlockSpec((t, 1), row)],
        out_shape=[jax.ShapeDtypeStruct((s, D), BF16), jax.ShapeDtypeStruct((s, D), F32),
                   jax.ShapeDtypeStruct((s, D), BF16), jax.ShapeDtypeStruct((s, 1), F32)],
        name="merge_fwd", compiler_params=_cp("parallel"))(x, pm, pm, pm, pm, o_gla, o_fox_t, o_mem, g_head, wg, wf, wm, wo, g_ffn)


def _merge_bwd(dh1b, pm, o_gla, o_fox_t, o_mem, g_head, wg, wf, wm, wgt, wft, wmt, wot, spread, d_to_do):
    s = dh1b.shape[0]
    t = min(s, 256)

    def body(dh_ref, g0_ref, g1_ref, g2_ref, gg_ref, og_ref, of_ref, om_ref, gh_ref, wg_ref, wf_ref, wm_ref,
             wgt_ref, wft_ref, wmt_ref, wot_ref, sp_ref, dd_ref,
             dgt_ref, dgg_ref, dog_ref, da_ref, dot_ref, dom_ref, dwg_ref, dwf_ref, dwm_ref, dgh_ref):
        @pl.when(pl.program_id(0) == 0)
        def _():
            dwg_ref[...] = jnp.zeros_like(dwg_ref)
            dwf_ref[...] = jnp.zeros_like(dwf_ref)
            dwm_ref[...] = jnp.zeros_like(dwm_ref)
            dgh_ref[...] = jnp.zeros_like(dgh_ref)

        dmerged = _dot(dh_ref[...], wot_ref[...])
        og = og_ref[...]
        xs, rs = _head_norm(og, None)
        on = jnp.concatenate(xs, axis=1)
        gg = gg_ref[...].astype(F32)
        sg = _sigmoid(gg)
        sil = gg * sg
        gh = gh_ref[...]
        ognb = (on * gh * sil).astype(BF16)
        ofb, omb = of_ref[...].T, om_ref[...]
        douts = []
        for idx, (gref, ob, w_ref, wt_ref, dw_ref) in enumerate((
                (g0_ref, ognb, wg_ref, wgt_ref, dwg_ref), (g1_ref, ofb, wf_ref, wft_ref, dwf_ref),
                (g2_ref, omb, wm_ref, wmt_ref, dwm_ref))):
            gt = _sigmoid(gref[...].astype(F32))
            y = _dot(ob, w_ref[...])
            dgt_ref[:, D * idx:D * (idx + 1)] = (dmerged * y * gt * (1.0 - gt)).astype(BF16)
            dy = (gt * dmerged).astype(BF16)
            dw_ref[...] += _dot_tn(ob, dy)
            douts.append(_dot(dy, wt_ref[...]))
        dogn, dof, dom = douts
        dofb = dof.astype(BF16)
        dom_ref[...] = dom.astype(BF16)
        ind = (_iota((1536, 128), 0) % 512 // 64 == _iota((1536, 128), 1)).astype(BF16)
        delta = _dot(_split3(dofb.astype(F32) * ofb.astype(F32)), ind)
        da_ref[...] = (_dot(dofb, sp_ref[...]) + _dot(_split3(delta), dd_ref[...])).astype(BF16)
        dot_ref[...] = dofb.T
        dgg_ref[...] = (dogn * on * gh * (sg * (1.0 + gg * (1.0 - sg)))).astype(BF16)
        d_on = dogn * sil
        dgh_ref[...] += _fold8(d_on * on)
        dxn = d_on * gh
        outs = []
        for h in range(4):
            cols = slice(128 * h, 128 * (h + 1))
            dh_, xh = dxn[:, cols], xs[h]
            outs.append(rs[h] * (dh_ - xh * jnp.mean(dh_ * xh, axis=-1, keepdims=True)))
        dog_ref[...] = jnp.concatenate(outs, axis=1).astype(BF16)

    row = lambda i: (i, 0)
    const = lambda i: (0, 0)
    return pl.pallas_call(
        body, grid=(s // t,),
        in_specs=[pl.BlockSpec((t, D), row), pl.BlockSpec((t, D), lambda i: (i, 0)), pl.BlockSpec((t, D), lambda i: (i, 1)),
                  pl.BlockSpec((t, D), lambda i: (i, 2)), pl.BlockSpec((t, 512), lambda i: (i, C_GG // 512)),
                  pl.BlockSpec((t, 512), row), pl.BlockSpec((512, t), lambda i: (0, i)), pl.BlockSpec((t, 512), row),
                  pl.BlockSpec((1, 512), const), pl.BlockSpec((512, D), const), pl.BlockSpec((512, D), const),
                  pl.BlockSpec((512, D), const), pl.BlockSpec((D, 512), const), pl.BlockSpec((D, 512), const),
                  pl.BlockSpec((D, 512), const), pl.BlockSpec((D, D), const),
                  pl.BlockSpec((512, 1024), const), pl.BlockSpec((384, 1024), const)],
        out_specs=[pl.BlockSpec((t, 3 * D), row), pl.BlockSpec((t, 512), row), pl.BlockSpec((t, 512), row),
                   pl.BlockSpec((t, 1024), row), pl.BlockSpec((512, t), lambda i: (0, i)), pl.BlockSpec((t, 512), row),
                   pl.BlockSpec((512, D), const), pl.BlockSpec((512, D), const), pl.BlockSpec((512, D), const),
                   pl.BlockSpec((8, 512), const)],
        out_shape=[jax.ShapeDtypeStruct((s, 3 * D), BF16), jax.ShapeDtypeStruct((s, 512), BF16),
                   jax.ShapeDtypeStruct((s, 512), BF16), jax.ShapeDtypeStruct((s, 1024), BF16),
                   jax.ShapeDtypeStruct((512, s), BF16), jax.ShapeDtypeStruct((s, 512), BF16),
                   jax.ShapeDtypeStruct((512, D), F32), jax.ShapeDtypeStruct((512, D), F32),
                   jax.ShapeDtypeStruct((512, D), F32), jax.ShapeDtypeStruct((8, 512), F32)],
        name="merge_bwd", compiler_params=_cp("arbitrary"))(
            dh1b, pm, pm, pm, pm, o_gla, o_fox_t, o_mem, g_head, wg, wf, wm, wgt, wft, wmt, wot, spread, d_to_do)


def _ff2_loss(a, w2, h1, g_final, target):
    s, k = a.shape
    tm = min(s, 256)

    def body(a_ref, w_ref, h1_ref, g_ref, t_ref, dh_ref, dhb_ref, loss_ref, dg_ref):
        @pl.when(pl.program_id(0) == 0)
        def _():
            loss_ref[...] = jnp.zeros_like(loss_ref)
            dg_ref[...] = jnp.zeros_like(dg_ref)

        h2 = h1_ref[...] + _dot(_relu2_bf16(a_ref[...]), w_ref[...])
        r = lax.rsqrt(jnp.mean(h2 * h2, axis=-1, keepdims=True) + EPS)
        xn = h2 * r
        g = g_ref[...]
        err = xn * g - t_ref[...]
        e2 = _fold8(err * err)
        part = e2[:, 0:128]
        for c in range(1, D // 128):
            part = part + e2[:, 128 * c:128 * (c + 1)]
        loss_ref[...] += part
        dy = err * (1.0 / D)
        dg_ref[...] += _fold8(dy * xn)
        dxn = dy * g
        dh = r * (dxn - xn * jnp.mean(dxn * xn, axis=-1, keepdims=True))
        dh_ref[...] = dh
        dhb_ref[...] = dh.astype(BF16)

    row = lambda i: (i, 0)
    const = lambda i: (0, 0)
    return pl.pallas_call(
        body, grid=(s // tm,),
        in_specs=[pl.BlockSpec((tm, k), row), pl.BlockSpec((k, D), const, pipeline_mode=pl.Buffered(1)),
                  pl.BlockSpec((tm, D), row), pl.BlockSpec((1, D), const), pl.BlockSpec((tm, D), row)],
        out_specs=[pl.BlockSpec((tm, D), row), pl.BlockSpec((tm, D), row), pl.BlockSpec((8, 128), const),
                   pl.BlockSpec((8, D), const)],
        out_shape=[jax.ShapeDtypeStruct((s, D), F32), jax.ShapeDtypeStruct((s, D), BF16),
                   jax.ShapeDtypeStruct((8, 128), F32), jax.ShapeDtypeStruct((8, D), F32)],
        name="ff2_loss", compiler_params=_cp("arbitrary"))(a, w2, h1, g_final, target)


def _adam(w, g, m, v, name):
    _, r, c = w.shape
    tr = r
    for cand in (512, 256, 128, 64, 32, 16, 8):
        if r % cand == 0 and cand * c * 4 <= (1 << 20):
            tr = cand
            break
    c1 = 1.0 - ADAM_B1 ** ADAM_STEP
    c2 = 1.0 - ADAM_B2 ** ADAM_STEP

    def body(w_ref, g_ref, m_ref, v_ref, d_ref, nm_ref, nv_ref):
        gv = g_ref[...]
        nm = ADAM_B1 * m_ref[...] + (1.0 - ADAM_B1) * gv
        nv = ADAM_B2 * v_ref[...] + (1.0 - ADAM_B2) * (gv * gv)
        d_ref[...] = -ADAM_LR * ((nm / c1) / (jnp.sqrt(nv / c2) + ADAM_EPS) + ADAM_WD * w_ref[...])
        nm_ref[...] = nm
        nv_ref[...] = nv

    spec = pl.BlockSpec((1, tr, c), lambda i: (0, i, 0))
    return pl.pallas_call(
        body, grid=(r // tr,), in_specs=[spec] * 4, out_specs=[spec] * 3,
        out_shape=[jax.ShapeDtypeStruct((1, r, c), F32)] * 3, name=name, compiler_params=_cp("parallel"))(w, g, m, v)


def _row_block(r):
    return max(d for d in range(16, 513, 16) if r % d == 0)


def _add_half(core, a, b, name):
    n, r, c = b.shape
    tr = _row_block(r)

    def body(core_ref, a_ref, b_ref, o_ref):
        o_ref[...] = (a_ref[...].astype(F32) + b_ref[...].astype(F32)).astype(BF16)

    spec = pl.BlockSpec((1, tr, c), lambda k, i, core_ref: (k, i, 0))
    half = pl.BlockSpec((1, tr, c), lambda k, i, core_ref: (k, i + core_ref[0] * (r // tr), 0))
    return pl.pallas_call(
        body, grid_spec=pltpu.PrefetchScalarGridSpec(num_scalar_prefetch=1, grid=(n, r // tr), in_specs=[half, spec],
                                                     out_specs=spec),
        out_shape=jax.ShapeDtypeStruct((n, r, c), BF16), name=name, compiler_params=_cp("parallel", "parallel"))(core, a, b)


def _sum4(a, name):
    _, r, c = a.shape
    tr = _row_block(r)

    def body(a_ref, o_ref):
        o_ref[...] = ((a_ref[0].astype(F32) + a_ref[1].astype(F32)) + a_ref[2].astype(F32)) + a_ref[3].astype(F32)

    return pl.pallas_call(body, grid=(r // tr,), in_specs=[pl.BlockSpec((4, tr, c), lambda i: (0, i, 0))],
                          out_specs=pl.BlockSpec((tr, c), lambda i: (i, 0)),
                          out_shape=jax.ShapeDtypeStruct((r, c), F32), name=name, compiler_params=_cp("parallel"))(a)


def _adam_small(w, gathered, m, v):
    c1 = 1.0 - ADAM_B1 ** ADAM_STEP
    c2 = 1.0 - ADAM_B2 ** ADAM_STEP

    def body(w_ref, g_ref, m_ref, v_ref, gs_ref, d_ref, nm_ref, nv_ref):
        gv = g_ref[0]
        for dev in range(1, N_DEV):
            gv = gv + g_ref[dev]
        gs_ref[...] = gv
        nm = ADAM_B1 * m_ref[...] + (1.0 - ADAM_B1) * gv
        nv = ADAM_B2 * v_ref[...] + (1.0 - ADAM_B2) * (gv * gv)
        d_ref[...] = -ADAM_LR * ((nm / c1) / (jnp.sqrt(nv / c2) + ADAM_EPS) + ADAM_WD * w_ref[...])
        nm_ref[...] = nm
        nv_ref[...] = nv

    return pl.pallas_call(body, out_shape=[jax.ShapeDtypeStruct((8, D), F32)] * 4, name="adam_small")(w, gathered, m, v)


def _place():
    return lax.axis_index("x"), lax.axis_index("y"), lax.axis_index("c")


def _other_chips(x, y):
    return [(1 - x, y), (x, 1 - y), (1 - x, 1 - y)]


GATHER_SEMS = [pltpu.SemaphoreType.DMA((6,)), pltpu.SemaphoreType.DMA((6,)), pltpu.SemaphoreType.DMA]


def _gather_ops(in_refs, out_refs, sems):
    (p_ref,), (out_ref,) = in_refs, out_refs
    send_sems, recv_sems, local_sem = sems
    hr = p_ref.shape[0] // 2
    x, y, cc = _place()
    sibling = (x, y, 1 - cc)
    chips = _other_chips(x, y)

    def half(chip, core):
        return out_ref.at[2 * chip[0] + chip[1], pl.ds(core * hr, hr), :]

    def copy(k, chip, core, to, src=None):
        return pltpu.make_async_remote_copy(
            src_ref=half(chip, core) if src is None else src, dst_ref=half(chip, core),
            send_sem=send_sems.at[k], recv_sem=recv_sems.at[k], device_id=to, device_id_type=MESH)

    mine = pltpu.make_async_copy(p_ref, out_ref.at[2 * x + y], local_sem)
    my_half = p_ref.at[pl.ds(cc * hr, hr), :]
    first = [copy(j, (x, y), cc, (*chip, cc), src=my_half) for j, chip in enumerate(chips)]
    passed = [copy(3 + j, chip, cc, sibling) for j, chip in enumerate(chips)]

    def start():
        mine.start()
        for cp in first:
            cp.start()

    def finish():
        for j, chip in enumerate(chips):
            copy(j, chip, cc, (x, y, cc)).wait_recv()
            passed[j].start()
        for j, chip in enumerate(chips):
            copy(3 + j, chip, 1 - cc, (x, y, cc)).wait_recv()
        for cp in first + passed:
            cp.wait_send()
        mine.wait()

    return start, finish


def _gather_side(p):
    return _Side([p], [jax.ShapeDtypeStruct((N_CHIPS,) + p.shape, p.dtype)], GATHER_SEMS, _gather_ops)


def _swap_halves(g):
    n, r, c = g.shape
    hr = r // 2

    def body(g_ref, out_ref, send_sem, recv_sem):
        x, y, cc = _place()
        cp = pltpu.make_async_remote_copy(
            src_ref=g_ref.at[:, pl.ds((1 - cc) * hr, hr), :], dst_ref=out_ref,
            send_sem=send_sem, recv_sem=recv_sem, device_id=(x, y, 1 - cc), device_id_type=MESH)
        cp.start()
        cp.wait()

    any_spec = pl.BlockSpec(memory_space=pl.ANY)
    return pl.pallas_call(
        body, out_shape=jax.ShapeDtypeStruct((n, hr, c), g.dtype), in_specs=[any_spec], out_specs=any_spec,
        scratch_shapes=[pltpu.SemaphoreType.DMA, pltpu.SemaphoreType.DMA], name="swap_halves")(g)


SCATTER_SEMS = [pltpu.SemaphoreType.DMA((7,)), pltpu.SemaphoreType.DMA((7,)), pltpu.SemaphoreType.DMA]


def _scatter_ops(in_refs, out_refs, sems):
    (p_ref,), (out_ref,) = in_refs, out_refs
    send_sems, recv_sems, local_sem = sems
    hr = p_ref.shape[1]
    x, y, cc = _place()
    me = 2 * x + y
    sibling = (x, y, 1 - cc)
    chips = _other_chips(x, y)
    ids = [2 * chip[0] + chip[1] for chip in chips]

    def land(src, core):
        return out_ref.at[src, pl.ds(core * hr, hr), :]

    def copy(k, src_ref, dst_ref, to):
        return pltpu.make_async_remote_copy(src_ref=src_ref, dst_ref=dst_ref, send_sem=send_sems.at[k],
                                            recv_sem=recv_sems.at[k], device_id=to, device_id_type=MESH)

    mine = pltpu.make_async_copy(p_ref.at[me], land(me, cc), local_sem)
    sends = [copy(j, p_ref.at[ids[j]], land(me, cc), (*chip, cc)) for j, chip in enumerate(chips)]
    sends.append(copy(3, p_ref.at[me], land(me, cc), sibling))
    passed = [copy(4 + j, land(ids[j], cc), land(ids[j], cc), sibling) for j in range(3)]

    def start():
        mine.start()
        for cp in sends:
            cp.start()

    def finish():
        for j in range(3):
            copy(j, p_ref.at[me], land(ids[j], cc), (x, y, cc)).wait_recv()
            passed[j].start()
        copy(3, p_ref.at[me], land(me, 1 - cc), (x, y, cc)).wait_recv()
        for j in range(3):
            copy(4 + j, p_ref.at[me], land(ids[j], 1 - cc), (x, y, cc)).wait_recv()
        for cp in sends + passed:
            cp.wait_send()
        mine.wait()

    return start, finish


def _scatter_side(p):
    n, hr, c = p.shape
    return _Side([p], [jax.ShapeDtypeStruct((n, 2 * hr, c), p.dtype)], SCATTER_SEMS, _scatter_ops)


def _gather_small(blk):
    m, n = blk.shape

    def body(x_ref, out_ref, send_sems, recv_sems, local_sem):
        x, y, cc = _place()
        me, sibling = (x, y, cc), (x, y, 1 - cc)
        chips = _other_chips(x, y)

        def slot(px, py, pc):
            return out_ref.at[4 * px + 2 * py + pc]

        def copy(k, block, to, src=None):
            return pltpu.make_async_remote_copy(
                src_ref=slot(*block) if src is None else src, dst_ref=slot(*block),
                send_sem=send_sems.at[k], recv_sem=recv_sems.at[k], device_id=to, device_id_type=MESH)

        mine = pltpu.make_async_copy(x_ref, slot(*me), local_sem)
        mine.start()
        first = [copy(0, me, sibling, src=x_ref)]
        first += [copy(1 + j, me, (*chip, cc), src=x_ref) for j, chip in enumerate(chips)]
        for cp in first:
            cp.start()
        passed = [copy(4 + j, (*chip, cc), sibling) for j, chip in enumerate(chips)]
        for j, chip in enumerate(chips):
            copy(1 + j, (*chip, cc), me).wait_recv()
            passed[j].start()
        copy(0, sibling, me).wait_recv()
        for j, chip in enumerate(chips):
            copy(4 + j, (*chip, 1 - cc), me).wait_recv()
        for cp in first + passed:
            cp.wait_send()
        mine.wait()

    vmem = pl.BlockSpec(memory_space=pltpu.VMEM)
    return pl.pallas_call(
        body, out_shape=jax.ShapeDtypeStruct((N_DEV, m, n), blk.dtype), in_specs=[vmem], out_specs=vmem,
        scratch_shapes=[pltpu.SemaphoreType.DMA((7,)), pltpu.SemaphoreType.DMA((7,)), pltpu.SemaphoreType.DMA],
        name="gather_small")(blk)


def _pack_a(sh, dtype):
    w = sh["w_in"].astype(dtype)
    return jnp.concatenate([w[:, 0:PACK_W], jnp.pad(w[:, PACK_W:], ((0, 0), (0, 2 * PACK_W - w.shape[1])))], axis=0)


def _pack_b(sh, dtype):
    o3 = jnp.concatenate([sh["w_gla_o"], sh["w_fox_o"], sh["w_mem_o"], jnp.zeros((512, 256), sh["w_gla_o"].dtype)], axis=1)
    au = jnp.pad(sh["w_alpha_up"], ((0, PACK_ROWS_B - 3072 - 16), (0, PACK_W - 64)))
    return jnp.concatenate([sh["w_ff1"], sh["w_ff2"], sh["w_mem_kv"], sh["w_out"], o3, au], axis=0).astype(dtype)


def _unpack_a(pa):
    return {"w_in": jnp.concatenate([pa[0:1024], pa[1024:2048, 0:1670 - PACK_W]], axis=1)}


def _unpack_b(pb):
    return {"w_ff1": pb[0:1024], "w_ff2": pb[1024:2048], "w_mem_kv": pb[2048:2304], "w_out": pb[2304:2560],
            "w_gla_o": pb[2560:3072, 0:256], "w_fox_o": pb[2560:3072, 256:512], "w_mem_o": pb[2560:3072, 512:768],
            "w_alpha_up": pb[3072:3088, 0:64]}


def _unpack(packed):
    return {**_unpack_a(packed[0:PACK_ROWS_A]), **_unpack_b(packed[PACK_ROWS_A:])}


def _split_shards(name, full):
    return jnp.split(full, N_CHIPS, axis=SHARD_AXIS[name])


def _pack_small(vals, scalar=None):
    row4 = jnp.concatenate([vals["b_alpha"].reshape(-1), vals["b_forget"].reshape(-1), jnp.zeros((D - 264,), F32)])
    row5 = jnp.concatenate([vals["g_gla_head"].reshape(-1), jnp.zeros((D - 512,), F32)])
    row6 = jnp.zeros((D,), F32) if scalar is None else jnp.broadcast_to(scalar, (D,))
    rows = [vals["g_mix"].reshape(-1), vals["g_mem"].reshape(-1), vals["g_ffn"].reshape(-1), vals["g_final"].reshape(-1),
            row4, row5, row6, jnp.zeros((D,), F32)]
    return jnp.stack(rows)


def _unpack_small(blk):
    return {"g_mix": blk[0].reshape(1, D), "g_mem": blk[1].reshape(1, D), "g_ffn": blk[2].reshape(1, D),
            "g_final": blk[3].reshape(D), "b_alpha": blk[4, 0:256].reshape(1, 256), "b_forget": blk[4, 256:264].reshape(1, 8),
            "g_gla_head": blk[5, 0:512].reshape(1, 4, 128)}


def _local_step(x, mem, target, wb, small, exchange=None):
    s = x.shape[0]
    nm = mem.shape[0]
    t = _row_tile(s)
    nb = s // t
    b_alpha = small["b_alpha"].reshape(1, 256)
    bias_e = jnp.concatenate([jnp.zeros((FF_LANE,), F32), small["b_forget"].reshape(-1),
                              jnp.zeros((PE_W - FF_LANE - 8,), F32)]).reshape(1, PE_W)
    g_mix, g_mem, g_ffn = small["g_mix"].reshape(1, D), small["g_mem"].reshape(1, D), small["g_ffn"].reshape(1, D)
    g_final = small["g_final"].reshape(1, D)
    g_head = small["g_gla_head"].reshape(1, 512)

    if exchange is None:
        u, r1 = _rms_fwd(x, g_mix, "norm_mix")
    else:
        u, r1, gathered = _rms_fwd(x, g_mix, "norm_mix", side=exchange.gather_a)
        wb = exchange.weights_a(gathered)
    w_in = wb["w_in"]
    w_main = jnp.concatenate([w_in[:, 3608:6680], w_in[:, 0:1536], w_in[:, 1552:3088], w_in[:, 3096:3608]], axis=1)
    w_e = jnp.concatenate([w_in[:, 1536:1552], w_in[:, 3088:3096], jnp.zeros((D, PE_W - 24), BF16)], axis=1)
    w_in_pt = _transpose(jnp.concatenate([w_main, w_e], axis=1), "t_w_in")
    big = min(s, 1024)
    if exchange is None:
        pm, pe = _proj(u, w_main, w_e)
    else:
        pm, pe, gathered = _proj(u, w_main, w_e, side=exchange.gather_b)
        wb = {**wb, **exchange.weights_b(gathered)}
    wau_p = jnp.concatenate([wb["w_alpha_up"], jnp.zeros((PE_W - 16, 256), BF16)], axis=0)
    o_gla, states = _gla_fwd(pm, pe, wau_p, b_alpha)
    fcum = _fcum_fwd(pe, bias_e)
    tb = _fox_tables()
    qf_aug, k_aug, v_aug, vt, qt, kt = _fox_prep(pm, fcum, None, tb, backward=False)
    o_fox, lse = _fox_fwd(k_aug, qf_aug, vt)
    mn, rm = _rms_fwd(mem, g_mem, "norm_mem")
    mkv = _mm_nn(mn, wb["w_mem_kv"], out_dtype=BF16, tm=nm, tn=512, tk=D, name="mem_kv")
    o_mem = _mem_attn_fwd(pm, mkv)
    merged, h1, u2, r2 = _merge_fwd(x, pm, o_gla, o_fox, o_mem, g_head, wb["w_gla_o"], wb["w_fox_o"], wb["w_mem_o"],
                                    wb["w_out"], g_ffn)
    a = _mm_nn(u2, wb["w_ff1"], out_dtype=BF16, tm=big, tn=1024, tk=D, name="ff1")
    dh2, dh2b, loss8, dgfin8 = _ff2_loss(a, wb["w_ff2"], h1, g_final, target)
    loss = 0.5 * jnp.sum(loss8) / D

    da = _mm_nn(dh2b, _transpose(wb["w_ff2"], "t_w_ff2"), out_dtype=BF16, tm=big, tn=1024, tk=D, name="d_act",
                epi=lambda acc, at: acc * (2.0 * jnp.maximum(at.astype(F32), 0.0)), extra=a)
    gw = {}
    gw["w_ff2"] = _mm_tn(a, dh2b, tm=1024, tn=D, ts=big, name="dw_ff2", a_fn=_relu2_bf16)
    gw["w_ff1"] = _mm_tn(u2, da, tm=D, tn=1024, ts=big, name="dw_ff1")
    dh1, dh1b, dgffn8 = _mm_norm_bwd([da], _transpose(wb["w_ff1"], "t_w_ff1"), h1, r2, g_ffn, dh2, name="d_h1", want_bf16=True)
    gw["w_out"] = _mm_tn(merged, dh1b, tm=D, tn=D, ts=big, name="dw_out")
    (dgates, dgg, do_gla, do_aug, do_t, do_mem, gw["w_gla_o"], gw["w_fox_o"], gw["w_mem_o"], dgh8) = _merge_bwd(
        dh1b, pm, o_gla, o_fox, o_mem, g_head, wb["w_gla_o"], wb["w_fox_o"], wb["w_mem_o"],
        *[_transpose(wb[n], "t_" + n) for n in ("w_gla_o", "w_fox_o", "w_mem_o", "w_out")], tb["spread"], tb["d_to_do"])
    dgq, dgk, dgv, de_gla, dwau_p, dba8 = _gla_bwd(pm, pe, wau_p, wau_p.T, b_alpha, do_gla, states)
    gw["w_alpha_up"] = dwau_p[0:16, :]
    q_aug = _fox_prep(pm, fcum, lse.reshape(8, s), tb, backward=True)
    dfq_t, dfrow, dfk_t, dfv_t, dfcol = _fox_bwd(q_aug, do_aug, qt, do_t, k_aug, v_aug, kt)
    dfq, dfk, dfv, df = _fox_post(dfq_t, dfk_t, dfv_t, dfrow[:, 0, :], dfcol.reshape(8, s), tb)
    de_fox, dbf8 = _fcum_bwd(pe, bias_e, df)
    dmq, dmk, dmv = _mem_attn_bwd(pm, mkv, do_mem)
    dmkv = jnp.concatenate([dmk, dmv], axis=1).astype(BF16)
    gw["w_mem_kv"] = _mm_tn(mn, dmkv, tm=D, tn=D, ts=nm, name="dw_mem_kv")
    dmn_g = _mm_nn(dmkv, _transpose(wb["w_mem_kv"], "t_w_mem_kv"), out_dtype=F32, tm=nm, tn=D, tk=D, name="d_mem_norm")
    dgmem8 = _gain_grad(dmn_g, mem, rm, "dg_mem")
    de = (de_gla + de_fox).astype(BF16)
    dproj = [dgates, dgq, dgk, dgv, dgg, dfq, dfk, dfv, dmq, de]
    dw_gates = _mm_tn(u, dgates, tm=D, tn=1024, ts=big, name="dw_in_gates")
    dw_g = _mm_tn_cat(u, [dgq, dgk, dgv], ts=big, name="dw_in_gla")
    dw_gf = _mm_tn_cat(u, [dgg, dfq], ts=big, name="dw_in_gg_fq")
    dw_f = _mm_tn_cat(u, [dfk, dfv], ts=big, name="dw_in_fk_fv")
    dw_m = _mm_tn_cat(u, [dmq, de], ts=big, name="dw_in_mq_narrow")
    gw["w_in"] = jnp.concatenate([dw_g, dw_gf[:, 0:512], dw_m[:, 512:528], dw_gf[:, 512:1024], dw_f,
                                  dw_m[:, 528:536], dw_m[:, 0:512], dw_gates], axis=1)
    if exchange is None:
        grad_x, dgmix8 = _mm_norm_bwd(dproj, w_in_pt, x, r1, g_mix, dh1, name="d_x", want_bf16=False)
        exchanged = None
    else:
        grad_x, dgmix8, exchanged = _mm_norm_bwd(dproj, w_in_pt, x, r1, g_mix, dh1, name="d_x", want_bf16=False,
                                                 side=exchange.scatter(gw))
    gs = {"g_mix": dgmix8.sum(0), "g_mem": dgmem8.sum(0), "g_ffn": dgffn8.sum(0), "g_final": dgfin8.sum(0),
          "b_alpha": dba8.sum(0), "b_forget": dbf8.sum(0)[FF_LANE:FF_LANE + 8], "g_gla_head": dgh8.sum(0)}
    return loss, grad_x, gw, gs, exchanged


def kernel(x, mem, g_mix, w_in, w_alpha_up, b_alpha, b_forget, g_gla_head, g_mem, w_mem_kv, w_gla_o, w_fox_o, w_mem_o, w_out, g_ffn, w_ff1, w_ff2, g_final, loss_target, m_g_mix, m_w_in, m_w_alpha_up, m_b_alpha, m_b_forget, m_g_gla_head, m_g_mem, m_w_mem_kv, m_w_gla_o, m_w_fox_o, m_w_mem_o, m_w_out, m_g_ffn, m_w_ff1, m_w_ff2, m_g_final, v_g_mix, v_w_in, v_w_alpha_up, v_b_alpha, v_b_forget, v_g_gla_head, v_g_mem, v_w_mem_kv, v_w_gla_o, v_w_fox_o, v_w_mem_o, v_w_out, v_g_ffn, v_w_ff1, v_w_ff2, v_g_final):
    args = dict(locals())
    w_sh = {n: args[n][0] for n in WEIGHTS}
    small = {n: args[n] for n in SMALL}

    def whole(parts):
        return {n: jnp.concatenate([p[n] for p in parts], axis=SHARD_AXIS[n]) for n in parts[0]}

    class Exchange:
        gather_a = _gather_side(_pack_a(w_sh, BF16))
        gather_b = _gather_side(_pack_b(w_sh, BF16))

        @staticmethod
        def weights_a(gathered):
            return whole([_unpack_a(gathered[k]) for k in range(N_CHIPS)])

        @staticmethod
        def weights_b(gathered):
            return whole([_unpack_b(gathered[k]) for k in range(N_CHIPS)])

        @staticmethod
        def scatter(gw):
            by_chip = {n: _split_shards(n, gw[n]) for n in WEIGHTS}
            packed = jnp.stack([jnp.concatenate([_pack_a({n: by_chip[n][k] for n in WEIGHTS}, BF16),
                                                 _pack_b({n: by_chip[n][k] for n in WEIGHTS}, BF16)], axis=0)
                                for k in range(N_CHIPS)])
            core = lax.axis_index("c").astype(jnp.int32).reshape(1)
            return _scatter_side(_add_half(core, packed, _swap_halves(packed), "chip_sum"))

    loss, grad_x, gw, gs, by_chip = _local_step(x[0], mem[0], loss_target[0], None, small, Exchange)
    g_out = {n: g[None] for n, g in _unpack(_sum4(by_chip, "shard_sum")).items()}
    d_out, m_out, v_out = {}, {}, {}
    for n in WEIGHTS:
        d_out[n], m_out[n], v_out[n] = _adam(args[n], g_out[n], args["m_" + n], args["v_" + n], "adam_" + n)

    small_all = _gather_small(_pack_small(gs, loss))
    sm = {n: args["m_" + n] for n in SMALL}
    sv = {n: args["v_" + n] for n in SMALL}
    gs_sum, sd, snm, snv = _adam_small(_pack_small(small), small_all, _pack_small(sm), _pack_small(sv))
    gs_o, sd_o, snm_o, snv_o = _unpack_small(gs_sum), _unpack_small(sd), _unpack_small(snm), _unpack_small(snv)

    names = ["g_mix", "w_in", "w_alpha_up", "b_alpha", "b_forget", "g_gla_head", "g_mem", "w_mem_kv", "w_gla_o", "w_fox_o",
             "w_mem_o", "w_out", "g_ffn", "w_ff1", "w_ff2", "g_final"]

    def pick(big, sml, n):
        return big[n] if n in big else sml[n]

    outs = [gs_sum[6, 0], grad_x[None]]
    for big, sml in ((g_out, gs_o), (d_out, sd_o), (m_out, snm_o), (v_out, snv_o)):
        outs += [pick(big, sml, n) for n in names]
    return tuple(outs)
```

```python
import functools

import numpy as np
import jax
import jax.numpy as jnp
from jax import lax
from jax.experimental import pallas as pl
from jax.experimental.pallas import tpu as pltpu

F32 = jnp.float32
BF16 = jnp.bfloat16
HI = lax.Precision.HIGHEST
MESH = pl.DeviceIdType.MESH

EPS = 1e-6
D = 1024
CHUNK = 64
GLA_TAU = 16.0
N_CHIPS = 4
N_DEV = 8
VMEM_LIMIT_BYTES = 56 * 1024 * 1024

ADAM_LR, ADAM_B1, ADAM_B2, ADAM_EPS, ADAM_WD, ADAM_STEP = 0.001, 0.9, 0.999, 1e-08, 0.01, 10

PM_W = 6656
PE_W = 128
C_GQ, C_GK, C_GV, C_GG, C_FQ, C_FK, C_FV, C_MQ = 3072, 3328, 3584, 4096, 4608, 5120, 5632, 6144
FF_LANE = 16

WEIGHTS = ("w_in", "w_alpha_up", "w_mem_kv", "w_gla_o", "w_fox_o", "w_mem_o", "w_out", "w_ff1", "w_ff2")
SHARD_AXIS = {"w_in": 1, "w_alpha_up": 1, "w_mem_kv": 0, "w_gla_o": 1, "w_fox_o": 1, "w_mem_o": 1, "w_out": 0,
              "w_ff1": 1, "w_ff2": 0}
SMALL = ("g_mix", "g_mem", "g_ffn", "g_final", "b_alpha", "b_forget", "g_gla_head")
PACK_W = 1024
PACK_ROWS_A = 2048
PACK_ROWS_B = 3104
PACK_ROWS = PACK_ROWS_A + PACK_ROWS_B


def _cp(*sem):
    return pltpu.CompilerParams(dimension_semantics=sem, vmem_limit_bytes=VMEM_LIMIT_BYTES)


def _dot(a, b, **kw):
    return jnp.dot(a, b, preferred_element_type=F32, **kw)


def _dot_nt(a, b, **kw):
    return lax.dot_general(a, b, (((1,), (1,)), ((), ())), preferred_element_type=F32, **kw)


def _dot_tn(a, b, **kw):
    return lax.dot_general(a, b, (((0,), (0,)), ((), ())), preferred_element_type=F32, **kw)


def _sigmoid(x):
    return 0.5 * jnp.tanh(0.5 * x) + 0.5


def _log_sigmoid(x):
    return -(jnp.maximum(-x, 0.0) + jnp.log1p(jnp.exp(-jnp.abs(x))))


def _fold8(x):
    m, n = x.shape
    return x.reshape(m // 8, 8, n).sum(axis=0)


def _iota(shape, dim):
    return lax.broadcasted_iota(jnp.int32, shape, dim)


def _row_tile(s):
    return min(s, 512)


class _Side:
    def __init__(self, inputs, out_shape, scratch, ops):
        self.inputs, self.out_shape, self.scratch, self.ops = list(inputs), list(out_shape), list(scratch), ops


ANY_SPEC = pl.BlockSpec(memory_space=pl.ANY)


def _mm_nn(a, b, *, out_dtype, tm, tn, tk, name, a_fn=None, epi=None, extra=None):
    m, k = a.shape
    _, n = b.shape
    nk = k // tk

    def body_one(*refs):
        a_ref, b_ref = refs[0], refs[1]
        at = a_ref[...] if a_fn is None else a_fn(a_ref[...])
        r = _dot(at, b_ref[...])
        if epi is not None:
            r = epi(r, None if extra is None else refs[2][...])
        refs[-1][...] = r.astype(out_dtype)

    if nk == 1:
        in_specs = [pl.BlockSpec((tm, k), lambda i, j: (i, 0)), pl.BlockSpec((k, tn), lambda i, j: (0, j))]
        args = [a, b]
        if extra is not None:
            in_specs.append(pl.BlockSpec((tm, tn), lambda i, j: (i, j)))
            args.append(extra)
        return pl.pallas_call(
            body_one, grid=(m // tm, n // tn), in_specs=in_specs, out_specs=pl.BlockSpec((tm, tn), lambda i, j: (i, j)),
            out_shape=jax.ShapeDtypeStruct((m, n), out_dtype), name=name, compiler_params=_cp("parallel", "parallel"))(*args)

    def body(*refs):
        if extra is None:
            a_ref, b_ref, o_ref, acc = refs
            x_ref = None
        else:
            a_ref, b_ref, x_ref, o_ref, acc = refs
        kk = pl.program_id(2)

        @pl.when(kk == 0)
        def _():
            acc[...] = jnp.zeros_like(acc)

        at = a_ref[...]
        if a_fn is not None:
            at = a_fn(at)
        acc[...] += _dot(at, b_ref[...])

        @pl.when(kk == nk - 1)
        def _():
            r = acc[...]
            if epi is not None:
                r = epi(r, None if x_ref is None else x_ref[...])
            o_ref[...] = r.astype(out_dtype)

    in_specs = [pl.BlockSpec((tm, tk), lambda i, j, kk: (i, kk)), pl.BlockSpec((tk, tn), lambda i, j, kk: (kk, j))]
    args = [a, b]
    if extra is not None:
        in_specs.append(pl.BlockSpec((tm, tn), lambda i, j, kk: (i, j)))
        args.append(extra)
    return pl.pallas_call(
        body, grid=(m // tm, n // tn, nk), in_specs=in_specs,
        out_specs=pl.BlockSpec((tm, tn), lambda i, j, kk: (i, j)),
        out_shape=jax.ShapeDtypeStruct((m, n), out_dtype),
        scratch_shapes=[pltpu.VMEM((tm, tn), F32)], name=name,
        compiler_params=_cp("parallel", "parallel", "arbitrary"))(*args)


def _mm_tn(a, b, *, tm, tn, ts, name, a_fn=None):
    s, m = a.shape
    _, n = b.shape
    ns = s // ts

    def body(a_ref, b_ref, o_ref, acc):
        kk = pl.program_id(2)

        @pl.when(kk == 0)
        def _():
            acc[...] = jnp.zeros_like(acc)

        at = a_ref[...]
        if a_fn is not None:
            at = a_fn(at)
        acc[...] += _dot_tn(at, b_ref[...])

        @pl.when(kk == ns - 1)
        def _():
            o_ref[...] = acc[...]

    return pl.pallas_call(
        body, grid=(m // tm, n // tn, ns),
        in_specs=[pl.BlockSpec((ts, tm), lambda i, j, kk: (kk, i)), pl.BlockSpec((ts, tn), lambda i, j, kk: (kk, j))],
        out_specs=pl.BlockSpec((tm, tn), lambda i, j, kk: (i, j)),
        out_shape=jax.ShapeDtypeStruct((m, n), F32),
        scratch_shapes=[pltpu.VMEM((tm, tn), F32)], name=name,
        compiler_params=_cp("parallel", "parallel", "arbitrary"))(a, b)


def _mm_tn_cat(a, bs, *, ts, name):
    s, m = a.shape
    n = sum(b.shape[1] for b in bs)
    ns = s // ts
    nb = len(bs)

    def body(*refs):
        a_ref, b_refs, o_ref, acc = refs[0], refs[1:1 + nb], refs[1 + nb], refs[2 + nb]
        kk = pl.program_id(0)

        @pl.when(kk == 0)
        def _():
            acc[...] = jnp.zeros_like(acc)

        bt = b_refs[0][...] if nb == 1 else jnp.concatenate([r[...] for r in b_refs], axis=1)
        acc[...] += _dot_tn(a_ref[...], bt)

        @pl.when(kk == ns - 1)
        def _():
            o_ref[...] = acc[...]

    return pl.pallas_call(
        body, grid=(ns,),
        in_specs=[pl.BlockSpec((ts, m), lambda kk: (kk, 0))] + [pl.BlockSpec((ts, b.shape[1]), lambda kk: (kk, 0)) for b in bs],
        out_specs=pl.BlockSpec((m, n), lambda kk: (0, 0)), out_shape=jax.ShapeDtypeStruct((m, n), F32),
        scratch_shapes=[pltpu.VMEM((m, n), F32)], name=name, compiler_params=_cp("arbitrary"))(a, *bs)


def _proj(u, w_main, w_e, side=None):
    s, k = u.shape
    n = w_main.shape[1]
    tm, tn = min(s, 1024), n // 4
    n_sin = 0 if side is None else len(side.inputs)
    n_sout = 0 if side is None else len(side.out_shape)

    def body(u_ref, w_ref, we_ref, *rest):
        pm_ref, pe_ref = rest[n_sin:n_sin + 2]
        i, j = pl.program_id(0), pl.program_id(1)
        if side is not None:
            start, finish = side.ops(rest[:n_sin], rest[n_sin + 2:n_sin + 2 + n_sout], rest[n_sin + 2 + n_sout:])
            pl.when((i == 0) & (j == 0))(start)
        ut = u_ref[...]
        pm_ref[...] = _dot(ut, w_ref[...]).astype(BF16)

        @pl.when(j == 0)
        def _():
            pe_ref[...] = _dot(ut, we_ref[...])

        if side is not None:
            pl.when((i == s // tm - 1) & (j == n // tn - 1))(finish)

    side_in = [] if side is None else side.inputs
    return pl.pallas_call(
        body, grid=(s // tm, n // tn),
        in_specs=[pl.BlockSpec((tm, k), lambda i, j: (i, 0)), pl.BlockSpec((k, tn), lambda i, j: (0, j)),
                  pl.BlockSpec((k, PE_W), lambda i, j: (0, 0))] + [ANY_SPEC] * n_sin,
        out_specs=[pl.BlockSpec((tm, tn), lambda i, j: (i, j)), pl.BlockSpec((tm, PE_W), lambda i, j: (i, 0))]
        + [ANY_SPEC] * n_sout,
        out_shape=[jax.ShapeDtypeStruct((s, n), BF16), jax.ShapeDtypeStruct((s, PE_W), F32)]
        + ([] if side is None else side.out_shape),
        scratch_shapes=[] if side is None else side.scratch,
        name="proj_main", compiler_params=_cp("arbitrary", "arbitrary"))(u, w_main, w_e, *side_in)


def _transpose(w, name):
    r, c = w.shape
    tr = min(r, 256)

    def body(w_ref, o_ref):
        o_ref[...] = w_ref[...].T

    return pl.pallas_call(body, grid=(r // tr,), in_specs=[pl.BlockSpec((tr, c), lambda i: (i, 0))],
                          out_specs=pl.BlockSpec((c, tr), lambda i: (0, i)),
                          out_shape=jax.ShapeDtypeStruct((c, r), w.dtype), name=name, compiler_params=_cp("parallel"))(w)


def _relu2_bf16(t):
    r = jnp.maximum(t.astype(F32), 0.0)
    return (r * r).astype(BF16)


def _rms_fwd(x, g, name, side=None):
    s, d = x.shape
    tm = min(s, 512)
    n_sin = 0 if side is None else len(side.inputs)
    n_sout = 0 if side is None else len(side.out_shape)

    def body(x_ref, g_ref, *rest):
        u_ref, r_ref = rest[n_sin:n_sin + 2]
        if side is not None:
            start, finish = side.ops(rest[:n_sin], rest[n_sin + 2:n_sin + 2 + n_sout], rest[n_sin + 2 + n_sout:])
            pl.when(pl.program_id(0) == 0)(start)
        xv = x_ref[...]
        r = lax.rsqrt(jnp.mean(xv * xv, axis=-1, keepdims=True) + EPS)
        u_ref[...] = ((xv * r) * g_ref[...]).astype(BF16)
        r_ref[...] = r
        if side is not None:
            pl.when(pl.program_id(0) == s // tm - 1)(finish)

    side_in = [] if side is None else side.inputs
    return pl.pallas_call(
        body, grid=(s // tm,),
        in_specs=[pl.BlockSpec((tm, d), lambda i: (i, 0)), pl.BlockSpec((1, d), lambda i: (0, 0))] + [ANY_SPEC] * n_sin,
        out_specs=[pl.BlockSpec((tm, d), lambda i: (i, 0)), pl.BlockSpec((tm, 1), lambda i: (i, 0))] + [ANY_SPEC] * n_sout,
        out_shape=[jax.ShapeDtypeStruct((s, d), BF16), jax.ShapeDtypeStruct((s, 1), F32)]
        + ([] if side is None else side.out_shape),
        scratch_shapes=[] if side is None else side.scratch,
        name=name, compiler_params=_cp("parallel" if side is None else "arbitrary"))(x, g, *side_in)


def _mm_norm_bwd(a_parts, b, xin, r, g, dres, *, name, want_bf16, side=None):
    s = a_parts[0].shape[0]
    k = b.shape[0]
    na = len(a_parts)
    offs = [sum(p.shape[1] for p in a_parts[:i]) for i in range(na)]
    assert offs[-1] + a_parts[-1].shape[1] == k
    tm = min(s, 512 if k <= 4096 else 256)
    n_out = 3 if want_bf16 else 2
    n_sin = 0 if side is None else len(side.inputs)
    n_sout = 0 if side is None else len(side.out_shape)

    def body(*refs):
        a_refs = refs[:na]
        b_ref, x_ref, r_ref, g_ref, dres_ref = refs[na:na + 5]
        rest = refs[na + 5:]
        outs = rest[n_sin:n_sin + n_out]
        dx_ref, dg_ref = outs[0], outs[-1]
        if side is not None:
            start, finish = side.ops(rest[:n_sin], rest[n_sin + n_out:n_sin + n_out + n_sout], rest[n_sin + n_out + n_sout:])
            pl.when(pl.program_id(0) == 0)(start)

        @pl.when(pl.program_id(0) == 0)
        def _():
            dg_ref[...] = jnp.zeros_like(dg_ref)

        du = _dot(a_refs[0][...], b_ref[0:a_parts[0].shape[1], :])
        for a_ref, off, part in zip(a_refs[1:], offs[1:], a_parts[1:]):
            du = du + _dot(a_ref[...], b_ref[off:off + part.shape[1], :])
        xn = x_ref[...] * r_ref[...]
        dg_ref[...] += _fold8(du * xn)
        dxn = du * g_ref[...]
        dx = dres_ref[...] + r_ref[...] * (dxn - xn * jnp.mean(dxn * xn, axis=-1, keepdims=True))
        dx_ref[...] = dx
        if want_bf16:
            outs[1][...] = dx.astype(BF16)
        if side is not None:
            pl.when(pl.program_id(0) == s // tm - 1)(finish)

    row = lambda i: (i, 0)
    const = lambda i: (0, 0)
    out_specs = [pl.BlockSpec((tm, D), row)]
    out_shape = [jax.ShapeDtypeStruct((s, D), F32)]
    if want_bf16:
        out_specs.append(pl.BlockSpec((tm, D), row))
        out_shape.append(jax.ShapeDtypeStruct((s, D), BF16))
    out_specs.append(pl.BlockSpec((8, D), const))
    out_shape.append(jax.ShapeDtypeStruct((8, D), F32))
    side_in = [] if side is None else side.inputs
    return pl.pallas_call(
        body, grid=(s // tm,),
        in_specs=[pl.BlockSpec((tm, p.shape[1]), row) for p in a_parts]
        + [pl.BlockSpec((k, D), const, pipeline_mode=pl.Buffered(1)),
           pl.BlockSpec((tm, D), row), pl.BlockSpec((tm, 1), row), pl.BlockSpec((1, D), const),
           pl.BlockSpec((tm, D), row)] + [ANY_SPEC] * n_sin,
        out_specs=out_specs + [ANY_SPEC] * n_sout, out_shape=out_shape + ([] if side is None else side.out_shape),
        scratch_shapes=[] if side is None else side.scratch,
        name=name, compiler_params=_cp("arbitrary"))(*a_parts, b, xin, r, g, dres, *side_in)


def _gla_consts():
    lmask = _iota((4 * CHUNK, CHUNK), 0) % CHUNK >= _iota((4 * CHUNK, CHUNK), 1)
    hmask = _iota((256, 256), 0) // CHUNK == _iota((256, 256), 1) // CHUNK
    bd = _iota((256, 512), 0) // CHUNK == _iota((256, 512), 1) // 128
    return lmask, hmask, bd


def _fold_heads(x):
    return x[0:64] + x[64:128] + x[128:192] + x[192:256]


def _gla_decays(la, b_scr, dec_scr):
    tri = (_iota((CHUNK, CHUNK), 0) >= _iota((CHUNK, CHUNK), 1)).astype(BF16)
    ones = jnp.ones((CHUNK, 128), BF16)
    for c in range(la.shape[0] // CHUNK):
        la3 = _split3(la[CHUNK * c:CHUNK * (c + 1)])
        b_scr[CHUNK * c:CHUNK * (c + 1), :] = _sum3(_dot(tri, la3), 1)
        dec_scr[c] = jnp.exp(_sum3(_dot_tn(la3, ones), 0))


def _gla_chunk(b, qc, kc):
    bl = b[CHUNK - 1:CHUNK, :]
    ep, en, ek = jnp.exp(b), jnp.exp(-b), jnp.exp(bl - b)
    return bl, ep, en, ek, qc * ep, qc * en, kc * en, kc * ep, kc * ek


def _gla_fwd(pm, pe, wau_p, b_alpha):
    s = pm.shape[0]
    t = min(s, 1024)
    nc = t // CHUNK

    def body(q_ref, k_ref, v_ref, e_ref, wau_ref, ba_ref, o_ref, st_ref, state, b_scr, dec_scr):
        @pl.when(pl.program_id(0) == 0)
        def _():
            state[...] = jnp.zeros_like(state)

        z = _dot(e_ref[...].astype(BF16), wau_ref[...]) + ba_ref[...]
        _gla_decays(_log_sigmoid(z) * (1.0 / GLA_TAU), b_scr, dec_scr)
        lmask, hmask, bd = _gla_consts()

        def chunk(c, carry):
            rows = pl.ds(pl.multiple_of(c * CHUNK, CHUNK), CHUNK)
            qc = q_ref[rows, :].astype(F32) * 0.125
            kc = k_ref[rows, :].astype(F32)
            vc = v_ref[rows, :]
            _, _, _, _, qp, qn, kn, kp, kk = _gla_chunk(b_scr[rows, :], qc, kc)
            decb = jnp.concatenate([dec_scr[c]] * 4, axis=1)
            qs = jnp.where(hmask, jnp.concatenate([qp] * 4, axis=0), 0.0).astype(BF16)
            qns = jnp.where(hmask, jnp.concatenate([qn] * 4, axis=0), 0.0).astype(BF16)
            attn = jnp.where(lmask, _dot_nt(qs, kn.astype(BF16)), _dot_nt(qns, kp.astype(BF16))).astype(BF16)
            st = state[...]
            o_intra = _fold_heads(jnp.where(bd, _dot(attn, vc), 0.0))
            o_ref[rows, :] = o_intra + _dot(qp.astype(BF16), st.astype(BF16))
            for h in range(4):
                st_ref[c, :, 128 * h:128 * (h + 1)] = st[64 * h:64 * (h + 1), 128 * h:128 * (h + 1)]
            kv = jnp.where(bd, _dot_tn(kk.astype(BF16), vc), 0.0)
            state[...] = st * decb + kv
            return carry

        lax.fori_loop(0, nc, chunk, 0)

    return pl.pallas_call(
        body, grid=(s // t,),
        in_specs=[pl.BlockSpec((t, 256), lambda i: (i, C_GQ // 256)), pl.BlockSpec((t, 256), lambda i: (i, C_GK // 256)),
                  pl.BlockSpec((t, 512), lambda i: (i, C_GV // 512)), pl.BlockSpec((t, PE_W), lambda i: (i, 0)),
                  pl.BlockSpec((PE_W, 256), lambda i: (0, 0)), pl.BlockSpec((1, 256), lambda i: (0, 0))],
        out_specs=[pl.BlockSpec((t, 512), lambda i: (i, 0)), pl.BlockSpec((nc, CHUNK, 512), lambda i: (i, 0, 0))],
        out_shape=[jax.ShapeDtypeStruct((s, 512), F32), jax.ShapeDtypeStruct((s // CHUNK, CHUNK, 512), F32)],
        scratch_shapes=[pltpu.VMEM((256, 512), F32), pltpu.VMEM((t, 256), F32), pltpu.VMEM((nc, 256, 128), F32)],
        name="gla_fwd", compiler_params=_cp("arbitrary"))(pm, pm, pm, pe, wau_p, b_alpha)


def _gla_bwd(pm, pe, wau_p, wau_pt, b_alpha, do, states):
    s = pm.shape[0]
    t = min(s, 1024)
    nc = t // CHUNK
    nb = s // t

    def body(q_ref, k_ref, v_ref, e_ref, wau_ref, waut_ref, ba_ref, do_ref, st_ref,
             dq_ref, dk_ref, dv_ref, de_ref, dwau_ref, dba_ref, gstate, b_scr, db_scr, dec_scr):
        @pl.when(pl.program_id(0) == 0)
        def _():
            gstate[...] = jnp.zeros_like(gstate)
            dwau_ref[...] = jnp.zeros_like(dwau_ref)
            dba_ref[...] = jnp.zeros_like(dba_ref)

        eb = e_ref[...].astype(BF16)
        z = _dot(eb, wau_ref[...]) + ba_ref[...]
        _gla_decays(_log_sigmoid(z) * (1.0 / GLA_TAU), b_scr, dec_scr)
        lmask, hmask, bd = _gla_consts()
        last_row = _iota((CHUNK, 256), 0) == CHUNK - 1

        def chunk(cc, carry):
            c = nc - 1 - cc
            rows = pl.ds(pl.multiple_of(c * CHUNK, CHUNK), CHUNK)
            qc = q_ref[rows, :].astype(F32) * 0.125
            kc = k_ref[rows, :].astype(F32)
            vc = v_ref[rows, :]
            dob = do_ref[rows, :]
            bl, ep, en, ek, qp, qn, kn, kp, kk = _gla_chunk(b_scr[rows, :], qc, kc)
            decb = jnp.concatenate([dec_scr[c]] * 4, axis=1)
            qs = jnp.where(hmask, jnp.concatenate([qp] * 4, axis=0), 0.0).astype(BF16)
            qns = jnp.where(hmask, jnp.concatenate([qn] * 4, axis=0), 0.0).astype(BF16)
            knb, kpb = kn.astype(BF16), kp.astype(BF16)
            attn = jnp.where(lmask, _dot_nt(qs, knb), _dot_nt(qns, kpb)).astype(BF16)
            st = jnp.where(bd, jnp.concatenate([st_ref[c]] * 4, axis=0), 0.0)
            g = gstate[...]
            gb = g.astype(BF16)
            do_s = jnp.where(bd, jnp.concatenate([dob] * 4, axis=0), jnp.zeros((), BF16))
            dattn = _dot_nt(do_s, vc)
            dv_ref[rows, :] = (_dot_tn(attn, do_s) + _dot(kk.astype(BF16), gb)).astype(BF16)
            dac = jnp.where(lmask, dattn, 0.0).astype(BF16)
            daa = jnp.where(lmask, 0.0, dattn).astype(BF16)
            dqp = _fold_heads(jnp.where(hmask, _dot(dac, knb), 0.0)) + _dot_nt(dob, st.astype(BF16))
            dqn = _fold_heads(jnp.where(hmask, _dot(daa, kpb), 0.0))
            dkn = _dot_tn(dac, qs)
            dkp = _dot_tn(daa, qns)
            dkk = _dot_nt(vc, gb)
            ddec = _dot_nt(jnp.ones((8, 1536), BF16), _split3(g * st))[0:1, :]
            gstate[...] = decb * g + jnp.where(bd, _dot_tn(qp.astype(BF16), dob), 0.0)
            dq_ref[rows, :] = ((dqp * ep + dqn * en) * 0.125).astype(BF16)
            dk_ref[rows, :] = (dkn * en + dkp * ep + dkk * ek).astype(BF16)
            dek = dkk * kc * ek
            db = (dqp * qc + dkp * kc) * ep - (dqn * qc + dkn * kc) * en - dek
            dbl = jnp.sum(dek, axis=0, keepdims=True) + ddec * jnp.exp(bl)
            db_scr[rows, :] = db + jnp.where(last_row, dbl, 0.0)
            return carry

        lax.fori_loop(0, nc, chunk, 0)
        triu = (_iota((CHUNK, CHUNK), 0) <= _iota((CHUNK, CHUNK), 1)).astype(BF16)
        dla = jnp.concatenate([_sum3(_dot(triu, _split3(db_scr[CHUNK * c:CHUNK * (c + 1), :])), 1) for c in range(nc)], axis=0)
        dz = dla * (1.0 / GLA_TAU) * _sigmoid(-z)
        dzb = dz.astype(BF16)
        dwau_ref[...] += _dot_tn(eb, dzb)
        dba_ref[...] += _fold8(dz)
        de_ref[...] = _dot(dzb, waut_ref[...])

    rev = lambda i: nb - 1 - i
    return pl.pallas_call(
        body, grid=(nb,),
        in_specs=[pl.BlockSpec((t, 256), lambda i: (rev(i), C_GQ // 256)), pl.BlockSpec((t, 256), lambda i: (rev(i), C_GK // 256)),
                  pl.BlockSpec((t, 512), lambda i: (rev(i), C_GV // 512)), pl.BlockSpec((t, PE_W), lambda i: (rev(i), 0)),
                  pl.BlockSpec((PE_W, 256), lambda i: (0, 0)), pl.BlockSpec((256, PE_W), lambda i: (0, 0)),
                  pl.BlockSpec((1, 256), lambda i: (0, 0)), pl.BlockSpec((t, 512), lambda i: (rev(i), 0)),
                  pl.BlockSpec((nc, CHUNK, 512), lambda i: (rev(i), 0, 0))],
        out_specs=[pl.BlockSpec((t, 256), lambda i: (rev(i), 0)), pl.BlockSpec((t, 256), lambda i: (rev(i), 0)),
                   pl.BlockSpec((t, 512), lambda i: (rev(i), 0)), pl.BlockSpec((t, PE_W), lambda i: (rev(i), 0)),
                   pl.BlockSpec((PE_W, 256), lambda i: (0, 0)), pl.BlockSpec((8, 256), lambda i: (0, 0))],
        out_shape=[jax.ShapeDtypeStruct((s, 256), BF16), jax.ShapeDtypeStruct((s, 256), BF16),
                   jax.ShapeDtypeStruct((s, 512), BF16), jax.ShapeDtypeStruct((s, PE_W), F32),
                   jax.ShapeDtypeStruct((PE_W, 256), F32), jax.ShapeDtypeStruct((8, 256), F32)],
        scratch_shapes=[pltpu.VMEM((256, 512), F32), pltpu.VMEM((t, 256), F32), pltpu.VMEM((t, 256), F32),
                        pltpu.VMEM((nc, 256, 128), F32)],
        name="gla_bwd", compiler_params=_cp("arbitrary"))(pm, pm, pm, pe, wau_p, wau_pt, b_alpha, do, states)


def _fcum_fwd(pe, bias):
    s = pe.shape[0]
    t = min(s, 512)

    def body(e_ref, b_ref, f_ref, carry):
        @pl.when(pl.program_id(0) == 0)
        def _():
            carry[...] = jnp.zeros_like(carry)

        lf = _log_sigmoid(e_ref[...] + b_ref[...])
        tri = (_iota((t, t), 0) >= _iota((t, t), 1)).astype(BF16)
        f = _sum3(_dot(tri, _split3(lf)), 1) + carry[0:1, :]
        f_ref[...] = f
        carry[...] = jnp.broadcast_to(f[t - 1:t, :], carry.shape)

    return pl.pallas_call(
        body, grid=(s // t,),
        in_specs=[pl.BlockSpec((t, PE_W), lambda i: (i, 0)), pl.BlockSpec((1, PE_W), lambda i: (0, 0))],
        out_specs=pl.BlockSpec((t, PE_W), lambda i: (i, 0)),
        out_shape=jax.ShapeDtypeStruct((s, PE_W), F32), scratch_shapes=[pltpu.VMEM((8, PE_W), F32)],
        name="fcum_fwd", compiler_params=_cp("arbitrary"))(pe, bias)


def _fcum_bwd(pe, bias, df):
    s = pe.shape[0]
    t = min(s, 512)
    nb = s // t

    def body(e_ref, b_ref, df_ref, de_ref, db_ref, carry):
        @pl.when(pl.program_id(0) == 0)
        def _():
            carry[...] = jnp.zeros_like(carry)
            db_ref[...] = jnp.zeros_like(db_ref)

        triu = (_iota((t, t), 0) <= _iota((t, t), 1)).astype(BF16)
        dlf = _sum3(_dot(triu, _split3(df_ref[...])), 1) + carry[0:1, :]
        carry[...] = jnp.broadcast_to(dlf[0:1, :], carry.shape)
        lane = _iota((t, PE_W), 1)
        dff = jnp.where((lane >= FF_LANE) & (lane < FF_LANE + 8), dlf * _sigmoid(-(e_ref[...] + b_ref[...])), 0.0)
        de_ref[...] = dff
        db_ref[...] += _fold8(dff)

    rev = lambda i: (nb - 1 - i, 0)
    return pl.pallas_call(
        body, grid=(nb,),
        in_specs=[pl.BlockSpec((t, PE_W), rev), pl.BlockSpec((1, PE_W), lambda i: (0, 0)), pl.BlockSpec((t, PE_W), rev)],
        out_specs=[pl.BlockSpec((t, PE_W), rev), pl.BlockSpec((8, PE_W), lambda i: (0, 0))],
        out_shape=[jax.ShapeDtypeStruct((s, PE_W), F32), jax.ShapeDtypeStruct((8, PE_W), F32)],
        scratch_shapes=[pltpu.VMEM((8, PE_W), F32)],
        name="fcum_bwd", compiler_params=_cp("arbitrary"))(pe, bias, df)


FOX_WIDE = 1024


def _split3(x):
    hi = x.astype(BF16)
    r = x - hi.astype(F32)
    mid = r.astype(BF16)
    lo = (r - mid.astype(F32)).astype(BF16)
    return jnp.concatenate([hi, mid, lo], axis=1)


def _sum3(x, axis):
    n = x.shape[axis] // 3
    parts = [lax.slice_in_dim(x, n * p, n * (p + 1), axis=axis) for p in range(3)]
    return (parts[0] + parts[1]) + parts[2]


def _fox_tables():
    heads, lane = np.arange(8), np.arange(64)
    spread = np.zeros((512, 1024), np.float32)
    spread[(64 * heads[:, None] + lane).ravel(), (128 * heads[:, None] + lane).ravel()] = 1.0
    def place(src_lane0, dst_off, val):
        t = np.zeros((384, 1024), np.float32)
        for p in range(3):
            t[128 * p + src_lane0 + heads, 128 * heads + dst_off + p] = val
        return t
    def const(off, val):
        c = np.zeros((1, 1024), np.float32)
        for p in range(3):
            c[0, 128 * heads + off + p] = val
        return c
    rows = np.zeros((8, 128), np.float32)
    rows[heads, FF_LANE + heads] = 1.0
    bf = lambda a: jnp.asarray(a, BF16)
    return dict(spread=bf(spread),
                f_to_q=bf(place(FF_LANE, 64, 1.0)), f_to_k=bf(place(FF_LANE, 67, -1.0)), d_to_do=bf(place(0, 64, 1.0)),
                ones_q=jnp.asarray(const(67, 1.0)), ones_k=jnp.asarray(const(64, 1.0)), ones_v=jnp.asarray(const(64, -1.0)),
                rows=jnp.asarray(rows))


LOG2E = 1.4426950408889634


def _fox_prep(pm, f128, lse8, tb, *, backward):
    s = pm.shape[0]
    tm = min(s, 1024) if backward else _row_tile(s)

    def body(*refs):
        if backward:
            q_ref, f_ref, lse_ref, sp_ref, fq_ref, cq_ref, rows_ref, qa_ref = refs
            f = f_ref[...] * LOG2E - _dot_tn(lse_ref[...], rows_ref[...], precision=HI)
            q2 = (q_ref[...].astype(F32) * (0.125 * LOG2E)).astype(BF16)
            qa_ref[...] = (_dot(q2, sp_ref[...]) + _dot(_split3(f), fq_ref[...]) + cq_ref[...]).astype(BF16)
            return
        (q_ref, k_ref, v_ref, f_ref, sp_ref, fq_ref, fk_ref, cq_ref, ck_ref, cv_ref,
         qa_ref, ka_ref, va_ref, vt_ref, qt_ref, kt_ref) = refs
        f3 = _split3(f_ref[...] * LOG2E)
        q, k, v = q_ref[...].astype(F32), k_ref[...], v_ref[...]
        sp = sp_ref[...]
        qa_ref[...] = (_dot((q * (0.125 * LOG2E)).astype(BF16), sp) + _dot(f3, fq_ref[...]) + cq_ref[...]).astype(BF16)
        ka_ref[...] = (_dot(k, sp) + _dot(f3, fk_ref[...]) + ck_ref[...]).astype(BF16)
        va_ref[...] = (_dot(v, sp) + cv_ref[...]).astype(BF16)
        vt_ref[...] = v.T
        qt_ref[...] = (q * 0.125).astype(BF16).T
        kt_ref[...] = (k.astype(F32) * 0.125).astype(BF16).T

    row = lambda i: (i, 0)
    const = lambda i: (0, 0)
    blk = lambda c: pl.BlockSpec((tm, 512), lambda i: (i, c // 512))
    wide = pl.BlockSpec((tm, 1024), row)
    mat = lambda a: pl.BlockSpec(a.shape, const)
    if backward:
        ins = [pm, f128, lse8, tb["spread"], tb["f_to_q"], tb["ones_q"], tb["rows"]]
        in_specs = [blk(C_FQ), pl.BlockSpec((tm, 128), row), pl.BlockSpec((8, tm), lambda i: (0, i))] + [mat(a) for a in ins[3:]]
        out_specs, out_shape = wide, jax.ShapeDtypeStruct((s, 1024), BF16)
    else:
        ins = [pm, pm, pm, f128, tb["spread"], tb["f_to_q"], tb["f_to_k"], tb["ones_q"], tb["ones_k"], tb["ones_v"]]
        in_specs = [blk(C_FQ), blk(C_FK), blk(C_FV), pl.BlockSpec((tm, 128), row)] + [mat(a) for a in ins[4:]]
        tr = pl.BlockSpec((512, tm), lambda i: (0, i))
        out_specs = [wide, wide, wide, tr, tr, tr]
        out_shape = [jax.ShapeDtypeStruct((s, 1024), BF16)] * 3 + [jax.ShapeDtypeStruct((512, s), BF16)] * 3
    return pl.pallas_call(body, grid=(s // tm,), in_specs=in_specs, out_specs=out_specs, out_shape=out_shape,
                          name="fox_prep_bwd" if backward else "fox_prep", compiler_params=_cp("parallel"))(*ins)


def _fox_post(dqt, dkt, dvt, rowsum8, colsum8, tb):
    s = dqt.shape[1]
    tm = min(s, 1024)

    def body(dqt_ref, dkt_ref, dvt_ref, rs_ref, cs_ref, rows_ref, dfq_ref, dfk_ref, dfv_ref, df_ref):
        dfq_ref[...] = dqt_ref[...].T.astype(BF16)
        dfk_ref[...] = dkt_ref[...].T
        dfv_ref[...] = dvt_ref[...].T
        df_ref[...] = _dot_tn(rs_ref[...] - cs_ref[...], rows_ref[...], precision=HI)

    row = lambda i: (i, 0)
    tr = pl.BlockSpec((512, tm), lambda i: (0, i))
    out = pl.BlockSpec((tm, 512), row)
    heads = pl.BlockSpec((8, tm), lambda i: (0, i))
    return pl.pallas_call(
        body, grid=(s // tm,),
        in_specs=[tr, tr, tr, heads, heads, pl.BlockSpec((8, 128), lambda i: (0, 0))],
        out_specs=[out, out, out, pl.BlockSpec((tm, 128), row)],
        out_shape=[jax.ShapeDtypeStruct((s, 512), BF16)] * 3 + [jax.ShapeDtypeStruct((s, 128), F32)],
        name="fox_post", compiler_params=_cp("parallel"))(dqt, dkt, dvt, rowsum8, colsum8, tb["rows"])


def _fox_fwd(k_aug, q_aug, vt):
    s = k_aug.shape[0]
    nh = 8
    tk = _row_tile(s)
    tq = min(s, 2 * FOX_WIDE)
    per = tq // tk

    def body(k_ref, q_ref, v_ref, o_ref, lse_ref, sbuf):
        i = pl.program_id(1)
        qa = q_ref[...]

        def scores(j):
            return _dot_nt(k_ref[pl.ds(pl.multiple_of(j * tk, tk), tk), :], qa)

        ones_row = (_iota((16, tk), 0) == 0).astype(BF16)

        def update(st, j, carry):
            m, acc = carry
            m2 = jnp.maximum(m, jnp.max(st, axis=0, keepdims=True))
            p = jnp.exp2(st - m2)
            vj = jnp.concatenate([v_ref[:, pl.ds(pl.multiple_of(j * tk, tk), tk)], ones_row], axis=0)
            return m2, jnp.exp2(m - m2) * acc + _dot(vj, p.astype(BF16))

        def step(a, carry):
            sbuf[1] = scores(2 * a + 1)
            carry = update(sbuf[0], 2 * a, carry)
            sbuf[0] = scores(2 * a + 2)
            return update(sbuf[1], 2 * a + 1, carry)

        n = i * per
        sbuf[0] = scores(0)
        carry = (jnp.full((1, tq), -1e30, F32), jnp.zeros((80, tq), F32))
        carry = lax.fori_loop(0, n // 2, step, carry)
        tri = _iota((tk, tk), 0) <= _iota((tk, tk), 1)
        late = [_dot_nt(k_ref[pl.ds(pl.multiple_of((n + r) * tk, tk), tk), :], qa[r * tk:, :]) for r in range(1, per)]
        for r in range(per):
            st = sbuf[0] if r == 0 else late[r - 1]
            head = jnp.where(tri, st[:, :tk], -1e30)
            st = head if st.shape[1] == tk else jnp.concatenate([head, st[:, tk:]], axis=1)
            part = update(st, n + r, tuple(c[:, r * tk:] for c in carry))
            carry = part if r == 0 else tuple(jnp.concatenate([old[:, :r * tk], new], axis=1) for old, new in zip(carry, part))
        m, acc = carry
        l = acc[64:65]
        o_ref[...] = (acc[0:64] / l).astype(BF16)
        lse_ref[0] = m + jnp.log2(l)

    return pl.pallas_call(
        body, grid=(nh, s // tq),
        in_specs=[pl.BlockSpec((s, 128), lambda h, i: (0, h)), pl.BlockSpec((tq, 128), lambda h, i: (i, h)),
                  pl.BlockSpec((64, s), lambda h, i: (h, 0))],
        out_specs=[pl.BlockSpec((64, tq), lambda h, i: (h, i)), pl.BlockSpec((1, 1, tq), lambda h, i: (h, 0, i))],
        out_shape=[jax.ShapeDtypeStruct((512, s), BF16), jax.ShapeDtypeStruct((nh, 1, s), F32)],
        scratch_shapes=[pltpu.VMEM((2, tk, tq), F32)],
        name="fox_fwd", compiler_params=_cp("parallel", "arbitrary"))(k_aug, q_aug, vt)


def _fox_bwd(q_aug, do_aug, qt, dot_, k_aug, v_aug, kt):
    s = q_aug.shape[0]
    nh = 8
    tq = _row_tile(s)
    tk = min(s, 2 * FOX_WIDE)
    per = tk // tq
    nqb = s // tq

    def body(qa_ref, da_ref, qt_ref, dt_ref, ka_ref, va_ref, kt_ref, dq_ref, rs_ref, dk_ref, dv_ref, dfk_ref):
        j = pl.program_id(1)

        @pl.when(j == 0)
        def _():
            dq_ref[...] = jnp.zeros_like(dq_ref)
            rs_ref[...] = jnp.zeros_like(rs_ref)

        ones_row = (_iota((16, tk), 0) == 0).astype(BF16)
        ka, va = ka_ref[...], va_ref[...]
        ks = jnp.concatenate([kt_ref[...], ones_row], axis=0)
        tri = _iota((tq, tq), 0) >= _iota((tq, tq), 1)

        def tile(i, w, carry):
            masked = w is not None
            w = tk if w is None else w
            rows = pl.ds(pl.multiple_of(i * tq, tq), tq)
            sp = _dot_nt(qa_ref[rows, :], ka[:w])
            if masked:
                last = jnp.where(tri, sp[:, w - tq:], -1e30)
                sp = last if w == tq else jnp.concatenate([sp[:, :w - tq], last], axis=1)
            p = jnp.exp2(sp)
            dsb = (p * _dot_nt(da_ref[rows, :], va[:w])).astype(BF16)
            dq = _dot_nt(ks[:, :w], dsb)
            dq_ref[:, rows] += dq[0:64]
            rs_ref[0, :, rows] += dq[64:72]
            new = (_dot(jnp.concatenate([qt_ref[:, rows], ones_row[:, :tq]], axis=0), dsb), _dot(dt_ref[:, rows], p.astype(BF16)))
            if w == tk:
                return tuple(c + d for c, d in zip(carry, new))
            return tuple(jnp.concatenate([c[:, :w] + d, c[:, w:]], axis=1) for c, d in zip(carry, new))

        carry = (jnp.zeros((80, tk), F32), jnp.zeros((64, tk), F32))
        for r in range(per):
            carry = tile(j * per + r, (r + 1) * tq, carry)
        dk, dv = lax.fori_loop((j + 1) * per, nqb, lambda i, c: tile(i, None, c), carry)
        dk_ref[...] = dk[0:64].astype(BF16)
        dv_ref[...] = dv.astype(BF16)
        dfk_ref[0] = dk[64:65]

    head_cols = lambda h, j: (0, h)
    head_rows = lambda h, j: (h, 0)
    once = dict(pipeline_mode=pl.Buffered(1))
    return pl.pallas_call(
        body, grid=(nh, s // tk),
        in_specs=[pl.BlockSpec((s, 128), head_cols, **once), pl.BlockSpec((s, 128), head_cols, **once),
                  pl.BlockSpec((64, s), head_rows, **once), pl.BlockSpec((64, s), head_rows, **once),
                  pl.BlockSpec((tk, 128), lambda h, j: (j, h)), pl.BlockSpec((tk, 128), lambda h, j: (j, h)),
                  pl.BlockSpec((64, tk), lambda h, j: (h, j))],
        out_specs=[pl.BlockSpec((64, s), head_rows), pl.BlockSpec((1, 8, s), lambda h, j: (h, 0, 0)),
                   pl.BlockSpec((64, tk), lambda h, j: (h, j)),
                   pl.BlockSpec((64, tk), lambda h, j: (h, j)), pl.BlockSpec((1, 1, tk), lambda h, j: (h, 0, j))],
        out_shape=[jax.ShapeDtypeStruct((512, s), F32), jax.ShapeDtypeStruct((nh, 8, s), F32),
                   jax.ShapeDtypeStruct((512, s), BF16),
                   jax.ShapeDtypeStruct((512, s), BF16), jax.ShapeDtypeStruct((nh, 1, s), F32)],
        name="fox_bwd", compiler_params=_cp("parallel", "arbitrary"))(q_aug, do_aug, qt, dot_, k_aug, v_aug, kt)


MEM_SCALE = 128 ** -0.5


def _mem_attn_fwd(pm, mkv):
    s = pm.shape[0]
    t = min(s, 1024)
    nm = mkv.shape[0]

    def body(q_ref, mk_ref, mv_ref, o_ref):
        for h in range(4):
            cols = slice(128 * h, 128 * (h + 1))
            sc = _dot_nt(q_ref[:, cols], mk_ref[:, cols]) * MEM_SCALE
            p = jnp.exp(sc - jnp.max(sc, axis=-1, keepdims=True))
            p = p / jnp.sum(p, axis=-1, keepdims=True)
            o_ref[:, cols] = _dot(p.astype(BF16), mv_ref[:, cols]).astype(BF16)

    return pl.pallas_call(
        body, grid=(s // t,),
        in_specs=[pl.BlockSpec((t, 512), lambda i: (i, C_MQ // 512)), pl.BlockSpec((nm, 512), lambda i: (0, 0)),
                  pl.BlockSpec((nm, 512), lambda i: (0, 1))],
        out_specs=pl.BlockSpec((t, 512), lambda i: (i, 0)),
        out_shape=jax.ShapeDtypeStruct((s, 512), BF16),
        name="mem_attn_fwd", compiler_params=_cp("parallel"))(pm, mkv, mkv)


def _mem_attn_bwd(pm, mkv, do):
    s = pm.shape[0]
    t = min(s, 1024)
    nm = mkv.shape[0]

    def body(q_ref, mk_ref, mv_ref, do_ref, dq_ref, dmk_ref, dmv_ref):
        @pl.when(pl.program_id(0) == 0)
        def _():
            dmk_ref[...] = jnp.zeros_like(dmk_ref)
            dmv_ref[...] = jnp.zeros_like(dmv_ref)

        for h in range(4):
            cols = slice(128 * h, 128 * (h + 1))
            qh, kh, vh, doh = q_ref[:, cols], mk_ref[:, cols], mv_ref[:, cols], do_ref[:, cols]
            sc = _dot_nt(qh, kh) * MEM_SCALE
            p = jnp.exp(sc - jnp.max(sc, axis=-1, keepdims=True))
            p = p / jnp.sum(p, axis=-1, keepdims=True)
            pb = p.astype(BF16)
            dp = _dot_nt(doh, vh)
            ds = (p * (dp - jnp.sum(p * dp, axis=-1, keepdims=True)) * MEM_SCALE).astype(BF16)
            dq_ref[:, cols] = _dot(ds, kh).astype(BF16)
            dmk_ref[:, cols] += _dot_tn(ds, qh)
            dmv_ref[:, cols] += _dot_tn(pb, doh)

    return pl.pallas_call(
        body, grid=(s // t,),
        in_specs=[pl.BlockSpec((t, 512), lambda i: (i, C_MQ // 512)), pl.BlockSpec((nm, 512), lambda i: (0, 0)),
                  pl.BlockSpec((nm, 512), lambda i: (0, 1)), pl.BlockSpec((t, 512), lambda i: (i, 0))],
        out_specs=[pl.BlockSpec((t, 512), lambda i: (i, 0)), pl.BlockSpec((nm, 512), lambda i: (0, 0)),
                   pl.BlockSpec((nm, 512), lambda i: (0, 0))],
        out_shape=[jax.ShapeDtypeStruct((s, 512), BF16), jax.ShapeDtypeStruct((nm, 512), F32),
                   jax.ShapeDtypeStruct((nm, 512), F32)],
        name="mem_attn_bwd", compiler_params=_cp("arbitrary"))(pm, mkv, mkv, do)


def _gain_grad(dxn_g, x, r, name):
    m, d = x.shape

    def body(d_ref, x_ref, r_ref, o_ref):
        o_ref[...] = _fold8(d_ref[...] * (x_ref[...] * r_ref[...]))

    return pl.pallas_call(body, out_shape=jax.ShapeDtypeStruct((8, d), F32), name=name,
                          compiler_params=pltpu.CompilerParams(vmem_limit_bytes=VMEM_LIMIT_BYTES))(dxn_g, x, r)


def _head_norm(o, gh):
    xs, rs = [], []
    for h in range(4):
        oh = o[:, 128 * h:128 * (h + 1)]
        r = lax.rsqrt(jnp.mean(oh * oh, axis=-1, keepdims=True) + EPS)
        xs.append(oh * r)
        rs.append(r)
    return xs, rs


def _merge_fwd(x, pm, o_gla, o_fox_t, o_mem, g_head, wg, wf, wm, wo, g_ffn):
    s = x.shape[0]
    t = min(s, 256)

    def body(x_ref, g0_ref, g1_ref, g2_ref, gg_ref, og_ref, of_ref, om_ref, gh_ref, wg_ref, wf_ref, wm_ref, wo_ref, gf_ref,
             mg_ref, h1_ref, u2_ref, r2_ref):
        xs, _ = _head_norm(og_ref[...], None)
        gg = gg_ref[...].astype(F32)
        sil = gg * _sigmoid(gg)
        ogn = jnp.concatenate(xs, axis=1) * gh_ref[...] * sil
        merged = (_sigmoid(g0_ref[...].astype(F32)) * _dot(ogn.astype(BF16), wg_ref[...])
                  + _sigmoid(g1_ref[...].astype(F32)) * _dot(of_ref[...].T, wf_ref[...])
                  + _sigmoid(g2_ref[...].astype(F32)) * _dot(om_ref[...], wm_ref[...]))
        mb = merged.astype(BF16)
        mg_ref[...] = mb
        h1 = x_ref[...] + _dot(mb, wo_ref[...])
        h1_ref[...] = h1
        r = lax.rsqrt(jnp.mean(h1 * h1, axis=-1, keepdims=True) + EPS)
        u2_ref[...] = ((h1 * r) * gf_ref[...]).astype(BF16)
        r2_ref[...] = r

    row = lambda i: (i, 0)
    const = lambda i: (0, 0)
    return pl.pallas_call(
        body, grid=(s // t,),
        in_specs=[pl.BlockSpec((t, D), row), pl.BlockSpec((t, D), lambda i: (i, 0)), pl.BlockSpec((t, D), lambda i: (i, 1)),
                  pl.BlockSpec((t, D), lambda i: (i, 2)), pl.BlockSpec((t, 512), lambda i: (i, C_GG // 512)),
                  pl.BlockSpec((t, 512), row), pl.BlockSpec((512, t), lambda i: (0, i)), pl.BlockSpec((t, 512), row),
                  pl.BlockSpec((1, 512), const), pl.BlockSpec((512, D), const), pl.BlockSpec((512, D), const),
                  pl.BlockSpec((512, D), const), pl.BlockSpec((D, D), const), pl.BlockSpec((1, D), const)],
        out_specs=[pl.BlockSpec((t, D), row), pl.BlockSpec((t, D), row), pl.BlockSpec((t, D), row), pl.BlockSpec((t, 1), row)],
        out_shape=[jax.ShapeDtypeStruct((s, D), BF16), jax.ShapeDtypeStruct((s, D), F32),
                   jax.ShapeDtypeStruct((s, D), BF16), jax.ShapeDtypeStruct((s, 1), F32)],
        name="merge_fwd", compiler_params=_cp("parallel"))(x, pm, pm, pm, pm, o_gla, o_fox_t, o_mem, g_head, wg, wf, wm, wo, g_ffn)


def _merge_bwd(dh1b, pm, o_gla, o_fox_t, o_mem, g_head, wg, wf, wm, wgt, wft, wmt, wot, spread, d_to_do):
    s = dh1b.shape[0]
    t = min(s, 256)

    def body(dh_ref, g0_ref, g1_ref, g2_ref, gg_ref, og_ref, of_ref, om_ref, gh_ref, wg_ref, wf_ref, wm_ref,
             wgt_ref, wft_ref, wmt_ref, wot_ref, sp_ref, dd_ref,
             dgt_ref, dgg_ref, dog_ref, da_ref, dot_ref, dom_ref, dwg_ref, dwf_ref, dwm_ref, dgh_ref):
        @pl.when(pl.program_id(0) == 0)
        def _():
            dwg_ref[...] = jnp.zeros_like(dwg_ref)
            dwf_ref[...] = jnp.zeros_like(dwf_ref)
            dwm_ref[...] = jnp.zeros_like(dwm_ref)
            dgh_ref[...] = jnp.zeros_like(dgh_ref)

        dmerged = _dot(dh_ref[...], wot_ref[...])
        og = og_ref[...]
        xs, rs = _head_norm(og, None)
        on = jnp.concatenate(xs, axis=1)
        gg = gg_ref[...].astype(F32)
        sg = _sigmoid(gg)
        sil = gg * sg
        gh = gh_ref[...]
        ognb = (on * gh * sil).astype(BF16)
        ofb, omb = of_ref[...].T, om_ref[...]
        douts = []
        for idx, (gref, ob, w_ref, wt_ref, dw_ref) in enumerate((
                (g0_ref, ognb, wg_ref, wgt_ref, dwg_ref), (g1_ref, ofb, wf_ref, wft_ref, dwf_ref),
                (g2_ref, omb, wm_ref, wmt_ref, dwm_ref))):
            gt = _sigmoid(gref[...].astype(F32))
            y = _dot(ob, w_ref[...])
            dgt_ref[:, D * idx:D * (idx + 1)] = (dmerged * y * gt * (1.0 - gt)).astype(BF16)
            dy = (gt * dmerged).astype(BF16)
            dw_ref[...] += _dot_tn(ob, dy)
            douts.append(_dot(dy, wt_ref[...]))
        dogn, dof, dom = douts
        dofb = dof.astype(BF16)
        dom_ref[...] = dom.astype(BF16)
        ind = (_iota((1536, 128), 0) % 512 // 64 == _iota((1536, 128), 1)).astype(BF16)
        delta = _dot(_split3(dofb.astype(F32) * ofb.astype(F32)), ind)
        da_ref[...] = (_dot(dofb, sp_ref[...]) + _dot(_split3(delta), dd_ref[...])).astype(BF16)
        dot_ref[...] = dofb.T
        dgg_ref[...] = (dogn * on * gh * (sg * (1.0 + gg * (1.0 - sg)))).astype(BF16)
        d_on = dogn * sil
        dgh_ref[...] += _fold8(d_on * on)
        dxn = d_on * gh
        outs = []
        for h in range(4):
            cols = slice(128 * h, 128 * (h + 1))
            dh_, xh = dxn[:, cols], xs[h]
            outs.append(rs[h] * (dh_ - xh * jnp.mean(dh_ * xh, axis=-1, keepdims=True)))
        dog_ref[...] = jnp.concatenate(outs, axis=1).astype(BF16)

    row = lambda i: (i, 0)
    const = lambda i: (0, 0)
    return pl.pallas_call(
        body, grid=(s // t,),
        in_specs=[pl.BlockSpec((t, D), row), pl.BlockSpec((t, D), lambda i: (i, 0)), pl.BlockSpec((t, D), lambda i: (i, 1)),
                  pl.BlockSpec((t, D), lambda i: (i, 2)), pl.BlockSpec((t, 512), lambda i: (i, C_GG // 512)),
                  pl.BlockSpec((t, 512), row), pl.BlockSpec((512, t), lambda i: (0, i)), pl.BlockSpec((t, 512), row),
                  pl.BlockSpec((1, 512), const), pl.BlockSpec((512, D), const), pl.BlockSpec((512, D), const),
                  pl.BlockSpec((512, D), const), pl.BlockSpec((D, 512), const), pl.BlockSpec((D, 512), const),
                  pl.BlockSpec((D, 512), const), pl.BlockSpec((D, D), const),
                  pl.BlockSpec((512, 1024), const), pl.BlockSpec((384, 1024), const)],
        out_specs=[pl.BlockSpec((t, 3 * D), row), pl.BlockSpec((t, 512), row), pl.BlockSpec((t, 512), row),
                   pl.BlockSpec((t, 1024), row), pl.BlockSpec((512, t), lambda i: (0, i)), pl.BlockSpec((t, 512), row),
                   pl.BlockSpec((512, D), const), pl.BlockSpec((512, D), const), pl.BlockSpec((512, D), const),
                   pl.BlockSpec((8, 512), const)],
        out_shape=[jax.ShapeDtypeStruct((s, 3 * D), BF16), jax.ShapeDtypeStruct((s, 512), BF16),
                   jax.ShapeDtypeStruct((s, 512), BF16), jax.ShapeDtypeStruct((s, 1024), BF16),
                   jax.ShapeDtypeStruct((512, s), BF16), jax.ShapeDtypeStruct((s, 512), BF16),
                   jax.ShapeDtypeStruct((512, D), F32), jax.ShapeDtypeStruct((512, D), F32),
                   jax.ShapeDtypeStruct((512, D), F32), jax.ShapeDtypeStruct((8, 512), F32)],
        name="merge_bwd", compiler_params=_cp("arbitrary"))(
            dh1b, pm, pm, pm, pm, o_gla, o_fox_t, o_mem, g_head, wg, wf, wm, wgt, wft, wmt, wot, spread, d_to_do)


def _ff2_loss(a, w2, h1, g_final, target):
    s, k = a.shape
    tm = min(s, 512)

    def body(a_ref, w_ref, h1_ref, g_ref, t_ref, dh_ref, dhb_ref, loss_ref, dg_ref):
        @pl.when(pl.program_id(0) == 0)
        def _():
            loss_ref[...] = jnp.zeros_like(loss_ref)
            dg_ref[...] = jnp.zeros_like(dg_ref)

        h2 = h1_ref[...] + _dot(_relu2_bf16(a_ref[...]), w_ref[...])
        r = lax.rsqrt(jnp.mean(h2 * h2, axis=-1, keepdims=True) + EPS)
        xn = h2 * r
        g = g_ref[...]
        err = xn * g - t_ref[...]
        e2 = _fold8(err * err)
        part = e2[:, 0:128]
        for c in range(1, D // 128):
            part = part + e2[:, 128 * c:128 * (c + 1)]
        loss_ref[...] += part
        dy = err * (1.0 / D)
        dg_ref[...] += _fold8(dy * xn)
        dxn = dy * g
        dh = r * (dxn - xn * jnp.mean(dxn * xn, axis=-1, keepdims=True))
        dh_ref[...] = dh
        dhb_ref[...] = dh.astype(BF16)

    row = lambda i: (i, 0)
    const = lambda i: (0, 0)
    return pl.pallas_call(
        body, grid=(s // tm,),
        in_specs=[pl.BlockSpec((tm, k), row), pl.BlockSpec((k, D), const, pipeline_mode=pl.Buffered(1)),
                  pl.BlockSpec((tm, D), row), pl.BlockSpec((1, D), const), pl.BlockSpec((tm, D), row)],
        out_specs=[pl.BlockSpec((tm, D), row), pl.BlockSpec((tm, D), row), pl.BlockSpec((8, 128), const),
                   pl.BlockSpec((8, D), const)],
        out_shape=[jax.ShapeDtypeStruct((s, D), F32), jax.ShapeDtypeStruct((s, D), BF16),
                   jax.ShapeDtypeStruct((8, 128), F32), jax.ShapeDtypeStruct((8, D), F32)],
        name="ff2_loss", compiler_params=_cp("arbitrary"))(a, w2, h1, g_final, target)


def _adam(w, g, m, v, name):
    _, r, c = w.shape
    tr = r
    for cand in (512, 256, 128, 64, 32, 16, 8):
        if r % cand == 0 and cand * c * 4 <= (1 << 20):
            tr = cand
            break
    c1 = 1.0 - ADAM_B1 ** ADAM_STEP
    c2 = 1.0 - ADAM_B2 ** ADAM_STEP

    def body(w_ref, g_ref, m_ref, v_ref, d_ref, nm_ref, nv_ref):
        gv = g_ref[...]
        nm = ADAM_B1 * m_ref[...] + (1.0 - ADAM_B1) * gv
        nv = ADAM_B2 * v_ref[...] + (1.0 - ADAM_B2) * (gv * gv)
        d_ref[...] = -ADAM_LR * ((nm / c1) / (jnp.sqrt(nv / c2) + ADAM_EPS) + ADAM_WD * w_ref[...])
        nm_ref[...] = nm
        nv_ref[...] = nv

    spec = pl.BlockSpec((1, tr, c), lambda i: (0, i, 0))
    return pl.pallas_call(
        body, grid=(r // tr,), in_specs=[spec] * 4, out_specs=[spec] * 3,
        out_shape=[jax.ShapeDtypeStruct((1, r, c), F32)] * 3, name=name, compiler_params=_cp("parallel"))(w, g, m, v)


def _row_block(r):
    return max(d for d in range(16, 513, 16) if r % d == 0)


def _add_half(core, a, b, name):
    n, r, c = b.shape
    tr = _row_block(r)

    def body(core_ref, a_ref, b_ref, o_ref):
        o_ref[...] = (a_ref[...].astype(F32) + b_ref[...].astype(F32)).astype(BF16)

    spec = pl.BlockSpec((1, tr, c), lambda k, i, core_ref: (k, i, 0))
    half = pl.BlockSpec((1, tr, c), lambda k, i, core_ref: (k, i + core_ref[0] * (r // tr), 0))
    return pl.pallas_call(
        body, grid_spec=pltpu.PrefetchScalarGridSpec(num_scalar_prefetch=1, grid=(n, r // tr), in_specs=[half, spec],
                                                     out_specs=spec),
        out_shape=jax.ShapeDtypeStruct((n, r, c), BF16), name=name, compiler_params=_cp("parallel", "parallel"))(core, a, b)


def _sum4(a, name):
    _, r, c = a.shape
    tr = _row_block(r)

    def body(a_ref, o_ref):
        o_ref[...] = ((a_ref[0].astype(F32) + a_ref[1].astype(F32)) + a_ref[2].astype(F32)) + a_ref[3].astype(F32)

    return pl.pallas_call(body, grid=(r // tr,), in_specs=[pl.BlockSpec((4, tr, c), lambda i: (0, i, 0))],
                          out_specs=pl.BlockSpec((tr, c), lambda i: (i, 0)),
                          out_shape=jax.ShapeDtypeStruct((r, c), F32), name=name, compiler_params=_cp("parallel"))(a)


def _adam_small(w, gathered, m, v):
    c1 = 1.0 - ADAM_B1 ** ADAM_STEP
    c2 = 1.0 - ADAM_B2 ** ADAM_STEP

    def body(w_ref, g_ref, m_ref, v_ref, gs_ref, d_ref, nm_ref, nv_ref):
        gv = g_ref[0]
        for dev in range(1, N_DEV):
            gv = gv + g_ref[dev]
        gs_ref[...] = gv
        nm = ADAM_B1 * m_ref[...] + (1.0 - ADAM_B1) * gv
        nv = ADAM_B2 * v_ref[...] + (1.0 - ADAM_B2) * (gv * gv)
        d_ref[...] = -ADAM_LR * ((nm / c1) / (jnp.sqrt(nv / c2) + ADAM_EPS) + ADAM_WD * w_ref[...])
        nm_ref[...] = nm
        nv_ref[...] = nv

    return pl.pallas_call(body, out_shape=[jax.ShapeDtypeStruct((8, D), F32)] * 4, name="adam_small")(w, gathered, m, v)


def _place():
    return lax.axis_index("x"), lax.axis_index("y"), lax.axis_index("c")


def _other_chips(x, y):
    return [(1 - x, y), (x, 1 - y), (1 - x, 1 - y)]


GATHER_SEMS = [pltpu.SemaphoreType.DMA((6,)), pltpu.SemaphoreType.DMA((6,)), pltpu.SemaphoreType.DMA]


def _gather_ops(in_refs, out_refs, sems):
    (p_ref,), (out_ref,) = in_refs, out_refs
    send_sems, recv_sems, local_sem = sems
    hr = p_ref.shape[0] // 2
    x, y, cc = _place()
    sibling = (x, y, 1 - cc)
    chips = _other_chips(x, y)

    def half(chip, core):
        return out_ref.at[2 * chip[0] + chip[1], pl.ds(core * hr, hr), :]

    def copy(k, chip, core, to, src=None):
        return pltpu.make_async_remote_copy(
            src_ref=half(chip, core) if src is None else src, dst_ref=half(chip, core),
            send_sem=send_sems.at[k], recv_sem=recv_sems.at[k], device_id=to, device_id_type=MESH)

    mine = pltpu.make_async_copy(p_ref, out_ref.at[2 * x + y], local_sem)
    my_half = p_ref.at[pl.ds(cc * hr, hr), :]
    first = [copy(j, (x, y), cc, (*chip, cc), src=my_half) for j, chip in enumerate(chips)]
    passed = [copy(3 + j, chip, cc, sibling) for j, chip in enumerate(chips)]

    def start():
        mine.start()
        for cp in first:
            cp.start()

    def finish():
        for j, chip in enumerate(chips):
            copy(j, chip, cc, (x, y, cc)).wait_recv()
            passed[j].start()
        for j, chip in enumerate(chips):
            copy(3 + j, chip, 1 - cc, (x, y, cc)).wait_recv()
        for cp in first + passed:
            cp.wait_send()
        mine.wait()

    return start, finish


def _gather_side(p):
    return _Side([p], [jax.ShapeDtypeStruct((N_CHIPS,) + p.shape, p.dtype)], GATHER_SEMS, _gather_ops)


def _swap_halves(g):
    n, r, c = g.shape
    hr = r // 2

    def body(g_ref, out_ref, send_sem, recv_sem):
        x, y, cc = _place()
        cp = pltpu.make_async_remote_copy(
            src_ref=g_ref.at[:, pl.ds((1 - cc) * hr, hr), :], dst_ref=out_ref,
            send_sem=send_sem, recv_sem=recv_sem, device_id=(x, y, 1 - cc), device_id_type=MESH)
        cp.start()
        cp.wait()

    any_spec = pl.BlockSpec(memory_space=pl.ANY)
    return pl.pallas_call(
        body, out_shape=jax.ShapeDtypeStruct((n, hr, c), g.dtype), in_specs=[any_spec], out_specs=any_spec,
        scratch_shapes=[pltpu.SemaphoreType.DMA, pltpu.SemaphoreType.DMA], name="swap_halves")(g)


SCATTER_SEMS = [pltpu.SemaphoreType.DMA((7,)), pltpu.SemaphoreType.DMA((7,)), pltpu.SemaphoreType.DMA]


def _scatter_ops(in_refs, out_refs, sems):
    (p_ref,), (out_ref,) = in_refs, out_refs
    send_sems, recv_sems, local_sem = sems
    hr = p_ref.shape[1]
    x, y, cc = _place()
    me = 2 * x + y
    sibling = (x, y, 1 - cc)
    chips = _other_chips(x, y)
    ids = [2 * chip[0] + chip[1] for chip in chips]

    def land(src, core):
        return out_ref.at[src, pl.ds(core * hr, hr), :]

    def copy(k, src_ref, dst_ref, to):
        return pltpu.make_async_remote_copy(src_ref=src_ref, dst_ref=dst_ref, send_sem=send_sems.at[k],
                                            recv_sem=recv_sems.at[k], device_id=to, device_id_type=MESH)

    mine = pltpu.make_async_copy(p_ref.at[me], land(me, cc), local_sem)
    sends = [copy(j, p_ref.at[ids[j]], land(me, cc), (*chip, cc)) for j, chip in enumerate(chips)]
    sends.append(copy(3, p_ref.at[me], land(me, cc), sibling))
    passed = [copy(4 + j, land(ids[j], cc), land(ids[j], cc), sibling) for j in range(3)]

    def start():
        mine.start()
        for cp in sends:
            cp.start()

    def finish():
        for j in range(3):
            copy(j, p_ref.at[me], land(ids[j], cc), (x, y, cc)).wait_recv()
            passed[j].start()
        copy(3, p_ref.at[me], land(me, 1 - cc), (x, y, cc)).wait_recv()
        for j in range(3):
            copy(4 + j, p_ref.at[me], land(ids[j], 1 - cc), (x, y, cc)).wait_recv()
        for cp in sends + passed:
            cp.wait_send()
        mine.wait()

    return start, finish


def _scatter_side(p):
    n, hr, c = p.shape
    return _Side([p], [jax.ShapeDtypeStruct((n, 2 * hr, c), p.dtype)], SCATTER_SEMS, _scatter_ops)


def _gather_small(blk):
    m, n = blk.shape

    def body(x_ref, out_ref, send_sems, recv_sems, local_sem):
        x, y, cc = _place()
        me, sibling = (x, y, cc), (x, y, 1 - cc)
        chips = _other_chips(x, y)

        def slot(px, py, pc):
            return out_ref.at[4 * px + 2 * py + pc]

        def copy(k, block, to, src=None):
            return pltpu.make_async_remote_copy(
                src_ref=slot(*block) if src is None else src, dst_ref=slot(*block),
                send_sem=send_sems.at[k], recv_sem=recv_sems.at[k], device_id=to, device_id_type=MESH)

        mine = pltpu.make_async_copy(x_ref, slot(*me), local_sem)
        mine.start()
        first = [copy(0, me, sibling, src=x_ref)]
        first += [copy(1 + j, me, (*chip, cc), src=x_ref) for j, chip in enumerate(chips)]
        for cp in first:
            cp.start()
        passed = [copy(4 + j, (*chip, cc), sibling) for j, chip in enumerate(chips)]
        for j, chip in enumerate(chips):
            copy(1 + j, (*chip, cc), me).wait_recv()
            passed[j].start()
        copy(0, sibling, me).wait_recv()
        for j, chip in enumerate(chips):
            copy(4 + j, (*chip, 1 - cc), me).wait_recv()
        for cp in first + passed:
            cp.wait_send()
        mine.wait()

    vmem = pl.BlockSpec(memory_space=pltpu.VMEM)
    return pl.pallas_call(
        body, out_shape=jax.ShapeDtypeStruct((N_DEV, m, n), blk.dtype), in_specs=[vmem], out_specs=vmem,
        scratch_shapes=[pltpu.SemaphoreType.DMA((7,)), pltpu.SemaphoreType.DMA((7,)), pltpu.SemaphoreType.DMA],
        name="gather_small")(blk)


def _pack_a(sh, dtype):
    w = sh["w_in"].astype(dtype)
    return jnp.concatenate([w[:, 0:PACK_W], jnp.pad(w[:, PACK_W:], ((0, 0), (0, 2 * PACK_W - w.shape[1])))], axis=0)


def _pack_b(sh, dtype):
    o3 = jnp.concatenate([sh["w_gla_o"], sh["w_fox_o"], sh["w_mem_o"], jnp.zeros((512, 256), sh["w_gla_o"].dtype)], axis=1)
    au = jnp.pad(sh["w_alpha_up"], ((0, PACK_ROWS_B - 3072 - 16), (0, PACK_W - 64)))
    return jnp.concatenate([sh["w_ff1"], sh["w_ff2"], sh["w_mem_kv"], sh["w_out"], o3, au], axis=0).astype(dtype)


def _unpack_a(pa):
    return {"w_in": jnp.concatenate([pa[0:1024], pa[1024:2048, 0:1670 - PACK_W]], axis=1)}


def _unpack_b(pb):
    return {"w_ff1": pb[0:1024], "w_ff2": pb[1024:2048], "w_mem_kv": pb[2048:2304], "w_out": pb[2304:2560],
            "w_gla_o": pb[2560:3072, 0:256], "w_fox_o": pb[2560:3072, 256:512], "w_mem_o": pb[2560:3072, 512:768],
            "w_alpha_up": pb[3072:3088, 0:64]}


def _unpack(packed):
    return {**_unpack_a(packed[0:PACK_ROWS_A]), **_unpack_b(packed[PACK_ROWS_A:])}


def _split_shards(name, full):
    return jnp.split(full, N_CHIPS, axis=SHARD_AXIS[name])


def _pack_small(vals, scalar=None):
    row4 = jnp.concatenate([vals["b_alpha"].reshape(-1), vals["b_forget"].reshape(-1), jnp.zeros((D - 264,), F32)])
    row5 = jnp.concatenate([vals["g_gla_head"].reshape(-1), jnp.zeros((D - 512,), F32)])
    row6 = jnp.zeros((D,), F32) if scalar is None else jnp.broadcast_to(scalar, (D,))
    rows = [vals["g_mix"].reshape(-1), vals["g_mem"].reshape(-1), vals["g_ffn"].reshape(-1), vals["g_final"].reshape(-1),
            row4, row5, row6, jnp.zeros((D,), F32)]
    return jnp.stack(rows)


def _unpack_small(blk):
    return {"g_mix": blk[0].reshape(1, D), "g_mem": blk[1].reshape(1, D), "g_ffn": blk[2].reshape(1, D),
            "g_final": blk[3].reshape(D), "b_alpha": blk[4, 0:256].reshape(1, 256), "b_forget": blk[4, 256:264].reshape(1, 8),
            "g_gla_head": blk[5, 0:512].reshape(1, 4, 128)}


def _local_step(x, mem, target, wb, small, exchange=None):
    s = x.shape[0]
    nm = mem.shape[0]
    t = _row_tile(s)
    nb = s // t
    b_alpha = small["b_alpha"].reshape(1, 256)
    bias_e = jnp.concatenate([jnp.zeros((FF_LANE,), F32), small["b_forget"].reshape(-1),
                              jnp.zeros((PE_W - FF_LANE - 8,), F32)]).reshape(1, PE_W)
    g_mix, g_mem, g_ffn = small["g_mix"].reshape(1, D), small["g_mem"].reshape(1, D), small["g_ffn"].reshape(1, D)
    g_final = small["g_final"].reshape(1, D)
    g_head = small["g_gla_head"].reshape(1, 512)

    if exchange is None:
        u, r1 = _rms_fwd(x, g_mix, "norm_mix")
    else:
        u, r1, gathered = _rms_fwd(x, g_mix, "norm_mix", side=exchange.gather_a)
        wb = exchange.weights_a(gathered)
    w_in = wb["w_in"]
    w_main = jnp.concatenate([w_in[:, 3608:6680], w_in[:, 0:1536], w_in[:, 1552:3088], w_in[:, 3096:3608]], axis=1)
    w_e = jnp.concatenate([w_in[:, 1536:1552], w_in[:, 3088:3096], jnp.zeros((D, PE_W - 24), BF16)], axis=1)
    w_in_pt = _transpose(jnp.concatenate([w_main, w_e], axis=1), "t_w_in")
    big = min(s, 1024)
    if exchange is None:
        pm, pe = _proj(u, w_main, w_e)
    else:
        pm, pe, gathered = _proj(u, w_main, w_e, side=exchange.gather_b)
        wb = {**wb, **exchange.weights_b(gathered)}
    wau_p = jnp.concatenate([wb["w_alpha_up"], jnp.zeros((PE_W - 16, 256), BF16)], axis=0)
    o_gla, states = _gla_fwd(pm, pe, wau_p, b_alpha)
    fcum = _fcum_fwd(pe, bias_e)
    tb = _fox_tables()
    qf_aug, k_aug, v_aug, vt, qt, kt = _fox_prep(pm, fcum, None, tb, backward=False)
    o_fox, lse = _fox_fwd(k_aug, qf_aug, vt)
    mn, rm = _rms_fwd(mem, g_mem, "norm_mem")
    mkv = _mm_nn(mn, wb["w_mem_kv"], out_dtype=BF16, tm=nm, tn=512, tk=D, name="mem_kv")
    o_mem = _mem_attn_fwd(pm, mkv)
    merged, h1, u2, r2 = _merge_fwd(x, pm, o_gla, o_fox, o_mem, g_head, wb["w_gla_o"], wb["w_fox_o"], wb["w_mem_o"],
                                    wb["w_out"], g_ffn)
    a = _mm_nn(u2, wb["w_ff1"], out_dtype=BF16, tm=big, tn=1024, tk=D, name="ff1")
    dh2, dh2b, loss8, dgfin8 = _ff2_loss(a, wb["w_ff2"], h1, g_final, target)
    loss = 0.5 * jnp.sum(loss8) / D

    da = _mm_nn(dh2b, _transpose(wb["w_ff2"], "t_w_ff2"), out_dtype=BF16, tm=big, tn=1024, tk=D, name="d_act",
                epi=lambda acc, at: acc * (2.0 * jnp.maximum(at.astype(F32), 0.0)), extra=a)
    gw = {}
    gw["w_ff2"] = _mm_tn(a, dh2b, tm=1024, tn=D, ts=big, name="dw_ff2", a_fn=_relu2_bf16)
    gw["w_ff1"] = _mm_tn(u2, da, tm=D, tn=1024, ts=big, name="dw_ff1")
    dh1, dh1b, dgffn8 = _mm_norm_bwd([da], _transpose(wb["w_ff1"], "t_w_ff1"), h1, r2, g_ffn, dh2, name="d_h1", want_bf16=True)
    gw["w_out"] = _mm_tn(merged, dh1b, tm=D, tn=D, ts=big, name="dw_out")
    (dgates, dgg, do_gla, do_aug, do_t, do_mem, gw["w_gla_o"], gw["w_fox_o"], gw["w_mem_o"], dgh8) = _merge_bwd(
        dh1b, pm, o_gla, o_fox, o_mem, g_head, wb["w_gla_o"], wb["w_fox_o"], wb["w_mem_o"],
        *[_transpose(wb[n], "t_" + n) for n in ("w_gla_o", "w_fox_o", "w_mem_o", "w_out")], tb["spread"], tb["d_to_do"])
    dgq, dgk, dgv, de_gla, dwau_p, dba8 = _gla_bwd(pm, pe, wau_p, wau_p.T, b_alpha, do_gla, states)
    gw["w_alpha_up"] = dwau_p[0:16, :]
    q_aug = _fox_prep(pm, fcum, lse.reshape(8, s), tb, backward=True)
    dfq_t, dfrow, dfk_t, dfv_t, dfcol = _fox_bwd(q_aug, do_aug, qt, do_t, k_aug, v_aug, kt)
    dfq, dfk, dfv, df = _fox_post(dfq_t, dfk_t, dfv_t, dfrow[:, 0, :], dfcol.reshape(8, s), tb)
    de_fox, dbf8 = _fcum_bwd(pe, bias_e, df)
    dmq, dmk, dmv = _mem_attn_bwd(pm, mkv, do_mem)
    dmkv = jnp.concatenate([dmk, dmv], axis=1).astype(BF16)
    gw["w_mem_kv"] = _mm_tn(mn, dmkv, tm=D, tn=D, ts=nm, name="dw_mem_kv")
    dmn_g = _mm_nn(dmkv, _transpose(wb["w_mem_kv"], "t_w_mem_kv"), out_dtype=F32, tm=nm, tn=D, tk=D, name="d_mem_norm")
    dgmem8 = _gain_grad(dmn_g, mem, rm, "dg_mem")
    de = (de_gla + de_fox).astype(BF16)
    dproj = [dgates, dgq, dgk, dgv, dgg, dfq, dfk, dfv, dmq, de]
    dw_gates = _mm_tn(u, dgates, tm=D, tn=1024, ts=big, name="dw_in_gates")
    dw_g = _mm_tn_cat(u, [dgq, dgk, dgv], ts=big, name="dw_in_gla")
    dw_gf = _mm_tn_cat(u, [dgg, dfq], ts=big, name="dw_in_gg_fq")
    dw_f = _mm_tn_cat(u, [dfk, dfv], ts=big, name="dw_in_fk_fv")
    dw_m = _mm_tn_cat(u, [dmq, de], ts=big, name="dw_in_mq_narrow")
    gw["w_in"] = jnp.concatenate([dw_g, dw_gf[:, 0:512], dw_m[:, 512:528], dw_gf[:, 512:1024], dw_f,
                                  dw_m[:, 528:536], dw_m[:, 0:512], dw_gates], axis=1)
    if exchange is None:
        grad_x, dgmix8 = _mm_norm_bwd(dproj, w_in_pt, x, r1, g_mix, dh1, name="d_x", want_bf16=False)
        exchanged = None
    else:
        grad_x, dgmix8, exchanged = _mm_norm_bwd(dproj, w_in_pt, x, r1, g_mix, dh1, name="d_x", want_bf16=False,
                                                 side=exchange.scatter(gw))
    gs = {"g_mix": dgmix8.sum(0), "g_mem": dgmem8.sum(0), "g_ffn": dgffn8.sum(0), "g_final": dgfin8.sum(0),
          "b_alpha": dba8.sum(0), "b_forget": dbf8.sum(0)[FF_LANE:FF_LANE + 8], "g_gla_head": dgh8.sum(0)}
    return loss, grad_x, gw, gs, exchanged


def kernel(x, mem, g_mix, w_in, w_alpha_up, b_alpha, b_forget, g_gla_head, g_mem, w_mem_kv, w_gla_o, w_fox_o, w_mem_o, w_out, g_ffn, w_ff1, w_ff2, g_final, loss_target, m_g_mix, m_w_in, m_w_alpha_up, m_b_alpha, m_b_forget, m_g_gla_head, m_g_mem, m_w_mem_kv, m_w_gla_o, m_w_fox_o, m_w_mem_o, m_w_out, m_g_ffn, m_w_ff1, m_w_ff2, m_g_final, v_g_mix, v_w_in, v_w_alpha_up, v_b_alpha, v_b_forget, v_g_gla_head, v_g_mem, v_w_mem_kv, v_w_gla_o, v_w_fox_o, v_w_mem_o, v_w_out, v_g_ffn, v_w_ff1, v_w_ff2, v_g_final):
    args = dict(locals())
    w_sh = {n: args[n][0] for n in WEIGHTS}
    small = {n: args[n] for n in SMALL}

    def whole(parts):
        return {n: jnp.concatenate([p[n] for p in parts], axis=SHARD_AXIS[n]) for n in parts[0]}

    class Exchange:
        gather_a = _gather_side(_pack_a(w_sh, BF16))
        gather_b = _gather_side(_pack_b(w_sh, BF16))

        @staticmethod
        def weights_a(gathered):
            return whole([_unpack_a(gathered[k]) for k in range(N_CHIPS)])

        @staticmethod
        def weights_b(gathered):
            return whole([_unpack_b(gathered[k]) for k in range(N_CHIPS)])

        @staticmethod
        def scatter(gw):
            by_chip = {n: _split_shards(n, gw[n]) for n in WEIGHTS}
            packed = jnp.stack([jnp.concatenate([_pack_a({n: by_chip[n][k] for n in WEIGHTS}, BF16),
                                                 _pack_b({n: by_chip[n][k] for n in WEIGHTS}, BF16)], axis=0)
                                for k in range(N_CHIPS)])
            core = lax.axis_index("c").astype(jnp.int32).reshape(1)
            return _scatter_side(_add_half(core, packed, _swap_halves(packed), "chip_sum"))

    loss, grad_x, gw, gs, by_chip = _local_step(x[0], mem[0], loss_target[0], None, small, Exchange)
    g_out = {n: g[None] for n, g in _unpack(_sum4(by_chip, "shard_sum")).items()}
    d_out, m_out, v_out = {}, {}, {}
    for n in WEIGHTS:
        d_out[n], m_out[n], v_out[n] = _adam(args[n], g_out[n], args["m_" + n], args["v_" + n], "adam_" + n)

    small_all = _gather_small(_pack_small(gs, loss))
    sm = {n: args["m_" + n] for n in SMALL}
    sv = {n: args["v_" + n] for n in SMALL}
    gs_sum, sd, snm, snv = _adam_small(_pack_small(small), small_all, _pack_small(sm), _pack_small(sv))
    gs_o, sd_o, snm_o, snv_o = _unpack_small(gs_sum), _unpack_small(sd), _unpack_small(snm), _unpack_small(snv)

    names = ["g_mix", "w_in", "w_alpha_up", "b_alpha", "b_forget", "g_gla_head", "g_mem", "w_mem_kv", "w_gla_o", "w_fox_o",
             "w_mem_o", "w_out", "g_ffn", "w_ff1", "w_ff2", "g_final"]

    def pick(big, sml, n):
        return big[n] if n in big else sml[n]

    outs = [gs_sum[6, 0], grad_x[None]]
    for big, sml in ((g_out, gs_o), (d_out, sd_o), (m_out, snm_o), (v_out, snv_o)):
        outs += [pick(big, sml, n) for n in names]
    return tuple(outs)
```

```python
import functools

import numpy as np
import jax
import jax.numpy as jnp
from jax import lax
from jax.experimental import pallas as pl
from jax.experimental.pallas import tpu as pltpu

F32 = jnp.float32
BF16 = jnp.bfloat16
HI = lax.Precision.HIGHEST
MESH = pl.DeviceIdType.MESH

EPS = 1e-6
D = 1024
CHUNK = 64
GLA_TAU = 16.0
N_CHIPS = 4
N_DEV = 8
VMEM_LIMIT_BYTES = 56 * 1024 * 1024

ADAM_LR, ADAM_B1, ADAM_B2, ADAM_EPS, ADAM_WD, ADAM_STEP = 0.001, 0.9, 0.999, 1e-08, 0.01, 10

PM_W = 6656
PE_W = 128
C_GQ, C_GK, C_GV, C_GG, C_FQ, C_FK, C_FV, C_MQ = 3072, 3328, 3584, 4096, 4608, 5120, 5632, 6144
FF_LANE = 16

WEIGHTS = ("w_in", "w_alpha_up", "w_mem_kv", "w_gla_o", "w_fox_o", "w_mem_o", "w_out", "w_ff1", "w_ff2")
SHARD_AXIS = {"w_in": 1, "w_alpha_up": 1, "w_mem_kv": 0, "w_gla_o": 1, "w_fox_o": 1, "w_mem_o": 1, "w_out": 0,
              "w_ff1": 1, "w_ff2": 0}
SMALL = ("g_mix", "g_mem", "g_ffn", "g_final", "b_alpha", "b_forget", "g_gla_head")
PACK_W = 1024
PACK_ROWS_A = 2048
PACK_ROWS_B = 3104
PACK_ROWS = PACK_ROWS_A + PACK_ROWS_B


def _cp(*sem):
    return pltpu.CompilerParams(dimension_semantics=sem, vmem_limit_bytes=VMEM_LIMIT_BYTES)


def _dot(a, b, **kw):
    return jnp.dot(a, b, preferred_element_type=F32, **kw)


def _dot_nt(a, b, **kw):
    return lax.dot_general(a, b, (((1,), (1,)), ((), ())), preferred_element_type=F32, **kw)


def _dot_tn(a, b, **kw):
    return lax.dot_general(a, b, (((0,), (0,)), ((), ())), preferred_element_type=F32, **kw)


def _sigmoid(x):
    return 0.5 * jnp.tanh(0.5 * x) + 0.5


def _log_sigmoid(x):
    return -(jnp.maximum(-x, 0.0) + jnp.log1p(jnp.exp(-jnp.abs(x))))


def _fold8(x):
    m, n = x.shape
    return x.reshape(m // 8, 8, n).sum(axis=0)


def _iota(shape, dim):
    return lax.broadcasted_iota(jnp.int32, shape, dim)


def _row_tile(s):
    return min(s, 512)


class _Side:
    def __init__(self, inputs, out_shape, scratch, ops):
        self.inputs, self.out_shape, self.scratch, self.ops = list(inputs), list(out_shape), list(scratch), ops


ANY_SPEC = pl.BlockSpec(memory_space=pl.ANY)


def _mm_nn(a, b, *, out_dtype, tm, tn, tk, name, a_fn=None, epi=None, extra=None):
    m, k = a.shape
    _, n = b.shape
    nk = k // tk

    def body_one(*refs):
        a_ref, b_ref = refs[0], refs[1]
        at = a_ref[...] if a_fn is None else a_fn(a_ref[...])
        r = _dot(at, b_ref[...])
        if epi is not None:
            r = epi(r, None if extra is None else refs[2][...])
        refs[-1][...] = r.astype(out_dtype)

    if nk == 1:
        in_specs = [pl.BlockSpec((tm, k), lambda i, j: (i, 0)), pl.BlockSpec((k, tn), lambda i, j: (0, j))]
        args = [a, b]
        if extra is not None:
            in_specs.append(pl.BlockSpec((tm, tn), lambda i, j: (i, j)))
            args.append(extra)
        return pl.pallas_call(
            body_one, grid=(m // tm, n // tn), in_specs=in_specs, out_specs=pl.BlockSpec((tm, tn), lambda i, j: (i, j)),
            out_shape=jax.ShapeDtypeStruct((m, n), out_dtype), name=name, compiler_params=_cp("parallel", "parallel"))(*args)

    def body(*refs):
        if extra is None:
            a_ref, b_ref, o_ref, acc = refs
            x_ref = None
        else:
            a_ref, b_ref, x_ref, o_ref, acc = refs
        kk = pl.program_id(2)

        @pl.when(kk == 0)
        def _():
            acc[...] = jnp.zeros_like(acc)

        at = a_ref[...]
        if a_fn is not None:
            at = a_fn(at)
        acc[...] += _dot(at, b_ref[...])

        @pl.when(kk == nk - 1)
        def _():
            r = acc[...]
            if epi is not None:
                r = epi(r, None if x_ref is None else x_ref[...])
            o_ref[...] = r.astype(out_dtype)

    in_specs = [pl.BlockSpec((tm, tk), lambda i, j, kk: (i, kk)), pl.BlockSpec((tk, tn), lambda i, j, kk: (kk, j))]
    args = [a, b]
    if extra is not None:
        in_specs.append(pl.BlockSpec((tm, tn), lambda i, j, kk: (i, j)))
        args.append(extra)
    return pl.pallas_call(
        body, grid=(m // tm, n // tn, nk), in_specs=in_specs,
        out_specs=pl.BlockSpec((tm, tn), lambda i, j, kk: (i, j)),
        out_shape=jax.ShapeDtypeStruct((m, n), out_dtype),
        scratch_shapes=[pltpu.VMEM((tm, tn), F32)], name=name,
        compiler_params=_cp("parallel", "parallel", "arbitrary"))(*args)


def _mm_tn(a, b, *, tm, tn, ts, name, a_fn=None):
    s, m = a.shape
    _, n = b.shape
    ns = s // ts

    def body(a_ref, b_ref, o_ref, acc):
        kk = pl.program_id(2)

        @pl.when(kk == 0)
        def _():
            acc[...] = jnp.zeros_like(acc)

        at = a_ref[...]
        if a_fn is not None:
            at = a_fn(at)
        acc[...] += _dot_tn(at, b_ref[...])

        @pl.when(kk == ns - 1)
        def _():
            o_ref[...] = acc[...]

    return pl.pallas_call(
        body, grid=(m // tm, n // tn, ns),
        in_specs=[pl.BlockSpec((ts, tm), lambda i, j, kk: (kk, i)), pl.BlockSpec((ts, tn), lambda i, j, kk: (kk, j))],
        out_specs=pl.BlockSpec((tm, tn), lambda i, j, kk: (i, j)),
        out_shape=jax.ShapeDtypeStruct((m, n), F32),
        scratch_shapes=[pltpu.VMEM((tm, tn), F32)], name=name,
        compiler_params=_cp("parallel", "parallel", "arbitrary"))(a, b)


def _mm_tn_cat(a, bs, *, ts, name):
    s, m = a.shape
    n = sum(b.shape[1] for b in bs)
    ns = s // ts
    nb = len(bs)

    def body(*refs):
        a_ref, b_refs, o_ref, acc = refs[0], refs[1:1 + nb], refs[1 + nb], refs[2 + nb]
        kk = pl.program_id(0)

        @pl.when(kk == 0)
        def _():
            acc[...] = jnp.zeros_like(acc)

        bt = b_refs[0][...] if nb == 1 else jnp.concatenate([r[...] for r in b_refs], axis=1)
        acc[...] += _dot_tn(a_ref[...], bt)

        @pl.when(kk == ns - 1)
        def _():
            o_ref[...] = acc[...]

    return pl.pallas_call(
        body, grid=(ns,),
        in_specs=[pl.BlockSpec((ts, m), lambda kk: (kk, 0))] + [pl.BlockSpec((ts, b.shape[1]), lambda kk: (kk, 0)) for b in bs],
        out_specs=pl.BlockSpec((m, n), lambda kk: (0, 0)), out_shape=jax.ShapeDtypeStruct((m, n), F32),
        scratch_shapes=[pltpu.VMEM((m, n), F32)], name=name, compiler_params=_cp("arbitrary"))(a, *bs)


def _proj(u, w_main, w_e, side=None):
    s, k = u.shape
    n = w_main.shape[1]
    tm, tn = min(s, 1024), n // 4
    n_sin = 0 if side is None else len(side.inputs)
    n_sout = 0 if side is None else len(side.out_shape)

    def body(u_ref, w_ref, we_ref, *rest):
        pm_ref, pe_ref = rest[n_sin:n_sin + 2]
        i, j = pl.program_id(0), pl.program_id(1)
        if side is not None:
            start, finish = side.ops(rest[:n_sin], rest[n_sin + 2:n_sin + 2 + n_sout], rest[n_sin + 2 + n_sout:])
            pl.when((i == 0) & (j == 0))(start)
        ut = u_ref[...]
        pm_ref[...] = _dot(ut, w_ref[...]).astype(BF16)

        @pl.when(j == 0)
        def _():
            pe_ref[...] = _dot(ut, we_ref[...])

        if side is not None:
            pl.when((i == s // tm - 1) & (j == n // tn - 1))(finish)

    side_in = [] if side is None else side.inputs
    return pl.pallas_call(
        body, grid=(s // tm, n // tn),
        in_specs=[pl.BlockSpec((tm, k), lambda i, j: (i, 0)), pl.BlockSpec((k, tn), lambda i, j: (0, j)),
                  pl.BlockSpec((k, PE_W), lambda i, j: (0, 0))] + [ANY_SPEC] * n_sin,
        out_specs=[pl.BlockSpec((tm, tn), lambda i, j: (i, j)), pl.BlockSpec((tm, PE_W), lambda i, j: (i, 0))]
        + [ANY_SPEC] * n_sout,
        out_shape=[jax.ShapeDtypeStruct((s, n), BF16), jax.ShapeDtypeStruct((s, PE_W), F32)]
        + ([] if side is None else side.out_shape),
        scratch_shapes=[] if side is None else side.scratch,
        name="proj_main", compiler_params=_cp("arbitrary", "arbitrary"))(u, w_main, w_e, *side_in)


def _transpose(w, name):
    r, c = w.shape
    tr = min(r, 256)

    def body(w_ref, o_ref):
        o_ref[...] = w_ref[...].T

    return pl.pallas_call(body, grid=(r // tr,), in_specs=[pl.BlockSpec((tr, c), lambda i: (i, 0))],
                          out_specs=pl.BlockSpec((c, tr), lambda i: (0, i)),
                          out_shape=jax.ShapeDtypeStruct((c, r), w.dtype), name=name, compiler_params=_cp("parallel"))(w)


def _relu2_bf16(t):
    r = jnp.maximum(t.astype(F32), 0.0)
    return (r * r).astype(BF16)


def _rms_fwd(x, g, name, side=None):
    s, d = x.shape
    tm = min(s, 512)
    n_sin = 0 if side is None else len(side.inputs)
    n_sout = 0 if side is None else len(side.out_shape)

    def body(x_ref, g_ref, *rest):
        u_ref, r_ref = rest[n_sin:n_sin + 2]
        if side is not None:
            start, finish = side.ops(rest[:n_sin], rest[n_sin + 2:n_sin + 2 + n_sout], rest[n_sin + 2 + n_sout:])
            pl.when(pl.program_id(0) == 0)(start)
        xv = x_ref[...]
        r = lax.rsqrt(jnp.mean(xv * xv, axis=-1, keepdims=True) + EPS)
        u_ref[...] = ((xv * r) * g_ref[...]).astype(BF16)
        r_ref[...] = r
        if side is not None:
            pl.when(pl.program_id(0) == s // tm - 1)(finish)

    side_in = [] if side is None else side.inputs
    return pl.pallas_call(
        body, grid=(s // tm,),
        in_specs=[pl.BlockSpec((tm, d), lambda i: (i, 0)), pl.BlockSpec((1, d), lambda i: (0, 0))] + [ANY_SPEC] * n_sin,
        out_specs=[pl.BlockSpec((tm, d), lambda i: (i, 0)), pl.BlockSpec((tm, 1), lambda i: (i, 0))] + [ANY_SPEC] * n_sout,
        out_shape=[jax.ShapeDtypeStruct((s, d), BF16), jax.ShapeDtypeStruct((s, 1), F32)]
        + ([] if side is None else side.out_shape),
        scratch_shapes=[] if side is None else side.scratch,
        name=name, compiler_params=_cp("parallel" if side is None else "arbitrary"))(x, g, *side_in)


def _mm_norm_bwd(a_parts, b, xin, r, g, dres, *, name, want_bf16, side=None):
    s = a_parts[0].shape[0]
    k = b.shape[0]
    na = len(a_parts)
    offs = [sum(p.shape[1] for p in a_parts[:i]) for i in range(na)]
    assert offs[-1] + a_parts[-1].shape[1] == k
    tm = min(s, 512)
    n_out = 3 if want_bf16 else 2
    n_sin = 0 if side is None else len(side.inputs)
    n_sout = 0 if side is None else len(side.out_shape)

    def body(*refs):
        a_refs = refs[:na]
        b_ref, x_ref, r_ref, g_ref, dres_ref = refs[na:na + 5]
        rest = refs[na + 5:]
        outs = rest[n_sin:n_sin + n_out]
        dx_ref, dg_ref = outs[0], outs[-1]
        if side is not None:
            start, finish = side.ops(rest[:n_sin], rest[n_sin + n_out:n_sin + n_out + n_sout], rest[n_sin + n_out + n_sout:])
            pl.when(pl.program_id(0) == 0)(start)

        @pl.when(pl.program_id(0) == 0)
        def _():
            dg_ref[...] = jnp.zeros_like(dg_ref)

        du = _dot(a_refs[0][...], b_ref[0:a_parts[0].shape[1], :])
        for a_ref, off, part in zip(a_refs[1:], offs[1:], a_parts[1:]):
            du = du + _dot(a_ref[...], b_ref[off:off + part.shape[1], :])
        xn = x_ref[...] * r_ref[...]
        dg_ref[...] += _fold8(du * xn)
        dxn = du * g_ref[...]
        dx = dres_ref[...] + r_ref[...] * (dxn - xn * jnp.mean(dxn * xn, axis=-1, keepdims=True))
        dx_ref[...] = dx
        if want_bf16:
            outs[1][...] = dx.astype(BF16)
        if side is not None:
            pl.when(pl.program_id(0) == s // tm - 1)(finish)

    row = lambda i: (i, 0)
    const = lambda i: (0, 0)
    out_specs = [pl.BlockSpec((tm, D), row)]
    out_shape = [jax.ShapeDtypeStruct((s, D), F32)]
    if want_bf16:
        out_specs.append(pl.BlockSpec((tm, D), row))
        out_shape.append(jax.ShapeDtypeStruct((s, D), BF16))
    out_specs.append(pl.BlockSpec((8, D), const))
    out_shape.append(jax.ShapeDtypeStruct((8, D), F32))
    side_in = [] if side is None else side.inputs
    return pl.pallas_call(
        body, grid=(s // tm,),
        in_specs=[pl.BlockSpec((tm, p.shape[1]), row) for p in a_parts]
        + [pl.BlockSpec((k, D), const, pipeline_mode=pl.Buffered(1)),
           pl.BlockSpec((tm, D), row), pl.BlockSpec((tm, 1), row), pl.BlockSpec((1, D), const),
           pl.BlockSpec((tm, D), row)] + [ANY_SPEC] * n_sin,
        out_specs=out_specs + [ANY_SPEC] * n_sout, out_shape=out_shape + ([] if side is None else side.out_shape),
        scratch_shapes=[] if side is None else side.scratch,
        name=name, compiler_params=_cp("arbitrary"))(*a_parts, b, xin, r, g, dres, *side_in)


def _gla_consts():
    lmask = _iota((4 * CHUNK, CHUNK), 0) % CHUNK >= _iota((4 * CHUNK, CHUNK), 1)
    hmask = _iota((256, 256), 0) // CHUNK == _iota((256, 256), 1) // CHUNK
    bd = _iota((256, 512), 0) // CHUNK == _iota((256, 512), 1) // 128
    return lmask, hmask, bd


def _fold_heads(x):
    return x[0:64] + x[64:128] + x[128:192] + x[192:256]


def _gla_decays(la, b_scr, dec_scr):
    tri = (_iota((CHUNK, CHUNK), 0) >= _iota((CHUNK, CHUNK), 1)).astype(BF16)
    ones = jnp.ones((CHUNK, 128), BF16)
    for c in range(la.shape[0] // CHUNK):
        la3 = _split3(la[CHUNK * c:CHUNK * (c + 1)])
        b_scr[CHUNK * c:CHUNK * (c + 1), :] = _sum3(_dot(tri, la3), 1)
        dec_scr[c] = jnp.exp(_sum3(_dot_tn(la3, ones), 0))


def _gla_chunk(b, qc, kc):
    bl = b[CHUNK - 1:CHUNK, :]
    ep, en, ek = jnp.exp(b), jnp.exp(-b), jnp.exp(bl - b)
    return bl, ep, en, ek, qc * ep, qc * en, kc * en, kc * ep, kc * ek


def _gla_fwd(pm, pe, wau_p, b_alpha):
    s = pm.shape[0]
    t = min(s, 1024)
    nc = t // CHUNK

    def body(q_ref, k_ref, v_ref, e_ref, wau_ref, ba_ref, o_ref, st_ref, state, b_scr, dec_scr):
        @pl.when(pl.program_id(0) == 0)
        def _():
            state[...] = jnp.zeros_like(state)

        z = _dot(e_ref[...].astype(BF16), wau_ref[...]) + ba_ref[...]
        _gla_decays(_log_sigmoid(z) * (1.0 / GLA_TAU), b_scr, dec_scr)
        lmask, hmask, bd = _gla_consts()

        def chunk(c, carry):
            rows = pl.ds(pl.multiple_of(c * CHUNK, CHUNK), CHUNK)
            qc = q_ref[rows, :].astype(F32) * 0.125
            kc = k_ref[rows, :].astype(F32)
            vc = v_ref[rows, :]
            _, _, _, _, qp, qn, kn, kp, kk = _gla_chunk(b_scr[rows, :], qc, kc)
            decb = jnp.concatenate([dec_scr[c]] * 4, axis=1)
            qs = jnp.where(hmask, jnp.concatenate([qp] * 4, axis=0), 0.0).astype(BF16)
            qns = jnp.where(hmask, jnp.concatenate([qn] * 4, axis=0), 0.0).astype(BF16)
            attn = jnp.where(lmask, _dot_nt(qs, kn.astype(BF16)), _dot_nt(qns, kp.astype(BF16))).astype(BF16)
            st = state[...]
            o_intra = _fold_heads(jnp.where(bd, _dot(attn, vc), 0.0))
            o_ref[rows, :] = o_intra + _dot(qp.astype(BF16), st.astype(BF16))
            for h in range(4):
                st_ref[c, :, 128 * h:128 * (h + 1)] = st[64 * h:64 * (h + 1), 128 * h:128 * (h + 1)]
            kv = jnp.where(bd, _dot_tn(kk.astype(BF16), vc), 0.0)
            state[...] = st * decb + kv
            return carry

        lax.fori_loop(0, nc, chunk, 0)

    return pl.pallas_call(
        body, grid=(s // t,),
        in_specs=[pl.BlockSpec((t, 256), lambda i: (i, C_GQ // 256)), pl.BlockSpec((t, 256), lambda i: (i, C_GK // 256)),
                  pl.BlockSpec((t, 512), lambda i: (i, C_GV // 512)), pl.BlockSpec((t, PE_W), lambda i: (i, 0)),
                  pl.BlockSpec((PE_W, 256), lambda i: (0, 0)), pl.BlockSpec((1, 256), lambda i: (0, 0))],
        out_specs=[pl.BlockSpec((t, 512), lambda i: (i, 0)), pl.BlockSpec((nc, CHUNK, 512), lambda i: (i, 0, 0))],
        out_shape=[jax.ShapeDtypeStruct((s, 512), F32), jax.ShapeDtypeStruct((s // CHUNK, CHUNK, 512), F32)],
        scratch_shapes=[pltpu.VMEM((256, 512), F32), pltpu.VMEM((t, 256), F32), pltpu.VMEM((nc, 256, 128), F32)],
        name="gla_fwd", compiler_params=_cp("arbitrary"))(pm, pm, pm, pe, wau_p, b_alpha)


def _gla_bwd(pm, pe, wau_p, wau_pt, b_alpha, do, states):
    s = pm.shape[0]
    t = min(s, 1024)
    nc = t // CHUNK
    nb = s // t

    def body(q_ref, k_ref, v_ref, e_ref, wau_ref, waut_ref, ba_ref, do_ref, st_ref,
             dq_ref, dk_ref, dv_ref, de_ref, dwau_ref, dba_ref, gstate, b_scr, db_scr, dec_scr):
        @pl.when(pl.program_id(0) == 0)
        def _():
            gstate[...] = jnp.zeros_like(gstate)
            dwau_ref[...] = jnp.zeros_like(dwau_ref)
            dba_ref[...] = jnp.zeros_like(dba_ref)

        eb = e_ref[...].astype(BF16)
        z = _dot(eb, wau_ref[...]) + ba_ref[...]
        _gla_decays(_log_sigmoid(z) * (1.0 / GLA_TAU), b_scr, dec_scr)
        lmask, hmask, bd = _gla_consts()
        last_row = _iota((CHUNK, 256), 0) == CHUNK - 1

        def chunk(cc, carry):
            c = nc - 1 - cc
            rows = pl.ds(pl.multiple_of(c * CHUNK, CHUNK), CHUNK)
            qc = q_ref[rows, :].astype(F32) * 0.125
            kc = k_ref[rows, :].astype(F32)
            vc = v_ref[rows, :]
            dob = do_ref[rows, :]
            bl, ep, en, ek, qp, qn, kn, kp, kk = _gla_chunk(b_scr[rows, :], qc, kc)
            decb = jnp.concatenate([dec_scr[c]] * 4, axis=1)
            qs = jnp.where(hmask, jnp.concatenate([qp] * 4, axis=0), 0.0).astype(BF16)
            qns = jnp.where(hmask, jnp.concatenate([qn] * 4, axis=0), 0.0).astype(BF16)
            knb, kpb = kn.astype(BF16), kp.astype(BF16)
            attn = jnp.where(lmask, _dot_nt(qs, knb), _dot_nt(qns, kpb)).astype(BF16)
            st = jnp.where(bd, jnp.concatenate([st_ref[c]] * 4, axis=0), 0.0)
            g = gstate[...]
            gb = g.astype(BF16)
            do_s = jnp.where(bd, jnp.concatenate([dob] * 4, axis=0), jnp.zeros((), BF16))
            dattn = _dot_nt(do_s, vc)
            dv_ref[rows, :] = (_dot_tn(attn, do_s) + _dot(kk.astype(BF16), gb)).astype(BF16)
            dac = jnp.where(lmask, dattn, 0.0).astype(BF16)
            daa = jnp.where(lmask, 0.0, dattn).astype(BF16)
            dqp = _fold_heads(jnp.where(hmask, _dot(dac, knb), 0.0)) + _dot_nt(dob, st.astype(BF16))
            dqn = _fold_heads(jnp.where(hmask, _dot(daa, kpb), 0.0))
            dkn = _dot_tn(dac, qs)
            dkp = _dot_tn(daa, qns)
            dkk = _dot_nt(vc, gb)
            ddec = _dot_nt(jnp.ones((8, 1536), BF16), _split3(g * st))[0:1, :]
            gstate[...] = decb * g + jnp.where(bd, _dot_tn(qp.astype(BF16), dob), 0.0)
            dq_ref[rows, :] = ((dqp * ep + dqn * en) * 0.125).astype(BF16)
            dk_ref[rows, :] = (dkn * en + dkp * ep + dkk * ek).astype(BF16)
            dek = dkk * kc * ek
            db = (dqp * qc + dkp * kc) * ep - (dqn * qc + dkn * kc) * en - dek
            dbl = jnp.sum(dek, axis=0, keepdims=True) + ddec * jnp.exp(bl)
            db_scr[rows, :] = db + jnp.where(last_row, dbl, 0.0)
            return carry

        lax.fori_loop(0, nc, chunk, 0)
        triu = (_iota((CHUNK, CHUNK), 0) <= _iota((CHUNK, CHUNK), 1)).astype(BF16)
        dla = jnp.concatenate([_sum3(_dot(triu, _split3(db_scr[CHUNK * c:CHUNK * (c + 1), :])), 1) for c in range(nc)], axis=0)
        dz = dla * (1.0 / GLA_TAU) * _sigmoid(-z)
        dzb = dz.astype(BF16)
        dwau_ref[...] += _dot_tn(eb, dzb)
        dba_ref[...] += _fold8(dz)
        de_ref[...] = _dot(dzb, waut_ref[...])

    rev = lambda i: nb - 1 - i
    return pl.pallas_call(
        body, grid=(nb,),
        in_specs=[pl.BlockSpec((t, 256), lambda i: (rev(i), C_GQ // 256)), pl.BlockSpec((t, 256), lambda i: (rev(i), C_GK // 256)),
                  pl.BlockSpec((t, 512), lambda i: (rev(i), C_GV // 512)), pl.BlockSpec((t, PE_W), lambda i: (rev(i), 0)),
                  pl.BlockSpec((PE_W, 256), lambda i: (0, 0)), pl.BlockSpec((256, PE_W), lambda i: (0, 0)),
                  pl.BlockSpec((1, 256), lambda i: (0, 0)), pl.BlockSpec((t, 512), lambda i: (rev(i), 0)),
                  pl.BlockSpec((nc, CHUNK, 512), lambda i: (rev(i), 0, 0))],
        out_specs=[pl.BlockSpec((t, 256), lambda i: (rev(i), 0)), pl.BlockSpec((t, 256), lambda i: (rev(i), 0)),
                   pl.BlockSpec((t, 512), lambda i: (rev(i), 0)), pl.BlockSpec((t, PE_W), lambda i: (rev(i), 0)),
                   pl.BlockSpec((PE_W, 256), lambda i: (0, 0)), pl.BlockSpec((8, 256), lambda i: (0, 0))],
        out_shape=[jax.ShapeDtypeStruct((s, 256), BF16), jax.ShapeDtypeStruct((s, 256), BF16),
                   jax.ShapeDtypeStruct((s, 512), BF16), jax.ShapeDtypeStruct((s, PE_W), F32),
                   jax.ShapeDtypeStruct((PE_W, 256), F32), jax.ShapeDtypeStruct((8, 256), F32)],
        scratch_shapes=[pltpu.VMEM((256, 512), F32), pltpu.VMEM((t, 256), F32), pltpu.VMEM((t, 256), F32),
                        pltpu.VMEM((nc, 256, 128), F32)],
        name="gla_bwd", compiler_params=_cp("arbitrary"))(pm, pm, pm, pe, wau_p, wau_pt, b_alpha, do, states)


def _fcum_fwd(pe, bias):
    s = pe.shape[0]
    t = min(s, 512)

    def body(e_ref, b_ref, f_ref, carry):
        @pl.when(pl.program_id(0) == 0)
        def _():
            carry[...] = jnp.zeros_like(carry)

        lf = _log_sigmoid(e_ref[...] + b_ref[...])
        tri = (_iota((t, t), 0) >= _iota((t, t), 1)).astype(BF16)
        f = _sum3(_dot(tri, _split3(lf)), 1) + carry[0:1, :]
        f_ref[...] = f
        carry[...] = jnp.broadcast_to(f[t - 1:t, :], carry.shape)

    return pl.pallas_call(
        body, grid=(s // t,),
        in_specs=[pl.BlockSpec((t, PE_W), lambda i: (i, 0)), pl.BlockSpec((1, PE_W), lambda i: (0, 0))],
        out_specs=pl.BlockSpec((t, PE_W), lambda i: (i, 0)),
        out_shape=jax.ShapeDtypeStruct((s, PE_W), F32), scratch_shapes=[pltpu.VMEM((8, PE_W), F32)],
        name="fcum_fwd", compiler_params=_cp("arbitrary"))(pe, bias)


def _fcum_bwd(pe, bias, df):
    s = pe.shape[0]
    t = min(s, 512)
    nb = s // t

    def body(e_ref, b_ref, df_ref, de_ref, db_ref, carry):
        @pl.when(pl.program_id(0) == 0)
        def _():
            carry[...] = jnp.zeros_like(carry)
            db_ref[...] = jnp.zeros_like(db_ref)

        triu = (_iota((t, t), 0) <= _iota((t, t), 1)).astype(BF16)
        dlf = _sum3(_dot(triu, _split3(df_ref[...])), 1) + carry[0:1, :]
        carry[...] = jnp.broadcast_to(dlf[0:1, :], carry.shape)
        lane = _iota((t, PE_W), 1)
        dff = jnp.where((lane >= FF_LANE) & (lane < FF_LANE + 8), dlf * _sigmoid(-(e_ref[...] + b_ref[...])), 0.0)
        de_ref[...] = dff
        db_ref[...] += _fold8(dff)

    rev = lambda i: (nb - 1 - i, 0)
    return pl.pallas_call(
        body, grid=(nb,),
        in_specs=[pl.BlockSpec((t, PE_W), rev), pl.BlockSpec((1, PE_W), lambda i: (0, 0)), pl.BlockSpec((t, PE_W), rev)],
        out_specs=[pl.BlockSpec((t, PE_W), rev), pl.BlockSpec((8, PE_W), lambda i: (0, 0))],
        out_shape=[jax.ShapeDtypeStruct((s, PE_W), F32), jax.ShapeDtypeStruct((8, PE_W), F32)],
        scratch_shapes=[pltpu.VMEM((8, PE_W), F32)],
        name="fcum_bwd", compiler_params=_cp("arbitrary"))(pe, bias, df)


FOX_WIDE = 1024


def _split3(x):
    hi = x.astype(BF16)
    r = x - hi.astype(F32)
    mid = r.astype(BF16)
    lo = (r - mid.astype(F32)).astype(BF16)
    return jnp.concatenate([hi, mid, lo], axis=1)


def _sum3(x, axis):
    n = x.shape[axis] // 3
    parts = [lax.slice_in_dim(x, n * p, n * (p + 1), axis=axis) for p in range(3)]
    return (parts[0] + parts[1]) + parts[2]


def _spread(x, sp):
    return jnp.concatenate([_dot(x[:, 128 * g:128 * (g + 1)], sp) for g in range(4)], axis=1)


def _fox_tables():
    heads, lane = np.arange(8), np.arange(64)
    spread = np.zeros((128, 256), np.float32)
    spread[(64 * heads[:2, None] + lane).ravel(), (128 * heads[:2, None] + lane).ravel()] = 1.0
    def place(src_lane0, dst_off, val):
        t = np.zeros((384, 1024), np.float32)
        for p in range(3):
            t[128 * p + src_lane0 + heads, 128 * heads + dst_off + p] = val
        return t
    def const(off, val):
        c = np.zeros((1, 1024), np.float32)
        for p in range(3):
            c[0, 128 * heads + off + p] = val
        return c
    rows = np.zeros((8, 128), np.float32)
    rows[heads, FF_LANE + heads] = 1.0
    bf = lambda a: jnp.asarray(a, BF16)
    return dict(spread=bf(spread),
                f_to_q=bf(place(FF_LANE, 64, 1.0)), f_to_k=bf(place(FF_LANE, 67, -1.0)), d_to_do=bf(place(0, 64, 1.0)),
                ones_q=jnp.asarray(const(67, 1.0)), ones_k=jnp.asarray(const(64, 1.0)), ones_v=jnp.asarray(const(64, -1.0)),
                rows=jnp.asarray(rows))


LOG2E = 1.4426950408889634


def _fox_prep(pm, f128, lse8, tb, *, backward):
    s = pm.shape[0]
    tm = min(s, 1024) if backward else _row_tile(s)

    def body(*refs):
        if backward:
            q_ref, f_ref, lse_ref, sp_ref, fq_ref, cq_ref, rows_ref, qa_ref = refs
            f = f_ref[...] * LOG2E - _dot_tn(lse_ref[...], rows_ref[...], precision=HI)
            q2 = (q_ref[...].astype(F32) * (0.125 * LOG2E)).astype(BF16)
            qa_ref[...] = (_spread(q2, sp_ref[...]) + _dot(_split3(f), fq_ref[...]) + cq_ref[...]).astype(BF16)
            return
        (q_ref, k_ref, v_ref, f_ref, sp_ref, fq_ref, fk_ref, cq_ref, ck_ref, cv_ref,
         qa_ref, ka_ref, va_ref, vt_ref, qt_ref, kt_ref) = refs
        f3 = _split3(f_ref[...] * LOG2E)
        q, k, v = q_ref[...].astype(F32), k_ref[...], v_ref[...]
        sp = sp_ref[...]
        qa_ref[...] = (_spread((q * (0.125 * LOG2E)).astype(BF16), sp) + _dot(f3, fq_ref[...]) + cq_ref[...]).astype(BF16)
        ka_ref[...] = (_spread(k, sp) + _dot(f3, fk_ref[...]) + ck_ref[...]).astype(BF16)
        va_ref[...] = (_spread(v, sp) + cv_ref[...]).astype(BF16)
        vt_ref[...] = v.T
        qt_ref[...] = (q * 0.125).astype(BF16).T
        kt_ref[...] = (k.astype(F32) * 0.125).astype(BF16).T

    row = lambda i: (i, 0)
    const = lambda i: (0, 0)
    blk = lambda c: pl.BlockSpec((tm, 512), lambda i: (i, c // 512))
    wide = pl.BlockSpec((tm, 1024), row)
    mat = lambda a: pl.BlockSpec(a.shape, const)
    if backward:
        ins = [pm, f128, lse8, tb["spread"], tb["f_to_q"], tb["ones_q"], tb["rows"]]
        in_specs = [blk(C_FQ), pl.BlockSpec((tm, 128), row), pl.BlockSpec((8, tm), lambda i: (0, i))] + [mat(a) for a in ins[3:]]
        out_specs, out_shape = wide, jax.ShapeDtypeStruct((s, 1024), BF16)
    else:
        ins = [pm, pm, pm, f128, tb["spread"], tb["f_to_q"], tb["f_to_k"], tb["ones_q"], tb["ones_k"], tb["ones_v"]]
        in_specs = [blk(C_FQ), blk(C_FK), blk(C_FV), pl.BlockSpec((tm, 128), row)] + [mat(a) for a in ins[4:]]
        tr = pl.BlockSpec((512, tm), lambda i: (0, i))
        out_specs = [wide, wide, wide, tr, tr, tr]
        out_shape = [jax.ShapeDtypeStruct((s, 1024), BF16)] * 3 + [jax.ShapeDtypeStruct((512, s), BF16)] * 3
    return pl.pallas_call(body, grid=(s // tm,), in_specs=in_specs, out_specs=out_specs, out_shape=out_shape,
                          name="fox_prep_bwd" if backward else "fox_prep", compiler_params=_cp("parallel"))(*ins)


def _fox_post(dqt, dkt, dvt, rowsum8, colsum8, tb):
    s = dqt.shape[1]
    tm = min(s, 1024)

    def body(dqt_ref, dkt_ref, dvt_ref, rs_ref, cs_ref, rows_ref, dfq_ref, dfk_ref, dfv_ref, df_ref):
        dfq_ref[...] = dqt_ref[...].T.astype(BF16)
        dfk_ref[...] = dkt_ref[...].T
        dfv_ref[...] = dvt_ref[...].T
        df_ref[...] = _dot_tn(rs_ref[...] - cs_ref[...], rows_ref[...], precision=HI)

    row = lambda i: (i, 0)
    tr = pl.BlockSpec((512, tm), lambda i: (0, i))
    out = pl.BlockSpec((tm, 512), row)
    heads = pl.BlockSpec((8, tm), lambda i: (0, i))
    return pl.pallas_call(
        body, grid=(s // tm,),
        in_specs=[tr, tr, tr, heads, heads, pl.BlockSpec((8, 128), lambda i: (0, 0))],
        out_specs=[out, out, out, pl.BlockSpec((tm, 128), row)],
        out_shape=[jax.ShapeDtypeStruct((s, 512), BF16)] * 3 + [jax.ShapeDtypeStruct((s, 128), F32)],
        name="fox_post", compiler_params=_cp("parallel"))(dqt, dkt, dvt, rowsum8, colsum8, tb["rows"])


def _fox_fwd(k_aug, q_aug, vt):
    s = k_aug.shape[0]
    nh = 8
    tk = _row_tile(s)
    tq = min(s, 2 * FOX_WIDE)
    per = tq // tk

    def body(k_ref, q_ref, v_ref, o_ref, lse_ref, sbuf):
        i = pl.program_id(1)
        qa = q_ref[...]

        def scores(j):
            return _dot_nt(k_ref[pl.ds(pl.multiple_of(j * tk, tk), tk), :], qa)

        ones_row = (_iota((16, tk), 0) == 0).astype(BF16)

        def update(st, j, carry):
            m, acc = carry
            m2 = jnp.maximum(m, jnp.max(st, axis=0, keepdims=True))
            p = jnp.exp2(st - m2)
            vj = jnp.concatenate([v_ref[:, pl.ds(pl.multiple_of(j * tk, tk), tk)], ones_row], axis=0)
            return m2, jnp.exp2(m - m2) * acc + _dot(vj, p.astype(BF16))

        def step(a, carry):
            sbuf[1] = scores(2 * a + 1)
            carry = update(sbuf[0], 2 * a, carry)
            sbuf[0] = scores(2 * a + 2)
            return update(sbuf[1], 2 * a + 1, carry)

        n = i * per
        sbuf[0] = scores(0)
        carry = (jnp.full((1, tq), -1e30, F32), jnp.zeros((80, tq), F32))
        carry = lax.fori_loop(0, n // 2, step, carry)
        tri = _iota((tk, tk), 0) <= _iota((tk, tk), 1)
        late = [_dot_nt(k_ref[pl.ds(pl.multiple_of((n + r) * tk, tk), tk), :], qa[r * tk:, :]) for r in range(1, per)]
        for r in range(per):
            st = sbuf[0] if r == 0 else late[r - 1]
            head = jnp.where(tri, st[:, :tk], -1e30)
            st = head if st.shape[1] == tk else jnp.concatenate([head, st[:, tk:]], axis=1)
            part = update(st, n + r, tuple(c[:, r * tk:] for c in carry))
            carry = part if r == 0 else tuple(jnp.concatenate([old[:, :r * tk], new], axis=1) for old, new in zip(carry, part))
        m, acc = carry
        l = acc[64:65]
        o_ref[...] = (acc[0:64] / l).astype(BF16)
        lse_ref[0] = m + jnp.log2(l)

    return pl.pallas_call(
        body, grid=(nh, s // tq),
        in_specs=[pl.BlockSpec((s, 128), lambda h, i: (0, h)), pl.BlockSpec((tq, 128), lambda h, i: (i, h)),
                  pl.BlockSpec((64, s), lambda h, i: (h, 0))],
        out_specs=[pl.BlockSpec((64, tq), lambda h, i: (h, i)), pl.BlockSpec((1, 1, tq), lambda h, i: (h, 0, i))],
        out_shape=[jax.ShapeDtypeStruct((512, s), BF16), jax.ShapeDtypeStruct((nh, 1, s), F32)],
        scratch_shapes=[pltpu.VMEM((2, tk, tq), F32)],
        name="fox_fwd", compiler_params=_cp("parallel", "arbitrary"))(k_aug, q_aug, vt)


def _fox_bwd(q_aug, do_aug, qt, dot_, k_aug, v_aug, kt):
    s = q_aug.shape[0]
    nh = 8
    tq = _row_tile(s)
    tk = min(s, 2 * FOX_WIDE)
    per = tk // tq
    nqb = s // tq

    def body(qa_ref, da_ref, qt_ref, dt_ref, ka_ref, va_ref, kt_ref, dq_ref, rs_ref, dk_ref, dv_ref, dfk_ref):
        j = pl.program_id(1)

        @pl.when(j == 0)
        def _():
            dq_ref[...] = jnp.zeros_like(dq_ref)
            rs_ref[...] = jnp.zeros_like(rs_ref)

        ones_row = (_iota((16, tk), 0) == 0).astype(BF16)
        ka, va = ka_ref[...], va_ref[...]
        ks = jnp.concatenate([kt_ref[...], ones_row], axis=0)
        tri = _iota((tq, tq), 0) >= _iota((tq, tq), 1)

        def tile(i, w, carry):
            masked = w is not None
            w = tk if w is None else w
            rows = pl.ds(pl.multiple_of(i * tq, tq), tq)
            sp = _dot_nt(qa_ref[rows, :], ka[:w])
            if masked:
                last = jnp.where(tri, sp[:, w - tq:], -1e30)
                sp = last if w == tq else jnp.concatenate([sp[:, :w - tq], last], axis=1)
            p = jnp.exp2(sp)
            dsb = (p * _dot_nt(da_ref[rows, :], va[:w])).astype(BF16)
            dq = _dot_nt(ks[:, :w], dsb)
            dq_ref[:, rows] += dq[0:64]
            rs_ref[0, :, rows] += dq[64:72]
            new = (_dot(jnp.concatenate([qt_ref[:, rows], ones_row[:, :tq]], axis=0), dsb), _dot(dt_ref[:, rows], p.astype(BF16)))
            if w == tk:
                return tuple(c + d for c, d in zip(carry, new))
            return tuple(jnp.concatenate([c[:, :w] + d, c[:, w:]], axis=1) for c, d in zip(carry, new))

        carry = (jnp.zeros((80, tk), F32), jnp.zeros((64, tk), F32))
        for r in range(per):
            carry = tile(j * per + r, (r + 1) * tq, carry)
        dk, dv = lax.fori_loop((j + 1) * per, nqb, lambda i, c: tile(i, None, c), carry)
        dk_ref[...] = dk[0:64].astype(BF16)
        dv_ref[...] = dv.astype(BF16)
        dfk_ref[0] = dk[64:65]

    head_cols = lambda h, j: (0, h)
    head_rows = lambda h, j: (h, 0)
    once = dict(pipeline_mode=pl.Buffered(1))
    return pl.pallas_call(
        body, grid=(nh, s // tk),
        in_specs=[pl.BlockSpec((s, 128), head_cols, **once), pl.BlockSpec((s, 128), head_cols, **once),
                  pl.BlockSpec((64, s), head_rows, **once), pl.BlockSpec((64, s), head_rows, **once),
                  pl.BlockSpec((tk, 128), lambda h, j: (j, h)), pl.BlockSpec((tk, 128), lambda h, j: (j, h)),
                  pl.BlockSpec((64, tk), lambda h, j: (h, j))],
        out_specs=[pl.BlockSpec((64, s), head_rows), pl.BlockSpec((1, 8, s), lambda h, j: (h, 0, 0)),
                   pl.BlockSpec((64, tk), lambda h, j: (h, j)),
                   pl.BlockSpec((64, tk), lambda h, j: (h, j)), pl.BlockSpec((1, 1, tk), lambda h, j: (h, 0, j))],
        out_shape=[jax.ShapeDtypeStruct((512, s), F32), jax.ShapeDtypeStruct((nh, 8, s), F32),
                   jax.ShapeDtypeStruct((512, s), BF16),
                   jax.ShapeDtypeStruct((512, s), BF16), jax.ShapeDtypeStruct((nh, 1, s), F32)],
        name="fox_bwd", compiler_params=_cp("parallel", "arbitrary"))(q_aug, do_aug, qt, dot_, k_aug, v_aug, kt)


MEM_SCALE = 128 ** -0.5


def _mem_attn_fwd(pm, mkv):
    s = pm.shape[0]
    t = min(s, 1024)
    nm = mkv.shape[0]

    def body(q_ref, mk_ref, mv_ref, o_ref):
        for h in range(4):
            cols = slice(128 * h, 128 * (h + 1))
            sc = _dot_nt(q_ref[:, cols], mk_ref[:, cols]) * MEM_SCALE
            p = jnp.exp(sc - jnp.max(sc, axis=-1, keepdims=True))
            p = p / jnp.sum(p, axis=-1, keepdims=True)
            o_ref[:, cols] = _dot(p.astype(BF16), mv_ref[:, cols]).astype(BF16)

    return pl.pallas_call(
        body, grid=(s // t,),
        in_specs=[pl.BlockSpec((t, 512), lambda i: (i, C_MQ // 512)), pl.BlockSpec((nm, 512), lambda i: (0, 0)),
                  pl.BlockSpec((nm, 512), lambda i: (0, 1))],
        out_specs=pl.BlockSpec((t, 512), lambda i: (i, 0)),
        out_shape=jax.ShapeDtypeStruct((s, 512), BF16),
        name="mem_attn_fwd", compiler_params=_cp("parallel"))(pm, mkv, mkv)


def _mem_attn_bwd(pm, mkv, do):
    s = pm.shape[0]
    t = min(s, 1024)
    nm = mkv.shape[0]

    def body(q_ref, mk_ref, mv_ref, do_ref, dq_ref, dmk_ref, dmv_ref):
        @pl.when(pl.program_id(0) == 0)
        def _():
            dmk_ref[...] = jnp.zeros_like(dmk_ref)
            dmv_ref[...] = jnp.zeros_like(dmv_ref)

        for h in range(4):
            cols = slice(128 * h, 128 * (h + 1))
            qh, kh, vh, doh = q_ref[:, cols], mk_ref[:, cols], mv_ref[:, cols], do_ref[:, cols]
            sc = _dot_nt(qh, kh) * MEM_SCALE
            p = jnp.exp(sc - jnp.max(sc, axis=-1, keepdims=True))
            p = p / jnp.sum(p, axis=-1, keepdims=True)
            pb = p.astype(BF16)
            dp = _dot_nt(doh, vh)
            ds = (p * (dp - jnp.sum(p * dp, axis=-1, keepdims=True)) * MEM_SCALE).astype(BF16)
            dq_ref[:, cols] = _dot(ds, kh).astype(BF16)
            dmk_ref[:, cols] += _dot_tn(ds, qh)
            dmv_ref[:, cols] += _dot_tn(pb, doh)

    return pl.pallas_call(
        body, grid=(s // t,),
        in_specs=[pl.BlockSpec((t, 512), lambda i: (i, C_MQ // 512)), pl.BlockSpec((nm, 512), lambda i: (0, 0)),
                  pl.BlockSpec((nm, 512), lambda i: (0, 1)), pl.BlockSpec((t, 512), lambda i: (i, 0))],
        out_specs=[pl.BlockSpec((t, 512), lambda i: (i, 0)), pl.BlockSpec((nm, 512), lambda i: (0, 0)),
                   pl.BlockSpec((nm, 512), lambda i: (0, 0))],
        out_shape=[jax.ShapeDtypeStruct((s, 512), BF16), jax.ShapeDtypeStruct((nm, 512), F32),
                   jax.ShapeDtypeStruct((nm, 512), F32)],
        name="mem_attn_bwd", compiler_params=_cp("arbitrary"))(pm, mkv, mkv, do)


def _gain_grad(dxn_g, x, r, name):
    m, d = x.shape

    def body(d_ref, x_ref, r_ref, o_ref):
        o_ref[...] = _fold8(d_ref[...] * (x_ref[...] * r_ref[...]))

    return pl.pallas_call(body, out_shape=jax.ShapeDtypeStruct((8, d), F32), name=name,
                          compiler_params=pltpu.CompilerParams(vmem_limit_bytes=VMEM_LIMIT_BYTES))(dxn_g, x, r)


def _head_norm(o, gh):
    xs, rs = [], []
    for h in range(4):
        oh = o[:, 128 * h:128 * (h + 1)]
        r = lax.rsqrt(jnp.mean(oh * oh, axis=-1, keepdims=True) + EPS)
        xs.append(oh * r)
        rs.append(r)
    return xs, rs


def _merge_fwd(x, pm, o_gla, o_fox_t, o_mem, g_head, wg, wf, wm, wo, g_ffn):
    s = x.shape[0]
    t = min(s, 512)

    def body(x_ref, g0_ref, g1_ref, g2_ref, gg_ref, og_ref, of_ref, om_ref, gh_ref, wg_ref, wf_ref, wm_ref, wo_ref, gf_ref,
             mg_ref, h1_ref, u2_ref, r2_ref):
        xs, _ = _head_norm(og_ref[...], None)
        gg = gg_ref[...].astype(F32)
        sil = gg * _sigmoid(gg)
        ogn = jnp.concatenate(xs, axis=1) * gh_ref[...] * sil
        merged = (_sigmoid(g0_ref[...].astype(F32)) * _dot(ogn.astype(BF16), wg_ref[...])
                  + _sigmoid(g1_ref[...].astype(F32)) * _dot(of_ref[...].T, wf_ref[...])
                  + _sigmoid(g2_ref[...].astype(F32)) * _dot(om_ref[...], wm_ref[...]))
        mb = merged.astype(BF16)
        mg_ref[...] = mb
        h1 = x_ref[...] + _dot(mb, wo_ref[...])
        h1_ref[...] = h1
        r = lax.rsqrt(jnp.mean(h1 * h1, axis=-1, keepdims=True) + EPS)
        u2_ref[...] = ((h1 * r) * gf_ref[...]).astype(BF16)
        r2_ref[...] = r

    row = lambda i: (i, 0)
    const = lambda i: (0, 0)
    return pl.pallas_call(
        body, grid=(s // t,),
        in_specs=[pl.BlockSpec((t, D), row), pl.BlockSpec((t, D), lambda i: (i, 0)), pl.BlockSpec((t, D), lambda i: (i, 1)),
                  pl.BlockSpec((t, D), lambda i: (i, 2)), pl.BlockSpec((t, 512), lambda i: (i, C_GG // 512)),
                  pl.BlockSpec((t, 512), row), pl.BlockSpec((512, t), lambda i: (0, i)), pl.BlockSpec((t, 512), row),
                  pl.BlockSpec((1, 512), const), pl.BlockSpec((512, D), const), pl.BlockSpec((512, D), const),
                  pl.BlockSpec((512, D), const), pl.BlockSpec((D, D), const), pl.BlockSpec((1, D), const)],
        out_specs=[pl.BlockSpec((t, D), row), pl.BlockSpec((t, D), row), pl.BlockSpec((t, D), row), pl.BlockSpec((t, 1), row)],
        out_shape=[jax.ShapeDtypeStruct((s, D), BF16), jax.ShapeDtypeStruct((s, D), F32),
                   jax.ShapeDtypeStruct((s, D), BF16), jax.ShapeDtypeStruct((s, 1), F32)],
        name="merge_fwd", compiler_params=_cp("parallel"))(x, pm, pm, pm, pm, o_gla, o_fox_t, o_mem, g_head, wg, wf, wm, wo, g_ffn)


def _merge_bwd(dh1b, pm, o_gla, o_fox_t, o_mem, g_head, wg, wf, wm, wgt, wft, wmt, wot, spread, d_to_do):
    s = dh1b.shape[0]
    t = min(s, 256)

    def body(dh_ref, g0_ref, g1_ref, g2_ref, gg_ref, og_ref, of_ref, om_ref, gh_ref, wg_ref, wf_ref, wm_ref,
             wgt_ref, wft_ref, wmt_ref, wot_ref, sp_ref, dd_ref,
             dgt_ref, dgg_ref, dog_ref, da_ref, dot_ref, dom_ref, dwg_ref, dwf_ref, dwm_ref, dgh_ref):
        @pl.when(pl.program_id(0) == 0)
        def _():
            dwg_ref[...] = jnp.zeros_like(dwg_ref)
            dwf_ref[...] = jnp.zeros_like(dwf_ref)
            dwm_ref[...] = jnp.zeros_like(dwm_ref)
            dgh_ref[...] = jnp.zeros_like(dgh_ref)

        dmerged = _dot(dh_ref[...], wot_ref[...])
        og = og_ref[...]
        xs, rs = _head_norm(og, None)
        on = jnp.concatenate(xs, axis=1)
        gg = gg_ref[...].astype(F32)
        sg = _sigmoid(gg)
        sil = gg * sg
        gh = gh_ref[...]
        ognb = (on * gh * sil).astype(BF16)
        ofb, omb = of_ref[...].T, om_ref[...]
        douts = []
        for idx, (gref, ob, w_ref, wt_ref, dw_ref) in enumerate((
                (g0_ref, ognb, wg_ref, wgt_ref, dwg_ref), (g1_ref, ofb, wf_ref, wft_ref, dwf_ref),
                (g2_ref, omb, wm_ref, wmt_ref, dwm_ref))):
            gt = _sigmoid(gref[...].astype(F32))
            y = _dot(ob, w_ref[...])
            dgt_ref[:, D * idx:D * (idx + 1)] = (dmerged * y * gt * (1.0 - gt)).astype(BF16)
            dy = (gt * dmerged).astype(BF16)
            dw_ref[...] += _dot_tn(ob, dy)
            douts.append(_dot(dy, wt_ref[...]))
        dogn, dof, dom = douts
        dofb = dof.astype(BF16)
        dom_ref[...] = dom.astype(BF16)
        ind = (_iota((1536, 128), 0) % 512 // 64 == _iota((1536, 128), 1)).astype(BF16)
        delta = _dot(_split3(dofb.astype(F32) * ofb.astype(F32)), ind)
        da_ref[...] = (_spread(dofb, sp_ref[...]) + _dot(_split3(delta), dd_ref[...])).astype(BF16)
        dot_ref[...] = dofb.T
        dgg_ref[...] = (dogn * on * gh * (sg * (1.0 + gg * (1.0 - sg)))).astype(BF16)
        d_on = dogn * sil
        dgh_ref[...] += _fold8(d_on * on)
        dxn = d_on * gh
        outs = []
        for h in range(4):
            cols = slice(128 * h, 128 * (h + 1))
            dh_, xh = dxn[:, cols], xs[h]
            outs.append(rs[h] * (dh_ - xh * jnp.mean(dh_ * xh, axis=-1, keepdims=True)))
        dog_ref[...] = jnp.concatenate(outs, axis=1).astype(BF16)

    row = lambda i: (i, 0)
    const = lambda i: (0, 0)
    return pl.pallas_call(
        body, grid=(s // t,),
        in_specs=[pl.BlockSpec((t, D), row), pl.BlockSpec((t, D), lambda i: (i, 0)), pl.BlockSpec((t, D), lambda i: (i, 1)),
                  pl.BlockSpec((t, D), lambda i: (i, 2)), pl.BlockSpec((t, 512), lambda i: (i, C_GG // 512)),
                  pl.BlockSpec((t, 512), row), pl.BlockSpec((512, t), lambda i: (0, i)), pl.BlockSpec((t, 512), row),
                  pl.BlockSpec((1, 512), const), pl.BlockSpec((512, D), const), pl.BlockSpec((512, D), const),
                  pl.BlockSpec((512, D), const), pl.BlockSpec((D, 512), const), pl.BlockSpec((D, 512), const),
                  pl.BlockSpec((D, 512), const), pl.BlockSpec((D, D), const),
                  pl.BlockSpec((128, 256), const), pl.BlockSpec((384, 1024), const)],
        out_specs=[pl.BlockSpec((t, 3 * D), row), pl.BlockSpec((t, 512), row), pl.BlockSpec((t, 512), row),
                   pl.BlockSpec((t, 1024), row), pl.BlockSpec((512, t), lambda i: (0, i)), pl.BlockSpec((t, 512), row),
                   pl.BlockSpec((512, D), const), pl.BlockSpec((512, D), const), pl.BlockSpec((512, D), const),
                   pl.BlockSpec((8, 512), const)],
        out_shape=[jax.ShapeDtypeStruct((s, 3 * D), BF16), jax.ShapeDtypeStruct((s, 512), BF16),
                   jax.ShapeDtypeStruct((s, 512), BF16), jax.ShapeDtypeStruct((s, 1024), BF16),
                   jax.ShapeDtypeStruct((512, s), BF16), jax.ShapeDtypeStruct((s, 512), BF16),
                   jax.ShapeDtypeStruct((512, D), F32), jax.ShapeDtypeStruct((512, D), F32),
                   jax.ShapeDtypeStruct((512, D), F32), jax.ShapeDtypeStruct((8, 512), F32)],
        name="merge_bwd", compiler_params=_cp("arbitrary"))(
            dh1b, pm, pm, pm, pm, o_gla, o_fox_t, o_mem, g_head, wg, wf, wm, wgt, wft, wmt, wot, spread, d_to_do)


def _ff2_loss(a, w2, h1, g_final, target):
    s, k = a.shape
    tm = min(s, 512)

    def body(a_ref, w_ref, h1_ref, g_ref, t_ref, dh_ref, dhb_ref, loss_ref, dg_ref):
        @pl.when(pl.program_id(0) == 0)
        def _():
            loss_ref[...] = jnp.zeros_like(loss_ref)
            dg_ref[...] = jnp.zeros_like(dg_ref)

        h2 = h1_ref[...] + _dot(_relu2_bf16(a_ref[...]), w_ref[...])
        r = lax.rsqrt(jnp.mean(h2 * h2, axis=-1, keepdims=True) + EPS)
        xn = h2 * r
        g = g_ref[...]
        err = xn * g - t_ref[...]
        e2 = _fold8(err * err)
        part = e2[:, 0:128]
        for c in range(1, D // 128):
            part = part + e2[:, 128 * c:128 * (c + 1)]
        loss_ref[...] += part
        dy = err * (1.0 / D)
        dg_ref[...] += _fold8(dy * xn)
        dxn = dy * g
        dh = r * (dxn - xn * jnp.mean(dxn * xn, axis=-1, keepdims=True))
        dh_ref[...] = dh
        dhb_ref[...] = dh.astype(BF16)

    row = lambda i: (i, 0)
    const = lambda i: (0, 0)
    return pl.pallas_call(
        body, grid=(s // tm,),
        in_specs=[pl.BlockSpec((tm, k), row), pl.BlockSpec((k, D), const, pipeline_mode=pl.Buffered(1)),
                  pl.BlockSpec((tm, D), row), pl.BlockSpec((1, D), const), pl.BlockSpec((tm, D), row)],
        out_specs=[pl.BlockSpec((tm, D), row), pl.BlockSpec((tm, D), row), pl.BlockSpec((8, 128), const),
                   pl.BlockSpec((8, D), const)],
        out_shape=[jax.ShapeDtypeStruct((s, D), F32), jax.ShapeDtypeStruct((s, D), BF16),
                   jax.ShapeDtypeStruct((8, 128), F32), jax.ShapeDtypeStruct((8, D), F32)],
        name="ff2_loss", compiler_params=_cp("arbitrary"))(a, w2, h1, g_final, target)


def _adam(w, g, m, v, name):
    _, r, c = w.shape
    tr = r
    for cand in (512, 256, 128, 64, 32, 16, 8):
        if r % cand == 0 and cand * c * 4 <= (1 << 20):
            tr = cand
            break
    c1 = 1.0 - ADAM_B1 ** ADAM_STEP
    c2 = 1.0 - ADAM_B2 ** ADAM_STEP

    def body(w_ref, g_ref, m_ref, v_ref, d_ref, nm_ref, nv_ref):
        gv = g_ref[...]
        nm = ADAM_B1 * m_ref[...] + (1.0 - ADAM_B1) * gv
        nv = ADAM_B2 * v_ref[...] + (1.0 - ADAM_B2) * (gv * gv)
        d_ref[...] = -ADAM_LR * ((nm / c1) / (jnp.sqrt(nv / c2) + ADAM_EPS) + ADAM_WD * w_ref[...])
        nm_ref[...] = nm
        nv_ref[...] = nv

    spec = pl.BlockSpec((1, tr, c), lambda i: (0, i, 0))
    return pl.pallas_call(
        body, grid=(r // tr,), in_specs=[spec] * 4, out_specs=[spec] * 3,
        out_shape=[jax.ShapeDtypeStruct((1, r, c), F32)] * 3, name=name, compiler_params=_cp("parallel"))(w, g, m, v)


def _row_block(r):
    return max(d for d in range(16, 513, 16) if r % d == 0)


def _add_half(core, a, b, name):
    n, r, c = b.shape
    tr = _row_block(r)

    def body(core_ref, a_ref, b_ref, o_ref):
        o_ref[...] = (a_ref[...].astype(F32) + b_ref[...].astype(F32)).astype(BF16)

    spec = pl.BlockSpec((1, tr, c), lambda k, i, core_ref: (k, i, 0))
    half = pl.BlockSpec((1, tr, c), lambda k, i, core_ref: (k, i + core_ref[0] * (r // tr), 0))
    return pl.pallas_call(
        body, grid_spec=pltpu.PrefetchScalarGridSpec(num_scalar_prefetch=1, grid=(n, r // tr), in_specs=[half, spec],
                                                     out_specs=spec),
        out_shape=jax.ShapeDtypeStruct((n, r, c), BF16), name=name, compiler_params=_cp("parallel", "parallel"))(core, a, b)


def _sum4(a, name):
    _, r, c = a.shape
    tr = _row_block(r)

    def body(a_ref, o_ref):
        o_ref[...] = ((a_ref[0].astype(F32) + a_ref[1].astype(F32)) + a_ref[2].astype(F32)) + a_ref[3].astype(F32)

    return pl.pallas_call(body, grid=(r // tr,), in_specs=[pl.BlockSpec((4, tr, c), lambda i: (0, i, 0))],
                          out_specs=pl.BlockSpec((tr, c), lambda i: (i, 0)),
                          out_shape=jax.ShapeDtypeStruct((r, c), F32), name=name, compiler_params=_cp("parallel"))(a)


def _adam_small(w, gathered, m, v):
    c1 = 1.0 - ADAM_B1 ** ADAM_STEP
    c2 = 1.0 - ADAM_B2 ** ADAM_STEP

    def body(w_ref, g_ref, m_ref, v_ref, gs_ref, d_ref, nm_ref, nv_ref):
        gv = g_ref[0]
        for dev in range(1, N_DEV):
            gv = gv + g_ref[dev]
        gs_ref[...] = gv
        nm = ADAM_B1 * m_ref[...] + (1.0 - ADAM_B1) * gv
        nv = ADAM_B2 * v_ref[...] + (1.0 - ADAM_B2) * (gv * gv)
        d_ref[...] = -ADAM_LR * ((nm / c1) / (jnp.sqrt(nv / c2) + ADAM_EPS) + ADAM_WD * w_ref[...])
        nm_ref[...] = nm
        nv_ref[...] = nv

    return pl.pallas_call(body, out_shape=[jax.ShapeDtypeStruct((8, D), F32)] * 4, name="adam_small")(w, gathered, m, v)


def _place():
    return lax.axis_index("x"), lax.axis_index("y"), lax.axis_index("c")


def _other_chips(x, y):
    return [(1 - x, y), (x, 1 - y), (1 - x, 1 - y)]


GATHER_SEMS = [pltpu.SemaphoreType.DMA((6,)), pltpu.SemaphoreType.DMA((6,)), pltpu.SemaphoreType.DMA]


def _gather_ops(in_refs, out_refs, sems):
    (p_ref,), (out_ref,) = in_refs, out_refs
    send_sems, recv_sems, local_sem = sems
    hr = p_ref.shape[0] // 2
    x, y, cc = _place()
    sibling = (x, y, 1 - cc)
    chips = _other_chips(x, y)

    def half(chip, core):
        return out_ref.at[2 * chip[0] + chip[1], pl.ds(core * hr, hr), :]

    def copy(k, chip, core, to, src=None):
        return pltpu.make_async_remote_copy(
            src_ref=half(chip, core) if src is None else src, dst_ref=half(chip, core),
            send_sem=send_sems.at[k], recv_sem=recv_sems.at[k], device_id=to, device_id_type=MESH)

    mine = pltpu.make_async_copy(p_ref, out_ref.at[2 * x + y], local_sem)
    my_half = p_ref.at[pl.ds(cc * hr, hr), :]
    first = [copy(j, (x, y), cc, (*chip, cc), src=my_half) for j, chip in enumerate(chips)]
    passed = [copy(3 + j, chip, cc, sibling) for j, chip in enumerate(chips)]

    def start():
        mine.start()
        for cp in first:
            cp.start()

    def finish():
        for j, chip in enumerate(chips):
            copy(j, chip, cc, (x, y, cc)).wait_recv()
            passed[j].start()
        for j, chip in enumerate(chips):
            copy(3 + j, chip, 1 - cc, (x, y, cc)).wait_recv()
        for cp in first + passed:
            cp.wait_send()
        mine.wait()

    return start, finish


def _gather_side(p):
    return _Side([p], [jax.ShapeDtypeStruct((N_CHIPS,) + p.shape, p.dtype)], GATHER_SEMS, _gather_ops)


def _swap_halves(g):
    n, r, c = g.shape
    hr = r // 2

    def body(g_ref, out_ref, send_sem, recv_sem):
        x, y, cc = _place()
        cp = pltpu.make_async_remote_copy(
            src_ref=g_ref.at[:, pl.ds((1 - cc) * hr, hr), :], dst_ref=out_ref,
            send_sem=send_sem, recv_sem=recv_sem, device_id=(x, y, 1 - cc), device_id_type=MESH)
        cp.start()
        cp.wait()

    any_spec = pl.BlockSpec(memory_space=pl.ANY)
    return pl.pallas_call(
        body, out_shape=jax.ShapeDtypeStruct((n, hr, c), g.dtype), in_specs=[any_spec], out_specs=any_spec,
        scratch_shapes=[pltpu.SemaphoreType.DMA, pltpu.SemaphoreType.DMA], name="swap_halves")(g)


SCATTER_SEMS = [pltpu.SemaphoreType.DMA((7,)), pltpu.SemaphoreType.DMA((7,)), pltpu.SemaphoreType.DMA]


def _scatter_ops(in_refs, out_refs, sems):
    (p_ref,), (out_ref,) = in_refs, out_refs
    send_sems, recv_sems, local_sem = sems
    hr = p_ref.shape[1]
    x, y, cc = _place()
    me = 2 * x + y
    sibling = (x, y, 1 - cc)
    chips = _other_chips(x, y)
    ids = [2 * chip[0] + chip[1] for chip in chips]

    def land(src, core):
        return out_ref.at[src, pl.ds(core * hr, hr), :]

    def copy(k, src_ref, dst_ref, to):
        return pltpu.make_async_remote_copy(src_ref=src_ref, dst_ref=dst_ref, send_sem=send_sems.at[k],
                                            recv_sem=recv_sems.at[k], device_id=to, device_id_type=MESH)

    mine = pltpu.make_async_copy(p_ref.at[me], land(me, cc), local_sem)
    sends = [copy(j, p_ref.at[ids[j]], land(me, cc), (*chip, cc)) for j, chip in enumerate(chips)]
    sends.append(copy(3, p_ref.at[me], land(me, cc), sibling))
    passed = [copy(4 + j, land(ids[j], cc), land(ids[j], cc), sibling) for j in range(3)]

    def start():
        mine.start()
        for cp in sends:
            cp.start()

    def finish():
        for j in range(3):
            copy(j, p_ref.at[me], land(ids[j], cc), (x, y, cc)).wait_recv()
            passed[j].start()
        copy(3, p_ref.at[me], land(me, 1 - cc), (x, y, cc)).wait_recv()
        for j in range(3):
            copy(4 + j, p_ref.at[me], land(ids[j], 1 - cc), (x, y, cc)).wait_recv()
        for cp in sends + passed:
            cp.wait_send()
        mine.wait()

    return start, finish


def _scatter_side(p):
    n, hr, c = p.shape
    return _Side([p], [jax.ShapeDtypeStruct((n, 2 * hr, c), p.dtype)], SCATTER_SEMS, _scatter_ops)


def _gather_small(blk):
    m, n = blk.shape

    def body(x_ref, out_ref, send_sems, recv_sems, local_sem):
        x, y, cc = _place()
        me, sibling = (x, y, cc), (x, y, 1 - cc)
        chips = _other_chips(x, y)

        def slot(px, py, pc):
            return out_ref.at[4 * px + 2 * py + pc]

        def copy(k, block, to, src=None):
            return pltpu.make_async_remote_copy(
                src_ref=slot(*block) if src is None else src, dst_ref=slot(*block),
                send_sem=send_sems.at[k], recv_sem=recv_sems.at[k], device_id=to, device_id_type=MESH)

        mine = pltpu.make_async_copy(x_ref, slot(*me), local_sem)
        mine.start()
        first = [copy(0, me, sibling, src=x_ref)]
        first += [copy(1 + j, me, (*chip, cc), src=x_ref) for j, chip in enumerate(chips)]
        for cp in first:
            cp.start()
        passed = [copy(4 + j, (*chip, cc), sibling) for j, chip in enumerate(chips)]
        for j, chip in enumerate(chips):
            copy(1 + j, (*chip, cc), me).wait_recv()
            passed[j].start()
        copy(0, sibling, me).wait_recv()
        for j, chip in enumerate(chips):
            copy(4 + j, (*chip, 1 - cc), me).wait_recv()
        for cp in first + passed:
            cp.wait_send()
        mine.wait()

    vmem = pl.BlockSpec(memory_space=pltpu.VMEM)
    return pl.pallas_call(
        body, out_shape=jax.ShapeDtypeStruct((N_DEV, m, n), blk.dtype), in_specs=[vmem], out_specs=vmem,
        scratch_shapes=[pltpu.SemaphoreType.DMA((7,)), pltpu.SemaphoreType.DMA((7,)), pltpu.SemaphoreType.DMA],
        name="gather_small")(blk)


def _pack_a(sh, dtype):
    w = sh["w_in"].astype(dtype)
    return jnp.concatenate([w[:, 0:PACK_W], jnp.pad(w[:, PACK_W:], ((0, 0), (0, 2 * PACK_W - w.shape[1])))], axis=0)


def _pack_b(sh, dtype):
    o3 = jnp.concatenate([sh["w_gla_o"], sh["w_fox_o"], sh["w_mem_o"], jnp.zeros((512, 256), sh["w_gla_o"].dtype)], axis=1)
    au = jnp.pad(sh["w_alpha_up"], ((0, PACK_ROWS_B - 3072 - 16), (0, PACK_W - 64)))
    return jnp.concatenate([sh["w_ff1"], sh["w_ff2"], sh["w_mem_kv"], sh["w_out"], o3, au], axis=0).astype(dtype)


def _unpack_a(pa):
    return {"w_in": jnp.concatenate([pa[0:1024], pa[1024:2048, 0:1670 - PACK_W]], axis=1)}


def _unpack_b(pb):
    return {"w_ff1": pb[0:1024], "w_ff2": pb[1024:2048], "w_mem_kv": pb[2048:2304], "w_out": pb[2304:2560],
            "w_gla_o": pb[2560:3072, 0:256], "w_fox_o": pb[2560:3072, 256:512], "w_mem_o": pb[2560:3072, 512:768],
            "w_alpha_up": pb[3072:3088, 0:64]}


def _unpack(packed):
    return {**_unpack_a(packed[0:PACK_ROWS_A]), **_unpack_b(packed[PACK_ROWS_A:])}


def _split_shards(name, full):
    return jnp.split(full, N_CHIPS, axis=SHARD_AXIS[name])


def _pack_small(vals, scalar=None):
    row4 = jnp.concatenate([vals["b_alpha"].reshape(-1), vals["b_forget"].reshape(-1), jnp.zeros((D - 264,), F32)])
    row5 = jnp.concatenate([vals["g_gla_head"].reshape(-1), jnp.zeros((D - 512,), F32)])
    row6 = jnp.zeros((D,), F32) if scalar is None else jnp.broadcast_to(scalar, (D,))
    rows = [vals["g_mix"].reshape(-1), vals["g_mem"].reshape(-1), vals["g_ffn"].reshape(-1), vals["g_final"].reshape(-1),
            row4, row5, row6, jnp.zeros((D,), F32)]
    return jnp.stack(rows)


def _unpack_small(blk):
    return {"g_mix": blk[0].reshape(1, D), "g_mem": blk[1].reshape(1, D), "g_ffn": blk[2].reshape(1, D),
            "g_final": blk[3].reshape(D), "b_alpha": blk[4, 0:256].reshape(1, 256), "b_forget": blk[4, 256:264].reshape(1, 8),
            "g_gla_head": blk[5, 0:512].reshape(1, 4, 128)}


def _local_step(x, mem, target, wb, small, exchange=None):
    s = x.shape[0]
    nm = mem.shape[0]
    t = _row_tile(s)
    nb = s // t
    b_alpha = small["b_alpha"].reshape(1, 256)
    bias_e = jnp.concatenate([jnp.zeros((FF_LANE,), F32), small["b_forget"].reshape(-1),
                              jnp.zeros((PE_W - FF_LANE - 8,), F32)]).reshape(1, PE_W)
    g_mix, g_mem, g_ffn = small["g_mix"].reshape(1, D), small["g_mem"].reshape(1, D), small["g_ffn"].reshape(1, D)
    g_final = small["g_final"].reshape(1, D)
    g_head = small["g_gla_head"].reshape(1, 512)

    if exchange is None:
        u, r1 = _rms_fwd(x, g_mix, "norm_mix")
    else:
        u, r1, gathered = _rms_fwd(x, g_mix, "norm_mix", side=exchange.gather_a)
        wb = exchange.weights_a(gathered)
    w_in = wb["w_in"]
    w_main = jnp.concatenate([w_in[:, 3608:6680], w_in[:, 0:1536], w_in[:, 1552:3088], w_in[:, 3096:3608]], axis=1)
    w_e = jnp.concatenate([w_in[:, 1536:1552], w_in[:, 3088:3096], jnp.zeros((D, PE_W - 24), BF16)], axis=1)
    w_in_pt = _transpose(jnp.concatenate([w_main, w_e], axis=1), "t_w_in")
    big = min(s, 1024)
    if exchange is None:
        pm, pe = _proj(u, w_main, w_e)
    else:
        pm, pe, gathered = _proj(u, w_main, w_e, side=exchange.gather_b)
        wb = {**wb, **exchange.weights_b(gathered)}
    wau_p = jnp.concatenate([wb["w_alpha_up"], jnp.zeros((PE_W - 16, 256), BF16)], axis=0)
    o_gla, states = _gla_fwd(pm, pe, wau_p, b_alpha)
    fcum = _fcum_fwd(pe, bias_e)
    tb = _fox_tables()
    qf_aug, k_aug, v_aug, vt, qt, kt = _fox_prep(pm, fcum, None, tb, backward=False)
    o_fox, lse = _fox_fwd(k_aug, qf_aug, vt)
    mn, rm = _rms_fwd(mem, g_mem, "norm_mem")
    mkv = _mm_nn(mn, wb["w_mem_kv"], out_dtype=BF16, tm=nm, tn=512, tk=D, name="mem_kv")
    o_mem = _mem_attn_fwd(pm, mkv)
    merged, h1, u2, r2 = _merge_fwd(x, pm, o_gla, o_fox, o_mem, g_head, wb["w_gla_o"], wb["w_fox_o"], wb["w_mem_o"],
                                    wb["w_out"], g_ffn)
    a = _mm_nn(u2, wb["w_ff1"], out_dtype=BF16, tm=big, tn=1024, tk=D, name="ff1")
    dh2, dh2b, loss8, dgfin8 = _ff2_loss(a, wb["w_ff2"], h1, g_final, target)
    loss = 0.5 * jnp.sum(loss8) / D

    da = _mm_nn(dh2b, _transpose(wb["w_ff2"], "t_w_ff2"), out_dtype=BF16, tm=big, tn=1024, tk=D, name="d_act",
                epi=lambda acc, at: acc * (2.0 * jnp.maximum(at.astype(F32), 0.0)), extra=a)
    gw = {}
    gw["w_ff2"] = _mm_tn(a, dh2b, tm=1024, tn=D, ts=big, name="dw_ff2", a_fn=_relu2_bf16)
    gw["w_ff1"] = _mm_tn(u2, da, tm=D, tn=1024, ts=big, name="dw_ff1")
    dh1, dh1b, dgffn8 = _mm_norm_bwd([da], _transpose(wb["w_ff1"], "t_w_ff1"), h1, r2, g_ffn, dh2, name="d_h1", want_bf16=True)
    gw["w_out"] = _mm_tn(merged, dh1b, tm=D, tn=D, ts=big, name="dw_out")
    (dgates, dgg, do_gla, do_aug, do_t, do_mem, gw["w_gla_o"], gw["w_fox_o"], gw["w_mem_o"], dgh8) = _merge_bwd(
        dh1b, pm, o_gla, o_fox, o_mem, g_head, wb["w_gla_o"], wb["w_fox_o"], wb["w_mem_o"],
        *[_transpose(wb[n], "t_" + n) for n in ("w_gla_o", "w_fox_o", "w_mem_o", "w_out")], tb["spread"], tb["d_to_do"])
    dgq, dgk, dgv, de_gla, dwau_p, dba8 = _gla_bwd(pm, pe, wau_p, wau_p.T, b_alpha, do_gla, states)
    gw["w_alpha_up"] = dwau_p[0:16, :]
    q_aug = _fox_prep(pm, fcum, lse.reshape(8, s), tb, backward=True)
    dfq_t, dfrow, dfk_t, dfv_t, dfcol = _fox_bwd(q_aug, do_aug, qt, do_t, k_aug, v_aug, kt)
    dfq, dfk, dfv, df = _fox_post(dfq_t, dfk_t, dfv_t, dfrow[:, 0, :], dfcol.reshape(8, s), tb)
    de_fox, dbf8 = _fcum_bwd(pe, bias_e, df)
    dmq, dmk, dmv = _mem_attn_bwd(pm, mkv, do_mem)
    dmkv = jnp.concatenate([dmk, dmv], axis=1).astype(BF16)
    gw["w_mem_kv"] = _mm_tn(mn, dmkv, tm=D, tn=D, ts=nm, name="dw_mem_kv")
    dmn_g = _mm_nn(dmkv, _transpose(wb["w_mem_kv"], "t_w_mem_kv"), out_dtype=F32, tm=nm, tn=D, tk=D, name="d_mem_norm")
    dgmem8 = _gain_grad(dmn_g, mem, rm, "dg_mem")
    de = (de_gla + de_fox).astype(BF16)
    dproj = [dgates, dgq, dgk, dgv, dgg, dfq, dfk, dfv, dmq, de]
    dw_gates = _mm_tn(u, dgates, tm=D, tn=1024, ts=big, name="dw_in_gates")
    dw_g = _mm_tn_cat(u, [dgq, dgk, dgv], ts=big, name="dw_in_gla")
    dw_gf = _mm_tn_cat(u, [dgg, dfq], ts=big, name="dw_in_gg_fq")
    dw_f = _mm_tn_cat(u, [dfk, dfv], ts=big, name="dw_in_fk_fv")
    dw_m = _mm_tn_cat(u, [dmq, de], ts=big, name="dw_in_mq_narrow")
    gw["w_in"] = jnp.concatenate([dw_g, dw_gf[:, 0:512], dw_m[:, 512:528], dw_gf[:, 512:1024], dw_f,
                                  dw_m[:, 528:536], dw_m[:, 0:512], dw_gates], axis=1)
    if exchange is None:
        grad_x, dgmix8 = _mm_norm_bwd(dproj, w_in_pt, x, r1, g_mix, dh1, name="d_x", want_bf16=False)
        exchanged = None
    else:
        grad_x, dgmix8, exchanged = _mm_norm_bwd(dproj, w_in_pt, x, r1, g_mix, dh1, name="d_x", want_bf16=False,
                                                 side=exchange.scatter(gw))
    gs = {"g_mix": dgmix8.sum(0), "g_mem": dgmem8.sum(0), "g_ffn": dgffn8.sum(0), "g_final": dgfin8.sum(0),
          "b_alpha": dba8.sum(0), "b_forget": dbf8.sum(0)[FF_LANE:FF_LANE + 8], "g_gla_head": dgh8.sum(0)}
    return loss, grad_x, gw, gs, exchanged


def kernel(x, mem, g_mix, w_in, w_alpha_up, b_alpha, b_forget, g_gla_head, g_mem, w_mem_kv, w_gla_o, w_fox_o, w_mem_o, w_out, g_ffn, w_ff1, w_ff2, g_final, loss_target, m_g_mix, m_w_in, m_w_alpha_up, m_b_alpha, m_b_forget, m_g_gla_head, m_g_mem, m_w_mem_kv, m_w_gla_o, m_w_fox_o, m_w_mem_o, m_w_out, m_g_ffn, m_w_ff1, m_w_ff2, m_g_final, v_g_mix, v_w_in, v_w_alpha_up, v_b_alpha, v_b_forget, v_g_gla_head, v_g_mem, v_w_mem_kv, v_w_gla_o, v_w_fox_o, v_w_mem_o, v_w_out, v_g_ffn, v_w_ff1, v_w_ff2, v_g_final):
    args = dict(locals())
    w_sh = {n: args[n][0] for n in WEIGHTS}
    small = {n: args[n] for n in SMALL}

    def whole(parts):
        return {n: jnp.concatenate([p[n] for p in parts], axis=SHARD_AXIS[n]) for n in parts[0]}

    class Exchange:
        gather_a = _gather_side(_pack_a(w_sh, BF16))
        gather_b = _gather_side(_pack_b(w_sh, BF16))

        @staticmethod
        def weights_a(gathered):
            return whole([_unpack_a(gathered[k]) for k in range(N_CHIPS)])

        @staticmethod
        def weights_b(gathered):
            return whole([_unpack_b(gathered[k]) for k in range(N_CHIPS)])

        @staticmethod
        def scatter(gw):
            by_chip = {n: _split_shards(n, gw[n]) for n in WEIGHTS}
            packed = jnp.stack([jnp.concatenate([_pack_a({n: by_chip[n][k] for n in WEIGHTS}, BF16),
                                                 _pack_b({n: by_chip[n][k] for n in WEIGHTS}, BF16)], axis=0)
                                for k in range(N_CHIPS)])
            core = lax.axis_index("c").astype(jnp.int32).reshape(1)
            return _scatter_side(_add_half(core, packed, _swap_halves(packed), "chip_sum"))

    loss, grad_x, gw, gs, by_chip = _local_step(x[0], mem[0], loss_target[0], None, small, Exchange)
    g_out = {n: g[None] for n, g in _unpack(_sum4(by_chip, "shard_sum")).items()}
    d_out, m_out, v_out = {}, {}, {}
    for n in WEIGHTS:
        d_out[n], m_out[n], v_out[n] = _adam(args[n], g_out[n], args["m_" + n], args["v_" + n], "adam_" + n)

    small_all = _gather_small(_pack_small(gs, loss))
    sm = {n: args["m_" + n] for n in SMALL}
    sv = {n: args["v_" + n] for n in SMALL}
    gs_sum, sd, snm, snv = _adam_small(_pack_small(small), small_all, _pack_small(sm), _pack_small(sv))
    gs_o, sd_o, snm_o, snv_o = _unpack_small(gs_sum), _unpack_small(sd), _unpack_small(snm), _unpack_small(snv)

    names = ["g_mix", "w_in", "w_alpha_up", "b_alpha", "b_forget", "g_gla_head", "g_mem", "w_mem_kv", "w_gla_o", "w_fox_o",
             "w_mem_o", "w_out", "g_ffn", "w_ff1", "w_ff2", "g_final"]

    def pick(big, sml, n):
        return big[n] if n in big else sml[n]

    outs = [gs_sum[6, 0], grad_x[None]]
    for big, sml in ((g_out, gs_o), (d_out, sd_o), (m_out, snm_o), (v_out, snv_o)):
        outs += [pick(big, sml, n) for n in names]
    return tuple(outs)
```

```python
import functools

import numpy as np
import jax
import jax.numpy as jnp
from jax import lax
from jax.experimental import pallas as pl
from jax.experimental.pallas import tpu as pltpu

F32 = jnp.float32
BF16 = jnp.bfloat16
HI = lax.Precision.HIGHEST
MESH = pl.DeviceIdType.MESH

EPS = 1e-6
D = 1024
CHUNK = 64
GLA_TAU = 16.0
N_CHIPS = 4
N_DEV = 8
VMEM_LIMIT_BYTES = 56 * 1024 * 1024

ADAM_LR, ADAM_B1, ADAM_B2, ADAM_EPS, ADAM_WD, ADAM_STEP = 0.001, 0.9, 0.999, 1e-08, 0.01, 10

PM_W = 6656
PE_W = 128
C_GQ, C_GK, C_GV, C_GG, C_FQ, C_FK, C_FV, C_MQ = 3072, 3328, 3584, 4096, 4608, 5120, 5632, 6144
FF_LANE = 16

WEIGHTS = ("w_in", "w_alpha_up", "w_mem_kv", "w_gla_o", "w_fox_o", "w_mem_o", "w_out", "w_ff1", "w_ff2")
SHARD_AXIS = {"w_in": 1, "w_alpha_up": 1, "w_mem_kv": 0, "w_gla_o": 1, "w_fox_o": 1, "w_mem_o": 1, "w_out": 0,
              "w_ff1": 1, "w_ff2": 0}
SMALL = ("g_mix", "g_mem", "g_ffn", "g_final", "b_alpha", "b_forget", "g_gla_head")
PACK_W = 1024
PACK_ROWS_A = 2048
PACK_ROWS_B = 3104
PACK_ROWS = PACK_ROWS_A + PACK_ROWS_B


def _cp(*sem):
    return pltpu.CompilerParams(dimension_semantics=sem, vmem_limit_bytes=VMEM_LIMIT_BYTES)


def _dot(a, b, **kw):
    return jnp.dot(a, b, preferred_element_type=F32, **kw)


def _dot_nt(a, b, **kw):
    return lax.dot_general(a, b, (((1,), (1,)), ((), ())), preferred_element_type=F32, **kw)


def _dot_tn(a, b, **kw):
    return lax.dot_general(a, b, (((0,), (0,)), ((), ())), preferred_element_type=F32, **kw)


def _sigmoid(x):
    return 0.5 * jnp.tanh(0.5 * x) + 0.5


def _log_sigmoid(x):
    return -(jnp.maximum(-x, 0.0) + jnp.log1p(jnp.exp(-jnp.abs(x))))


def _fold8(x):
    m, n = x.shape
    return x.reshape(m // 8, 8, n).sum(axis=0)


def _iota(shape, dim):
    return lax.broadcasted_iota(jnp.int32, shape, dim)


def _row_tile(s):
    return min(s, 512)


class _Side:
    def __init__(self, inputs, out_shape, scratch, ops):
        self.inputs, self.out_shape, self.scratch, self.ops = list(inputs), list(out_shape), list(scratch), ops


ANY_SPEC = pl.BlockSpec(memory_space=pl.ANY)


def _mm_nn(a, b, *, out_dtype, tm, tn, tk, name, a_fn=None, epi=None, extra=None):
    m, k = a.shape
    _, n = b.shape
    nk = k // tk

    def body_one(*refs):
        a_ref, b_ref = refs[0], refs[1]
        at = a_ref[...] if a_fn is None else a_fn(a_ref[...])
        r = _dot(at, b_ref[...])
        if epi is not None:
            r = epi(r, None if extra is None else refs[2][...])
        refs[-1][...] = r.astype(out_dtype)

    if nk == 1:
        in_specs = [pl.BlockSpec((tm, k), lambda i, j: (i, 0)), pl.BlockSpec((k, tn), lambda i, j: (0, j))]
        args = [a, b]
        if extra is not None:
            in_specs.append(pl.BlockSpec((tm, tn), lambda i, j: (i, j)))
            args.append(extra)
        return pl.pallas_call(
            body_one, grid=(m // tm, n // tn), in_specs=in_specs, out_specs=pl.BlockSpec((tm, tn), lambda i, j: (i, j)),
            out_shape=jax.ShapeDtypeStruct((m, n), out_dtype), name=name, compiler_params=_cp("parallel", "parallel"))(*args)

    def body(*refs):
        if extra is None:
            a_ref, b_ref, o_ref, acc = refs
            x_ref = None
        else:
            a_ref, b_ref, x_ref, o_ref, acc = refs
        kk = pl.program_id(2)

        @pl.when(kk == 0)
        def _():
            acc[...] = jnp.zeros_like(acc)

        at = a_ref[...]
        if a_fn is not None:
            at = a_fn(at)
        acc[...] += _dot(at, b_ref[...])

        @pl.when(kk == nk - 1)
        def _():
            r = acc[...]
            if epi is not None:
                r = epi(r, None if x_ref is None else x_ref[...])
            o_ref[...] = r.astype(out_dtype)

    in_specs = [pl.BlockSpec((tm, tk), lambda i, j, kk: (i, kk)), pl.BlockSpec((tk, tn), lambda i, j, kk: (kk, j))]
    args = [a, b]
    if extra is not None:
        in_specs.append(pl.BlockSpec((tm, tn), lambda i, j, kk: (i, j)))
        args.append(extra)
    return pl.pallas_call(
        body, grid=(m // tm, n // tn, nk), in_specs=in_specs,
        out_specs=pl.BlockSpec((tm, tn), lambda i, j, kk: (i, j)),
        out_shape=jax.ShapeDtypeStruct((m, n), out_dtype),
        scratch_shapes=[pltpu.VMEM((tm, tn), F32)], name=name,
        compiler_params=_cp("parallel", "parallel", "arbitrary"))(*args)


def _mm_tn(a, b, *, tm, tn, ts, name, a_fn=None):
    s, m = a.shape
    _, n = b.shape
    ns = s // ts

    def body(a_ref, b_ref, o_ref, acc):
        kk = pl.program_id(2)

        @pl.when(kk == 0)
        def _():
            acc[...] = jnp.zeros_like(acc)

        at = a_ref[...]
        if a_fn is not None:
            at = a_fn(at)
        acc[...] += _dot_tn(at, b_ref[...])

        @pl.when(kk == ns - 1)
        def _():
            o_ref[...] = acc[...]

    return pl.pallas_call(
        body, grid=(m // tm, n // tn, ns),
        in_specs=[pl.BlockSpec((ts, tm), lambda i, j, kk: (kk, i)), pl.BlockSpec((ts, tn), lambda i, j, kk: (kk, j))],
        out_specs=pl.BlockSpec((tm, tn), lambda i, j, kk: (i, j)),
        out_shape=jax.ShapeDtypeStruct((m, n), F32),
        scratch_shapes=[pltpu.VMEM((tm, tn), F32)], name=name,
        compiler_params=_cp("parallel", "parallel", "arbitrary"))(a, b)


def _mm_tn_cat(a, bs, *, ts, name):
    s, m = a.shape
    n = sum(b.shape[1] for b in bs)
    ns = s // ts
    nb = len(bs)

    def body(*refs):
        a_ref, b_refs, o_ref, acc = refs[0], refs[1:1 + nb], refs[1 + nb], refs[2 + nb]
        kk = pl.program_id(0)

        @pl.when(kk == 0)
        def _():
            acc[...] = jnp.zeros_like(acc)

        bt = b_refs[0][...] if nb == 1 else jnp.concatenate([r[...] for r in b_refs], axis=1)
        acc[...] += _dot_tn(a_ref[...], bt)

        @pl.when(kk == ns - 1)
        def _():
            o_ref[...] = acc[...]

    return pl.pallas_call(
        body, grid=(ns,),
        in_specs=[pl.BlockSpec((ts, m), lambda kk: (kk, 0))] + [pl.BlockSpec((ts, b.shape[1]), lambda kk: (kk, 0)) for b in bs],
        out_specs=pl.BlockSpec((m, n), lambda kk: (0, 0)), out_shape=jax.ShapeDtypeStruct((m, n), F32),
        scratch_shapes=[pltpu.VMEM((m, n), F32)], name=name, compiler_params=_cp("arbitrary"))(a, *bs)


def _proj(u, w_main, w_e, side=None):
    s, k = u.shape
    n = w_main.shape[1]
    tm, tn = min(s, 1024), n // 4
    n_sin = 0 if side is None else len(side.inputs)
    n_sout = 0 if side is None else len(side.out_shape)

    def body(u_ref, w_ref, we_ref, *rest):
        pm_ref, pe_ref = rest[n_sin:n_sin + 2]
        i, j = pl.program_id(0), pl.program_id(1)
        if side is not None:
            start, finish = side.ops(rest[:n_sin], rest[n_sin + 2:n_sin + 2 + n_sout], rest[n_sin + 2 + n_sout:])
            pl.when((i == 0) & (j == 0))(start)
        ut = u_ref[...]
        pm_ref[...] = _dot(ut, w_ref[...]).astype(BF16)

        @pl.when(j == 0)
        def _():
            pe_ref[...] = _dot(ut, we_ref[...])

        if side is not None:
            pl.when((i == s // tm - 1) & (j == n // tn - 1))(finish)

    side_in = [] if side is None else side.inputs
    return pl.pallas_call(
        body, grid=(s // tm, n // tn),
        in_specs=[pl.BlockSpec((tm, k), lambda i, j: (i, 0)), pl.BlockSpec((k, tn), lambda i, j: (0, j)),
                  pl.BlockSpec((k, PE_W), lambda i, j: (0, 0))] + [ANY_SPEC] * n_sin,
        out_specs=[pl.BlockSpec((tm, tn), lambda i, j: (i, j)), pl.BlockSpec((tm, PE_W), lambda i, j: (i, 0))]
        + [ANY_SPEC] * n_sout,
        out_shape=[jax.ShapeDtypeStruct((s, n), BF16), jax.ShapeDtypeStruct((s, PE_W), F32)]
        + ([] if side is None else side.out_shape),
        scratch_shapes=[] if side is None else side.scratch,
        name="proj_main", compiler_params=_cp("arbitrary", "arbitrary"))(u, w_main, w_e, *side_in)


def _transpose(w, name):
    r, c = w.shape
    tr = min(r, 256)

    def body(w_ref, o_ref):
        o_ref[...] = w_ref[...].T

    return pl.pallas_call(body, grid=(r // tr,), in_specs=[pl.BlockSpec((tr, c), lambda i: (i, 0))],
                          out_specs=pl.BlockSpec((c, tr), lambda i: (0, i)),
                          out_shape=jax.ShapeDtypeStruct((c, r), w.dtype), name=name, compiler_params=_cp("parallel"))(w)


def _relu2_bf16(t):
    r = jnp.maximum(t.astype(F32), 0.0)
    return (r * r).astype(BF16)


def _rms_fwd(x, g, name, side=None):
    s, d = x.shape
    tm = min(s, 512)
    n_sin = 0 if side is None else len(side.inputs)
    n_sout = 0 if side is None else len(side.out_shape)

    def body(x_ref, g_ref, *rest):
        u_ref, r_ref = rest[n_sin:n_sin + 2]
        if side is not None:
            start, finish = side.ops(rest[:n_sin], rest[n_sin + 2:n_sin + 2 + n_sout], rest[n_sin + 2 + n_sout:])
            pl.when(pl.program_id(0) == 0)(start)
        xv = x_ref[...]
        r = lax.rsqrt(jnp.mean(xv * xv, axis=-1, keepdims=True) + EPS)
        u_ref[...] = ((xv * r) * g_ref[...]).astype(BF16)
        r_ref[...] = r
        if side is not None:
            pl.when(pl.program_id(0) == s // tm - 1)(finish)

    side_in = [] if side is None else side.inputs
    return pl.pallas_call(
        body, grid=(s // tm,),
        in_specs=[pl.BlockSpec((tm, d), lambda i: (i, 0)), pl.BlockSpec((1, d), lambda i: (0, 0))] + [ANY_SPEC] * n_sin,
        out_specs=[pl.BlockSpec((tm, d), lambda i: (i, 0)), pl.BlockSpec((tm, 1), lambda i: (i, 0))] + [ANY_SPEC] * n_sout,
        out_shape=[jax.ShapeDtypeStruct((s, d), BF16), jax.ShapeDtypeStruct((s, 1), F32)]
        + ([] if side is None else side.out_shape),
        scratch_shapes=[] if side is None else side.scratch,
        name=name, compiler_params=_cp("parallel" if side is None else "arbitrary"))(x, g, *side_in)


def _mm_norm_bwd(a_parts, b, xin, r, g, dres, *, name, want_bf16, side=None):
    s = a_parts[0].shape[0]
    k = b.shape[0]
    na = len(a_parts)
    offs = [sum(p.shape[1] for p in a_parts[:i]) for i in range(na)]
    assert offs[-1] + a_parts[-1].shape[1] == k
    tm = min(s, 512)
    n_out = 3 if want_bf16 else 2
    n_sin = 0 if side is None else len(side.inputs)
    n_sout = 0 if side is None else len(side.out_shape)

    def body(*refs):
        a_refs = refs[:na]
        b_ref, x_ref, r_ref, g_ref, dres_ref = refs[na:na + 5]
        rest = refs[na + 5:]
        outs = rest[n_sin:n_sin + n_out]
        dx_ref, dg_ref = outs[0], outs[-1]
        if side is not None:
            start, finish = side.ops(rest[:n_sin], rest[n_sin + n_out:n_sin + n_out + n_sout], rest[n_sin + n_out + n_sout:])
            pl.when(pl.program_id(0) == 0)(start)

        @pl.when(pl.program_id(0) == 0)
        def _():
            dg_ref[...] = jnp.zeros_like(dg_ref)

        du = _dot(a_refs[0][...], b_ref[0:a_parts[0].shape[1], :])
        for a_ref, off, part in zip(a_refs[1:], offs[1:], a_parts[1:]):
            du = du + _dot(a_ref[...], b_ref[off:off + part.shape[1], :])
        xn = x_ref[...] * r_ref[...]
        dg_ref[...] += _fold8(du * xn)
        dxn = du * g_ref[...]
        dx = dres_ref[...] + r_ref[...] * (dxn - xn * jnp.mean(dxn * xn, axis=-1, keepdims=True))
        dx_ref[...] = dx
        if want_bf16:
            outs[1][...] = dx.astype(BF16)
        if side is not None:
            pl.when(pl.program_id(0) == s // tm - 1)(finish)

    row = lambda i: (i, 0)
    const = lambda i: (0, 0)
    out_specs = [pl.BlockSpec((tm, D), row)]
    out_shape = [jax.ShapeDtypeStruct((s, D), F32)]
    if want_bf16:
        out_specs.append(pl.BlockSpec((tm, D), row))
        out_shape.append(jax.ShapeDtypeStruct((s, D), BF16))
    out_specs.append(pl.BlockSpec((8, D), const))
    out_shape.append(jax.ShapeDtypeStruct((8, D), F32))
    side_in = [] if side is None else side.inputs
    return pl.pallas_call(
        body, grid=(s // tm,),
        in_specs=[pl.BlockSpec((tm, p.shape[1]), row) for p in a_parts]
        + [pl.BlockSpec((k, D), const, pipeline_mode=pl.Buffered(1)),
           pl.BlockSpec((tm, D), row), pl.BlockSpec((tm, 1), row), pl.BlockSpec((1, D), const),
           pl.BlockSpec((tm, D), row)] + [ANY_SPEC] * n_sin,
        out_specs=out_specs + [ANY_SPEC] * n_sout, out_shape=out_shape + ([] if side is None else side.out_shape),
        scratch_shapes=[] if side is None else side.scratch,
        name=name, compiler_params=_cp("arbitrary"))(*a_parts, b, xin, r, g, dres, *side_in)


def _gla_consts():
    lmask = _iota((4 * CHUNK, CHUNK), 0) % CHUNK >= _iota((4 * CHUNK, CHUNK), 1)
    hmask = _iota((256, 256), 0) // CHUNK == _iota((256, 256), 1) // CHUNK
    bd = _iota((256, 512), 0) // CHUNK == _iota((256, 512), 1) // 128
    return lmask, hmask, bd


def _fold_heads(x):
    return x[0:64] + x[64:128] + x[128:192] + x[192:256]


def _gla_decays(la, b_scr, dec_scr):
    tri = (_iota((CHUNK, CHUNK), 0) >= _iota((CHUNK, CHUNK), 1)).astype(BF16)
    ones = jnp.ones((CHUNK, 128), BF16)
    for c in range(la.shape[0] // CHUNK):
        la3 = _split3(la[CHUNK * c:CHUNK * (c + 1)])
        b_scr[CHUNK * c:CHUNK * (c + 1), :] = _sum3(_dot(tri, la3), 1)
        dec_scr[c] = jnp.exp(_sum3(_dot_tn(la3, ones), 0))


def _gla_chunk(b, qc, kc):
    bl = b[CHUNK - 1:CHUNK, :]
    ep, en, ek = jnp.exp(b), jnp.exp(-b), jnp.exp(bl - b)
    return bl, ep, en, ek, qc * ep, qc * en, kc * en, kc * ep, kc * ek


def _gla_fwd(pm, pe, wau_p, b_alpha):
    s = pm.shape[0]
    t = min(s, 1024)
    nc = t // CHUNK

    def body(q_ref, k_ref, v_ref, e_ref, wau_ref, ba_ref, o_ref, st_ref, state, b_scr, dec_scr):
        @pl.when(pl.program_id(0) == 0)
        def _():
            state[...] = jnp.zeros_like(state)

        z = _dot(e_ref[...].astype(BF16), wau_ref[...]) + ba_ref[...]
        _gla_decays(_log_sigmoid(z) * (1.0 / GLA_TAU), b_scr, dec_scr)
        lmask, hmask, bd = _gla_consts()

        def chunk(c, carry):
            rows = pl.ds(pl.multiple_of(c * CHUNK, CHUNK), CHUNK)
            qc = q_ref[rows, :].astype(F32) * 0.125
            kc = k_ref[rows, :].astype(F32)
            vc = v_ref[rows, :]
            _, _, _, _, qp, qn, kn, kp, kk = _gla_chunk(b_scr[rows, :], qc, kc)
            decb = jnp.concatenate([dec_scr[c]] * 4, axis=1)
            qs = jnp.where(hmask, jnp.concatenate([qp] * 4, axis=0), 0.0).astype(BF16)
            qns = jnp.where(hmask, jnp.concatenate([qn] * 4, axis=0), 0.0).astype(BF16)
            attn = jnp.where(lmask, _dot_nt(qs, kn.astype(BF16)), _dot_nt(qns, kp.astype(BF16))).astype(BF16)
            st = state[...]
            o_intra = _fold_heads(jnp.where(bd, _dot(attn, vc), 0.0))
            o_ref[rows, :] = o_intra + _dot(qp.astype(BF16), st.astype(BF16))
            for h in range(4):
                st_ref[c, :, 128 * h:128 * (h + 1)] = st[64 * h:64 * (h + 1), 128 * h:128 * (h + 1)]
            kv = jnp.where(bd, _dot_tn(kk.astype(BF16), vc), 0.0)
            state[...] = st * decb + kv
            return carry

        lax.fori_loop(0, nc, chunk, 0)

    return pl.pallas_call(
        body, grid=(s // t,),
        in_specs=[pl.BlockSpec((t, 256), lambda i: (i, C_GQ // 256)), pl.BlockSpec((t, 256), lambda i: (i, C_GK // 256)),
                  pl.BlockSpec((t, 512), lambda i: (i, C_GV // 512)), pl.BlockSpec((t, PE_W), lambda i: (i, 0)),
                  pl.BlockSpec((PE_W, 256), lambda i: (0, 0)), pl.BlockSpec((1, 256), lambda i: (0, 0))],
        out_specs=[pl.BlockSpec((t, 512), lambda i: (i, 0)), pl.BlockSpec((nc, CHUNK, 512), lambda i: (i, 0, 0))],
        out_shape=[jax.ShapeDtypeStruct((s, 512), F32), jax.ShapeDtypeStruct((s // CHUNK, CHUNK, 512), F32)],
        scratch_shapes=[pltpu.VMEM((256, 512), F32), pltpu.VMEM((t, 256), F32), pltpu.VMEM((nc, 256, 128), F32)],
        name="gla_fwd", compiler_params=_cp("arbitrary"))(pm, pm, pm, pe, wau_p, b_alpha)


def _gla_bwd(pm, pe, wau_p, wau_pt, b_alpha, do, states):
    s = pm.shape[0]
    t = min(s, 1024)
    nc = t // CHUNK
    nb = s // t

    def body(q_ref, k_ref, v_ref, e_ref, wau_ref, waut_ref, ba_ref, do_ref, st_ref,
             dq_ref, dk_ref, dv_ref, de_ref, dwau_ref, dba_ref, gstate, b_scr, db_scr, dec_scr):
        @pl.when(pl.program_id(0) == 0)
        def _():
            gstate[...] = jnp.zeros_like(gstate)
            dwau_ref[...] = jnp.zeros_like(dwau_ref)
            dba_ref[...] = jnp.zeros_like(dba_ref)

        eb = e_ref[...].astype(BF16)
        z = _dot(eb, wau_ref[...]) + ba_ref[...]
        _gla_decays(_log_sigmoid(z) * (1.0 / GLA_TAU), b_scr, dec_scr)
        lmask, hmask, bd = _gla_consts()
        last_row = _iota((CHUNK, 256), 0) == CHUNK - 1

        def chunk(cc, carry):
            c = nc - 1 - cc
            rows = pl.ds(pl.multiple_of(c * CHUNK, CHUNK), CHUNK)
            qc = q_ref[rows, :].astype(F32) * 0.125
            kc = k_ref[rows, :].astype(F32)
            vc = v_ref[rows, :]
            dob = do_ref[rows, :]
            bl, ep, en, ek, qp, qn, kn, kp, kk = _gla_chunk(b_scr[rows, :], qc, kc)
            decb = jnp.concatenate([dec_scr[c]] * 4, axis=1)
            qs = jnp.where(hmask, jnp.concatenate([qp] * 4, axis=0), 0.0).astype(BF16)
            qns = jnp.where(hmask, jnp.concatenate([qn] * 4, axis=0), 0.0).astype(BF16)
            knb, kpb = kn.astype(BF16), kp.astype(BF16)
            attn = jnp.where(lmask, _dot_nt(qs, knb), _dot_nt(qns, kpb)).astype(BF16)
            st = jnp.where(bd, jnp.concatenate([st_ref[c]] * 4, axis=0), 0.0)
            g = gstate[...]
            gb = g.astype(BF16)
            do_s = jnp.where(bd, jnp.concatenate([dob] * 4, axis=0), jnp.zeros((), BF16))
            dattn = _dot_nt(do_s, vc)
            dv_ref[rows, :] = (_dot_tn(attn, do_s) + _dot(kk.astype(BF16), gb)).astype(BF16)
            dac = jnp.where(lmask, dattn, 0.0).astype(BF16)
            daa = jnp.where(lmask, 0.0, dattn).astype(BF16)
            dqp = _fold_heads(jnp.where(hmask, _dot(dac, knb), 0.0)) + _dot_nt(dob, st.astype(BF16))
            dqn = _fold_heads(jnp.where(hmask, _dot(daa, kpb), 0.0))
            dkn = _dot_tn(dac, qs)
            dkp = _dot_tn(daa, qns)
            dkk = _dot_nt(vc, gb)
            ddec = _dot_nt(jnp.ones((8, 1536), BF16), _split3(g * st))[0:1, :]
            gstate[...] = decb * g + jnp.where(bd, _dot_tn(qp.astype(BF16), dob), 0.0)
            dq_ref[rows, :] = ((dqp * ep + dqn * en) * 0.125).astype(BF16)
            dk_ref[rows, :] = (dkn * en + dkp * ep + dkk * ek).astype(BF16)
            dek = dkk * kc * ek
            db = (dqp * qc + dkp * kc) * ep - (dqn * qc + dkn * kc) * en - dek
            dbl = jnp.sum(dek, axis=0, keepdims=True) + ddec * jnp.exp(bl)
            db_scr[rows, :] = db + jnp.where(last_row, dbl, 0.0)
            return carry

        lax.fori_loop(0, nc, chunk, 0)
        triu = (_iota((CHUNK, CHUNK), 0) <= _iota((CHUNK, CHUNK), 1)).astype(BF16)
        dla = jnp.concatenate([_sum3(_dot(triu, _split3(db_scr[CHUNK * c:CHUNK * (c + 1), :])), 1) for c in range(nc)], axis=0)
        dz = dla * (1.0 / GLA_TAU) * _sigmoid(-z)
        dzb = dz.astype(BF16)
        dwau_ref[...] += _dot_tn(eb, dzb)
        dba_ref[...] += _fold8(dz)
        de_ref[...] = _dot(dzb, waut_ref[...])

    rev = lambda i: nb - 1 - i
    return pl.pallas_call(
        body, grid=(nb,),
        in_specs=[pl.BlockSpec((t, 256), lambda i: (rev(i), C_GQ // 256)), pl.BlockSpec((t, 256), lambda i: (rev(i), C_GK // 256)),
                  pl.BlockSpec((t, 512), lambda i: (rev(i), C_GV // 512)), pl.BlockSpec((t, PE_W), lambda i: (rev(i), 0)),
                  pl.BlockSpec((PE_W, 256), lambda i: (0, 0)), pl.BlockSpec((256, PE_W), lambda i: (0, 0)),
                  pl.BlockSpec((1, 256), lambda i: (0, 0)), pl.BlockSpec((t, 512), lambda i: (rev(i), 0)),
                  pl.BlockSpec((nc, CHUNK, 512), lambda i: (rev(i), 0, 0))],
        out_specs=[pl.BlockSpec((t, 256), lambda i: (rev(i), 0)), pl.BlockSpec((t, 256), lambda i: (rev(i), 0)),
                   pl.BlockSpec((t, 512), lambda i: (rev(i), 0)), pl.BlockSpec((t, PE_W), lambda i: (rev(i), 0)),
                   pl.BlockSpec((PE_W, 256), lambda i: (0, 0)), pl.BlockSpec((8, 256), lambda i: (0, 0))],
        out_shape=[jax.ShapeDtypeStruct((s, 256), BF16), jax.ShapeDtypeStruct((s, 256), BF16),
                   jax.ShapeDtypeStruct((s, 512), BF16), jax.ShapeDtypeStruct((s, PE_W), F32),
                   jax.ShapeDtypeStruct((PE_W, 256), F32), jax.ShapeDtypeStruct((8, 256), F32)],
        scratch_shapes=[pltpu.VMEM((256, 512), F32), pltpu.VMEM((t, 256), F32), pltpu.VMEM((t, 256), F32),
                        pltpu.VMEM((nc, 256, 128), F32)],
        name="gla_bwd", compiler_params=_cp("arbitrary"))(pm, pm, pm, pe, wau_p, wau_pt, b_alpha, do, states)


def _fcum_fwd(pe, bias):
    s = pe.shape[0]
    t = min(s, 512)

    def body(e_ref, b_ref, f_ref, carry):
        @pl.when(pl.program_id(0) == 0)
        def _():
            carry[...] = jnp.zeros_like(carry)

        lf = _log_sigmoid(e_ref[...] + b_ref[...])
        tri = (_iota((t, t), 0) >= _iota((t, t), 1)).astype(BF16)
        f = _sum3(_dot(tri, _split3(lf)), 1) + carry[0:1, :]
        f_ref[...] = f
        carry[...] = jnp.broadcast_to(f[t - 1:t, :], carry.shape)

    return pl.pallas_call(
        body, grid=(s // t,),
        in_specs=[pl.BlockSpec((t, PE_W), lambda i: (i, 0)), pl.BlockSpec((1, PE_W), lambda i: (0, 0))],
        out_specs=pl.BlockSpec((t, PE_W), lambda i: (i, 0)),
        out_shape=jax.ShapeDtypeStruct((s, PE_W), F32), scratch_shapes=[pltpu.VMEM((8, PE_W), F32)],
        name="fcum_fwd", compiler_params=_cp("arbitrary"))(pe, bias)


def _fcum_bwd(pe, bias, df):
    s = pe.shape[0]
    t = min(s, 512)
    nb = s // t

    def body(e_ref, b_ref, df_ref, de_ref, db_ref, carry):
        @pl.when(pl.program_id(0) == 0)
        def _():
            carry[...] = jnp.zeros_like(carry)
            db_ref[...] = jnp.zeros_like(db_ref)

        triu = (_iota((t, t), 0) <= _iota((t, t), 1)).astype(BF16)
        dlf = _sum3(_dot(triu, _split3(df_ref[...])), 1) + carry[0:1, :]
        carry[...] = jnp.broadcast_to(dlf[0:1, :], carry.shape)
        lane = _iota((t, PE_W), 1)
        dff = jnp.where((lane >= FF_LANE) & (lane < FF_LANE + 8), dlf * _sigmoid(-(e_ref[...] + b_ref[...])), 0.0)
        de_ref[...] = dff
        db_ref[...] += _fold8(dff)

    rev = lambda i: (nb - 1 - i, 0)
    return pl.pallas_call(
        body, grid=(nb,),
        in_specs=[pl.BlockSpec((t, PE_W), rev), pl.BlockSpec((1, PE_W), lambda i: (0, 0)), pl.BlockSpec((t, PE_W), rev)],
        out_specs=[pl.BlockSpec((t, PE_W), rev), pl.BlockSpec((8, PE_W), lambda i: (0, 0))],
        out_shape=[jax.ShapeDtypeStruct((s, PE_W), F32), jax.ShapeDtypeStruct((8, PE_W), F32)],
        scratch_shapes=[pltpu.VMEM((8, PE_W), F32)],
        name="fcum_bwd", compiler_params=_cp("arbitrary"))(pe, bias, df)


FOX_WIDE = 1024


def _split3(x):
    hi = x.astype(BF16)
    r = x - hi.astype(F32)
    mid = r.astype(BF16)
    lo = (r - mid.astype(F32)).astype(BF16)
    return jnp.concatenate([hi, mid, lo], axis=1)


def _pieces(x, lane0):
    lane = _iota(x.shape, 1)
    x = jnp.where((lane >= lane0) & (lane < lane0 + 8), x, 0.0)
    hi = x.astype(BF16).astype(F32)
    r = x - hi
    mid = r.astype(BF16).astype(F32)
    lo = (r - mid).astype(BF16).astype(F32)
    return (hi + pltpu.roll(mid, 8, 1) + pltpu.roll(lo, 16, 1)).astype(BF16)


def _sum3(x, axis):
    n = x.shape[axis] // 3
    parts = [lax.slice_in_dim(x, n * p, n * (p + 1), axis=axis) for p in range(3)]
    return (parts[0] + parts[1]) + parts[2]


def _spread(x, sp):
    return jnp.concatenate([_dot(x[:, 128 * g:128 * (g + 1)], sp) for g in range(4)], axis=1)


def _fox_tables():
    heads, lane = np.arange(8), np.arange(64)
    spread = np.zeros((128, 256), np.float32)
    spread[(64 * heads[:2, None] + lane).ravel(), (128 * heads[:2, None] + lane).ravel()] = 1.0
    def place(src_lane0, dst_off, val):
        t = np.zeros((128, 1024), np.float32)
        for p in range(3):
            for src in {src_lane0 + 8 * p, (src_lane0 - 8 * p) % 128}:
                t[src + heads, 128 * heads + dst_off + p] = val
        return t
    def const(off, val):
        c = np.zeros((1, 1024), np.float32)
        for p in range(3):
            c[0, 128 * heads + off + p] = val
        return c
    rows = np.zeros((8, 128), np.float32)
    rows[heads, FF_LANE + heads] = 1.0
    bf = lambda a: jnp.asarray(a, BF16)
    return dict(spread=bf(spread),
                f_to_q=bf(place(FF_LANE, 64, 1.0)), f_to_k=bf(place(FF_LANE, 67, -1.0)), d_to_do=bf(place(0, 64, 1.0)),
                ones_q=jnp.asarray(const(67, 1.0)), ones_k=jnp.asarray(const(64, 1.0)), ones_v=jnp.asarray(const(64, -1.0)),
                rows=jnp.asarray(rows))


LOG2E = 1.4426950408889634


def _fox_prep(pm, f128, lse8, tb, *, backward):
    s = pm.shape[0]
    tm = min(s, 1024) if backward else _row_tile(s)

    def body(*refs):
        if backward:
            q_ref, f_ref, lse_ref, sp_ref, fq_ref, cq_ref, rows_ref, qa_ref = refs
            f = f_ref[...] * LOG2E - _dot_tn(lse_ref[...], rows_ref[...], precision=HI)
            q2 = (q_ref[...].astype(F32) * (0.125 * LOG2E)).astype(BF16)
            qa_ref[...] = (_spread(q2, sp_ref[...]) + _dot(_pieces(f, FF_LANE), fq_ref[...]) + cq_ref[...]).astype(BF16)
            return
        (q_ref, k_ref, v_ref, f_ref, sp_ref, fq_ref, fk_ref, cq_ref, ck_ref, cv_ref,
         qa_ref, ka_ref, va_ref, vt_ref, qt_ref, kt_ref) = refs
        f3 = _pieces(f_ref[...] * LOG2E, FF_LANE)
        q, k, v = q_ref[...].astype(F32), k_ref[...], v_ref[...]
        sp = sp_ref[...]
        qa_ref[...] = (_spread((q * (0.125 * LOG2E)).astype(BF16), sp) + _dot(f3, fq_ref[...]) + cq_ref[...]).astype(BF16)
        ka_ref[...] = (_spread(k, sp) + _dot(f3, fk_ref[...]) + ck_ref[...]).astype(BF16)
        va_ref[...] = (_spread(v, sp) + cv_ref[...]).astype(BF16)
        vt_ref[...] = v.T
        qt_ref[...] = (q * 0.125).astype(BF16).T
        kt_ref[...] = (k.astype(F32) * 0.125).astype(BF16).T

    row = lambda i: (i, 0)
    const = lambda i: (0, 0)
    blk = lambda c: pl.BlockSpec((tm, 512), lambda i: (i, c // 512))
    wide = pl.BlockSpec((tm, 1024), row)
    mat = lambda a: pl.BlockSpec(a.shape, const)
    if backward:
        ins = [pm, f128, lse8, tb["spread"], tb["f_to_q"], tb["ones_q"], tb["rows"]]
        in_specs = [blk(C_FQ), pl.BlockSpec((tm, 128), row), pl.BlockSpec((8, tm), lambda i: (0, i))] + [mat(a) for a in ins[3:]]
        out_specs, out_shape = wide, jax.ShapeDtypeStruct((s, 1024), BF16)
    else:
        ins = [pm, pm, pm, f128, tb["spread"], tb["f_to_q"], tb["f_to_k"], tb["ones_q"], tb["ones_k"], tb["ones_v"]]
        in_specs = [blk(C_FQ), blk(C_FK), blk(C_FV), pl.BlockSpec((tm, 128), row)] + [mat(a) for a in ins[4:]]
        tr = pl.BlockSpec((512, tm), lambda i: (0, i))
        out_specs = [wide, wide, wide, tr, tr, tr]
        out_shape = [jax.ShapeDtypeStruct((s, 1024), BF16)] * 3 + [jax.ShapeDtypeStruct((512, s), BF16)] * 3
    return pl.pallas_call(body, grid=(s // tm,), in_specs=in_specs, out_specs=out_specs, out_shape=out_shape,
                          name="fox_prep_bwd" if backward else "fox_prep", compiler_params=_cp("parallel"))(*ins)


def _fox_post(dqt, dkt, dvt, rowsum8, colsum8, tb):
    s = dqt.shape[1]
    tm = min(s, 1024)

    def body(dqt_ref, dkt_ref, dvt_ref, rs_ref, cs_ref, rows_ref, dfq_ref, dfk_ref, dfv_ref, df_ref):
        dfq_ref[...] = dqt_ref[...].T.astype(BF16)
        dfk_ref[...] = dkt_ref[...].T
        dfv_ref[...] = dvt_ref[...].T
        df_ref[...] = _dot_tn(rs_ref[...] - cs_ref[...], rows_ref[...], precision=HI)

    row = lambda i: (i, 0)
    tr = pl.BlockSpec((512, tm), lambda i: (0, i))
    out = pl.BlockSpec((tm, 512), row)
    heads = pl.BlockSpec((8, tm), lambda i: (0, i))
    return pl.pallas_call(
        body, grid=(s // tm,),
        in_specs=[tr, tr, tr, heads, heads, pl.BlockSpec((8, 128), lambda i: (0, 0))],
        out_specs=[out, out, out, pl.BlockSpec((tm, 128), row)],
        out_shape=[jax.ShapeDtypeStruct((s, 512), BF16)] * 3 + [jax.ShapeDtypeStruct((s, 128), F32)],
        name="fox_post", compiler_params=_cp("parallel"))(dqt, dkt, dvt, rowsum8, colsum8, tb["rows"])


def _fox_fwd(k_aug, q_aug, vt):
    s = k_aug.shape[0]
    nh = 8
    tk = _row_tile(s)
    tq = min(s, 2 * FOX_WIDE)
    per = tq // tk

    def body(k_ref, q_ref, v_ref, o_ref, lse_ref, sbuf):
        i = pl.program_id(1)
        qa = q_ref[...]

        def scores(j):
            return _dot_nt(k_ref[pl.ds(pl.multiple_of(j * tk, tk), tk), :], qa)

        ones_row = (_iota((16, tk), 0) == 0).astype(BF16)

        def update(st, j, carry):
            m, acc = carry
            m2 = jnp.maximum(m, jnp.max(st, axis=0, keepdims=True))
            p = jnp.exp2(st - m2)
            vj = jnp.concatenate([v_ref[:, pl.ds(pl.multiple_of(j * tk, tk), tk)], ones_row], axis=0)
            return m2, jnp.exp2(m - m2) * acc + _dot(vj, p.astype(BF16))

        def step(a, carry):
            sbuf[1] = scores(2 * a + 1)
            carry = update(sbuf[0], 2 * a, carry)
            sbuf[0] = scores(2 * a + 2)
            return update(sbuf[1], 2 * a + 1, carry)

        n = i * per
        sbuf[0] = scores(0)
        carry = (jnp.full((1, tq), -1e30, F32), jnp.zeros((80, tq), F32))
        carry = lax.fori_loop(0, n // 2, step, carry)
        tri = _iota((tk, tk), 0) <= _iota((tk, tk), 1)
        late = [_dot_nt(k_ref[pl.ds(pl.multiple_of((n + r) * tk, tk), tk), :], qa[r * tk:, :]) for r in range(1, per)]
        for r in range(per):
            st = sbuf[0] if r == 0 else late[r - 1]
            head = jnp.where(tri, st[:, :tk], -1e30)
            st = head if st.shape[1] == tk else jnp.concatenate([head, st[:, tk:]], axis=1)
            part = update(st, n + r, tuple(c[:, r * tk:] for c in carry))
            carry = part if r == 0 else tuple(jnp.concatenate([old[:, :r * tk], new], axis=1) for old, new in zip(carry, part))
        m, acc = carry
        l = acc[64:65]
        o_ref[...] = (acc[0:64] / l).astype(BF16)
        lse_ref[0] = m + jnp.log2(l)

    return pl.pallas_call(
        body, grid=(nh, s // tq),
        in_specs=[pl.BlockSpec((s, 128), lambda h, i: (0, h)), pl.BlockSpec((tq, 128), lambda h, i: (i, h)),
                  pl.BlockSpec((64, s), lambda h, i: (h, 0))],
        out_specs=[pl.BlockSpec((64, tq), lambda h, i: (h, i)), pl.BlockSpec((1, 1, tq), lambda h, i: (h, 0, i))],
        out_shape=[jax.ShapeDtypeStruct((512, s), BF16), jax.ShapeDtypeStruct((nh, 1, s), F32)],
        scratch_shapes=[pltpu.VMEM((2, tk, tq), F32)],
        name="fox_fwd", compiler_params=_cp("parallel", "arbitrary"))(k_aug, q_aug, vt)


def _fox_bwd(q_aug, do_aug, qt, dot_, k_aug, v_aug, kt):
    s = q_aug.shape[0]
    nh = 8
    tq = _row_tile(s)
    tk = min(s, 2 * FOX_WIDE)
    per = tk // tq
    nqb = s // tq

    def body(qa_ref, da_ref, qt_ref, dt_ref, ka_ref, va_ref, kt_ref, dq_ref, rs_ref, dk_ref, dv_ref, dfk_ref):
        j = pl.program_id(1)

        @pl.when(j == 0)
        def _():
            dq_ref[...] = jnp.zeros_like(dq_ref)
            rs_ref[...] = jnp.zeros_like(rs_ref)

        ones_row = (_iota((16, tk), 0) == 0).astype(BF16)
        ka, va = ka_ref[...], va_ref[...]
        ks = jnp.concatenate([kt_ref[...], ones_row], axis=0)
        tri = _iota((tq, tq), 0) >= _iota((tq, tq), 1)

        def tile(i, w, carry):
            masked = w is not None
            w = tk if w is None else w
            rows = pl.ds(pl.multiple_of(i * tq, tq), tq)
            sp = _dot_nt(qa_ref[rows, :], ka[:w])
            if masked:
                last = jnp.where(tri, sp[:, w - tq:], -1e30)
                sp = last if w == tq else jnp.concatenate([sp[:, :w - tq], last], axis=1)
            p = jnp.exp2(sp)
            dsb = (p * _dot_nt(da_ref[rows, :], va[:w])).astype(BF16)
            dq = _dot_nt(ks[:, :w], dsb)
            dq_ref[:, rows] += dq[0:64]
            rs_ref[0, :, rows] += dq[64:72]
            new = (_dot(jnp.concatenate([qt_ref[:, rows], ones_row[:, :tq]], axis=0), dsb), _dot(dt_ref[:, rows], p.astype(BF16)))
            if w == tk:
                return tuple(c + d for c, d in zip(carry, new))
            return tuple(jnp.concatenate([c[:, :w] + d, c[:, w:]], axis=1) for c, d in zip(carry, new))

        carry = (jnp.zeros((80, tk), F32), jnp.zeros((64, tk), F32))
        for r in range(per):
            carry = tile(j * per + r, (r + 1) * tq, carry)
        dk, dv = lax.fori_loop((j + 1) * per, nqb, lambda i, c: tile(i, None, c), carry)
        dk_ref[...] = dk[0:64].astype(BF16)
        dv_ref[...] = dv.astype(BF16)
        dfk_ref[0] = dk[64:65]

    head_cols = lambda h, j: (0, h)
    head_rows = lambda h, j: (h, 0)
    once = dict(pipeline_mode=pl.Buffered(1))
    return pl.pallas_call(
        body, grid=(nh, s // tk),
        in_specs=[pl.BlockSpec((s, 128), head_cols, **once), pl.BlockSpec((s, 128), head_cols, **once),
                  pl.BlockSpec((64, s), head_rows, **once), pl.BlockSpec((64, s), head_rows, **once),
                  pl.BlockSpec((tk, 128), lambda h, j: (j, h)), pl.BlockSpec((tk, 128), lambda h, j: (j, h)),
                  pl.BlockSpec((64, tk), lambda h, j: (h, j))],
        out_specs=[pl.BlockSpec((64, s), head_rows), pl.BlockSpec((1, 8, s), lambda h, j: (h, 0, 0)),
                   pl.BlockSpec((64, tk), lambda h, j: (h, j)),
                   pl.BlockSpec((64, tk), lambda h, j: (h, j)), pl.BlockSpec((1, 1, tk), lambda h, j: (h, 0, j))],
        out_shape=[jax.ShapeDtypeStruct((512, s), F32), jax.ShapeDtypeStruct((nh, 8, s), F32),
                   jax.ShapeDtypeStruct((512, s), BF16),
                   jax.ShapeDtypeStruct((512, s), BF16), jax.ShapeDtypeStruct((nh, 1, s), F32)],
        name="fox_bwd", compiler_params=_cp("parallel", "arbitrary"))(q_aug, do_aug, qt, dot_, k_aug, v_aug, kt)


MEM_SCALE = 128 ** -0.5


def _mem_attn_fwd(pm, mkv):
    s = pm.shape[0]
    t = min(s, 1024)
    nm = mkv.shape[0]

    def body(q_ref, mk_ref, mv_ref, o_ref):
        for h in range(4):
            cols = slice(128 * h, 128 * (h + 1))
            sc = _dot_nt(q_ref[:, cols], mk_ref[:, cols]) * MEM_SCALE
            p = jnp.exp(sc - jnp.max(sc, axis=-1, keepdims=True))
            p = p / jnp.sum(p, axis=-1, keepdims=True)
            o_ref[:, cols] = _dot(p.astype(BF16), mv_ref[:, cols]).astype(BF16)

    return pl.pallas_call(
        body, grid=(s // t,),
        in_specs=[pl.BlockSpec((t, 512), lambda i: (i, C_MQ // 512)), pl.BlockSpec((nm, 512), lambda i: (0, 0)),
                  pl.BlockSpec((nm, 512), lambda i: (0, 1))],
        out_specs=pl.BlockSpec((t, 512), lambda i: (i, 0)),
        out_shape=jax.ShapeDtypeStruct((s, 512), BF16),
        name="mem_attn_fwd", compiler_params=_cp("parallel"))(pm, mkv, mkv)


def _mem_attn_bwd(pm, mkv, do):
    s = pm.shape[0]
    t = min(s, 1024)
    nm = mkv.shape[0]

    def body(q_ref, mk_ref, mv_ref, do_ref, dq_ref, dmk_ref, dmv_ref):
        @pl.when(pl.program_id(0) == 0)
        def _():
            dmk_ref[...] = jnp.zeros_like(dmk_ref)
            dmv_ref[...] = jnp.zeros_like(dmv_ref)

        for h in range(4):
            cols = slice(128 * h, 128 * (h + 1))
            qh, kh, vh, doh = q_ref[:, cols], mk_ref[:, cols], mv_ref[:, cols], do_ref[:, cols]
            sc = _dot_nt(qh, kh) * MEM_SCALE
            p = jnp.exp(sc - jnp.max(sc, axis=-1, keepdims=True))
            p = p / jnp.sum(p, axis=-1, keepdims=True)
            pb = p.astype(BF16)
            dp = _dot_nt(doh, vh)
            ds = (p * (dp - jnp.sum(p * dp, axis=-1, keepdims=True)) * MEM_SCALE).astype(BF16)
            dq_ref[:, cols] = _dot(ds, kh).astype(BF16)
            dmk_ref[:, cols] += _dot_tn(ds, qh)
            dmv_ref[:, cols] += _dot_tn(pb, doh)

    return pl.pallas_call(
        body, grid=(s // t,),
        in_specs=[pl.BlockSpec((t, 512), lambda i: (i, C_MQ // 512)), pl.BlockSpec((nm, 512), lambda i: (0, 0)),
                  pl.BlockSpec((nm, 512), lambda i: (0, 1)), pl.BlockSpec((t, 512), lambda i: (i, 0))],
        out_specs=[pl.BlockSpec((t, 512), lambda i: (i, 0)), pl.BlockSpec((nm, 512), lambda i: (0, 0)),
                   pl.BlockSpec((nm, 512), lambda i: (0, 0))],
        out_shape=[jax.ShapeDtypeStruct((s, 512), BF16), jax.ShapeDtypeStruct((nm, 512), F32),
                   jax.ShapeDtypeStruct((nm, 512), F32)],
        name="mem_attn_bwd", compiler_params=_cp("arbitrary"))(pm, mkv, mkv, do)


def _gain_grad(dxn_g, x, r, name):
    m, d = x.shape

    def body(d_ref, x_ref, r_ref, o_ref):
        o_ref[...] = _fold8(d_ref[...] * (x_ref[...] * r_ref[...]))

    return pl.pallas_call(body, out_shape=jax.ShapeDtypeStruct((8, d), F32), name=name,
                          compiler_params=pltpu.CompilerParams(vmem_limit_bytes=VMEM_LIMIT_BYTES))(dxn_g, x, r)


def _head_norm(o, gh):
    xs, rs = [], []
    for h in range(4):
        oh = o[:, 128 * h:128 * (h + 1)]
        r = lax.rsqrt(jnp.mean(oh * oh, axis=-1, keepdims=True) + EPS)
        xs.append(oh * r)
        rs.append(r)
    return xs, rs


def _merge_fwd(x, pm, o_gla, o_fox_t, o_mem, g_head, wg, wf, wm, wo, g_ffn):
    s = x.shape[0]
    t = min(s, 512)

    def body(x_ref, g0_ref, g1_ref, g2_ref, gg_ref, og_ref, of_ref, om_ref, gh_ref, wg_ref, wf_ref, wm_ref, wo_ref, gf_ref,
             mg_ref, h1_ref, u2_ref, r2_ref):
        xs, _ = _head_norm(og_ref[...], None)
        gg = gg_ref[...].astype(F32)
        sil = gg * _sigmoid(gg)
        ogn = jnp.concatenate(xs, axis=1) * gh_ref[...] * sil
        merged = (_sigmoid(g0_ref[...].astype(F32)) * _dot(ogn.astype(BF16), wg_ref[...])
                  + _sigmoid(g1_ref[...].astype(F32)) * _dot(of_ref[...].T, wf_ref[...])
                  + _sigmoid(g2_ref[...].astype(F32)) * _dot(om_ref[...], wm_ref[...]))
        mb = merged.astype(BF16)
        mg_ref[...] = mb
        h1 = x_ref[...] + _dot(mb, wo_ref[...])
        h1_ref[...] = h1
        r = lax.rsqrt(jnp.mean(h1 * h1, axis=-1, keepdims=True) + EPS)
        u2_ref[...] = ((h1 * r) * gf_ref[...]).astype(BF16)
        r2_ref[...] = r

    row = lambda i: (i, 0)
    const = lambda i: (0, 0)
    return pl.pallas_call(
        body, grid=(s // t,),
        in_specs=[pl.BlockSpec((t, D), row), pl.BlockSpec((t, D), lambda i: (i, 0)), pl.BlockSpec((t, D), lambda i: (i, 1)),
                  pl.BlockSpec((t, D), lambda i: (i, 2)), pl.BlockSpec((t, 512), lambda i: (i, C_GG // 512)),
                  pl.BlockSpec((t, 512), row), pl.BlockSpec((512, t), lambda i: (0, i)), pl.BlockSpec((t, 512), row),
                  pl.BlockSpec((1, 512), const), pl.BlockSpec((512, D), const), pl.BlockSpec((512, D), const),
                  pl.BlockSpec((512, D), const), pl.BlockSpec((D, D), const), pl.BlockSpec((1, D), const)],
        out_specs=[pl.BlockSpec((t, D), row), pl.BlockSpec((t, D), row), pl.BlockSpec((t, D), row), pl.BlockSpec((t, 1), row)],
        out_shape=[jax.ShapeDtypeStruct((s, D), BF16), jax.ShapeDtypeStruct((s, D), F32),
                   jax.ShapeDtypeStruct((s, D), BF16), jax.ShapeDtypeStruct((s, 1), F32)],
        name="merge_fwd", compiler_params=_cp("parallel"))(x, pm, pm, pm, pm, o_gla, o_fox_t, o_mem, g_head, wg, wf, wm, wo, g_ffn)


def _merge_bwd(dh1b, pm, o_gla, o_fox_t, o_mem, g_head, wg, wf, wm, wgt, wft, wmt, wot, spread, d_to_do):
    s = dh1b.shape[0]
    t = min(s, 256)

    def body(dh_ref, g0_ref, g1_ref, g2_ref, gg_ref, og_ref, of_ref, om_ref, gh_ref, wg_ref, wf_ref, wm_ref,
             wgt_ref, wft_ref, wmt_ref, wot_ref, sp_ref, dd_ref,
             dgt_ref, dgg_ref, dog_ref, da_ref, dot_ref, dom_ref, dwg_ref, dwf_ref, dwm_ref, dgh_ref):
        @pl.when(pl.program_id(0) == 0)
        def _():
            dwg_ref[...] = jnp.zeros_like(dwg_ref)
            dwf_ref[...] = jnp.zeros_like(dwf_ref)
            dwm_ref[...] = jnp.zeros_like(dwm_ref)
            dgh_ref[...] = jnp.zeros_like(dgh_ref)

        dmerged = _dot(dh_ref[...], wot_ref[...])
        og = og_ref[...]
        xs, rs = _head_norm(og, None)
        on = jnp.concatenate(xs, axis=1)
        gg = gg_ref[...].astype(F32)
        sg = _sigmoid(gg)
        sil = gg * sg
        gh = gh_ref[...]
        ognb = (on * gh * sil).astype(BF16)
        ofb, omb = of_ref[...].T, om_ref[...]
        douts = []
        for idx, (gref, ob, w_ref, wt_ref, dw_ref) in enumerate((
                (g0_ref, ognb, wg_ref, wgt_ref, dwg_ref), (g1_ref, ofb, wf_ref, wft_ref, dwf_ref),
                (g2_ref, omb, wm_ref, wmt_ref, dwm_ref))):
            gt = _sigmoid(gref[...].astype(F32))
            y = _dot(ob, w_ref[...])
            dgt_ref[:, D * idx:D * (idx + 1)] = (dmerged * y * gt * (1.0 - gt)).astype(BF16)
            dy = (gt * dmerged).astype(BF16)
            dw_ref[...] += _dot_tn(ob, dy)
            douts.append(_dot(dy, wt_ref[...]))
        dogn, dof, dom = douts
        dofb = dof.astype(BF16)
        dom_ref[...] = dom.astype(BF16)
        ind = (_iota((1536, 128), 0) % 512 // 64 == _iota((1536, 128), 1)).astype(BF16)
        delta = _dot(_split3(dofb.astype(F32) * ofb.astype(F32)), ind)
        da_ref[...] = (_spread(dofb, sp_ref[...]) + _dot(_pieces(delta, 0), dd_ref[...])).astype(BF16)
        dot_ref[...] = dofb.T
        dgg_ref[...] = (dogn * on * gh * (sg * (1.0 + gg * (1.0 - sg)))).astype(BF16)
        d_on = dogn * sil
        dgh_ref[...] += _fold8(d_on * on)
        dxn = d_on * gh
        outs = []
        for h in range(4):
            cols = slice(128 * h, 128 * (h + 1))
            dh_, xh = dxn[:, cols], xs[h]
            outs.append(rs[h] * (dh_ - xh * jnp.mean(dh_ * xh, axis=-1, keepdims=True)))
        dog_ref[...] = jnp.concatenate(outs, axis=1).astype(BF16)

    row = lambda i: (i, 0)
    const = lambda i: (0, 0)
    return pl.pallas_call(
        body, grid=(s // t,),
        in_specs=[pl.BlockSpec((t, D), row), pl.BlockSpec((t, D), lambda i: (i, 0)), pl.BlockSpec((t, D), lambda i: (i, 1)),
                  pl.BlockSpec((t, D), lambda i: (i, 2)), pl.BlockSpec((t, 512), lambda i: (i, C_GG // 512)),
                  pl.BlockSpec((t, 512), row), pl.BlockSpec((512, t), lambda i: (0, i)), pl.BlockSpec((t, 512), row),
                  pl.BlockSpec((1, 512), const), pl.BlockSpec((512, D), const), pl.BlockSpec((512, D), const),
                  pl.BlockSpec((512, D), const), pl.BlockSpec((D, 512), const), pl.BlockSpec((D, 512), const),
                  pl.BlockSpec((D, 512), const), pl.BlockSpec((D, D), const),
                  pl.BlockSpec((128, 256), const), pl.BlockSpec((128, 1024), const)],
        out_specs=[pl.BlockSpec((t, 3 * D), row), pl.BlockSpec((t, 512), row), pl.BlockSpec((t, 512), row),
                   pl.BlockSpec((t, 1024), row), pl.BlockSpec((512, t), lambda i: (0, i)), pl.BlockSpec((t, 512), row),
                   pl.BlockSpec((512, D), const), pl.BlockSpec((512, D), const), pl.BlockSpec((512, D), const),
                   pl.BlockSpec((8, 512), const)],
        out_shape=[jax.ShapeDtypeStruct((s, 3 * D), BF16), jax.ShapeDtypeStruct((s, 512), BF16),
                   jax.ShapeDtypeStruct((s, 512), BF16), jax.ShapeDtypeStruct((s, 1024), BF16),
                   jax.ShapeDtypeStruct((512, s), BF16), jax.ShapeDtypeStruct((s, 512), BF16),
                   jax.ShapeDtypeStruct((512, D), F32), jax.ShapeDtypeStruct((512, D), F32),
                   jax.ShapeDtypeStruct((512, D), F32), jax.ShapeDtypeStruct((8, 512), F32)],
        name="merge_bwd", compiler_params=_cp("arbitrary"))(
            dh1b, pm, pm, pm, pm, o_gla, o_fox_t, o_mem, g_head, wg, wf, wm, wgt, wft, wmt, wot, spread, d_to_do)


def _ff2_loss(a, w2, h1, g_final, target):
    s, k = a.shape
    tm = min(s, 512)

    def body(a_ref, w_ref, h1_ref, g_ref, t_ref, dh_ref, dhb_ref, loss_ref, dg_ref):
        @pl.when(pl.program_id(0) == 0)
        def _():
            loss_ref[...] = jnp.zeros_like(loss_ref)
            dg_ref[...] = jnp.zeros_like(dg_ref)

        h2 = h1_ref[...] + _dot(_relu2_bf16(a_ref[...]), w_ref[...])
        r = lax.rsqrt(jnp.mean(h2 * h2, axis=-1, keepdims=True) + EPS)
        xn = h2 * r
        g = g_ref[...]
        err = xn * g - t_ref[...]
        e2 = _fold8(err * err)
        part = e2[:, 0:128]
        for c in range(1, D // 128):
            part = part + e2[:, 128 * c:128 * (c + 1)]
        loss_ref[...] += part
        dy = err * (1.0 / D)
        dg_ref[...] += _fold8(dy * xn)
        dxn = dy * g
        dh = r * (dxn - xn * jnp.mean(dxn * xn, axis=-1, keepdims=True))
        dh_ref[...] = dh
        dhb_ref[...] = dh.astype(BF16)

    row = lambda i: (i, 0)
    const = lambda i: (0, 0)
    return pl.pallas_call(
        body, grid=(s // tm,),
        in_specs=[pl.BlockSpec((tm, k), row), pl.BlockSpec((k, D), const, pipeline_mode=pl.Buffered(1)),
                  pl.BlockSpec((tm, D), row), pl.BlockSpec((1, D), const), pl.BlockSpec((tm, D), row)],
        out_specs=[pl.BlockSpec((tm, D), row), pl.BlockSpec((tm, D), row), pl.BlockSpec((8, 128), const),
                   pl.BlockSpec((8, D), const)],
        out_shape=[jax.ShapeDtypeStruct((s, D), F32), jax.ShapeDtypeStruct((s, D), BF16),
                   jax.ShapeDtypeStruct((8, 128), F32), jax.ShapeDtypeStruct((8, D), F32)],
        name="ff2_loss", compiler_params=_cp("arbitrary"))(a, w2, h1, g_final, target)


def _adam(w, g, m, v, name):
    _, r, c = w.shape
    tr = r
    for cand in (512, 256, 128, 64, 32, 16, 8):
        if r % cand == 0 and cand * c * 4 <= (1 << 20):
            tr = cand
            break
    c1 = 1.0 - ADAM_B1 ** ADAM_STEP
    c2 = 1.0 - ADAM_B2 ** ADAM_STEP

    def body(w_ref, g_ref, m_ref, v_ref, d_ref, nm_ref, nv_ref):
        gv = g_ref[...]
        nm = ADAM_B1 * m_ref[...] + (1.0 - ADAM_B1) * gv
        nv = ADAM_B2 * v_ref[...] + (1.0 - ADAM_B2) * (gv * gv)
        d_ref[...] = -ADAM_LR * ((nm / c1) / (jnp.sqrt(nv / c2) + ADAM_EPS) + ADAM_WD * w_ref[...])
        nm_ref[...] = nm
        nv_ref[...] = nv

    spec = pl.BlockSpec((1, tr, c), lambda i: (0, i, 0))
    return pl.pallas_call(
        body, grid=(r // tr,), in_specs=[spec] * 4, out_specs=[spec] * 3,
        out_shape=[jax.ShapeDtypeStruct((1, r, c), F32)] * 3, name=name, compiler_params=_cp("parallel"))(w, g, m, v)


def _row_block(r):
    return max(d for d in range(16, 513, 16) if r % d == 0)


def _add_half(core, a, b, name):
    n, r, c = b.shape
    tr = _row_block(r)

    def body(core_ref, a_ref, b_ref, o_ref):
        o_ref[...] = (a_ref[...].astype(F32) + b_ref[...].astype(F32)).astype(BF16)

    spec = pl.BlockSpec((1, tr, c), lambda k, i, core_ref: (k, i, 0))
    half = pl.BlockSpec((1, tr, c), lambda k, i, core_ref: (k, i + core_ref[0] * (r // tr), 0))
    return pl.pallas_call(
        body, grid_spec=pltpu.PrefetchScalarGridSpec(num_scalar_prefetch=1, grid=(n, r // tr), in_specs=[half, spec],
                                                     out_specs=spec),
        out_shape=jax.ShapeDtypeStruct((n, r, c), BF16), name=name, compiler_params=_cp("parallel", "parallel"))(core, a, b)


def _sum4(a, name):
    _, r, c = a.shape
    tr = _row_block(r)

    def body(a_ref, o_ref):
        o_ref[...] = ((a_ref[0].astype(F32) + a_ref[1].astype(F32)) + a_ref[2].astype(F32)) + a_ref[3].astype(F32)

    return pl.pallas_call(body, grid=(r // tr,), in_specs=[pl.BlockSpec((4, tr, c), lambda i: (0, i, 0))],
                          out_specs=pl.BlockSpec((tr, c), lambda i: (i, 0)),
                          out_shape=jax.ShapeDtypeStruct((r, c), F32), name=name, compiler_params=_cp("parallel"))(a)


def _adam_small(w, gathered, m, v):
    c1 = 1.0 - ADAM_B1 ** ADAM_STEP
    c2 = 1.0 - ADAM_B2 ** ADAM_STEP

    def body(w_ref, g_ref, m_ref, v_ref, gs_ref, d_ref, nm_ref, nv_ref):
        gv = g_ref[0]
        for dev in range(1, N_DEV):
            gv = gv + g_ref[dev]
        gs_ref[...] = gv
        nm = ADAM_B1 * m_ref[...] + (1.0 - ADAM_B1) * gv
        nv = ADAM_B2 * v_ref[...] + (1.0 - ADAM_B2) * (gv * gv)
        d_ref[...] = -ADAM_LR * ((nm / c1) / (jnp.sqrt(nv / c2) + ADAM_EPS) + ADAM_WD * w_ref[...])
        nm_ref[...] = nm
        nv_ref[...] = nv

    return pl.pallas_call(body, out_shape=[jax.ShapeDtypeStruct((8, D), F32)] * 4, name="adam_small")(w, gathered, m, v)


def _place():
    return lax.axis_index("x"), lax.axis_index("y"), lax.axis_index("c")


def _other_chips(x, y):
    return [(1 - x, y), (x, 1 - y), (1 - x, 1 - y)]


GATHER_SEMS = [pltpu.SemaphoreType.DMA((6,)), pltpu.SemaphoreType.DMA((6,)), pltpu.SemaphoreType.DMA]


def _gather_ops(in_refs, out_refs, sems):
    (p_ref,), (out_ref,) = in_refs, out_refs
    send_sems, recv_sems, local_sem = sems
    hr = p_ref.shape[0] // 2
    x, y, cc = _place()
    sibling = (x, y, 1 - cc)
    chips = _other_chips(x, y)

    def half(chip, core):
        return out_ref.at[2 * chip[0] + chip[1], pl.ds(core * hr, hr), :]

    def copy(k, chip, core, to, src=None):
        return pltpu.make_async_remote_copy(
            src_ref=half(chip, core) if src is None else src, dst_ref=half(chip, core),
            send_sem=send_sems.at[k], recv_sem=recv_sems.at[k], device_id=to, device_id_type=MESH)

    mine = pltpu.make_async_copy(p_ref, out_ref.at[2 * x + y], local_sem)
    my_half = p_ref.at[pl.ds(cc * hr, hr), :]
    first = [copy(j, (x, y), cc, (*chip, cc), src=my_half) for j, chip in enumerate(chips)]
    passed = [copy(3 + j, chip, cc, sibling) for j, chip in enumerate(chips)]

    def start():
        mine.start()
        for cp in first:
            cp.start()

    def finish():
        for j, chip in enumerate(chips):
            copy(j, chip, cc, (x, y, cc)).wait_recv()
            passed[j].start()
        for j, chip in enumerate(chips):
            copy(3 + j, chip, 1 - cc, (x, y, cc)).wait_recv()
        for cp in first + passed:
            cp.wait_send()
        mine.wait()

    return start, finish


def _gather_side(p):
    return _Side([p], [jax.ShapeDtypeStruct((N_CHIPS,) + p.shape, p.dtype)], GATHER_SEMS, _gather_ops)


def _swap_halves(g):
    n, r, c = g.shape
    hr = r // 2

    def body(g_ref, out_ref, send_sem, recv_sem):
        x, y, cc = _place()
        cp = pltpu.make_async_remote_copy(
            src_ref=g_ref.at[:, pl.ds((1 - cc) * hr, hr), :], dst_ref=out_ref,
            send_sem=send_sem, recv_sem=recv_sem, device_id=(x, y, 1 - cc), device_id_type=MESH)
        cp.start()
        cp.wait()

    any_spec = pl.BlockSpec(memory_space=pl.ANY)
    return pl.pallas_call(
        body, out_shape=jax.ShapeDtypeStruct((n, hr, c), g.dtype), in_specs=[any_spec], out_specs=any_spec,
        scratch_shapes=[pltpu.SemaphoreType.DMA, pltpu.SemaphoreType.DMA], name="swap_halves")(g)


SCATTER_SEMS = [pltpu.SemaphoreType.DMA((7,)), pltpu.SemaphoreType.DMA((7,)), pltpu.SemaphoreType.DMA]


def _scatter_ops(in_refs, out_refs, sems):
    (p_ref,), (out_ref,) = in_refs, out_refs
    send_sems, recv_sems, local_sem = sems
    hr = p_ref.shape[1]
    x, y, cc = _place()
    me = 2 * x + y
    sibling = (x, y, 1 - cc)
    chips = _other_chips(x, y)
    ids = [2 * chip[0] + chip[1] for chip in chips]

    def land(src, core):
        return out_ref.at[src, pl.ds(core * hr, hr), :]

    def copy(k, src_ref, dst_ref, to):
        return pltpu.make_async_remote_copy(src_ref=src_ref, dst_ref=dst_ref, send_sem=send_sems.at[k],
                                            recv_sem=recv_sems.at[k], device_id=to, device_id_type=MESH)

    mine = pltpu.make_async_copy(p_ref.at[me], land(me, cc), local_sem)
    sends = [copy(j, p_ref.at[ids[j]], land(me, cc), (*chip, cc)) for j, chip in enumerate(chips)]
    sends.append(copy(3, p_ref.at[me], land(me, cc), sibling))
    passed = [copy(4 + j, land(ids[j], cc), land(ids[j], cc), sibling) for j in range(3)]

    def start():
        mine.start()
        for cp in sends:
            cp.start()

    def finish():
        for j in range(3):
            copy(j, p_ref.at[me], land(ids[j], cc), (x, y, cc)).wait_recv()
            passed[j].start()
        copy(3, p_ref.at[me], land(me, 1 - cc), (x, y, cc)).wait_recv()
        for j in range(3):
            copy(4 + j, p_ref.at[me], land(ids[j], 1 - cc), (x, y, cc)).wait_recv()
        for cp in sends + passed:
            cp.wait_send()
        mine.wait()

    return start, finish


def _scatter_side(p):
    n, hr, c = p.shape
    return _Side([p], [jax.ShapeDtypeStruct((n, 2 * hr, c), p.dtype)], SCATTER_SEMS, _scatter_ops)


def _gather_small(blk):
    m, n = blk.shape

    def body(x_ref, out_ref, send_sems, recv_sems, local_sem):
        x, y, cc = _place()
        me, sibling = (x, y, cc), (x, y, 1 - cc)
        chips = _other_chips(x, y)

        def slot(px, py, pc):
            return out_ref.at[4 * px + 2 * py + pc]

        def copy(k, block, to, src=None):
            return pltpu.make_async_remote_copy(
                src_ref=slot(*block) if src is None else src, dst_ref=slot(*block),
                send_sem=send_sems.at[k], recv_sem=recv_sems.at[k], device_id=to, device_id_type=MESH)

        mine = pltpu.make_async_copy(x_ref, slot(*me), local_sem)
        mine.start()
        first = [copy(0, me, sibling, src=x_ref)]
        first += [copy(1 + j, me, (*chip, cc), src=x_ref) for j, chip in enumerate(chips)]
        for cp in first:
            cp.start()
        passed = [copy(4 + j, (*chip, cc), sibling) for j, chip in enumerate(chips)]
        for j, chip in enumerate(chips):
            copy(1 + j, (*chip, cc), me).wait_recv()
            passed[j].start()
        copy(0, sibling, me).wait_recv()
        for j, chip in enumerate(chips):
            copy(4 + j, (*chip, 1 - cc), me).wait_recv()
        for cp in first + passed:
            cp.wait_send()
        mine.wait()

    vmem = pl.BlockSpec(memory_space=pltpu.VMEM)
    return pl.pallas_call(
        body, out_shape=jax.ShapeDtypeStruct((N_DEV, m, n), blk.dtype), in_specs=[vmem], out_specs=vmem,
        scratch_shapes=[pltpu.SemaphoreType.DMA((7,)), pltpu.SemaphoreType.DMA((7,)), pltpu.SemaphoreType.DMA],
        name="gather_small")(blk)


def _pack_a(sh, dtype):
    w = sh["w_in"].astype(dtype)
    return jnp.concatenate([w[:, 0:PACK_W], jnp.pad(w[:, PACK_W:], ((0, 0), (0, 2 * PACK_W - w.shape[1])))], axis=0)


def _pack_b(sh, dtype):
    o3 = jnp.concatenate([sh["w_gla_o"], sh["w_fox_o"], sh["w_mem_o"], jnp.zeros((512, 256), sh["w_gla_o"].dtype)], axis=1)
    au = jnp.pad(sh["w_alpha_up"], ((0, PACK_ROWS_B - 3072 - 16), (0, PACK_W - 64)))
    return jnp.concatenate([sh["w_ff1"], sh["w_ff2"], sh["w_mem_kv"], sh["w_out"], o3, au], axis=0).astype(dtype)


def _unpack_a(pa):
    return {"w_in": jnp.concatenate([pa[0:1024], pa[1024:2048, 0:1670 - PACK_W]], axis=1)}


def _unpack_b(pb):
    return {"w_ff1": pb[0:1024], "w_ff2": pb[1024:2048], "w_mem_kv": pb[2048:2304], "w_out": pb[2304:2560],
            "w_gla_o": pb[2560:3072, 0:256], "w_fox_o": pb[2560:3072, 256:512], "w_mem_o": pb[2560:3072, 512:768],
            "w_alpha_up": pb[3072:3088, 0:64]}


def _unpack(packed):
    return {**_unpack_a(packed[0:PACK_ROWS_A]), **_unpack_b(packed[PACK_ROWS_A:])}


def _split_shards(name, full):
    return jnp.split(full, N_CHIPS, axis=SHARD_AXIS[name])


def _pack_small(vals, scalar=None):
    row4 = jnp.concatenate([vals["b_alpha"].reshape(-1), vals["b_forget"].reshape(-1), jnp.zeros((D - 264,), F32)])
    row5 = jnp.concatenate([vals["g_gla_head"].reshape(-1), jnp.zeros((D - 512,), F32)])
    row6 = jnp.zeros((D,), F32) if scalar is None else jnp.broadcast_to(scalar, (D,))
    rows = [vals["g_mix"].reshape(-1), vals["g_mem"].reshape(-1), vals["g_ffn"].reshape(-1), vals["g_final"].reshape(-1),
            row4, row5, row6, jnp.zeros((D,), F32)]
    return jnp.stack(rows)


def _unpack_small(blk):
    return {"g_mix": blk[0].reshape(1, D), "g_mem": blk[1].reshape(1, D), "g_ffn": blk[2].reshape(1, D),
            "g_final": blk[3].reshape(D), "b_alpha": blk[4, 0:256].reshape(1, 256), "b_forget": blk[4, 256:264].reshape(1, 8),
            "g_gla_head": blk[5, 0:512].reshape(1, 4, 128)}


def _local_step(x, mem, target, wb, small, exchange=None):
    s = x.shape[0]
    nm = mem.shape[0]
    t = _row_tile(s)
    nb = s // t
    b_alpha = small["b_alpha"].reshape(1, 256)
    bias_e = jnp.concatenate([jnp.zeros((FF_LANE,), F32), small["b_forget"].reshape(-1),
                              jnp.zeros((PE_W - FF_LANE - 8,), F32)]).reshape(1, PE_W)
    g_mix, g_mem, g_ffn = small["g_mix"].reshape(1, D), small["g_mem"].reshape(1, D), small["g_ffn"].reshape(1, D)
    g_final = small["g_final"].reshape(1, D)
    g_head = small["g_gla_head"].reshape(1, 512)

    if exchange is None:
        u, r1 = _rms_fwd(x, g_mix, "norm_mix")
    else:
        u, r1, gathered = _rms_fwd(x, g_mix, "norm_mix", side=exchange.gather_a)
        wb = exchange.weights_a(gathered)
    w_in = wb["w_in"]
    w_main = jnp.concatenate([w_in[:, 3608:6680], w_in[:, 0:1536], w_in[:, 1552:3088], w_in[:, 3096:3608]], axis=1)
    w_e = jnp.concatenate([w_in[:, 1536:1552], w_in[:, 3088:3096], jnp.zeros((D, PE_W - 24), BF16)], axis=1)
    w_in_pt = _transpose(jnp.concatenate([w_main, w_e], axis=1), "t_w_in")
    big = min(s, 1024)
    if exchange is None:
        pm, pe = _proj(u, w_main, w_e)
    else:
        pm, pe, gathered = _proj(u, w_main, w_e, side=exchange.gather_b)
        wb = {**wb, **exchange.weights_b(gathered)}
    wau_p = jnp.concatenate([wb["w_alpha_up"], jnp.zeros((PE_W - 16, 256), BF16)], axis=0)
    o_gla, states = _gla_fwd(pm, pe, wau_p, b_alpha)
    fcum = _fcum_fwd(pe, bias_e)
    tb = _fox_tables()
    qf_aug, k_aug, v_aug, vt, qt, kt = _fox_prep(pm, fcum, None, tb, backward=False)
    o_fox, lse = _fox_fwd(k_aug, qf_aug, vt)
    mn, rm = _rms_fwd(mem, g_mem, "norm_mem")
    mkv = _mm_nn(mn, wb["w_mem_kv"], out_dtype=BF16, tm=nm, tn=512, tk=D, name="mem_kv")
    o_mem = _mem_attn_fwd(pm, mkv)
    merged, h1, u2, r2 = _merge_fwd(x, pm, o_gla, o_fox, o_mem, g_head, wb["w_gla_o"], wb["w_fox_o"], wb["w_mem_o"],
                                    wb["w_out"], g_ffn)
    a = _mm_nn(u2, wb["w_ff1"], out_dtype=BF16, tm=big, tn=1024, tk=D, name="ff1")
    dh2, dh2b, loss8, dgfin8 = _ff2_loss(a, wb["w_ff2"], h1, g_final, target)
    loss = 0.5 * jnp.sum(loss8) / D

    da = _mm_nn(dh2b, _transpose(wb["w_ff2"], "t_w_ff2"), out_dtype=BF16, tm=big, tn=1024, tk=D, name="d_act",
                epi=lambda acc, at: acc * (2.0 * jnp.maximum(at.astype(F32), 0.0)), extra=a)
    gw = {}
    gw["w_ff2"] = _mm_tn(a, dh2b, tm=1024, tn=D, ts=big, name="dw_ff2", a_fn=_relu2_bf16)
    gw["w_ff1"] = _mm_tn(u2, da, tm=D, tn=1024, ts=big, name="dw_ff1")
    dh1, dh1b, dgffn8 = _mm_norm_bwd([da], _transpose(wb["w_ff1"], "t_w_ff1"), h1, r2, g_ffn, dh2, name="d_h1", want_bf16=True)
    gw["w_out"] = _mm_tn(merged, dh1b, tm=D, tn=D, ts=big, name="dw_out")
    (dgates, dgg, do_gla, do_aug, do_t, do_mem, gw["w_gla_o"], gw["w_fox_o"], gw["w_mem_o"], dgh8) = _merge_bwd(
        dh1b, pm, o_gla, o_fox, o_mem, g_head, wb["w_gla_o"], wb["w_fox_o"], wb["w_mem_o"],
        *[_transpose(wb[n], "t_" + n) for n in ("w_gla_o", "w_fox_o", "w_mem_o", "w_out")], tb["spread"], tb["d_to_do"])
    dgq, dgk, dgv, de_gla, dwau_p, dba8 = _gla_bwd(pm, pe, wau_p, wau_p.T, b_alpha, do_gla, states)
    gw["w_alpha_up"] = dwau_p[0:16, :]
    q_aug = _fox_prep(pm, fcum, lse.reshape(8, s), tb, backward=True)
    dfq_t, dfrow, dfk_t, dfv_t, dfcol = _fox_bwd(q_aug, do_aug, qt, do_t, k_aug, v_aug, kt)
    dfq, dfk, dfv, df = _fox_post(dfq_t, dfk_t, dfv_t, dfrow[:, 0, :], dfcol.reshape(8, s), tb)
    de_fox, dbf8 = _fcum_bwd(pe, bias_e, df)
    dmq, dmk, dmv = _mem_attn_bwd(pm, mkv, do_mem)
    dmkv = jnp.concatenate([dmk, dmv], axis=1).astype(BF16)
    gw["w_mem_kv"] = _mm_tn(mn, dmkv, tm=D, tn=D, ts=nm, name="dw_mem_kv")
    dmn_g = _mm_nn(dmkv, _transpose(wb["w_mem_kv"], "t_w_mem_kv"), out_dtype=F32, tm=nm, tn=D, tk=D, name="d_mem_norm")
    dgmem8 = _gain_grad(dmn_g, mem, rm, "dg_mem")
    de = (de_gla + de_fox).astype(BF16)
    dproj = [dgates, dgq, dgk, dgv, dgg, dfq, dfk, dfv, dmq, de]
    dw_gates = _mm_tn(u, dgates, tm=D, tn=1024, ts=big, name="dw_in_gates")
    dw_g = _mm_tn_cat(u, [dgq, dgk, dgv], ts=big, name="dw_in_gla")
    dw_gf = _mm_tn_cat(u, [dgg, dfq], ts=big, name="dw_in_gg_fq")
    dw_f = _mm_tn_cat(u, [dfk, dfv], ts=big, name="dw_in_fk_fv")
    dw_m = _mm_tn_cat(u, [dmq, de], ts=big, name="dw_in_mq_narrow")
    gw["w_in"] = jnp.concatenate([dw_g, dw_gf[:, 0:512], dw_m[:, 512:528], dw_gf[:, 512:1024], dw_f,
                                  dw_m[:, 528:536], dw_m[:, 0:512], dw_gates], axis=1)
    if exchange is None:
        grad_x, dgmix8 = _mm_norm_bwd(dproj, w_in_pt, x, r1, g_mix, dh1, name="d_x", want_bf16=False)
        exchanged = None
    else:
        grad_x, dgmix8, exchanged = _mm_norm_bwd(dproj, w_in_pt, x, r1, g_mix, dh1, name="d_x", want_bf16=False,
                                                 side=exchange.scatter(gw))
    gs = {"g_mix": dgmix8.sum(0), "g_mem": dgmem8.sum(0), "g_ffn": dgffn8.sum(0), "g_final": dgfin8.sum(0),
          "b_alpha": dba8.sum(0), "b_forget": dbf8.sum(0)[FF_LANE:FF_LANE + 8], "g_gla_head": dgh8.sum(0)}
    return loss, grad_x, gw, gs, exchanged


def kernel(x, mem, g_mix, w_in, w_alpha_up, b_alpha, b_forget, g_gla_head, g_mem, w_mem_kv, w_gla_o, w_fox_o, w_mem_o, w_out, g_ffn, w_ff1, w_ff2, g_final, loss_target, m_g_mix, m_w_in, m_w_alpha_up, m_b_alpha, m_b_forget, m_g_gla_head, m_g_mem, m_w_mem_kv, m_w_gla_o, m_w_fox_o, m_w_mem_o, m_w_out, m_g_ffn, m_w_ff1, m_w_ff2, m_g_final, v_g_mix, v_w_in, v_w_alpha_up, v_b_alpha, v_b_forget, v_g_gla_head, v_g_mem, v_w_mem_kv, v_w_gla_o, v_w_fox_o, v_w_mem_o, v_w_out, v_g_ffn, v_w_ff1, v_w_ff2, v_g_final):
    args = dict(locals())
    w_sh = {n: args[n][0] for n in WEIGHTS}
    small = {n: args[n] for n in SMALL}

    def whole(parts):
        return {n: jnp.concatenate([p[n] for p in parts], axis=SHARD_AXIS[n]) for n in parts[0]}

    class Exchange:
        gather_a = _gather_side(_pack_a(w_sh, BF16))
        gather_b = _gather_side(_pack_b(w_sh, BF16))

        @staticmethod
        def weights_a(gathered):
            return whole([_unpack_a(gathered[k]) for k in range(N_CHIPS)])

        @staticmethod
        def weights_b(gathered):
            return whole([_unpack_b(gathered[k]) for k in range(N_CHIPS)])

        @staticmethod
        def scatter(gw):
            by_chip = {n: _split_shards(n, gw[n]) for n in WEIGHTS}
            packed = jnp.stack([jnp.concatenate([_pack_a({n: by_chip[n][k] for n in WEIGHTS}, BF16),
                                                 _pack_b({n: by_chip[n][k] for n in WEIGHTS}, BF16)], axis=0)
                                for k in range(N_CHIPS)])
            core = lax.axis_index("c").astype(jnp.int32).reshape(1)
            return _scatter_side(_add_half(core, packed, _swap_halves(packed), "chip_sum"))

    loss, grad_x, gw, gs, by_chip = _local_step(x[0], mem[0], loss_target[0], None, small, Exchange)
    g_out = {n: g[None] for n, g in _unpack(_sum4(by_chip, "shard_sum")).items()}
    d_out, m_out, v_out = {}, {}, {}
    for n in WEIGHTS:
        d_out[n], m_out[n], v_out[n] = _adam(args[n], g_out[n], args["m_" + n], args["v_" + n], "adam_" + n)

    small_all = _gather_small(_pack_small(gs, loss))
    sm = {n: args["m_" + n] for n in SMALL}
    sv = {n: args["v_" + n] for n in SMALL}
    gs_sum, sd, snm, snv = _adam_small(_pack_small(small), small_all, _pack_small(sm), _pack_small(sv))
    gs_o, sd_o, snm_o, snv_o = _unpack_small(gs_sum), _unpack_small(sd), _unpack_small(snm), _unpack_small(snv)

    names = ["g_mix", "w_in", "w_alpha_up", "b_alpha", "b_forget", "g_gla_head", "g_mem", "w_mem_kv", "w_gla_o", "w_fox_o",
             "w_mem_o", "w_out", "g_ffn", "w_ff1", "w_ff2", "g_final"]

    def pick(big, sml, n):
        return big[n] if n in big else sml[n]

    outs = [gs_sum[6, 0], grad_x[None]]
    for big, sml in ((g_out, gs_o), (d_out, sd_o), (m_out, snm_o), (v_out, snv_o)):
        outs += [pick(big, sml, n) for n in names]
    return tuple(outs)
```

```python
import functools

import numpy as np
import jax
import jax.numpy as jnp
from jax import lax
from jax.experimental import pallas as pl
from jax.experimental.pallas import tpu as pltpu

F32 = jnp.float32
BF16 = jnp.bfloat16
HI = lax.Precision.HIGHEST
MESH = pl.DeviceIdType.MESH

EPS = 1e-6
D = 1024
CHUNK = 64
GLA_TAU = 16.0
N_CHIPS = 4
N_DEV = 8
VMEM_LIMIT_BYTES = 56 * 1024 * 1024

ADAM_LR, ADAM_B1, ADAM_B2, ADAM_EPS, ADAM_WD, ADAM_STEP = 0.001, 0.9, 0.999, 1e-08, 0.01, 10

PM_W = 6656
PE_W = 128
C_GQ, C_GK, C_GV, C_GG, C_FQ, C_FK, C_FV, C_MQ = 3072, 3328, 3584, 4096, 4608, 5120, 5632, 6144
FF_LANE = 16

WEIGHTS = ("w_in", "w_alpha_up", "w_mem_kv", "w_gla_o", "w_fox_o", "w_mem_o", "w_out", "w_ff1", "w_ff2")
SHARD_AXIS = {"w_in": 1, "w_alpha_up": 1, "w_mem_kv": 0, "w_gla_o": 1, "w_fox_o": 1, "w_mem_o": 1, "w_out": 0,
              "w_ff1": 1, "w_ff2": 0}
SMALL = ("g_mix", "g_mem", "g_ffn", "g_final", "b_alpha", "b_forget", "g_gla_head")
PACK_W = 1024
PACK_ROWS_A = 2048
PACK_ROWS_B = 3104
PACK_ROWS = PACK_ROWS_A + PACK_ROWS_B


def _cp(*sem):
    return pltpu.CompilerParams(dimension_semantics=sem, vmem_limit_bytes=VMEM_LIMIT_BYTES)


def _dot(a, b, **kw):
    return jnp.dot(a, b, preferred_element_type=F32, **kw)


def _dot_nt(a, b, **kw):
    return lax.dot_general(a, b, (((1,), (1,)), ((), ())), preferred_element_type=F32, **kw)


def _dot_tn(a, b, **kw):
    return lax.dot_general(a, b, (((0,), (0,)), ((), ())), preferred_element_type=F32, **kw)


def _sigmoid(x):
    return 0.5 * jnp.tanh(0.5 * x) + 0.5


def _log_sigmoid(x):
    return -(jnp.maximum(-x, 0.0) + jnp.log1p(jnp.exp(-jnp.abs(x))))


def _fold8(x):
    m, n = x.shape
    return x.reshape(m // 8, 8, n).sum(axis=0)


def _iota(shape, dim):
    return lax.broadcasted_iota(jnp.int32, shape, dim)


def _row_tile(s):
    return min(s, 512)


class _Side:
    def __init__(self, inputs, out_shape, scratch, ops):
        self.inputs, self.out_shape, self.scratch, self.ops = list(inputs), list(out_shape), list(scratch), ops


ANY_SPEC = pl.BlockSpec(memory_space=pl.ANY)


def _mm_nn(a, b, *, out_dtype, tm, tn, tk, name, a_fn=None, epi=None, extra=None):
    m, k = a.shape
    _, n = b.shape
    nk = k // tk

    def body_one(*refs):
        a_ref, b_ref = refs[0], refs[1]
        at = a_ref[...] if a_fn is None else a_fn(a_ref[...])
        r = _dot(at, b_ref[...])
        if epi is not None:
            r = epi(r, None if extra is None else refs[2][...])
        refs[-1][...] = r.astype(out_dtype)

    if nk == 1:
        in_specs = [pl.BlockSpec((tm, k), lambda i, j: (i, 0)), pl.BlockSpec((k, tn), lambda i, j: (0, j))]
        args = [a, b]
        if extra is not None:
            in_specs.append(pl.BlockSpec((tm, tn), lambda i, j: (i, j)))
            args.append(extra)
        return pl.pallas_call(
            body_one, grid=(m // tm, n // tn), in_specs=in_specs, out_specs=pl.BlockSpec((tm, tn), lambda i, j: (i, j)),
            out_shape=jax.ShapeDtypeStruct((m, n), out_dtype), name=name, compiler_params=_cp("parallel", "parallel"))(*args)

    def body(*refs):
        if extra is None:
            a_ref, b_ref, o_ref, acc = refs
            x_ref = None
        else:
            a_ref, b_ref, x_ref, o_ref, acc = refs
        kk = pl.program_id(2)

        @pl.when(kk == 0)
        def _():
            acc[...] = jnp.zeros_like(acc)

        at = a_ref[...]
        if a_fn is not None:
            at = a_fn(at)
        acc[...] += _dot(at, b_ref[...])

        @pl.when(kk == nk - 1)
        def _():
            r = acc[...]
            if epi is not None:
                r = epi(r, None if x_ref is None else x_ref[...])
            o_ref[...] = r.astype(out_dtype)

    in_specs = [pl.BlockSpec((tm, tk), lambda i, j, kk: (i, kk)), pl.BlockSpec((tk, tn), lambda i, j, kk: (kk, j))]
    args = [a, b]
    if extra is not None:
        in_specs.append(pl.BlockSpec((tm, tn), lambda i, j, kk: (i, j)))
        args.append(extra)
    return pl.pallas_call(
        body, grid=(m // tm, n // tn, nk), in_specs=in_specs,
        out_specs=pl.BlockSpec((tm, tn), lambda i, j, kk: (i, j)),
        out_shape=jax.ShapeDtypeStruct((m, n), out_dtype),
        scratch_shapes=[pltpu.VMEM((tm, tn), F32)], name=name,
        compiler_params=_cp("parallel", "parallel", "arbitrary"))(*args)


def _mm_tn(a, b, *, tm, tn, ts, name, a_fn=None):
    s, m = a.shape
    _, n = b.shape
    ns = s // ts

    def body(a_ref, b_ref, o_ref, acc):
        kk = pl.program_id(2)

        @pl.when(kk == 0)
        def _():
            acc[...] = jnp.zeros_like(acc)

        at = a_ref[...]
        if a_fn is not None:
            at = a_fn(at)
        acc[...] += _dot_tn(at, b_ref[...])

        @pl.when(kk == ns - 1)
        def _():
            o_ref[...] = acc[...]

    return pl.pallas_call(
        body, grid=(m // tm, n // tn, ns),
        in_specs=[pl.BlockSpec((ts, tm), lambda i, j, kk: (kk, i)), pl.BlockSpec((ts, tn), lambda i, j, kk: (kk, j))],
        out_specs=pl.BlockSpec((tm, tn), lambda i, j, kk: (i, j)),
        out_shape=jax.ShapeDtypeStruct((m, n), F32),
        scratch_shapes=[pltpu.VMEM((tm, tn), F32)], name=name,
        compiler_params=_cp("parallel", "parallel", "arbitrary"))(a, b)


def _mm_tn_cat(a, bs, *, ts, name):
    s, m = a.shape
    n = sum(b.shape[1] for b in bs)
    ns = s // ts
    nb = len(bs)

    def body(*refs):
        a_ref, b_refs, o_ref, acc = refs[0], refs[1:1 + nb], refs[1 + nb], refs[2 + nb]
        kk = pl.program_id(0)

        @pl.when(kk == 0)
        def _():
            acc[...] = jnp.zeros_like(acc)

        bt = b_refs[0][...] if nb == 1 else jnp.concatenate([r[...] for r in b_refs], axis=1)
        acc[...] += _dot_tn(a_ref[...], bt)

        @pl.when(kk == ns - 1)
        def _():
            o_ref[...] = acc[...]

    return pl.pallas_call(
        body, grid=(ns,),
        in_specs=[pl.BlockSpec((ts, m), lambda kk: (kk, 0))] + [pl.BlockSpec((ts, b.shape[1]), lambda kk: (kk, 0)) for b in bs],
        out_specs=pl.BlockSpec((m, n), lambda kk: (0, 0)), out_shape=jax.ShapeDtypeStruct((m, n), F32),
        scratch_shapes=[pltpu.VMEM((m, n), F32)], name=name, compiler_params=_cp("arbitrary"))(a, *bs)


def _proj(u, w_main, w_e, side=None):
    s, k = u.shape
    n = w_main.shape[1]
    tm, tn = min(s, 1024), n // 4
    n_sin = 0 if side is None else len(side.inputs)
    n_sout = 0 if side is None else len(side.out_shape)

    def body(u_ref, w_ref, we_ref, *rest):
        pm_ref, pe_ref = rest[n_sin:n_sin + 2]
        i, j = pl.program_id(0), pl.program_id(1)
        if side is not None:
            start, forward, finish = side.ops(rest[:n_sin], rest[n_sin + 2:n_sin + 2 + n_sout], rest[n_sin + 2 + n_sout:])
            pl.when((i == 0) & (j == 0))(start)
        ut = u_ref[...]
        pm_ref[...] = _dot(ut, w_ref[...]).astype(BF16)

        @pl.when(j == 0)
        def _():
            pe_ref[...] = _dot(ut, we_ref[...])

        if side is not None:
            pl.when((i == (s // tm) * 5 // 8) & (j == n // tn - 1))(forward)
            pl.when((i == s // tm - 1) & (j == n // tn - 1))(finish)

    side_in = [] if side is None else side.inputs
    return pl.pallas_call(
        body, grid=(s // tm, n // tn),
        in_specs=[pl.BlockSpec((tm, k), lambda i, j: (i, 0)), pl.BlockSpec((k, tn), lambda i, j: (0, j)),
                  pl.BlockSpec((k, PE_W), lambda i, j: (0, 0))] + [ANY_SPEC] * n_sin,
        out_specs=[pl.BlockSpec((tm, tn), lambda i, j: (i, j)), pl.BlockSpec((tm, PE_W), lambda i, j: (i, 0))]
        + [ANY_SPEC] * n_sout,
        out_shape=[jax.ShapeDtypeStruct((s, n), BF16), jax.ShapeDtypeStruct((s, PE_W), F32)]
        + ([] if side is None else side.out_shape),
        scratch_shapes=[] if side is None else side.scratch,
        name="proj_main", compiler_params=_cp("arbitrary", "arbitrary"))(u, w_main, w_e, *side_in)


def _transpose(w, name):
    r, c = w.shape
    tr = min(r, 256)

    def body(w_ref, o_ref):
        o_ref[...] = w_ref[...].T

    return pl.pallas_call(body, grid=(r // tr,), in_specs=[pl.BlockSpec((tr, c), lambda i: (i, 0))],
                          out_specs=pl.BlockSpec((c, tr), lambda i: (0, i)),
                          out_shape=jax.ShapeDtypeStruct((c, r), w.dtype), name=name, compiler_params=_cp("parallel"))(w)


def _relu2_bf16(t):
    r = jnp.maximum(t.astype(F32), 0.0)
    return (r * r).astype(BF16)


def _rms_fwd(x, g, name, side=None):
    s, d = x.shape
    tm = min(s, 512)
    n_sin = 0 if side is None else len(side.inputs)
    n_sout = 0 if side is None else len(side.out_shape)

    def body(x_ref, g_ref, *rest):
        u_ref, r_ref = rest[n_sin:n_sin + 2]
        if side is not None:
            start, forward, finish = side.ops(rest[:n_sin], rest[n_sin + 2:n_sin + 2 + n_sout], rest[n_sin + 2 + n_sout:])
            pl.when(pl.program_id(0) == 0)(start)
        xv = x_ref[...]
        r = lax.rsqrt(jnp.mean(xv * xv, axis=-1, keepdims=True) + EPS)
        u_ref[...] = ((xv * r) * g_ref[...]).astype(BF16)
        r_ref[...] = r
        if side is not None:
            pl.when(pl.program_id(0) == s // tm - 1)(forward)
            pl.when(pl.program_id(0) == s // tm - 1)(finish)

    side_in = [] if side is None else side.inputs
    return pl.pallas_call(
        body, grid=(s // tm,),
        in_specs=[pl.BlockSpec((tm, d), lambda i: (i, 0)), pl.BlockSpec((1, d), lambda i: (0, 0))] + [ANY_SPEC] * n_sin,
        out_specs=[pl.BlockSpec((tm, d), lambda i: (i, 0)), pl.BlockSpec((tm, 1), lambda i: (i, 0))] + [ANY_SPEC] * n_sout,
        out_shape=[jax.ShapeDtypeStruct((s, d), BF16), jax.ShapeDtypeStruct((s, 1), F32)]
        + ([] if side is None else side.out_shape),
        scratch_shapes=[] if side is None else side.scratch,
        name=name, compiler_params=_cp("parallel" if side is None else "arbitrary"))(x, g, *side_in)


def _mm_norm_bwd(a_parts, b, xin, r, g, dres, *, name, want_bf16, side=None):
    s = a_parts[0].shape[0]
    k = b.shape[0]
    na = len(a_parts)
    offs = [sum(p.shape[1] for p in a_parts[:i]) for i in range(na)]
    assert offs[-1] + a_parts[-1].shape[1] == k
    tm = min(s, 512)
    n_out = 3 if want_bf16 else 2
    n_sin = 0 if side is None else len(side.inputs)
    n_sout = 0 if side is None else len(side.out_shape)

    def body(*refs):
        a_refs = refs[:na]
        b_ref, x_ref, r_ref, g_ref, dres_ref = refs[na:na + 5]
        rest = refs[na + 5:]
        outs = rest[n_sin:n_sin + n_out]
        dx_ref, dg_ref = outs[0], outs[-1]
        if side is not None:
            start, forward, finish = side.ops(rest[:n_sin], rest[n_sin + n_out:n_sin + n_out + n_sout], rest[n_sin + n_out + n_sout:])
            pl.when(pl.program_id(0) == 0)(start)

        @pl.when(pl.program_id(0) == 0)
        def _():
            dg_ref[...] = jnp.zeros_like(dg_ref)

        du = _dot(a_refs[0][...], b_ref[0:a_parts[0].shape[1], :])
        for a_ref, off, part in zip(a_refs[1:], offs[1:], a_parts[1:]):
            du = du + _dot(a_ref[...], b_ref[off:off + part.shape[1], :])
        xn = x_ref[...] * r_ref[...]
        dg_ref[...] += _fold8(du * xn)
        dxn = du * g_ref[...]
        dx = dres_ref[...] + r_ref[...] * (dxn - xn * jnp.mean(dxn * xn, axis=-1, keepdims=True))
        dx_ref[...] = dx
        if want_bf16:
            outs[1][...] = dx.astype(BF16)
        if side is not None:
            pl.when(pl.program_id(0) == (s // tm) * 13 // 16)(forward)
            pl.when(pl.program_id(0) == s // tm - 1)(finish)

    row = lambda i: (i, 0)
    const = lambda i: (0, 0)
    out_specs = [pl.BlockSpec((tm, D), row)]
    out_shape = [jax.ShapeDtypeStruct((s, D), F32)]
    if want_bf16:
        out_specs.append(pl.BlockSpec((tm, D), row))
        out_shape.append(jax.ShapeDtypeStruct((s, D), BF16))
    out_specs.append(pl.BlockSpec((8, D), const))
    out_shape.append(jax.ShapeDtypeStruct((8, D), F32))
    side_in = [] if side is None else side.inputs
    return pl.pallas_call(
        body, grid=(s // tm,),
        in_specs=[pl.BlockSpec((tm, p.shape[1]), row) for p in a_parts]
        + [pl.BlockSpec((k, D), const, pipeline_mode=pl.Buffered(1)),
           pl.BlockSpec((tm, D), row), pl.BlockSpec((tm, 1), row), pl.BlockSpec((1, D), const),
           pl.BlockSpec((tm, D), row)] + [ANY_SPEC] * n_sin,
        out_specs=out_specs + [ANY_SPEC] * n_sout, out_shape=out_shape + ([] if side is None else side.out_shape),
        scratch_shapes=[] if side is None else side.scratch,
        name=name, compiler_params=_cp("arbitrary"))(*a_parts, b, xin, r, g, dres, *side_in)


def _gla_consts():
    lmask = _iota((4 * CHUNK, CHUNK), 0) % CHUNK >= _iota((4 * CHUNK, CHUNK), 1)
    hmask = _iota((256, 256), 0) // CHUNK == _iota((256, 256), 1) // CHUNK
    bd = _iota((256, 512), 0) // CHUNK == _iota((256, 512), 1) // 128
    return lmask, hmask, bd


def _fold_heads(x):
    return x[0:64] + x[64:128] + x[128:192] + x[192:256]


def _gla_decays(la, b_scr, dec_scr):
    tri = (_iota((CHUNK, CHUNK), 0) >= _iota((CHUNK, CHUNK), 1)).astype(BF16)
    ones = jnp.ones((CHUNK, 128), BF16)
    for c in range(la.shape[0] // CHUNK):
        la3 = _split3(la[CHUNK * c:CHUNK * (c + 1)])
        b_scr[CHUNK * c:CHUNK * (c + 1), :] = _sum3(_dot(tri, la3), 1)
        dec_scr[c] = jnp.exp(_sum3(_dot_tn(la3, ones), 0))


def _gla_chunk(b, qc, kc):
    bl = b[CHUNK - 1:CHUNK, :]
    ep, en, ek = jnp.exp(b), jnp.exp(-b), jnp.exp(bl - b)
    return bl, ep, en, ek, qc * ep, qc * en, kc * en, kc * ep, kc * ek


def _gla_fwd(pm, pe, wau_p, b_alpha):
    s = pm.shape[0]
    t = min(s, 1024)
    nc = t // CHUNK

    def body(q_ref, k_ref, v_ref, e_ref, wau_ref, ba_ref, o_ref, st_ref, state, b_scr, dec_scr):
        @pl.when(pl.program_id(0) == 0)
        def _():
            state[...] = jnp.zeros_like(state)

        z = _dot(e_ref[...].astype(BF16), wau_ref[...]) + ba_ref[...]
        _gla_decays(_log_sigmoid(z) * (1.0 / GLA_TAU), b_scr, dec_scr)
        lmask, hmask, bd = _gla_consts()

        def chunk(c, carry):
            rows = pl.ds(pl.multiple_of(c * CHUNK, CHUNK), CHUNK)
            qc = q_ref[rows, :].astype(F32) * 0.125
            kc = k_ref[rows, :].astype(F32)
            vc = v_ref[rows, :]
            _, _, _, _, qp, qn, kn, kp, kk = _gla_chunk(b_scr[rows, :], qc, kc)
            decb = jnp.concatenate([dec_scr[c]] * 4, axis=1)
            qs = jnp.where(hmask, jnp.concatenate([qp] * 4, axis=0), 0.0).astype(BF16)
            qns = jnp.where(hmask, jnp.concatenate([qn] * 4, axis=0), 0.0).astype(BF16)
            attn = jnp.where(lmask, _dot_nt(qs, kn.astype(BF16)), _dot_nt(qns, kp.astype(BF16))).astype(BF16)
            st = state[...]
            o_intra = _fold_heads(jnp.where(bd, _dot(attn, vc), 0.0))
            o_ref[rows, :] = o_intra + _dot(qp.astype(BF16), st.astype(BF16))
            for h in range(4):
                st_ref[c, :, 128 * h:128 * (h + 1)] = st[64 * h:64 * (h + 1), 128 * h:128 * (h + 1)]
            kv = jnp.where(bd, _dot_tn(kk.astype(BF16), vc), 0.0)
            state[...] = st * decb + kv
            return carry

        lax.fori_loop(0, nc, chunk, 0)

    return pl.pallas_call(
        body, grid=(s // t,),
        in_specs=[pl.BlockSpec((t, 256), lambda i: (i, C_GQ // 256)), pl.BlockSpec((t, 256), lambda i: (i, C_GK // 256)),
                  pl.BlockSpec((t, 512), lambda i: (i, C_GV // 512)), pl.BlockSpec((t, PE_W), lambda i: (i, 0)),
                  pl.BlockSpec((PE_W, 256), lambda i: (0, 0)), pl.BlockSpec((1, 256), lambda i: (0, 0))],
        out_specs=[pl.BlockSpec((t, 512), lambda i: (i, 0)), pl.BlockSpec((nc, CHUNK, 512), lambda i: (i, 0, 0))],
        out_shape=[jax.ShapeDtypeStruct((s, 512), F32), jax.ShapeDtypeStruct((s // CHUNK, CHUNK, 512), F32)],
        scratch_shapes=[pltpu.VMEM((256, 512), F32), pltpu.VMEM((t, 256), F32), pltpu.VMEM((nc, 256, 128), F32)],
        name="gla_fwd", compiler_params=_cp("arbitrary"))(pm, pm, pm, pe, wau_p, b_alpha)


def _gla_bwd(pm, pe, wau_p, wau_pt, b_alpha, do, states):
    s = pm.shape[0]
    t = min(s, 1024)
    nc = t // CHUNK
    nb = s // t

    def body(q_ref, k_ref, v_ref, e_ref, wau_ref, waut_ref, ba_ref, do_ref, st_ref,
             dq_ref, dk_ref, dv_ref, de_ref, dwau_ref, dba_ref, gstate, b_scr, db_scr, dec_scr):
        @pl.when(pl.program_id(0) == 0)
        def _():
            gstate[...] = jnp.zeros_like(gstate)
            dwau_ref[...] = jnp.zeros_like(dwau_ref)
            dba_ref[...] = jnp.zeros_like(dba_ref)

        eb = e_ref[...].astype(BF16)
        z = _dot(eb, wau_ref[...]) + ba_ref[...]
        _gla_decays(_log_sigmoid(z) * (1.0 / GLA_TAU), b_scr, dec_scr)
        lmask, hmask, bd = _gla_consts()
        last_row = _iota((CHUNK, 256), 0) == CHUNK - 1

        def chunk(cc, carry):
            c = nc - 1 - cc
            rows = pl.ds(pl.multiple_of(c * CHUNK, CHUNK), CHUNK)
            qc = q_ref[rows, :].astype(F32) * 0.125
            kc = k_ref[rows, :].astype(F32)
            vc = v_ref[rows, :]
            dob = do_ref[rows, :]
            bl, ep, en, ek, qp, qn, kn, kp, kk = _gla_chunk(b_scr[rows, :], qc, kc)
            decb = jnp.concatenate([dec_scr[c]] * 4, axis=1)
            qs = jnp.where(hmask, jnp.concatenate([qp] * 4, axis=0), 0.0).astype(BF16)
            qns = jnp.where(hmask, jnp.concatenate([qn] * 4, axis=0), 0.0).astype(BF16)
            knb, kpb = kn.astype(BF16), kp.astype(BF16)
            attn = jnp.where(lmask, _dot_nt(qs, knb), _dot_nt(qns, kpb)).astype(BF16)
            st = jnp.where(bd, jnp.concatenate([st_ref[c]] * 4, axis=0), 0.0)
            g = gstate[...]
            gb = g.astype(BF16)
            do_s = jnp.where(bd, jnp.concatenate([dob] * 4, axis=0), jnp.zeros((), BF16))
            dattn = _dot_nt(do_s, vc)
            dv_ref[rows, :] = (_dot_tn(attn, do_s) + _dot(kk.astype(BF16), gb)).astype(BF16)
            dac = jnp.where(lmask, dattn, 0.0).astype(BF16)
            daa = jnp.where(lmask, 0.0, dattn).astype(BF16)
            dqp = _fold_heads(jnp.where(hmask, _dot(dac, knb), 0.0)) + _dot_nt(dob, st.astype(BF16))
            dqn = _fold_heads(jnp.where(hmask, _dot(daa, kpb), 0.0))
            dkn = _dot_tn(dac, qs)
            dkp = _dot_tn(daa, qns)
            dkk = _dot_nt(vc, gb)
            ddec = _dot_nt(jnp.ones((8, 1536), BF16), _split3(g * st))[0:1, :]
            gstate[...] = decb * g + jnp.where(bd, _dot_tn(qp.astype(BF16), dob), 0.0)
            dq_ref[rows, :] = ((dqp * ep + dqn * en) * 0.125).astype(BF16)
            dk_ref[rows, :] = (dkn * en + dkp * ep + dkk * ek).astype(BF16)
            dek = dkk * kc * ek
            db = (dqp * qc + dkp * kc) * ep - (dqn * qc + dkn * kc) * en - dek
            dbl = jnp.sum(dek, axis=0, keepdims=True) + ddec * jnp.exp(bl)
            db_scr[rows, :] = db + jnp.where(last_row, dbl, 0.0)
            return carry

        lax.fori_loop(0, nc, chunk, 0)
        triu = (_iota((CHUNK, CHUNK), 0) <= _iota((CHUNK, CHUNK), 1)).astype(BF16)
        dla = jnp.concatenate([_sum3(_dot(triu, _split3(db_scr[CHUNK * c:CHUNK * (c + 1), :])), 1) for c in range(nc)], axis=0)
        dz = dla * (1.0 / GLA_TAU) * _sigmoid(-z)
        dzb = dz.astype(BF16)
        dwau_ref[...] += _dot_tn(eb, dzb)
        dba_ref[...] += _fold8(dz)
        de_ref[...] = _dot(dzb, waut_ref[...])

    rev = lambda i: nb - 1 - i
    return pl.pallas_call(
        body, grid=(nb,),
        in_specs=[pl.BlockSpec((t, 256), lambda i: (rev(i), C_GQ // 256)), pl.BlockSpec((t, 256), lambda i: (rev(i), C_GK // 256)),
                  pl.BlockSpec((t, 512), lambda i: (rev(i), C_GV // 512)), pl.BlockSpec((t, PE_W), lambda i: (rev(i), 0)),
                  pl.BlockSpec((PE_W, 256), lambda i: (0, 0)), pl.BlockSpec((256, PE_W), lambda i: (0, 0)),
                  pl.BlockSpec((1, 256), lambda i: (0, 0)), pl.BlockSpec((t, 512), lambda i: (rev(i), 0)),
                  pl.BlockSpec((nc, CHUNK, 512), lambda i: (rev(i), 0, 0))],
        out_specs=[pl.BlockSpec((t, 256), lambda i: (rev(i), 0)), pl.BlockSpec((t, 256), lambda i: (rev(i), 0)),
                   pl.BlockSpec((t, 512), lambda i: (rev(i), 0)), pl.BlockSpec((t, PE_W), lambda i: (rev(i), 0)),
                   pl.BlockSpec((PE_W, 256), lambda i: (0, 0)), pl.BlockSpec((8, 256), lambda i: (0, 0))],
        out_shape=[jax.ShapeDtypeStruct((s, 256), BF16), jax.ShapeDtypeStruct((s, 256), BF16),
                   jax.ShapeDtypeStruct((s, 512), BF16), jax.ShapeDtypeStruct((s, PE_W), F32),
                   jax.ShapeDtypeStruct((PE_W, 256), F32), jax.ShapeDtypeStruct((8, 256), F32)],
        scratch_shapes=[pltpu.VMEM((256, 512), F32), pltpu.VMEM((t, 256), F32), pltpu.VMEM((t, 256), F32),
                        pltpu.VMEM((nc, 256, 128), F32)],
        name="gla_bwd", compiler_params=_cp("arbitrary"))(pm, pm, pm, pe, wau_p, wau_pt, b_alpha, do, states)


def _fcum_fwd(pe, bias):
    s = pe.shape[0]
    t = min(s, 512)

    def body(e_ref, b_ref, f_ref, carry):
        @pl.when(pl.program_id(0) == 0)
        def _():
            carry[...] = jnp.zeros_like(carry)

        lf = _log_sigmoid(e_ref[...] + b_ref[...])
        tri = (_iota((t, t), 0) >= _iota((t, t), 1)).astype(BF16)
        f = _sum3(_dot(tri, _split3(lf)), 1) + carry[0:1, :]
        f_ref[...] = f
        carry[...] = jnp.broadcast_to(f[t - 1:t, :], carry.shape)

    return pl.pallas_call(
        body, grid=(s // t,),
        in_specs=[pl.BlockSpec((t, PE_W), lambda i: (i, 0)), pl.BlockSpec((1, PE_W), lambda i: (0, 0))],
        out_specs=pl.BlockSpec((t, PE_W), lambda i: (i, 0)),
        out_shape=jax.ShapeDtypeStruct((s, PE_W), F32), scratch_shapes=[pltpu.VMEM((8, PE_W), F32)],
        name="fcum_fwd", compiler_params=_cp("arbitrary"))(pe, bias)


def _fcum_bwd(pe, bias, df):
    s = pe.shape[0]
    t = min(s, 512)
    nb = s // t

    def body(e_ref, b_ref, df_ref, de_ref, db_ref, carry):
        @pl.when(pl.program_id(0) == 0)
        def _():
            carry[...] = jnp.zeros_like(carry)
            db_ref[...] = jnp.zeros_like(db_ref)

        triu = (_iota((t, t), 0) <= _iota((t, t), 1)).astype(BF16)
        dlf = _sum3(_dot(triu, _split3(df_ref[...])), 1) + carry[0:1, :]
        carry[...] = jnp.broadcast_to(dlf[0:1, :], carry.shape)
        lane = _iota((t, PE_W), 1)
        dff = jnp.where((lane >= FF_LANE) & (lane < FF_LANE + 8), dlf * _sigmoid(-(e_ref[...] + b_ref[...])), 0.0)
        de_ref[...] = dff
        db_ref[...] += _fold8(dff)

    rev = lambda i: (nb - 1 - i, 0)
    return pl.pallas_call(
        body, grid=(nb,),
        in_specs=[pl.BlockSpec((t, PE_W), rev), pl.BlockSpec((1, PE_W), lambda i: (0, 0)), pl.BlockSpec((t, PE_W), rev)],
        out_specs=[pl.BlockSpec((t, PE_W), rev), pl.BlockSpec((8, PE_W), lambda i: (0, 0))],
        out_shape=[jax.ShapeDtypeStruct((s, PE_W), F32), jax.ShapeDtypeStruct((8, PE_W), F32)],
        scratch_shapes=[pltpu.VMEM((8, PE_W), F32)],
        name="fcum_bwd", compiler_params=_cp("arbitrary"))(pe, bias, df)


FOX_WIDE = 1024


def _split3(x):
    hi = x.astype(BF16)
    r = x - hi.astype(F32)
    mid = r.astype(BF16)
    lo = (r - mid.astype(F32)).astype(BF16)
    return jnp.concatenate([hi, mid, lo], axis=1)


def _pieces(x, lane0):
    lane = _iota(x.shape, 1)
    x = jnp.where((lane >= lane0) & (lane < lane0 + 8), x, 0.0)
    hi = x.astype(BF16).astype(F32)
    r = x - hi
    mid = r.astype(BF16).astype(F32)
    lo = (r - mid).astype(BF16).astype(F32)
    return (hi + pltpu.roll(mid, 8, 1) + pltpu.roll(lo, 16, 1)).astype(BF16)


def _sum3(x, axis):
    n = x.shape[axis] // 3
    parts = [lax.slice_in_dim(x, n * p, n * (p + 1), axis=axis) for p in range(3)]
    return (parts[0] + parts[1]) + parts[2]


def _spread(x, sp):
    return jnp.concatenate([_dot(x[:, 128 * g:128 * (g + 1)], sp) for g in range(4)], axis=1)


def _fox_tables():
    heads, lane = np.arange(8), np.arange(64)
    spread = np.zeros((128, 256), np.float32)
    spread[(64 * heads[:2, None] + lane).ravel(), (128 * heads[:2, None] + lane).ravel()] = 1.0
    def place(src_lane0, dst_off, val):
        t = np.zeros((128, 1024), np.float32)
        for p in range(3):
            for src in {src_lane0 + 8 * p, (src_lane0 - 8 * p) % 128}:
                t[src + heads, 128 * heads + dst_off + p] = val
        return t
    def const(off, val):
        c = np.zeros((1, 1024), np.float32)
        for p in range(3):
            c[0, 128 * heads + off + p] = val
        return c
    rows = np.zeros((8, 128), np.float32)
    rows[heads, FF_LANE + heads] = 1.0
    bf = lambda a: jnp.asarray(a, BF16)
    return dict(spread=bf(spread),
                f_to_q=bf(place(FF_LANE, 64, 1.0)), f_to_k=bf(place(FF_LANE, 67, -1.0)), d_to_do=bf(place(0, 64, 1.0)),
                ones_q=jnp.asarray(const(67, 1.0)), ones_k=jnp.asarray(const(64, 1.0)), ones_v=jnp.asarray(const(64, -1.0)),
                rows=jnp.asarray(rows))


LOG2E = 1.4426950408889634


def _fox_prep(pm, f128, lse8, tb, *, backward):
    s = pm.shape[0]
    tm = min(s, 1024) if backward else _row_tile(s)

    def body(*refs):
        if backward:
            q_ref, f_ref, lse_ref, sp_ref, fq_ref, cq_ref, rows_ref, qa_ref = refs
            f = f_ref[...] * LOG2E - _dot_tn(lse_ref[...], rows_ref[...], precision=HI)
            q2 = (q_ref[...].astype(F32) * (0.125 * LOG2E)).astype(BF16)
            qa_ref[...] = (_spread(q2, sp_ref[...]) + _dot(_pieces(f, FF_LANE), fq_ref[...]) + cq_ref[...]).astype(BF16)
            return
        (q_ref, k_ref, v_ref, f_ref, sp_ref, fq_ref, fk_ref, cq_ref, ck_ref, cv_ref,
         qa_ref, ka_ref, va_ref, vt_ref, qt_ref, kt_ref) = refs
        f3 = _pieces(f_ref[...] * LOG2E, FF_LANE)
        q, k, v = q_ref[...].astype(F32), k_ref[...], v_ref[...]
        sp = sp_ref[...]
        qa_ref[...] = (_spread((q * (0.125 * LOG2E)).astype(BF16), sp) + _dot(f3, fq_ref[...]) + cq_ref[...]).astype(BF16)
        ka_ref[...] = (_spread(k, sp) + _dot(f3, fk_ref[...]) + ck_ref[...]).astype(BF16)
        va_ref[...] = (_spread(v, sp) + cv_ref[...]).astype(BF16)
        vt_ref[...] = v.T
        qt_ref[...] = (q * 0.125).astype(BF16).T
        kt_ref[...] = (k.astype(F32) * 0.125).astype(BF16).T

    row = lambda i: (i, 0)
    const = lambda i: (0, 0)
    blk = lambda c: pl.BlockSpec((tm, 512), lambda i: (i, c // 512))
    wide = pl.BlockSpec((tm, 1024), row)
    mat = lambda a: pl.BlockSpec(a.shape, const)
    if backward:
        ins = [pm, f128, lse8, tb["spread"], tb["f_to_q"], tb["ones_q"], tb["rows"]]
        in_specs = [blk(C_FQ), pl.BlockSpec((tm, 128), row), pl.BlockSpec((8, tm), lambda i: (0, i))] + [mat(a) for a in ins[3:]]
        out_specs, out_shape = wide, jax.ShapeDtypeStruct((s, 1024), BF16)
    else:
        ins = [pm, pm, pm, f128, tb["spread"], tb["f_to_q"], tb["f_to_k"], tb["ones_q"], tb["ones_k"], tb["ones_v"]]
        in_specs = [blk(C_FQ), blk(C_FK), blk(C_FV), pl.BlockSpec((tm, 128), row)] + [mat(a) for a in ins[4:]]
        tr = pl.BlockSpec((512, tm), lambda i: (0, i))
        out_specs = [wide, wide, wide, tr, tr, tr]
        out_shape = [jax.ShapeDtypeStruct((s, 1024), BF16)] * 3 + [jax.ShapeDtypeStruct((512, s), BF16)] * 3
    return pl.pallas_call(body, grid=(s // tm,), in_specs=in_specs, out_specs=out_specs, out_shape=out_shape,
                          name="fox_prep_bwd" if backward else "fox_prep", compiler_params=_cp("parallel"))(*ins)


def _fox_post(dqt, dkt, dvt, rowsum8, colsum8, tb):
    s = dqt.shape[1]
    tm = min(s, 1024)

    def body(dqt_ref, dkt_ref, dvt_ref, rs_ref, cs_ref, rows_ref, dfq_ref, dfk_ref, dfv_ref, df_ref):
        dfq_ref[...] = dqt_ref[...].T.astype(BF16)
        dfk_ref[...] = dkt_ref[...].T
        dfv_ref[...] = dvt_ref[...].T
        df_ref[...] = _dot_tn(rs_ref[...] - cs_ref[...], rows_ref[...], precision=HI)

    row = lambda i: (i, 0)
    tr = pl.BlockSpec((512, tm), lambda i: (0, i))
    out = pl.BlockSpec((tm, 512), row)
    heads = pl.BlockSpec((8, tm), lambda i: (0, i))
    return pl.pallas_call(
        body, grid=(s // tm,),
        in_specs=[tr, tr, tr, heads, heads, pl.BlockSpec((8, 128), lambda i: (0, 0))],
        out_specs=[out, out, out, pl.BlockSpec((tm, 128), row)],
        out_shape=[jax.ShapeDtypeStruct((s, 512), BF16)] * 3 + [jax.ShapeDtypeStruct((s, 128), F32)],
        name="fox_post", compiler_params=_cp("parallel"))(dqt, dkt, dvt, rowsum8, colsum8, tb["rows"])


def _fox_fwd(k_aug, q_aug, vt):
    s = k_aug.shape[0]
    nh = 8
    tk = _row_tile(s)
    tq = min(s, 2 * FOX_WIDE)
    per = tq // tk

    def body(k_ref, q_ref, v_ref, o_ref, lse_ref, sbuf):
        i = pl.program_id(1)
        qa = q_ref[...]

        def scores(j):
            return _dot_nt(k_ref[pl.ds(pl.multiple_of(j * tk, tk), tk), :], qa)

        ones_row = (_iota((16, tk), 0) == 0).astype(BF16)

        def update(st, j, carry):
            m, acc = carry
            m2 = jnp.maximum(m, jnp.max(st, axis=0, keepdims=True))
            p = jnp.exp2(st - m2)
            vj = jnp.concatenate([v_ref[:, pl.ds(pl.multiple_of(j * tk, tk), tk)], ones_row], axis=0)
            return m2, jnp.exp2(m - m2) * acc + _dot(vj, p.astype(BF16))

        def step(a, carry):
            sbuf[1] = scores(2 * a + 1)
            carry = update(sbuf[0], 2 * a, carry)
            sbuf[0] = scores(2 * a + 2)
            return update(sbuf[1], 2 * a + 1, carry)

        n = i * per
        sbuf[0] = scores(0)
        carry = (jnp.full((1, tq), -1e30, F32), jnp.zeros((80, tq), F32))
        carry = lax.fori_loop(0, n // 2, step, carry)
        tri = _iota((tk, tk), 0) <= _iota((tk, tk), 1)
        late = [_dot_nt(k_ref[pl.ds(pl.multiple_of((n + r) * tk, tk), tk), :], qa[r * tk:, :]) for r in range(1, per)]
        for r in range(per):
            st = sbuf[0] if r == 0 else late[r - 1]
            head = jnp.where(tri, st[:, :tk], -1e30)
            st = head if st.shape[1] == tk else jnp.concatenate([head, st[:, tk:]], axis=1)
            part = update(st, n + r, tuple(c[:, r * tk:] for c in carry))
            carry = part if r == 0 else tuple(jnp.concatenate([old[:, :r * tk], new], axis=1) for old, new in zip(carry, part))
        m, acc = carry
        l = acc[64:65]
        o_ref[...] = (acc[0:64] / l).astype(BF16)
        lse_ref[0] = m + jnp.log2(l)

    return pl.pallas_call(
        body, grid=(nh, s // tq),
        in_specs=[pl.BlockSpec((s, 128), lambda h, i: (0, h)), pl.BlockSpec((tq, 128), lambda h, i: (i, h)),
                  pl.BlockSpec((64, s), lambda h, i: (h, 0))],
        out_specs=[pl.BlockSpec((64, tq), lambda h, i: (h, i)), pl.BlockSpec((1, 1, tq), lambda h, i: (h, 0, i))],
        out_shape=[jax.ShapeDtypeStruct((512, s), BF16), jax.ShapeDtypeStruct((nh, 1, s), F32)],
        scratch_shapes=[pltpu.VMEM((2, tk, tq), F32)],
        name="fox_fwd", compiler_params=_cp("parallel", "arbitrary"))(k_aug, q_aug, vt)


def _fox_bwd(q_aug, do_aug, qt, dot_, k_aug, v_aug, kt):
    s = q_aug.shape[0]
    nh = 8
    tq = _row_tile(s)
    tk = min(s, 2 * FOX_WIDE)
    per = tk // tq
    nqb = s // tq

    def body(qa_ref, da_ref, qt_ref, dt_ref, ka_ref, va_ref, kt_ref, dq_ref, rs_ref, dk_ref, dv_ref, dfk_ref):
        j = pl.program_id(1)

        @pl.when(j == 0)
        def _():
            dq_ref[...] = jnp.zeros_like(dq_ref)
            rs_ref[...] = jnp.zeros_like(rs_ref)

        ones_row = (_iota((16, tk), 0) == 0).astype(BF16)
        ka, va = ka_ref[...], va_ref[...]
        ks = jnp.concatenate([kt_ref[...], ones_row], axis=0)
        tri = _iota((tq, tq), 0) >= _iota((tq, tq), 1)

        def tile(i, w, carry):
            masked = w is not None
            w = tk if w is None else w
            rows = pl.ds(pl.multiple_of(i * tq, tq), tq)
            sp = _dot_nt(qa_ref[rows, :], ka[:w])
            if masked:
                last = jnp.where(tri, sp[:, w - tq:], -1e30)
                sp = last if w == tq else jnp.concatenate([sp[:, :w - tq], last], axis=1)
            p = jnp.exp2(sp)
            dsb = (p * _dot_nt(da_ref[rows, :], va[:w])).astype(BF16)
            dq = _dot_nt(ks[:, :w], dsb)
            dq_ref[:, rows] += dq[0:64]
            rs_ref[0, :, rows] += dq[64:72]
            new = (_dot(jnp.concatenate([qt_ref[:, rows], ones_row[:, :tq]], axis=0), dsb), _dot(dt_ref[:, rows], p.astype(BF16)))
            if w == tk:
                return tuple(c + d for c, d in zip(carry, new))
            return tuple(jnp.concatenate([c[:, :w] + d, c[:, w:]], axis=1) for c, d in zip(carry, new))

        carry = (jnp.zeros((80, tk), F32), jnp.zeros((64, tk), F32))
        for r in range(per):
            carry = tile(j * per + r, (r + 1) * tq, carry)
        dk, dv = lax.fori_loop((j + 1) * per, nqb, lambda i, c: tile(i, None, c), carry)
        dk_ref[...] = dk[0:64].astype(BF16)
        dv_ref[...] = dv.astype(BF16)
        dfk_ref[0] = dk[64:65]

    head_cols = lambda h, j: (0, h)
    head_rows = lambda h, j: (h, 0)
    once = dict(pipeline_mode=pl.Buffered(1))
    return pl.pallas_call(
        body, grid=(nh, s // tk),
        in_specs=[pl.BlockSpec((s, 128), head_cols, **once), pl.BlockSpec((s, 128), head_cols, **once),
                  pl.BlockSpec((64, s), head_rows, **once), pl.BlockSpec((64, s), head_rows, **once),
                  pl.BlockSpec((tk, 128), lambda h, j: (j, h)), pl.BlockSpec((tk, 128), lambda h, j: (j, h)),
                  pl.BlockSpec((64, tk), lambda h, j: (h, j))],
        out_specs=[pl.BlockSpec((64, s), head_rows), pl.BlockSpec((1, 8, s), lambda h, j: (h, 0, 0)),
                   pl.BlockSpec((64, tk), lambda h, j: (h, j)),
                   pl.BlockSpec((64, tk), lambda h, j: (h, j)), pl.BlockSpec((1, 1, tk), lambda h, j: (h, 0, j))],
        out_shape=[jax.ShapeDtypeStruct((512, s), F32), jax.ShapeDtypeStruct((nh, 8, s), F32),
                   jax.ShapeDtypeStruct((512, s), BF16),
                   jax.ShapeDtypeStruct((512, s), BF16), jax.ShapeDtypeStruct((nh, 1, s), F32)],
        name="fox_bwd", compiler_params=_cp("parallel", "arbitrary"))(q_aug, do_aug, qt, dot_, k_aug, v_aug, kt)


MEM_SCALE = 128 ** -0.5


def _mem_attn_fwd(pm, mkv):
    s = pm.shape[0]
    t = min(s, 1024)
    nm = mkv.shape[0]

    def body(q_ref, mk_ref, mv_ref, o_ref):
        for h in range(4):
            cols = slice(128 * h, 128 * (h + 1))
            sc = _dot_nt(q_ref[:, cols], mk_ref[:, cols]) * MEM_SCALE
            p = jnp.exp(sc - jnp.max(sc, axis=-1, keepdims=True))
            p = p / jnp.sum(p, axis=-1, keepdims=True)
            o_ref[:, cols] = _dot(p.astype(BF16), mv_ref[:, cols]).astype(BF16)

    return pl.pallas_call(
        body, grid=(s // t,),
        in_specs=[pl.BlockSpec((t, 512), lambda i: (i, C_MQ // 512)), pl.BlockSpec((nm, 512), lambda i: (0, 0)),
                  pl.BlockSpec((nm, 512), lambda i: (0, 1))],
        out_specs=pl.BlockSpec((t, 512), lambda i: (i, 0)),
        out_shape=jax.ShapeDtypeStruct((s, 512), BF16),
        name="mem_attn_fwd", compiler_params=_cp("parallel"))(pm, mkv, mkv)


def _mem_attn_bwd(pm, mkv, do):
    s = pm.shape[0]
    t = min(s, 1024)
    nm = mkv.shape[0]

    def body(q_ref, mk_ref, mv_ref, do_ref, dq_ref, dmk_ref, dmv_ref):
        @pl.when(pl.program_id(0) == 0)
        def _():
            dmk_ref[...] = jnp.zeros_like(dmk_ref)
            dmv_ref[...] = jnp.zeros_like(dmv_ref)

        for h in range(4):
            cols = slice(128 * h, 128 * (h + 1))
            qh, kh, vh, doh = q_ref[:, cols], mk_ref[:, cols], mv_ref[:, cols], do_ref[:, cols]
            sc = _dot_nt(qh, kh) * MEM_SCALE
            p = jnp.exp(sc - jnp.max(sc, axis=-1, keepdims=True))
            p = p / jnp.sum(p, axis=-1, keepdims=True)
            pb = p.astype(BF16)
            dp = _dot_nt(doh, vh)
            ds = (p * (dp - jnp.sum(p * dp, axis=-1, keepdims=True)) * MEM_SCALE).astype(BF16)
            dq_ref[:, cols] = _dot(ds, kh).astype(BF16)
            dmk_ref[:, cols] += _dot_tn(ds, qh)
            dmv_ref[:, cols] += _dot_tn(pb, doh)

    return pl.pallas_call(
        body, grid=(s // t,),
        in_specs=[pl.BlockSpec((t, 512), lambda i: (i, C_MQ // 512)), pl.BlockSpec((nm, 512), lambda i: (0, 0)),
                  pl.BlockSpec((nm, 512), lambda i: (0, 1)), pl.BlockSpec((t, 512), lambda i: (i, 0))],
        out_specs=[pl.BlockSpec((t, 512), lambda i: (i, 0)), pl.BlockSpec((nm, 512), lambda i: (0, 0)),
                   pl.BlockSpec((nm, 512), lambda i: (0, 0))],
        out_shape=[jax.ShapeDtypeStruct((s, 512), BF16), jax.ShapeDtypeStruct((nm, 512), F32),
                   jax.ShapeDtypeStruct((nm, 512), F32)],
        name="mem_attn_bwd", compiler_params=_cp("arbitrary"))(pm, mkv, mkv, do)


def _gain_grad(dxn_g, x, r, name):
    m, d = x.shape

    def body(d_ref, x_ref, r_ref, o_ref):
        o_ref[...] = _fold8(d_ref[...] * (x_ref[...] * r_ref[...]))

    return pl.pallas_call(body, out_shape=jax.ShapeDtypeStruct((8, d), F32), name=name,
                          compiler_params=pltpu.CompilerParams(vmem_limit_bytes=VMEM_LIMIT_BYTES))(dxn_g, x, r)


def _head_norm(o, gh):
    xs, rs = [], []
    for h in range(4):
        oh = o[:, 128 * h:128 * (h + 1)]
        r = lax.rsqrt(jnp.mean(oh * oh, axis=-1, keepdims=True) + EPS)
        xs.append(oh * r)
        rs.append(r)
    return xs, rs


def _merge_fwd(x, pm, o_gla, o_fox_t, o_mem, g_head, wg, wf, wm, wo, g_ffn):
    s = x.shape[0]
    t = min(s, 512)

    def body(x_ref, g0_ref, g1_ref, g2_ref, gg_ref, og_ref, of_ref, om_ref, gh_ref, wg_ref, wf_ref, wm_ref, wo_ref, gf_ref,
             mg_ref, h1_ref, u2_ref, r2_ref):
        xs, _ = _head_norm(og_ref[...], None)
        gg = gg_ref[...].astype(F32)
        sil = gg * _sigmoid(gg)
        ogn = jnp.concatenate(xs, axis=1) * gh_ref[...] * sil
        merged = (_sigmoid(g0_ref[...].astype(F32)) * _dot(ogn.astype(BF16), wg_ref[...])
                  + _sigmoid(g1_ref[...].astype(F32)) * _dot(of_ref[...].T, wf_ref[...])
                  + _sigmoid(g2_ref[...].astype(F32)) * _dot(om_ref[...], wm_ref[...]))
        mb = merged.astype(BF16)
        mg_ref[...] = mb
        h1 = x_ref[...] + _dot(mb, wo_ref[...])
        h1_ref[...] = h1
        r = lax.rsqrt(jnp.mean(h1 * h1, axis=-1, keepdims=True) + EPS)
        u2_ref[...] = ((h1 * r) * gf_ref[...]).astype(BF16)
        r2_ref[...] = r

    row = lambda i: (i, 0)
    const = lambda i: (0, 0)
    return pl.pallas_call(
        body, grid=(s // t,),
        in_specs=[pl.BlockSpec((t, D), row), pl.BlockSpec((t, D), lambda i: (i, 0)), pl.BlockSpec((t, D), lambda i: (i, 1)),
                  pl.BlockSpec((t, D), lambda i: (i, 2)), pl.BlockSpec((t, 512), lambda i: (i, C_GG // 512)),
                  pl.BlockSpec((t, 512), row), pl.BlockSpec((512, t), lambda i: (0, i)), pl.BlockSpec((t, 512), row),
                  pl.BlockSpec((1, 512), const), pl.BlockSpec((512, D), const), pl.BlockSpec((512, D), const),
                  pl.BlockSpec((512, D), const), pl.BlockSpec((D, D), const), pl.BlockSpec((1, D), const)],
        out_specs=[pl.BlockSpec((t, D), row), pl.BlockSpec((t, D), row), pl.BlockSpec((t, D), row), pl.BlockSpec((t, 1), row)],
        out_shape=[jax.ShapeDtypeStruct((s, D), BF16), jax.ShapeDtypeStruct((s, D), F32),
                   jax.ShapeDtypeStruct((s, D), BF16), jax.ShapeDtypeStruct((s, 1), F32)],
        name="merge_fwd", compiler_params=_cp("parallel"))(x, pm, pm, pm, pm, o_gla, o_fox_t, o_mem, g_head, wg, wf, wm, wo, g_ffn)


def _merge_bwd(dh1b, pm, o_gla, o_fox_t, o_mem, g_head, wg, wf, wm, wgt, wft, wmt, wot, spread, d_to_do):
    s = dh1b.shape[0]
    t = min(s, 256)

    def body(dh_ref, g0_ref, g1_ref, g2_ref, gg_ref, og_ref, of_ref, om_ref, gh_ref, wg_ref, wf_ref, wm_ref,
             wgt_ref, wft_ref, wmt_ref, wot_ref, sp_ref, dd_ref,
             dgt_ref, dgg_ref, dog_ref, da_ref, dot_ref, dom_ref, dwg_ref, dwf_ref, dwm_ref, dgh_ref):
        @pl.when(pl.program_id(0) == 0)
        def _():
            dwg_ref[...] = jnp.zeros_like(dwg_ref)
            dwf_ref[...] = jnp.zeros_like(dwf_ref)
            dwm_ref[...] = jnp.zeros_like(dwm_ref)
            dgh_ref[...] = jnp.zeros_like(dgh_ref)

        dmerged = _dot(dh_ref[...], wot_ref[...])
        og = og_ref[...]
        xs, rs = _head_norm(og, None)
        on = jnp.concatenate(xs, axis=1)
        gg = gg_ref[...].astype(F32)
        sg = _sigmoid(gg)
        sil = gg * sg
        gh = gh_ref[...]
        ognb = (on * gh * sil).astype(BF16)
        ofb, omb = of_ref[...].T, om_ref[...]
        douts = []
        for idx, (gref, ob, w_ref, wt_ref, dw_ref) in enumerate((
                (g0_ref, ognb, wg_ref, wgt_ref, dwg_ref), (g1_ref, ofb, wf_ref, wft_ref, dwf_ref),
                (g2_ref, omb, wm_ref, wmt_ref, dwm_ref))):
            gt = _sigmoid(gref[...].astype(F32))
            y = _dot(ob, w_ref[...])
            dgt_ref[:, D * idx:D * (idx + 1)] = (dmerged * y * gt * (1.0 - gt)).astype(BF16)
            dy = (gt * dmerged).astype(BF16)
            dw_ref[...] += _dot_tn(ob, dy)
            douts.append(_dot(dy, wt_ref[...]))
        dogn, dof, dom = douts
        dofb = dof.astype(BF16)
        dom_ref[...] = dom.astype(BF16)
        ind = (_iota((1536, 128), 0) % 512 // 64 == _iota((1536, 128), 1)).astype(BF16)
        delta = _dot(_split3(dofb.astype(F32) * ofb.astype(F32)), ind)
        da_ref[...] = (_spread(dofb, sp_ref[...]) + _dot(_pieces(delta, 0), dd_ref[...])).astype(BF16)
        dot_ref[...] = dofb.T
        dgg_ref[...] = (dogn * on * gh * (sg * (1.0 + gg * (1.0 - sg)))).astype(BF16)
        d_on = dogn * sil
        dgh_ref[...] += _fold8(d_on * on)
        dxn = d_on * gh
        outs = []
        for h in range(4):
            cols = slice(128 * h, 128 * (h + 1))
            dh_, xh = dxn[:, cols], xs[h]
            outs.append(rs[h] * (dh_ - xh * jnp.mean(dh_ * xh, axis=-1, keepdims=True)))
        dog_ref[...] = jnp.concatenate(outs, axis=1).astype(BF16)

    row = lambda i: (i, 0)
    const = lambda i: (0, 0)
    return pl.pallas_call(
        body, grid=(s // t,),
        in_specs=[pl.BlockSpec((t, D), row), pl.BlockSpec((t, D), lambda i: (i, 0)), pl.BlockSpec((t, D), lambda i: (i, 1)),
                  pl.BlockSpec((t, D), lambda i: (i, 2)), pl.BlockSpec((t, 512), lambda i: (i, C_GG // 512)),
                  pl.BlockSpec((t, 512), row), pl.BlockSpec((512, t), lambda i: (0, i)), pl.BlockSpec((t, 512), row),
                  pl.BlockSpec((1, 512), const), pl.BlockSpec((512, D), const), pl.BlockSpec((512, D), const),
                  pl.BlockSpec((512, D), const), pl.BlockSpec((D, 512), const), pl.BlockSpec((D, 512), const),
                  pl.BlockSpec((D, 512), const), pl.BlockSpec((D, D), const),
                  pl.BlockSpec((128, 256), const), pl.BlockSpec((128, 1024), const)],
        out_specs=[pl.BlockSpec((t, 3 * D), row), pl.BlockSpec((t, 512), row), pl.BlockSpec((t, 512), row),
                   pl.BlockSpec((t, 1024), row), pl.BlockSpec((512, t), lambda i: (0, i)), pl.BlockSpec((t, 512), row),
                   pl.BlockSpec((512, D), const), pl.BlockSpec((512, D), const), pl.BlockSpec((512, D), const),
                   pl.BlockSpec((8, 512), const)],
        out_shape=[jax.ShapeDtypeStruct((s, 3 * D), BF16), jax.ShapeDtypeStruct((s, 512), BF16),
                   jax.ShapeDtypeStruct((s, 512), BF16), jax.ShapeDtypeStruct((s, 1024), BF16),
                   jax.ShapeDtypeStruct((512, s), BF16), jax.ShapeDtypeStruct((s, 512), BF16),
                   jax.ShapeDtypeStruct((512, D), F32), jax.ShapeDtypeStruct((512, D), F32),
                   jax.ShapeDtypeStruct((512, D), F32), jax.ShapeDtypeStruct((8, 512), F32)],
        name="merge_bwd", compiler_params=_cp("arbitrary"))(
            dh1b, pm, pm, pm, pm, o_gla, o_fox_t, o_mem, g_head, wg, wf, wm, wgt, wft, wmt, wot, spread, d_to_do)


def _ff2_loss(a, w2, h1, g_final, target):
    s, k = a.shape
    tm = min(s, 512)

    def body(a_ref, w_ref, h1_ref, g_ref, t_ref, dh_ref, dhb_ref, loss_ref, dg_ref):
        @pl.when(pl.program_id(0) == 0)
        def _():
            loss_ref[...] = jnp.zeros_like(loss_ref)
            dg_ref[...] = jnp.zeros_like(dg_ref)

        h2 = h1_ref[...] + _dot(_relu2_bf16(a_ref[...]), w_ref[...])
        r = lax.rsqrt(jnp.mean(h2 * h2, axis=-1, keepdims=True) + EPS)
        xn = h2 * r
        g = g_ref[...]
        err = xn * g - t_ref[...]
        e2 = _fold8(err * err)
        part = e2[:, 0:128]
        for c in range(1, D // 128):
            part = part + e2[:, 128 * c:128 * (c + 1)]
        loss_ref[...] += part
        dy = err * (1.0 / D)
        dg_ref[...] += _fold8(dy * xn)
        dxn = dy * g
        dh = r * (dxn - xn * jnp.mean(dxn * xn, axis=-1, keepdims=True))
        dh_ref[...] = dh
        dhb_ref[...] = dh.astype(BF16)

    row = lambda i: (i, 0)
    const = lambda i: (0, 0)
    return pl.pallas_call(
        body, grid=(s // tm,),
        in_specs=[pl.BlockSpec((tm, k), row), pl.BlockSpec((k, D), const, pipeline_mode=pl.Buffered(1)),
                  pl.BlockSpec((tm, D), row), pl.BlockSpec((1, D), const), pl.BlockSpec((tm, D), row)],
        out_specs=[pl.BlockSpec((tm, D), row), pl.BlockSpec((tm, D), row), pl.BlockSpec((8, 128), const),
                   pl.BlockSpec((8, D), const)],
        out_shape=[jax.ShapeDtypeStruct((s, D), F32), jax.ShapeDtypeStruct((s, D), BF16),
                   jax.ShapeDtypeStruct((8, 128), F32), jax.ShapeDtypeStruct((8, D), F32)],
        name="ff2_loss", compiler_params=_cp("arbitrary"))(a, w2, h1, g_final, target)


def _adam(w, g, m, v, name):
    _, r, c = w.shape
    tr = r
    for cand in (512, 256, 128, 64, 32, 16, 8):
        if r % cand == 0 and cand * c * 4 <= (1 << 20):
            tr = cand
            break
    c1 = 1.0 - ADAM_B1 ** ADAM_STEP
    c2 = 1.0 - ADAM_B2 ** ADAM_STEP

    def body(w_ref, g_ref, m_ref, v_ref, d_ref, nm_ref, nv_ref):
        gv = g_ref[...]
        nm = ADAM_B1 * m_ref[...] + (1.0 - ADAM_B1) * gv
        nv = ADAM_B2 * v_ref[...] + (1.0 - ADAM_B2) * (gv * gv)
        d_ref[...] = -ADAM_LR * ((nm / c1) / (jnp.sqrt(nv / c2) + ADAM_EPS) + ADAM_WD * w_ref[...])
        nm_ref[...] = nm
        nv_ref[...] = nv

    spec = pl.BlockSpec((1, tr, c), lambda i: (0, i, 0))
    return pl.pallas_call(
        body, grid=(r // tr,), in_specs=[spec] * 4, out_specs=[spec] * 3,
        out_shape=[jax.ShapeDtypeStruct((1, r, c), F32)] * 3, name=name, compiler_params=_cp("parallel"))(w, g, m, v)


def _row_block(r):
    return max(d for d in range(16, 513, 16) if r % d == 0)


def _add_half(core, a, b, name):
    n, r, c = b.shape
    tr = _row_block(r)

    def body(core_ref, a_ref, b_ref, o_ref):
        o_ref[...] = (a_ref[...].astype(F32) + b_ref[...].astype(F32)).astype(BF16)

    spec = pl.BlockSpec((1, tr, c), lambda k, i, core_ref: (k, i, 0))
    half = pl.BlockSpec((1, tr, c), lambda k, i, core_ref: (k, i + core_ref[0] * (r // tr), 0))
    return pl.pallas_call(
        body, grid_spec=pltpu.PrefetchScalarGridSpec(num_scalar_prefetch=1, grid=(n, r // tr), in_specs=[half, spec],
                                                     out_specs=spec),
        out_shape=jax.ShapeDtypeStruct((n, r, c), BF16), name=name, compiler_params=_cp("parallel", "parallel"))(core, a, b)


def _sum4(a, name):
    _, r, c = a.shape
    tr = _row_block(r)

    def body(a_ref, o_ref):
        o_ref[...] = ((a_ref[0].astype(F32) + a_ref[1].astype(F32)) + a_ref[2].astype(F32)) + a_ref[3].astype(F32)

    return pl.pallas_call(body, grid=(r // tr,), in_specs=[pl.BlockSpec((4, tr, c), lambda i: (0, i, 0))],
                          out_specs=pl.BlockSpec((tr, c), lambda i: (i, 0)),
                          out_shape=jax.ShapeDtypeStruct((r, c), F32), name=name, compiler_params=_cp("parallel"))(a)


def _adam_small(w, gathered, m, v):
    c1 = 1.0 - ADAM_B1 ** ADAM_STEP
    c2 = 1.0 - ADAM_B2 ** ADAM_STEP

    def body(w_ref, g_ref, m_ref, v_ref, gs_ref, d_ref, nm_ref, nv_ref):
        gv = g_ref[0]
        for dev in range(1, N_DEV):
            gv = gv + g_ref[dev]
        gs_ref[...] = gv
        nm = ADAM_B1 * m_ref[...] + (1.0 - ADAM_B1) * gv
        nv = ADAM_B2 * v_ref[...] + (1.0 - ADAM_B2) * (gv * gv)
        d_ref[...] = -ADAM_LR * ((nm / c1) / (jnp.sqrt(nv / c2) + ADAM_EPS) + ADAM_WD * w_ref[...])
        nm_ref[...] = nm
        nv_ref[...] = nv

    return pl.pallas_call(body, out_shape=[jax.ShapeDtypeStruct((8, D), F32)] * 4, name="adam_small")(w, gathered, m, v)


def _place():
    return lax.axis_index("x"), lax.axis_index("y"), lax.axis_index("c")


def _other_chips(x, y):
    return [(1 - x, y), (x, 1 - y), (1 - x, 1 - y)]


GATHER_SEMS = [pltpu.SemaphoreType.DMA((6,)), pltpu.SemaphoreType.DMA((6,)), pltpu.SemaphoreType.DMA]


def _gather_ops(in_refs, out_refs, sems):
    (p_ref,), (out_ref,) = in_refs, out_refs
    send_sems, recv_sems, local_sem = sems
    hr = p_ref.shape[0] // 2
    x, y, cc = _place()
    sibling = (x, y, 1 - cc)
    chips = _other_chips(x, y)

    def half(chip, core):
        return out_ref.at[2 * chip[0] + chip[1], pl.ds(core * hr, hr), :]

    def copy(k, chip, core, to, src=None):
        return pltpu.make_async_remote_copy(
            src_ref=half(chip, core) if src is None else src, dst_ref=half(chip, core),
            send_sem=send_sems.at[k], recv_sem=recv_sems.at[k], device_id=to, device_id_type=MESH)

    mine = pltpu.make_async_copy(p_ref, out_ref.at[2 * x + y], local_sem)
    my_half = p_ref.at[pl.ds(cc * hr, hr), :]
    first = [copy(j, (x, y), cc, (*chip, cc), src=my_half) for j, chip in enumerate(chips)]
    passed = [copy(3 + j, chip, cc, sibling) for j, chip in enumerate(chips)]

    def start():
        mine.start()
        for cp in first:
            cp.start()

    def forward():
        for j, chip in enumerate(chips):
            copy(j, chip, cc, (x, y, cc)).wait_recv()
            passed[j].start()

    def finish():
        for j, chip in enumerate(chips):
            copy(3 + j, chip, 1 - cc, (x, y, cc)).wait_recv()
        for cp in first + passed:
            cp.wait_send()
        mine.wait()

    return start, forward, finish


def _gather_side(p):
    return _Side([p], [jax.ShapeDtypeStruct((N_CHIPS,) + p.shape, p.dtype)], GATHER_SEMS, _gather_ops)


def _swap_halves(g):
    n, r, c = g.shape
    hr = r // 2

    def body(g_ref, out_ref, send_sem, recv_sem):
        x, y, cc = _place()
        cp = pltpu.make_async_remote_copy(
            src_ref=g_ref.at[:, pl.ds((1 - cc) * hr, hr), :], dst_ref=out_ref,
            send_sem=send_sem, recv_sem=recv_sem, device_id=(x, y, 1 - cc), device_id_type=MESH)
        cp.start()
        cp.wait()

    any_spec = pl.BlockSpec(memory_space=pl.ANY)
    return pl.pallas_call(
        body, out_shape=jax.ShapeDtypeStruct((n, hr, c), g.dtype), in_specs=[any_spec], out_specs=any_spec,
        scratch_shapes=[pltpu.SemaphoreType.DMA, pltpu.SemaphoreType.DMA], name="swap_halves")(g)


SCATTER_SEMS = [pltpu.SemaphoreType.DMA((7,)), pltpu.SemaphoreType.DMA((7,)), pltpu.SemaphoreType.DMA]


def _scatter_ops(in_refs, out_refs, sems):
    (p_ref,), (out_ref,) = in_refs, out_refs
    send_sems, recv_sems, local_sem = sems
    hr = p_ref.shape[1]
    x, y, cc = _place()
    me = 2 * x + y
    sibling = (x, y, 1 - cc)
    chips = _other_chips(x, y)
    ids = [2 * chip[0] + chip[1] for chip in chips]

    def land(src, core):
        return out_ref.at[src, pl.ds(core * hr, hr), :]

    def copy(k, src_ref, dst_ref, to):
        return pltpu.make_async_remote_copy(src_ref=src_ref, dst_ref=dst_ref, send_sem=send_sems.at[k],
                                            recv_sem=recv_sems.at[k], device_id=to, device_id_type=MESH)

    mine = pltpu.make_async_copy(p_ref.at[me], land(me, cc), local_sem)
    sends = [copy(j, p_ref.at[ids[j]], land(me, cc), (*chip, cc)) for j, chip in enumerate(chips)]
    sends.append(copy(3, p_ref.at[me], land(me, cc), sibling))
    passed = [copy(4 + j, land(ids[j], cc), land(ids[j], cc), sibling) for j in range(3)]

    def start():
        mine.start()
        for cp in sends:
            cp.start()

    def forward():
        for j in range(3):
            copy(j, p_ref.at[me], land(ids[j], cc), (x, y, cc)).wait_recv()
            passed[j].start()

    def finish():
        copy(3, p_ref.at[me], land(me, 1 - cc), (x, y, cc)).wait_recv()
        for j in range(3):
            copy(4 + j, p_ref.at[me], land(ids[j], 1 - cc), (x, y, cc)).wait_recv()
        for cp in sends + passed:
            cp.wait_send()
        mine.wait()

    return start, forward, finish


def _scatter_side(p):
    n, hr, c = p.shape
    return _Side([p], [jax.ShapeDtypeStruct((n, 2 * hr, c), p.dtype)], SCATTER_SEMS, _scatter_ops)


def _gather_small(blk):
    m, n = blk.shape

    def body(x_ref, out_ref, send_sems, recv_sems, local_sem):
        x, y, cc = _place()
        me, sibling = (x, y, cc), (x, y, 1 - cc)
        chips = _other_chips(x, y)

        def slot(px, py, pc):
            return out_ref.at[4 * px + 2 * py + pc]

        def copy(k, block, to, src=None):
            return pltpu.make_async_remote_copy(
                src_ref=slot(*block) if src is None else src, dst_ref=slot(*block),
                send_sem=send_sems.at[k], recv_sem=recv_sems.at[k], device_id=to, device_id_type=MESH)

        mine = pltpu.make_async_copy(x_ref, slot(*me), local_sem)
        mine.start()
        first = [copy(0, me, sibling, src=x_ref)]
        first += [copy(1 + j, me, (*chip, cc), src=x_ref) for j, chip in enumerate(chips)]
        for cp in first:
            cp.start()
        passed = [copy(4 + j, (*chip, cc), sibling) for j, chip in enumerate(chips)]
        for j, chip in enumerate(chips):
            copy(1 + j, (*chip, cc), me).wait_recv()
            passed[j].start()
        copy(0, sibling, me).wait_recv()
        for j, chip in enumerate(chips):
            copy(4 + j, (*chip, 1 - cc), me).wait_recv()
        for cp in first + passed:
            cp.wait_send()
        mine.wait()

    vmem = pl.BlockSpec(memory_space=pltpu.VMEM)
    return pl.pallas_call(
        body, out_shape=jax.ShapeDtypeStruct((N_DEV, m, n), blk.dtype), in_specs=[vmem], out_specs=vmem,
        scratch_shapes=[pltpu.SemaphoreType.DMA((7,)), pltpu.SemaphoreType.DMA((7,)), pltpu.SemaphoreType.DMA],
        name="gather_small")(blk)


def _pack_a(sh, dtype):
    w = sh["w_in"].astype(dtype)
    return jnp.concatenate([w[:, 0:PACK_W], jnp.pad(w[:, PACK_W:], ((0, 0), (0, 2 * PACK_W - w.shape[1])))], axis=0)


def _pack_b(sh, dtype):
    o3 = jnp.concatenate([sh["w_gla_o"], sh["w_fox_o"], sh["w_mem_o"], jnp.zeros((512, 256), sh["w_gla_o"].dtype)], axis=1)
    au = jnp.pad(sh["w_alpha_up"], ((0, PACK_ROWS_B - 3072 - 16), (0, PACK_W - 64)))
    return jnp.concatenate([sh["w_ff1"], sh["w_ff2"], sh["w_mem_kv"], sh["w_out"], o3, au], axis=0).astype(dtype)


def _unpack_a(pa):
    return {"w_in": jnp.concatenate([pa[0:1024], pa[1024:2048, 0:1670 - PACK_W]], axis=1)}


def _unpack_b(pb):
    return {"w_ff1": pb[0:1024], "w_ff2": pb[1024:2048], "w_mem_kv": pb[2048:2304], "w_out": pb[2304:2560],
            "w_gla_o": pb[2560:3072, 0:256], "w_fox_o": pb[2560:3072, 256:512], "w_mem_o": pb[2560:3072, 512:768],
            "w_alpha_up": pb[3072:3088, 0:64]}


def _unpack(packed):
    return {**_unpack_a(packed[0:PACK_ROWS_A]), **_unpack_b(packed[PACK_ROWS_A:])}


def _split_shards(name, full):
    return jnp.split(full, N_CHIPS, axis=SHARD_AXIS[name])


def _pack_small(vals, scalar=None):
    row4 = jnp.concatenate([vals["b_alpha"].reshape(-1), vals["b_forget"].reshape(-1), jnp.zeros((D - 264,), F32)])
    row5 = jnp.concatenate([vals["g_gla_head"].reshape(-1), jnp.zeros((D - 512,), F32)])
    row6 = jnp.zeros((D,), F32) if scalar is None else jnp.broadcast_to(scalar, (D,))
    rows = [vals["g_mix"].reshape(-1), vals["g_mem"].reshape(-1), vals["g_ffn"].reshape(-1), vals["g_final"].reshape(-1),
            row4, row5, row6, jnp.zeros((D,), F32)]
    return jnp.stack(rows)


def _unpack_small(blk):
    return {"g_mix": blk[0].reshape(1, D), "g_mem": blk[1].reshape(1, D), "g_ffn": blk[2].reshape(1, D),
            "g_final": blk[3].reshape(D), "b_alpha": blk[4, 0:256].reshape(1, 256), "b_forget": blk[4, 256:264].reshape(1, 8),
            "g_gla_head": blk[5, 0:512].reshape(1, 4, 128)}


def _local_step(x, mem, target, wb, small, exchange=None):
    s = x.shape[0]
    nm = mem.shape[0]
    t = _row_tile(s)
    nb = s // t
    b_alpha = small["b_alpha"].reshape(1, 256)
    bias_e = jnp.concatenate([jnp.zeros((FF_LANE,), F32), small["b_forget"].reshape(-1),
                              jnp.zeros((PE_W - FF_LANE - 8,), F32)]).reshape(1, PE_W)
    g_mix, g_mem, g_ffn = small["g_mix"].reshape(1, D), small["g_mem"].reshape(1, D), small["g_ffn"].reshape(1, D)
    g_final = small["g_final"].reshape(1, D)
    g_head = small["g_gla_head"].reshape(1, 512)

    if exchange is None:
        u, r1 = _rms_fwd(x, g_mix, "norm_mix")
    else:
        u, r1, gathered = _rms_fwd(x, g_mix, "norm_mix", side=exchange.gather_a)
        wb = exchange.weights_a(gathered)
    w_in = wb["w_in"]
    w_main = jnp.concatenate([w_in[:, 3608:6680], w_in[:, 0:1536], w_in[:, 1552:3088], w_in[:, 3096:3608]], axis=1)
    w_e = jnp.concatenate([w_in[:, 1536:1552], w_in[:, 3088:3096], jnp.zeros((D, PE_W - 24), BF16)], axis=1)
    w_in_pt = _transpose(jnp.concatenate([w_main, w_e], axis=1), "t_w_in")
    big = min(s, 1024)
    if exchange is None:
        pm, pe = _proj(u, w_main, w_e)
    else:
        pm, pe, gathered = _proj(u, w_main, w_e, side=exchange.gather_b)
        wb = {**wb, **exchange.weights_b(gathered)}
    wau_p = jnp.concatenate([wb["w_alpha_up"], jnp.zeros((PE_W - 16, 256), BF16)], axis=0)
    o_gla, states = _gla_fwd(pm, pe, wau_p, b_alpha)
    fcum = _fcum_fwd(pe, bias_e)
    tb = _fox_tables()
    qf_aug, k_aug, v_aug, vt, qt, kt = _fox_prep(pm, fcum, None, tb, backward=False)
    o_fox, lse = _fox_fwd(k_aug, qf_aug, vt)
    mn, rm = _rms_fwd(mem, g_mem, "norm_mem")
    mkv = _mm_nn(mn, wb["w_mem_kv"], out_dtype=BF16, tm=nm, tn=512, tk=D, name="mem_kv")
    o_mem = _mem_attn_fwd(pm, mkv)
    merged, h1, u2, r2 = _merge_fwd(x, pm, o_gla, o_fox, o_mem, g_head, wb["w_gla_o"], wb["w_fox_o"], wb["w_mem_o"],
                                    wb["w_out"], g_ffn)
    a = _mm_nn(u2, wb["w_ff1"], out_dtype=BF16, tm=big, tn=1024, tk=D, name="ff1")
    dh2, dh2b, loss8, dgfin8 = _ff2_loss(a, wb["w_ff2"], h1, g_final, target)
    loss = 0.5 * jnp.sum(loss8) / D

    da = _mm_nn(dh2b, _transpose(wb["w_ff2"], "t_w_ff2"), out_dtype=BF16, tm=big, tn=1024, tk=D, name="d_act",
                epi=lambda acc, at: acc * (2.0 * jnp.maximum(at.astype(F32), 0.0)), extra=a)
    gw = {}
    gw["w_ff2"] = _mm_tn(a, dh2b, tm=1024, tn=D, ts=big, name="dw_ff2", a_fn=_relu2_bf16)
    gw["w_ff1"] = _mm_tn(u2, da, tm=D, tn=1024, ts=big, name="dw_ff1")
    dh1, dh1b, dgffn8 = _mm_norm_bwd([da], _transpose(wb["w_ff1"], "t_w_ff1"), h1, r2, g_ffn, dh2, name="d_h1", want_bf16=True)
    gw["w_out"] = _mm_tn(merged, dh1b, tm=D, tn=D, ts=big, name="dw_out")
    (dgates, dgg, do_gla, do_aug, do_t, do_mem, gw["w_gla_o"], gw["w_fox_o"], gw["w_mem_o"], dgh8) = _merge_bwd(
        dh1b, pm, o_gla, o_fox, o_mem, g_head, wb["w_gla_o"], wb["w_fox_o"], wb["w_mem_o"],
        *[_transpose(wb[n], "t_" + n) for n in ("w_gla_o", "w_fox_o", "w_mem_o", "w_out")], tb["spread"], tb["d_to_do"])
    dgq, dgk, dgv, de_gla, dwau_p, dba8 = _gla_bwd(pm, pe, wau_p, wau_p.T, b_alpha, do_gla, states)
    gw["w_alpha_up"] = dwau_p[0:16, :]
    q_aug = _fox_prep(pm, fcum, lse.reshape(8, s), tb, backward=True)
    dfq_t, dfrow, dfk_t, dfv_t, dfcol = _fox_bwd(q_aug, do_aug, qt, do_t, k_aug, v_aug, kt)
    dfq, dfk, dfv, df = _fox_post(dfq_t, dfk_t, dfv_t, dfrow[:, 0, :], dfcol.reshape(8, s), tb)
    de_fox, dbf8 = _fcum_bwd(pe, bias_e, df)
    dmq, dmk, dmv = _mem_attn_bwd(pm, mkv, do_mem)
    dmkv = jnp.concatenate([dmk, dmv], axis=1).astype(BF16)
    gw["w_mem_kv"] = _mm_tn(mn, dmkv, tm=D, tn=D, ts=nm, name="dw_mem_kv")
    dmn_g = _mm_nn(dmkv, _transpose(wb["w_mem_kv"], "t_w_mem_kv"), out_dtype=F32, tm=nm, tn=D, tk=D, name="d_mem_norm")
    dgmem8 = _gain_grad(dmn_g, mem, rm, "dg_mem")
    de = (de_gla + de_fox).astype(BF16)
    dproj = [dgates, dgq, dgk, dgv, dgg, dfq, dfk, dfv, dmq, de]
    dw_gates = _mm_tn(u, dgates, tm=D, tn=1024, ts=big, name="dw_in_gates")
    dw_g = _mm_tn_cat(u, [dgq, dgk, dgv], ts=big, name="dw_in_gla")
    dw_gf = _mm_tn_cat(u, [dgg, dfq], ts=big, name="dw_in_gg_fq")
    dw_f = _mm_tn_cat(u, [dfk, dfv], ts=big, name="dw_in_fk_fv")
    dw_m = _mm_tn_cat(u, [dmq, de], ts=big, name="dw_in_mq_narrow")
    gw["w_in"] = jnp.concatenate([dw_g, dw_gf[:, 0:512], dw_m[:, 512:528], dw_gf[:, 512:1024], dw_f,
                                  dw_m[:, 528:536], dw_m[:, 0:512], dw_gates], axis=1)
    if exchange is None:
        grad_x, dgmix8 = _mm_norm_bwd(dproj, w_in_pt, x, r1, g_mix, dh1, name="d_x", want_bf16=False)
        exchanged = None
    else:
        grad_x, dgmix8, exchanged = _mm_norm_bwd(dproj, w_in_pt, x, r1, g_mix, dh1, name="d_x", want_bf16=False,
                                                 side=exchange.scatter(gw))
    gs = {"g_mix": dgmix8.sum(0), "g_mem": dgmem8.sum(0), "g_ffn": dgffn8.sum(0), "g_final": dgfin8.sum(0),
          "b_alpha": dba8.sum(0), "b_forget": dbf8.sum(0)[FF_LANE:FF_LANE + 8], "g_gla_head": dgh8.sum(0)}
    return loss, grad_x, gw, gs, exchanged


def kernel(x, mem, g_mix, w_in, w_alpha_up, b_alpha, b_forget, g_gla_head, g_mem, w_mem_kv, w_gla_o, w_fox_o, w_mem_o, w_out, g_ffn, w_ff1, w_ff2, g_final, loss_target, m_g_mix, m_w_in, m_w_alpha_up, m_b_alpha, m_b_forget, m_g_gla_head, m_g_mem, m_w_mem_kv, m_w_gla_o, m_w_fox_o, m_w_mem_o, m_w_out, m_g_ffn, m_w_ff1, m_w_ff2, m_g_final, v_g_mix, v_w_in, v_w_alpha_up, v_b_alpha, v_b_forget, v_g_gla_head, v_g_mem, v_w_mem_kv, v_w_gla_o, v_w_fox_o, v_w_mem_o, v_w_out, v_g_ffn, v_w_ff1, v_w_ff2, v_g_final):
    args = dict(locals())
    w_sh = {n: args[n][0] for n in WEIGHTS}
    small = {n: args[n] for n in SMALL}

    def whole(parts):
        return {n: jnp.concatenate([p[n] for p in parts], axis=SHARD_AXIS[n]) for n in parts[0]}

    class Exchange:
        gather_a = _gather_side(_pack_a(w_sh, BF16))
        gather_b = _gather_side(_pack_b(w_sh, BF16))

        @staticmethod
        def weights_a(gathered):
            return whole([_unpack_a(gathered[k]) for k in range(N_CHIPS)])

        @staticmethod
        def weights_b(gathered):
            return whole([_unpack_b(gathered[k]) for k in range(N_CHIPS)])

        @staticmethod
        def scatter(gw):
            by_chip = {n: _split_shards(n, gw[n]) for n in WEIGHTS}
            packed = jnp.stack([jnp.concatenate([_pack_a({n: by_chip[n][k] for n in WEIGHTS}, BF16),
                                                 _pack_b({n: by_chip[n][k] for n in WEIGHTS}, BF16)], axis=0)
                                for k in range(N_CHIPS)])
            core = lax.axis_index("c").astype(jnp.int32).reshape(1)
            return _scatter_side(_add_half(core, packed, _swap_halves(packed), "chip_sum"))

    loss, grad_x, gw, gs, by_chip = _local_step(x[0], mem[0], loss_target[0], None, small, Exchange)
    g_out = {n: g[None] for n, g in _unpack(_sum4(by_chip, "shard_sum")).items()}
    d_out, m_out, v_out = {}, {}, {}
    for n in WEIGHTS:
        d_out[n], m_out[n], v_out[n] = _adam(args[n], g_out[n], args["m_" + n], args["v_" + n], "adam_" + n)

    small_all = _gather_small(_pack_small(gs, loss))
    sm = {n: args["m_" + n] for n in SMALL}
    sv = {n: args["v_" + n] for n in SMALL}
    gs_sum, sd, snm, snv = _adam_small(_pack_small(small), small_all, _pack_small(sm), _pack_small(sv))
    gs_o, sd_o, snm_o, snv_o = _unpack_small(gs_sum), _unpack_small(sd), _unpack_small(snm), _unpack_small(snv)

    names = ["g_mix", "w_in", "w_alpha_up", "b_alpha", "b_forget", "g_gla_head", "g_mem", "w_mem_kv", "w_gla_o", "w_fox_o",
             "w_mem_o", "w_out", "g_ffn", "w_ff1", "w_ff2", "g_final"]

    def pick(big, sml, n):
        return big[n] if n in big else sml[n]

    outs = [gs_sum[6, 0], grad_x[None]]
    for big, sml in ((g_out, gs_o), (d_out, sd_o), (m_out, snm_o), (v_out, snv_o)):
        outs += [pick(big, sml, n) for n in names]
    return tuple(outs)
```

```python
import functools

import numpy as np
import jax
import jax.numpy as jnp
from jax import lax
from jax.experimental import pallas as pl
from jax.experimental.pallas import tpu as pltpu

F32 = jnp.float32
BF16 = jnp.bfloat16
HI = lax.Precision.HIGHEST
MESH = pl.DeviceIdType.MESH

EPS = 1e-6
D = 1024
CHUNK = 64
GLA_TAU = 16.0
N_CHIPS = 4
N_DEV = 8
VMEM_LIMIT_BYTES = 56 * 1024 * 1024

ADAM_LR, ADAM_B1, ADAM_B2, ADAM_EPS, ADAM_WD, ADAM_STEP = 0.001, 0.9, 0.999, 1e-08, 0.01, 10

PM_W = 6656
PE_W = 128
C_GQ, C_GK, C_GV, C_GG, C_FQ, C_FK, C_FV, C_MQ = 3072, 3328, 3584, 4096, 4608, 5120, 5632, 6144
FF_LANE = 16

WEIGHTS = ("w_in", "w_alpha_up", "w_mem_kv", "w_gla_o", "w_fox_o", "w_mem_o", "w_out", "w_ff1", "w_ff2")
SHARD_AXIS = {"w_in": 1, "w_alpha_up": 1, "w_mem_kv": 0, "w_gla_o": 1, "w_fox_o": 1, "w_mem_o": 1, "w_out": 0,
              "w_ff1": 1, "w_ff2": 0}
SMALL = ("g_mix", "g_mem", "g_ffn", "g_final", "b_alpha", "b_forget", "g_gla_head")
PACK_W = 1024
PACK_ROWS_A = 2048
PACK_ROWS_B = 3104
PACK_ROWS = PACK_ROWS_A + PACK_ROWS_B


def _cp(*sem):
    return pltpu.CompilerParams(dimension_semantics=sem, vmem_limit_bytes=VMEM_LIMIT_BYTES)


def _dot(a, b, **kw):
    return jnp.dot(a, b, preferred_element_type=F32, **kw)


def _dot_nt(a, b, **kw):
    return lax.dot_general(a, b, (((1,), (1,)), ((), ())), preferred_element_type=F32, **kw)


def _dot_tn(a, b, **kw):
    return lax.dot_general(a, b, (((0,), (0,)), ((), ())), preferred_element_type=F32, **kw)


def _sigmoid(x):
    return 0.5 * jnp.tanh(0.5 * x) + 0.5


def _log_sigmoid(x):
    return -(jnp.maximum(-x, 0.0) + jnp.log1p(jnp.exp(-jnp.abs(x))))


def _fold8(x):
    m, n = x.shape
    return x.reshape(m // 8, 8, n).sum(axis=0)


def _iota(shape, dim):
    return lax.broadcasted_iota(jnp.int32, shape, dim)


def _row_tile(s):
    return min(s, 512)


class _Side:
    def __init__(self, inputs, out_shape, scratch, ops):
        self.inputs, self.out_shape, self.scratch, self.ops = list(inputs), list(out_shape), list(scratch), ops


ANY_SPEC = pl.BlockSpec(memory_space=pl.ANY)


def _mm_nn(a, b, *, out_dtype, tm, tn, tk, name, a_fn=None, epi=None, extra=None):
    m, k = a.shape
    _, n = b.shape
    nk = k // tk

    def body_one(*refs):
        a_ref, b_ref = refs[0], refs[1]
        at = a_ref[...] if a_fn is None else a_fn(a_ref[...])
        r = _dot(at, b_ref[...])
        if epi is not None:
            r = epi(r, None if extra is None else refs[2][...])
        refs[-1][...] = r.astype(out_dtype)

    if nk == 1:
        in_specs = [pl.BlockSpec((tm, k), lambda i, j: (i, 0)), pl.BlockSpec((k, tn), lambda i, j: (0, j))]
        args = [a, b]
        if extra is not None:
            in_specs.append(pl.BlockSpec((tm, tn), lambda i, j: (i, j)))
            args.append(extra)
        return pl.pallas_call(
            body_one, grid=(m // tm, n // tn), in_specs=in_specs, out_specs=pl.BlockSpec((tm, tn), lambda i, j: (i, j)),
            out_shape=jax.ShapeDtypeStruct((m, n), out_dtype), name=name, compiler_params=_cp("parallel", "parallel"))(*args)

    def body(*refs):
        if extra is None:
            a_ref, b_ref, o_ref, acc = refs
            x_ref = None
        else:
            a_ref, b_ref, x_ref, o_ref, acc = refs
        kk = pl.program_id(2)

        @pl.when(kk == 0)
        def _():
            acc[...] = jnp.zeros_like(acc)

        at = a_ref[...]
        if a_fn is not None:
            at = a_fn(at)
        acc[...] += _dot(at, b_ref[...])

        @pl.when(kk == nk - 1)
        def _():
            r = acc[...]
            if epi is not None:
                r = epi(r, None if x_ref is None else x_ref[...])
            o_ref[...] = r.astype(out_dtype)

    in_specs = [pl.BlockSpec((tm, tk), lambda i, j, kk: (i, kk)), pl.BlockSpec((tk, tn), lambda i, j, kk: (kk, j))]
    args = [a, b]
    if extra is not None:
        in_specs.append(pl.BlockSpec((tm, tn), lambda i, j, kk: (i, j)))
        args.append(extra)
    return pl.pallas_call(
        body, grid=(m // tm, n // tn, nk), in_specs=in_specs,
        out_specs=pl.BlockSpec((tm, tn), lambda i, j, kk: (i, j)),
        out_shape=jax.ShapeDtypeStruct((m, n), out_dtype),
        scratch_shapes=[pltpu.VMEM((tm, tn), F32)], name=name,
        compiler_params=_cp("parallel", "parallel", "arbitrary"))(*args)


def _mm_tn(a, b, *, tm, tn, ts, name, a_fn=None):
    s, m = a.shape
    _, n = b.shape
    ns = s // ts

    def body(a_ref, b_ref, o_ref, acc):
        kk = pl.program_id(2)

        @pl.when(kk == 0)
        def _():
            acc[...] = jnp.zeros_like(acc)

        at = a_ref[...]
        if a_fn is not None:
            at = a_fn(at)
        acc[...] += _dot_tn(at, b_ref[...])

        @pl.when(kk == ns - 1)
        def _():
            o_ref[...] = acc[...]

    return pl.pallas_call(
        body, grid=(m // tm, n // tn, ns),
        in_specs=[pl.BlockSpec((ts, tm), lambda i, j, kk: (kk, i)), pl.BlockSpec((ts, tn), lambda i, j, kk: (kk, j))],
        out_specs=pl.BlockSpec((tm, tn), lambda i, j, kk: (i, j)),
        out_shape=jax.ShapeDtypeStruct((m, n), F32),
        scratch_shapes=[pltpu.VMEM((tm, tn), F32)], name=name,
        compiler_params=_cp("parallel", "parallel", "arbitrary"))(a, b)


def _mm_tn_cat(a, bs, *, ts, name):
    s, m = a.shape
    n = sum(b.shape[1] for b in bs)
    ns = s // ts
    nb = len(bs)

    def body(*refs):
        a_ref, b_refs, o_ref, acc = refs[0], refs[1:1 + nb], refs[1 + nb], refs[2 + nb]
        kk = pl.program_id(0)

        @pl.when(kk == 0)
        def _():
            acc[...] = jnp.zeros_like(acc)

        bt = b_refs[0][...] if nb == 1 else jnp.concatenate([r[...] for r in b_refs], axis=1)
        acc[...] += _dot_tn(a_ref[...], bt)

        @pl.when(kk == ns - 1)
        def _():
            o_ref[...] = acc[...]

    return pl.pallas_call(
        body, grid=(ns,),
        in_specs=[pl.BlockSpec((ts, m), lambda kk: (kk, 0))] + [pl.BlockSpec((ts, b.shape[1]), lambda kk: (kk, 0)) for b in bs],
        out_specs=pl.BlockSpec((m, n), lambda kk: (0, 0)), out_shape=jax.ShapeDtypeStruct((m, n), F32),
        scratch_shapes=[pltpu.VMEM((m, n), F32)], name=name, compiler_params=_cp("arbitrary"))(a, *bs)


def _proj(u, w_main, w_e, side=None):
    s, k = u.shape
    n = w_main.shape[1]
    tm, tn = min(s, 1024), n // 4
    n_sin = 0 if side is None else len(side.inputs)
    n_sout = 0 if side is None else len(side.out_shape)

    def body(u_ref, w_ref, we_ref, *rest):
        pm_ref, pe_ref = rest[n_sin:n_sin + 2]
        i, j = pl.program_id(0), pl.program_id(1)
        if side is not None:
            start, forward, finish = side.ops(rest[:n_sin], rest[n_sin + 2:n_sin + 2 + n_sout], rest[n_sin + 2 + n_sout:])
            pl.when((i == 0) & (j == 0))(start)
        ut = u_ref[...]
        pm_ref[...] = _dot(ut, w_ref[...]).astype(BF16)

        @pl.when(j == 0)
        def _():
            pe_ref[...] = _dot(ut, we_ref[...])

        if side is not None:
            pl.when((i == (s // tm) * 5 // 8) & (j == n // tn - 1))(forward)
            pl.when((i == s // tm - 1) & (j == n // tn - 1))(finish)

    side_in = [] if side is None else side.inputs
    return pl.pallas_call(
        body, grid=(s // tm, n // tn),
        in_specs=[pl.BlockSpec((tm, k), lambda i, j: (i, 0)), pl.BlockSpec((k, tn), lambda i, j: (0, j)),
                  pl.BlockSpec((k, PE_W), lambda i, j: (0, 0))] + [ANY_SPEC] * n_sin,
        out_specs=[pl.BlockSpec((tm, tn), lambda i, j: (i, j)), pl.BlockSpec((tm, PE_W), lambda i, j: (i, 0))]
        + [ANY_SPEC] * n_sout,
        out_shape=[jax.ShapeDtypeStruct((s, n), BF16), jax.ShapeDtypeStruct((s, PE_W), F32)]
        + ([] if side is None else side.out_shape),
        scratch_shapes=[] if side is None else side.scratch,
        name="proj_main", compiler_params=_cp("arbitrary", "arbitrary"))(u, w_main, w_e, *side_in)


def _transpose(w, name):
    r, c = w.shape
    tr = min(r, 256)

    def body(w_ref, o_ref):
        o_ref[...] = w_ref[...].T

    return pl.pallas_call(body, grid=(r // tr,), in_specs=[pl.BlockSpec((tr, c), lambda i: (i, 0))],
                          out_specs=pl.BlockSpec((c, tr), lambda i: (0, i)),
                          out_shape=jax.ShapeDtypeStruct((c, r), w.dtype), name=name, compiler_params=_cp("parallel"))(w)


def _relu2_bf16(t):
    r = jnp.maximum(t.astype(F32), 0.0)
    return (r * r).astype(BF16)


def _rms_fwd(x, g, name, side=None):
    s, d = x.shape
    tm = min(s, 512)
    n_sin = 0 if side is None else len(side.inputs)
    n_sout = 0 if side is None else len(side.out_shape)

    def body(x_ref, g_ref, *rest):
        u_ref, r_ref = rest[n_sin:n_sin + 2]
        if side is not None:
            start, forward, finish = side.ops(rest[:n_sin], rest[n_sin + 2:n_sin + 2 + n_sout], rest[n_sin + 2 + n_sout:])
            pl.when(pl.program_id(0) == 0)(start)
        xv = x_ref[...]
        r = lax.rsqrt(jnp.mean(xv * xv, axis=-1, keepdims=True) + EPS)
        u_ref[...] = ((xv * r) * g_ref[...]).astype(BF16)
        r_ref[...] = r
        if side is not None:
            pl.when(pl.program_id(0) == s // tm - 1)(forward)
            pl.when(pl.program_id(0) == s // tm - 1)(finish)

    side_in = [] if side is None else side.inputs
    return pl.pallas_call(
        body, grid=(s // tm,),
        in_specs=[pl.BlockSpec((tm, d), lambda i: (i, 0)), pl.BlockSpec((1, d), lambda i: (0, 0))] + [ANY_SPEC] * n_sin,
        out_specs=[pl.BlockSpec((tm, d), lambda i: (i, 0)), pl.BlockSpec((tm, 1), lambda i: (i, 0))] + [ANY_SPEC] * n_sout,
        out_shape=[jax.ShapeDtypeStruct((s, d), BF16), jax.ShapeDtypeStruct((s, 1), F32)]
        + ([] if side is None else side.out_shape),
        scratch_shapes=[] if side is None else side.scratch,
        name=name, compiler_params=_cp("parallel" if side is None else "arbitrary"))(x, g, *side_in)


def _mm_norm_bwd(a_parts, b, xin, r, g, dres, *, name, want_bf16, side=None):
    s = a_parts[0].shape[0]
    k = b.shape[0]
    na = len(a_parts)
    offs = [sum(p.shape[1] for p in a_parts[:i]) for i in range(na)]
    assert offs[-1] + a_parts[-1].shape[1] == k
    tm = min(s, 512)
    n_out = 3 if want_bf16 else 2
    n_sin = 0 if side is None else len(side.inputs)
    n_sout = 0 if side is None else len(side.out_shape)

    def body(*refs):
        a_refs = refs[:na]
        b_ref, x_ref, r_ref, g_ref, dres_ref = refs[na:na + 5]
        rest = refs[na + 5:]
        outs = rest[n_sin:n_sin + n_out]
        dx_ref, dg_ref = outs[0], outs[-1]
        if side is not None:
            start, forward, finish = side.ops(rest[:n_sin], rest[n_sin + n_out:n_sin + n_out + n_sout], rest[n_sin + n_out + n_sout:])
            pl.when(pl.program_id(0) == 0)(start)

        @pl.when(pl.program_id(0) == 0)
        def _():
            dg_ref[...] = jnp.zeros_like(dg_ref)

        du = _dot(a_refs[0][...], b_ref[0:a_parts[0].shape[1], :])
        for a_ref, off, part in zip(a_refs[1:], offs[1:], a_parts[1:]):
            du = du + _dot(a_ref[...], b_ref[off:off + part.shape[1], :])
        xn = x_ref[...] * r_ref[...]
        dg_ref[...] += _fold8(du * xn)
        dxn = du * g_ref[...]
        dx = dres_ref[...] + r_ref[...] * (dxn - xn * jnp.mean(dxn * xn, axis=-1, keepdims=True))
        dx_ref[...] = dx
        if want_bf16:
            outs[1][...] = dx.astype(BF16)
        if side is not None:
            pl.when(pl.program_id(0) == (s // tm) * 13 // 16)(forward)
            pl.when(pl.program_id(0) == s // tm - 1)(finish)

    row = lambda i: (i, 0)
    const = lambda i: (0, 0)
    out_specs = [pl.BlockSpec((tm, D), row)]
    out_shape = [jax.ShapeDtypeStruct((s, D), F32)]
    if want_bf16:
        out_specs.append(pl.BlockSpec((tm, D), row))
        out_shape.append(jax.ShapeDtypeStruct((s, D), BF16))
    out_specs.append(pl.BlockSpec((8, D), const))
    out_shape.append(jax.ShapeDtypeStruct((8, D), F32))
    side_in = [] if side is None else side.inputs
    return pl.pallas_call(
        body, grid=(s // tm,),
        in_specs=[pl.BlockSpec((tm, p.shape[1]), row) for p in a_parts]
        + [pl.BlockSpec((k, D), const, pipeline_mode=pl.Buffered(1)),
           pl.BlockSpec((tm, D), row), pl.BlockSpec((tm, 1), row), pl.BlockSpec((1, D), const),
           pl.BlockSpec((tm, D), row)] + [ANY_SPEC] * n_sin,
        out_specs=out_specs + [ANY_SPEC] * n_sout, out_shape=out_shape + ([] if side is None else side.out_shape),
        scratch_shapes=[] if side is None else side.scratch,
        name=name, compiler_params=_cp("arbitrary"))(*a_parts, b, xin, r, g, dres, *side_in)


def _gla_consts():
    lmask = _iota((4 * CHUNK, CHUNK), 0) % CHUNK >= _iota((4 * CHUNK, CHUNK), 1)
    hmask = _iota((256, 256), 0) // CHUNK == _iota((256, 256), 1) // CHUNK
    bd = _iota((256, 512), 0) // CHUNK == _iota((256, 512), 1) // 128
    return lmask, hmask, bd


def _fold_heads(x):
    return x[0:64] + x[64:128] + x[128:192] + x[192:256]


def _gla_decays(la, b_scr, dec_scr):
    tri = (_iota((CHUNK, CHUNK), 0) >= _iota((CHUNK, CHUNK), 1)).astype(BF16)
    ones = jnp.ones((CHUNK, 128), BF16)
    for c in range(la.shape[0] // CHUNK):
        la3 = _split3(la[CHUNK * c:CHUNK * (c + 1)])
        b_scr[CHUNK * c:CHUNK * (c + 1), :] = _sum3(_dot(tri, la3), 1)
        dec_scr[c] = jnp.exp(_sum3(_dot_tn(la3, ones), 0))


def _gla_chunk(b, qc, kc):
    bl = b[CHUNK - 1:CHUNK, :]
    ep, en, ek = jnp.exp(b), jnp.exp(-b), jnp.exp(bl - b)
    return bl, ep, en, ek, qc * ep, qc * en, kc * en, kc * ep, kc * ek


def _gla_fwd(pm, pe, wau_p, b_alpha):
    s = pm.shape[0]
    t = min(s, 1024)
    nc = t // CHUNK

    def body(q_ref, k_ref, v_ref, e_ref, wau_ref, ba_ref, o_ref, st_ref, state, b_scr, dec_scr):
        @pl.when(pl.program_id(0) == 0)
        def _():
            state[...] = jnp.zeros_like(state)

        z = _dot(e_ref[...].astype(BF16), wau_ref[...]) + ba_ref[...]
        _gla_decays(_log_sigmoid(z) * (1.0 / GLA_TAU), b_scr, dec_scr)
        lmask, hmask, bd = _gla_consts()

        def chunk(c, carry):
            rows = pl.ds(pl.multiple_of(c * CHUNK, CHUNK), CHUNK)
            qc = q_ref[rows, :].astype(F32) * 0.125
            kc = k_ref[rows, :].astype(F32)
            vc = v_ref[rows, :]
            _, _, _, _, qp, qn, kn, kp, kk = _gla_chunk(b_scr[rows, :], qc, kc)
            decb = jnp.concatenate([dec_scr[c]] * 4, axis=1)
            qs = jnp.where(hmask, jnp.concatenate([qp] * 4, axis=0), 0.0).astype(BF16)
            qns = jnp.where(hmask, jnp.concatenate([qn] * 4, axis=0), 0.0).astype(BF16)
            attn = jnp.where(lmask, _dot_nt(qs, kn.astype(BF16)), _dot_nt(qns, kp.astype(BF16))).astype(BF16)
            st = state[...]
            o_intra = _fold_heads(jnp.where(bd, _dot(attn, vc), 0.0))
            o_ref[rows, :] = o_intra + _dot(qp.astype(BF16), st.astype(BF16))
            for h in range(4):
                st_ref[c, :, 128 * h:128 * (h + 1)] = st[64 * h:64 * (h + 1), 128 * h:128 * (h + 1)]
            kv = jnp.where(bd, _dot_tn(kk.astype(BF16), vc), 0.0)
            state[...] = st * decb + kv
            return carry

        lax.fori_loop(0, nc, chunk, 0)

    return pl.pallas_call(
        body, grid=(s // t,),
        in_specs=[pl.BlockSpec((t, 256), lambda i: (i, C_GQ // 256)), pl.BlockSpec((t, 256), lambda i: (i, C_GK // 256)),
                  pl.BlockSpec((t, 512), lambda i: (i, C_GV // 512)), pl.BlockSpec((t, PE_W), lambda i: (i, 0)),
                  pl.BlockSpec((PE_W, 256), lambda i: (0, 0)), pl.BlockSpec((1, 256), lambda i: (0, 0))],
        out_specs=[pl.BlockSpec((t, 512), lambda i: (i, 0)), pl.BlockSpec((nc, CHUNK, 512), lambda i: (i, 0, 0))],
        out_shape=[jax.ShapeDtypeStruct((s, 512), F32), jax.ShapeDtypeStruct((s // CHUNK, CHUNK, 512), F32)],
        scratch_shapes=[pltpu.VMEM((256, 512), F32), pltpu.VMEM((t, 256), F32), pltpu.VMEM((nc, 256, 128), F32)],
        name="gla_fwd", compiler_params=_cp("arbitrary"))(pm, pm, pm, pe, wau_p, b_alpha)


def _gla_bwd(pm, pe, wau_p, wau_pt, b_alpha, do, states):
    s = pm.shape[0]
    t = min(s, 1024)
    nc = t // CHUNK
    nb = s // t

    def body(q_ref, k_ref, v_ref, e_ref, wau_ref, waut_ref, ba_ref, do_ref, st_ref,
             dq_ref, dk_ref, dv_ref, de_ref, dwau_ref, dba_ref, gstate, b_scr, db_scr, dec_scr):
        @pl.when(pl.program_id(0) == 0)
        def _():
            gstate[...] = jnp.zeros_like(gstate)
            dwau_ref[...] = jnp.zeros_like(dwau_ref)
            dba_ref[...] = jnp.zeros_like(dba_ref)

        eb = e_ref[...].astype(BF16)
        z = _dot(eb, wau_ref[...]) + ba_ref[...]
        _gla_decays(_log_sigmoid(z) * (1.0 / GLA_TAU), b_scr, dec_scr)
        lmask, hmask, bd = _gla_consts()
        last_row = _iota((CHUNK, 256), 0) == CHUNK - 1

        def chunk(cc, carry):
            c = nc - 1 - cc
            rows = pl.ds(pl.multiple_of(c * CHUNK, CHUNK), CHUNK)
            qc = q_ref[rows, :].astype(F32) * 0.125
            kc = k_ref[rows, :].astype(F32)
            vc = v_ref[rows, :]
            dob = do_ref[rows, :]
            bl, ep, en, ek, qp, qn, kn, kp, kk = _gla_chunk(b_scr[rows, :], qc, kc)
            decb = jnp.concatenate([dec_scr[c]] * 4, axis=1)
            qs = jnp.where(hmask, jnp.concatenate([qp] * 4, axis=0), 0.0).astype(BF16)
            qns = jnp.where(hmask, jnp.concatenate([qn] * 4, axis=0), 0.0).astype(BF16)
            knb, kpb = kn.astype(BF16), kp.astype(BF16)
            attn = jnp.where(lmask, _dot_nt(qs, knb), _dot_nt(qns, kpb)).astype(BF16)
            st = jnp.where(bd, jnp.concatenate([st_ref[c]] * 4, axis=0), 0.0)
            g = gstate[...]
            gb = g.astype(BF16)
            do_s = jnp.where(bd, jnp.concatenate([dob] * 4, axis=0), jnp.zeros((), BF16))
            dattn = _dot_nt(do_s, vc)
            dv_ref[rows, :] = (_dot_tn(attn, do_s) + _dot(kk.astype(BF16), gb)).astype(BF16)
            dac = jnp.where(lmask, dattn, 0.0).astype(BF16)
            daa = jnp.where(lmask, 0.0, dattn).astype(BF16)
            dqp = _fold_heads(jnp.where(hmask, _dot(dac, knb), 0.0)) + _dot_nt(dob, st.astype(BF16))
            dqn = _fold_heads(jnp.where(hmask, _dot(daa, kpb), 0.0))
            dkn = _dot_tn(dac, qs)
            dkp = _dot_tn(daa, qns)
            dkk = _dot_nt(vc, gb)
            ddec = _dot_nt(jnp.ones((8, 1536), BF16), _split3(g * st))[0:1, :]
            gstate[...] = decb * g + jnp.where(bd, _dot_tn(qp.astype(BF16), dob), 0.0)
            dq_ref[rows, :] = ((dqp * ep + dqn * en) * 0.125).astype(BF16)
            dk_ref[rows, :] = (dkn * en + dkp * ep + dkk * ek).astype(BF16)
            dek = dkk * kc * ek
            db = (dqp * qc + dkp * kc) * ep - (dqn * qc + dkn * kc) * en - dek
            dbl = jnp.sum(dek, axis=0, keepdims=True) + ddec * jnp.exp(bl)
            db_scr[rows, :] = db + jnp.where(last_row, dbl, 0.0)
            return carry

        lax.fori_loop(0, nc, chunk, 0)
        triu = (_iota((CHUNK, CHUNK), 0) <= _iota((CHUNK, CHUNK), 1)).astype(BF16)
        dla = jnp.concatenate([_sum3(_dot(triu, _split3(db_scr[CHUNK * c:CHUNK * (c + 1), :])), 1) for c in range(nc)], axis=0)
        dz = dla * (1.0 / GLA_TAU) * _sigmoid(-z)
        dzb = dz.astype(BF16)
        dwau_ref[...] += _dot_tn(eb, dzb)
        dba_ref[...] += _fold8(dz)
        de_ref[...] = _dot(dzb, waut_ref[...])

    rev = lambda i: nb - 1 - i
    return pl.pallas_call(
        body, grid=(nb,),
        in_specs=[pl.BlockSpec((t, 256), lambda i: (rev(i), C_GQ // 256)), pl.BlockSpec((t, 256), lambda i: (rev(i), C_GK // 256)),
                  pl.BlockSpec((t, 512), lambda i: (rev(i), C_GV // 512)), pl.BlockSpec((t, PE_W), lambda i: (rev(i), 0)),
                  pl.BlockSpec((PE_W, 256), lambda i: (0, 0)), pl.BlockSpec((256, PE_W), lambda i: (0, 0)),
                  pl.BlockSpec((1, 256), lambda i: (0, 0)), pl.BlockSpec((t, 512), lambda i: (rev(i), 0)),
                  pl.BlockSpec((nc, CHUNK, 512), lambda i: (rev(i), 0, 0))],
        out_specs=[pl.BlockSpec((t, 256), lambda i: (rev(i), 0)), pl.BlockSpec((t, 256), lambda i: (rev(i), 0)),
                   pl.BlockSpec((t, 512), lambda i: (rev(i), 0)), pl.BlockSpec((t, PE_W), lambda i: (rev(i), 0)),
                   pl.BlockSpec((PE_W, 256), lambda i: (0, 0)), pl.BlockSpec((8, 256), lambda i: (0, 0))],
        out_shape=[jax.ShapeDtypeStruct((s, 256), BF16), jax.ShapeDtypeStruct((s, 256), BF16),
                   jax.ShapeDtypeStruct((s, 512), BF16), jax.ShapeDtypeStruct((s, PE_W), F32),
                   jax.ShapeDtypeStruct((PE_W, 256), F32), jax.ShapeDtypeStruct((8, 256), F32)],
        scratch_shapes=[pltpu.VMEM((256, 512), F32), pltpu.VMEM((t, 256), F32), pltpu.VMEM((t, 256), F32),
                        pltpu.VMEM((nc, 256, 128), F32)],
        name="gla_bwd", compiler_params=_cp("arbitrary"))(pm, pm, pm, pe, wau_p, wau_pt, b_alpha, do, states)


def _fcum_fwd(pe, bias):
    s = pe.shape[0]
    t = min(s, 512)

    def body(e_ref, b_ref, f_ref, carry):
        @pl.when(pl.program_id(0) == 0)
        def _():
            carry[...] = jnp.zeros_like(carry)

        lf = _log_sigmoid(e_ref[...] + b_ref[...])
        tri = (_iota((t, t), 0) >= _iota((t, t), 1)).astype(BF16)
        f = _sum3(_dot(tri, _split3(lf)), 1) + carry[0:1, :]
        f_ref[...] = f
        carry[...] = jnp.broadcast_to(f[t - 1:t, :], carry.shape)

    return pl.pallas_call(
        body, grid=(s // t,),
        in_specs=[pl.BlockSpec((t, PE_W), lambda i: (i, 0)), pl.BlockSpec((1, PE_W), lambda i: (0, 0))],
        out_specs=pl.BlockSpec((t, PE_W), lambda i: (i, 0)),
        out_shape=jax.ShapeDtypeStruct((s, PE_W), F32), scratch_shapes=[pltpu.VMEM((8, PE_W), F32)],
        name="fcum_fwd", compiler_params=_cp("arbitrary"))(pe, bias)


def _fcum_bwd(pe, bias, df):
    s = pe.shape[0]
    t = min(s, 512)
    nb = s // t

    def body(e_ref, b_ref, df_ref, de_ref, db_ref, carry):
        @pl.when(pl.program_id(0) == 0)
        def _():
            carry[...] = jnp.zeros_like(carry)
            db_ref[...] = jnp.zeros_like(db_ref)

        triu = (_iota((t, t), 0) <= _iota((t, t), 1)).astype(BF16)
        dlf = _sum3(_dot(triu, _split3(df_ref[...])), 1) + carry[0:1, :]
        carry[...] = jnp.broadcast_to(dlf[0:1, :], carry.shape)
        lane = _iota((t, PE_W), 1)
        dff = jnp.where((lane >= FF_LANE) & (lane < FF_LANE + 8), dlf * _sigmoid(-(e_ref[...] + b_ref[...])), 0.0)
        de_ref[...] = dff
        db_ref[...] += _fold8(dff)

    rev = lambda i: (nb - 1 - i, 0)
    return pl.pallas_call(
        body, grid=(nb,),
        in_specs=[pl.BlockSpec((t, PE_W), rev), pl.BlockSpec((1, PE_W), lambda i: (0, 0)), pl.BlockSpec((t, PE_W), rev)],
        out_specs=[pl.BlockSpec((t, PE_W), rev), pl.BlockSpec((8, PE_W), lambda i: (0, 0))],
        out_shape=[jax.ShapeDtypeStruct((s, PE_W), F32), jax.ShapeDtypeStruct((8, PE_W), F32)],
        scratch_shapes=[pltpu.VMEM((8, PE_W), F32)],
        name="fcum_bwd", compiler_params=_cp("arbitrary"))(pe, bias, df)


FOX_WIDE = 1024


def _split3(x):
    hi = x.astype(BF16)
    r = x - hi.astype(F32)
    mid = r.astype(BF16)
    lo = (r - mid.astype(F32)).astype(BF16)
    return jnp.concatenate([hi, mid, lo], axis=1)


def _pieces(x, lane0):
    lane = _iota(x.shape, 1)
    x = jnp.where((lane >= lane0) & (lane < lane0 + 8), x, 0.0)
    hi = x.astype(BF16).astype(F32)
    r = x - hi
    mid = r.astype(BF16).astype(F32)
    lo = (r - mid).astype(BF16).astype(F32)
    return (hi + pltpu.roll(mid, 8, 1) + pltpu.roll(lo, 16, 1)).astype(BF16)


def _sum3(x, axis):
    n = x.shape[axis] // 3
    parts = [lax.slice_in_dim(x, n * p, n * (p + 1), axis=axis) for p in range(3)]
    return (parts[0] + parts[1]) + parts[2]


def _spread(x, sp):
    return jnp.concatenate([_dot(x[:, 128 * g:128 * (g + 1)], sp) for g in range(4)], axis=1)


def _fox_tables():
    heads, lane = np.arange(8), np.arange(64)
    spread = np.zeros((128, 256), np.float32)
    spread[(64 * heads[:2, None] + lane).ravel(), (128 * heads[:2, None] + lane).ravel()] = 1.0
    def place(src_lane0, dst_off, val):
        t = np.zeros((128, 1024), np.float32)
        for p in range(3):
            for src in {src_lane0 + 8 * p, (src_lane0 - 8 * p) % 128}:
                t[src + heads, 128 * heads + dst_off + p] = val
        return t
    def const(off, val):
        c = np.zeros((1, 1024), np.float32)
        for p in range(3):
            c[0, 128 * heads + off + p] = val
        return c
    rows = np.zeros((8, 128), np.float32)
    rows[heads, FF_LANE + heads] = 1.0
    bf = lambda a: jnp.asarray(a, BF16)
    return dict(spread=bf(spread),
                f_to_q=bf(place(FF_LANE, 64, 1.0)), f_to_k=bf(place(FF_LANE, 67, -1.0)), d_to_do=bf(place(0, 64, 1.0)),
                ones_q=jnp.asarray(const(67, 1.0)), ones_k=jnp.asarray(const(64, 1.0)), ones_v=jnp.asarray(const(64, -1.0)),
                rows=jnp.asarray(rows))


LOG2E = 1.4426950408889634


def _fox_prep(pm, f128, lse8, tb, *, backward):
    s = pm.shape[0]
    tm = min(s, 1024) if backward else _row_tile(s)

    def body(*refs):
        if backward:
            q_ref, f_ref, lse_ref, sp_ref, fq_ref, cq_ref, rows_ref, qa_ref = refs
            f = f_ref[...] * LOG2E - _dot_tn(lse_ref[...], rows_ref[...], precision=HI)
            q2 = (q_ref[...].astype(F32) * (0.125 * LOG2E)).astype(BF16)
            qa_ref[...] = (_spread(q2, sp_ref[...]) + _dot(_pieces(f, FF_LANE), fq_ref[...]) + cq_ref[...]).astype(BF16)
            return
        (q_ref, k_ref, v_ref, f_ref, sp_ref, fq_ref, fk_ref, cq_ref, ck_ref, cv_ref,
         qa_ref, ka_ref, va_ref, vt_ref, qt_ref, kt_ref) = refs
        f3 = _pieces(f_ref[...] * LOG2E, FF_LANE)
        q, k, v = q_ref[...].astype(F32), k_ref[...], v_ref[...]
        sp = sp_ref[...]
        qa_ref[...] = (_spread((q * (0.125 * LOG2E)).astype(BF16), sp) + _dot(f3, fq_ref[...]) + cq_ref[...]).astype(BF16)
        ka_ref[...] = (_spread(k, sp) + _dot(f3, fk_ref[...]) + ck_ref[...]).astype(BF16)
        va_ref[...] = (_spread(v, sp) + cv_ref[...]).astype(BF16)
        vt_ref[...] = v.T
        qt_ref[...] = (q * 0.125).astype(BF16).T
        kt_ref[...] = (k.astype(F32) * 0.125).astype(BF16).T

    row = lambda i: (i, 0)
    const = lambda i: (0, 0)
    blk = lambda c: pl.BlockSpec((tm, 512), lambda i: (i, c // 512))
    wide = pl.BlockSpec((tm, 1024), row)
    mat = lambda a: pl.BlockSpec(a.shape, const)
    if backward:
        ins = [pm, f128, lse8, tb["spread"], tb["f_to_q"], tb["ones_q"], tb["rows"]]
        in_specs = [blk(C_FQ), pl.BlockSpec((tm, 128), row), pl.BlockSpec((8, tm), lambda i: (0, i))] + [mat(a) for a in ins[3:]]
        out_specs, out_shape = wide, jax.ShapeDtypeStruct((s, 1024), BF16)
    else:
        ins = [pm, pm, pm, f128, tb["spread"], tb["f_to_q"], tb["f_to_k"], tb["ones_q"], tb["ones_k"], tb["ones_v"]]
        in_specs = [blk(C_FQ), blk(C_FK), blk(C_FV), pl.BlockSpec((tm, 128), row)] + [mat(a) for a in ins[4:]]
        tr = pl.BlockSpec((512, tm), lambda i: (0, i))
        out_specs = [wide, wide, wide, tr, tr, tr]
        out_shape = [jax.ShapeDtypeStruct((s, 1024), BF16)] * 3 + [jax.ShapeDtypeStruct((512, s), BF16)] * 3
    return pl.pallas_call(body, grid=(s // tm,), in_specs=in_specs, out_specs=out_specs, out_shape=out_shape,
                          name="fox_prep_bwd" if backward else "fox_prep", compiler_params=_cp("parallel"))(*ins)


def _fox_post(dqt, dkt, dvt, rowsum8, colsum8, tb):
    s = dqt.shape[1]
    tm = min(s, 1024)

    def body(dqt_ref, dkt_ref, dvt_ref, rs_ref, cs_ref, rows_ref, dfq_ref, dfk_ref, dfv_ref, df_ref):
        dfq_ref[...] = dqt_ref[...].T.astype(BF16)
        dfk_ref[...] = dkt_ref[...].T
        dfv_ref[...] = dvt_ref[...].T
        df_ref[...] = _dot_tn(rs_ref[...] - cs_ref[...], rows_ref[...], precision=HI)

    row = lambda i: (i, 0)
    tr = pl.BlockSpec((512, tm), lambda i: (0, i))
    out = pl.BlockSpec((tm, 512), row)
    heads = pl.BlockSpec((8, tm), lambda i: (0, i))
    return pl.pallas_call(
        body, grid=(s // tm,),
        in_specs=[tr, tr, tr, heads, heads, pl.BlockSpec((8, 128), lambda i: (0, 0))],
        out_specs=[out, out, out, pl.BlockSpec((tm, 128), row)],
        out_shape=[jax.ShapeDtypeStruct((s, 512), BF16)] * 3 + [jax.ShapeDtypeStruct((s, 128), F32)],
        name="fox_post", compiler_params=_cp("parallel"))(dqt, dkt, dvt, rowsum8, colsum8, tb["rows"])


def _fox_fwd(k_aug, q_aug, vt):
    s = k_aug.shape[0]
    nh = 8
    tk = _row_tile(s)
    tq = min(s, 2 * FOX_WIDE)
    per = tq // tk

    def body(k_ref, q_ref, v_ref, o_ref, lse_ref, sbuf):
        i = pl.program_id(1)
        qa = q_ref[...]

        def scores(j):
            return _dot_nt(k_ref[pl.ds(pl.multiple_of(j * tk, tk), tk), :], qa)

        ones_row = (_iota((16, tk), 0) == 0).astype(BF16)

        def update(st, j, carry):
            m, acc = carry
            m2 = jnp.maximum(m, jnp.max(st, axis=0, keepdims=True))
            p = jnp.exp2(st - m2)
            vj = jnp.concatenate([v_ref[:, pl.ds(pl.multiple_of(j * tk, tk), tk)], ones_row], axis=0)
            return m2, jnp.exp2(m - m2) * acc + _dot(vj, p.astype(BF16))

        def step(a, carry):
            sbuf[1] = scores(2 * a + 1)
            carry = update(sbuf[0], 2 * a, carry)
            sbuf[0] = scores(2 * a + 2)
            return update(sbuf[1], 2 * a + 1, carry)

        n = i * per
        sbuf[0] = scores(0)
        carry = (jnp.full((1, tq), -1e30, F32), jnp.zeros((80, tq), F32))
        carry = lax.fori_loop(0, n // 2, step, carry)
        tri = _iota((tk, tk), 0) <= _iota((tk, tk), 1)
        late = [_dot_nt(k_ref[pl.ds(pl.multiple_of((n + r) * tk, tk), tk), :], qa[r * tk:, :]) for r in range(1, per)]
        for r in range(per):
            st = sbuf[0] if r == 0 else late[r - 1]
            head = jnp.where(tri, st[:, :tk], -1e30)
            st = head if st.shape[1] == tk else jnp.concatenate([head, st[:, tk:]], axis=1)
            part = update(st, n + r, tuple(c[:, r * tk:] for c in carry))
            carry = part if r == 0 else tuple(jnp.concatenate([old[:, :r * tk], new], axis=1) for old, new in zip(carry, part))
        m, acc = carry
        l = acc[64:65]
        o_ref[...] = (acc[0:64] / l).astype(BF16)
        lse_ref[0] = m + jnp.log2(l)

    return pl.pallas_call(
        body, grid=(nh, s // tq),
        in_specs=[pl.BlockSpec((s, 128), lambda h, i: (0, h)), pl.BlockSpec((tq, 128), lambda h, i: (i, h)),
                  pl.BlockSpec((64, s), lambda h, i: (h, 0))],
        out_specs=[pl.BlockSpec((64, tq), lambda h, i: (h, i)), pl.BlockSpec((1, 1, tq), lambda h, i: (h, 0, i))],
        out_shape=[jax.ShapeDtypeStruct((512, s), BF16), jax.ShapeDtypeStruct((nh, 1, s), F32)],
        scratch_shapes=[pltpu.VMEM((2, tk, tq), F32)],
        name="fox_fwd", compiler_params=_cp("parallel", "arbitrary"))(k_aug, q_aug, vt)


def _fox_bwd(q_aug, do_aug, qt, dot_, k_aug, v_aug, kt):
    s = q_aug.shape[0]
    nh = 8
    tq = _row_tile(s)
    tk = min(s, 2 * FOX_WIDE)
    per = tk // tq
    nqb = s // tq

    def body(qa_ref, da_ref, qt_ref, dt_ref, ka_ref, va_ref, kt_ref, dq_ref, rs_ref, dk_ref, dv_ref, dfk_ref):
        j = pl.program_id(1)

        @pl.when(j == 0)
        def _():
            dq_ref[...] = jnp.zeros_like(dq_ref)
            rs_ref[...] = jnp.zeros_like(rs_ref)

        ones_row = (_iota((16, tk), 0) == 0).astype(BF16)
        ka, va = ka_ref[...], va_ref[...]
        ks = jnp.concatenate([kt_ref[...], ones_row], axis=0)
        tri = _iota((tq, tq), 0) >= _iota((tq, tq), 1)

        def tile(i, w, carry):
            masked = w is not None
            w = tk if w is None else w
            rows = pl.ds(pl.multiple_of(i * tq, tq), tq)
            sp = _dot_nt(qa_ref[rows, :], ka[:w])
            if masked:
                last = jnp.where(tri, sp[:, w - tq:], -1e30)
                sp = last if w == tq else jnp.concatenate([sp[:, :w - tq], last], axis=1)
            p = jnp.exp2(sp)
            dsb = (p * _dot_nt(da_ref[rows, :], va[:w])).astype(BF16)
            dq = _dot_nt(ks[:, :w], dsb)
            dq_ref[:, rows] += dq[0:64]
            rs_ref[0, :, rows] += dq[64:72]
            new = (_dot(jnp.concatenate([qt_ref[:, rows], ones_row[:, :tq]], axis=0), dsb), _dot(dt_ref[:, rows], p.astype(BF16)))
            if w == tk:
                return tuple(c + d for c, d in zip(carry, new))
            return tuple(jnp.concatenate([c[:, :w] + d, c[:, w:]], axis=1) for c, d in zip(carry, new))

        carry = (jnp.zeros((80, tk), F32), jnp.zeros((64, tk), F32))
        for r in range(per):
            carry = tile(j * per + r, (r + 1) * tq, carry)
        dk, dv = lax.fori_loop((j + 1) * per, nqb, lambda i, c: tile(i, None, c), carry)
        dk_ref[...] = dk[0:64].astype(BF16)
        dv_ref[...] = dv.astype(BF16)
        dfk_ref[0] = dk[64:65]

    head_cols = lambda h, j: (0, h)
    head_rows = lambda h, j: (h, 0)
    once = dict(pipeline_mode=pl.Buffered(1))
    return pl.pallas_call(
        body, grid=(nh, s // tk),
        in_specs=[pl.BlockSpec((s, 128), head_cols, **once), pl.BlockSpec((s, 128), head_cols, **once),
                  pl.BlockSpec((64, s), head_rows, **once), pl.BlockSpec((64, s), head_rows, **once),
                  pl.BlockSpec((tk, 128), lambda h, j: (j, h)), pl.BlockSpec((tk, 128), lambda h, j: (j, h)),
                  pl.BlockSpec((64, tk), lambda h, j: (h, j))],
        out_specs=[pl.BlockSpec((64, s), head_rows), pl.BlockSpec((1, 8, s), lambda h, j: (h, 0, 0)),
                   pl.BlockSpec((64, tk), lambda h, j: (h, j)),
                   pl.BlockSpec((64, tk), lambda h, j: (h, j)), pl.BlockSpec((1, 1, tk), lambda h, j: (h, 0, j))],
        out_shape=[jax.ShapeDtypeStruct((512, s), F32), jax.ShapeDtypeStruct((nh, 8, s), F32),
                   jax.ShapeDtypeStruct((512, s), BF16),
                   jax.ShapeDtypeStruct((512, s), BF16), jax.ShapeDtypeStruct((nh, 1, s), F32)],
        name="fox_bwd", compiler_params=_cp("parallel", "arbitrary"))(q_aug, do_aug, qt, dot_, k_aug, v_aug, kt)


MEM_SCALE = 128 ** -0.5


def _mem_attn_fwd(pm, mkv):
    s = pm.shape[0]
    t = min(s, 1024)
    nm = mkv.shape[0]

    def body(q_ref, mk_ref, mv_ref, o_ref):
        for h in range(4):
            cols = slice(128 * h, 128 * (h + 1))
            sc = _dot_nt(q_ref[:, cols], mk_ref[:, cols]) * MEM_SCALE
            p = jnp.exp(sc - jnp.max(sc, axis=-1, keepdims=True))
            p = p / jnp.sum(p, axis=-1, keepdims=True)
            o_ref[:, cols] = _dot(p.astype(BF16), mv_ref[:, cols]).astype(BF16)

    return pl.pallas_call(
        body, grid=(s // t,),
        in_specs=[pl.BlockSpec((t, 512), lambda i: (i, C_MQ // 512)), pl.BlockSpec((nm, 512), lambda i: (0, 0)),
                  pl.BlockSpec((nm, 512), lambda i: (0, 1))],
        out_specs=pl.BlockSpec((t, 512), lambda i: (i, 0)),
        out_shape=jax.ShapeDtypeStruct((s, 512), BF16),
        name="mem_attn_fwd", compiler_params=_cp("parallel"))(pm, mkv, mkv)


def _mem_attn_bwd(pm, mkv, do):
    s = pm.shape[0]
    t = min(s, 1024)
    nm = mkv.shape[0]

    def body(q_ref, mk_ref, mv_ref, do_ref, dq_ref, dmk_ref, dmv_ref):
        @pl.when(pl.program_id(0) == 0)
        def _():
            dmk_ref[...] = jnp.zeros_like(dmk_ref)
            dmv_ref[...] = jnp.zeros_like(dmv_ref)

        for h in range(4):
            cols = slice(128 * h, 128 * (h + 1))
            qh, kh, vh, doh = q_ref[:, cols], mk_ref[:, cols], mv_ref[:, cols], do_ref[:, cols]
            sc = _dot_nt(qh, kh) * MEM_SCALE
            p = jnp.exp(sc - jnp.max(sc, axis=-1, keepdims=True))
            p = p / jnp.sum(p, axis=-1, keepdims=True)
            pb = p.astype(BF16)
            dp = _dot_nt(doh, vh)
            ds = (p * (dp - jnp.sum(p * dp, axis=-1, keepdims=True)) * MEM_SCALE).astype(BF16)
            dq_ref[:, cols] = _dot(ds, kh).astype(BF16)
            dmk_ref[:, cols] += _dot_tn(ds, qh)
            dmv_ref[:, cols] += _dot_tn(pb, doh)

    return pl.pallas_call(
        body, grid=(s // t,),
        in_specs=[pl.BlockSpec((t, 512), lambda i: (i, C_MQ // 512)), pl.BlockSpec((nm, 512), lambda i: (0, 0)),
                  pl.BlockSpec((nm, 512), lambda i: (0, 1)), pl.BlockSpec((t, 512), lambda i: (i, 0))],
        out_specs=[pl.BlockSpec((t, 512), lambda i: (i, 0)), pl.BlockSpec((nm, 512), lambda i: (0, 0)),
                   pl.BlockSpec((nm, 512), lambda i: (0, 0))],
        out_shape=[jax.ShapeDtypeStruct((s, 512), BF16), jax.ShapeDtypeStruct((nm, 512), F32),
                   jax.ShapeDtypeStruct((nm, 512), F32)],
        name="mem_attn_bwd", compiler_params=_cp("arbitrary"))(pm, mkv, mkv, do)


def _gain_grad(dxn_g, x, r, name):
    m, d = x.shape

    def body(d_ref, x_ref, r_ref, o_ref):
        o_ref[...] = _fold8(d_ref[...] * (x_ref[...] * r_ref[...]))

    return pl.pallas_call(body, out_shape=jax.ShapeDtypeStruct((8, d), F32), name=name,
                          compiler_params=pltpu.CompilerParams(vmem_limit_bytes=VMEM_LIMIT_BYTES))(dxn_g, x, r)


def _head_norm(o, gh):
    xs, rs = [], []
    for h in range(4):
        oh = o[:, 128 * h:128 * (h + 1)]
        r = lax.rsqrt(jnp.mean(oh * oh, axis=-1, keepdims=True) + EPS)
        xs.append(oh * r)
        rs.append(r)
    return xs, rs


def _merge_fwd(x, pm, o_gla, o_fox_t, o_mem, g_head, wg, wf, wm, wo, g_ffn):
    s = x.shape[0]
    t = min(s, 512)

    def body(x_ref, g0_ref, g1_ref, g2_ref, gg_ref, og_ref, of_ref, om_ref, gh_ref, wg_ref, wf_ref, wm_ref, wo_ref, gf_ref,
             mg_ref, h1_ref, u2_ref, r2_ref):
        xs, _ = _head_norm(og_ref[...], None)
        gg = gg_ref[...].astype(F32)
        sil = gg * _sigmoid(gg)
        ogn = jnp.concatenate(xs, axis=1) * gh_ref[...] * sil
        merged = (_sigmoid(g0_ref[...].astype(F32)) * _dot(ogn.astype(BF16), wg_ref[...])
                  + _sigmoid(g1_ref[...].astype(F32)) * _dot(of_ref[...].T, wf_ref[...])
                  + _sigmoid(g2_ref[...].astype(F32)) * _dot(om_ref[...], wm_ref[...]))
        mb = merged.astype(BF16)
        mg_ref[...] = mb
        h1 = x_ref[...] + _dot(mb, wo_ref[...])
        h1_ref[...] = h1
        r = lax.rsqrt(jnp.mean(h1 * h1, axis=-1, keepdims=True) + EPS)
        u2_ref[...] = ((h1 * r) * gf_ref[...]).astype(BF16)
        r2_ref[...] = r

    row = lambda i: (i, 0)
    const = lambda i: (0, 0)
    return pl.pallas_call(
        body, grid=(s // t,),
        in_specs=[pl.BlockSpec((t, D), row), pl.BlockSpec((t, D), lambda i: (i, 0)), pl.BlockSpec((t, D), lambda i: (i, 1)),
                  pl.BlockSpec((t, D), lambda i: (i, 2)), pl.BlockSpec((t, 512), lambda i: (i, C_GG // 512)),
                  pl.BlockSpec((t, 512), row), pl.BlockSpec((512, t), lambda i: (0, i)), pl.BlockSpec((t, 512), row),
                  pl.BlockSpec((1, 512), const), pl.BlockSpec((512, D), const), pl.BlockSpec((512, D), const),
                  pl.BlockSpec((512, D), const), pl.BlockSpec((D, D), const), pl.BlockSpec((1, D), const)],
        out_specs=[pl.BlockSpec((t, D), row), pl.BlockSpec((t, D), row), pl.BlockSpec((t, D), row), pl.BlockSpec((t, 1), row)],
        out_shape=[jax.ShapeDtypeStruct((s, D), BF16), jax.ShapeDtypeStruct((s, D), F32),
                   jax.ShapeDtypeStruct((s, D), BF16), jax.ShapeDtypeStruct((s, 1), F32)],
        name="merge_fwd", compiler_params=_cp("parallel"))(x, pm, pm, pm, pm, o_gla, o_fox_t, o_mem, g_head, wg, wf, wm, wo, g_ffn)


def _merge_bwd(dh1b, pm, o_gla, o_fox_t, o_mem, g_head, wg, wf, wm, wgt, wft, wmt, wot, spread, d_to_do):
    s = dh1b.shape[0]
    t = min(s, 256)

    def body(dh_ref, g0_ref, g1_ref, g2_ref, gg_ref, og_ref, of_ref, om_ref, gh_ref, wg_ref, wf_ref, wm_ref,
             wgt_ref, wft_ref, wmt_ref, wot_ref, sp_ref, dd_ref,
             dgt_ref, dgg_ref, dog_ref, da_ref, dot_ref, dom_ref, dwg_ref, dwf_ref, dwm_ref, dgh_ref):
        @pl.when(pl.program_id(0) == 0)
        def _():
            dwg_ref[...] = jnp.zeros_like(dwg_ref)
            dwf_ref[...] = jnp.zeros_like(dwf_ref)
            dwm_ref[...] = jnp.zeros_like(dwm_ref)
            dgh_ref[...] = jnp.zeros_like(dgh_ref)

        dmerged = _dot(dh_ref[...], wot_ref[...])
        og = og_ref[...]
        xs, rs = _head_norm(og, None)
        on = jnp.concatenate(xs, axis=1)
        gg = gg_ref[...].astype(F32)
        sg = _sigmoid(gg)
        sil = gg * sg
        gh = gh_ref[...]
        ognb = (on * gh * sil).astype(BF16)
        ofb, omb = of_ref[...].T, om_ref[...]
        douts = []
        for idx, (gref, ob, w_ref, wt_ref, dw_ref) in enumerate((
                (g0_ref, ognb, wg_ref, wgt_ref, dwg_ref), (g1_ref, ofb, wf_ref, wft_ref, dwf_ref),
                (g2_ref, omb, wm_ref, wmt_ref, dwm_ref))):
            gt = _sigmoid(gref[...].astype(F32))
            y = _dot(ob, w_ref[...])
            dgt_ref[:, D * idx:D * (idx + 1)] = (dmerged * y * gt * (1.0 - gt)).astype(BF16)
            dy = (gt * dmerged).astype(BF16)
            dw_ref[...] += _dot_tn(ob, dy)
            douts.append(_dot(dy, wt_ref[...]))
        dogn, dof, dom = douts
        dofb = dof.astype(BF16)
        dom_ref[...] = dom.astype(BF16)
        ind = (_iota((1536, 128), 0) % 512 // 64 == _iota((1536, 128), 1)).astype(BF16)
        delta = _dot(_split3(dofb.astype(F32) * ofb.astype(F32)), ind)
        da_ref[...] = (_spread(dofb, sp_ref[...]) + _dot(_pieces(delta, 0), dd_ref[...])).astype(BF16)
        dot_ref[...] = dofb.T
        dgg_ref[...] = (dogn * on * gh * (sg * (1.0 + gg * (1.0 - sg)))).astype(BF16)
        d_on = dogn * sil
        dgh_ref[...] += _fold8(d_on * on)
        dxn = d_on * gh
        outs = []
        for h in range(4):
            cols = slice(128 * h, 128 * (h + 1))
            dh_, xh = dxn[:, cols], xs[h]
            outs.append(rs[h] * (dh_ - xh * jnp.mean(dh_ * xh, axis=-1, keepdims=True)))
        dog_ref[...] = jnp.concatenate(outs, axis=1).astype(BF16)

    row = lambda i: (i, 0)
    const = lambda i: (0, 0)
    return pl.pallas_call(
        body, grid=(s // t,),
        in_specs=[pl.BlockSpec((t, D), row), pl.BlockSpec((t, D), lambda i: (i, 0)), pl.BlockSpec((t, D), lambda i: (i, 1)),
                  pl.BlockSpec((t, D), lambda i: (i, 2)), pl.BlockSpec((t, 512), lambda i: (i, C_GG // 512)),
                  pl.BlockSpec((t, 512), row), pl.BlockSpec((512, t), lambda i: (0, i)), pl.BlockSpec((t, 512), row),
                  pl.BlockSpec((1, 512), const), pl.BlockSpec((512, D), const), pl.BlockSpec((512, D), const),
                  pl.BlockSpec((512, D), const), pl.BlockSpec((D, 512), const), pl.BlockSpec((D, 512), const),
                  pl.BlockSpec((D, 512), const), pl.BlockSpec((D, D), const),
                  pl.BlockSpec((128, 256), const), pl.BlockSpec((128, 1024), const)],
        out_specs=[pl.BlockSpec((t, 3 * D), row), pl.BlockSpec((t, 512), row), pl.BlockSpec((t, 512), row),
                   pl.BlockSpec((t, 1024), row), pl.BlockSpec((512, t), lambda i: (0, i)), pl.BlockSpec((t, 512), row),
                   pl.BlockSpec((512, D), const), pl.BlockSpec((512, D), const), pl.BlockSpec((512, D), const),
                   pl.BlockSpec((8, 512), const)],
        out_shape=[jax.ShapeDtypeStruct((s, 3 * D), BF16), jax.ShapeDtypeStruct((s, 512), BF16),
                   jax.ShapeDtypeStruct((s, 512), BF16), jax.ShapeDtypeStruct((s, 1024), BF16),
                   jax.ShapeDtypeStruct((512, s), BF16), jax.ShapeDtypeStruct((s, 512), BF16),
                   jax.ShapeDtypeStruct((512, D), F32), jax.ShapeDtypeStruct((512, D), F32),
                   jax.ShapeDtypeStruct((512, D), F32), jax.ShapeDtypeStruct((8, 512), F32)],
        name="merge_bwd", compiler_params=_cp("arbitrary"))(
            dh1b, pm, pm, pm, pm, o_gla, o_fox_t, o_mem, g_head, wg, wf, wm, wgt, wft, wmt, wot, spread, d_to_do)


def _ff2_loss(a, w2, h1, g_final, target):
    s, k = a.shape
    tm = min(s, 512)

    def body(a_ref, w_ref, h1_ref, g_ref, t_ref, dh_ref, dhb_ref, loss_ref, dg_ref):
        @pl.when(pl.program_id(0) == 0)
        def _():
            loss_ref[...] = jnp.zeros_like(loss_ref)
            dg_ref[...] = jnp.zeros_like(dg_ref)

        h2 = h1_ref[...] + _dot(_relu2_bf16(a_ref[...]), w_ref[...])
        r = lax.rsqrt(jnp.mean(h2 * h2, axis=-1, keepdims=True) + EPS)
        xn = h2 * r
        g = g_ref[...]
        err = xn * g - t_ref[...]
        e2 = _fold8(err * err)
        part = e2[:, 0:128]
        for c in range(1, D // 128):
            part = part + e2[:, 128 * c:128 * (c + 1)]
        loss_ref[...] += part
        dy = err * (1.0 / D)
        dg_ref[...] += _fold8(dy * xn)
        dxn = dy * g
        dh = r * (dxn - xn * jnp.mean(dxn * xn, axis=-1, keepdims=True))
        dh_ref[...] = dh
        dhb_ref[...] = dh.astype(BF16)

    row = lambda i: (i, 0)
    const = lambda i: (0, 0)
    return pl.pallas_call(
        body, grid=(s // tm,),
        in_specs=[pl.BlockSpec((tm, k), row), pl.BlockSpec((k, D), const, pipeline_mode=pl.Buffered(1)),
                  pl.BlockSpec((tm, D), row), pl.BlockSpec((1, D), const), pl.BlockSpec((tm, D), row)],
        out_specs=[pl.BlockSpec((tm, D), row), pl.BlockSpec((tm, D), row), pl.BlockSpec((8, 128), const),
                   pl.BlockSpec((8, D), const)],
        out_shape=[jax.ShapeDtypeStruct((s, D), F32), jax.ShapeDtypeStruct((s, D), BF16),
                   jax.ShapeDtypeStruct((8, 128), F32), jax.ShapeDtypeStruct((8, D), F32)],
        name="ff2_loss", compiler_params=_cp("arbitrary"))(a, w2, h1, g_final, target)


def _adam(w, g, m, v, name):
    _, r, c = w.shape
    tr = r
    for cand in (512, 256, 128, 64, 32, 16, 8):
        if r % cand == 0 and cand * c * 4 <= (1 << 20):
            tr = cand
            break
    c1 = 1.0 - ADAM_B1 ** ADAM_STEP
    c2 = 1.0 - ADAM_B2 ** ADAM_STEP

    def body(w_ref, g_ref, m_ref, v_ref, d_ref, nm_ref, nv_ref):
        gv = g_ref[...]
        nm = ADAM_B1 * m_ref[...] + (1.0 - ADAM_B1) * gv
        nv = ADAM_B2 * v_ref[...] + (1.0 - ADAM_B2) * (gv * gv)
        d_ref[...] = -ADAM_LR * ((nm / c1) / (jnp.sqrt(nv / c2) + ADAM_EPS) + ADAM_WD * w_ref[...])
        nm_ref[...] = nm
        nv_ref[...] = nv

    spec = pl.BlockSpec((1, tr, c), lambda i: (0, i, 0))
    return pl.pallas_call(
        body, grid=(r // tr,), in_specs=[spec] * 4, out_specs=[spec] * 3,
        out_shape=[jax.ShapeDtypeStruct((1, r, c), F32)] * 3, name=name, compiler_params=_cp("parallel"))(w, g, m, v)


def _row_block(r):
    return max(d for d in range(16, 513, 16) if r % d == 0)


def _add_half(core, a, b, name):
    n, r, c = b.shape
    tr = _row_block(r)

    def body(core_ref, a_ref, b_ref, o_ref):
        o_ref[...] = (a_ref[...].astype(F32) + b_ref[...].astype(F32)).astype(BF16)

    spec = pl.BlockSpec((1, tr, c), lambda k, i, core_ref: (k, i, 0))
    half = pl.BlockSpec((1, tr, c), lambda k, i, core_ref: (k, i + core_ref[0] * (r // tr), 0))
    return pl.pallas_call(
        body, grid_spec=pltpu.PrefetchScalarGridSpec(num_scalar_prefetch=1, grid=(n, r // tr), in_specs=[half, spec],
                                                     out_specs=spec),
        out_shape=jax.ShapeDtypeStruct((n, r, c), BF16), name=name, compiler_params=_cp("parallel", "parallel"))(core, a, b)


def _sum4(a, name):
    _, r, c = a.shape
    tr = _row_block(r)

    def body(a_ref, o_ref):
        o_ref[...] = ((a_ref[0].astype(F32) + a_ref[1].astype(F32)) + a_ref[2].astype(F32)) + a_ref[3].astype(F32)

    return pl.pallas_call(body, grid=(r // tr,), in_specs=[pl.BlockSpec((4, tr, c), lambda i: (0, i, 0))],
                          out_specs=pl.BlockSpec((tr, c), lambda i: (i, 0)),
                          out_shape=jax.ShapeDtypeStruct((r, c), F32), name=name, compiler_params=_cp("parallel"))(a)


def _adam_small(w, gathered, m, v):
    c1 = 1.0 - ADAM_B1 ** ADAM_STEP
    c2 = 1.0 - ADAM_B2 ** ADAM_STEP

    def body(w_ref, g_ref, m_ref, v_ref, gs_ref, d_ref, nm_ref, nv_ref):
        gv = g_ref[0]
        for dev in range(1, N_DEV):
            gv = gv + g_ref[dev]
        gs_ref[...] = gv
        nm = ADAM_B1 * m_ref[...] + (1.0 - ADAM_B1) * gv
        nv = ADAM_B2 * v_ref[...] + (1.0 - ADAM_B2) * (gv * gv)
        d_ref[...] = -ADAM_LR * ((nm / c1) / (jnp.sqrt(nv / c2) + ADAM_EPS) + ADAM_WD * w_ref[...])
        nm_ref[...] = nm
        nv_ref[...] = nv

    return pl.pallas_call(body, out_shape=[jax.ShapeDtypeStruct((8, D), F32)] * 4, name="adam_small")(w, gathered, m, v)


def _place():
    return lax.axis_index("x"), lax.axis_index("y"), lax.axis_index("c")


def _other_chips(x, y):
    return [(1 - x, y), (x, 1 - y), (1 - x, 1 - y)]


GATHER_SEMS = [pltpu.SemaphoreType.DMA((7,)), pltpu.SemaphoreType.DMA((7,)), pltpu.SemaphoreType.DMA]


def _gather_ops(in_refs, out_refs, sems):
    (p_ref,), (out_ref,) = in_refs, out_refs
    send_sems, recv_sems, local_sem = sems
    hr = p_ref.shape[0] // 2
    q0 = hr // 2 // 16 * 16
    x, y, cc = _place()
    me, sibling = (x, y, cc), (x, y, 1 - cc)
    xn, yn, dg = _other_chips(x, y)

    def rows(chip, core, part=None):
        lo, n = {None: (0, hr), 0: (0, q0), 1: (q0, hr - q0)}[part]
        return out_ref.at[2 * chip[0] + chip[1], pl.ds(core * hr + lo, n), :]

    def copy(k, where, to, src=None):
        return pltpu.make_async_remote_copy(src_ref=where if src is None else src, dst_ref=where, send_sem=send_sems.at[k],
                                            recv_sem=recv_sems.at[k], device_id=to, device_id_type=MESH)

    mine = pltpu.make_async_copy(p_ref, out_ref.at[2 * x + y], local_sem)
    my_half = p_ref.at[pl.ds(cc * hr, hr), :]
    own = [copy(0, rows((x, y), cc), (*xn, cc), src=my_half), copy(1, rows((x, y), cc), (*yn, cc), src=my_half)]
    relay_x = copy(2, rows(yn, cc, 0), (*xn, cc))
    relay_y = copy(3, rows(xn, cc, 1), (*yn, cc))
    passed = [copy(4 + j, rows(chip, cc), sibling) for j, chip in enumerate((xn, yn, dg))]

    def start():
        mine.start()
        for cp in own:
            cp.start()

    def forward():
        copy(0, rows(xn, cc), me).wait_recv()
        relay_y.start()
        passed[0].start()
        copy(1, rows(yn, cc), me).wait_recv()
        relay_x.start()
        passed[1].start()
        copy(2, rows(dg, cc, 0), me).wait_recv()
        copy(3, rows(dg, cc, 1), me).wait_recv()
        passed[2].start()

    def finish():
        for j, chip in enumerate((xn, yn, dg)):
            copy(4 + j, rows(chip, 1 - cc), me).wait_recv()
        for cp in own + [relay_x, relay_y] + passed:
            cp.wait_send()
        mine.wait()

    return start, forward, finish


def _gather_side(p):
    return _Side([p], [jax.ShapeDtypeStruct((N_CHIPS,) + p.shape, p.dtype)], GATHER_SEMS, _gather_ops)


def _swap_halves(g):
    n, r, c = g.shape
    hr = r // 2

    def body(g_ref, out_ref, send_sem, recv_sem):
        x, y, cc = _place()
        cp = pltpu.make_async_remote_copy(
            src_ref=g_ref.at[:, pl.ds((1 - cc) * hr, hr), :], dst_ref=out_ref,
            send_sem=send_sem, recv_sem=recv_sem, device_id=(x, y, 1 - cc), device_id_type=MESH)
        cp.start()
        cp.wait()

    any_spec = pl.BlockSpec(memory_space=pl.ANY)
    return pl.pallas_call(
        body, out_shape=jax.ShapeDtypeStruct((n, hr, c), g.dtype), in_specs=[any_spec], out_specs=any_spec,
        scratch_shapes=[pltpu.SemaphoreType.DMA, pltpu.SemaphoreType.DMA], name="swap_halves")(g)


SCATTER_SEMS = [pltpu.SemaphoreType.DMA((7,)), pltpu.SemaphoreType.DMA((7,)), pltpu.SemaphoreType.DMA]


def _scatter_ops(in_refs, out_refs, sems):
    (p_ref,), (out_ref,) = in_refs, out_refs
    send_sems, recv_sems, local_sem = sems
    hr = p_ref.shape[1]
    x, y, cc = _place()
    me = 2 * x + y
    sibling = (x, y, 1 - cc)
    chips = _other_chips(x, y)
    ids = [2 * chip[0] + chip[1] for chip in chips]

    def land(src, core):
        return out_ref.at[src, pl.ds(core * hr, hr), :]

    def copy(k, src_ref, dst_ref, to):
        return pltpu.make_async_remote_copy(src_ref=src_ref, dst_ref=dst_ref, send_sem=send_sems.at[k],
                                            recv_sem=recv_sems.at[k], device_id=to, device_id_type=MESH)

    mine = pltpu.make_async_copy(p_ref.at[me], land(me, cc), local_sem)
    sends = [copy(j, p_ref.at[ids[j]], land(me, cc), (*chip, cc)) for j, chip in enumerate(chips)]
    sends.append(copy(3, p_ref.at[me], land(me, cc), sibling))
    passed = [copy(4 + j, land(ids[j], cc), land(ids[j], cc), sibling) for j in range(3)]

    def start():
        mine.start()
        for cp in sends:
            cp.start()

    def forward():
        for j in range(3):
            copy(j, p_ref.at[me], land(ids[j], cc), (x, y, cc)).wait_recv()
            passed[j].start()

    def finish():
        copy(3, p_ref.at[me], land(me, 1 - cc), (x, y, cc)).wait_recv()
        for j in range(3):
            copy(4 + j, p_ref.at[me], land(ids[j], 1 - cc), (x, y, cc)).wait_recv()
        for cp in sends + passed:
            cp.wait_send()
        mine.wait()

    return start, forward, finish


def _scatter_side(p):
    n, hr, c = p.shape
    return _Side([p], [jax.ShapeDtypeStruct((n, 2 * hr, c), p.dtype)], SCATTER_SEMS, _scatter_ops)


def _gather_small(blk):
    m, n = blk.shape

    def body(x_ref, out_ref, send_sems, recv_sems, local_sem):
        x, y, cc = _place()
        me, sibling = (x, y, cc), (x, y, 1 - cc)
        chips = _other_chips(x, y)

        def slot(px, py, pc):
            return out_ref.at[4 * px + 2 * py + pc]

        def copy(k, block, to, src=None):
            return pltpu.make_async_remote_copy(
                src_ref=slot(*block) if src is None else src, dst_ref=slot(*block),
                send_sem=send_sems.at[k], recv_sem=recv_sems.at[k], device_id=to, device_id_type=MESH)

        mine = pltpu.make_async_copy(x_ref, slot(*me), local_sem)
        mine.start()
        first = [copy(0, me, sibling, src=x_ref)]
        first += [copy(1 + j, me, (*chip, cc), src=x_ref) for j, chip in enumerate(chips)]
        for cp in first:
            cp.start()
        passed = [copy(4 + j, (*chip, cc), sibling) for j, chip in enumerate(chips)]
        for j, chip in enumerate(chips):
            copy(1 + j, (*chip, cc), me).wait_recv()
            passed[j].start()
        copy(0, sibling, me).wait_recv()
        for j, chip in enumerate(chips):
            copy(4 + j, (*chip, 1 - cc), me).wait_recv()
        for cp in first + passed:
            cp.wait_send()
        mine.wait()

    vmem = pl.BlockSpec(memory_space=pltpu.VMEM)
    return pl.pallas_call(
        body, out_shape=jax.ShapeDtypeStruct((N_DEV, m, n), blk.dtype), in_specs=[vmem], out_specs=vmem,
        scratch_shapes=[pltpu.SemaphoreType.DMA((7,)), pltpu.SemaphoreType.DMA((7,)), pltpu.SemaphoreType.DMA],
        name="gather_small")(blk)


def _pack_a(sh, dtype):
    w = sh["w_in"].astype(dtype)
    return jnp.concatenate([w[:, 0:PACK_W], jnp.pad(w[:, PACK_W:], ((0, 0), (0, 2 * PACK_W - w.shape[1])))], axis=0)


def _pack_b(sh, dtype):
    o3 = jnp.concatenate([sh["w_gla_o"], sh["w_fox_o"], sh["w_mem_o"], jnp.zeros((512, 256), sh["w_gla_o"].dtype)], axis=1)
    au = jnp.pad(sh["w_alpha_up"], ((0, PACK_ROWS_B - 3072 - 16), (0, PACK_W - 64)))
    return jnp.concatenate([sh["w_ff1"], sh["w_ff2"], sh["w_mem_kv"], sh["w_out"], o3, au], axis=0).astype(dtype)


def _unpack_a(pa):
    return {"w_in": jnp.concatenate([pa[0:1024], pa[1024:2048, 0:1670 - PACK_W]], axis=1)}


def _unpack_b(pb):
    return {"w_ff1": pb[0:1024], "w_ff2": pb[1024:2048], "w_mem_kv": pb[2048:2304], "w_out": pb[2304:2560],
            "w_gla_o": pb[2560:3072, 0:256], "w_fox_o": pb[2560:3072, 256:512], "w_mem_o": pb[2560:3072, 512:768],
            "w_alpha_up": pb[3072:3088, 0:64]}


def _unpack(packed):
    return {**_unpack_a(packed[0:PACK_ROWS_A]), **_unpack_b(packed[PACK_ROWS_A:])}


def _split_shards(name, full):
    return jnp.split(full, N_CHIPS, axis=SHARD_AXIS[name])


def _pack_small(vals, scalar=None):
    row4 = jnp.concatenate([vals["b_alpha"].reshape(-1), vals["b_forget"].reshape(-1), jnp.zeros((D - 264,), F32)])
    row5 = jnp.concatenate([vals["g_gla_head"].reshape(-1), jnp.zeros((D - 512,), F32)])
    row6 = jnp.zeros((D,), F32) if scalar is None else jnp.broadcast_to(scalar, (D,))
    rows = [vals["g_mix"].reshape(-1), vals["g_mem"].reshape(-1), vals["g_ffn"].reshape(-1), vals["g_final"].reshape(-1),
            row4, row5, row6, jnp.zeros((D,), F32)]
    return jnp.stack(rows)


def _unpack_small(blk):
    return {"g_mix": blk[0].reshape(1, D), "g_mem": blk[1].reshape(1, D), "g_ffn": blk[2].reshape(1, D),
            "g_final": blk[3].reshape(D), "b_alpha": blk[4, 0:256].reshape(1, 256), "b_forget": blk[4, 256:264].reshape(1, 8),
            "g_gla_head": blk[5, 0:512].reshape(1, 4, 128)}


def _local_step(x, mem, target, wb, small, exchange=None):
    s = x.shape[0]
    nm = mem.shape[0]
    t = _row_tile(s)
    nb = s // t
    b_alpha = small["b_alpha"].reshape(1, 256)
    bias_e = jnp.concatenate([jnp.zeros((FF_LANE,), F32), small["b_forget"].reshape(-1),
                              jnp.zeros((PE_W - FF_LANE - 8,), F32)]).reshape(1, PE_W)
    g_mix, g_mem, g_ffn = small["g_mix"].reshape(1, D), small["g_mem"].reshape(1, D), small["g_ffn"].reshape(1, D)
    g_final = small["g_final"].reshape(1, D)
    g_head = small["g_gla_head"].reshape(1, 512)

    if exchange is None:
        u, r1 = _rms_fwd(x, g_mix, "norm_mix")
    else:
        u, r1, gathered = _rms_fwd(x, g_mix, "norm_mix", side=exchange.gather_a)
        wb = exchange.weights_a(gathered)
    w_in = wb["w_in"]
    w_main = jnp.concatenate([w_in[:, 3608:6680], w_in[:, 0:1536], w_in[:, 1552:3088], w_in[:, 3096:3608]], axis=1)
    w_e = jnp.concatenate([w_in[:, 1536:1552], w_in[:, 3088:3096], jnp.zeros((D, PE_W - 24), BF16)], axis=1)
    w_in_pt = _transpose(jnp.concatenate([w_main, w_e], axis=1), "t_w_in")
    big = min(s, 1024)
    if exchange is None:
        pm, pe = _proj(u, w_main, w_e)
    else:
        pm, pe, gathered = _proj(u, w_main, w_e, side=exchange.gather_b)
        wb = {**wb, **exchange.weights_b(gathered)}
    wau_p = jnp.concatenate([wb["w_alpha_up"], jnp.zeros((PE_W - 16, 256), BF16)], axis=0)
    o_gla, states = _gla_fwd(pm, pe, wau_p, b_alpha)
    fcum = _fcum_fwd(pe, bias_e)
    tb = _fox_tables()
    qf_aug, k_aug, v_aug, vt, qt, kt = _fox_prep(pm, fcum, None, tb, backward=False)
    o_fox, lse = _fox_fwd(k_aug, qf_aug, vt)
    mn, rm = _rms_fwd(mem, g_mem, "norm_mem")
    mkv = _mm_nn(mn, wb["w_mem_kv"], out_dtype=BF16, tm=nm, tn=512, tk=D, name="mem_kv")
    o_mem = _mem_attn_fwd(pm, mkv)
    merged, h1, u2, r2 = _merge_fwd(x, pm, o_gla, o_fox, o_mem, g_head, wb["w_gla_o"], wb["w_fox_o"], wb["w_mem_o"],
                                    wb["w_out"], g_ffn)
    a = _mm_nn(u2, wb["w_ff1"], out_dtype=BF16, tm=big, tn=1024, tk=D, name="ff1")
    dh2, dh2b, loss8, dgfin8 = _ff2_loss(a, wb["w_ff2"], h1, g_final, target)
    loss = 0.5 * jnp.sum(loss8) / D

    da = _mm_nn(dh2b, _transpose(wb["w_ff2"], "t_w_ff2"), out_dtype=BF16, tm=big, tn=1024, tk=D, name="d_act",
                epi=lambda acc, at: acc * (2.0 * jnp.maximum(at.astype(F32), 0.0)), extra=a)
    gw = {}
    gw["w_ff2"] = _mm_tn(a, dh2b, tm=1024, tn=D, ts=big, name="dw_ff2", a_fn=_relu2_bf16)
    gw["w_ff1"] = _mm_tn(u2, da, tm=D, tn=1024, ts=big, name="dw_ff1")
    dh1, dh1b, dgffn8 = _mm_norm_bwd([da], _transpose(wb["w_ff1"], "t_w_ff1"), h1, r2, g_ffn, dh2, name="d_h1", want_bf16=True)
    gw["w_out"] = _mm_tn(merged, dh1b, tm=D, tn=D, ts=big, name="dw_out")
    (dgates, dgg, do_gla, do_aug, do_t, do_mem, gw["w_gla_o"], gw["w_fox_o"], gw["w_mem_o"], dgh8) = _merge_bwd(
        dh1b, pm, o_gla, o_fox, o_mem, g_head, wb["w_gla_o"], wb["w_fox_o"], wb["w_mem_o"],
        *[_transpose(wb[n], "t_" + n) for n in ("w_gla_o", "w_fox_o", "w_mem_o", "w_out")], tb["spread"], tb["d_to_do"])
    dgq, dgk, dgv, de_gla, dwau_p, dba8 = _gla_bwd(pm, pe, wau_p, wau_p.T, b_alpha, do_gla, states)
    gw["w_alpha_up"] = dwau_p[0:16, :]
    q_aug = _fox_prep(pm, fcum, lse.reshape(8, s), tb, backward=True)
    dfq_t, dfrow, dfk_t, dfv_t, dfcol = _fox_bwd(q_aug, do_aug, qt, do_t, k_aug, v_aug, kt)
    dfq, dfk, dfv, df = _fox_post(dfq_t, dfk_t, dfv_t, dfrow[:, 0, :], dfcol.reshape(8, s), tb)
    de_fox, dbf8 = _fcum_bwd(pe, bias_e, df)
    dmq, dmk, dmv = _mem_attn_bwd(pm, mkv, do_mem)
    dmkv = jnp.concatenate([dmk, dmv], axis=1).astype(BF16)
    gw["w_mem_kv"] = _mm_tn(mn, dmkv, tm=D, tn=D, ts=nm, name="dw_mem_kv")
    dmn_g = _mm_nn(dmkv, _transpose(wb["w_mem_kv"], "t_w_mem_kv"), out_dtype=F32, tm=nm, tn=D, tk=D, name="d_mem_norm")
    dgmem8 = _gain_grad(dmn_g, mem, rm, "dg_mem")
    de = (de_gla + de_fox).astype(BF16)
    dproj = [dgates, dgq, dgk, dgv, dgg, dfq, dfk, dfv, dmq, de]
    dw_gates = _mm_tn(u, dgates, tm=D, tn=1024, ts=big, name="dw_in_gates")
    dw_g = _mm_tn_cat(u, [dgq, dgk, dgv], ts=big, name="dw_in_gla")
    dw_gf = _mm_tn_cat(u, [dgg, dfq], ts=big, name="dw_in_gg_fq")
    dw_f = _mm_tn_cat(u, [dfk, dfv], ts=big, name="dw_in_fk_fv")
    dw_m = _mm_tn_cat(u, [dmq, de], ts=big, name="dw_in_mq_narrow")
    gw["w_in"] = jnp.concatenate([dw_g, dw_gf[:, 0:512], dw_m[:, 512:528], dw_gf[:, 512:1024], dw_f,
                                  dw_m[:, 528:536], dw_m[:, 0:512], dw_gates], axis=1)
    if exchange is None:
        grad_x, dgmix8 = _mm_norm_bwd(dproj, w_in_pt, x, r1, g_mix, dh1, name="d_x", want_bf16=False)
        exchanged = None
    else:
        grad_x, dgmix8, exchanged = _mm_norm_bwd(dproj, w_in_pt, x, r1, g_mix, dh1, name="d_x", want_bf16=False,
                                                 side=exchange.scatter(gw))
    gs = {"g_mix": dgmix8.sum(0), "g_mem": dgmem8.sum(0), "g_ffn": dgffn8.sum(0), "g_final": dgfin8.sum(0),
          "b_alpha": dba8.sum(0), "b_forget": dbf8.sum(0)[FF_LANE:FF_LANE + 8], "g_gla_head": dgh8.sum(0)}
    return loss, grad_x, gw, gs, exchanged


def kernel(x, mem, g_mix, w_in, w_alpha_up, b_alpha, b_forget, g_gla_head, g_mem, w_mem_kv, w_gla_o, w_fox_o, w_mem_o, w_out, g_ffn, w_ff1, w_ff2, g_final, loss_target, m_g_mix, m_w_in, m_w_alpha_up, m_b_alpha, m_b_forget, m_g_gla_head, m_g_mem, m_w_mem_kv, m_w_gla_o, m_w_fox_o, m_w_mem_o, m_w_out, m_g_ffn, m_w_ff1, m_w_ff2, m_g_final, v_g_mix, v_w_in, v_w_alpha_up, v_b_alpha, v_b_forget, v_g_gla_head, v_g_mem, v_w_mem_kv, v_w_gla_o, v_w_fox_o, v_w_mem_o, v_w_out, v_g_ffn, v_w_ff1, v_w_ff2, v_g_final):
    args = dict(locals())
    w_sh = {n: args[n][0] for n in WEIGHTS}
    small = {n: args[n] for n in SMALL}

    def whole(parts):
        return {n: jnp.concatenate([p[n] for p in parts], axis=SHARD_AXIS[n]) for n in parts[0]}

    class Exchange:
        gather_a = _gather_side(_pack_a(w_sh, BF16))
        gather_b = _gather_side(_pack_b(w_sh, BF16))

        @staticmethod
        def weights_a(gathered):
            return whole([_unpack_a(gathered[k]) for k in range(N_CHIPS)])

        @staticmethod
        def weights_b(gathered):
            return whole([_unpack_b(gathered[k]) for k in range(N_CHIPS)])

        @staticmethod
        def scatter(gw):
            by_chip = {n: _split_shards(n, gw[n]) for n in WEIGHTS}
            packed = jnp.stack([jnp.concatenate([_pack_a({n: by_chip[n][k] for n in WEIGHTS}, BF16),
                                                 _pack_b({n: by_chip[n][k] for n in WEIGHTS}, BF16)], axis=0)
                                for k in range(N_CHIPS)])
            core = lax.axis_index("c").astype(jnp.int32).reshape(1)
            return _scatter_side(_add_half(core, packed, _swap_halves(packed), "chip_sum"))

    loss, grad_x, gw, gs, by_chip = _local_step(x[0], mem[0], loss_target[0], None, small, Exchange)
    g_out = {n: g[None] for n, g in _unpack(_sum4(by_chip, "shard_sum")).items()}
    d_out, m_out, v_out = {}, {}, {}
    for n in WEIGHTS:
        d_out[n], m_out[n], v_out[n] = _adam(args[n], g_out[n], args["m_" + n], args["v_" + n], "adam_" + n)

    small_all = _gather_small(_pack_small(gs, loss))
    sm = {n: args["m_" + n] for n in SMALL}
    sv = {n: args["v_" + n] for n in SMALL}
    gs_sum, sd, snm, snv = _adam_small(_pack_small(small), small_all, _pack_small(sm), _pack_small(sv))
    gs_o, sd_o, snm_o, snv_o = _unpack_small(gs_sum), _unpack_small(sd), _unpack_small(snm), _unpack_small(snv)

    names = ["g_mix", "w_in", "w_alpha_up", "b_alpha", "b_forget", "g_gla_head", "g_mem", "w_mem_kv", "w_gla_o", "w_fox_o",
             "w_mem_o", "w_out", "g_ffn", "w_ff1", "w_ff2", "g_final"]

    def pick(big, sml, n):
        return big[n] if n in big else sml[n]

    outs = [gs_sum[6, 0], grad_x[None]]
    for big, sml in ((g_out, gs_o), (d_out, sd_o), (m_out, snm_o), (v_out, snv_o)):
        outs += [pick(big, sml, n) for n in names]
    return tuple(outs)
```

```python
import functools

import numpy as np
import jax
import jax.numpy as jnp
from jax import lax
from jax.experimental import pallas as pl
from jax.experimental.pallas import tpu as pltpu

F32 = jnp.float32
BF16 = jnp.bfloat16
HI = lax.Precision.HIGHEST
MESH = pl.DeviceIdType.MESH

EPS = 1e-6
D = 1024
CHUNK = 64
GLA_TAU = 16.0
N_CHIPS = 4
N_DEV = 8
VMEM_LIMIT_BYTES = 56 * 1024 * 1024

ADAM_LR, ADAM_B1, ADAM_B2, ADAM_EPS, ADAM_WD, ADAM_STEP = 0.001, 0.9, 0.999, 1e-08, 0.01, 10

PM_W = 6656
PE_W = 128
C_GQ, C_GK, C_GV, C_GG, C_FQ, C_FK, C_FV, C_MQ = 3072, 3328, 3584, 4096, 4608, 5120, 5632, 6144
FF_LANE = 16

WEIGHTS = ("w_in", "w_alpha_up", "w_mem_kv", "w_gla_o", "w_fox_o", "w_mem_o", "w_out", "w_ff1", "w_ff2")
SHARD_AXIS = {"w_in": 1, "w_alpha_up": 1, "w_mem_kv": 0, "w_gla_o": 1, "w_fox_o": 1, "w_mem_o": 1, "w_out": 0,
              "w_ff1": 1, "w_ff2": 0}
SMALL = ("g_mix", "g_mem", "g_ffn", "g_final", "b_alpha", "b_forget", "g_gla_head")
PACK_W = 1024
PACK_ROWS_A = 2048
PACK_ROWS_B = 3104
PACK_ROWS = PACK_ROWS_A + PACK_ROWS_B


def _cp(*sem):
    return pltpu.CompilerParams(dimension_semantics=sem, vmem_limit_bytes=VMEM_LIMIT_BYTES)


def _dot(a, b, **kw):
    return jnp.dot(a, b, preferred_element_type=F32, **kw)


def _dot_nt(a, b, **kw):
    return lax.dot_general(a, b, (((1,), (1,)), ((), ())), preferred_element_type=F32, **kw)


def _dot_tn(a, b, **kw):
    return lax.dot_general(a, b, (((0,), (0,)), ((), ())), preferred_element_type=F32, **kw)


def _sigmoid(x):
    return 0.5 * jnp.tanh(0.5 * x) + 0.5


def _log_sigmoid(x):
    return -(jnp.maximum(-x, 0.0) + jnp.log1p(jnp.exp(-jnp.abs(x))))


def _fold8(x):
    m, n = x.shape
    return x.reshape(m // 8, 8, n).sum(axis=0)


def _iota(shape, dim):
    return lax.broadcasted_iota(jnp.int32, shape, dim)


def _row_tile(s):
    return min(s, 512)


class _Side:
    def __init__(self, inputs, out_shape, scratch, ops):
        self.inputs, self.out_shape, self.scratch, self.ops = list(inputs), list(out_shape), list(scratch), ops


ANY_SPEC = pl.BlockSpec(memory_space=pl.ANY)


def _mm_nn(a, b, *, out_dtype, tm, tn, tk, name, a_fn=None, epi=None, extra=None):
    m, k = a.shape
    _, n = b.shape
    nk = k // tk

    def body_one(*refs):
        a_ref, b_ref = refs[0], refs[1]
        at = a_ref[...] if a_fn is None else a_fn(a_ref[...])
        r = _dot(at, b_ref[...])
        if epi is not None:
            r = epi(r, None if extra is None else refs[2][...])
        refs[-1][...] = r.astype(out_dtype)

    if nk == 1:
        in_specs = [pl.BlockSpec((tm, k), lambda i, j: (i, 0)), pl.BlockSpec((k, tn), lambda i, j: (0, j))]
        args = [a, b]
        if extra is not None:
            in_specs.append(pl.BlockSpec((tm, tn), lambda i, j: (i, j)))
            args.append(extra)
        return pl.pallas_call(
            body_one, grid=(m // tm, n // tn), in_specs=in_specs, out_specs=pl.BlockSpec((tm, tn), lambda i, j: (i, j)),
            out_shape=jax.ShapeDtypeStruct((m, n), out_dtype), name=name, compiler_params=_cp("parallel", "parallel"))(*args)

    def body(*refs):
        if extra is None:
            a_ref, b_ref, o_ref, acc = refs
            x_ref = None
        else:
            a_ref, b_ref, x_ref, o_ref, acc = refs
        kk = pl.program_id(2)

        @pl.when(kk == 0)
        def _():
            acc[...] = jnp.zeros_like(acc)

        at = a_ref[...]
        if a_fn is not None:
            at = a_fn(at)
        acc[...] += _dot(at, b_ref[...])

        @pl.when(kk == nk - 1)
        def _():
            r = acc[...]
            if epi is not None:
                r = epi(r, None if x_ref is None else x_ref[...])
            o_ref[...] = r.astype(out_dtype)

    in_specs = [pl.BlockSpec((tm, tk), lambda i, j, kk: (i, kk)), pl.BlockSpec((tk, tn), lambda i, j, kk: (kk, j))]
    args = [a, b]
    if extra is not None:
        in_specs.append(pl.BlockSpec((tm, tn), lambda i, j, kk: (i, j)))
        args.append(extra)
    return pl.pallas_call(
        body, grid=(m // tm, n // tn, nk), in_specs=in_specs,
        out_specs=pl.BlockSpec((tm, tn), lambda i, j, kk: (i, j)),
        out_shape=jax.ShapeDtypeStruct((m, n), out_dtype),
        scratch_shapes=[pltpu.VMEM((tm, tn), F32)], name=name,
        compiler_params=_cp("parallel", "parallel", "arbitrary"))(*args)


def _mm_tn(a, b, *, tm, tn, ts, name, a_fn=None):
    s, m = a.shape
    _, n = b.shape
    ns = s // ts

    def body(a_ref, b_ref, o_ref, acc):
        kk = pl.program_id(2)

        @pl.when(kk == 0)
        def _():
            acc[...] = jnp.zeros_like(acc)

        at = a_ref[...]
        if a_fn is not None:
            at = a_fn(at)
        acc[...] += _dot_tn(at, b_ref[...])

        @pl.when(kk == ns - 1)
        def _():
            o_ref[...] = acc[...]

    return pl.pallas_call(
        body, grid=(m // tm, n // tn, ns),
        in_specs=[pl.BlockSpec((ts, tm), lambda i, j, kk: (kk, i)), pl.BlockSpec((ts, tn), lambda i, j, kk: (kk, j))],
        out_specs=pl.BlockSpec((tm, tn), lambda i, j, kk: (i, j)),
        out_shape=jax.ShapeDtypeStruct((m, n), F32),
        scratch_shapes=[pltpu.VMEM((tm, tn), F32)], name=name,
        compiler_params=_cp("parallel", "parallel", "arbitrary"))(a, b)


def _mm_tn_cat(a, bs, *, ts, name):
    s, m = a.shape
    n = sum(b.shape[1] for b in bs)
    ns = s // ts
    nb = len(bs)

    def body(*refs):
        a_ref, b_refs, o_ref, acc = refs[0], refs[1:1 + nb], refs[1 + nb], refs[2 + nb]
        kk = pl.program_id(0)

        @pl.when(kk == 0)
        def _():
            acc[...] = jnp.zeros_like(acc)

        bt = b_refs[0][...] if nb == 1 else jnp.concatenate([r[...] for r in b_refs], axis=1)
        acc[...] += _dot_tn(a_ref[...], bt)

        @pl.when(kk == ns - 1)
        def _():
            o_ref[...] = acc[...]

    return pl.pallas_call(
        body, grid=(ns,),
        in_specs=[pl.BlockSpec((ts, m), lambda kk: (kk, 0))] + [pl.BlockSpec((ts, b.shape[1]), lambda kk: (kk, 0)) for b in bs],
        out_specs=pl.BlockSpec((m, n), lambda kk: (0, 0)), out_shape=jax.ShapeDtypeStruct((m, n), F32),
        scratch_shapes=[pltpu.VMEM((m, n), F32)], name=name, compiler_params=_cp("arbitrary"))(a, *bs)


def _proj(u, w_main, w_e, side=None):
    s, k = u.shape
    n = w_main.shape[1]
    tm, tn = min(s, 1024), n // 4
    n_sin = 0 if side is None else len(side.inputs)
    n_sout = 0 if side is None else len(side.out_shape)

    def body(u_ref, w_ref, we_ref, *rest):
        pm_ref, pe_ref = rest[n_sin:n_sin + 2]
        i, j = pl.program_id(0), pl.program_id(1)
        if side is not None:
            start, forward, finish = side.ops(rest[:n_sin], rest[n_sin + 2:n_sin + 2 + n_sout], rest[n_sin + 2 + n_sout:])
            pl.when((i == 0) & (j == 0))(start)
        ut = u_ref[...]
        pm_ref[...] = _dot(ut, w_ref[...]).astype(BF16)

        @pl.when(j == 0)
        def _():
            pe_ref[...] = _dot(ut, we_ref[...])

        if side is not None:
            pl.when((i == (s // tm) * 5 // 8) & (j == n // tn - 1))(forward)
            pl.when((i == s // tm - 1) & (j == n // tn - 1))(finish)

    side_in = [] if side is None else side.inputs
    return pl.pallas_call(
        body, grid=(s // tm, n // tn),
        in_specs=[pl.BlockSpec((tm, k), lambda i, j: (i, 0)), pl.BlockSpec((k, tn), lambda i, j: (0, j)),
                  pl.BlockSpec((k, PE_W), lambda i, j: (0, 0))] + [ANY_SPEC] * n_sin,
        out_specs=[pl.BlockSpec((tm, tn), lambda i, j: (i, j)), pl.BlockSpec((tm, PE_W), lambda i, j: (i, 0))]
        + [ANY_SPEC] * n_sout,
        out_shape=[jax.ShapeDtypeStruct((s, n), BF16), jax.ShapeDtypeStruct((s, PE_W), F32)]
        + ([] if side is None else side.out_shape),
        scratch_shapes=[] if side is None else side.scratch,
        name="proj_main", compiler_params=_cp("arbitrary", "arbitrary"))(u, w_main, w_e, *side_in)


def _transpose(w, name):
    r, c = w.shape
    tr = min(r, 256)

    def body(w_ref, o_ref):
        o_ref[...] = w_ref[...].T

    return pl.pallas_call(body, grid=(r // tr,), in_specs=[pl.BlockSpec((tr, c), lambda i: (i, 0))],
                          out_specs=pl.BlockSpec((c, tr), lambda i: (0, i)),
                          out_shape=jax.ShapeDtypeStruct((c, r), w.dtype), name=name, compiler_params=_cp("parallel"))(w)


def _relu2_bf16(t):
    r = jnp.maximum(t.astype(F32), 0.0)
    return (r * r).astype(BF16)


def _rms_fwd(x, g, name, side=None):
    s, d = x.shape
    tm = min(s, 512)
    n_sin = 0 if side is None else len(side.inputs)
    n_sout = 0 if side is None else len(side.out_shape)

    def body(x_ref, g_ref, *rest):
        u_ref, r_ref = rest[n_sin:n_sin + 2]
        if side is not None:
            start, forward, finish = side.ops(rest[:n_sin], rest[n_sin + 2:n_sin + 2 + n_sout], rest[n_sin + 2 + n_sout:])
            pl.when(pl.program_id(0) == 0)(start)
        xv = x_ref[...]
        r = lax.rsqrt(jnp.mean(xv * xv, axis=-1, keepdims=True) + EPS)
        u_ref[...] = ((xv * r) * g_ref[...]).astype(BF16)
        r_ref[...] = r
        if side is not None:
            pl.when(pl.program_id(0) == s // tm - 1)(forward)
            pl.when(pl.program_id(0) == s // tm - 1)(finish)

    side_in = [] if side is None else side.inputs
    return pl.pallas_call(
        body, grid=(s // tm,),
        in_specs=[pl.BlockSpec((tm, d), lambda i: (i, 0)), pl.BlockSpec((1, d), lambda i: (0, 0))] + [ANY_SPEC] * n_sin,
        out_specs=[pl.BlockSpec((tm, d), lambda i: (i, 0)), pl.BlockSpec((tm, 1), lambda i: (i, 0))] + [ANY_SPEC] * n_sout,
        out_shape=[jax.ShapeDtypeStruct((s, d), BF16), jax.ShapeDtypeStruct((s, 1), F32)]
        + ([] if side is None else side.out_shape),
        scratch_shapes=[] if side is None else side.scratch,
        name=name, compiler_params=_cp("parallel" if side is None else "arbitrary"))(x, g, *side_in)


def _mm_norm_bwd(a_parts, b, xin, r, g, dres, *, name, want_bf16, side=None):
    s = a_parts[0].shape[0]
    k = b.shape[0]
    na = len(a_parts)
    offs = [sum(p.shape[1] for p in a_parts[:i]) for i in range(na)]
    assert offs[-1] + a_parts[-1].shape[1] == k
    tm = min(s, 512)
    n_out = 3 if want_bf16 else 2
    n_sin = 0 if side is None else len(side.inputs)
    n_sout = 0 if side is None else len(side.out_shape)

    def body(*refs):
        a_refs = refs[:na]
        b_ref, x_ref, r_ref, g_ref, dres_ref = refs[na:na + 5]
        rest = refs[na + 5:]
        outs = rest[n_sin:n_sin + n_out]
        dx_ref, dg_ref = outs[0], outs[-1]
        if side is not None:
            start, forward, finish = side.ops(rest[:n_sin], rest[n_sin + n_out:n_sin + n_out + n_sout], rest[n_sin + n_out + n_sout:])
            pl.when(pl.program_id(0) == 0)(start)

        @pl.when(pl.program_id(0) == 0)
        def _():
            dg_ref[...] = jnp.zeros_like(dg_ref)

        du = _dot(a_refs[0][...], b_ref[0:a_parts[0].shape[1], :])
        for a_ref, off, part in zip(a_refs[1:], offs[1:], a_parts[1:]):
            du = du + _dot(a_ref[...], b_ref[off:off + part.shape[1], :])
        xn = x_ref[...] * r_ref[...]
        dg_ref[...] += _fold8(du * xn)
        dxn = du * g_ref[...]
        dx = dres_ref[...] + r_ref[...] * (dxn - xn * jnp.mean(dxn * xn, axis=-1, keepdims=True))
        dx_ref[...] = dx
        if want_bf16:
            outs[1][...] = dx.astype(BF16)
        if side is not None:
            pl.when(pl.program_id(0) == (s // tm) * 13 // 16)(forward)
            pl.when(pl.program_id(0) == s // tm - 1)(finish)

    row = lambda i: (i, 0)
    const = lambda i: (0, 0)
    out_specs = [pl.BlockSpec((tm, D), row)]
    out_shape = [jax.ShapeDtypeStruct((s, D), F32)]
    if want_bf16:
        out_specs.append(pl.BlockSpec((tm, D), row))
        out_shape.append(jax.ShapeDtypeStruct((s, D), BF16))
    out_specs.append(pl.BlockSpec((8, D), const))
    out_shape.append(jax.ShapeDtypeStruct((8, D), F32))
    side_in = [] if side is None else side.inputs
    return pl.pallas_call(
        body, grid=(s // tm,),
        in_specs=[pl.BlockSpec((tm, p.shape[1]), row) for p in a_parts]
        + [pl.BlockSpec((k, D), const, pipeline_mode=pl.Buffered(1)),
           pl.BlockSpec((tm, D), row), pl.BlockSpec((tm, 1), row), pl.BlockSpec((1, D), const),
           pl.BlockSpec((tm, D), row)] + [ANY_SPEC] * n_sin,
        out_specs=out_specs + [ANY_SPEC] * n_sout, out_shape=out_shape + ([] if side is None else side.out_shape),
        scratch_shapes=[] if side is None else side.scratch,
        name=name, compiler_params=_cp("arbitrary"))(*a_parts, b, xin, r, g, dres, *side_in)


def _gla_consts():
    lmask = _iota((4 * CHUNK, CHUNK), 0) % CHUNK >= _iota((4 * CHUNK, CHUNK), 1)
    hmask = _iota((256, 256), 0) // CHUNK == _iota((256, 256), 1) // CHUNK
    bd = _iota((256, 512), 0) // CHUNK == _iota((256, 512), 1) // 128
    return lmask, hmask, bd


def _fold_heads(x):
    return x[0:64] + x[64:128] + x[128:192] + x[192:256]


def _gla_decays(la, b_scr, dec_scr):
    tri = (_iota((CHUNK, CHUNK), 0) >= _iota((CHUNK, CHUNK), 1)).astype(BF16)
    ones = jnp.ones((CHUNK, 128), BF16)
    for c in range(la.shape[0] // CHUNK):
        la3 = _split3(la[CHUNK * c:CHUNK * (c + 1)])
        b_scr[CHUNK * c:CHUNK * (c + 1), :] = _sum3(_dot(tri, la3), 1)
        dec_scr[c] = jnp.exp(_sum3(_dot_tn(la3, ones), 0))


def _gla_chunk(b, qc, kc):
    bl = b[CHUNK - 1:CHUNK, :]
    ep, en, ek = jnp.exp(b), jnp.exp(-b), jnp.exp(bl - b)
    return bl, ep, en, ek, qc * ep, qc * en, kc * en, kc * ep, kc * ek


def _gla_fwd(pm, pe, wau_p, b_alpha):
    s = pm.shape[0]
    t = min(s, 1024)
    nc = t // CHUNK

    def body(q_ref, k_ref, v_ref, e_ref, wau_ref, ba_ref, o_ref, st_ref, state, b_scr, dec_scr):
        @pl.when(pl.program_id(0) == 0)
        def _():
            state[...] = jnp.zeros_like(state)

        z = _dot(e_ref[...].astype(BF16), wau_ref[...]) + ba_ref[...]
        _gla_decays(_log_sigmoid(z) * (1.0 / GLA_TAU), b_scr, dec_scr)
        lmask, hmask, bd = _gla_consts()

        def chunk(c, carry):
            rows = pl.ds(pl.multiple_of(c * CHUNK, CHUNK), CHUNK)
            qc = q_ref[rows, :].astype(F32) * 0.125
            kc = k_ref[rows, :].astype(F32)
            vc = v_ref[rows, :]
            _, _, _, _, qp, qn, kn, kp, kk = _gla_chunk(b_scr[rows, :], qc, kc)
            decb = jnp.concatenate([dec_scr[c]] * 4, axis=1)
            qs = jnp.where(hmask, jnp.concatenate([qp] * 4, axis=0), 0.0).astype(BF16)
            qns = jnp.where(hmask, jnp.concatenate([qn] * 4, axis=0), 0.0).astype(BF16)
            attn = jnp.where(lmask, _dot_nt(qs, kn.astype(BF16)), _dot_nt(qns, kp.astype(BF16))).astype(BF16)
            st = state[...]
            o_intra = _fold_heads(jnp.where(bd, _dot(attn, vc), 0.0))
            o_ref[rows, :] = o_intra + _dot(qp.astype(BF16), st.astype(BF16))
            for h in range(4):
                st_ref[c, :, 128 * h:128 * (h + 1)] = st[64 * h:64 * (h + 1), 128 * h:128 * (h + 1)]
            kv = jnp.where(bd, _dot_tn(kk.astype(BF16), vc), 0.0)
            state[...] = st * decb + kv
            return carry

        lax.fori_loop(0, nc, chunk, 0)

    return pl.pallas_call(
        body, grid=(s // t,),
        in_specs=[pl.BlockSpec((t, 256), lambda i: (i, C_GQ // 256)), pl.BlockSpec((t, 256), lambda i: (i, C_GK // 256)),
                  pl.BlockSpec((t, 512), lambda i: (i, C_GV // 512)), pl.BlockSpec((t, PE_W), lambda i: (i, 0)),
                  pl.BlockSpec((PE_W, 256), lambda i: (0, 0)), pl.BlockSpec((1, 256), lambda i: (0, 0))],
        out_specs=[pl.BlockSpec((t, 512), lambda i: (i, 0)), pl.BlockSpec((nc, CHUNK, 512), lambda i: (i, 0, 0))],
        out_shape=[jax.ShapeDtypeStruct((s, 512), F32), jax.ShapeDtypeStruct((s // CHUNK, CHUNK, 512), F32)],
        scratch_shapes=[pltpu.VMEM((256, 512), F32), pltpu.VMEM((t, 256), F32), pltpu.VMEM((nc, 256, 128), F32)],
        name="gla_fwd", compiler_params=_cp("arbitrary"))(pm, pm, pm, pe, wau_p, b_alpha)


def _gla_bwd(pm, pe, wau_p, wau_pt, b_alpha, do, states):
    s = pm.shape[0]
    t = min(s, 1024)
    nc = t // CHUNK
    nb = s // t

    def body(q_ref, k_ref, v_ref, e_ref, wau_ref, waut_ref, ba_ref, do_ref, st_ref,
             dq_ref, dk_ref, dv_ref, de_ref, dwau_ref, dba_ref, gstate, b_scr, db_scr, dec_scr):
        @pl.when(pl.program_id(0) == 0)
        def _():
            gstate[...] = jnp.zeros_like(gstate)
            dwau_ref[...] = jnp.zeros_like(dwau_ref)
            dba_ref[...] = jnp.zeros_like(dba_ref)

        eb = e_ref[...].astype(BF16)
        z = _dot(eb, wau_ref[...]) + ba_ref[...]
        _gla_decays(_log_sigmoid(z) * (1.0 / GLA_TAU), b_scr, dec_scr)
        lmask, hmask, bd = _gla_consts()
        last_row = _iota((CHUNK, 256), 0) == CHUNK - 1

        def chunk(cc, carry):
            c = nc - 1 - cc
            rows = pl.ds(pl.multiple_of(c * CHUNK, CHUNK), CHUNK)
            qc = q_ref[rows, :].astype(F32) * 0.125
            kc = k_ref[rows, :].astype(F32)
            vc = v_ref[rows, :]
            dob = do_ref[rows, :]
            bl, ep, en, ek, qp, qn, kn, kp, kk = _gla_chunk(b_scr[rows, :], qc, kc)
            decb = jnp.concatenate([dec_scr[c]] * 4, axis=1)
            qs = jnp.where(hmask, jnp.concatenate([qp] * 4, axis=0), 0.0).astype(BF16)
            qns = jnp.where(hmask, jnp.concatenate([qn] * 4, axis=0), 0.0).astype(BF16)
            knb, kpb = kn.astype(BF16), kp.astype(BF16)
            attn = jnp.where(lmask, _dot_nt(qs, knb), _dot_nt(qns, kpb)).astype(BF16)
            st = jnp.where(bd, jnp.concatenate([st_ref[c]] * 4, axis=0), 0.0)
            g = gstate[...]
            gb = g.astype(BF16)
            do_s = jnp.where(bd, jnp.concatenate([dob] * 4, axis=0), jnp.zeros((), BF16))
            dattn = _dot_nt(do_s, vc)
            dv_ref[rows, :] = (_dot_tn(attn, do_s) + _dot(kk.astype(BF16), gb)).astype(BF16)
            dac = jnp.where(lmask, dattn, 0.0).astype(BF16)
            daa = jnp.where(lmask, 0.0, dattn).astype(BF16)
            dqp = _fold_heads(jnp.where(hmask, _dot(dac, knb), 0.0)) + _dot_nt(dob, st.astype(BF16))
            dqn = _fold_heads(jnp.where(hmask, _dot(daa, kpb), 0.0))
            dkn = _dot_tn(dac, qs)
            dkp = _dot_tn(daa, qns)
            dkk = _dot_nt(vc, gb)
            ddec = _dot_nt(jnp.ones((8, 1536), BF16), _split3(g * st))[0:1, :]
            gstate[...] = decb * g + jnp.where(bd, _dot_tn(qp.astype(BF16), dob), 0.0)
            dq_ref[rows, :] = ((dqp * ep + dqn * en) * 0.125).astype(BF16)
            dk_ref[rows, :] = (dkn * en + dkp * ep + dkk * ek).astype(BF16)
            dek = dkk * kc * ek
            db = (dqp * qc + dkp * kc) * ep - (dqn * qc + dkn * kc) * en - dek
            dbl = jnp.sum(dek, axis=0, keepdims=True) + ddec * jnp.exp(bl)
            db_scr[rows, :] = db + jnp.where(last_row, dbl, 0.0)
            return carry

        lax.fori_loop(0, nc, chunk, 0)
        triu = (_iota((CHUNK, CHUNK), 0) <= _iota((CHUNK, CHUNK), 1)).astype(BF16)
        dla = jnp.concatenate([_sum3(_dot(triu, _split3(db_scr[CHUNK * c:CHUNK * (c + 1), :])), 1) for c in range(nc)], axis=0)
        dz = dla * (1.0 / GLA_TAU) * _sigmoid(-z)
        dzb = dz.astype(BF16)
        dwau_ref[...] += _dot_tn(eb, dzb)
        dba_ref[...] += _fold8(dz)
        de_ref[...] = _dot(dzb, waut_ref[...])

    rev = lambda i: nb - 1 - i
    return pl.pallas_call(
        body, grid=(nb,),
        in_specs=[pl.BlockSpec((t, 256), lambda i: (rev(i), C_GQ // 256)), pl.BlockSpec((t, 256), lambda i: (rev(i), C_GK // 256)),
                  pl.BlockSpec((t, 512), lambda i: (rev(i), C_GV // 512)), pl.BlockSpec((t, PE_W), lambda i: (rev(i), 0)),
                  pl.BlockSpec((PE_W, 256), lambda i: (0, 0)), pl.BlockSpec((256, PE_W), lambda i: (0, 0)),
                  pl.BlockSpec((1, 256), lambda i: (0, 0)), pl.BlockSpec((t, 512), lambda i: (rev(i), 0)),
                  pl.BlockSpec((nc, CHUNK, 512), lambda i: (rev(i), 0, 0))],
        out_specs=[pl.BlockSpec((t, 256), lambda i: (rev(i), 0)), pl.BlockSpec((t, 256), lambda i: (rev(i), 0)),
                   pl.BlockSpec((t, 512), lambda i: (rev(i), 0)), pl.BlockSpec((t, PE_W), lambda i: (rev(i), 0)),
                   pl.BlockSpec((PE_W, 256), lambda i: (0, 0)), pl.BlockSpec((8, 256), lambda i: (0, 0))],
        out_shape=[jax.ShapeDtypeStruct((s, 256), BF16), jax.ShapeDtypeStruct((s, 256), BF16),
                   jax.ShapeDtypeStruct((s, 512), BF16), jax.ShapeDtypeStruct((s, PE_W), F32),
                   jax.ShapeDtypeStruct((PE_W, 256), F32), jax.ShapeDtypeStruct((8, 256), F32)],
        scratch_shapes=[pltpu.VMEM((256, 512), F32), pltpu.VMEM((t, 256), F32), pltpu.VMEM((t, 256), F32),
                        pltpu.VMEM((nc, 256, 128), F32)],
        name="gla_bwd", compiler_params=_cp("arbitrary"))(pm, pm, pm, pe, wau_p, wau_pt, b_alpha, do, states)


def _fcum_bwd(pe, bias, df):
    s = pe.shape[0]
    t = min(s, 512)
    nb = s // t

    def body(e_ref, b_ref, df_ref, de_ref, db_ref, carry):
        @pl.when(pl.program_id(0) == 0)
        def _():
            carry[...] = jnp.zeros_like(carry)
            db_ref[...] = jnp.zeros_like(db_ref)

        triu = (_iota((t, t), 0) <= _iota((t, t), 1)).astype(BF16)
        dlf = _sum3(_dot(triu, _split3(df_ref[...])), 1) + carry[0:1, :]
        carry[...] = jnp.broadcast_to(dlf[0:1, :], carry.shape)
        lane = _iota((t, PE_W), 1)
        dff = jnp.where((lane >= FF_LANE) & (lane < FF_LANE + 8), dlf * _sigmoid(-(e_ref[...] + b_ref[...])), 0.0)
        de_ref[...] = dff
        db_ref[...] += _fold8(dff)

    rev = lambda i: (nb - 1 - i, 0)
    return pl.pallas_call(
        body, grid=(nb,),
        in_specs=[pl.BlockSpec((t, PE_W), rev), pl.BlockSpec((1, PE_W), lambda i: (0, 0)), pl.BlockSpec((t, PE_W), rev)],
        out_specs=[pl.BlockSpec((t, PE_W), rev), pl.BlockSpec((8, PE_W), lambda i: (0, 0))],
        out_shape=[jax.ShapeDtypeStruct((s, PE_W), F32), jax.ShapeDtypeStruct((8, PE_W), F32)],
        scratch_shapes=[pltpu.VMEM((8, PE_W), F32)],
        name="fcum_bwd", compiler_params=_cp("arbitrary"))(pe, bias, df)


FOX_WIDE = 1024


def _split3(x):
    hi = x.astype(BF16)
    r = x - hi.astype(F32)
    mid = r.astype(BF16)
    lo = (r - mid.astype(F32)).astype(BF16)
    return jnp.concatenate([hi, mid, lo], axis=1)


def _pieces(x, lane0):
    lane = _iota(x.shape, 1)
    x = jnp.where((lane >= lane0) & (lane < lane0 + 8), x, 0.0)
    hi = x.astype(BF16).astype(F32)
    r = x - hi
    mid = r.astype(BF16).astype(F32)
    lo = (r - mid).astype(BF16).astype(F32)
    return (hi + pltpu.roll(mid, 8, 1) + pltpu.roll(lo, 16, 1)).astype(BF16)


def _sum3(x, axis):
    n = x.shape[axis] // 3
    parts = [lax.slice_in_dim(x, n * p, n * (p + 1), axis=axis) for p in range(3)]
    return (parts[0] + parts[1]) + parts[2]


def _spread(x, sp):
    return jnp.concatenate([_dot(x[:, 128 * g:128 * (g + 1)], sp) for g in range(4)], axis=1)


def _fox_tables():
    heads, lane = np.arange(8), np.arange(64)
    spread = np.zeros((128, 256), np.float32)
    spread[(64 * heads[:2, None] + lane).ravel(), (128 * heads[:2, None] + lane).ravel()] = 1.0
    def place(src_lane0, dst_off, val):
        t = np.zeros((128, 1024), np.float32)
        for p in range(3):
            for src in {src_lane0 + 8 * p, (src_lane0 - 8 * p) % 128}:
                t[src + heads, 128 * heads + dst_off + p] = val
        return t
    def const(off, val):
        c = np.zeros((1, 1024), np.float32)
        for p in range(3):
            c[0, 128 * heads + off + p] = val
        return c
    rows = np.zeros((8, 128), np.float32)
    rows[heads, FF_LANE + heads] = 1.0
    bf = lambda a: jnp.asarray(a, BF16)
    return dict(spread=bf(spread),
                f_to_q=bf(place(FF_LANE, 64, 1.0)), f_to_k=bf(place(FF_LANE, 67, -1.0)), d_to_do=bf(place(0, 64, 1.0)),
                ones_q=jnp.asarray(const(67, 1.0)), ones_k=jnp.asarray(const(64, 1.0)), ones_v=jnp.asarray(const(64, -1.0)),
                rows=jnp.asarray(rows))


LOG2E = 1.4426950408889634


def _fox_prep(pm, f128, lse8, tb, *, backward):
    s = pm.shape[0]
    tm = min(s, 1024) if backward else _row_tile(s)

    def body(*refs):
        if backward:
            q_ref, f_ref, lse_ref, sp_ref, fq_ref, cq_ref, rows_ref, qa_ref = refs
            f = f_ref[...] * LOG2E - _dot_tn(lse_ref[...], rows_ref[...], precision=HI)
            q2 = (q_ref[...].astype(F32) * (0.125 * LOG2E)).astype(BF16)
            qa_ref[...] = (_spread(q2, sp_ref[...]) + _dot(_pieces(f, FF_LANE), fq_ref[...]) + cq_ref[...]).astype(BF16)
            return
        (q_ref, k_ref, v_ref, e_ref, b_ref, sp_ref, fq_ref, fk_ref, cq_ref, ck_ref, cv_ref,
         qa_ref, ka_ref, va_ref, vt_ref, qt_ref, kt_ref, f_ref, carry) = refs

        @pl.when(pl.program_id(0) == 0)
        def _():
            carry[...] = jnp.zeros_like(carry)

        lf = _log_sigmoid(e_ref[...] + b_ref[...])
        tri = (_iota((tm, tm), 0) >= _iota((tm, tm), 1)).astype(BF16)
        f = _sum3(_dot(tri, _split3(lf)), 1) + carry[0:1, :]
        f_ref[...] = f
        carry[...] = jnp.broadcast_to(f[tm - 1:tm, :], carry.shape)
        f3 = _pieces(f * LOG2E, FF_LANE)
        q, k, v = q_ref[...].astype(F32), k_ref[...], v_ref[...]
        sp = sp_ref[...]
        qa_ref[...] = (_spread((q * (0.125 * LOG2E)).astype(BF16), sp) + _dot(f3, fq_ref[...]) + cq_ref[...]).astype(BF16)
        ka_ref[...] = (_spread(k, sp) + _dot(f3, fk_ref[...]) + ck_ref[...]).astype(BF16)
        va_ref[...] = (_spread(v, sp) + cv_ref[...]).astype(BF16)
        vt_ref[...] = v.T
        qt_ref[...] = (q * 0.125).astype(BF16).T
        kt_ref[...] = (k.astype(F32) * 0.125).astype(BF16).T

    row = lambda i: (i, 0)
    const = lambda i: (0, 0)
    blk = lambda c: pl.BlockSpec((tm, 512), lambda i: (i, c // 512))
    wide = pl.BlockSpec((tm, 1024), row)
    mat = lambda a: pl.BlockSpec(a.shape, const)
    if backward:
        ins = [pm, f128, lse8, tb["spread"], tb["f_to_q"], tb["ones_q"], tb["rows"]]
        in_specs = [blk(C_FQ), pl.BlockSpec((tm, 128), row), pl.BlockSpec((8, tm), lambda i: (0, i))] + [mat(a) for a in ins[3:]]
        out_specs, out_shape = wide, jax.ShapeDtypeStruct((s, 1024), BF16)
    else:
        pe, bias = f128
        ins = [pm, pm, pm, pe, bias, tb["spread"], tb["f_to_q"], tb["f_to_k"], tb["ones_q"], tb["ones_k"], tb["ones_v"]]
        in_specs = [blk(C_FQ), blk(C_FK), blk(C_FV), pl.BlockSpec((tm, 128), row)] + [mat(a) for a in ins[4:]]
        tr = pl.BlockSpec((512, tm), lambda i: (0, i))
        out_specs = [wide, wide, wide, tr, tr, tr, pl.BlockSpec((tm, 128), row)]
        out_shape = ([jax.ShapeDtypeStruct((s, 1024), BF16)] * 3 + [jax.ShapeDtypeStruct((512, s), BF16)] * 3
                     + [jax.ShapeDtypeStruct((s, 128), F32)])
    return pl.pallas_call(body, grid=(s // tm,), in_specs=in_specs, out_specs=out_specs, out_shape=out_shape,
                          scratch_shapes=[] if backward else [pltpu.VMEM((8, 128), F32)],
                          name="fox_prep_bwd" if backward else "fox_prep",
                          compiler_params=_cp("parallel" if backward else "arbitrary"))(*ins)


def _fox_post(dqt, dkt, dvt, rowsum8, colsum8, tb):
    s = dqt.shape[1]
    tm = min(s, 1024)

    def body(dqt_ref, dkt_ref, dvt_ref, rs_ref, cs_ref, rows_ref, dfq_ref, dfk_ref, dfv_ref, df_ref):
        dfq_ref[...] = dqt_ref[...].T.astype(BF16)
        dfk_ref[...] = dkt_ref[...].T
        dfv_ref[...] = dvt_ref[...].T
        df_ref[...] = _dot_tn(rs_ref[...] - cs_ref[...], rows_ref[...], precision=HI)

    row = lambda i: (i, 0)
    tr = pl.BlockSpec((512, tm), lambda i: (0, i))
    out = pl.BlockSpec((tm, 512), row)
    heads = pl.BlockSpec((8, tm), lambda i: (0, i))
    return pl.pallas_call(
        body, grid=(s // tm,),
        in_specs=[tr, tr, tr, heads, heads, pl.BlockSpec((8, 128), lambda i: (0, 0))],
        out_specs=[out, out, out, pl.BlockSpec((tm, 128), row)],
        out_shape=[jax.ShapeDtypeStruct((s, 512), BF16)] * 3 + [jax.ShapeDtypeStruct((s, 128), F32)],
        name="fox_post", compiler_params=_cp("parallel"))(dqt, dkt, dvt, rowsum8, colsum8, tb["rows"])


def _fox_fwd(k_aug, q_aug, vt):
    s = k_aug.shape[0]
    nh = 8
    tk = _row_tile(s)
    tq = min(s, 2 * FOX_WIDE)
    per = tq // tk

    def body(k_ref, q_ref, v_ref, o_ref, lse_ref, sbuf):
        i = pl.program_id(1)
        qa = q_ref[...]

        def scores(j):
            return _dot_nt(k_ref[pl.ds(pl.multiple_of(j * tk, tk), tk), :], qa)

        ones_row = (_iota((16, tk), 0) == 0).astype(BF16)

        def update(st, j, carry):
            m, acc = carry
            m2 = jnp.maximum(m, jnp.max(st, axis=0, keepdims=True))
            p = jnp.exp2(st - m2)
            vj = jnp.concatenate([v_ref[:, pl.ds(pl.multiple_of(j * tk, tk), tk)], ones_row], axis=0)
            return m2, jnp.exp2(m - m2) * acc + _dot(vj, p.astype(BF16))

        def step(a, carry):
            sbuf[1] = scores(2 * a + 1)
            carry = update(sbuf[0], 2 * a, carry)
            sbuf[0] = scores(2 * a + 2)
            return update(sbuf[1], 2 * a + 1, carry)

        n = i * per
        sbuf[0] = scores(0)
        carry = (jnp.full((1, tq), -1e30, F32), jnp.zeros((80, tq), F32))
        carry = lax.fori_loop(0, n // 2, step, carry)
        tri = _iota((tk, tk), 0) <= _iota((tk, tk), 1)
        late = [_dot_nt(k_ref[pl.ds(pl.multiple_of((n + r) * tk, tk), tk), :], qa[r * tk:, :]) for r in range(1, per)]
        for r in range(per):
            st = sbuf[0] if r == 0 else late[r - 1]
            head = jnp.where(tri, st[:, :tk], -1e30)
            st = head if st.shape[1] == tk else jnp.concatenate([head, st[:, tk:]], axis=1)
            part = update(st, n + r, tuple(c[:, r * tk:] for c in carry))
            carry = part if r == 0 else tuple(jnp.concatenate([old[:, :r * tk], new], axis=1) for old, new in zip(carry, part))
        m, acc = carry
        l = acc[64:65]
        o_ref[...] = (acc[0:64] / l).astype(BF16)
        lse_ref[0] = m + jnp.log2(l)

    return pl.pallas_call(
        body, grid=(nh, s // tq),
        in_specs=[pl.BlockSpec((s, 128), lambda h, i: (0, h)), pl.BlockSpec((tq, 128), lambda h, i: (i, h)),
                  pl.BlockSpec((64, s), lambda h, i: (h, 0))],
        out_specs=[pl.BlockSpec((64, tq), lambda h, i: (h, i)), pl.BlockSpec((1, 1, tq), lambda h, i: (h, 0, i))],
        out_shape=[jax.ShapeDtypeStruct((512, s), BF16), jax.ShapeDtypeStruct((nh, 1, s), F32)],
        scratch_shapes=[pltpu.VMEM((2, tk, tq), F32)],
        name="fox_fwd", compiler_params=_cp("parallel", "arbitrary"))(k_aug, q_aug, vt)


def _fox_bwd(q_aug, do_aug, qt, dot_, k_aug, v_aug, kt):
    s = q_aug.shape[0]
    nh = 8
    tq = _row_tile(s)
    tk = min(s, 2 * FOX_WIDE)
    per = tk // tq
    nqb = s // tq

    def body(qa_ref, da_ref, qt_ref, dt_ref, ka_ref, va_ref, kt_ref, dq_ref, rs_ref, dk_ref, dv_ref, dfk_ref):
        j = pl.program_id(1)

        @pl.when(j == 0)
        def _():
            dq_ref[...] = jnp.zeros_like(dq_ref)
            rs_ref[...] = jnp.zeros_like(rs_ref)

        ones_row = (_iota((16, tk), 0) == 0).astype(BF16)
        ka, va = ka_ref[...], va_ref[...]
        ks = jnp.concatenate([kt_ref[...], ones_row], axis=0)
        tri = _iota((tq, tq), 0) >= _iota((tq, tq), 1)

        def tile(i, w, carry):
            masked = w is not None
            w = tk if w is None else w
            rows = pl.ds(pl.multiple_of(i * tq, tq), tq)
            sp = _dot_nt(qa_ref[rows, :], ka[:w])
            if masked:
                last = jnp.where(tri, sp[:, w - tq:], -1e30)
                sp = last if w == tq else jnp.concatenate([sp[:, :w - tq], last], axis=1)
            p = jnp.exp2(sp)
            dsb = (p * _dot_nt(da_ref[rows, :], va[:w])).astype(BF16)
            dq = _dot_nt(ks[:, :w], dsb)
            dq_ref[:, rows] += dq[0:64]
            rs_ref[0, :, rows] += dq[64:72]
            new = (_dot(jnp.concatenate([qt_ref[:, rows], ones_row[:, :tq]], axis=0), dsb), _dot(dt_ref[:, rows], p.astype(BF16)))
            if w == tk:
                return tuple(c + d for c, d in zip(carry, new))
            return tuple(jnp.concatenate([c[:, :w] + d, c[:, w:]], axis=1) for c, d in zip(carry, new))

        carry = (jnp.zeros((80, tk), F32), jnp.zeros((64, tk), F32))
        for r in range(per):
            carry = tile(j * per + r, (r + 1) * tq, carry)
        dk, dv = lax.fori_loop((j + 1) * per, nqb, lambda i, c: tile(i, None, c), carry)
        dk_ref[...] = dk[0:64].astype(BF16)
        dv_ref[...] = dv.astype(BF16)
        dfk_ref[0] = dk[64:65]

    head_cols = lambda h, j: (0, h)
    head_rows = lambda h, j: (h, 0)
    once = dict(pipeline_mode=pl.Buffered(1))
    return pl.pallas_call(
        body, grid=(nh, s // tk),
        in_specs=[pl.BlockSpec((s, 128), head_cols, **once), pl.BlockSpec((s, 128), head_cols, **once),
                  pl.BlockSpec((64, s), head_rows, **once), pl.BlockSpec((64, s), head_rows, **once),
                  pl.BlockSpec((tk, 128), lambda h, j: (j, h)), pl.BlockSpec((tk, 128), lambda h, j: (j, h)),
                  pl.BlockSpec((64, tk), lambda h, j: (h, j))],
        out_specs=[pl.BlockSpec((64, s), head_rows), pl.BlockSpec((1, 8, s), lambda h, j: (h, 0, 0)),
                   pl.BlockSpec((64, tk), lambda h, j: (h, j)),
                   pl.BlockSpec((64, tk), lambda h, j: (h, j)), pl.BlockSpec((1, 1, tk), lambda h, j: (h, 0, j))],
        out_shape=[jax.ShapeDtypeStruct((512, s), F32), jax.ShapeDtypeStruct((nh, 8, s), F32),
                   jax.ShapeDtypeStruct((512, s), BF16),
                   jax.ShapeDtypeStruct((512, s), BF16), jax.ShapeDtypeStruct((nh, 1, s), F32)],
        name="fox_bwd", compiler_params=_cp("parallel", "arbitrary"))(q_aug, do_aug, qt, dot_, k_aug, v_aug, kt)


MEM_SCALE = 128 ** -0.5


def _mem_attn_fwd(pm, mkv):
    s = pm.shape[0]
    t = min(s, 1024)
    nm = mkv.shape[0]

    def body(q_ref, mk_ref, mv_ref, o_ref):
        for h in range(4):
            cols = slice(128 * h, 128 * (h + 1))
            sc = _dot_nt(q_ref[:, cols], mk_ref[:, cols]) * MEM_SCALE
            p = jnp.exp(sc - jnp.max(sc, axis=-1, keepdims=True))
            p = p / jnp.sum(p, axis=-1, keepdims=True)
            o_ref[:, cols] = _dot(p.astype(BF16), mv_ref[:, cols]).astype(BF16)

    return pl.pallas_call(
        body, grid=(s // t,),
        in_specs=[pl.BlockSpec((t, 512), lambda i: (i, C_MQ // 512)), pl.BlockSpec((nm, 512), lambda i: (0, 0)),
                  pl.BlockSpec((nm, 512), lambda i: (0, 1))],
        out_specs=pl.BlockSpec((t, 512), lambda i: (i, 0)),
        out_shape=jax.ShapeDtypeStruct((s, 512), BF16),
        name="mem_attn_fwd", compiler_params=_cp("parallel"))(pm, mkv, mkv)


def _mem_attn_bwd(pm, mkv, do):
    s = pm.shape[0]
    t = min(s, 1024)
    nm = mkv.shape[0]

    def body(q_ref, mk_ref, mv_ref, do_ref, dq_ref, dmk_ref, dmv_ref):
        @pl.when(pl.program_id(0) == 0)
        def _():
            dmk_ref[...] = jnp.zeros_like(dmk_ref)
            dmv_ref[...] = jnp.zeros_like(dmv_ref)

        for h in range(4):
            cols = slice(128 * h, 128 * (h + 1))
            qh, kh, vh, doh = q_ref[:, cols], mk_ref[:, cols], mv_ref[:, cols], do_ref[:, cols]
            sc = _dot_nt(qh, kh) * MEM_SCALE
            p = jnp.exp(sc - jnp.max(sc, axis=-1, keepdims=True))
            p = p / jnp.sum(p, axis=-1, keepdims=True)
            pb = p.astype(BF16)
            dp = _dot_nt(doh, vh)
            ds = (p * (dp - jnp.sum(p * dp, axis=-1, keepdims=True)) * MEM_SCALE).astype(BF16)
            dq_ref[:, cols] = _dot(ds, kh).astype(BF16)
            dmk_ref[:, cols] += _dot_tn(ds, qh)
            dmv_ref[:, cols] += _dot_tn(pb, doh)

    return pl.pallas_call(
        body, grid=(s // t,),
        in_specs=[pl.BlockSpec((t, 512), lambda i: (i, C_MQ // 512)), pl.BlockSpec((nm, 512), lambda i: (0, 0)),
                  pl.BlockSpec((nm, 512), lambda i: (0, 1)), pl.BlockSpec((t, 512), lambda i: (i, 0))],
        out_specs=[pl.BlockSpec((t, 512), lambda i: (i, 0)), pl.BlockSpec((nm, 512), lambda i: (0, 0)),
                   pl.BlockSpec((nm, 512), lambda i: (0, 0))],
        out_shape=[jax.ShapeDtypeStruct((s, 512), BF16), jax.ShapeDtypeStruct((nm, 512), F32),
                   jax.ShapeDtypeStruct((nm, 512), F32)],
        name="mem_attn_bwd", compiler_params=_cp("arbitrary"))(pm, mkv, mkv, do)


def _gain_grad(dxn_g, x, r, name):
    m, d = x.shape

    def body(d_ref, x_ref, r_ref, o_ref):
        o_ref[...] = _fold8(d_ref[...] * (x_ref[...] * r_ref[...]))

    return pl.pallas_call(body, out_shape=jax.ShapeDtypeStruct((8, d), F32), name=name,
                          compiler_params=pltpu.CompilerParams(vmem_limit_bytes=VMEM_LIMIT_BYTES))(dxn_g, x, r)


def _head_norm(o, gh):
    xs, rs = [], []
    for h in range(4):
        oh = o[:, 128 * h:128 * (h + 1)]
        r = lax.rsqrt(jnp.mean(oh * oh, axis=-1, keepdims=True) + EPS)
        xs.append(oh * r)
        rs.append(r)
    return xs, rs


def _merge_fwd(x, pm, o_gla, o_fox_t, o_mem, g_head, wg, wf, wm, wo, g_ffn):
    s = x.shape[0]
    t = min(s, 512)

    def body(x_ref, g0_ref, g1_ref, g2_ref, gg_ref, og_ref, of_ref, om_ref, gh_ref, wg_ref, wf_ref, wm_ref, wo_ref, gf_ref,
             mg_ref, h1_ref, u2_ref, r2_ref):
        xs, _ = _head_norm(og_ref[...], None)
        gg = gg_ref[...].astype(F32)
        sil = gg * _sigmoid(gg)
        ogn = jnp.concatenate(xs, axis=1) * gh_ref[...] * sil
        merged = (_sigmoid(g0_ref[...].astype(F32)) * _dot(ogn.astype(BF16), wg_ref[...])
                  + _sigmoid(g1_ref[...].astype(F32)) * _dot(of_ref[...].T, wf_ref[...])
                  + _sigmoid(g2_ref[...].astype(F32)) * _dot(om_ref[...], wm_ref[...]))
        mb = merged.astype(BF16)
        mg_ref[...] = mb
        h1 = x_ref[...] + _dot(mb, wo_ref[...])
        h1_ref[...] = h1
        r = lax.rsqrt(jnp.mean(h1 * h1, axis=-1, keepdims=True) + EPS)
        u2_ref[...] = ((h1 * r) * gf_ref[...]).astype(BF16)
        r2_ref[...] = r

    row = lambda i: (i, 0)
    const = lambda i: (0, 0)
    return pl.pallas_call(
        body, grid=(s // t,),
        in_specs=[pl.BlockSpec((t, D), row), pl.BlockSpec((t, D), lambda i: (i, 0)), pl.BlockSpec((t, D), lambda i: (i, 1)),
                  pl.BlockSpec((t, D), lambda i: (i, 2)), pl.BlockSpec((t, 512), lambda i: (i, C_GG // 512)),
                  pl.BlockSpec((t, 512), row), pl.BlockSpec((512, t), lambda i: (0, i)), pl.BlockSpec((t, 512), row),
                  pl.BlockSpec((1, 512), const), pl.BlockSpec((512, D), const), pl.BlockSpec((512, D), const),
                  pl.BlockSpec((512, D), const), pl.BlockSpec((D, D), const), pl.BlockSpec((1, D), const)],
        out_specs=[pl.BlockSpec((t, D), row), pl.BlockSpec((t, D), row), pl.BlockSpec((t, D), row), pl.BlockSpec((t, 1), row)],
        out_shape=[jax.ShapeDtypeStruct((s, D), BF16), jax.ShapeDtypeStruct((s, D), F32),
                   jax.ShapeDtypeStruct((s, D), BF16), jax.ShapeDtypeStruct((s, 1), F32)],
        name="merge_fwd", compiler_params=_cp("parallel"))(x, pm, pm, pm, pm, o_gla, o_fox_t, o_mem, g_head, wg, wf, wm, wo, g_ffn)


def _merge_bwd(dh1b, pm, o_gla, o_fox_t, o_mem, g_head, wg, wf, wm, wgt, wft, wmt, wot, spread, d_to_do):
    s = dh1b.shape[0]
    t = min(s, 256)

    def body(dh_ref, g0_ref, g1_ref, g2_ref, gg_ref, og_ref, of_ref, om_ref, gh_ref, wg_ref, wf_ref, wm_ref,
             wgt_ref, wft_ref, wmt_ref, wot_ref, sp_ref, dd_ref,
             dgt_ref, dgg_ref, dog_ref, da_ref, dot_ref, dom_ref, dwg_ref, dwf_ref, dwm_ref, dgh_ref):
        @pl.when(pl.program_id(0) == 0)
        def _():
            dwg_ref[...] = jnp.zeros_like(dwg_ref)
            dwf_ref[...] = jnp.zeros_like(dwf_ref)
            dwm_ref[...] = jnp.zeros_like(dwm_ref)
            dgh_ref[...] = jnp.zeros_like(dgh_ref)

        dmerged = _dot(dh_ref[...], wot_ref[...])
        og = og_ref[...]
        xs, rs = _head_norm(og, None)
        on = jnp.concatenate(xs, axis=1)
        gg = gg_ref[...].astype(F32)
        sg = _sigmoid(gg)
        sil = gg * sg
        gh = gh_ref[...]
        ognb = (on * gh * sil).astype(BF16)
        ofb, omb = of_ref[...].T, om_ref[...]
        douts = []
        for idx, (gref, ob, w_ref, wt_ref, dw_ref) in enumerate((
                (g0_ref, ognb, wg_ref, wgt_ref, dwg_ref), (g1_ref, ofb, wf_ref, wft_ref, dwf_ref),
                (g2_ref, omb, wm_ref, wmt_ref, dwm_ref))):
            gt = _sigmoid(gref[...].astype(F32))
            y = _dot(ob, w_ref[...])
            dgt_ref[:, D * idx:D * (idx + 1)] = (dmerged * y * gt * (1.0 - gt)).astype(BF16)
            dy = (gt * dmerged).astype(BF16)
            dw_ref[...] += _dot_tn(ob, dy)
            douts.append(_dot(dy, wt_ref[...]))
        dogn, dof, dom = douts
        dofb = dof.astype(BF16)
        dom_ref[...] = dom.astype(BF16)
        ind = (_iota((1536, 128), 0) % 512 // 64 == _iota((1536, 128), 1)).astype(BF16)
        delta = _dot(_split3(dofb.astype(F32) * ofb.astype(F32)), ind)
        da_ref[...] = (_spread(dofb, sp_ref[...]) + _dot(_pieces(delta, 0), dd_ref[...])).astype(BF16)
        dot_ref[...] = dofb.T
        dgg_ref[...] = (dogn * on * gh * (sg * (1.0 + gg * (1.0 - sg)))).astype(BF16)
        d_on = dogn * sil
        dgh_ref[...] += _fold8(d_on * on)
        dxn = d_on * gh
        outs = []
        for h in range(4):
            cols = slice(128 * h, 128 * (h + 1))
            dh_, xh = dxn[:, cols], xs[h]
            outs.append(rs[h] * (dh_ - xh * jnp.mean(dh_ * xh, axis=-1, keepdims=True)))
        dog_ref[...] = jnp.concatenate(outs, axis=1).astype(BF16)

    row = lambda i: (i, 0)
    const = lambda i: (0, 0)
    return pl.pallas_call(
        body, grid=(s // t,),
        in_specs=[pl.BlockSpec((t, D), row), pl.BlockSpec((t, D), lambda i: (i, 0)), pl.BlockSpec((t, D), lambda i: (i, 1)),
                  pl.BlockSpec((t, D), lambda i: (i, 2)), pl.BlockSpec((t, 512), lambda i: (i, C_GG // 512)),
                  pl.BlockSpec((t, 512), row), pl.BlockSpec((512, t), lambda i: (0, i)), pl.BlockSpec((t, 512), row),
                  pl.BlockSpec((1, 512), const), pl.BlockSpec((512, D), const), pl.BlockSpec((512, D), const),
                  pl.BlockSpec((512, D), const), pl.BlockSpec((D, 512), const), pl.BlockSpec((D, 512), const),
                  pl.BlockSpec((D, 512), const), pl.BlockSpec((D, D), const),
                  pl.BlockSpec((128, 256), const), pl.BlockSpec((128, 1024), const)],
        out_specs=[pl.BlockSpec((t, 3 * D), row), pl.BlockSpec((t, 512), row), pl.BlockSpec((t, 512), row),
                   pl.BlockSpec((t, 1024), row), pl.BlockSpec((512, t), lambda i: (0, i)), pl.BlockSpec((t, 512), row),
                   pl.BlockSpec((512, D), const), pl.BlockSpec((512, D), const), pl.BlockSpec((512, D), const),
                   pl.BlockSpec((8, 512), const)],
        out_shape=[jax.ShapeDtypeStruct((s, 3 * D), BF16), jax.ShapeDtypeStruct((s, 512), BF16),
                   jax.ShapeDtypeStruct((s, 512), BF16), jax.ShapeDtypeStruct((s, 1024), BF16),
                   jax.ShapeDtypeStruct((512, s), BF16), jax.ShapeDtypeStruct((s, 512), BF16),
                   jax.ShapeDtypeStruct((512, D), F32), jax.ShapeDtypeStruct((512, D), F32),
                   jax.ShapeDtypeStruct((512, D), F32), jax.ShapeDtypeStruct((8, 512), F32)],
        name="merge_bwd", compiler_params=_cp("arbitrary"))(
            dh1b, pm, pm, pm, pm, o_gla, o_fox_t, o_mem, g_head, wg, wf, wm, wgt, wft, wmt, wot, spread, d_to_do)


def _ff2_loss(a, w2, h1, g_final, target):
    s, k = a.shape
    tm = min(s, 512)

    def body(a_ref, w_ref, h1_ref, g_ref, t_ref, dh_ref, dhb_ref, loss_ref, dg_ref):
        @pl.when(pl.program_id(0) == 0)
        def _():
            loss_ref[...] = jnp.zeros_like(loss_ref)
            dg_ref[...] = jnp.zeros_like(dg_ref)

        h2 = h1_ref[...] + _dot(_relu2_bf16(a_ref[...]), w_ref[...])
        r = lax.rsqrt(jnp.mean(h2 * h2, axis=-1, keepdims=True) + EPS)
        xn = h2 * r
        g = g_ref[...]
        err = xn * g - t_ref[...]
        e2 = _fold8(err * err)
        part = e2[:, 0:128]
        for c in range(1, D // 128):
            part = part + e2[:, 128 * c:128 * (c + 1)]
        loss_ref[...] += part
        dy = err * (1.0 / D)
        dg_ref[...] += _fold8(dy * xn)
        dxn = dy * g
        dh = r * (dxn - xn * jnp.mean(dxn * xn, axis=-1, keepdims=True))
        dh_ref[...] = dh
        dhb_ref[...] = dh.astype(BF16)

    row = lambda i: (i, 0)
    const = lambda i: (0, 0)
    return pl.pallas_call(
        body, grid=(s // tm,),
        in_specs=[pl.BlockSpec((tm, k), row), pl.BlockSpec((k, D), const, pipeline_mode=pl.Buffered(1)),
                  pl.BlockSpec((tm, D), row), pl.BlockSpec((1, D), const), pl.BlockSpec((tm, D), row)],
        out_specs=[pl.BlockSpec((tm, D), row), pl.BlockSpec((tm, D), row), pl.BlockSpec((8, 128), const),
                   pl.BlockSpec((8, D), const)],
        out_shape=[jax.ShapeDtypeStruct((s, D), F32), jax.ShapeDtypeStruct((s, D), BF16),
                   jax.ShapeDtypeStruct((8, 128), F32), jax.ShapeDtypeStruct((8, D), F32)],
        name="ff2_loss", compiler_params=_cp("arbitrary"))(a, w2, h1, g_final, target)


def _adam(w, g, m, v, name):
    _, r, c = w.shape
    tr = r
    for cand in (512, 256, 128, 64, 32, 16, 8):
        if r % cand == 0 and cand * c * 4 <= (1 << 20):
            tr = cand
            break
    c1 = 1.0 - ADAM_B1 ** ADAM_STEP
    c2 = 1.0 - ADAM_B2 ** ADAM_STEP

    def body(w_ref, g_ref, m_ref, v_ref, d_ref, nm_ref, nv_ref):
        gv = g_ref[...]
        nm = ADAM_B1 * m_ref[...] + (1.0 - ADAM_B1) * gv
        nv = ADAM_B2 * v_ref[...] + (1.0 - ADAM_B2) * (gv * gv)
        d_ref[...] = -ADAM_LR * ((nm / c1) / (jnp.sqrt(nv / c2) + ADAM_EPS) + ADAM_WD * w_ref[...])
        nm_ref[...] = nm
        nv_ref[...] = nv

    spec = pl.BlockSpec((1, tr, c), lambda i: (0, i, 0))
    return pl.pallas_call(
        body, grid=(r // tr,), in_specs=[spec] * 4, out_specs=[spec] * 3,
        out_shape=[jax.ShapeDtypeStruct((1, r, c), F32)] * 3, name=name, compiler_params=_cp("parallel"))(w, g, m, v)


def _row_block(r):
    return max(d for d in range(16, 513, 16) if r % d == 0)


def _add_half(core, a, b, name):
    n, r, c = b.shape
    tr = _row_block(r)

    def body(core_ref, a_ref, b_ref, o_ref):
        o_ref[...] = (a_ref[...].astype(F32) + b_ref[...].astype(F32)).astype(BF16)

    spec = pl.BlockSpec((1, tr, c), lambda k, i, core_ref: (k, i, 0))
    half = pl.BlockSpec((1, tr, c), lambda k, i, core_ref: (k, i + core_ref[0] * (r // tr), 0))
    return pl.pallas_call(
        body, grid_spec=pltpu.PrefetchScalarGridSpec(num_scalar_prefetch=1, grid=(n, r // tr), in_specs=[half, spec],
                                                     out_specs=spec),
        out_shape=jax.ShapeDtypeStruct((n, r, c), BF16), name=name, compiler_params=_cp("parallel", "parallel"))(core, a, b)


def _sum4(a, name):
    _, r, c = a.shape
    tr = _row_block(r)

    def body(a_ref, o_ref):
        o_ref[...] = ((a_ref[0].astype(F32) + a_ref[1].astype(F32)) + a_ref[2].astype(F32)) + a_ref[3].astype(F32)

    return pl.pallas_call(body, grid=(r // tr,), in_specs=[pl.BlockSpec((4, tr, c), lambda i: (0, i, 0))],
                          out_specs=pl.BlockSpec((tr, c), lambda i: (i, 0)),
                          out_shape=jax.ShapeDtypeStruct((r, c), F32), name=name, compiler_params=_cp("parallel"))(a)


def _adam_small(w, gathered, m, v):
    c1 = 1.0 - ADAM_B1 ** ADAM_STEP
    c2 = 1.0 - ADAM_B2 ** ADAM_STEP

    def body(w_ref, g_ref, m_ref, v_ref, gs_ref, d_ref, nm_ref, nv_ref):
        gv = g_ref[0]
        for dev in range(1, N_DEV):
            gv = gv + g_ref[dev]
        gs_ref[...] = gv
        nm = ADAM_B1 * m_ref[...] + (1.0 - ADAM_B1) * gv
        nv = ADAM_B2 * v_ref[...] + (1.0 - ADAM_B2) * (gv * gv)
        d_ref[...] = -ADAM_LR * ((nm / c1) / (jnp.sqrt(nv / c2) + ADAM_EPS) + ADAM_WD * w_ref[...])
        nm_ref[...] = nm
        nv_ref[...] = nv

    return pl.pallas_call(body, out_shape=[jax.ShapeDtypeStruct((8, D), F32)] * 4, name="adam_small")(w, gathered, m, v)


def _place():
    return lax.axis_index("x"), lax.axis_index("y"), lax.axis_index("c")


def _other_chips(x, y):
    return [(1 - x, y), (x, 1 - y), (1 - x, 1 - y)]


GATHER_SEMS = [pltpu.SemaphoreType.DMA((6,)), pltpu.SemaphoreType.DMA((6,)), pltpu.SemaphoreType.DMA]


def _gather_ops(in_refs, out_refs, sems):
    (p_ref,), (out_ref,) = in_refs, out_refs
    send_sems, recv_sems, local_sem = sems
    hr = p_ref.shape[0] // 2
    x, y, cc = _place()
    sibling = (x, y, 1 - cc)
    chips = _other_chips(x, y)

    def half(chip, core):
        return out_ref.at[2 * chip[0] + chip[1], pl.ds(core * hr, hr), :]

    def copy(k, chip, core, to, src=None):
        return pltpu.make_async_remote_copy(
            src_ref=half(chip, core) if src is None else src, dst_ref=half(chip, core),
            send_sem=send_sems.at[k], recv_sem=recv_sems.at[k], device_id=to, device_id_type=MESH)

    mine = pltpu.make_async_copy(p_ref, out_ref.at[2 * x + y], local_sem)
    my_half = p_ref.at[pl.ds(cc * hr, hr), :]
    first = [copy(j, (x, y), cc, (*chip, cc), src=my_half) for j, chip in enumerate(chips)]
    passed = [copy(3 + j, chip, cc, sibling) for j, chip in enumerate(chips)]

    def start():
        mine.start()
        for cp in first:
            cp.start()

    def forward():
        for j, chip in enumerate(chips):
            copy(j, chip, cc, (x, y, cc)).wait_recv()
            passed[j].start()

    def finish():
        for j, chip in enumerate(chips):
            copy(3 + j, chip, 1 - cc, (x, y, cc)).wait_recv()
        for cp in first + passed:
            cp.wait_send()
        mine.wait()

    return start, forward, finish


def _gather_side(p):
    return _Side([p], [jax.ShapeDtypeStruct((N_CHIPS,) + p.shape, p.dtype)], GATHER_SEMS, _gather_ops)


def _swap_halves(g):
    n, r, c = g.shape
    hr = r // 2

    def body(g_ref, out_ref, send_sem, recv_sem):
        x, y, cc = _place()
        cp = pltpu.make_async_remote_copy(
            src_ref=g_ref.at[:, pl.ds((1 - cc) * hr, hr), :], dst_ref=out_ref,
            send_sem=send_sem, recv_sem=recv_sem, device_id=(x, y, 1 - cc), device_id_type=MESH)
        cp.start()
        cp.wait()

    any_spec = pl.BlockSpec(memory_space=pl.ANY)
    return pl.pallas_call(
        body, out_shape=jax.ShapeDtypeStruct((n, hr, c), g.dtype), in_specs=[any_spec], out_specs=any_spec,
        scratch_shapes=[pltpu.SemaphoreType.DMA, pltpu.SemaphoreType.DMA], name="swap_halves")(g)


SCATTER_SEMS = [pltpu.SemaphoreType.DMA((7,)), pltpu.SemaphoreType.DMA((7,)), pltpu.SemaphoreType.DMA]


def _scatter_ops(in_refs, out_refs, sems):
    (p_ref,), (out_ref,) = in_refs, out_refs
    send_sems, recv_sems, local_sem = sems
    hr = p_ref.shape[1]
    x, y, cc = _place()
    me = 2 * x + y
    sibling = (x, y, 1 - cc)
    chips = _other_chips(x, y)
    ids = [2 * chip[0] + chip[1] for chip in chips]

    def land(src, core):
        return out_ref.at[src, pl.ds(core * hr, hr), :]

    def copy(k, src_ref, dst_ref, to):
        return pltpu.make_async_remote_copy(src_ref=src_ref, dst_ref=dst_ref, send_sem=send_sems.at[k],
                                            recv_sem=recv_sems.at[k], device_id=to, device_id_type=MESH)

    mine = pltpu.make_async_copy(p_ref.at[me], land(me, cc), local_sem)
    sends = [copy(j, p_ref.at[ids[j]], land(me, cc), (*chip, cc)) for j, chip in enumerate(chips)]
    sends.append(copy(3, p_ref.at[me], land(me, cc), sibling))
    passed = [copy(4 + j, land(ids[j], cc), land(ids[j], cc), sibling) for j in range(3)]

    def start():
        mine.start()
        for cp in sends:
            cp.start()

    def forward():
        for j in range(3):
            copy(j, p_ref.at[me], land(ids[j], cc), (x, y, cc)).wait_recv()
            passed[j].start()

    def finish():
        copy(3, p_ref.at[me], land(me, 1 - cc), (x, y, cc)).wait_recv()
        for j in range(3):
            copy(4 + j, p_ref.at[me], land(ids[j], 1 - cc), (x, y, cc)).wait_recv()
        for cp in sends + passed:
            cp.wait_send()
        mine.wait()

    return start, forward, finish


def _scatter_side(p):
    n, hr, c = p.shape
    return _Side([p], [jax.ShapeDtypeStruct((n, 2 * hr, c), p.dtype)], SCATTER_SEMS, _scatter_ops)


def _gather_small(blk):
    m, n = blk.shape

    def body(x_ref, out_ref, send_sems, recv_sems, local_sem):
        x, y, cc = _place()
        me, sibling = (x, y, cc), (x, y, 1 - cc)
        chips = _other_chips(x, y)

        def slot(px, py, pc):
            return out_ref.at[4 * px + 2 * py + pc]

        def copy(k, block, to, src=None):
            return pltpu.make_async_remote_copy(
                src_ref=slot(*block) if src is None else src, dst_ref=slot(*block),
                send_sem=send_sems.at[k], recv_sem=recv_sems.at[k], device_id=to, device_id_type=MESH)

        mine = pltpu.make_async_copy(x_ref, slot(*me), local_sem)
        mine.start()
        first = [copy(0, me, sibling, src=x_ref)]
        first += [copy(1 + j, me, (*chip, cc), src=x_ref) for j, chip in enumerate(chips)]
        for cp in first:
            cp.start()
        passed = [copy(4 + j, (*chip, cc), sibling) for j, chip in enumerate(chips)]
        for j, chip in enumerate(chips):
            copy(1 + j, (*chip, cc), me).wait_recv()
            passed[j].start()
        copy(0, sibling, me).wait_recv()
        for j, chip in enumerate(chips):
            copy(4 + j, (*chip, 1 - cc), me).wait_recv()
        for cp in first + passed:
            cp.wait_send()
        mine.wait()

    vmem = pl.BlockSpec(memory_space=pltpu.VMEM)
    return pl.pallas_call(
        body, out_shape=jax.ShapeDtypeStruct((N_DEV, m, n), blk.dtype), in_specs=[vmem], out_specs=vmem,
        scratch_shapes=[pltpu.SemaphoreType.DMA((7,)), pltpu.SemaphoreType.DMA((7,)), pltpu.SemaphoreType.DMA],
        name="gather_small")(blk)


def _pack_a(sh, dtype):
    w = sh["w_in"].astype(dtype)
    return jnp.concatenate([w[:, 0:PACK_W], jnp.pad(w[:, PACK_W:], ((0, 0), (0, 2 * PACK_W - w.shape[1])))], axis=0)


def _pack_b(sh, dtype):
    o3 = jnp.concatenate([sh["w_gla_o"], sh["w_fox_o"], sh["w_mem_o"], jnp.zeros((512, 256), sh["w_gla_o"].dtype)], axis=1)
    au = jnp.pad(sh["w_alpha_up"], ((0, PACK_ROWS_B - 3072 - 16), (0, PACK_W - 64)))
    return jnp.concatenate([sh["w_ff1"], sh["w_ff2"], sh["w_mem_kv"], sh["w_out"], o3, au], axis=0).astype(dtype)


def _unpack_a(pa):
    return {"w_in": jnp.concatenate([pa[0:1024], pa[1024:2048, 0:1670 - PACK_W]], axis=1)}


def _unpack_b(pb):
    return {"w_ff1": pb[0:1024], "w_ff2": pb[1024:2048], "w_mem_kv": pb[2048:2304], "w_out": pb[2304:2560],
            "w_gla_o": pb[2560:3072, 0:256], "w_fox_o": pb[2560:3072, 256:512], "w_mem_o": pb[2560:3072, 512:768],
            "w_alpha_up": pb[3072:3088, 0:64]}


def _unpack(packed):
    return {**_unpack_a(packed[0:PACK_ROWS_A]), **_unpack_b(packed[PACK_ROWS_A:])}


def _split_shards(name, full):
    return jnp.split(full, N_CHIPS, axis=SHARD_AXIS[name])


def _pack_small(vals, scalar=None):
    row4 = jnp.concatenate([vals["b_alpha"].reshape(-1), vals["b_forget"].reshape(-1), jnp.zeros((D - 264,), F32)])
    row5 = jnp.concatenate([vals["g_gla_head"].reshape(-1), jnp.zeros((D - 512,), F32)])
    row6 = jnp.zeros((D,), F32) if scalar is None else jnp.broadcast_to(scalar, (D,))
    rows = [vals["g_mix"].reshape(-1), vals["g_mem"].reshape(-1), vals["g_ffn"].reshape(-1), vals["g_final"].reshape(-1),
            row4, row5, row6, jnp.zeros((D,), F32)]
    return jnp.stack(rows)


def _unpack_small(blk):
    return {"g_mix": blk[0].reshape(1, D), "g_mem": blk[1].reshape(1, D), "g_ffn": blk[2].reshape(1, D),
            "g_final": blk[3].reshape(D), "b_alpha": blk[4, 0:256].reshape(1, 256), "b_forget": blk[4, 256:264].reshape(1, 8),
            "g_gla_head": blk[5, 0:512].reshape(1, 4, 128)}


def _local_step(x, mem, target, wb, small, exchange=None):
    s = x.shape[0]
    nm = mem.shape[0]
    t = _row_tile(s)
    nb = s // t
    b_alpha = small["b_alpha"].reshape(1, 256)
    bias_e = jnp.concatenate([jnp.zeros((FF_LANE,), F32), small["b_forget"].reshape(-1),
                              jnp.zeros((PE_W - FF_LANE - 8,), F32)]).reshape(1, PE_W)
    g_mix, g_mem, g_ffn = small["g_mix"].reshape(1, D), small["g_mem"].reshape(1, D), small["g_ffn"].reshape(1, D)
    g_final = small["g_final"].reshape(1, D)
    g_head = small["g_gla_head"].reshape(1, 512)

    if exchange is None:
        u, r1 = _rms_fwd(x, g_mix, "norm_mix")
    else:
        u, r1, gathered = _rms_fwd(x, g_mix, "norm_mix", side=exchange.gather_a)
        wb = exchange.weights_a(gathered)
    w_in = wb["w_in"]
    w_main = jnp.concatenate([w_in[:, 3608:6680], w_in[:, 0:1536], w_in[:, 1552:3088], w_in[:, 3096:3608]], axis=1)
    w_e = jnp.concatenate([w_in[:, 1536:1552], w_in[:, 3088:3096], jnp.zeros((D, PE_W - 24), BF16)], axis=1)
    w_in_pt = _transpose(jnp.concatenate([w_main, w_e], axis=1), "t_w_in")
    big = min(s, 1024)
    if exchange is None:
        pm, pe = _proj(u, w_main, w_e)
    else:
        pm, pe, gathered = _proj(u, w_main, w_e, side=exchange.gather_b)
        wb = {**wb, **exchange.weights_b(gathered)}
    wau_p = jnp.concatenate([wb["w_alpha_up"], jnp.zeros((PE_W - 16, 256), BF16)], axis=0)
    o_gla, states = _gla_fwd(pm, pe, wau_p, b_alpha)
    tb = _fox_tables()
    qf_aug, k_aug, v_aug, vt, qt, kt, fcum = _fox_prep(pm, (pe, bias_e), None, tb, backward=False)
    o_fox, lse = _fox_fwd(k_aug, qf_aug, vt)
    mn, rm = _rms_fwd(mem, g_mem, "norm_mem")
    mkv = _mm_nn(mn, wb["w_mem_kv"], out_dtype=BF16, tm=nm, tn=512, tk=D, name="mem_kv")
    o_mem = _mem_attn_fwd(pm, mkv)
    merged, h1, u2, r2 = _merge_fwd(x, pm, o_gla, o_fox, o_mem, g_head, wb["w_gla_o"], wb["w_fox_o"], wb["w_mem_o"],
                                    wb["w_out"], g_ffn)
    a = _mm_nn(u2, wb["w_ff1"], out_dtype=BF16, tm=big, tn=1024, tk=D, name="ff1")
    dh2, dh2b, loss8, dgfin8 = _ff2_loss(a, wb["w_ff2"], h1, g_final, target)
    loss = 0.5 * jnp.sum(loss8) / D

    da = _mm_nn(dh2b, _transpose(wb["w_ff2"], "t_w_ff2"), out_dtype=BF16, tm=big, tn=1024, tk=D, name="d_act",
                epi=lambda acc, at: acc * (2.0 * jnp.maximum(at.astype(F32), 0.0)), extra=a)
    gw = {}
    gw["w_ff2"] = _mm_tn(a, dh2b, tm=1024, tn=D, ts=big, name="dw_ff2", a_fn=_relu2_bf16)
    gw["w_ff1"] = _mm_tn(u2, da, tm=D, tn=1024, ts=big, name="dw_ff1")
    dh1, dh1b, dgffn8 = _mm_norm_bwd([da], _transpose(wb["w_ff1"], "t_w_ff1"), h1, r2, g_ffn, dh2, name="d_h1", want_bf16=True)
    gw["w_out"] = _mm_tn(merged, dh1b, tm=D, tn=D, ts=big, name="dw_out")
    (dgates, dgg, do_gla, do_aug, do_t, do_mem, gw["w_gla_o"], gw["w_fox_o"], gw["w_mem_o"], dgh8) = _merge_bwd(
        dh1b, pm, o_gla, o_fox, o_mem, g_head, wb["w_gla_o"], wb["w_fox_o"], wb["w_mem_o"],
        *[_transpose(wb[n], "t_" + n) for n in ("w_gla_o", "w_fox_o", "w_mem_o", "w_out")], tb["spread"], tb["d_to_do"])
    dgq, dgk, dgv, de_gla, dwau_p, dba8 = _gla_bwd(pm, pe, wau_p, wau_p.T, b_alpha, do_gla, states)
    gw["w_alpha_up"] = dwau_p[0:16, :]
    q_aug = _fox_prep(pm, fcum, lse.reshape(8, s), tb, backward=True)
    dfq_t, dfrow, dfk_t, dfv_t, dfcol = _fox_bwd(q_aug, do_aug, qt, do_t, k_aug, v_aug, kt)
    dfq, dfk, dfv, df = _fox_post(dfq_t, dfk_t, dfv_t, dfrow[:, 0, :], dfcol.reshape(8, s), tb)
    de_fox, dbf8 = _fcum_bwd(pe, bias_e, df)
    dmq, dmk, dmv = _mem_attn_bwd(pm, mkv, do_mem)
    dmkv = jnp.concatenate([dmk, dmv], axis=1).astype(BF16)
    gw["w_mem_kv"] = _mm_tn(mn, dmkv, tm=D, tn=D, ts=nm, name="dw_mem_kv")
    dmn_g = _mm_nn(dmkv, _transpose(wb["w_mem_kv"], "t_w_mem_kv"), out_dtype=F32, tm=nm, tn=D, tk=D, name="d_mem_norm")
    dgmem8 = _gain_grad(dmn_g, mem, rm, "dg_mem")
    de = (de_gla + de_fox).astype(BF16)
    dproj = [dgates, dgq, dgk, dgv, dgg, dfq, dfk, dfv, dmq, de]
    dw_gates = _mm_tn(u, dgates, tm=D, tn=1024, ts=big, name="dw_in_gates")
    dw_g = _mm_tn_cat(u, [dgq, dgk, dgv], ts=big, name="dw_in_gla")
    dw_gf = _mm_tn_cat(u, [dgg, dfq], ts=big, name="dw_in_gg_fq")
    dw_f = _mm_tn_cat(u, [dfk, dfv], ts=big, name="dw_in_fk_fv")
    dw_m = _mm_tn_cat(u, [dmq, de], ts=big, name="dw_in_mq_narrow")
    gw["w_in"] = jnp.concatenate([dw_g, dw_gf[:, 0:512], dw_m[:, 512:528], dw_gf[:, 512:1024], dw_f,
                                  dw_m[:, 528:536], dw_m[:, 0:512], dw_gates], axis=1)
    if exchange is None:
        grad_x, dgmix8 = _mm_norm_bwd(dproj, w_in_pt, x, r1, g_mix, dh1, name="d_x", want_bf16=False)
        exchanged = None
    else:
        grad_x, dgmix8, exchanged = _mm_norm_bwd(dproj, w_in_pt, x, r1, g_mix, dh1, name="d_x", want_bf16=False,
                                                 side=exchange.scatter(gw))
    gs = {"g_mix": dgmix8.sum(0), "g_mem": dgmem8.sum(0), "g_ffn": dgffn8.sum(0), "g_final": dgfin8.sum(0),
          "b_alpha": dba8.sum(0), "b_forget": dbf8.sum(0)[FF_LANE:FF_LANE + 8], "g_gla_head": dgh8.sum(0)}
    return loss, grad_x, gw, gs, exchanged


def kernel(x, mem, g_mix, w_in, w_alpha_up, b_alpha, b_forget, g_gla_head, g_mem, w_mem_kv, w_gla_o, w_fox_o, w_mem_o, w_out, g_ffn, w_ff1, w_ff2, g_final, loss_target, m_g_mix, m_w_in, m_w_alpha_up, m_b_alpha, m_b_forget, m_g_gla_head, m_g_mem, m_w_mem_kv, m_w_gla_o, m_w_fox_o, m_w_mem_o, m_w_out, m_g_ffn, m_w_ff1, m_w_ff2, m_g_final, v_g_mix, v_w_in, v_w_alpha_up, v_b_alpha, v_b_forget, v_g_gla_head, v_g_mem, v_w_mem_kv, v_w_gla_o, v_w_fox_o, v_w_mem_o, v_w_out, v_g_ffn, v_w_ff1, v_w_ff2, v_g_final):
    args = dict(locals())
    w_sh = {n: args[n][0] for n in WEIGHTS}
    small = {n: args[n] for n in SMALL}

    def whole(parts):
        return {n: jnp.concatenate([p[n] for p in parts], axis=SHARD_AXIS[n]) for n in parts[0]}

    class Exchange:
        gather_a = _gather_side(_pack_a(w_sh, BF16))
        gather_b = _gather_side(_pack_b(w_sh, BF16))

        @staticmethod
        def weights_a(gathered):
            return whole([_unpack_a(gathered[k]) for k in range(N_CHIPS)])

        @staticmethod
        def weights_b(gathered):
            return whole([_unpack_b(gathered[k]) for k in range(N_CHIPS)])

        @staticmethod
        def scatter(gw):
            by_chip = {n: _split_shards(n, gw[n]) for n in WEIGHTS}
            packed = jnp.stack([jnp.concatenate([_pack_a({n: by_chip[n][k] for n in WEIGHTS}, BF16),
                                                 _pack_b({n: by_chip[n][k] for n in WEIGHTS}, BF16)], axis=0)
                                for k in range(N_CHIPS)])
            core = lax.axis_index("c").astype(jnp.int32).reshape(1)
            return _scatter_side(_add_half(core, packed, _swap_halves(packed), "chip_sum"))

    loss, grad_x, gw, gs, by_chip = _local_step(x[0], mem[0], loss_target[0], None, small, Exchange)
    g_out = {n: g[None] for n, g in _unpack(_sum4(by_chip, "shard_sum")).items()}
    d_out, m_out, v_out = {}, {}, {}
    for n in WEIGHTS:
        d_out[n], m_out[n], v_out[n] = _adam(args[n], g_out[n], args["m_" + n], args["v_" + n], "adam_" + n)

    small_all = _gather_small(_pack_small(gs, loss))
    sm = {n: args["m_" + n] for n in SMALL}
    sv = {n: args["v_" + n] for n in SMALL}
    gs_sum, sd, snm, snv = _adam_small(_pack_small(small), small_all, _pack_small(sm), _pack_small(sv))
    gs_o, sd_o, snm_o, snv_o = _unpack_small(gs_sum), _unpack_small(sd), _unpack_small(snm), _unpack_small(snv)

    names = ["g_mix", "w_in", "w_alpha_up", "b_alpha", "b_forget", "g_gla_head", "g_mem", "w_mem_kv", "w_gla_o", "w_fox_o",
             "w_mem_o", "w_out", "g_ffn", "w_ff1", "w_ff2", "g_final"]

    def pick(big, sml, n):
        return big[n] if n in big else sml[n]

    outs = [gs_sum[6, 0], grad_x[None]]
    for big, sml in ((g_out, gs_o), (d_out, sd_o), (m_out, snm_o), (v_out, snv_o)):
        outs += [pick(big, sml, n) for n in names]
    return tuple(outs)
```

```python
import functools

import numpy as np
import jax
import jax.numpy as jnp
from jax import lax
from jax.experimental import pallas as pl
from jax.experimental.pallas import tpu as pltpu

F32 = jnp.float32
BF16 = jnp.bfloat16
HI = lax.Precision.HIGHEST
MESH = pl.DeviceIdType.MESH

EPS = 1e-6
D = 1024
CHUNK = 64
GLA_TAU = 16.0
N_CHIPS = 4
N_DEV = 8
VMEM_LIMIT_BYTES = 56 * 1024 * 1024

ADAM_LR, ADAM_B1, ADAM_B2, ADAM_EPS, ADAM_WD, ADAM_STEP = 0.001, 0.9, 0.999, 1e-08, 0.01, 10

PM_W = 6656
PE_W = 128
C_GQ, C_GK, C_GV, C_GG, C_FQ, C_FK, C_FV, C_MQ = 3072, 3328, 3584, 4096, 4608, 5120, 5632, 6144
FF_LANE = 16

WEIGHTS = ("w_in", "w_alpha_up", "w_mem_kv", "w_gla_o", "w_fox_o", "w_mem_o", "w_out", "w_ff1", "w_ff2")
SHARD_AXIS = {"w_in": 1, "w_alpha_up": 1, "w_mem_kv": 0, "w_gla_o": 1, "w_fox_o": 1, "w_mem_o": 1, "w_out": 0,
              "w_ff1": 1, "w_ff2": 0}
SMALL = ("g_mix", "g_mem", "g_ffn", "g_final", "b_alpha", "b_forget", "g_gla_head")
PACK_W = 1024
PACK_ROWS_A = 2048
PACK_ROWS_B = 3104
PACK_ROWS = PACK_ROWS_A + PACK_ROWS_B


def _cp(*sem):
    return pltpu.CompilerParams(dimension_semantics=sem, vmem_limit_bytes=VMEM_LIMIT_BYTES)


def _dot(a, b, **kw):
    return jnp.dot(a, b, preferred_element_type=F32, **kw)


def _dot_nt(a, b, **kw):
    return lax.dot_general(a, b, (((1,), (1,)), ((), ())), preferred_element_type=F32, **kw)


def _dot_tn(a, b, **kw):
    return lax.dot_general(a, b, (((0,), (0,)), ((), ())), preferred_element_type=F32, **kw)


def _sigmoid(x):
    return 0.5 * jnp.tanh(0.5 * x) + 0.5


def _log_sigmoid(x):
    return -(jnp.maximum(-x, 0.0) + jnp.log1p(jnp.exp(-jnp.abs(x))))


def _fold8(x):
    m, n = x.shape
    return x.reshape(m // 8, 8, n).sum(axis=0)


def _iota(shape, dim):
    return lax.broadcasted_iota(jnp.int32, shape, dim)


def _row_tile(s):
    return min(s, 512)


class _Side:
    def __init__(self, inputs, out_shape, scratch, ops):
        self.inputs, self.out_shape, self.scratch, self.ops = list(inputs), list(out_shape), list(scratch), ops


ANY_SPEC = pl.BlockSpec(memory_space=pl.ANY)


def _mm_nn(a, b, *, out_dtype, tm, tn, tk, name, a_fn=None, epi=None, extra=None):
    m, k = a.shape
    _, n = b.shape
    nk = k // tk

    def body_one(*refs):
        a_ref, b_ref = refs[0], refs[1]
        at = a_ref[...] if a_fn is None else a_fn(a_ref[...])
        r = _dot(at, b_ref[...])
        if epi is not None:
            r = epi(r, None if extra is None else refs[2][...])
        refs[-1][...] = r.astype(out_dtype)

    if nk == 1:
        in_specs = [pl.BlockSpec((tm, k), lambda i, j: (i, 0)), pl.BlockSpec((k, tn), lambda i, j: (0, j))]
        args = [a, b]
        if extra is not None:
            in_specs.append(pl.BlockSpec((tm, tn), lambda i, j: (i, j)))
            args.append(extra)
        return pl.pallas_call(
            body_one, grid=(m // tm, n // tn), in_specs=in_specs, out_specs=pl.BlockSpec((tm, tn), lambda i, j: (i, j)),
            out_shape=jax.ShapeDtypeStruct((m, n), out_dtype), name=name, compiler_params=_cp("parallel", "parallel"))(*args)

    def body(*refs):
        if extra is None:
            a_ref, b_ref, o_ref, acc = refs
            x_ref = None
        else:
            a_ref, b_ref, x_ref, o_ref, acc = refs
        kk = pl.program_id(2)

        @pl.when(kk == 0)
        def _():
            acc[...] = jnp.zeros_like(acc)

        at = a_ref[...]
        if a_fn is not None:
            at = a_fn(at)
        acc[...] += _dot(at, b_ref[...])

        @pl.when(kk == nk - 1)
        def _():
            r = acc[...]
            if epi is not None:
                r = epi(r, None if x_ref is None else x_ref[...])
            o_ref[...] = r.astype(out_dtype)

    in_specs = [pl.BlockSpec((tm, tk), lambda i, j, kk: (i, kk)), pl.BlockSpec((tk, tn), lambda i, j, kk: (kk, j))]
    args = [a, b]
    if extra is not None:
        in_specs.append(pl.BlockSpec((tm, tn), lambda i, j, kk: (i, j)))
        args.append(extra)
    return pl.pallas_call(
        body, grid=(m // tm, n // tn, nk), in_specs=in_specs,
        out_specs=pl.BlockSpec((tm, tn), lambda i, j, kk: (i, j)),
        out_shape=jax.ShapeDtypeStruct((m, n), out_dtype),
        scratch_shapes=[pltpu.VMEM((tm, tn), F32)], name=name,
        compiler_params=_cp("parallel", "parallel", "arbitrary"))(*args)


def _mm_tn(a, b, *, tm, tn, ts, name, a_fn=None):
    s, m = a.shape
    _, n = b.shape
    ns = s // ts

    def body(a_ref, b_ref, o_ref, acc):
        kk = pl.program_id(2)

        @pl.when(kk == 0)
        def _():
            acc[...] = jnp.zeros_like(acc)

        at = a_ref[...]
        if a_fn is not None:
            at = a_fn(at)
        acc[...] += _dot_tn(at, b_ref[...])

        @pl.when(kk == ns - 1)
        def _():
            o_ref[...] = acc[...]

    return pl.pallas_call(
        body, grid=(m // tm, n // tn, ns),
        in_specs=[pl.BlockSpec((ts, tm), lambda i, j, kk: (kk, i)), pl.BlockSpec((ts, tn), lambda i, j, kk: (kk, j))],
        out_specs=pl.BlockSpec((tm, tn), lambda i, j, kk: (i, j)),
        out_shape=jax.ShapeDtypeStruct((m, n), F32),
        scratch_shapes=[pltpu.VMEM((tm, tn), F32)], name=name,
        compiler_params=_cp("parallel", "parallel", "arbitrary"))(a, b)


def _mm_tn_cat(a, bs, *, ts, name):
    s, m = a.shape
    n = sum(b.shape[1] for b in bs)
    ns = s // ts
    nb = len(bs)

    def body(*refs):
        a_ref, b_refs, o_ref, acc = refs[0], refs[1:1 + nb], refs[1 + nb], refs[2 + nb]
        kk = pl.program_id(0)

        @pl.when(kk == 0)
        def _():
            acc[...] = jnp.zeros_like(acc)

        bt = b_refs[0][...] if nb == 1 else jnp.concatenate([r[...] for r in b_refs], axis=1)
        acc[...] += _dot_tn(a_ref[...], bt)

        @pl.when(kk == ns - 1)
        def _():
            o_ref[...] = acc[...]

    return pl.pallas_call(
        body, grid=(ns,),
        in_specs=[pl.BlockSpec((ts, m), lambda kk: (kk, 0))] + [pl.BlockSpec((ts, b.shape[1]), lambda kk: (kk, 0)) for b in bs],
        out_specs=pl.BlockSpec((m, n), lambda kk: (0, 0)), out_shape=jax.ShapeDtypeStruct((m, n), F32),
        scratch_shapes=[pltpu.VMEM((m, n), F32)], name=name, compiler_params=_cp("arbitrary"))(a, *bs)


def _proj(u, w_main, w_e, side=None):
    s, k = u.shape
    n = w_main.shape[1]
    tm, tn = min(s, 1024), n // 4
    n_sin = 0 if side is None else len(side.inputs)
    n_sout = 0 if side is None else len(side.out_shape)

    def body(u_ref, w_ref, we_ref, *rest):
        pm_ref, pe_ref = rest[n_sin:n_sin + 2]
        i, j = pl.program_id(0), pl.program_id(1)
        if side is not None:
            start, forward, finish = side.ops(rest[:n_sin], rest[n_sin + 2:n_sin + 2 + n_sout], rest[n_sin + 2 + n_sout:])
            pl.when((i == 0) & (j == 0))(start)
        ut = u_ref[...]
        pm_ref[...] = _dot(ut, w_ref[...]).astype(BF16)

        @pl.when(j == 0)
        def _():
            pe_ref[...] = _dot(ut, we_ref[...])

        if side is not None:
            pl.when((i == (s // tm) * 5 // 8) & (j == n // tn - 1))(forward)
            pl.when((i == s // tm - 1) & (j == n // tn - 1))(finish)

    side_in = [] if side is None else side.inputs
    return pl.pallas_call(
        body, grid=(s // tm, n // tn),
        in_specs=[pl.BlockSpec((tm, k), lambda i, j: (i, 0)), pl.BlockSpec((k, tn), lambda i, j: (0, j)),
                  pl.BlockSpec((k, PE_W), lambda i, j: (0, 0))] + [ANY_SPEC] * n_sin,
        out_specs=[pl.BlockSpec((tm, tn), lambda i, j: (i, j)), pl.BlockSpec((tm, PE_W), lambda i, j: (i, 0))]
        + [ANY_SPEC] * n_sout,
        out_shape=[jax.ShapeDtypeStruct((s, n), BF16), jax.ShapeDtypeStruct((s, PE_W), F32)]
        + ([] if side is None else side.out_shape),
        scratch_shapes=[] if side is None else side.scratch,
        name="proj_main", compiler_params=_cp("arbitrary", "arbitrary"))(u, w_main, w_e, *side_in)


def _transpose(w, name):
    r, c = w.shape
    tr = min(r, 256)

    def body(w_ref, o_ref):
        o_ref[...] = w_ref[...].T

    return pl.pallas_call(body, grid=(r // tr,), in_specs=[pl.BlockSpec((tr, c), lambda i: (i, 0))],
                          out_specs=pl.BlockSpec((c, tr), lambda i: (0, i)),
                          out_shape=jax.ShapeDtypeStruct((c, r), w.dtype), name=name, compiler_params=_cp("parallel"))(w)


def _relu2_bf16(t):
    r = jnp.maximum(t.astype(F32), 0.0)
    return (r * r).astype(BF16)


def _rms_fwd(x, g, name, side=None):
    s, d = x.shape
    tm = min(s, 512)
    n_sin = 0 if side is None else len(side.inputs)
    n_sout = 0 if side is None else len(side.out_shape)

    def body(x_ref, g_ref, *rest):
        u_ref, r_ref = rest[n_sin:n_sin + 2]
        if side is not None:
            start, forward, finish = side.ops(rest[:n_sin], rest[n_sin + 2:n_sin + 2 + n_sout], rest[n_sin + 2 + n_sout:])
            pl.when(pl.program_id(0) == 0)(start)
        xv = x_ref[...]
        r = lax.rsqrt(jnp.mean(xv * xv, axis=-1, keepdims=True) + EPS)
        u_ref[...] = ((xv * r) * g_ref[...]).astype(BF16)
        r_ref[...] = r
        if side is not None:
            pl.when(pl.program_id(0) == s // tm - 1)(forward)
            pl.when(pl.program_id(0) == s // tm - 1)(finish)

    side_in = [] if side is None else side.inputs
    return pl.pallas_call(
        body, grid=(s // tm,),
        in_specs=[pl.BlockSpec((tm, d), lambda i: (i, 0)), pl.BlockSpec((1, d), lambda i: (0, 0))] + [ANY_SPEC] * n_sin,
        out_specs=[pl.BlockSpec((tm, d), lambda i: (i, 0)), pl.BlockSpec((tm, 1), lambda i: (i, 0))] + [ANY_SPEC] * n_sout,
        out_shape=[jax.ShapeDtypeStruct((s, d), BF16), jax.ShapeDtypeStruct((s, 1), F32)]
        + ([] if side is None else side.out_shape),
        scratch_shapes=[] if side is None else side.scratch,
        name=name, compiler_params=_cp("parallel" if side is None else "arbitrary"))(x, g, *side_in)


def _mm_norm_bwd(a_parts, b, xin, r, g, dres, *, name, want_bf16, side=None):
    s = a_parts[0].shape[0]
    k = b.shape[0]
    na = len(a_parts)
    offs = [sum(p.shape[1] for p in a_parts[:i]) for i in range(na)]
    assert offs[-1] + a_parts[-1].shape[1] == k
    tm = min(s, 512)
    n_out = 3 if want_bf16 else 2
    n_sin = 0 if side is None else len(side.inputs)
    n_sout = 0 if side is None else len(side.out_shape)

    def body(*refs):
        a_refs = refs[:na]
        b_ref, x_ref, r_ref, g_ref, dres_ref = refs[na:na + 5]
        rest = refs[na + 5:]
        outs = rest[n_sin:n_sin + n_out]
        dx_ref, dg_ref = outs[0], outs[-1]
        if side is not None:
            start, forward, finish = side.ops(rest[:n_sin], rest[n_sin + n_out:n_sin + n_out + n_sout], rest[n_sin + n_out + n_sout:])
            pl.when(pl.program_id(0) == 0)(start)

        @pl.when(pl.program_id(0) == 0)
        def _():
            dg_ref[...] = jnp.zeros_like(dg_ref)

        du = _dot(a_refs[0][...], b_ref[0:a_parts[0].shape[1], :])
        for a_ref, off, part in zip(a_refs[1:], offs[1:], a_parts[1:]):
            du = du + _dot(a_ref[...], b_ref[off:off + part.shape[1], :])
        xn = x_ref[...] * r_ref[...]
        dg_ref[...] += _fold8(du * xn)
        dxn = du * g_ref[...]
        dx = dres_ref[...] + r_ref[...] * (dxn - xn * jnp.mean(dxn * xn, axis=-1, keepdims=True))
        dx_ref[...] = dx
        if want_bf16:
            outs[1][...] = dx.astype(BF16)
        if side is not None:
            pl.when(pl.program_id(0) == (s // tm) * 13 // 16)(forward)
            pl.when(pl.program_id(0) == s // tm - 1)(finish)

    row = lambda i: (i, 0)
    const = lambda i: (0, 0)
    out_specs = [pl.BlockSpec((tm, D), row)]
    out_shape = [jax.ShapeDtypeStruct((s, D), F32)]
    if want_bf16:
        out_specs.append(pl.BlockSpec((tm, D), row))
        out_shape.append(jax.ShapeDtypeStruct((s, D), BF16))
    out_specs.append(pl.BlockSpec((8, D), const))
    out_shape.append(jax.ShapeDtypeStruct((8, D), F32))
    side_in = [] if side is None else side.inputs
    return pl.pallas_call(
        body, grid=(s // tm,),
        in_specs=[pl.BlockSpec((tm, p.shape[1]), row) for p in a_parts]
        + [pl.BlockSpec((k, D), const, pipeline_mode=pl.Buffered(1)),
           pl.BlockSpec((tm, D), row), pl.BlockSpec((tm, 1), row), pl.BlockSpec((1, D), const),
           pl.BlockSpec((tm, D), row)] + [ANY_SPEC] * n_sin,
        out_specs=out_specs + [ANY_SPEC] * n_sout, out_shape=out_shape + ([] if side is None else side.out_shape),
        scratch_shapes=[] if side is None else side.scratch,
        name=name, compiler_params=_cp("arbitrary"))(*a_parts, b, xin, r, g, dres, *side_in)


def _gla_consts():
    lmask = _iota((4 * CHUNK, CHUNK), 0) % CHUNK >= _iota((4 * CHUNK, CHUNK), 1)
    hmask = _iota((256, 256), 0) // CHUNK == _iota((256, 256), 1) // CHUNK
    bd = _iota((256, 512), 0) // CHUNK == _iota((256, 512), 1) // 128
    return lmask, hmask, bd


def _fold_heads(x):
    return x[0:64] + x[64:128] + x[128:192] + x[192:256]


def _gla_decays(la, b_scr, dec_scr):
    tri = (_iota((CHUNK, CHUNK), 0) >= _iota((CHUNK, CHUNK), 1)).astype(BF16)
    ones = jnp.ones((CHUNK, 128), BF16)
    for c in range(la.shape[0] // CHUNK):
        la3 = _split3(la[CHUNK * c:CHUNK * (c + 1)])
        b_scr[CHUNK * c:CHUNK * (c + 1), :] = _sum3(_dot(tri, la3), 1)
        dec_scr[c] = jnp.exp(_sum3(_dot_tn(la3, ones), 0))


def _gla_chunk(b, qc, kc):
    bl = b[CHUNK - 1:CHUNK, :]
    ep, en, ek = jnp.exp(b), jnp.exp(-b), jnp.exp(bl - b)
    return bl, ep, en, ek, qc * ep, qc * en, kc * en, kc * ep, kc * ek


def _gla_fwd(pm, pe, wau_p, b_alpha):
    s = pm.shape[0]
    t = min(s, 1024)
    nc = t // CHUNK

    def body(q_ref, k_ref, v_ref, e_ref, wau_ref, ba_ref, o_ref, st_ref, state, b_scr, dec_scr):
        @pl.when(pl.program_id(0) == 0)
        def _():
            state[...] = jnp.zeros_like(state)

        z = _dot(e_ref[...].astype(BF16), wau_ref[...]) + ba_ref[...]
        _gla_decays(_log_sigmoid(z) * (1.0 / GLA_TAU), b_scr, dec_scr)
        lmask, hmask, bd = _gla_consts()

        def chunk(c, carry):
            rows = pl.ds(pl.multiple_of(c * CHUNK, CHUNK), CHUNK)
            qc = q_ref[rows, :].astype(F32) * 0.125
            kc = k_ref[rows, :].astype(F32)
            vc = v_ref[rows, :]
            _, _, _, _, qp, qn, kn, kp, kk = _gla_chunk(b_scr[rows, :], qc, kc)
            decb = jnp.concatenate([dec_scr[c]] * 4, axis=1)
            qs = jnp.where(hmask, jnp.concatenate([qp] * 4, axis=0), 0.0).astype(BF16)
            qns = jnp.where(hmask, jnp.concatenate([qn] * 4, axis=0), 0.0).astype(BF16)
            attn = jnp.where(lmask, _dot_nt(qs, kn.astype(BF16)), _dot_nt(qns, kp.astype(BF16))).astype(BF16)
            st = state[...]
            o_intra = _fold_heads(jnp.where(bd, _dot(attn, vc), 0.0))
            o_ref[rows, :] = o_intra + _dot(qp.astype(BF16), st.astype(BF16))
            for h in range(4):
                st_ref[c, :, 128 * h:128 * (h + 1)] = st[64 * h:64 * (h + 1), 128 * h:128 * (h + 1)]
            kv = jnp.where(bd, _dot_tn(kk.astype(BF16), vc), 0.0)
            state[...] = st * decb + kv
            return carry

        lax.fori_loop(0, nc, chunk, 0)

    return pl.pallas_call(
        body, grid=(s // t,),
        in_specs=[pl.BlockSpec((t, 256), lambda i: (i, C_GQ // 256)), pl.BlockSpec((t, 256), lambda i: (i, C_GK // 256)),
                  pl.BlockSpec((t, 512), lambda i: (i, C_GV // 512)), pl.BlockSpec((t, PE_W), lambda i: (i, 0)),
                  pl.BlockSpec((PE_W, 256), lambda i: (0, 0)), pl.BlockSpec((1, 256), lambda i: (0, 0))],
        out_specs=[pl.BlockSpec((t, 512), lambda i: (i, 0)), pl.BlockSpec((nc, CHUNK, 512), lambda i: (i, 0, 0))],
        out_shape=[jax.ShapeDtypeStruct((s, 512), F32), jax.ShapeDtypeStruct((s // CHUNK, CHUNK, 512), F32)],
        scratch_shapes=[pltpu.VMEM((256, 512), F32), pltpu.VMEM((t, 256), F32), pltpu.VMEM((nc, 256, 128), F32)],
        name="gla_fwd", compiler_params=_cp("arbitrary"))(pm, pm, pm, pe, wau_p, b_alpha)


def _gla_bwd(pm, pe, wau_p, wau_pt, b_alpha, do, states):
    s = pm.shape[0]
    t = min(s, 1024)
    nc = t // CHUNK
    nb = s // t

    def body(q_ref, k_ref, v_ref, e_ref, wau_ref, waut_ref, ba_ref, do_ref, st_ref,
             dq_ref, dk_ref, dv_ref, de_ref, dwau_ref, dba_ref, gstate, b_scr, db_scr, dec_scr):
        @pl.when(pl.program_id(0) == 0)
        def _():
            gstate[...] = jnp.zeros_like(gstate)
            dwau_ref[...] = jnp.zeros_like(dwau_ref)
            dba_ref[...] = jnp.zeros_like(dba_ref)

        eb = e_ref[...].astype(BF16)
        z = _dot(eb, wau_ref[...]) + ba_ref[...]
        _gla_decays(_log_sigmoid(z) * (1.0 / GLA_TAU), b_scr, dec_scr)
        lmask, hmask, bd = _gla_consts()
        last_row = _iota((CHUNK, 256), 0) == CHUNK - 1

        def chunk(cc, carry):
            c = nc - 1 - cc
            rows = pl.ds(pl.multiple_of(c * CHUNK, CHUNK), CHUNK)
            qc = q_ref[rows, :].astype(F32) * 0.125
            kc = k_ref[rows, :].astype(F32)
            vc = v_ref[rows, :]
            dob = do_ref[rows, :]
            bl, ep, en, ek, qp, qn, kn, kp, kk = _gla_chunk(b_scr[rows, :], qc, kc)
            decb = jnp.concatenate([dec_scr[c]] * 4, axis=1)
            qs = jnp.where(hmask, jnp.concatenate([qp] * 4, axis=0), 0.0).astype(BF16)
            qns = jnp.where(hmask, jnp.concatenate([qn] * 4, axis=0), 0.0).astype(BF16)
            knb, kpb = kn.astype(BF16), kp.astype(BF16)
            attn = jnp.where(lmask, _dot_nt(qs, knb), _dot_nt(qns, kpb)).astype(BF16)
            st = jnp.where(bd, jnp.concatenate([st_ref[c]] * 4, axis=0), 0.0)
            g = gstate[...]
            gb = g.astype(BF16)
            do_s = jnp.where(bd, jnp.concatenate([dob] * 4, axis=0), jnp.zeros((), BF16))
            dattn = _dot_nt(do_s, vc)
            dv_ref[rows, :] = (_dot_tn(attn, do_s) + _dot(kk.astype(BF16), gb)).astype(BF16)
            dac = jnp.where(lmask, dattn, 0.0).astype(BF16)
            daa = jnp.where(lmask, 0.0, dattn).astype(BF16)
            dqp = _fold_heads(jnp.where(hmask, _dot(dac, knb), 0.0)) + _dot_nt(dob, st.astype(BF16))
            dqn = _fold_heads(jnp.where(hmask, _dot(daa, kpb), 0.0))
            dkn = _dot_tn(dac, qs)
            dkp = _dot_tn(daa, qns)
            dkk = _dot_nt(vc, gb)
            ddec = _dot_nt(jnp.ones((8, 1536), BF16), _split3(g * st))[0:1, :]
            gstate[...] = decb * g + jnp.where(bd, _dot_tn(qp.astype(BF16), dob), 0.0)
            dq_ref[rows, :] = ((dqp * ep + dqn * en) * 0.125).astype(BF16)
            dk_ref[rows, :] = (dkn * en + dkp * ep + dkk * ek).astype(BF16)
            dek = dkk * kc * ek
            db = (dqp * qc + dkp * kc) * ep - (dqn * qc + dkn * kc) * en - dek
            dbl = jnp.sum(dek, axis=0, keepdims=True) + ddec * jnp.exp(bl)
            db_scr[rows, :] = db + jnp.where(last_row, dbl, 0.0)
            return carry

        lax.fori_loop(0, nc, chunk, 0)
        triu = (_iota((CHUNK, CHUNK), 0) <= _iota((CHUNK, CHUNK), 1)).astype(BF16)
        dla = jnp.concatenate([_sum3(_dot(triu, _split3(db_scr[CHUNK * c:CHUNK * (c + 1), :])), 1) for c in range(nc)], axis=0)
        dz = dla * (1.0 / GLA_TAU) * _sigmoid(-z)
        dzb = dz.astype(BF16)
        dwau_ref[...] += _dot_tn(eb, dzb)
        dba_ref[...] += _fold8(dz)
        de_ref[...] = _dot(dzb, waut_ref[...])

    rev = lambda i: nb - 1 - i
    return pl.pallas_call(
        body, grid=(nb,),
        in_specs=[pl.BlockSpec((t, 256), lambda i: (rev(i), C_GQ // 256)), pl.BlockSpec((t, 256), lambda i: (rev(i), C_GK // 256)),
                  pl.BlockSpec((t, 512), lambda i: (rev(i), C_GV // 512)), pl.BlockSpec((t, PE_W), lambda i: (rev(i), 0)),
                  pl.BlockSpec((PE_W, 256), lambda i: (0, 0)), pl.BlockSpec((256, PE_W), lambda i: (0, 0)),
                  pl.BlockSpec((1, 256), lambda i: (0, 0)), pl.BlockSpec((t, 512), lambda i: (rev(i), 0)),
                  pl.BlockSpec((nc, CHUNK, 512), lambda i: (rev(i), 0, 0))],
        out_specs=[pl.BlockSpec((t, 256), lambda i: (rev(i), 0)), pl.BlockSpec((t, 256), lambda i: (rev(i), 0)),
                   pl.BlockSpec((t, 512), lambda i: (rev(i), 0)), pl.BlockSpec((t, PE_W), lambda i: (rev(i), 0)),
                   pl.BlockSpec((PE_W, 256), lambda i: (0, 0)), pl.BlockSpec((8, 256), lambda i: (0, 0))],
        out_shape=[jax.ShapeDtypeStruct((s, 256), BF16), jax.ShapeDtypeStruct((s, 256), BF16),
                   jax.ShapeDtypeStruct((s, 512), BF16), jax.ShapeDtypeStruct((s, PE_W), F32),
                   jax.ShapeDtypeStruct((PE_W, 256), F32), jax.ShapeDtypeStruct((8, 256), F32)],
        scratch_shapes=[pltpu.VMEM((256, 512), F32), pltpu.VMEM((t, 256), F32), pltpu.VMEM((t, 256), F32),
                        pltpu.VMEM((nc, 256, 128), F32)],
        name="gla_bwd", compiler_params=_cp("arbitrary"))(pm, pm, pm, pe, wau_p, wau_pt, b_alpha, do, states)


FOX_WIDE = 1024


def _split3(x):
    hi = x.astype(BF16)
    r = x - hi.astype(F32)
    mid = r.astype(BF16)
    lo = (r - mid.astype(F32)).astype(BF16)
    return jnp.concatenate([hi, mid, lo], axis=1)


def _pieces(x, lane0):
    lane = _iota(x.shape, 1)
    x = jnp.where((lane >= lane0) & (lane < lane0 + 8), x, 0.0)
    hi = x.astype(BF16).astype(F32)
    r = x - hi
    mid = r.astype(BF16).astype(F32)
    lo = (r - mid).astype(BF16).astype(F32)
    return (hi + pltpu.roll(mid, 8, 1) + pltpu.roll(lo, 16, 1)).astype(BF16)


def _sum3(x, axis):
    n = x.shape[axis] // 3
    parts = [lax.slice_in_dim(x, n * p, n * (p + 1), axis=axis) for p in range(3)]
    return (parts[0] + parts[1]) + parts[2]


def _spread(x, sp):
    return jnp.concatenate([_dot(x[:, 128 * g:128 * (g + 1)], sp) for g in range(4)], axis=1)


def _fox_tables():
    heads, lane = np.arange(8), np.arange(64)
    spread = np.zeros((128, 256), np.float32)
    spread[(64 * heads[:2, None] + lane).ravel(), (128 * heads[:2, None] + lane).ravel()] = 1.0
    def place(src_lane0, dst_off, val):
        t = np.zeros((128, 1024), np.float32)
        for p in range(3):
            for src in {src_lane0 + 8 * p, (src_lane0 - 8 * p) % 128}:
                t[src + heads, 128 * heads + dst_off + p] = val
        return t
    def const(off, val):
        c = np.zeros((1, 1024), np.float32)
        for p in range(3):
            c[0, 128 * heads + off + p] = val
        return c
    rows = np.zeros((8, 128), np.float32)
    rows[heads, FF_LANE + heads] = 1.0
    bf = lambda a: jnp.asarray(a, BF16)
    return dict(spread=bf(spread),
                f_to_q=bf(place(FF_LANE, 64, 1.0)), f_to_k=bf(place(FF_LANE, 67, -1.0)), d_to_do=bf(place(0, 64, 1.0)),
                ones_q=jnp.asarray(const(67, 1.0)), ones_k=jnp.asarray(const(64, 1.0)), ones_v=jnp.asarray(const(64, -1.0)),
                rows=jnp.asarray(rows))


LOG2E = 1.4426950408889634


def _fox_prep(pm, f128, lse8, tb, *, backward):
    s = pm.shape[0]
    tm = min(s, 1024) if backward else _row_tile(s)

    def body(*refs):
        if backward:
            q_ref, f_ref, lse_ref, sp_ref, fq_ref, cq_ref, rows_ref, qa_ref = refs
            f = f_ref[...] * LOG2E - _dot_tn(lse_ref[...], rows_ref[...], precision=HI)
            q2 = (q_ref[...].astype(F32) * (0.125 * LOG2E)).astype(BF16)
            qa_ref[...] = (_spread(q2, sp_ref[...]) + _dot(_pieces(f, FF_LANE), fq_ref[...]) + cq_ref[...]).astype(BF16)
            return
        (q_ref, k_ref, v_ref, e_ref, b_ref, sp_ref, fq_ref, fk_ref, cq_ref, ck_ref, cv_ref,
         qa_ref, ka_ref, va_ref, vt_ref, qt_ref, kt_ref, f_ref, carry) = refs

        @pl.when(pl.program_id(0) == 0)
        def _():
            carry[...] = jnp.zeros_like(carry)

        lf = _log_sigmoid(e_ref[...] + b_ref[...])
        tri = (_iota((tm, tm), 0) >= _iota((tm, tm), 1)).astype(BF16)
        f = _sum3(_dot(tri, _split3(lf)), 1) + carry[0:1, :]
        f_ref[...] = f
        carry[...] = jnp.broadcast_to(f[tm - 1:tm, :], carry.shape)
        f3 = _pieces(f * LOG2E, FF_LANE)
        q, k, v = q_ref[...].astype(F32), k_ref[...], v_ref[...]
        sp = sp_ref[...]
        qa_ref[...] = (_spread((q * (0.125 * LOG2E)).astype(BF16), sp) + _dot(f3, fq_ref[...]) + cq_ref[...]).astype(BF16)
        ka_ref[...] = (_spread(k, sp) + _dot(f3, fk_ref[...]) + ck_ref[...]).astype(BF16)
        va_ref[...] = (_spread(v, sp) + cv_ref[...]).astype(BF16)
        vt_ref[...] = v.T
        qt_ref[...] = (q * 0.125).astype(BF16).T
        kt_ref[...] = (k.astype(F32) * 0.125).astype(BF16).T

    row = lambda i: (i, 0)
    const = lambda i: (0, 0)
    blk = lambda c: pl.BlockSpec((tm, 512), lambda i: (i, c // 512))
    wide = pl.BlockSpec((tm, 1024), row)
    mat = lambda a: pl.BlockSpec(a.shape, const)
    if backward:
        ins = [pm, f128, lse8, tb["spread"], tb["f_to_q"], tb["ones_q"], tb["rows"]]
        in_specs = [blk(C_FQ), pl.BlockSpec((tm, 128), row), pl.BlockSpec((8, tm), lambda i: (0, i))] + [mat(a) for a in ins[3:]]
        out_specs, out_shape = wide, jax.ShapeDtypeStruct((s, 1024), BF16)
    else:
        pe, bias = f128
        ins = [pm, pm, pm, pe, bias, tb["spread"], tb["f_to_q"], tb["f_to_k"], tb["ones_q"], tb["ones_k"], tb["ones_v"]]
        in_specs = [blk(C_FQ), blk(C_FK), blk(C_FV), pl.BlockSpec((tm, 128), row)] + [mat(a) for a in ins[4:]]
        tr = pl.BlockSpec((512, tm), lambda i: (0, i))
        out_specs = [wide, wide, wide, tr, tr, tr, pl.BlockSpec((tm, 128), row)]
        out_shape = ([jax.ShapeDtypeStruct((s, 1024), BF16)] * 3 + [jax.ShapeDtypeStruct((512, s), BF16)] * 3
                     + [jax.ShapeDtypeStruct((s, 128), F32)])
    return pl.pallas_call(body, grid=(s // tm,), in_specs=in_specs, out_specs=out_specs, out_shape=out_shape,
                          scratch_shapes=[] if backward else [pltpu.VMEM((8, 128), F32)],
                          name="fox_prep_bwd" if backward else "fox_prep",
                          compiler_params=_cp("parallel" if backward else "arbitrary"))(*ins)


def _fox_post(dqt, dkt, dvt, rowsum8, colsum8, pe, bias, tb):
    s = dqt.shape[1]
    tm = min(s, 1024)
    nb = s // tm

    def body(dqt_ref, dkt_ref, dvt_ref, rs_ref, cs_ref, e_ref, b_ref, rows_ref, dfq_ref, dfk_ref, dfv_ref, de_ref, db_ref, carry):
        @pl.when(pl.program_id(0) == 0)
        def _():
            carry[...] = jnp.zeros_like(carry)
            db_ref[...] = jnp.zeros_like(db_ref)

        dfq_ref[...] = dqt_ref[...].T.astype(BF16)
        dfk_ref[...] = dkt_ref[...].T
        dfv_ref[...] = dvt_ref[...].T
        df = _dot_tn(rs_ref[...] - cs_ref[...], rows_ref[...], precision=HI)
        triu = (_iota((tm, tm), 0) <= _iota((tm, tm), 1)).astype(BF16)
        dlf = _sum3(_dot(triu, _split3(df)), 1) + carry[0:1, :]
        carry[...] = jnp.broadcast_to(dlf[0:1, :], carry.shape)
        lane = _iota((tm, PE_W), 1)
        dff = jnp.where((lane >= FF_LANE) & (lane < FF_LANE + 8), dlf * _sigmoid(-(e_ref[...] + b_ref[...])), 0.0)
        de_ref[...] = dff
        db_ref[...] += _fold8(dff)

    rev = lambda i: (nb - 1 - i, 0)
    tr = pl.BlockSpec((512, tm), lambda i: (0, nb - 1 - i))
    out = pl.BlockSpec((tm, 512), rev)
    heads = pl.BlockSpec((8, tm), lambda i: (0, nb - 1 - i))
    const = lambda i: (0, 0)
    return pl.pallas_call(
        body, grid=(nb,),
        in_specs=[tr, tr, tr, heads, heads, pl.BlockSpec((tm, PE_W), rev), pl.BlockSpec((1, PE_W), const),
                  pl.BlockSpec((8, 128), const)],
        out_specs=[out, out, out, pl.BlockSpec((tm, PE_W), rev), pl.BlockSpec((8, PE_W), const)],
        out_shape=[jax.ShapeDtypeStruct((s, 512), BF16)] * 3 + [jax.ShapeDtypeStruct((s, PE_W), F32),
                                                                 jax.ShapeDtypeStruct((8, PE_W), F32)],
        scratch_shapes=[pltpu.VMEM((8, PE_W), F32)],
        name="fox_post", compiler_params=_cp("arbitrary"))(dqt, dkt, dvt, rowsum8, colsum8, pe, bias, tb["rows"])


def _fox_fwd(k_aug, q_aug, vt):
    s = k_aug.shape[0]
    nh = 8
    tk = _row_tile(s)
    tq = min(s, 2 * FOX_WIDE)
    per = tq // tk

    def body(k_ref, q_ref, v_ref, o_ref, lse_ref, sbuf):
        i = pl.program_id(1)
        qa = q_ref[...]

        def scores(j):
            return _dot_nt(k_ref[pl.ds(pl.multiple_of(j * tk, tk), tk), :], qa)

        ones_row = (_iota((16, tk), 0) == 0).astype(BF16)

        def update(st, j, carry):
            m, acc = carry
            m2 = jnp.maximum(m, jnp.max(st, axis=0, keepdims=True))
            p = jnp.exp2(st - m2)
            vj = jnp.concatenate([v_ref[:, pl.ds(pl.multiple_of(j * tk, tk), tk)], ones_row], axis=0)
            return m2, jnp.exp2(m - m2) * acc + _dot(vj, p.astype(BF16))

        def step(a, carry):
            sbuf[1] = scores(2 * a + 1)
            carry = update(sbuf[0], 2 * a, carry)
            sbuf[0] = scores(2 * a + 2)
            return update(sbuf[1], 2 * a + 1, carry)

        n = i * per
        sbuf[0] = scores(0)
        carry = (jnp.full((1, tq), -1e30, F32), jnp.zeros((80, tq), F32))
        carry = lax.fori_loop(0, n // 2, step, carry)
        tri = _iota((tk, tk), 0) <= _iota((tk, tk), 1)
        late = [_dot_nt(k_ref[pl.ds(pl.multiple_of((n + r) * tk, tk), tk), :], qa[r * tk:, :]) for r in range(1, per)]
        for r in range(per):
            st = sbuf[0] if r == 0 else late[r - 1]
            head = jnp.where(tri, st[:, :tk], -1e30)
            st = head if st.shape[1] == tk else jnp.concatenate([head, st[:, tk:]], axis=1)
            part = update(st, n + r, tuple(c[:, r * tk:] for c in carry))
            carry = part if r == 0 else tuple(jnp.concatenate([old[:, :r * tk], new], axis=1) for old, new in zip(carry, part))
        m, acc = carry
        l = acc[64:65]
        o_ref[...] = (acc[0:64] / l).astype(BF16)
        lse_ref[0] = m + jnp.log2(l)

    return pl.pallas_call(
        body, grid=(nh, s // tq),
        in_specs=[pl.BlockSpec((s, 128), lambda h, i: (0, h)), pl.BlockSpec((tq, 128), lambda h, i: (i, h)),
                  pl.BlockSpec((64, s), lambda h, i: (h, 0))],
        out_specs=[pl.BlockSpec((64, tq), lambda h, i: (h, i)), pl.BlockSpec((1, 1, tq), lambda h, i: (h, 0, i))],
        out_shape=[jax.ShapeDtypeStruct((512, s), BF16), jax.ShapeDtypeStruct((nh, 1, s), F32)],
        scratch_shapes=[pltpu.VMEM((2, tk, tq), F32)],
        name="fox_fwd", compiler_params=_cp("parallel", "arbitrary"))(k_aug, q_aug, vt)


def _fox_bwd(q_aug, do_aug, qt, dot_, k_aug, v_aug, kt):
    s = q_aug.shape[0]
    nh = 8
    tq = _row_tile(s)
    tk = min(s, 2 * FOX_WIDE)
    per = tk // tq
    nqb = s // tq

    def body(qa_ref, da_ref, qt_ref, dt_ref, ka_ref, va_ref, kt_ref, dq_ref, rs_ref, dk_ref, dv_ref, dfk_ref):
        j = pl.program_id(1)

        @pl.when(j == 0)
        def _():
            dq_ref[...] = jnp.zeros_like(dq_ref)
            rs_ref[...] = jnp.zeros_like(rs_ref)

        ones_row = (_iota((16, tk), 0) == 0).astype(BF16)
        ka, va = ka_ref[...], va_ref[...]
        ks = jnp.concatenate([kt_ref[...], ones_row], axis=0)
        tri = _iota((tq, tq), 0) >= _iota((tq, tq), 1)

        def tile(i, w, carry):
            masked = w is not None
            w = tk if w is None else w
            rows = pl.ds(pl.multiple_of(i * tq, tq), tq)
            sp = _dot_nt(qa_ref[rows, :], ka[:w])
            if masked:
                last = jnp.where(tri, sp[:, w - tq:], -1e30)
                sp = last if w == tq else jnp.concatenate([sp[:, :w - tq], last], axis=1)
            p = jnp.exp2(sp)
            dsb = (p * _dot_nt(da_ref[rows, :], va[:w])).astype(BF16)
            dq = _dot_nt(ks[:, :w], dsb)
            dq_ref[:, rows] += dq[0:64]
            rs_ref[0, :, rows] += dq[64:72]
            new = (_dot(jnp.concatenate([qt_ref[:, rows], ones_row[:, :tq]], axis=0), dsb), _dot(dt_ref[:, rows], p.astype(BF16)))
            if w == tk:
                return tuple(c + d for c, d in zip(carry, new))
            return tuple(jnp.concatenate([c[:, :w] + d, c[:, w:]], axis=1) for c, d in zip(carry, new))

        carry = (jnp.zeros((80, tk), F32), jnp.zeros((64, tk), F32))
        for r in range(per):
            carry = tile(j * per + r, (r + 1) * tq, carry)
        dk, dv = lax.fori_loop((j + 1) * per, nqb, lambda i, c: tile(i, None, c), carry)
        dk_ref[...] = dk[0:64].astype(BF16)
        dv_ref[...] = dv.astype(BF16)
        dfk_ref[0] = dk[64:65]

    head_cols = lambda h, j: (0, h)
    head_rows = lambda h, j: (h, 0)
    once = dict(pipeline_mode=pl.Buffered(1))
    return pl.pallas_call(
        body, grid=(nh, s // tk),
        in_specs=[pl.BlockSpec((s, 128), head_cols, **once), pl.BlockSpec((s, 128), head_cols, **once),
                  pl.BlockSpec((64, s), head_rows, **once), pl.BlockSpec((64, s), head_rows, **once),
                  pl.BlockSpec((tk, 128), lambda h, j: (j, h)), pl.BlockSpec((tk, 128), lambda h, j: (j, h)),
                  pl.BlockSpec((64, tk), lambda h, j: (h, j))],
        out_specs=[pl.BlockSpec((64, s), head_rows), pl.BlockSpec((1, 8, s), lambda h, j: (h, 0, 0)),
                   pl.BlockSpec((64, tk), lambda h, j: (h, j)),
                   pl.BlockSpec((64, tk), lambda h, j: (h, j)), pl.BlockSpec((1, 1, tk), lambda h, j: (h, 0, j))],
        out_shape=[jax.ShapeDtypeStruct((512, s), F32), jax.ShapeDtypeStruct((nh, 8, s), F32),
                   jax.ShapeDtypeStruct((512, s), BF16),
                   jax.ShapeDtypeStruct((512, s), BF16), jax.ShapeDtypeStruct((nh, 1, s), F32)],
        name="fox_bwd", compiler_params=_cp("parallel", "arbitrary"))(q_aug, do_aug, qt, dot_, k_aug, v_aug, kt)


MEM_SCALE = 128 ** -0.5


def _mem_attn_fwd(pm, mkv):
    s = pm.shape[0]
    t = min(s, 1024)
    nm = mkv.shape[0]

    def body(q_ref, mk_ref, mv_ref, o_ref):
        for h in range(4):
            cols = slice(128 * h, 128 * (h + 1))
            sc = _dot_nt(q_ref[:, cols], mk_ref[:, cols]) * MEM_SCALE
            p = jnp.exp(sc - jnp.max(sc, axis=-1, keepdims=True))
            p = p / jnp.sum(p, axis=-1, keepdims=True)
            o_ref[:, cols] = _dot(p.astype(BF16), mv_ref[:, cols]).astype(BF16)

    return pl.pallas_call(
        body, grid=(s // t,),
        in_specs=[pl.BlockSpec((t, 512), lambda i: (i, C_MQ // 512)), pl.BlockSpec((nm, 512), lambda i: (0, 0)),
                  pl.BlockSpec((nm, 512), lambda i: (0, 1))],
        out_specs=pl.BlockSpec((t, 512), lambda i: (i, 0)),
        out_shape=jax.ShapeDtypeStruct((s, 512), BF16),
        name="mem_attn_fwd", compiler_params=_cp("parallel"))(pm, mkv, mkv)


def _mem_attn_bwd(pm, mkv, do):
    s = pm.shape[0]
    t = min(s, 1024)
    nm = mkv.shape[0]

    def body(q_ref, mk_ref, mv_ref, do_ref, dq_ref, dmk_ref, dmv_ref):
        @pl.when(pl.program_id(0) == 0)
        def _():
            dmk_ref[...] = jnp.zeros_like(dmk_ref)
            dmv_ref[...] = jnp.zeros_like(dmv_ref)

        for h in range(4):
            cols = slice(128 * h, 128 * (h + 1))
            qh, kh, vh, doh = q_ref[:, cols], mk_ref[:, cols], mv_ref[:, cols], do_ref[:, cols]
            sc = _dot_nt(qh, kh) * MEM_SCALE
            p = jnp.exp(sc - jnp.max(sc, axis=-1, keepdims=True))
            p = p / jnp.sum(p, axis=-1, keepdims=True)
            pb = p.astype(BF16)
            dp = _dot_nt(doh, vh)
            ds = (p * (dp - jnp.sum(p * dp, axis=-1, keepdims=True)) * MEM_SCALE).astype(BF16)
            dq_ref[:, cols] = _dot(ds, kh).astype(BF16)
            dmk_ref[:, cols] += _dot_tn(ds, qh)
            dmv_ref[:, cols] += _dot_tn(pb, doh)

    return pl.pallas_call(
        body, grid=(s // t,),
        in_specs=[pl.BlockSpec((t, 512), lambda i: (i, C_MQ // 512)), pl.BlockSpec((nm, 512), lambda i: (0, 0)),
                  pl.BlockSpec((nm, 512), lambda i: (0, 1)), pl.BlockSpec((t, 512), lambda i: (i, 0))],
        out_specs=[pl.BlockSpec((t, 512), lambda i: (i, 0)), pl.BlockSpec((nm, 512), lambda i: (0, 0)),
                   pl.BlockSpec((nm, 512), lambda i: (0, 0))],
        out_shape=[jax.ShapeDtypeStruct((s, 512), BF16), jax.ShapeDtypeStruct((nm, 512), F32),
                   jax.ShapeDtypeStruct((nm, 512), F32)],
        name="mem_attn_bwd", compiler_params=_cp("arbitrary"))(pm, mkv, mkv, do)


def _gain_grad(dxn_g, x, r, name):
    m, d = x.shape

    def body(d_ref, x_ref, r_ref, o_ref):
        o_ref[...] = _fold8(d_ref[...] * (x_ref[...] * r_ref[...]))

    return pl.pallas_call(body, out_shape=jax.ShapeDtypeStruct((8, d), F32), name=name,
                          compiler_params=pltpu.CompilerParams(vmem_limit_bytes=VMEM_LIMIT_BYTES))(dxn_g, x, r)


def _head_norm(o, gh):
    xs, rs = [], []
    for h in range(4):
        oh = o[:, 128 * h:128 * (h + 1)]
        r = lax.rsqrt(jnp.mean(oh * oh, axis=-1, keepdims=True) + EPS)
        xs.append(oh * r)
        rs.append(r)
    return xs, rs


def _merge_fwd(x, pm, o_gla, o_fox_t, o_mem, g_head, wg, wf, wm, wo, g_ffn):
    s = x.shape[0]
    t = min(s, 512)

    def body(x_ref, g0_ref, g1_ref, g2_ref, gg_ref, og_ref, of_ref, om_ref, gh_ref, wg_ref, wf_ref, wm_ref, wo_ref, gf_ref,
             mg_ref, h1_ref, u2_ref, r2_ref):
        xs, _ = _head_norm(og_ref[...], None)
        gg = gg_ref[...].astype(F32)
        sil = gg * _sigmoid(gg)
        ogn = jnp.concatenate(xs, axis=1) * gh_ref[...] * sil
        merged = (_sigmoid(g0_ref[...].astype(F32)) * _dot(ogn.astype(BF16), wg_ref[...])
                  + _sigmoid(g1_ref[...].astype(F32)) * _dot(of_ref[...].T, wf_ref[...])
                  + _sigmoid(g2_ref[...].astype(F32)) * _dot(om_ref[...], wm_ref[...]))
        mb = merged.astype(BF16)
        mg_ref[...] = mb
        h1 = x_ref[...] + _dot(mb, wo_ref[...])
        h1_ref[...] = h1
        r = lax.rsqrt(jnp.mean(h1 * h1, axis=-1, keepdims=True) + EPS)
        u2_ref[...] = ((h1 * r) * gf_ref[...]).astype(BF16)
        r2_ref[...] = r

    row = lambda i: (i, 0)
    const = lambda i: (0, 0)
    return pl.pallas_call(
        body, grid=(s // t,),
        in_specs=[pl.BlockSpec((t, D), row), pl.BlockSpec((t, D), lambda i: (i, 0)), pl.BlockSpec((t, D), lambda i: (i, 1)),
                  pl.BlockSpec((t, D), lambda i: (i, 2)), pl.BlockSpec((t, 512), lambda i: (i, C_GG // 512)),
                  pl.BlockSpec((t, 512), row), pl.BlockSpec((512, t), lambda i: (0, i)), pl.BlockSpec((t, 512), row),
                  pl.BlockSpec((1, 512), const), pl.BlockSpec((512, D), const), pl.BlockSpec((512, D), const),
                  pl.BlockSpec((512, D), const), pl.BlockSpec((D, D), const), pl.BlockSpec((1, D), const)],
        out_specs=[pl.BlockSpec((t, D), row), pl.BlockSpec((t, D), row), pl.BlockSpec((t, D), row), pl.BlockSpec((t, 1), row)],
        out_shape=[jax.ShapeDtypeStruct((s, D), BF16), jax.ShapeDtypeStruct((s, D), F32),
                   jax.ShapeDtypeStruct((s, D), BF16), jax.ShapeDtypeStruct((s, 1), F32)],
        name="merge_fwd", compiler_params=_cp("parallel"))(x, pm, pm, pm, pm, o_gla, o_fox_t, o_mem, g_head, wg, wf, wm, wo, g_ffn)


def _merge_bwd(dh1b, pm, o_gla, o_fox_t, o_mem, g_head, wg, wf, wm, wgt, wft, wmt, wot, spread, d_to_do):
    s = dh1b.shape[0]
    t = min(s, 256)

    def body(dh_ref, g0_ref, g1_ref, g2_ref, gg_ref, og_ref, of_ref, om_ref, gh_ref, wg_ref, wf_ref, wm_ref,
             wgt_ref, wft_ref, wmt_ref, wot_ref, sp_ref, dd_ref,
             dgt_ref, dgg_ref, dog_ref, da_ref, dot_ref, dom_ref, dwg_ref, dwf_ref, dwm_ref, dgh_ref):
        @pl.when(pl.program_id(0) == 0)
        def _():
            dwg_ref[...] = jnp.zeros_like(dwg_ref)
            dwf_ref[...] = jnp.zeros_like(dwf_ref)
            dwm_ref[...] = jnp.zeros_like(dwm_ref)
            dgh_ref[...] = jnp.zeros_like(dgh_ref)

        dmerged = _dot(dh_ref[...], wot_ref[...])
        og = og_ref[...]
        xs, rs = _head_norm(og, None)
        on = jnp.concatenate(xs, axis=1)
        gg = gg_ref[...].astype(F32)
        sg = _sigmoid(gg)
        sil = gg * sg
        gh = gh_ref[...]
        ognb = (on * gh * sil).astype(BF16)
        ofb, omb = of_ref[...].T, om_ref[...]
        douts = []
        for idx, (gref, ob, w_ref, wt_ref, dw_ref) in enumerate((
                (g0_ref, ognb, wg_ref, wgt_ref, dwg_ref), (g1_ref, ofb, wf_ref, wft_ref, dwf_ref),
                (g2_ref, omb, wm_ref, wmt_ref, dwm_ref))):
            gt = _sigmoid(gref[...].astype(F32))
            y = _dot(ob, w_ref[...])
            dgt_ref[:, D * idx:D * (idx + 1)] = (dmerged * y * gt * (1.0 - gt)).astype(BF16)
            dy = (gt * dmerged).astype(BF16)
            dw_ref[...] += _dot_tn(ob, dy)
            douts.append(_dot(dy, wt_ref[...]))
        dogn, dof, dom = douts
        dofb = dof.astype(BF16)
        dom_ref[...] = dom.astype(BF16)
        ind = (_iota((1536, 128), 0) % 512 // 64 == _iota((1536, 128), 1)).astype(BF16)
        delta = _dot(_split3(dofb.astype(F32) * ofb.astype(F32)), ind)
        da_ref[...] = (_spread(dofb, sp_ref[...]) + _dot(_pieces(delta, 0), dd_ref[...])).astype(BF16)
        dot_ref[...] = dofb.T
        dgg_ref[...] = (dogn * on * gh * (sg * (1.0 + gg * (1.0 - sg)))).astype(BF16)
        d_on = dogn * sil
        dgh_ref[...] += _fold8(d_on * on)
        dxn = d_on * gh
        outs = []
        for h in range(4):
            cols = slice(128 * h, 128 * (h + 1))
            dh_, xh = dxn[:, cols], xs[h]
            outs.append(rs[h] * (dh_ - xh * jnp.mean(dh_ * xh, axis=-1, keepdims=True)))
        dog_ref[...] = jnp.concatenate(outs, axis=1).astype(BF16)

    row = lambda i: (i, 0)
    const = lambda i: (0, 0)
    return pl.pallas_call(
        body, grid=(s // t,),
        in_specs=[pl.BlockSpec((t, D), row), pl.BlockSpec((t, D), lambda i: (i, 0)), pl.BlockSpec((t, D), lambda i: (i, 1)),
                  pl.BlockSpec((t, D), lambda i: (i, 2)), pl.BlockSpec((t, 512), lambda i: (i, C_GG // 512)),
                  pl.BlockSpec((t, 512), row), pl.BlockSpec((512, t), lambda i: (0, i)), pl.BlockSpec((t, 512), row),
                  pl.BlockSpec((1, 512), const), pl.BlockSpec((512, D), const), pl.BlockSpec((512, D), const),
                  pl.BlockSpec((512, D), const), pl.BlockSpec((D, 512), const), pl.BlockSpec((D, 512), const),
                  pl.BlockSpec((D, 512), const), pl.BlockSpec((D, D), const),
                  pl.BlockSpec((128, 256), const), pl.BlockSpec((128, 1024), const)],
        out_specs=[pl.BlockSpec((t, 3 * D), row), pl.BlockSpec((t, 512), row), pl.BlockSpec((t, 512), row),
                   pl.BlockSpec((t, 1024), row), pl.BlockSpec((512, t), lambda i: (0, i)), pl.BlockSpec((t, 512), row),
                   pl.BlockSpec((512, D), const), pl.BlockSpec((512, D), const), pl.BlockSpec((512, D), const),
                   pl.BlockSpec((8, 512), const)],
        out_shape=[jax.ShapeDtypeStruct((s, 3 * D), BF16), jax.ShapeDtypeStruct((s, 512), BF16),
                   jax.ShapeDtypeStruct((s, 512), BF16), jax.ShapeDtypeStruct((s, 1024), BF16),
                   jax.ShapeDtypeStruct((512, s), BF16), jax.ShapeDtypeStruct((s, 512), BF16),
                   jax.ShapeDtypeStruct((512, D), F32), jax.ShapeDtypeStruct((512, D), F32),
                   jax.ShapeDtypeStruct((512, D), F32), jax.ShapeDtypeStruct((8, 512), F32)],
        name="merge_bwd", compiler_params=_cp("arbitrary"))(
            dh1b, pm, pm, pm, pm, o_gla, o_fox_t, o_mem, g_head, wg, wf, wm, wgt, wft, wmt, wot, spread, d_to_do)


def _ff2_loss(a, w2, h1, g_final, target):
    s, k = a.shape
    tm = min(s, 512)

    def body(a_ref, w_ref, h1_ref, g_ref, t_ref, dh_ref, dhb_ref, loss_ref, dg_ref):
        @pl.when(pl.program_id(0) == 0)
        def _():
            loss_ref[...] = jnp.zeros_like(loss_ref)
            dg_ref[...] = jnp.zeros_like(dg_ref)

        h2 = h1_ref[...] + _dot(_relu2_bf16(a_ref[...]), w_ref[...])
        r = lax.rsqrt(jnp.mean(h2 * h2, axis=-1, keepdims=True) + EPS)
        xn = h2 * r
        g = g_ref[...]
        err = xn * g - t_ref[...]
        e2 = _fold8(err * err)
        part = e2[:, 0:128]
        for c in range(1, D // 128):
            part = part + e2[:, 128 * c:128 * (c + 1)]
        loss_ref[...] += part
        dy = err * (1.0 / D)
        dg_ref[...] += _fold8(dy * xn)
        dxn = dy * g
        dh = r * (dxn - xn * jnp.mean(dxn * xn, axis=-1, keepdims=True))
        dh_ref[...] = dh
        dhb_ref[...] = dh.astype(BF16)

    row = lambda i: (i, 0)
    const = lambda i: (0, 0)
    return pl.pallas_call(
        body, grid=(s // tm,),
        in_specs=[pl.BlockSpec((tm, k), row), pl.BlockSpec((k, D), const, pipeline_mode=pl.Buffered(1)),
                  pl.BlockSpec((tm, D), row), pl.BlockSpec((1, D), const), pl.BlockSpec((tm, D), row)],
        out_specs=[pl.BlockSpec((tm, D), row), pl.BlockSpec((tm, D), row), pl.BlockSpec((8, 128), const),
                   pl.BlockSpec((8, D), const)],
        out_shape=[jax.ShapeDtypeStruct((s, D), F32), jax.ShapeDtypeStruct((s, D), BF16),
                   jax.ShapeDtypeStruct((8, 128), F32), jax.ShapeDtypeStruct((8, D), F32)],
        name="ff2_loss", compiler_params=_cp("arbitrary"))(a, w2, h1, g_final, target)


def _adam(w, g, m, v, name):
    _, r, c = w.shape
    tr = r
    for cand in (512, 256, 128, 64, 32, 16, 8):
        if r % cand == 0 and cand * c * 4 <= (1 << 20):
            tr = cand
            break
    c1 = 1.0 - ADAM_B1 ** ADAM_STEP
    c2 = 1.0 - ADAM_B2 ** ADAM_STEP

    def body(w_ref, g_ref, m_ref, v_ref, d_ref, nm_ref, nv_ref):
        gv = g_ref[...]
        nm = ADAM_B1 * m_ref[...] + (1.0 - ADAM_B1) * gv
        nv = ADAM_B2 * v_ref[...] + (1.0 - ADAM_B2) * (gv * gv)
        d_ref[...] = -ADAM_LR * ((nm / c1) / (jnp.sqrt(nv / c2) + ADAM_EPS) + ADAM_WD * w_ref[...])
        nm_ref[...] = nm
        nv_ref[...] = nv

    spec = pl.BlockSpec((1, tr, c), lambda i: (0, i, 0))
    return pl.pallas_call(
        body, grid=(r // tr,), in_specs=[spec] * 4, out_specs=[spec] * 3,
        out_shape=[jax.ShapeDtypeStruct((1, r, c), F32)] * 3, name=name, compiler_params=_cp("parallel"))(w, g, m, v)


def _row_block(r):
    return max(d for d in range(16, 513, 16) if r % d == 0)


def _add_half(core, a, b, name):
    n, r, c = b.shape
    tr = _row_block(r)

    def body(core_ref, a_ref, b_ref, o_ref):
        o_ref[...] = (a_ref[...].astype(F32) + b_ref[...].astype(F32)).astype(BF16)

    spec = pl.BlockSpec((1, tr, c), lambda k, i, core_ref: (k, i, 0))
    half = pl.BlockSpec((1, tr, c), lambda k, i, core_ref: (k, i + core_ref[0] * (r // tr), 0))
    return pl.pallas_call(
        body, grid_spec=pltpu.PrefetchScalarGridSpec(num_scalar_prefetch=1, grid=(n, r // tr), in_specs=[half, spec],
                                                     out_specs=spec),
        out_shape=jax.ShapeDtypeStruct((n, r, c), BF16), name=name, compiler_params=_cp("parallel", "parallel"))(core, a, b)


def _sum4(a, name):
    _, r, c = a.shape
    tr = _row_block(r)

    def body(a_ref, o_ref):
        o_ref[...] = ((a_ref[0].astype(F32) + a_ref[1].astype(F32)) + a_ref[2].astype(F32)) + a_ref[3].astype(F32)

    return pl.pallas_call(body, grid=(r // tr,), in_specs=[pl.BlockSpec((4, tr, c), lambda i: (0, i, 0))],
                          out_specs=pl.BlockSpec((tr, c), lambda i: (i, 0)),
                          out_shape=jax.ShapeDtypeStruct((r, c), F32), name=name, compiler_params=_cp("parallel"))(a)


def _adam_small(w, gathered, m, v):
    c1 = 1.0 - ADAM_B1 ** ADAM_STEP
    c2 = 1.0 - ADAM_B2 ** ADAM_STEP

    def body(w_ref, g_ref, m_ref, v_ref, gs_ref, d_ref, nm_ref, nv_ref):
        gv = g_ref[0]
        for dev in range(1, N_DEV):
            gv = gv + g_ref[dev]
        gs_ref[...] = gv
        nm = ADAM_B1 * m_ref[...] + (1.0 - ADAM_B1) * gv
        nv = ADAM_B2 * v_ref[...] + (1.0 - ADAM_B2) * (gv * gv)
        d_ref[...] = -ADAM_LR * ((nm / c1) / (jnp.sqrt(nv / c2) + ADAM_EPS) + ADAM_WD * w_ref[...])
        nm_ref[...] = nm
        nv_ref[...] = nv

    return pl.pallas_call(body, out_shape=[jax.ShapeDtypeStruct((8, D), F32)] * 4, name="adam_small")(w, gathered, m, v)


def _place():
    return lax.axis_index("x"), lax.axis_index("y"), lax.axis_index("c")


def _other_chips(x, y):
    return [(1 - x, y), (x, 1 - y), (1 - x, 1 - y)]


GATHER_SEMS = [pltpu.SemaphoreType.DMA((6,)), pltpu.SemaphoreType.DMA((6,)), pltpu.SemaphoreType.DMA]


def _gather_ops(in_refs, out_refs, sems):
    (p_ref,), (out_ref,) = in_refs, out_refs
    send_sems, recv_sems, local_sem = sems
    hr = p_ref.shape[0] // 2
    x, y, cc = _place()
    sibling = (x, y, 1 - cc)
    chips = _other_chips(x, y)

    def half(chip, core):
        return out_ref.at[2 * chip[0] + chip[1], pl.ds(core * hr, hr), :]

    def copy(k, chip, core, to, src=None):
        return pltpu.make_async_remote_copy(
            src_ref=half(chip, core) if src is None else src, dst_ref=half(chip, core),
            send_sem=send_sems.at[k], recv_sem=recv_sems.at[k], device_id=to, device_id_type=MESH)

    mine = pltpu.make_async_copy(p_ref, out_ref.at[2 * x + y], local_sem)
    my_half = p_ref.at[pl.ds(cc * hr, hr), :]
    first = [copy(j, (x, y), cc, (*chip, cc), src=my_half) for j, chip in enumerate(chips)]
    passed = [copy(3 + j, chip, cc, sibling) for j, chip in enumerate(chips)]

    def start():
        mine.start()
        for cp in first:
            cp.start()

    def forward():
        for j, chip in enumerate(chips):
            copy(j, chip, cc, (x, y, cc)).wait_recv()
            passed[j].start()

    def finish():
        for j, chip in enumerate(chips):
            copy(3 + j, chip, 1 - cc, (x, y, cc)).wait_recv()
        for cp in first + passed:
            cp.wait_send()
        mine.wait()

    return start, forward, finish


def _gather_side(p):
    return _Side([p], [jax.ShapeDtypeStruct((N_CHIPS,) + p.shape, p.dtype)], GATHER_SEMS, _gather_ops)


def _swap_halves(g):
    n, r, c = g.shape
    hr = r // 2

    def body(g_ref, out_ref, send_sem, recv_sem):
        x, y, cc = _place()
        cp = pltpu.make_async_remote_copy(
            src_ref=g_ref.at[:, pl.ds((1 - cc) * hr, hr), :], dst_ref=out_ref,
            send_sem=send_sem, recv_sem=recv_sem, device_id=(x, y, 1 - cc), device_id_type=MESH)
        cp.start()
        cp.wait()

    any_spec = pl.BlockSpec(memory_space=pl.ANY)
    return pl.pallas_call(
        body, out_shape=jax.ShapeDtypeStruct((n, hr, c), g.dtype), in_specs=[any_spec], out_specs=any_spec,
        scratch_shapes=[pltpu.SemaphoreType.DMA, pltpu.SemaphoreType.DMA], name="swap_halves")(g)


SCATTER_SEMS = [pltpu.SemaphoreType.DMA((7,)), pltpu.SemaphoreType.DMA((7,)), pltpu.SemaphoreType.DMA]


def _scatter_ops(in_refs, out_refs, sems):
    (p_ref,), (out_ref,) = in_refs, out_refs
    send_sems, recv_sems, local_sem = sems
    hr = p_ref.shape[1]
    x, y, cc = _place()
    me = 2 * x + y
    sibling = (x, y, 1 - cc)
    chips = _other_chips(x, y)
    ids = [2 * chip[0] + chip[1] for chip in chips]

    def land(src, core):
        return out_ref.at[src, pl.ds(core * hr, hr), :]

    def copy(k, src_ref, dst_ref, to):
        return pltpu.make_async_remote_copy(src_ref=src_ref, dst_ref=dst_ref, send_sem=send_sems.at[k],
                                            recv_sem=recv_sems.at[k], device_id=to, device_id_type=MESH)

    mine = pltpu.make_async_copy(p_ref.at[me], land(me, cc), local_sem)
    sends = [copy(j, p_ref.at[ids[j]], land(me, cc), (*chip, cc)) for j, chip in enumerate(chips)]
    sends.append(copy(3, p_ref.at[me], land(me, cc), sibling))
    passed = [copy(4 + j, land(ids[j], cc), land(ids[j], cc), sibling) for j in range(3)]

    def start():
        mine.start()
        for cp in sends:
            cp.start()

    def forward():
        for j in range(3):
            copy(j, p_ref.at[me], land(ids[j], cc), (x, y, cc)).wait_recv()
            passed[j].start()

    def finish():
        copy(3, p_ref.at[me], land(me, 1 - cc), (x, y, cc)).wait_recv()
        for j in range(3):
            copy(4 + j, p_ref.at[me], land(ids[j], 1 - cc), (x, y, cc)).wait_recv()
        for cp in sends + passed:
            cp.wait_send()
        mine.wait()

    return start, forward, finish


def _scatter_side(p):
    n, hr, c = p.shape
    return _Side([p], [jax.ShapeDtypeStruct((n, 2 * hr, c), p.dtype)], SCATTER_SEMS, _scatter_ops)


def _gather_small(blk):
    m, n = blk.shape

    def body(x_ref, out_ref, send_sems, recv_sems, local_sem):
        x, y, cc = _place()
        me, sibling = (x, y, cc), (x, y, 1 - cc)
        chips = _other_chips(x, y)

        def slot(px, py, pc):
            return out_ref.at[4 * px + 2 * py + pc]

        def copy(k, block, to, src=None):
            return pltpu.make_async_remote_copy(
                src_ref=slot(*block) if src is None else src, dst_ref=slot(*block),
                send_sem=send_sems.at[k], recv_sem=recv_sems.at[k], device_id=to, device_id_type=MESH)

        mine = pltpu.make_async_copy(x_ref, slot(*me), local_sem)
        mine.start()
        first = [copy(0, me, sibling, src=x_ref)]
        first += [copy(1 + j, me, (*chip, cc), src=x_ref) for j, chip in enumerate(chips)]
        for cp in first:
            cp.start()
        passed = [copy(4 + j, (*chip, cc), sibling) for j, chip in enumerate(chips)]
        for j, chip in enumerate(chips):
            copy(1 + j, (*chip, cc), me).wait_recv()
            passed[j].start()
        copy(0, sibling, me).wait_recv()
        for j, chip in enumerate(chips):
            copy(4 + j, (*chip, 1 - cc), me).wait_recv()
        for cp in first + passed:
            cp.wait_send()
        mine.wait()

    vmem = pl.BlockSpec(memory_space=pltpu.VMEM)
    return pl.pallas_call(
        body, out_shape=jax.ShapeDtypeStruct((N_DEV, m, n), blk.dtype), in_specs=[vmem], out_specs=vmem,
        scratch_shapes=[pltpu.SemaphoreType.DMA((7,)), pltpu.SemaphoreType.DMA((7,)), pltpu.SemaphoreType.DMA],
        name="gather_small")(blk)


def _pack_a(sh, dtype):
    w = sh["w_in"].astype(dtype)
    return jnp.concatenate([w[:, 0:PACK_W], jnp.pad(w[:, PACK_W:], ((0, 0), (0, 2 * PACK_W - w.shape[1])))], axis=0)


def _pack_b(sh, dtype):
    o3 = jnp.concatenate([sh["w_gla_o"], sh["w_fox_o"], sh["w_mem_o"], jnp.zeros((512, 256), sh["w_gla_o"].dtype)], axis=1)
    au = jnp.pad(sh["w_alpha_up"], ((0, PACK_ROWS_B - 3072 - 16), (0, PACK_W - 64)))
    return jnp.concatenate([sh["w_ff1"], sh["w_ff2"], sh["w_mem_kv"], sh["w_out"], o3, au], axis=0).astype(dtype)


def _unpack_a(pa):
    return {"w_in": jnp.concatenate([pa[0:1024], pa[1024:2048, 0:1670 - PACK_W]], axis=1)}


def _unpack_b(pb):
    return {"w_ff1": pb[0:1024], "w_ff2": pb[1024:2048], "w_mem_kv": pb[2048:2304], "w_out": pb[2304:2560],
            "w_gla_o": pb[2560:3072, 0:256], "w_fox_o": pb[2560:3072, 256:512], "w_mem_o": pb[2560:3072, 512:768],
            "w_alpha_up": pb[3072:3088, 0:64]}


def _unpack(packed):
    return {**_unpack_a(packed[0:PACK_ROWS_A]), **_unpack_b(packed[PACK_ROWS_A:])}


def _split_shards(name, full):
    return jnp.split(full, N_CHIPS, axis=SHARD_AXIS[name])


def _pack_small(vals, scalar=None):
    row4 = jnp.concatenate([vals["b_alpha"].reshape(-1), vals["b_forget"].reshape(-1), jnp.zeros((D - 264,), F32)])
    row5 = jnp.concatenate([vals["g_gla_head"].reshape(-1), jnp.zeros((D - 512,), F32)])
    row6 = jnp.zeros((D,), F32) if scalar is None else jnp.broadcast_to(scalar, (D,))
    rows = [vals["g_mix"].reshape(-1), vals["g_mem"].reshape(-1), vals["g_ffn"].reshape(-1), vals["g_final"].reshape(-1),
            row4, row5, row6, jnp.zeros((D,), F32)]
    return jnp.stack(rows)


def _unpack_small(blk):
    return {"g_mix": blk[0].reshape(1, D), "g_mem": blk[1].reshape(1, D), "g_ffn": blk[2].reshape(1, D),
            "g_final": blk[3].reshape(D), "b_alpha": blk[4, 0:256].reshape(1, 256), "b_forget": blk[4, 256:264].reshape(1, 8),
            "g_gla_head": blk[5, 0:512].reshape(1, 4, 128)}


def _local_step(x, mem, target, wb, small, exchange=None):
    s = x.shape[0]
    nm = mem.shape[0]
    t = _row_tile(s)
    nb = s // t
    b_alpha = small["b_alpha"].reshape(1, 256)
    bias_e = jnp.concatenate([jnp.zeros((FF_LANE,), F32), small["b_forget"].reshape(-1),
                              jnp.zeros((PE_W - FF_LANE - 8,), F32)]).reshape(1, PE_W)
    g_mix, g_mem, g_ffn = small["g_mix"].reshape(1, D), small["g_mem"].reshape(1, D), small["g_ffn"].reshape(1, D)
    g_final = small["g_final"].reshape(1, D)
    g_head = small["g_gla_head"].reshape(1, 512)

    if exchange is None:
        u, r1 = _rms_fwd(x, g_mix, "norm_mix")
    else:
        u, r1, gathered = _rms_fwd(x, g_mix, "norm_mix", side=exchange.gather_a)
        wb = exchange.weights_a(gathered)
    w_in = wb["w_in"]
    w_main = jnp.concatenate([w_in[:, 3608:6680], w_in[:, 0:1536], w_in[:, 1552:3088], w_in[:, 3096:3608]], axis=1)
    w_e = jnp.concatenate([w_in[:, 1536:1552], w_in[:, 3088:3096], jnp.zeros((D, PE_W - 24), BF16)], axis=1)
    w_in_pt = _transpose(jnp.concatenate([w_main, w_e], axis=1), "t_w_in")
    big = min(s, 1024)
    if exchange is None:
        pm, pe = _proj(u, w_main, w_e)
    else:
        pm, pe, gathered = _proj(u, w_main, w_e, side=exchange.gather_b)
        wb = {**wb, **exchange.weights_b(gathered)}
    wau_p = jnp.concatenate([wb["w_alpha_up"], jnp.zeros((PE_W - 16, 256), BF16)], axis=0)
    o_gla, states = _gla_fwd(pm, pe, wau_p, b_alpha)
    tb = _fox_tables()
    qf_aug, k_aug, v_aug, vt, qt, kt, fcum = _fox_prep(pm, (pe, bias_e), None, tb, backward=False)
    o_fox, lse = _fox_fwd(k_aug, qf_aug, vt)
    mn, rm = _rms_fwd(mem, g_mem, "norm_mem")
    mkv = _mm_nn(mn, wb["w_mem_kv"], out_dtype=BF16, tm=nm, tn=512, tk=D, name="mem_kv")
    o_mem = _mem_attn_fwd(pm, mkv)
    merged, h1, u2, r2 = _merge_fwd(x, pm, o_gla, o_fox, o_mem, g_head, wb["w_gla_o"], wb["w_fox_o"], wb["w_mem_o"],
                                    wb["w_out"], g_ffn)
    a = _mm_nn(u2, wb["w_ff1"], out_dtype=BF16, tm=big, tn=1024, tk=D, name="ff1")
    dh2, dh2b, loss8, dgfin8 = _ff2_loss(a, wb["w_ff2"], h1, g_final, target)
    loss = 0.5 * jnp.sum(loss8) / D

    da = _mm_nn(dh2b, _transpose(wb["w_ff2"], "t_w_ff2"), out_dtype=BF16, tm=big, tn=1024, tk=D, name="d_act",
                epi=lambda acc, at: acc * (2.0 * jnp.maximum(at.astype(F32), 0.0)), extra=a)
    gw = {}
    gw["w_ff2"] = _mm_tn(a, dh2b, tm=1024, tn=D, ts=big, name="dw_ff2", a_fn=_relu2_bf16)
    gw["w_ff1"] = _mm_tn(u2, da, tm=D, tn=1024, ts=big, name="dw_ff1")
    dh1, dh1b, dgffn8 = _mm_norm_bwd([da], _transpose(wb["w_ff1"], "t_w_ff1"), h1, r2, g_ffn, dh2, name="d_h1", want_bf16=True)
    gw["w_out"] = _mm_tn(merged, dh1b, tm=D, tn=D, ts=big, name="dw_out")
    (dgates, dgg, do_gla, do_aug, do_t, do_mem, gw["w_gla_o"], gw["w_fox_o"], gw["w_mem_o"], dgh8) = _merge_bwd(
        dh1b, pm, o_gla, o_fox, o_mem, g_head, wb["w_gla_o"], wb["w_fox_o"], wb["w_mem_o"],
        *[_transpose(wb[n], "t_" + n) for n in ("w_gla_o", "w_fox_o", "w_mem_o", "w_out")], tb["spread"], tb["d_to_do"])
    dgq, dgk, dgv, de_gla, dwau_p, dba8 = _gla_bwd(pm, pe, wau_p, wau_p.T, b_alpha, do_gla, states)
    gw["w_alpha_up"] = dwau_p[0:16, :]
    q_aug = _fox_prep(pm, fcum, lse.reshape(8, s), tb, backward=True)
    dfq_t, dfrow, dfk_t, dfv_t, dfcol = _fox_bwd(q_aug, do_aug, qt, do_t, k_aug, v_aug, kt)
    dfq, dfk, dfv, de_fox, dbf8 = _fox_post(dfq_t, dfk_t, dfv_t, dfrow[:, 0, :], dfcol.reshape(8, s), pe, bias_e, tb)
    dmq, dmk, dmv = _mem_attn_bwd(pm, mkv, do_mem)
    dmkv = jnp.concatenate([dmk, dmv], axis=1).astype(BF16)
    gw["w_mem_kv"] = _mm_tn(mn, dmkv, tm=D, tn=D, ts=nm, name="dw_mem_kv")
    dmn_g = _mm_nn(dmkv, _transpose(wb["w_mem_kv"], "t_w_mem_kv"), out_dtype=F32, tm=nm, tn=D, tk=D, name="d_mem_norm")
    dgmem8 = _gain_grad(dmn_g, mem, rm, "dg_mem")
    de = (de_gla + de_fox).astype(BF16)
    dproj = [dgates, dgq, dgk, dgv, dgg, dfq, dfk, dfv, dmq, de]
    dw_gates = _mm_tn(u, dgates, tm=D, tn=1024, ts=big, name="dw_in_gates")
    dw_g = _mm_tn_cat(u, [dgq, dgk, dgv], ts=big, name="dw_in_gla")
    dw_gf = _mm_tn_cat(u, [dgg, dfq], ts=big, name="dw_in_gg_fq")
    dw_f = _mm_tn_cat(u, [dfk, dfv], ts=big, name="dw_in_fk_fv")
    dw_m = _mm_tn_cat(u, [dmq, de], ts=big, name="dw_in_mq_narrow")
    gw["w_in"] = jnp.concatenate([dw_g, dw_gf[:, 0:512], dw_m[:, 512:528], dw_gf[:, 512:1024], dw_f,
                                  dw_m[:, 528:536], dw_m[:, 0:512], dw_gates], axis=1)
    if exchange is None:
        grad_x, dgmix8 = _mm_norm_bwd(dproj, w_in_pt, x, r1, g_mix, dh1, name="d_x", want_bf16=False)
        exchanged = None
    else:
        grad_x, dgmix8, exchanged = _mm_norm_bwd(dproj, w_in_pt, x, r1, g_mix, dh1, name="d_x", want_bf16=False,
                                                 side=exchange.scatter(gw))
    gs = {"g_mix": dgmix8.sum(0), "g_mem": dgmem8.sum(0), "g_ffn": dgffn8.sum(0), "g_final": dgfin8.sum(0),
          "b_alpha": dba8.sum(0), "b_forget": dbf8.sum(0)[FF_LANE:FF_LANE + 8], "g_gla_head": dgh8.sum(0)}
    return loss, grad_x, gw, gs, exchanged


def kernel(x, mem, g_mix, w_in, w_alpha_up, b_alpha, b_forget, g_gla_head, g_mem, w_mem_kv, w_gla_o, w_fox_o, w_mem_o, w_out, g_ffn, w_ff1, w_ff2, g_final, loss_target, m_g_mix, m_w_in, m_w_alpha_up, m_b_alpha, m_b_forget, m_g_gla_head, m_g_mem, m_w_mem_kv, m_w_gla_o, m_w_fox_o, m_w_mem_o, m_w_out, m_g_ffn, m_w_ff1, m_w_ff2, m_g_final, v_g_mix, v_w_in, v_w_alpha_up, v_b_alpha, v_b_forget, v_g_gla_head, v_g_mem, v_w_mem_kv, v_w_gla_o, v_w_fox_o, v_w_mem_o, v_w_out, v_g_ffn, v_w_ff1, v_w_ff2, v_g_final):
    args = dict(locals())
    w_sh = {n: args[n][0] for n in WEIGHTS}
    small = {n: args[n] for n in SMALL}

    def whole(parts):
        return {n: jnp.concatenate([p[n] for p in parts], axis=SHARD_AXIS[n]) for n in parts[0]}

    class Exchange:
        gather_a = _gather_side(_pack_a(w_sh, BF16))
        gather_b = _gather_side(_pack_b(w_sh, BF16))

        @staticmethod
        def weights_a(gathered):
            return whole([_unpack_a(gathered[k]) for k in range(N_CHIPS)])

        @staticmethod
        def weights_b(gathered):
            return whole([_unpack_b(gathered[k]) for k in range(N_CHIPS)])

        @staticmethod
        def scatter(gw):
            by_chip = {n: _split_shards(n, gw[n]) for n in WEIGHTS}
            packed = jnp.stack([jnp.concatenate([_pack_a({n: by_chip[n][k] for n in WEIGHTS}, BF16),
                                                 _pack_b({n: by_chip[n][k] for n in WEIGHTS}, BF16)], axis=0)
                                for k in range(N_CHIPS)])
            core = lax.axis_index("c").astype(jnp.int32).reshape(1)
            return _scatter_side(_add_half(core, packed, _swap_halves(packed), "chip_sum"))

    loss, grad_x, gw, gs, by_chip = _local_step(x[0], mem[0], loss_target[0], None, small, Exchange)
    g_out = {n: g[None] for n, g in _unpack(_sum4(by_chip, "shard_sum")).items()}
    d_out, m_out, v_out = {}, {}, {}
    for n in WEIGHTS:
        d_out[n], m_out[n], v_out[n] = _adam(args[n], g_out[n], args["m_" + n], args["v_" + n], "adam_" + n)

    small_all = _gather_small(_pack_small(gs, loss))
    sm = {n: args["m_" + n] for n in SMALL}
    sv = {n: args["v_" + n] for n in SMALL}
    gs_sum, sd, snm, snv = _adam_small(_pack_small(small), small_all, _pack_small(sm), _pack_small(sv))
    gs_o, sd_o, snm_o, snv_o = _unpack_small(gs_sum), _unpack_small(sd), _unpack_small(snm), _unpack_small(snv)

    names = ["g_mix", "w_in", "w_alpha_up", "b_alpha", "b_forget", "g_gla_head", "g_mem", "w_mem_kv", "w_gla_o", "w_fox_o",
             "w_mem_o", "w_out", "g_ffn", "w_ff1", "w_ff2", "g_final"]

    def pick(big, sml, n):
        return big[n] if n in big else sml[n]

    outs = [gs_sum[6, 0], grad_x[None]]
    for big, sml in ((g_out, gs_o), (d_out, sd_o), (m_out, snm_o), (v_out, snv_o)):
        outs += [pick(big, sml, n) for n in names]
    return tuple(outs)
```
